```python
import jax, jax.numpy as jnp
from jax import lax
import numpy as np

D_MODEL = 1024
BATCH = 8
SEQ = 4096
DEPTH = 1

POOL_WINDOWS = (2, 4, 8, 16)
N_POOL_GROUPS = len(POOL_WINDOWS)
POOL_WIDTH = D_MODEL
POOL_GROUP = POOL_WIDTH // N_POOL_GROUPS
HEAD_DIM = 64
N_Q_HEADS = D_MODEL // HEAD_DIM
N_KV_HEADS = 2
GQA_GROUP = N_Q_HEADS // N_KV_HEADS
WINDOW = 128
BLOCK = 128
ROPE_DIM = HEAD_DIM // 4
ROPE_THETA = 500000.0
Q_WIDTH = N_Q_HEADS * HEAD_DIM
KV_WIDTH = N_KV_HEADS * HEAD_DIM
D_FF = 2816
CONV_WIDTH = 3
EPS = 1e-6
IN_WIDTH = POOL_WIDTH + Q_WIDTH + 2 * KV_WIDTH + 2 * D_MODEL

kernel_name = "hybrid_pool_swa_sink_convglu_block"


def rmsnorm(x, g):
    xf = x.astype(jnp.float32)
    r = lax.rsqrt(jnp.mean(xf * xf, axis=-1, keepdims=True) + EPS)
    return (xf * r * g.astype(jnp.float32)).astype(x.dtype)


def partial_rope(x, positions):
    half = ROPE_DIM // 2
    inv_freq = ROPE_THETA ** (-jnp.arange(0, ROPE_DIM, 2, dtype=jnp.float32) / ROPE_DIM)
    ang = positions.astype(jnp.float32)[..., None] * inv_freq
    cos = jnp.cos(ang)[:, :, None, :]
    sin = jnp.sin(ang)[:, :, None, :]
    xf = x.astype(jnp.float32)
    x1, x2, xp = xf[..., :half], xf[..., half:ROPE_DIM], xf[..., ROPE_DIM:]
    out = jnp.concatenate([x1 * cos - x2 * sin, x2 * cos + x1 * sin, xp], axis=-1)
    return out.astype(x.dtype)


def pool_mixer(u, w_pool, pool_scale):
    B, S, _ = u.shape
    ug = u.reshape(B, S, N_POOL_GROUPS, POOL_GROUP).astype(jnp.float32)
    cs = jnp.cumsum(ug, axis=1)
    t = jnp.arange(S, dtype=jnp.float32)
    pooled = []
    for g, w in enumerate(POOL_WINDOWS):
        csg = cs[:, :, g]
        shifted = jnp.pad(csg, ((0, 0), (w, 0), (0, 0)))[:, :S]
        count = jnp.minimum(t + 1.0, float(w))[None, :, None]
        pooled.append((csg - shifted) / count)
    pooled = jnp.stack(pooled, axis=2) - ug
    mixed = jnp.einsum('bsgc,gcd->bsgd', pooled.astype(u.dtype), w_pool)
    return mixed.reshape(B, S, POOL_WIDTH) * pool_scale


def swa_sink_attention(q, k, v, sinks):
    B, S = q.shape[0], q.shape[1]
    nb = S // BLOCK
    qb = q.reshape(B, nb, BLOCK, N_KV_HEADS, GQA_GROUP, HEAD_DIM)

    def band(t):
        tb = t.reshape(B, nb, BLOCK, N_KV_HEADS, HEAD_DIM)
        prev = jnp.pad(tb, ((0, 0), (1, 0), (0, 0), (0, 0), (0, 0)))[:, :-1]
        return jnp.concatenate([prev, tb], axis=2)

    kb, vb = band(k), band(v)
    s = jnp.einsum('bnqhgd,bnkhd->bhgnqk', qb, kb,
                   preferred_element_type=jnp.float32)
    q_pos = jnp.arange(BLOCK)[:, None] + BLOCK
    k_pos = jnp.arange(2 * BLOCK)[None, :]
    rel_ok = (k_pos <= q_pos) & (q_pos - k_pos < WINDOW)
    blk_ok = (jnp.arange(nb)[:, None, None] > 0) | (k_pos[None] >= BLOCK)
    mask = rel_ok[None] & blk_ok
    s = jnp.where(mask, s, -jnp.inf)
    sink = sinks.astype(jnp.float32).reshape(1, N_KV_HEADS, GQA_GROUP, 1, 1, 1)
    m = jnp.maximum(jnp.max(s, axis=-1, keepdims=True), sink)
    p = jnp.exp(s - m)
    denom = jnp.sum(p, axis=-1, keepdims=True) + jnp.exp(sink - m)
    probs = (p / denom).astype(v.dtype)
    out = jnp.einsum('bhgnqk,bnkhd->bnqhgd', probs, vb)
    return out.reshape(B, S, Q_WIDTH)


def causal_depthwise_conv(u, w, b):
    S = u.shape[1]
    up = jnp.pad(u, ((0, 0), (CONV_WIDTH - 1, 0), (0, 0)))
    y = b
    for j in range(CONV_WIDTH):
        y = y + w[j] * up[:, j:j + S]
    return y


def _fwd_setup_inputs(seed: int = 0) -> dict:
    key = jax.random.key(seed)
    ks = jax.random.split(key, 18)
    f32 = jnp.float32
    nrm = lambda k, shape, s: jax.random.normal(k, shape, f32) * s
    x = jax.random.normal(ks[0], (BATCH, SEQ, D_MODEL), f32)
    offsets = jax.random.randint(ks[1], (BATCH, 1), 0, 1024, dtype=jnp.int32)
    positions = offsets + jnp.arange(SEQ, dtype=jnp.int32)[None, :]
    return {
        "x": x,
        "positions": positions,
        "attn_norm": 1.0 + nrm(ks[2], (DEPTH, D_MODEL), 0.05),
        "w_in": nrm(ks[3], (DEPTH, D_MODEL, IN_WIDTH), D_MODEL ** -0.5),
        "b_gate": nrm(ks[4], (DEPTH, 2 * D_MODEL), 0.1),
        "w_pool": nrm(ks[5], (DEPTH, N_POOL_GROUPS, POOL_GROUP, POOL_GROUP), POOL_GROUP ** -0.5),
        "pool_scale": 1.0 + nrm(ks[6], (DEPTH, POOL_WIDTH), 0.1),
        "q_norm": 1.0 + nrm(ks[7], (DEPTH, HEAD_DIM), 0.05),
        "k_norm": 1.0 + nrm(ks[8], (DEPTH, HEAD_DIM), 0.05),
        "sinks": nrm(ks[9], (DEPTH, N_Q_HEADS), 0.5),
        "w_out": nrm(ks[10], (DEPTH, D_MODEL, D_MODEL), D_MODEL ** -0.5),
        "ffn_norm": 1.0 + nrm(ks[11], (DEPTH, D_MODEL), 0.05),
        "w_up": nrm(ks[12], (DEPTH, D_MODEL, 2 * D_FF), D_MODEL ** -0.5),
        "conv_w": nrm(ks[13], (DEPTH, CONV_WIDTH, 2 * D_FF), CONV_WIDTH ** -0.5),
        "conv_b": nrm(ks[14], (DEPTH, 2 * D_FF), 0.02),
        "w_down": nrm(ks[15], (DEPTH, D_FF, D_MODEL), D_FF ** -0.5),
    }


def _fwd_reference(x, positions, attn_norm, w_in, b_gate, w_pool, pool_scale, q_norm, k_norm,
              sinks, w_out, ffn_norm, w_up, conv_w, conv_b, w_down):
    B, S, _ = x.shape
    scale = HEAD_DIM ** -0.5
    for l in range(DEPTH):
        h = rmsnorm(x, attn_norm[l])
        z = h @ w_in[l]
        o1 = POOL_WIDTH
        o2 = o1 + Q_WIDTH
        o3 = o2 + KV_WIDTH
        o4 = o3 + KV_WIDTH
        u_pool = z[..., :o1]
        q = z[..., o1:o2].reshape(B, S, N_Q_HEADS, HEAD_DIM)
        k = z[..., o2:o3].reshape(B, S, N_KV_HEADS, HEAD_DIM)
        v = z[..., o3:o4].reshape(B, S, N_KV_HEADS, HEAD_DIM)
        gates = jax.nn.sigmoid((z[..., o4:] + b_gate[l]).astype(jnp.float32)).astype(x.dtype)
        g_pool, g_attn = gates[..., :D_MODEL], gates[..., D_MODEL:]

        a = pool_mixer(u_pool, w_pool[l], pool_scale[l])

        q = partial_rope(rmsnorm(q, q_norm[l]), positions) * scale
        k = partial_rope(rmsnorm(k, k_norm[l]), positions)
        b = swa_sink_attention(q, k, v, sinks[l])

        x = x + (g_pool * a + g_attn * b) @ w_out[l]

        h = rmsnorm(x, ffn_norm[l])
        up = causal_depthwise_conv(h @ w_up[l], conv_w[l], conv_b[l])
        gate, val = up[..., :D_FF], up[..., D_FF:]
        x = x + (jax.nn.silu(gate) * val) @ w_down[l]
    return x


import jax as _jax
import jax.numpy as _jnp

TWIN_FORMAT = 'train_step'
FWD_PARAMS = ['x', 'positions', 'attn_norm', 'w_in', 'b_gate', 'w_pool', 'pool_scale', 'q_norm', 'k_norm', 'sinks', 'w_out', 'ffn_norm', 'w_up', 'conv_w', 'conv_b', 'w_down']
TWIN_WEIGHTS = ['attn_norm', 'w_in', 'b_gate', 'w_pool', 'pool_scale', 'q_norm', 'k_norm', 'sinks', 'w_out', 'ffn_norm', 'w_up', 'conv_w', 'conv_b', 'w_down']
TWIN_DIFF_INPUT = 'x'
TWIN_INPUTS = ['x', 'positions', 'attn_norm', 'w_in', 'b_gate', 'w_pool', 'pool_scale', 'q_norm', 'k_norm', 'sinks', 'w_out', 'ffn_norm', 'w_up', 'conv_w', 'conv_b', 'w_down', 'loss_target', 'm_attn_norm', 'm_w_in', 'm_b_gate', 'm_w_pool', 'm_pool_scale', 'm_q_norm', 'm_k_norm', 'm_sinks', 'm_w_out', 'm_ffn_norm', 'm_w_up', 'm_conv_w', 'm_conv_b', 'm_w_down', 'v_attn_norm', 'v_w_in', 'v_b_gate', 'v_w_pool', 'v_pool_scale', 'v_q_norm', 'v_k_norm', 'v_sinks', 'v_w_out', 'v_ffn_norm', 'v_w_up', 'v_conv_w', 'v_conv_b', 'v_w_down']
TWIN_OUTPUTS = ['loss', 'grad_x', 'grad_attn_norm', 'grad_w_in', 'grad_b_gate', 'grad_w_pool', 'grad_pool_scale', 'grad_q_norm', 'grad_k_norm', 'grad_sinks', 'grad_w_out', 'grad_ffn_norm', 'grad_w_up', 'grad_conv_w', 'grad_conv_b', 'grad_w_down', 'delta_attn_norm', 'delta_w_in', 'delta_b_gate', 'delta_w_pool', 'delta_pool_scale', 'delta_q_norm', 'delta_k_norm', 'delta_sinks', 'delta_w_out', 'delta_ffn_norm', 'delta_w_up', 'delta_conv_w', 'delta_conv_b', 'delta_w_down', 'new_m_attn_norm', 'new_m_w_in', 'new_m_b_gate', 'new_m_w_pool', 'new_m_pool_scale', 'new_m_q_norm', 'new_m_k_norm', 'new_m_sinks', 'new_m_w_out', 'new_m_ffn_norm', 'new_m_w_up', 'new_m_conv_w', 'new_m_conv_b', 'new_m_w_down', 'new_v_attn_norm', 'new_v_w_in', 'new_v_b_gate', 'new_v_w_pool', 'new_v_pool_scale', 'new_v_q_norm', 'new_v_k_norm', 'new_v_sinks', 'new_v_w_out', 'new_v_ffn_norm', 'new_v_w_up', 'new_v_conv_w', 'new_v_conv_b', 'new_v_w_down']
TWIN_LEAF_KINDS = {'loss': 'loss', 'grad_x': 'grad_x', 'grad_attn_norm': 'grad_w', 'grad_w_in': 'grad_w', 'grad_b_gate': 'grad_w', 'grad_w_pool': 'grad_w', 'grad_pool_scale': 'grad_w', 'grad_q_norm': 'grad_w', 'grad_k_norm': 'grad_w', 'grad_sinks': 'grad_w', 'grad_w_out': 'grad_w', 'grad_ffn_norm': 'grad_w', 'grad_w_up': 'grad_w', 'grad_conv_w': 'grad_w', 'grad_conv_b': 'grad_w', 'grad_w_down': 'grad_w', 'delta_attn_norm': 'delta_w', 'delta_w_in': 'delta_w', 'delta_b_gate': 'delta_w', 'delta_w_pool': 'delta_w', 'delta_pool_scale': 'delta_w', 'delta_q_norm': 'delta_w', 'delta_k_norm': 'delta_w', 'delta_sinks': 'delta_w', 'delta_w_out': 'delta_w', 'delta_ffn_norm': 'delta_w', 'delta_w_up': 'delta_w', 'delta_conv_w': 'delta_w', 'delta_conv_b': 'delta_w', 'delta_w_down': 'delta_w', 'new_m_attn_norm': 'new_m', 'new_m_w_in': 'new_m', 'new_m_b_gate': 'new_m', 'new_m_w_pool': 'new_m', 'new_m_pool_scale': 'new_m', 'new_m_q_norm': 'new_m', 'new_m_k_norm': 'new_m', 'new_m_sinks': 'new_m', 'new_m_w_out': 'new_m', 'new_m_ffn_norm': 'new_m', 'new_m_w_up': 'new_m', 'new_m_conv_w': 'new_m', 'new_m_conv_b': 'new_m', 'new_m_w_down': 'new_m', 'new_v_attn_norm': 'new_v', 'new_v_w_in': 'new_v', 'new_v_b_gate': 'new_v', 'new_v_w_pool': 'new_v', 'new_v_pool_scale': 'new_v', 'new_v_q_norm': 'new_v', 'new_v_k_norm': 'new_v', 'new_v_sinks': 'new_v', 'new_v_w_out': 'new_v', 'new_v_ffn_norm': 'new_v', 'new_v_w_up': 'new_v', 'new_v_conv_w': 'new_v', 'new_v_conv_b': 'new_v', 'new_v_w_down': 'new_v'}


def _forward(args):
    return _fwd_reference(*[args[k] for k in FWD_PARAMS])


def _output_shape():
    out = _jax.eval_shape(lambda: _forward(_fwd_setup_inputs(0)))
    return out.shape, out.dtype

N_MICROBATCH = 1
ADAM_LR = 0.001
ADAM_B1 = 0.9
ADAM_B2 = 0.999
ADAM_EPS = 1e-08
ADAM_WD = 0.01
ADAM_STEP = 10
PER_EXAMPLE_BATCH_AXIS = {'x': 0, 'positions': 0, 'loss_target': 0}
SHARED_INPUTS = []
_WEIGHT_DTYPES = {'attn_norm': _jnp.float32, 'w_in': _jnp.float32, 'b_gate': _jnp.float32, 'w_pool': _jnp.float32, 'pool_scale': _jnp.float32, 'q_norm': _jnp.float32, 'k_norm': _jnp.float32, 'sinks': _jnp.float32, 'w_out': _jnp.float32, 'ffn_norm': _jnp.float32, 'w_up': _jnp.float32, 'conv_w': _jnp.float32, 'conv_b': _jnp.float32, 'w_down': _jnp.float32}
MOMENT_SCALE = {'attn_norm': 7.896903e+00, 'w_in': 2.063885e-01, 'b_gate': 1.859991e+00, 'w_pool': 5.490134e-01, 'pool_scale': 7.340738e+00, 'q_norm': 1.585774e+00, 'k_norm': 1.567460e+00, 'sinks': 2.333430e-01, 'w_out': 4.316210e-01, 'ffn_norm': 2.712430e+01, 'w_up': 2.527921e-01, 'conv_w': 3.803112e+00, 'conv_b': 3.303387e+00, 'w_down': 3.336528e-01}


def _to_microbatches(a, axis):
    t = _jnp.moveaxis(a, axis, 0)
    t = t.reshape((N_MICROBATCH, t.shape[0] // N_MICROBATCH) + t.shape[1:])
    return _jnp.moveaxis(t, 1, axis + 1)


def setup_inputs(seed: int = 0) -> dict:
    inp = _fwd_setup_inputs(seed)
    key = _jax.random.fold_in(_jax.random.key(seed), 7919)
    shape, _ = _output_shape()
    out = dict(inp)
    out["loss_target"] = _jax.random.normal(_jax.random.fold_in(key, 0), shape, _jnp.float32)
    for i, name in enumerate(TWIN_WEIGHTS):
        w = inp[name].astype(_jnp.float32)
        if MOMENT_SCALE is None:
            s = _jnp.sqrt(_jnp.mean(_jnp.square(w)) + 1e-30)
        else:
            s = MOMENT_SCALE[name]
        km, kv = _jax.random.split(_jax.random.fold_in(key, i + 1))
        out[name] = w
        out["m_" + name] = s * _jax.random.normal(km, w.shape, _jnp.float32)
        out["v_" + name] = (s * s) * _jax.random.uniform(kv, w.shape, _jnp.float32, 0.5, 1.5)
    if N_MICROBATCH > 1:
        for name, axis in PER_EXAMPLE_BATCH_AXIS.items():
            out[name] = _to_microbatches(out[name], axis)
    return {'x': out['x'], 'positions': out['positions'], 'attn_norm': out['attn_norm'], 'w_in': out['w_in'], 'b_gate': out['b_gate'], 'w_pool': out['w_pool'], 'pool_scale': out['pool_scale'], 'q_norm': out['q_norm'], 'k_norm': out['k_norm'], 'sinks': out['sinks'], 'w_out': out['w_out'], 'ffn_norm': out['ffn_norm'], 'w_up': out['w_up'], 'conv_w': out['conv_w'], 'conv_b': out['conv_b'], 'w_down': out['w_down'], 'loss_target': out['loss_target'], 'm_attn_norm': out['m_attn_norm'], 'm_w_in': out['m_w_in'], 'm_b_gate': out['m_b_gate'], 'm_w_pool': out['m_w_pool'], 'm_pool_scale': out['m_pool_scale'], 'm_q_norm': out['m_q_norm'], 'm_k_norm': out['m_k_norm'], 'm_sinks': out['m_sinks'], 'm_w_out': out['m_w_out'], 'm_ffn_norm': out['m_ffn_norm'], 'm_w_up': out['m_w_up'], 'm_conv_w': out['m_conv_w'], 'm_conv_b': out['m_conv_b'], 'm_w_down': out['m_w_down'], 'v_attn_norm': out['v_attn_norm'], 'v_w_in': out['v_w_in'], 'v_b_gate': out['v_b_gate'], 'v_w_pool': out['v_w_pool'], 'v_pool_scale': out['v_pool_scale'], 'v_q_norm': out['v_q_norm'], 'v_k_norm': out['v_k_norm'], 'v_sinks': out['v_sinks'], 'v_w_out': out['v_w_out'], 'v_ffn_norm': out['v_ffn_norm'], 'v_w_up': out['v_w_up'], 'v_conv_w': out['v_conv_w'], 'v_conv_b': out['v_conv_b'], 'v_w_down': out['v_w_down']}


def _loss(weights, diff, rest, loss_target):
    with _jax.named_scope("forward"):
        args = {**rest, TWIN_DIFF_INPUT: diff, **{k: w.astype(_WEIGHT_DTYPES[k]) for k, w in weights.items()}}
        y = _forward(args)
    with _jax.named_scope("loss_head"):
        err = _jnp.square(y.astype(_jnp.float32) - loss_target)
        return 0.5 * _jnp.sum(_jnp.mean(err, axis=-1)) if err.ndim else 0.5 * err


def _adamw(w, g, m, v):
    m = ADAM_B1 * m + (1.0 - ADAM_B1) * g
    v = ADAM_B2 * v + (1.0 - ADAM_B2) * _jnp.square(g)
    m_hat = m / (1.0 - ADAM_B1 ** ADAM_STEP)
    v_hat = v / (1.0 - ADAM_B2 ** ADAM_STEP)
    delta = -ADAM_LR * (m_hat / (_jnp.sqrt(v_hat) + ADAM_EPS) + ADAM_WD * w)
    return delta, m, v


def reference(x, positions, attn_norm, w_in, b_gate, w_pool, pool_scale, q_norm, k_norm, sinks, w_out, ffn_norm, w_up, conv_w, conv_b, w_down, loss_target, m_attn_norm, m_w_in, m_b_gate, m_w_pool, m_pool_scale, m_q_norm, m_k_norm, m_sinks, m_w_out, m_ffn_norm, m_w_up, m_conv_w, m_conv_b, m_w_down, v_attn_norm, v_w_in, v_b_gate, v_w_pool, v_pool_scale, v_q_norm, v_k_norm, v_sinks, v_w_out, v_ffn_norm, v_w_up, v_conv_w, v_conv_b, v_w_down):
    given = dict(x=x, positions=positions, attn_norm=attn_norm, w_in=w_in, b_gate=b_gate, w_pool=w_pool, pool_scale=pool_scale, q_norm=q_norm, k_norm=k_norm, sinks=sinks, w_out=w_out, ffn_norm=ffn_norm, w_up=w_up, conv_w=conv_w, conv_b=conv_b, w_down=w_down, loss_target=loss_target, m_attn_norm=m_attn_norm, m_w_in=m_w_in, m_b_gate=m_b_gate, m_w_pool=m_w_pool, m_pool_scale=m_pool_scale, m_q_norm=m_q_norm, m_k_norm=m_k_norm, m_sinks=m_sinks, m_w_out=m_w_out, m_ffn_norm=m_ffn_norm, m_w_up=m_w_up, m_conv_w=m_conv_w, m_conv_b=m_conv_b, m_w_down=m_w_down, v_attn_norm=v_attn_norm, v_w_in=v_w_in, v_b_gate=v_b_gate, v_w_pool=v_w_pool, v_pool_scale=v_pool_scale, v_q_norm=v_q_norm, v_k_norm=v_k_norm, v_sinks=v_sinks, v_w_out=v_w_out, v_ffn_norm=v_ffn_norm, v_w_up=v_w_up, v_conv_w=v_conv_w, v_conv_b=v_conv_b, v_w_down=v_w_down)
    weights = {n: given[n] for n in TWIN_WEIGHTS}
    shared = {n: given[n] for n in SHARED_INPUTS}
    per_example = {n: given[n] for n in ['x', 'positions']}
    grad_fn = _jax.value_and_grad(_loss, argnums=(0, 1))

    def one_microbatch(ex, loss_target):
        ex = dict(ex)
        diff = ex.pop(TWIN_DIFF_INPUT)
        return grad_fn(weights, diff, {**shared, **ex}, loss_target)

    if N_MICROBATCH == 1:
        loss, (grad_w, grad_x) = one_microbatch(per_example, given["loss_target"])
    else:
        def body(carry, xs):
            loss_sum, grad_sum = carry
            l_k, (gw_k, gx_k) = one_microbatch(xs[0], xs[1])
            with _jax.named_scope("update"):
                return (loss_sum + l_k, _jax.tree.map(_jnp.add, grad_sum, gw_k)), gx_k

        init = (_jnp.zeros((), _jnp.float32), _jax.tree.map(_jnp.zeros_like, weights))
        (loss, grad_w), grad_x = _jax.lax.scan(body, init, (per_example, given["loss_target"]))
    with _jax.named_scope("update"):
        delta_w, new_m, new_v = {}, {}, {}
        for n in TWIN_WEIGHTS:
            delta_w[n], new_m[n], new_v[n] = _adamw(weights[n], grad_w[n], given["m_" + n], given["v_" + n])
    return (loss, grad_x, *[grad_w[n] for n in TWIN_WEIGHTS], *[delta_w[n] for n in TWIN_WEIGHTS],
            *[new_m[n] for n in TWIN_WEIGHTS], *[new_v[n] for n in TWIN_WEIGHTS])
```

```python
import functools

import jax
import jax.numpy as jnp
from jax import lax
from jax.experimental import pallas as pl
from jax.experimental.pallas import tpu as pltpu

D = 1024
D_FF = 2816
HEAD = 64
N_Q = 16
N_KV = 2
GQA = 8
BLK = 128
ROPE_DIM = 16
ROPE_THETA = 500000.0
POOL_GROUP = 256
Q_W = 1024
KV_W = 128
QKV_W = Q_W + 2 * KV_W
IN_W = 4352
UP_SHARD = 1408
EPS = 1e-6
N_CHIPS = 4
N_DEV = 8

LR = 0.001
B1 = 0.9
B2 = 0.999
ADAM_EPS = 1e-08
WD = 0.01
STEP = 10

BF = jnp.bfloat16
F32 = jnp.float32
MESH = pl.DeviceIdType.MESH
VMEM_LIMIT_MB = 56


def _cp(sem, vmem_mb=VMEM_LIMIT_MB):
    return pltpu.CompilerParams(dimension_semantics=sem, vmem_limit_bytes=vmem_mb << 20)


def _full(shape):
    nd = len(shape)
    return pl.BlockSpec(shape, lambda *_: (0,) * nd)


def _sds(shape, dtype):
    return jax.ShapeDtypeStruct(shape, dtype)


def _nt(a, b):
    return lax.dot_general(a, b, (((1,), (1,)), ((), ())), preferred_element_type=F32)


def _tn(a, b):
    return lax.dot_general(a, b, (((0,), (0,)), ((), ())), preferred_element_type=F32)


def _mm(a, b):
    return jnp.dot(a, b, preferred_element_type=F32)


def _head_sum(v, bd):
    hi = v.astype(BF)
    lo = (v - hi.astype(F32)).astype(BF)
    return _mm(hi, bd) + _mm(lo, bd)


def _rope_tables(pos_ref, invf_ref):
    ang = pos_ref[...].astype(F32) * invf_ref[...]
    cos = jnp.cos(ang)
    sin = jnp.sin(ang)
    lane = lax.broadcasted_iota(jnp.int32, (1, 2 * HEAD), 1) % HEAD
    sa = jnp.where(lane < ROPE_DIM // 2, -sin, 0.0)
    sb = jnp.where(lane < ROPE_DIM // 2, 0.0, jnp.where(lane < ROPE_DIM, sin, 0.0))
    return cos, sa, sb


def _tile_lanes(t, reps):
    return t if reps == 1 else jnp.tile(t, (1, reps))


def _rope(v, cos, sa, sb):
    w = v.shape[1]
    reps = w // (2 * HEAD)
    half = ROPE_DIM // 2
    return (v * _tile_lanes(cos, reps) + pltpu.roll(v, w - half, 1) * _tile_lanes(sa, reps)
            + pltpu.roll(v, half, 1) * _tile_lanes(sb, reps))


def _rope_t(dy, cos, sa, sb):
    w = dy.shape[1]
    reps = w // (2 * HEAD)
    half = ROPE_DIM // 2
    return (dy * _tile_lanes(cos, reps) + pltpu.roll(dy * _tile_lanes(sa, reps), half, 1)
            + pltpu.roll(dy * _tile_lanes(sb, reps), w - half, 1))


def _attn_in_proj(x, g1, w_in, b_gate, tm):
    s = x.shape[0]

    def body(x_ref, g_ref, w_ref, b_ref, h_ref, u_ref, qkv_ref, gate_ref):
        xv = x_ref[...]
        r = lax.rsqrt(jnp.mean(xv * xv, axis=-1, keepdims=True) + EPS)
        h = (xv * r * g_ref[...]).astype(BF)
        h_ref[...] = h
        u_ref[...] = _mm(h, w_ref[:, 0:D])
        qkv_ref[...] = _mm(h, w_ref[:, D:D + QKV_W])
        gate_ref[...] = jax.nn.sigmoid(_mm(h, w_ref[:, D + QKV_W:IN_W]) + b_ref[...])

    row = lambda w: pl.BlockSpec((tm, w), lambda i: (i, 0))
    return pl.pallas_call(
        body, grid=(s // tm,),
        in_specs=[row(D), _full((1, D)), _full((D, IN_W)), _full((1, 2 * D))],
        out_specs=[row(D), row(D), row(QKV_W), row(2 * D)],
        out_shape=[_sds((s, D), BF), _sds((s, D), F32), _sds((s, QKV_W), F32), _sds((s, 2 * D), F32)],
        compiler_params=_cp(("parallel",)), name="attn_in_proj")(x, g1, w_in, b_gate)


def _qk_prep(qkv, pos, wq, wk, invf, bd, tm):
    s = qkv.shape[0]

    def body(qkv_ref, pos_ref, wq_ref, wk_ref, invf_ref, bd_ref, qh_ref, kh_ref, vh_ref):
        cos, sa, sb = _rope_tables(pos_ref, invf_ref)
        q = qkv_ref[:, 0:Q_W]
        k = qkv_ref[:, Q_W:Q_W + KV_W]
        v = qkv_ref[:, Q_W + KV_W:QKV_W]
        rq = lax.rsqrt(_head_sum(q * q, bd_ref[...]) * (1.0 / HEAD) + EPS)
        qr = _rope(q * rq * wq_ref[...], cos, sa, sb) * (HEAD ** -0.5)
        rk = lax.rsqrt(_head_sum(k * k, bd_ref[0:KV_W, 0:KV_W]) * (1.0 / HEAD) + EPS)
        kr = _rope(k * rk * wk_ref[...], cos, sa, sb)
        for h in range(N_Q):
            qh_ref[h] = qr[:, HEAD * h:HEAD * (h + 1)].astype(BF)
        for h in range(N_KV):
            kh_ref[h] = kr[:, HEAD * h:HEAD * (h + 1)].astype(BF)
            vh_ref[h] = v[:, HEAD * h:HEAD * (h + 1)].astype(BF)

    heads = lambda n: pl.BlockSpec((n, tm, HEAD), lambda i: (0, i, 0))
    return pl.pallas_call(
        body, grid=(s // tm,),
        in_specs=[pl.BlockSpec((tm, QKV_W), lambda i: (i, 0)), pl.BlockSpec((tm, 1), lambda i: (i, 0)),
                  _full((1, Q_W)), _full((1, KV_W)), _full((1, 2 * HEAD)), _full((Q_W, Q_W))],
        out_specs=[heads(N_Q), heads(N_KV), heads(N_KV)],
        out_shape=[_sds((N_Q, s, HEAD), BF), _sds((N_KV, s, HEAD), BF), _sds((N_KV, s, HEAD), BF)],
        compiler_params=_cp(("parallel",)), name="qk_prep")(qkv, pos, wq, wk, invf, bd)


def _sink_column(sink_ref, kh):
    row_g = lax.broadcasted_iota(jnp.int32, (GQA * BLK, 1), 0) // BLK
    col = jnp.zeros((GQA * BLK, 1), F32)
    for g in range(GQA):
        col = jnp.where(row_g == g, sink_ref[kh * GQA + g], col)
    return col


def _attn_probs(q, k, n, sink_col):
    sc = _nt(q, k)
    qi = lax.broadcasted_iota(jnp.int32, sc.shape, 0) % BLK + BLK
    ki = lax.broadcasted_iota(jnp.int32, sc.shape, 1)
    lo = jnp.where(n > 0, qi - BLK, BLK - 1)
    ok = (ki <= qi) & (ki > lo)
    sc = jnp.where(ok, sc, -jnp.inf)
    m = jnp.maximum(jnp.max(sc, axis=-1, keepdims=True), sink_col)
    p = jnp.exp(sc - m)
    es = jnp.exp(sink_col - m)
    inv = 1.0 / (jnp.sum(p, axis=-1, keepdims=True) + es)
    return p * inv, es * inv


def _attn_fwd(qh, kh, vh, sinks):
    s = qh.shape[1]
    nb = s // BLK

    def body(sink_ref, q_ref, kp_ref, kc_ref, vp_ref, vc_ref, o_ref):
        khd = pl.program_id(0)
        n = pl.program_id(1)
        q = q_ref[...].reshape(GQA * BLK, HEAD)
        k = jnp.concatenate([kp_ref[0], kc_ref[0]], axis=0)
        v = jnp.concatenate([vp_ref[0], vc_ref[0]], axis=0)
        probs, _ = _attn_probs(q, k, n, _sink_column(sink_ref, khd))
        o = _mm(probs.astype(BF), v)
        for j in range(GQA // 2):
            o_ref[:, 2 * HEAD * j:2 * HEAD * (j + 1)] = jnp.concatenate(
                [o[2 * j * BLK:(2 * j + 1) * BLK], o[(2 * j + 1) * BLK:(2 * j + 2) * BLK]], axis=1)

    prev = pl.BlockSpec((1, BLK, HEAD), lambda h, n: (h, jnp.maximum(n - 1, 0), 0))
    cur = pl.BlockSpec((1, BLK, HEAD), lambda h, n: (h, n, 0))
    return pl.pallas_call(
        body, grid=(N_KV, nb),
        in_specs=[pl.BlockSpec(memory_space=pltpu.SMEM),
                  pl.BlockSpec((GQA, BLK, HEAD), lambda h, n: (h, n, 0)), prev, cur, prev, cur],
        out_specs=pl.BlockSpec((BLK, GQA * HEAD), lambda h, n: (n, h)),
        out_shape=_sds((s, Q_W), F32),
        compiler_params=_cp(("parallel", "parallel")), name="attn_fwd")(sinks, qh, kh, kh, vh, vh)


def _pool_fwd(u, w_pool, pool_scale, ts):
    s = u.shape[0]
    halo = 16

    def body(u_ref, wp_ref, ps_ref, a_ref, pooled_ref, buf):
        g = pl.program_id(0)
        i = pl.program_id(1)

        @pl.when(i == 0)
        def _():
            buf[0:halo, :] = jnp.zeros((halo, POOL_GROUP), F32)

        buf[halo:halo + ts, :] = u_ref[...]
        t = (i * ts + lax.broadcasted_iota(jnp.int32, (ts, 1), 0)).astype(F32)
        for gi in range(4):
            @pl.when(g == gi)
            def _(gi=gi):
                w = 2 << gi
                cur = buf[pl.ds(halo, ts), :]
                acc = cur
                for j in range(1, w):
                    acc = acc + buf[pl.ds(halo - j, ts), :]
                pooled = (acc / jnp.minimum(t + 1.0, float(w)) - cur).astype(BF)
                pooled_ref[...] = pooled
                a_ref[...] = _mm(pooled, wp_ref[0]) * ps_ref[...]

        buf[0:halo, :] = buf[ts:ts + halo, :]

    col = pl.BlockSpec((ts, POOL_GROUP), lambda g, i: (i, g))
    return pl.pallas_call(
        body, grid=(4, s // ts),
        in_specs=[col, pl.BlockSpec((1, POOL_GROUP, POOL_GROUP), lambda g, i: (g, 0, 0)),
                  pl.BlockSpec((1, POOL_GROUP), lambda g, i: (0, g))],
        out_specs=[col, col],
        out_shape=[_sds((s, D), F32), _sds((s, D), BF)],
        scratch_shapes=[pltpu.VMEM((halo + ts, POOL_GROUP), F32)],
        compiler_params=_cp(("parallel", "arbitrary")), name="pool_fwd")(u, w_pool, pool_scale)


def _mix_out_proj(a, b, gates, x, w_out, g2, tm):
    s = x.shape[0]

    def body(a_ref, b_ref, gate_ref, x_ref, w_ref, g_ref, mix_ref, y_ref, h_ref):
        mix = (gate_ref[:, 0:D] * a_ref[...] + gate_ref[:, D:2 * D] * b_ref[...]).astype(BF)
        mix_ref[...] = mix
        y = x_ref[...] + _mm(mix, w_ref[...])
        y_ref[...] = y
        r = lax.rsqrt(jnp.mean(y * y, axis=-1, keepdims=True) + EPS)
        h_ref[...] = (y * r * g_ref[...]).astype(BF)

    row = lambda w: pl.BlockSpec((tm, w), lambda i: (i, 0))
    return pl.pallas_call(
        body, grid=(s // tm,),
        in_specs=[row(D), row(D), row(2 * D), row(D), _full((D, D)), _full((1, D))],
        out_specs=[row(D), row(D), row(D)],
        out_shape=[_sds((s, D), BF), _sds((s, D), F32), _sds((s, D), BF)],
        compiler_params=_cp(("parallel",)), name="mix_out_proj")(a, b, gates, x, w_out, g2)


def _ffn_up(h2, w_up, conv_w, conv_b, tm):
    s = h2.shape[0]

    def body(h_ref, wg_ref, wv_ref, cwg_ref, cwv_ref, cbg_ref, cbv_ref,
             preg_ref, prev_ref, upg_ref, upv_ref, act_ref, bufg, bufv):
        i = pl.program_id(1)

        @pl.when(i == 0)
        def _():
            bufg[0:8, :] = jnp.zeros((8, UP_SHARD), F32)
            bufv[0:8, :] = jnp.zeros((8, UP_SHARD), F32)

        h = h_ref[...]

        def conv_half(w_ref, cw_ref, cb_ref, buf, pre_ref, up_ref):
            pre = _mm(h, w_ref[0])
            pre_ref[...] = pre.astype(BF)
            buf[8:8 + tm, :] = pre
            cw = cw_ref[0]
            up = cb_ref[...] + cw[0:1] * buf[pl.ds(6, tm), :]
            up = up + cw[1:2] * buf[pl.ds(7, tm), :]
            up = up + cw[2:3] * pre
            buf[0:8, :] = buf[tm:tm + 8, :]
            up_ref[...] = up.astype(BF)
            return up

        gate = conv_half(wg_ref, cwg_ref, cbg_ref, bufg, preg_ref, upg_ref)
        val = conv_half(wv_ref, cwv_ref, cbv_ref, bufv, prev_ref, upv_ref)
        act_ref[...] = (gate * jax.nn.sigmoid(gate) * val).astype(BF)

    tile = pl.BlockSpec((tm, UP_SHARD), lambda j, i: (i, j))
    wspec = lambda off: pl.BlockSpec((1, D, UP_SHARD), lambda j, i: (j + off, 0, 0))
    cwspec = lambda off: pl.BlockSpec((1, 3, UP_SHARD), lambda j, i: (j + off, 0, 0))
    cbspec = lambda off: pl.BlockSpec((1, UP_SHARD), lambda j, i: (0, j + off))
    half = _sds((s, D_FF), BF)
    return pl.pallas_call(
        body, grid=(2, s // tm),
        in_specs=[pl.BlockSpec((tm, D), lambda j, i: (i, 0)), wspec(0), wspec(2), cwspec(0), cwspec(2),
                  cbspec(0), cbspec(2)],
        out_specs=[tile] * 5, out_shape=[half] * 5,
        scratch_shapes=[pltpu.VMEM((8 + tm + 8, UP_SHARD), F32), pltpu.VMEM((8 + tm + 8, UP_SHARD), F32)],
        compiler_params=_cp(("parallel", "arbitrary")), name="ffn_up")(
            h2, w_up, w_up, conv_w, conv_w, conv_b, conv_b)


def _ffn_down_loss(act, w_down, y1, tgt, tm):
    s = y1.shape[0]

    def body(act_ref, w_ref, y_ref, t_ref, dy_ref, dyb_ref, loss_ref):
        @pl.when(pl.program_id(0) == 0)
        def _():
            loss_ref[...] = jnp.zeros_like(loss_ref)

        e = y_ref[...] + _mm(act_ref[...], w_ref[...]) - t_ref[...]
        dy = e * (1.0 / D)
        dy_ref[...] = dy
        dyb_ref[...] = dy.astype(BF)
        e2 = (e * e).reshape(tm // 8, 8, D).sum(axis=0)
        part = e2[:, 0:128]
        for j in range(1, D // 128):
            part = part + e2[:, 128 * j:128 * (j + 1)]
        loss_ref[...] += part

    row = lambda w: pl.BlockSpec((tm, w), lambda i: (i, 0))
    return pl.pallas_call(
        body, grid=(s // tm,),
        in_specs=[row(D_FF), _full((D_FF, D)), row(D), row(D)],
        out_specs=[row(D), row(D), _full((8, 128))],
        out_shape=[_sds((s, D), F32), _sds((s, D), BF), _sds((8, 128), F32)],
        compiler_params=_cp(("arbitrary",)), name="ffn_down_loss")(act, w_down, y1, tgt)


def _grad_matmul(a, b, tn, tk, name, lead=None, prev=None, lead_off=0):
    s, m = a.shape
    n = b.shape[1]
    nj = n // tn

    def body(*refs):
        a_ref, b_ref = refs[0], refs[1]
        o_ref = refs[-1]
        acc = _tn(a_ref[...], b_ref[...])
        acc = acc if lead is None else acc[None]

        @pl.when(pl.program_id(1) == 0)
        def _():
            o_ref[...] = acc

        @pl.when(pl.program_id(1) > 0)
        def _():
            o_ref[...] += acc

    in_specs = [pl.BlockSpec((tk, m), lambda j, k: (k, 0)), pl.BlockSpec((tk, tn), lambda j, k: (k, j))]
    args = [a, b]
    aliases = {}
    if lead is None:
        out_spec = pl.BlockSpec((m, tn), lambda j, k: (0, j))
        out_shape = _sds((m, n), F32)
    else:
        out_spec = pl.BlockSpec((1, m, tn), lambda j, k: (j + lead_off, 0, 0))
        out_shape = _sds((lead, m, tn), F32)
        if prev is not None:
            in_specs.append(pl.BlockSpec(memory_space=pl.ANY))
            args.append(prev)
            aliases = {2: 0}
    return pl.pallas_call(
        body, grid=(nj, s // tk), in_specs=in_specs, out_specs=out_spec, out_shape=out_shape,
        input_output_aliases=aliases,
        compiler_params=_cp(("parallel", "arbitrary")), name=name)(*args)


def _ffn_act_bwd(dyb, w_down, up_g, up_v, pre_g, pre_v, conv_w, tm):
    s = dyb.shape[0]
    nt = s // tm

    def body(dy_ref, wd_ref, upg_ref, upv_ref, preg_ref, prev_ref, hg_ref, hv_ref, cwg_ref, cwv_ref,
             dpg_ref, dpv_ref, dcwg_ref, dcwv_ref, dcbg_ref, dcbv_ref, nxg, nxv, pbg, pbv):
        i = pl.program_id(1)
        ti = nt - 1 - i

        @pl.when(i == 0)
        def _():
            nxg[tm:tm + 8, :] = jnp.zeros((8, UP_SHARD), F32)
            nxv[tm:tm + 8, :] = jnp.zeros((8, UP_SHARD), F32)
            dcwg_ref[...] = jnp.zeros_like(dcwg_ref)
            dcwv_ref[...] = jnp.zeros_like(dcwv_ref)
            dcbg_ref[...] = jnp.zeros_like(dcbg_ref)
            dcbv_ref[...] = jnp.zeros_like(dcbv_ref)

        dact = _nt(dy_ref[...], wd_ref[...])
        g = upg_ref[...].astype(F32)
        v = upv_ref[...].astype(F32)
        sg = jax.nn.sigmoid(g)
        d_v = dact * (g * sg)
        d_g = dact * v * (sg * (1.0 + g * (1.0 - sg)))

        def conv_bwd(d_up, nx, pb, pre_ref, halo_ref, cw_ref, dp_ref, dcw_ref, dcb_ref):
            nx[0:tm, :] = d_up
            cw = cw_ref[0]
            dp = cw[2:3] * d_up + cw[1:2] * nx[pl.ds(1, tm), :] + cw[0:1] * nx[pl.ds(2, tm), :]
            dp_ref[...] = dp.astype(BF)
            nx[tm:tm + 8, :] = nx[0:8, :]
            pb[0:8, :] = jnp.where(ti > 0, halo_ref[8:16, :].astype(F32), 0.0)
            pb[8:8 + tm, :] = pre_ref[...].astype(F32)
            dcw_ref[0, 0:1, :] += jnp.sum(d_up * pb[pl.ds(6, tm), :], axis=0, keepdims=True)
            dcw_ref[0, 1:2, :] += jnp.sum(d_up * pb[pl.ds(7, tm), :], axis=0, keepdims=True)
            dcw_ref[0, 2:3, :] += jnp.sum(d_up * pb[pl.ds(8, tm), :], axis=0, keepdims=True)
            dcb_ref[...] += jnp.sum(d_up, axis=0, keepdims=True)

        conv_bwd(d_g, nxg, pbg, preg_ref, hg_ref, cwg_ref, dpg_ref, dcwg_ref, dcbg_ref)
        conv_bwd(d_v, nxv, pbv, prev_ref, hv_ref, cwv_ref, dpv_ref, dcwv_ref, dcbv_ref)

    tile = pl.BlockSpec((tm, UP_SHARD), lambda j, i: (nt - 1 - i, j))
    halo = pl.BlockSpec((16, UP_SHARD), lambda j, i: (jnp.maximum((nt - 1 - i) * (tm // 16) - 1, 0), j))
    cwspec = lambda off: pl.BlockSpec((1, 3, UP_SHARD), lambda j, i: (j + off, 0, 0))
    acc_cw = pl.BlockSpec((1, 3, UP_SHARD), lambda j, i: (j, 0, 0))
    acc_cb = pl.BlockSpec((1, UP_SHARD), lambda j, i: (0, j))
    buf = pltpu.VMEM((8 + tm + 8, UP_SHARD), F32)
    return pl.pallas_call(
        body, grid=(2, nt),
        in_specs=[pl.BlockSpec((tm, D), lambda j, i: (nt - 1 - i, 0)),
                  pl.BlockSpec((UP_SHARD, D), lambda j, i: (j, 0)),
                  tile, tile, tile, tile, halo, halo, cwspec(0), cwspec(2)],
        out_specs=[tile, tile, acc_cw, acc_cw, acc_cb, acc_cb],
        out_shape=[_sds((s, D_FF), BF), _sds((s, D_FF), BF), _sds((2, 3, UP_SHARD), F32),
                   _sds((2, 3, UP_SHARD), F32), _sds((1, D_FF), F32), _sds((1, D_FF), F32)],
        scratch_shapes=[buf, buf, buf, buf],
        compiler_params=_cp(("parallel", "arbitrary")), name="ffn_act_bwd")(
            dyb, w_down, up_g, up_v, pre_g, pre_v, pre_g, pre_v, conv_w, conv_w)


def _rms_bwd(dh, y, g):
    r = lax.rsqrt(jnp.mean(y * y, axis=-1, keepdims=True) + EPS)
    n = y * r
    dn = dh * g
    return r * (dn - n * jnp.mean(dn * n, axis=-1, keepdims=True)), dh * n


def _ffn_up_bwd(dp_g, dp_v, w_up, y1, dy2, g2, tm):
    s = y1.shape[0]

    def body(dg_ref, dv_ref, w_ref, y_ref, dy2_ref, g_ref, dy1_ref, dgn_ref):
        @pl.when(pl.program_id(0) == 0)
        def _():
            dgn_ref[...] = jnp.zeros_like(dgn_ref)

        dh = _nt(dg_ref[:, 0:UP_SHARD], w_ref[0])
        dh = dh + _nt(dg_ref[:, UP_SHARD:D_FF], w_ref[1])
        dh = dh + _nt(dv_ref[:, 0:UP_SHARD], w_ref[2])
        dh = dh + _nt(dv_ref[:, UP_SHARD:D_FF], w_ref[3])
        dy, dgn = _rms_bwd(dh, y_ref[...], g_ref[...])
        dy1_ref[...] = dy2_ref[...] + dy
        dgn_ref[...] += jnp.sum(dgn, axis=0, keepdims=True)

    row = lambda w: pl.BlockSpec((tm, w), lambda i: (i, 0))
    return pl.pallas_call(
        body, grid=(s // tm,),
        in_specs=[row(D_FF), row(D_FF), _full((4, D, UP_SHARD)), row(D), row(D), _full((1, D))],
        out_specs=[row(D), _full((1, D))],
        out_shape=[_sds((s, D), F32), _sds((1, D), F32)],
        compiler_params=_cp(("arbitrary",)), name="ffn_up_bwd")(dp_g, dp_v, w_up, y1, dy2, g2)


def _out_proj_bwd(dy1, w_out, a, b, gates, mix, tm):
    s = dy1.shape[0]

    def body(dy_ref, w_ref, a_ref, b_ref, gate_ref, mix_ref, da_ref, db_ref, dzg_ref, dbg_ref, dw_ref):
        @pl.when(pl.program_id(0) == 0)
        def _():
            dbg_ref[...] = jnp.zeros_like(dbg_ref)
            dw_ref[...] = jnp.zeros_like(dw_ref)

        dyb = dy_ref[...].astype(BF)
        dmix = _nt(dyb, w_ref[...])
        gp = gate_ref[:, 0:D]
        ga = gate_ref[:, D:2 * D]
        da_ref[...] = dmix * gp
        db_ref[...] = dmix * ga
        dzp = dmix * a_ref[...] * (gp * (1.0 - gp))
        dza = dmix * b_ref[...] * (ga * (1.0 - ga))
        dzg_ref[:, 0:D] = dzp.astype(BF)
        dzg_ref[:, D:2 * D] = dza.astype(BF)
        dbg_ref[:, 0:D] += jnp.sum(dzp, axis=0, keepdims=True)
        dbg_ref[:, D:2 * D] += jnp.sum(dza, axis=0, keepdims=True)
        dw_ref[...] += _tn(mix_ref[...], dyb)

    row = lambda w: pl.BlockSpec((tm, w), lambda i: (i, 0))
    return pl.pallas_call(
        body, grid=(s // tm,),
        in_specs=[row(D), _full((D, D)), row(D), row(D), row(2 * D), row(D)],
        out_specs=[row(D), row(D), row(2 * D), _full((1, 2 * D)), _full((D, D))],
        out_shape=[_sds((s, D), F32), _sds((s, D), F32), _sds((s, 2 * D), BF), _sds((1, 2 * D), F32),
                   _sds((D, D), F32)],
        compiler_params=_cp(("arbitrary",)), name="out_proj_bwd")(dy1, w_out, a, b, gates, mix)


def _pool_bwd(da, pooled, w_pool, pool_scale, ts):
    s = da.shape[0]
    nt = s // ts
    halo = 16

    def body(da_ref, pooled_ref, wp_ref, ps_ref, du_ref, dwp_ref, dps_ref, buf):
        g = pl.program_id(0)
        i = pl.program_id(1)
        ti = nt - 1 - i

        @pl.when(i == 0)
        def _():
            buf[ts:ts + halo, :] = jnp.zeros((halo, POOL_GROUP), F32)
            dwp_ref[...] = jnp.zeros_like(dwp_ref)
            dps_ref[...] = jnp.zeros_like(dps_ref)

        pooled = pooled_ref[...]
        dav = da_ref[...]
        dps_ref[...] += jnp.sum(dav * _mm(pooled, wp_ref[0]), axis=0, keepdims=True)
        dm = (dav * ps_ref[...]).astype(BF)
        dwp_ref[0] += _tn(pooled, dm)
        dpool = _nt(dm, wp_ref[0])
        t = (ti * ts + lax.broadcasted_iota(jnp.int32, (ts, 1), 0)).astype(F32)
        for gi in range(4):
            @pl.when(g == gi)
            def _(gi=gi):
                w = 2 << gi
                buf[0:ts, :] = dpool / jnp.minimum(t + 1.0, float(w))
                acc = buf[pl.ds(0, ts), :]
                for j in range(1, w):
                    acc = acc + buf[pl.ds(j, ts), :]
                du_ref[...] = (acc - dpool).astype(BF)

        buf[ts:ts + halo, :] = buf[0:halo, :]

    col = pl.BlockSpec((ts, POOL_GROUP), lambda g, i: (nt - 1 - i, g))
    return pl.pallas_call(
        body, grid=(4, nt),
        in_specs=[col, col, pl.BlockSpec((1, POOL_GROUP, POOL_GROUP), lambda g, i: (g, 0, 0)),
                  pl.BlockSpec((1, POOL_GROUP), lambda g, i: (0, g))],
        out_specs=[col, pl.BlockSpec((1, POOL_GROUP, POOL_GROUP), lambda g, i: (g, 0, 0)),
                   pl.BlockSpec((1, POOL_GROUP), lambda g, i: (0, g))],
        out_shape=[_sds((s, D), BF), _sds((4, POOL_GROUP, POOL_GROUP), F32), _sds((1, D), F32)],
        scratch_shapes=[pltpu.VMEM((ts + halo, POOL_GROUP), F32)],
        compiler_params=_cp(("parallel", "arbitrary")), name="pool_bwd")(da, pooled, w_pool, pool_scale)


def _attn_bwd(qh, kh, vh, sinks, db):
    s = qh.shape[1]
    nb = s // BLK

    def body(sink_ref, q_ref, kp_ref, kc_ref, vp_ref, vc_ref, do_ref,
             dq_ref, dk_ref, dv_ref, dsink_ref, ck, cv):
        khd = pl.program_id(0)
        n = pl.program_id(1)

        @pl.when(n == 0)
        def _():
            ck[...] = jnp.zeros_like(ck)
            cv[...] = jnp.zeros_like(cv)
            dsink_ref[...] = jnp.zeros_like(dsink_ref)

        @pl.when(n < nb)
        def _():
            q = q_ref[...].reshape(GQA * BLK, HEAD)
            k = jnp.concatenate([kp_ref[0], kc_ref[0]], axis=0)
            v = jnp.concatenate([vp_ref[0], vc_ref[0]], axis=0)
            dov = do_ref[...]
            do = jnp.concatenate([dov[:, HEAD * g:HEAD * (g + 1)] for g in range(GQA)], axis=0).astype(BF)
            probs, psink = _attn_probs(q, k, n, _sink_column(sink_ref, khd))
            dp = _nt(do, v)
            delta = jnp.sum(probs * dp, axis=-1, keepdims=True)
            ds = (probs * (dp - delta)).astype(BF)
            dq_ref[...] = _mm(ds, k).reshape(GQA, BLK, HEAD)
            dk = _tn(ds, q)
            dv = _tn(probs.astype(BF), do)
            dk_ref[0] = ck[...] + dk[0:BLK]
            dv_ref[0] = cv[...] + dv[0:BLK]
            ck[...] = dk[BLK:2 * BLK]
            cv[...] = dv[BLK:2 * BLK]
            dsk = psink * delta
            lane = lax.broadcasted_iota(jnp.int32, (1, 128), 1)
            acc = jnp.zeros((1, 128), F32)
            for g in range(GQA):
                acc = acc - jnp.where(lane == g, jnp.sum(dsk[g * BLK:(g + 1) * BLK], axis=0, keepdims=True), 0.0)
            dsink_ref[0] += acc

        @pl.when(n == nb)
        def _():
            dk_ref[0] = ck[...]
            dv_ref[0] = cv[...]

    last = nb - 1
    prev = pl.BlockSpec((1, BLK, HEAD), lambda h, n: (h, jnp.maximum(jnp.minimum(n, last) - 1, 0), 0))
    cur = pl.BlockSpec((1, BLK, HEAD), lambda h, n: (h, jnp.minimum(n, last), 0))
    kv_out = pl.BlockSpec((1, BLK, HEAD), lambda h, n: (h, jnp.maximum(n - 1, 0), 0))
    return pl.pallas_call(
        body, grid=(N_KV, nb + 1),
        in_specs=[pl.BlockSpec(memory_space=pltpu.SMEM),
                  pl.BlockSpec((GQA, BLK, HEAD), lambda h, n: (h, jnp.minimum(n, last), 0)),
                  prev, cur, prev, cur,
                  pl.BlockSpec((BLK, GQA * HEAD), lambda h, n: (jnp.minimum(n, last), h))],
        out_specs=[pl.BlockSpec((GQA, BLK, HEAD), lambda h, n: (h, jnp.minimum(n, last), 0)), kv_out, kv_out,
                   pl.BlockSpec((1, 1, 128), lambda h, n: (h, 0, 0))],
        out_shape=[_sds((N_Q, s, HEAD), F32), _sds((N_KV, s, HEAD), F32), _sds((N_KV, s, HEAD), F32),
                   _sds((N_KV, 1, 128), F32)],
        scratch_shapes=[pltpu.VMEM((BLK, HEAD), F32), pltpu.VMEM((BLK, HEAD), F32)],
        compiler_params=_cp(("parallel", "arbitrary")), name="attn_bwd")(sinks, qh, kh, kh, vh, vh, db)


def _qk_prep_bwd(dqh, dkh, dvh, qkv, pos, wq, wk, invf, bd, tm):
    s = qkv.shape[0]

    def fold_heads(row):
        out = row[:, 0:HEAD]
        for h in range(1, row.shape[1] // HEAD):
            out = out + row[:, HEAD * h:HEAD * (h + 1)]
        return out

    def body(dq_ref, dk_ref, dv_ref, qkv_ref, pos_ref, wq_ref, wk_ref, invf_ref, bd_ref,
             dz_ref, dwq_ref, dwk_ref):
        @pl.when(pl.program_id(0) == 0)
        def _():
            dwq_ref[...] = jnp.zeros_like(dwq_ref)
            dwk_ref[...] = jnp.zeros_like(dwk_ref)

        cos, sa, sb = _rope_tables(pos_ref, invf_ref)

        def norm_rope_bwd(dy, xin, w, bdm):
            dn = _rope_t(dy, cos, sa, sb)
            r = lax.rsqrt(_head_sum(xin * xin, bdm) * (1.0 / HEAD) + EPS)
            nh = xin * r
            gw = dn * w
            dx = r * (gw - nh * (_head_sum(gw * nh, bdm) * (1.0 / HEAD)))
            return dx, fold_heads(jnp.sum(dn * nh, axis=0, keepdims=True))

        dq = jnp.concatenate([dq_ref[h] for h in range(N_Q)], axis=1) * (HEAD ** -0.5)
        dk = jnp.concatenate([dk_ref[h] for h in range(N_KV)], axis=1)
        dxq, dwq = norm_rope_bwd(dq, qkv_ref[:, 0:Q_W], wq_ref[...], bd_ref[...])
        dxk, dwk = norm_rope_bwd(dk, qkv_ref[:, Q_W:Q_W + KV_W], wk_ref[...], bd_ref[0:KV_W, 0:KV_W])
        dz_ref[:, 0:Q_W] = dxq.astype(BF)
        dz_ref[:, Q_W:Q_W + KV_W] = dxk.astype(BF)
        dz_ref[:, Q_W + KV_W:QKV_W] = jnp.concatenate([dv_ref[h] for h in range(N_KV)], axis=1).astype(BF)
        dwq_ref[...] += dwq
        dwk_ref[...] += dwk

    heads = lambda n: pl.BlockSpec((n, tm, HEAD), lambda i: (0, i, 0))
    return pl.pallas_call(
        body, grid=(s // tm,),
        in_specs=[heads(N_Q), heads(N_KV), heads(N_KV), pl.BlockSpec((tm, QKV_W), lambda i: (i, 0)),
                  pl.BlockSpec((tm, 1), lambda i: (i, 0)), _full((1, Q_W)), _full((1, KV_W)),
                  _full((1, 2 * HEAD)), _full((Q_W, Q_W))],
        out_specs=[pl.BlockSpec((tm, QKV_W), lambda i: (i, 0)), _full((1, HEAD)), _full((1, HEAD))],
        out_shape=[_sds((s, QKV_W), BF), _sds((1, HEAD), F32), _sds((1, HEAD), F32)],
        compiler_params=_cp(("arbitrary",)), name="qk_prep_bwd")(dqh, dkh, dvh, qkv, pos, wq, wk, invf, bd)


def _in_proj_bwd(du, dzq, dzg, w_in, x, g1, dy1, tm):
    s = x.shape[0]

    def body(du_ref, dzq_ref, dzg_ref, w_ref, x_ref, g_ref, dy_ref, gx_ref, dgn_ref):
        @pl.when(pl.program_id(0) == 0)
        def _():
            dgn_ref[...] = jnp.zeros_like(dgn_ref)

        dh = _nt(du_ref[...], w_ref[:, 0:D])
        dh = dh + _nt(dzq_ref[...], w_ref[:, D:D + QKV_W])
        dh = dh + _nt(dzg_ref[...], w_ref[:, D + QKV_W:IN_W])
        dx, dgn = _rms_bwd(dh, x_ref[...], g_ref[...])
        gx_ref[...] = dy_ref[...] + dx
        dgn_ref[...] += jnp.sum(dgn, axis=0, keepdims=True)

    row = lambda w: pl.BlockSpec((tm, w), lambda i: (i, 0))
    return pl.pallas_call(
        body, grid=(s // tm,),
        in_specs=[row(D), row(QKV_W), row(2 * D), _full((D, IN_W)), row(D), _full((1, D)), row(D)],
        out_specs=[row(D), _full((1, D))],
        out_shape=[_sds((s, D), F32), _sds((1, D), F32)],
        compiler_params=_cp(("arbitrary",)), name="in_proj_bwd")(du, dzq, dzg, w_in, x, g1, dy1)


def _adamw(w, g, m, v, tr, name):
    r, c = w.shape

    def body(w_ref, g_ref, m_ref, v_ref, go_ref, d_ref, mo_ref, vo_ref):
        gv = g_ref[...]
        mn = B1 * m_ref[...] + (1.0 - B1) * gv
        vn = B2 * v_ref[...] + (1.0 - B2) * (gv * gv)
        m_hat = mn / (1.0 - B1 ** STEP)
        v_hat = vn / (1.0 - B2 ** STEP)
        go_ref[...] = gv
        d_ref[...] = -LR * (m_hat / (jnp.sqrt(v_hat) + ADAM_EPS) + WD * w_ref[...])
        mo_ref[...] = mn
        vo_ref[...] = vn

    blk = pl.BlockSpec((tr, c), lambda i: (i, 0))
    return pl.pallas_call(
        body, grid=(r // tr,), in_specs=[blk] * 4, out_specs=[blk] * 4, out_shape=[_sds((r, c), F32)] * 4,
        compiler_params=_cp(("parallel",)), name=name)(w, g, m, v)


def _place():
    x, y, c = lax.axis_index("x"), lax.axis_index("y"), lax.axis_index("c")
    chips = [(1 - x, y), (x, 1 - y), (1 - x, 1 - y)]
    return x, y, c, chips


def _rows(ref, lead, h, rh):
    sl = pl.ds(pl.multiple_of(h * rh, 16), rh)
    return ref.at[sl, :] if lead is None else ref.at[lead, sl, :]


def _all_gather_weights(halved, whole):
    nh, nw = len(halved), len(whole)
    na = nh + nw
    arrays = list(halved) + list(whole)

    def body(*refs):
        ins, outs = refs[:na], refs[na:2 * na]
        ici_send, ici_recv, fwd_send, fwd_recv, loc_sem = refs[2 * na:]
        x, y, c, chips = _place()
        me = 2 * x + y
        sibling = (x, y, 1 - c)
        local = [pltpu.make_async_copy(ins[a], outs[a].at[me], loc_sem.at[a]) for a in range(na)]
        for cp in local:
            cp.start()

        def ici(a, j, src_chip, src=None):
            if a < nh:
                rh = arrays[a].shape[0] // 2
                dst = _rows(outs[a], src_chip, c, rh)
                src = dst if src is None else _rows(src, None, c, rh)
            else:
                dst = outs[a].at[src_chip]
                src = dst if src is None else src
            return pltpu.make_async_remote_copy(
                src_ref=src, dst_ref=dst, send_sem=ici_send.at[3 * a + j], recv_sem=ici_recv.at[3 * a + j],
                device_id=(*chips[j], c), device_id_type=MESH)

        def fwd(a, j, half):
            rh = arrays[a].shape[0] // 2
            kj = 2 * chips[j][0] + chips[j][1]
            blk = _rows(outs[a], kj, half, rh)
            return pltpu.make_async_remote_copy(
                src_ref=blk, dst_ref=blk, send_sem=fwd_send.at[3 * a + j], recv_sem=fwd_recv.at[3 * a + j],
                device_id=sibling, device_id_type=MESH)

        sends = [ici(a, j, me, src=ins[a]) for a in range(na) for j in range(3)]
        for cp in sends:
            cp.start()
        passed = []
        for a in range(na):
            for j in range(3):
                kj = 2 * chips[j][0] + chips[j][1]
                ici(a, j, kj).wait_recv()
                if a < nh:
                    cp = fwd(a, j, c)
                    cp.start()
                    passed.append(cp)
        for a in range(nh):
            for j in range(3):
                fwd(a, j, 1 - c).wait_recv()
        for cp in sends + passed:
            cp.wait_send()
        for cp in local:
            cp.wait()

    any_spec = pl.BlockSpec(memory_space=pl.ANY)
    return pl.pallas_call(
        body, in_specs=[any_spec] * na, out_specs=[any_spec] * na,
        out_shape=[_sds((N_CHIPS,) + a.shape, a.dtype) for a in arrays],
        scratch_shapes=[pltpu.SemaphoreType.DMA((3 * na,)), pltpu.SemaphoreType.DMA((3 * na,)),
                        pltpu.SemaphoreType.DMA((3 * nh,)), pltpu.SemaphoreType.DMA((3 * nh,)),
                        pltpu.SemaphoreType.DMA((na,))],
        name="all_gather_weights")(*arrays)


def _sibling_halves(grads):
    na = len(grads)

    def body(*refs):
        ins, outs = refs[:na], refs[na:2 * na]
        send_sem, recv_sem = refs[2 * na:]
        x, y, c, _ = _place()
        copies = []
        for a in range(na):
            rh = grads[a].shape[1] // 2
            src = ins[a].at[:, pl.ds(pl.multiple_of((1 - c) * rh, 8), rh), :]
            copies.append(pltpu.make_async_remote_copy(
                src_ref=src, dst_ref=outs[a], send_sem=send_sem.at[a], recv_sem=recv_sem.at[a],
                device_id=(x, y, 1 - c), device_id_type=MESH))
        for cp in copies:
            cp.start()
        for cp in copies:
            cp.wait()

    any_spec = pl.BlockSpec(memory_space=pl.ANY)
    return pl.pallas_call(
        body, in_specs=[any_spec] * na, out_specs=[any_spec] * na,
        out_shape=[_sds((N_CHIPS, g.shape[1] // 2, g.shape[2]), F32) for g in grads],
        scratch_shapes=[pltpu.SemaphoreType.DMA((na,)), pltpu.SemaphoreType.DMA((na,))],
        name="sibling_halves")(*grads)


def _pair_sum(g, recv, c, tr, name):
    _, r, cols = g.shape
    rh = r // 2
    nr = rh // tr

    def body(c_ref, g_ref, r_ref, o_ref):
        o_ref[...] = g_ref[...] + r_ref[...]

    grid_spec = pltpu.PrefetchScalarGridSpec(
        num_scalar_prefetch=1, grid=(N_CHIPS, nr),
        in_specs=[pl.BlockSpec((1, tr, cols), lambda k, i, c_ref: (k, c_ref[0] * nr + i, 0)),
                  pl.BlockSpec((1, tr, cols), lambda k, i, c_ref: (k, i, 0))],
        out_specs=pl.BlockSpec((1, tr, cols), lambda k, i, c_ref: (k, i, 0)))
    return pl.pallas_call(
        body, grid_spec=grid_spec, out_shape=_sds((N_CHIPS, rh, cols), F32),
        compiler_params=_cp(("parallel", "parallel")), name=name)(c, g, recv)


def _chip_exchange(halves, small):
    na = len(halves)
    srows = small.shape[0]

    def body(*refs):
        ins, small_ref = refs[:na], refs[na]
        outs, small_out = refs[na + 1:2 * na + 1], refs[2 * na + 1]
        send_sem, recv_sem, s_send, s_recv = refs[2 * na + 2:]
        x, y, c, chips = _place()
        me = 4 * x + 2 * y + c
        copies = []
        for a in range(na):
            for j in range(3):
                kj = 2 * chips[j][0] + chips[j][1]
                copies.append(pltpu.make_async_remote_copy(
                    src_ref=ins[a].at[kj], dst_ref=outs[a].at[j],
                    send_sem=send_sem.at[3 * a + j], recv_sem=recv_sem.at[3 * a + j],
                    device_id=(*chips[j], c), device_id_type=MESH))
        for r in range(1, N_DEV):
            peer = (x ^ (r >> 2), y ^ ((r >> 1) & 1), c ^ (r & 1))
            copies.append(pltpu.make_async_remote_copy(
                src_ref=small_ref, dst_ref=small_out.at[me],
                send_sem=s_send.at[r - 1], recv_sem=s_recv.at[r - 1], device_id=peer, device_id_type=MESH))
        for cp in copies:
            cp.start()
        small_out[pl.ds(me, 1)] = small_ref[...][None]
        for cp in copies:
            cp.wait()

    any_spec = pl.BlockSpec(memory_space=pl.ANY)
    vmem = pl.BlockSpec(memory_space=pltpu.VMEM)
    return pl.pallas_call(
        body, in_specs=[any_spec] * na + [vmem], out_specs=[any_spec] * na + [vmem],
        out_shape=[_sds((3,) + h.shape[1:], F32) for h in halves] + [_sds((N_DEV, srows, 128), F32)],
        scratch_shapes=[pltpu.SemaphoreType.DMA((3 * na,)), pltpu.SemaphoreType.DMA((3 * na,)),
                        pltpu.SemaphoreType.DMA((N_DEV - 1,)), pltpu.SemaphoreType.DMA((N_DEV - 1,))],
        name="chip_exchange")(*halves, small)


def _chip_sum(half, recv, me, tr, name):
    _, rh, cols = half.shape

    def body(me_ref, h_ref, r0_ref, r1_ref, r2_ref, o_ref):
        o_ref[...] = ((h_ref[0] + r0_ref[0]) + r1_ref[0]) + r2_ref[0]

    rspec = lambda j: pl.BlockSpec((1, tr, cols), lambda i, me_ref: (j, i, 0))
    grid_spec = pltpu.PrefetchScalarGridSpec(
        num_scalar_prefetch=1, grid=(rh // tr,),
        in_specs=[pl.BlockSpec((1, tr, cols), lambda i, me_ref: (me_ref[0], i, 0)), rspec(0), rspec(1), rspec(2)],
        out_specs=pl.BlockSpec((tr, cols), lambda i, me_ref: (i, 0)))
    return pl.pallas_call(
        body, grid_spec=grid_spec, out_shape=_sds((rh, cols), F32),
        compiler_params=_cp(("parallel",)), name=name)(me, half, recv, recv, recv)


def _sibling_exchange(halves):
    na = len(halves)

    def body(*refs):
        ins, outs = refs[:na], refs[na:2 * na]
        send_sem, recv_sem, loc_sem = refs[2 * na:]
        x, y, c, _ = _place()
        copies, local = [], []
        for a in range(na):
            rh = halves[a].shape[0]
            mine = _rows(outs[a], None, c, rh)
            local.append(pltpu.make_async_copy(ins[a], mine, loc_sem.at[a]))
            copies.append(pltpu.make_async_remote_copy(
                src_ref=ins[a], dst_ref=mine, send_sem=send_sem.at[a], recv_sem=recv_sem.at[a],
                device_id=(x, y, 1 - c), device_id_type=MESH))
        for cp in local + copies:
            cp.start()
        for a in range(na):
            copies[a].wait_send()
            rh = halves[a].shape[0]
            other = _rows(outs[a], None, 1 - c, rh)
            pltpu.make_async_remote_copy(
                src_ref=ins[a], dst_ref=other, send_sem=send_sem.at[a], recv_sem=recv_sem.at[a],
                device_id=(x, y, 1 - c), device_id_type=MESH).wait_recv()
        for cp in local:
            cp.wait()

    any_spec = pl.BlockSpec(memory_space=pl.ANY)
    return pl.pallas_call(
        body, in_specs=[any_spec] * na, out_specs=[any_spec] * na,
        out_shape=[_sds((2 * h.shape[0], h.shape[1]), F32) for h in halves],
        scratch_shapes=[pltpu.SemaphoreType.DMA((na,)), pltpu.SemaphoreType.DMA((na,)),
                        pltpu.SemaphoreType.DMA((na,))],
        name="sibling_exchange")(*halves)


def _device_sum(stack):
    _, rows, _ = stack.shape

    def body(s_ref, o_ref):
        acc = s_ref[0]
        for d in range(1, N_DEV):
            acc = acc + s_ref[d]
        o_ref[...] = acc

    return pl.pallas_call(body, out_shape=_sds((rows, 128), F32), name="device_sum")(stack)


def _pack(parts, rows):
    flat = jnp.concatenate([p.reshape(-1) for p in parts])
    return jnp.pad(flat, (0, rows * 128 - flat.shape[0])).reshape(rows, 128)


def _unpack(buf, shapes):
    flat = buf.reshape(-1)
    out, off = [], 0
    for shp in shapes:
        n = 1
        for d in shp:
            n *= d
        out.append(flat[off:off + n].reshape(shp))
        off += n
    return out


def _rows_for(n):
    return -(-n // (8 * 128)) * 8


def _forward_backward(xs, pos, tgt, win, wup, wout, wdown, wpool, cw, attn_norm, b_gate, pool_scale, q_norm,
                      k_norm, sinks, ffn_norm, conv_b):
    s = xs.shape[0]
    tm = min(256, s)
    tk = min(512, s)
    inv_freq = ROPE_THETA ** (-jnp.arange(0, ROPE_DIM, 2, dtype=F32) / ROPE_DIM)
    lane = jnp.arange(2 * HEAD) % HEAD
    invf = jnp.where(lane < ROPE_DIM, inv_freq[lane % (ROPE_DIM // 2)], 0.0).reshape(1, 2 * HEAD)
    wq = jnp.tile(q_norm, (1, N_Q))
    wk = jnp.tile(k_norm, (1, N_KV))
    head_of = jnp.arange(Q_W) // HEAD
    bd = (head_of[:, None] == head_of[None, :]).astype(BF)
    sink = sinks[0]

    h1, u, qkv, gates = _attn_in_proj(xs, attn_norm, win, b_gate, tm)
    qh, kh, vh = _qk_prep(qkv, pos, wq, wk, invf, bd, tm)
    battn = _attn_fwd(qh, kh, vh, sink)
    apool, pooled = _pool_fwd(u, wpool, pool_scale, min(512, s))
    mix, y1, h2 = _mix_out_proj(apool, battn, gates, xs, wout, ffn_norm, tm)
    pre_g, pre_v, up_g, up_v, act = _ffn_up(h2, wup, cw, conv_b, tm)
    dy2, dy2b, loss_acc = _ffn_down_loss(act, wdown, y1, tgt, tm)

    d_wdown = _grad_matmul(act, dy2b, 512, tk, "grad_w_down")
    dp_g, dp_v, dcw_g, dcw_v, dcb_g, dcb_v = _ffn_act_bwd(dy2b, wdown, up_g, up_v, pre_g, pre_v, cw, tm)
    d_wup = _grad_matmul(h2, dp_g, UP_SHARD, tk, "grad_w_up_gate", lead=N_CHIPS)
    d_wup = _grad_matmul(h2, dp_v, UP_SHARD, tk, "grad_w_up_value", lead=N_CHIPS, prev=d_wup, lead_off=2)
    dy1, d_ffn_norm = _ffn_up_bwd(dp_g, dp_v, wup, y1, dy2, ffn_norm, tm)
    da, db, dzg, d_bgate, d_wout = _out_proj_bwd(dy1, wout, apool, battn, gates, mix, tm)
    du, d_wpool, d_pscale = _pool_bwd(da, pooled, wpool, pool_scale, min(512, s))
    dqh, dkh, dvh, dsink = _attn_bwd(qh, kh, vh, sink, db)
    dzq, d_qn, d_kn = _qk_prep_bwd(dqh, dkh, dvh, qkv, pos, wq, wk, invf, bd, tm)
    grad_x, d_attn_norm = _in_proj_bwd(du, dzq, dzg, win, xs, attn_norm, dy1, tm)
    d_win = jnp.concatenate([
        _grad_matmul(h1, du, D, tk, "grad_w_in_pool"),
        _grad_matmul(h1, dzq, QKV_W, tk, "grad_w_in_qkv"),
        _grad_matmul(h1, dzg, D, tk, "grad_w_in_gates")], axis=1)
    small_parts = [d_attn_norm, d_bgate, d_pscale, d_qn, d_kn, dsink[:, 0, 0:GQA], d_ffn_norm,
                   jnp.concatenate([dcb_g, dcb_v], axis=1), jnp.concatenate([dcw_g, dcw_v], axis=0)]
    return loss_acc, grad_x, d_win, d_wup, d_wout, d_wdown, d_wpool, small_parts


def kernel(x, positions, attn_norm, w_in, b_gate, w_pool, pool_scale, q_norm, k_norm, sinks, w_out, ffn_norm, w_up, conv_w, conv_b, w_down, loss_target, m_attn_norm, m_w_in, m_b_gate, m_w_pool, m_pool_scale, m_q_norm, m_k_norm, m_sinks, m_w_out, m_ffn_norm, m_w_up, m_conv_w, m_conv_b, m_w_down, v_attn_norm, v_w_in, v_b_gate, v_w_pool, v_pool_scale, v_q_norm, v_k_norm, v_sinks, v_w_out, v_ffn_norm, v_w_up, v_conv_w, v_conv_b, v_w_down):
    s = x.shape[1]
    xs = x[0]
    tgt = loss_target[0]
    pos = positions[0].reshape(s, 1)
    cx, cy, cc = lax.axis_index("x"), lax.axis_index("y"), lax.axis_index("c")
    chip = 2 * cx + cy

    shards = [w_in[0].astype(BF), w_up[0].astype(BF), w_out[0].astype(BF), w_down[0].astype(BF),
              w_pool[0].reshape(4 * 64, POOL_GROUP).astype(BF)]
    g_in, g_up, g_out, g_down, g_pool, g_cw = _all_gather_weights(shards, [conv_w[0]])
    win = g_in.transpose(1, 0, 2).reshape(D, IN_W)
    wout = g_out.reshape(D, D)
    wdown = g_down.reshape(D_FF, D)
    wpool = g_pool.reshape(N_CHIPS, 4, 64, POOL_GROUP).transpose(1, 0, 2, 3).reshape(4, POOL_GROUP, POOL_GROUP)

    loss_acc, grad_x, d_win, d_wup, d_wout, d_wdown, d_wpool, small_parts = _forward_backward(
        xs, pos, tgt, win, g_up, wout, wdown, wpool, g_cw, attn_norm, b_gate, pool_scale, q_norm, k_norm, sinks,
        ffn_norm, conv_b)
    loss = lax.psum(jnp.sum(loss_acc) * (0.5 / D), ("x", "y", "c"))

    big = [d_win.reshape(D, N_CHIPS, IN_W // N_CHIPS).transpose(1, 0, 2),
           d_wup,
           d_wout.reshape(N_CHIPS, D // N_CHIPS, D),
           d_wdown.reshape(N_CHIPS, D_FF // N_CHIPS, D),
           d_wpool.reshape(4, N_CHIPS, 64, POOL_GROUP).transpose(1, 0, 2, 3).reshape(N_CHIPS, 4 * 64, POOL_GROUP)]
    names = ["w_in", "w_up", "w_out", "w_down", "w_pool"]
    row_tile = [256, 256, 128, 176, 128]
    small_shapes = [(1, D), (1, 2 * D), (1, D), (1, HEAD), (1, HEAD), (1, N_Q), (1, D), (1, 2 * D_FF),
                    (N_CHIPS, 3, UP_SHARD)]
    n_small = sum(p.size for p in small_parts)
    small = _pack(small_parts, _rows_for(n_small))

    c_arr = cc.reshape(1).astype(jnp.int32)
    chip_arr = chip.reshape(1).astype(jnp.int32)
    from_sibling = _sibling_halves(big)
    halves = [_pair_sum(g, r, c_arr, t, "pair_sum_" + nm) for g, r, t, nm in zip(big, from_sibling, row_tile, names)]
    *from_chips, small_all = _chip_exchange(halves, small)
    mine = [_chip_sum(h, r, chip_arr, t, "chip_sum_" + nm) for h, r, t, nm in zip(halves, from_chips, row_tile, names)]
    g_shards = _sibling_exchange(mine)
    small_sum = _device_sum(small_all)
    (g_attn_norm, g_bgate, g_pscale, g_qn, g_kn, g_sinks, g_ffn_norm, g_convb, g_convw_all) = _unpack(
        small_sum, small_shapes)
    g_convw = lax.dynamic_index_in_dim(g_convw_all, chip, axis=0, keepdims=False)

    def two_d(a):
        return a.reshape(-1, a.shape[-1])

    big_w = [w_in, w_up, w_out, w_down, w_pool]
    big_m = [m_w_in, m_w_up, m_w_out, m_w_down, m_w_pool]
    big_v = [v_w_in, v_w_up, v_w_out, v_w_down, v_w_pool]
    big_out = {}
    for nm, w, g, m, v, t in zip(names, big_w, g_shards, big_m, big_v, [256, 256, 128, 176, 128]):
        res = _adamw(two_d(w), g, two_d(m), two_d(v), t, "adamw_" + nm)
        big_out[nm] = [r.reshape(w.shape) for r in res]

    small_names = ["attn_norm", "b_gate", "pool_scale", "q_norm", "k_norm", "sinks", "ffn_norm", "conv_b", "conv_w"]
    sm_w = [attn_norm, b_gate, pool_scale, q_norm, k_norm, sinks, ffn_norm, conv_b, conv_w]
    sm_m = [m_attn_norm, m_b_gate, m_pool_scale, m_q_norm, m_k_norm, m_sinks, m_ffn_norm, m_conv_b, m_conv_w]
    sm_v = [v_attn_norm, v_b_gate, v_pool_scale, v_q_norm, v_k_norm, v_sinks, v_ffn_norm, v_conv_b, v_conv_w]
    sm_g = [g_attn_norm, g_bgate, g_pscale, g_qn, g_kn, g_sinks, g_ffn_norm, g_convb, g_convw]
    sm_rows = _rows_for(sum(w.size for w in sm_w))
    res = _adamw(_pack(sm_w, sm_rows), _pack(sm_g, sm_rows), _pack(sm_m, sm_rows), _pack(sm_v, sm_rows),
                 sm_rows, "adamw_small")
    sm_out = [_unpack(r, [w.shape for w in sm_w]) for r in res]
    small_out = {nm: [sm_out[k][i] for k in range(4)] for i, nm in enumerate(small_names)}

    order = ["attn_norm", "w_in", "b_gate", "w_pool", "pool_scale", "q_norm", "k_norm", "sinks", "w_out",
             "ffn_norm", "w_up", "conv_w", "conv_b", "w_down"]
    allout = {**big_out, **small_out}
    outs = [loss, grad_x[None]]
    for k in range(4):
        outs += [allout[nm][k] for nm in order]
    return tuple(outs)
```

```python
import functools

import jax
import jax.numpy as jnp
from jax import lax
from jax.experimental import pallas as pl
from jax.experimental.pallas import tpu as pltpu

D = 1024
D_FF = 2816
HEAD = 64
N_Q = 16
N_KV = 2
GQA = 8
BLK = 128
ROPE_DIM = 16
ROPE_THETA = 500000.0
POOL_GROUP = 256
Q_W = 1024
KV_W = 128
QKV_W = Q_W + 2 * KV_W
IN_W = 4352
UP_SHARD = 1408
EPS = 1e-6
N_CHIPS = 4
N_DEV = 8

LR = 0.001
B1 = 0.9
B2 = 0.999
ADAM_EPS = 1e-08
WD = 0.01
STEP = 10

BF = jnp.bfloat16
F32 = jnp.float32
MESH = pl.DeviceIdType.MESH
VMEM_LIMIT_MB = 56


def _cp(sem, vmem_mb=VMEM_LIMIT_MB):
    return pltpu.CompilerParams(dimension_semantics=sem, vmem_limit_bytes=vmem_mb << 20)


def _full(shape):
    nd = len(shape)
    return pl.BlockSpec(shape, lambda *_: (0,) * nd)


def _sds(shape, dtype):
    return jax.ShapeDtypeStruct(shape, dtype)


def _nt(a, b):
    return lax.dot_general(a, b, (((1,), (1,)), ((), ())), preferred_element_type=F32)


def _tn(a, b):
    return lax.dot_general(a, b, (((0,), (0,)), ((), ())), preferred_element_type=F32)


def _mm(a, b):
    return jnp.dot(a, b, preferred_element_type=F32)


def _head_sum(v, bd):
    hi = v.astype(BF)
    lo = (v - hi.astype(F32)).astype(BF)
    return _mm(hi, bd) + _mm(lo, bd)


def _rope_tables(pos_ref, invf_ref):
    ang = pos_ref[...].astype(F32) * invf_ref[...]
    cos = jnp.cos(ang)
    sin = jnp.sin(ang)
    lane = lax.broadcasted_iota(jnp.int32, (1, 2 * HEAD), 1) % HEAD
    sa = jnp.where(lane < ROPE_DIM // 2, -sin, 0.0)
    sb = jnp.where(lane < ROPE_DIM // 2, 0.0, jnp.where(lane < ROPE_DIM, sin, 0.0))
    return cos, sa, sb


def _tile_lanes(t, reps):
    return t if reps == 1 else jnp.tile(t, (1, reps))


def _rope(v, cos, sa, sb):
    w = v.shape[1]
    reps = w // (2 * HEAD)
    half = ROPE_DIM // 2
    return (v * _tile_lanes(cos, reps) + pltpu.roll(v, w - half, 1) * _tile_lanes(sa, reps)
            + pltpu.roll(v, half, 1) * _tile_lanes(sb, reps))


def _rope_t(dy, cos, sa, sb):
    w = dy.shape[1]
    reps = w // (2 * HEAD)
    half = ROPE_DIM // 2
    return (dy * _tile_lanes(cos, reps) + pltpu.roll(dy * _tile_lanes(sa, reps), half, 1)
            + pltpu.roll(dy * _tile_lanes(sb, reps), w - half, 1))


def _attn_in_proj(x, g1, w_in, b_gate, tm):
    s = x.shape[0]

    def body(x_ref, g_ref, w_ref, b_ref, h_ref, u_ref, qkv_ref, gate_ref):
        xv = x_ref[...]
        r = lax.rsqrt(jnp.mean(xv * xv, axis=-1, keepdims=True) + EPS)
        h = (xv * r * g_ref[...]).astype(BF)
        h_ref[...] = h
        u_ref[...] = _mm(h, w_ref[:, 0:D])
        qkv_ref[...] = _mm(h, w_ref[:, D:D + QKV_W])
        gate_ref[...] = jax.nn.sigmoid(_mm(h, w_ref[:, D + QKV_W:IN_W]) + b_ref[...])

    row = lambda w: pl.BlockSpec((tm, w), lambda i: (i, 0))
    return pl.pallas_call(
        body, grid=(s // tm,),
        in_specs=[row(D), _full((1, D)), _full((D, IN_W)), _full((1, 2 * D))],
        out_specs=[row(D), row(D), row(QKV_W), row(2 * D)],
        out_shape=[_sds((s, D), BF), _sds((s, D), F32), _sds((s, QKV_W), F32), _sds((s, 2 * D), F32)],
        compiler_params=_cp(("parallel",)), name="attn_in_proj")(x, g1, w_in, b_gate)


def _qk_prep(qkv, pos, wq, wk, invf, bd, tm):
    s = qkv.shape[0]

    def body(qkv_ref, pos_ref, wq_ref, wk_ref, invf_ref, bd_ref, qh_ref, kh_ref, vh_ref):
        cos, sa, sb = _rope_tables(pos_ref, invf_ref)
        q = qkv_ref[:, 0:Q_W]
        k = qkv_ref[:, Q_W:Q_W + KV_W]
        v = qkv_ref[:, Q_W + KV_W:QKV_W]
        rq = lax.rsqrt(_head_sum(q * q, bd_ref[...]) * (1.0 / HEAD) + EPS)
        qr = _rope(q * rq * wq_ref[...], cos, sa, sb) * (HEAD ** -0.5)
        rk = lax.rsqrt(_head_sum(k * k, bd_ref[0:KV_W, 0:KV_W]) * (1.0 / HEAD) + EPS)
        kr = _rope(k * rk * wk_ref[...], cos, sa, sb)
        for h in range(N_Q):
            qh_ref[h] = qr[:, HEAD * h:HEAD * (h + 1)].astype(BF)
        for h in range(N_KV):
            kh_ref[h] = kr[:, HEAD * h:HEAD * (h + 1)].astype(BF)
            vh_ref[h] = v[:, HEAD * h:HEAD * (h + 1)].astype(BF)

    heads = lambda n: pl.BlockSpec((n, tm, HEAD), lambda i: (0, i, 0))
    return pl.pallas_call(
        body, grid=(s // tm,),
        in_specs=[pl.BlockSpec((tm, QKV_W), lambda i: (i, 0)), pl.BlockSpec((tm, 1), lambda i: (i, 0)),
                  _full((1, Q_W)), _full((1, KV_W)), _full((1, 2 * HEAD)), _full((Q_W, Q_W))],
        out_specs=[heads(N_Q), heads(N_KV), heads(N_KV)],
        out_shape=[_sds((N_Q, s, HEAD), BF), _sds((N_KV, s, HEAD), BF), _sds((N_KV, s, HEAD), BF)],
        compiler_params=_cp(("parallel",)), name="qk_prep")(qkv, pos, wq, wk, invf, bd)


def _sink_column(sink_ref, kh):
    row_g = lax.broadcasted_iota(jnp.int32, (GQA * BLK, 1), 0) // BLK
    col = jnp.zeros((GQA * BLK, 1), F32)
    for g in range(GQA):
        col = jnp.where(row_g == g, sink_ref[kh * GQA + g], col)
    return col


def _attn_probs(q, k, n, sink_col):
    sc = _nt(q, k)
    qi = lax.broadcasted_iota(jnp.int32, sc.shape, 0) % BLK + BLK
    ki = lax.broadcasted_iota(jnp.int32, sc.shape, 1)
    lo = jnp.where(n > 0, qi - BLK, BLK - 1)
    ok = (ki <= qi) & (ki > lo)
    sc = jnp.where(ok, sc, -jnp.inf)
    m = jnp.maximum(jnp.max(sc, axis=-1, keepdims=True), sink_col)
    p = jnp.exp(sc - m)
    es = jnp.exp(sink_col - m)
    inv = 1.0 / (jnp.sum(p, axis=-1, keepdims=True) + es)
    return p * inv, es * inv


def _attn_fwd(qh, kh, vh, sinks):
    s = qh.shape[1]
    nb = s // BLK

    def body(sink_ref, q_ref, kp_ref, kc_ref, vp_ref, vc_ref, o_ref):
        khd = pl.program_id(0)
        n = pl.program_id(1)
        q = q_ref[...].reshape(GQA * BLK, HEAD)
        k = jnp.concatenate([kp_ref[0], kc_ref[0]], axis=0)
        v = jnp.concatenate([vp_ref[0], vc_ref[0]], axis=0)
        probs, _ = _attn_probs(q, k, n, _sink_column(sink_ref, khd))
        o = _mm(probs.astype(BF), v)
        for j in range(GQA // 2):
            o_ref[:, 2 * HEAD * j:2 * HEAD * (j + 1)] = jnp.concatenate(
                [o[2 * j * BLK:(2 * j + 1) * BLK], o[(2 * j + 1) * BLK:(2 * j + 2) * BLK]], axis=1)

    prev = pl.BlockSpec((1, BLK, HEAD), lambda h, n: (h, jnp.maximum(n - 1, 0), 0))
    cur = pl.BlockSpec((1, BLK, HEAD), lambda h, n: (h, n, 0))
    return pl.pallas_call(
        body, grid=(N_KV, nb),
        in_specs=[pl.BlockSpec(memory_space=pltpu.SMEM),
                  pl.BlockSpec((GQA, BLK, HEAD), lambda h, n: (h, n, 0)), prev, cur, prev, cur],
        out_specs=pl.BlockSpec((BLK, GQA * HEAD), lambda h, n: (n, h)),
        out_shape=_sds((s, Q_W), F32),
        compiler_params=_cp(("parallel", "parallel")), name="attn_fwd")(sinks, qh, kh, kh, vh, vh)


def _pool_fwd(u, w_pool, pool_scale, ts):
    s = u.shape[0]
    halo = 16

    def body(u_ref, wp_ref, ps_ref, a_ref, pooled_ref, buf):
        g = pl.program_id(0)
        i = pl.program_id(1)

        @pl.when(i == 0)
        def _():
            buf[0:halo, :] = jnp.zeros((halo, POOL_GROUP), F32)

        buf[halo:halo + ts, :] = u_ref[...]
        t = (i * ts + lax.broadcasted_iota(jnp.int32, (ts, 1), 0)).astype(F32)
        for gi in range(4):
            @pl.when(g == gi)
            def _(gi=gi):
                w = 2 << gi
                cur = buf[pl.ds(halo, ts), :]
                acc = cur
                for j in range(1, w):
                    acc = acc + buf[pl.ds(halo - j, ts), :]
                pooled = (acc / jnp.minimum(t + 1.0, float(w)) - cur).astype(BF)
                pooled_ref[...] = pooled
                a_ref[...] = _mm(pooled, wp_ref[0]) * ps_ref[...]

        buf[0:halo, :] = buf[ts:ts + halo, :]

    col = pl.BlockSpec((ts, POOL_GROUP), lambda g, i: (i, g))
    return pl.pallas_call(
        body, grid=(4, s // ts),
        in_specs=[col, pl.BlockSpec((1, POOL_GROUP, POOL_GROUP), lambda g, i: (g, 0, 0)),
                  pl.BlockSpec((1, POOL_GROUP), lambda g, i: (0, g))],
        out_specs=[col, col],
        out_shape=[_sds((s, D), F32), _sds((s, D), BF)],
        scratch_shapes=[pltpu.VMEM((halo + ts, POOL_GROUP), F32)],
        compiler_params=_cp(("parallel", "arbitrary")), name="pool_fwd")(u, w_pool, pool_scale)


def _mix_out_proj(a, b, gates, x, w_out, g2, tm):
    s = x.shape[0]

    def body(a_ref, b_ref, gate_ref, x_ref, w_ref, g_ref, mix_ref, y_ref, h_ref):
        mix = (gate_ref[:, 0:D] * a_ref[...] + gate_ref[:, D:2 * D] * b_ref[...]).astype(BF)
        mix_ref[...] = mix
        y = x_ref[...] + _mm(mix, w_ref[...])
        y_ref[...] = y
        r = lax.rsqrt(jnp.mean(y * y, axis=-1, keepdims=True) + EPS)
        h_ref[...] = (y * r * g_ref[...]).astype(BF)

    row = lambda w: pl.BlockSpec((tm, w), lambda i: (i, 0))
    return pl.pallas_call(
        body, grid=(s // tm,),
        in_specs=[row(D), row(D), row(2 * D), row(D), _full((D, D)), _full((1, D))],
        out_specs=[row(D), row(D), row(D)],
        out_shape=[_sds((s, D), BF), _sds((s, D), F32), _sds((s, D), BF)],
        compiler_params=_cp(("parallel",)), name="mix_out_proj")(a, b, gates, x, w_out, g2)


def _ffn_up(h2, w_up, conv_w, conv_b, tm):
    s = h2.shape[0]

    def body(h_ref, wg_ref, wv_ref, cwg_ref, cwv_ref, cbg_ref, cbv_ref,
             preg_ref, prev_ref, upg_ref, upv_ref, act_ref, halog, halov):
        i = pl.program_id(1)

        @pl.when(i == 0)
        def _():
            halog[...] = jnp.zeros_like(halog)
            halov[...] = jnp.zeros_like(halov)

        h = h_ref[...]

        def conv_half(w_ref, cw_ref, cb_ref, halo, pre_ref, up_ref):
            pre = _mm(h, w_ref[0])
            pre_ref[...] = pre.astype(BF)
            ext = jnp.concatenate([halo[...], pre], axis=0)
            cw = cw_ref[0]
            up = cb_ref[...] + cw[0:1] * pltpu.roll(ext, 2, 0)[8:8 + tm]
            up = up + cw[1:2] * pltpu.roll(ext, 1, 0)[8:8 + tm]
            up = up + cw[2:3] * pre
            halo[...] = pre[tm - 8:tm]
            up_ref[...] = up.astype(BF)
            return up

        gate = conv_half(wg_ref, cwg_ref, cbg_ref, halog, preg_ref, upg_ref)
        val = conv_half(wv_ref, cwv_ref, cbv_ref, halov, prev_ref, upv_ref)
        act_ref[...] = (gate * jax.nn.sigmoid(gate) * val).astype(BF)

    tile = pl.BlockSpec((tm, UP_SHARD), lambda j, i: (i, j))
    wspec = lambda off: pl.BlockSpec((1, D, UP_SHARD), lambda j, i: (j + off, 0, 0))
    cwspec = lambda off: pl.BlockSpec((1, 3, UP_SHARD), lambda j, i: (j + off, 0, 0))
    cbspec = lambda off: pl.BlockSpec((1, UP_SHARD), lambda j, i: (0, j + off))
    half = _sds((s, D_FF), BF)
    return pl.pallas_call(
        body, grid=(2, s // tm),
        in_specs=[pl.BlockSpec((tm, D), lambda j, i: (i, 0)), wspec(0), wspec(2), cwspec(0), cwspec(2),
                  cbspec(0), cbspec(2)],
        out_specs=[tile] * 5, out_shape=[half] * 5,
        scratch_shapes=[pltpu.VMEM((8, UP_SHARD), F32), pltpu.VMEM((8, UP_SHARD), F32)],
        compiler_params=_cp(("parallel", "arbitrary")), name="ffn_up")(
            h2, w_up, w_up, conv_w, conv_w, conv_b, conv_b)


def _ffn_down_loss(act, w_down, y1, tgt, tm):
    s = y1.shape[0]

    def body(act_ref, w_ref, y_ref, t_ref, dy_ref, dyb_ref, loss_ref):
        @pl.when(pl.program_id(0) == 0)
        def _():
            loss_ref[...] = jnp.zeros_like(loss_ref)

        e = y_ref[...] + _mm(act_ref[...], w_ref[...]) - t_ref[...]
        dy = e * (1.0 / D)
        dy_ref[...] = dy
        dyb_ref[...] = dy.astype(BF)
        e2 = (e * e).reshape(tm // 8, 8, D).sum(axis=0)
        part = e2[:, 0:128]
        for j in range(1, D // 128):
            part = part + e2[:, 128 * j:128 * (j + 1)]
        loss_ref[...] += part

    row = lambda w: pl.BlockSpec((tm, w), lambda i: (i, 0))
    return pl.pallas_call(
        body, grid=(s // tm,),
        in_specs=[row(D_FF), _full((D_FF, D)), row(D), row(D)],
        out_specs=[row(D), row(D), _full((8, 128))],
        out_shape=[_sds((s, D), F32), _sds((s, D), BF), _sds((8, 128), F32)],
        compiler_params=_cp(("arbitrary",)), name="ffn_down_loss")(act, w_down, y1, tgt)


def _grad_matmul(a, b, tn, tk, name, lead=None, prev=None, lead_off=0):
    s, m = a.shape
    n = b.shape[1]
    nj = n // tn

    def body(*refs):
        a_ref, b_ref = refs[0], refs[1]
        o_ref = refs[-1]
        acc = _tn(a_ref[...], b_ref[...])
        acc = acc if lead is None else acc[None]

        @pl.when(pl.program_id(1) == 0)
        def _():
            o_ref[...] = acc

        @pl.when(pl.program_id(1) > 0)
        def _():
            o_ref[...] += acc

    in_specs = [pl.BlockSpec((tk, m), lambda j, k: (k, 0)), pl.BlockSpec((tk, tn), lambda j, k: (k, j))]
    args = [a, b]
    aliases = {}
    if lead is None:
        out_spec = pl.BlockSpec((m, tn), lambda j, k: (0, j))
        out_shape = _sds((m, n), F32)
    else:
        out_spec = pl.BlockSpec((1, m, tn), lambda j, k: (j + lead_off, 0, 0))
        out_shape = _sds((lead, m, tn), F32)
        if prev is not None:
            in_specs.append(pl.BlockSpec(memory_space=pl.ANY))
            args.append(prev)
            aliases = {2: 0}
    return pl.pallas_call(
        body, grid=(nj, s // tk), in_specs=in_specs, out_specs=out_spec, out_shape=out_shape,
        input_output_aliases=aliases,
        compiler_params=_cp(("parallel", "arbitrary")), name=name)(*args)


def _ffn_act_bwd(dyb, w_down, up_g, up_v, pre_g, pre_v, conv_w, tm):
    s = dyb.shape[0]
    nt = s // tm

    def body(dy_ref, wd_ref, upg_ref, upv_ref, preg_ref, prev_ref, cwg_ref, cwv_ref,
             dpg_ref, dpv_ref, dcwg_ref, dcwv_ref, dcbg_ref, dcbv_ref, nxg, nxv):
        i = pl.program_id(1)

        @pl.when(i == 0)
        def _():
            nxg[...] = jnp.zeros_like(nxg)
            nxv[...] = jnp.zeros_like(nxv)
            dcwg_ref[...] = jnp.zeros_like(dcwg_ref)
            dcwv_ref[...] = jnp.zeros_like(dcwv_ref)
            dcbg_ref[...] = jnp.zeros_like(dcbg_ref)
            dcbv_ref[...] = jnp.zeros_like(dcbv_ref)

        dact = _nt(dy_ref[...], wd_ref[...])
        g = upg_ref[...].astype(F32)
        v = upv_ref[...].astype(F32)
        sg = jax.nn.sigmoid(g)
        d_v = dact * (g * sg)
        d_g = dact * v * (sg * (1.0 + g * (1.0 - sg)))

        def conv_bwd(d_up, nx, pre_ref, cw_ref, dp_ref, dcw_ref, dcb_ref):
            ext = jnp.concatenate([d_up, nx[...]], axis=0)
            s1 = pltpu.roll(ext, tm + 8 - 1, 0)[0:tm]
            s2 = pltpu.roll(ext, tm + 8 - 2, 0)[0:tm]
            cw = cw_ref[0]
            dp_ref[...] = (cw[2:3] * d_up + cw[1:2] * s1 + cw[0:1] * s2).astype(BF)
            nx[...] = d_up[0:8]
            pre = pre_ref[...].astype(F32)
            dcw_ref[0, 0:1, :] += jnp.sum(s2 * pre, axis=0, keepdims=True)
            dcw_ref[0, 1:2, :] += jnp.sum(s1 * pre, axis=0, keepdims=True)
            dcw_ref[0, 2:3, :] += jnp.sum(d_up * pre, axis=0, keepdims=True)
            dcb_ref[...] += jnp.sum(d_up, axis=0, keepdims=True)

        conv_bwd(d_g, nxg, preg_ref, cwg_ref, dpg_ref, dcwg_ref, dcbg_ref)
        conv_bwd(d_v, nxv, prev_ref, cwv_ref, dpv_ref, dcwv_ref, dcbv_ref)

    tile = pl.BlockSpec((tm, UP_SHARD), lambda j, i: (nt - 1 - i, j))
    cwspec = lambda off: pl.BlockSpec((1, 3, UP_SHARD), lambda j, i: (j + off, 0, 0))
    acc_cw = pl.BlockSpec((1, 3, UP_SHARD), lambda j, i: (j, 0, 0))
    acc_cb = pl.BlockSpec((1, UP_SHARD), lambda j, i: (0, j))
    buf = pltpu.VMEM((8, UP_SHARD), F32)
    return pl.pallas_call(
        body, grid=(2, nt),
        in_specs=[pl.BlockSpec((tm, D), lambda j, i: (nt - 1 - i, 0)),
                  pl.BlockSpec((UP_SHARD, D), lambda j, i: (j, 0)),
                  tile, tile, tile, tile, cwspec(0), cwspec(2)],
        out_specs=[tile, tile, acc_cw, acc_cw, acc_cb, acc_cb],
        out_shape=[_sds((s, D_FF), BF), _sds((s, D_FF), BF), _sds((2, 3, UP_SHARD), F32),
                   _sds((2, 3, UP_SHARD), F32), _sds((1, D_FF), F32), _sds((1, D_FF), F32)],
        scratch_shapes=[buf, buf],
        compiler_params=_cp(("parallel", "arbitrary")), name="ffn_act_bwd")(
            dyb, w_down, up_g, up_v, pre_g, pre_v, conv_w, conv_w)


def _rms_bwd(dh, y, g):
    r = lax.rsqrt(jnp.mean(y * y, axis=-1, keepdims=True) + EPS)
    n = y * r
    dn = dh * g
    return r * (dn - n * jnp.mean(dn * n, axis=-1, keepdims=True)), dh * n


def _ffn_up_bwd(dp_g, dp_v, w_up, y1, dy2, g2, tm):
    s = y1.shape[0]

    def body(dg_ref, dv_ref, w_ref, y_ref, dy2_ref, g_ref, dy1_ref, dgn_ref):
        @pl.when(pl.program_id(0) == 0)
        def _():
            dgn_ref[...] = jnp.zeros_like(dgn_ref)

        dh = _nt(dg_ref[:, 0:UP_SHARD], w_ref[0])
        dh = dh + _nt(dg_ref[:, UP_SHARD:D_FF], w_ref[1])
        dh = dh + _nt(dv_ref[:, 0:UP_SHARD], w_ref[2])
        dh = dh + _nt(dv_ref[:, UP_SHARD:D_FF], w_ref[3])
        dy, dgn = _rms_bwd(dh, y_ref[...], g_ref[...])
        dy1_ref[...] = dy2_ref[...] + dy
        dgn_ref[...] += jnp.sum(dgn, axis=0, keepdims=True)

    row = lambda w: pl.BlockSpec((tm, w), lambda i: (i, 0))
    return pl.pallas_call(
        body, grid=(s // tm,),
        in_specs=[row(D_FF), row(D_FF), _full((4, D, UP_SHARD)), row(D), row(D), _full((1, D))],
        out_specs=[row(D), _full((1, D))],
        out_shape=[_sds((s, D), F32), _sds((1, D), F32)],
        compiler_params=_cp(("arbitrary",)), name="ffn_up_bwd")(dp_g, dp_v, w_up, y1, dy2, g2)


def _out_proj_bwd(dy1, w_out, a, b, gates, mix, tm):
    s = dy1.shape[0]

    def body(dy_ref, w_ref, a_ref, b_ref, gate_ref, mix_ref, da_ref, db_ref, dzg_ref, dbg_ref, dw_ref):
        @pl.when(pl.program_id(0) == 0)
        def _():
            dbg_ref[...] = jnp.zeros_like(dbg_ref)
            dw_ref[...] = jnp.zeros_like(dw_ref)

        dyb = dy_ref[...].astype(BF)
        dmix = _nt(dyb, w_ref[...])
        gp = gate_ref[:, 0:D]
        ga = gate_ref[:, D:2 * D]
        da_ref[...] = dmix * gp
        db_ref[...] = dmix * ga
        dzp = dmix * a_ref[...] * (gp * (1.0 - gp))
        dza = dmix * b_ref[...] * (ga * (1.0 - ga))
        dzg_ref[:, 0:D] = dzp.astype(BF)
        dzg_ref[:, D:2 * D] = dza.astype(BF)
        dbg_ref[:, 0:D] += jnp.sum(dzp, axis=0, keepdims=True)
        dbg_ref[:, D:2 * D] += jnp.sum(dza, axis=0, keepdims=True)
        dw_ref[...] += _tn(mix_ref[...], dyb)

    row = lambda w: pl.BlockSpec((tm, w), lambda i: (i, 0))
    return pl.pallas_call(
        body, grid=(s // tm,),
        in_specs=[row(D), _full((D, D)), row(D), row(D), row(2 * D), row(D)],
        out_specs=[row(D), row(D), row(2 * D), _full((1, 2 * D)), _full((D, D))],
        out_shape=[_sds((s, D), F32), _sds((s, D), F32), _sds((s, 2 * D), BF), _sds((1, 2 * D), F32),
                   _sds((D, D), F32)],
        compiler_params=_cp(("arbitrary",)), name="out_proj_bwd")(dy1, w_out, a, b, gates, mix)


def _pool_bwd(da, pooled, w_pool, pool_scale, ts):
    s = da.shape[0]
    nt = s // ts
    halo = 16

    def body(da_ref, pooled_ref, wp_ref, ps_ref, du_ref, dwp_ref, dps_ref, buf):
        g = pl.program_id(0)
        i = pl.program_id(1)
        ti = nt - 1 - i

        @pl.when(i == 0)
        def _():
            buf[ts:ts + halo, :] = jnp.zeros((halo, POOL_GROUP), F32)
            dwp_ref[...] = jnp.zeros_like(dwp_ref)
            dps_ref[...] = jnp.zeros_like(dps_ref)

        pooled = pooled_ref[...]
        dav = da_ref[...]
        dps_ref[...] += jnp.sum(dav * _mm(pooled, wp_ref[0]), axis=0, keepdims=True)
        dm = (dav * ps_ref[...]).astype(BF)
        dwp_ref[0] += _tn(pooled, dm)
        dpool = _nt(dm, wp_ref[0])
        t = (ti * ts + lax.broadcasted_iota(jnp.int32, (ts, 1), 0)).astype(F32)
        for gi in range(4):
            @pl.when(g == gi)
            def _(gi=gi):
                w = 2 << gi
                buf[0:ts, :] = dpool / jnp.minimum(t + 1.0, float(w))
                acc = buf[pl.ds(0, ts), :]
                for j in range(1, w):
                    acc = acc + buf[pl.ds(j, ts), :]
                du_ref[...] = (acc - dpool).astype(BF)

        buf[ts:ts + halo, :] = buf[0:halo, :]

    col = pl.BlockSpec((ts, POOL_GROUP), lambda g, i: (nt - 1 - i, g))
    return pl.pallas_call(
        body, grid=(4, nt),
        in_specs=[col, col, pl.BlockSpec((1, POOL_GROUP, POOL_GROUP), lambda g, i: (g, 0, 0)),
                  pl.BlockSpec((1, POOL_GROUP), lambda g, i: (0, g))],
        out_specs=[col, pl.BlockSpec((1, POOL_GROUP, POOL_GROUP), lambda g, i: (g, 0, 0)),
                   pl.BlockSpec((1, POOL_GROUP), lambda g, i: (0, g))],
        out_shape=[_sds((s, D), BF), _sds((4, POOL_GROUP, POOL_GROUP), F32), _sds((1, D), F32)],
        scratch_shapes=[pltpu.VMEM((ts + halo, POOL_GROUP), F32)],
        compiler_params=_cp(("parallel", "arbitrary")), name="pool_bwd")(da, pooled, w_pool, pool_scale)


def _attn_bwd(qh, kh, vh, sinks, db):
    s = qh.shape[1]
    nb = s // BLK

    def body(sink_ref, q_ref, kp_ref, kc_ref, vp_ref, vc_ref, do_ref,
             dq_ref, dk_ref, dv_ref, dsink_ref, ck, cv):
        khd = pl.program_id(0)
        n = pl.program_id(1)

        @pl.when(n == 0)
        def _():
            ck[...] = jnp.zeros_like(ck)
            cv[...] = jnp.zeros_like(cv)
            dsink_ref[...] = jnp.zeros_like(dsink_ref)

        @pl.when(n < nb)
        def _():
            q = q_ref[...].reshape(GQA * BLK, HEAD)
            k = jnp.concatenate([kp_ref[0], kc_ref[0]], axis=0)
            v = jnp.concatenate([vp_ref[0], vc_ref[0]], axis=0)
            dov = do_ref[...]
            do = jnp.concatenate([dov[:, HEAD * g:HEAD * (g + 1)] for g in range(GQA)], axis=0).astype(BF)
            probs, psink = _attn_probs(q, k, n, _sink_column(sink_ref, khd))
            dp = _nt(do, v)
            delta = jnp.sum(probs * dp, axis=-1, keepdims=True)
            ds = (probs * (dp - delta)).astype(BF)
            dq_ref[...] = _mm(ds, k).reshape(GQA, BLK, HEAD)
            dk = _tn(ds, q)
            dv = _tn(probs.astype(BF), do)
            dk_ref[0] = ck[...] + dk[0:BLK]
            dv_ref[0] = cv[...] + dv[0:BLK]
            ck[...] = dk[BLK:2 * BLK]
            cv[...] = dv[BLK:2 * BLK]
            dsk = psink * delta
            lane = lax.broadcasted_iota(jnp.int32, (1, 128), 1)
            acc = jnp.zeros((1, 128), F32)
            for g in range(GQA):
                acc = acc - jnp.where(lane == g, jnp.sum(dsk[g * BLK:(g + 1) * BLK], axis=0, keepdims=True), 0.0)
            dsink_ref[0] += acc

        @pl.when(n == nb)
        def _():
            dk_ref[0] = ck[...]
            dv_ref[0] = cv[...]

    last = nb - 1
    prev = pl.BlockSpec((1, BLK, HEAD), lambda h, n: (h, jnp.maximum(jnp.minimum(n, last) - 1, 0), 0))
    cur = pl.BlockSpec((1, BLK, HEAD), lambda h, n: (h, jnp.minimum(n, last), 0))
    kv_out = pl.BlockSpec((1, BLK, HEAD), lambda h, n: (h, jnp.maximum(n - 1, 0), 0))
    return pl.pallas_call(
        body, grid=(N_KV, nb + 1),
        in_specs=[pl.BlockSpec(memory_space=pltpu.SMEM),
                  pl.BlockSpec((GQA, BLK, HEAD), lambda h, n: (h, jnp.minimum(n, last), 0)),
                  prev, cur, prev, cur,
                  pl.BlockSpec((BLK, GQA * HEAD), lambda h, n: (jnp.minimum(n, last), h))],
        out_specs=[pl.BlockSpec((GQA, BLK, HEAD), lambda h, n: (h, jnp.minimum(n, last), 0)), kv_out, kv_out,
                   pl.BlockSpec((1, 1, 128), lambda h, n: (h, 0, 0))],
        out_shape=[_sds((N_Q, s, HEAD), F32), _sds((N_KV, s, HEAD), F32), _sds((N_KV, s, HEAD), F32),
                   _sds((N_KV, 1, 128), F32)],
        scratch_shapes=[pltpu.VMEM((BLK, HEAD), F32), pltpu.VMEM((BLK, HEAD), F32)],
        compiler_params=_cp(("parallel", "arbitrary")), name="attn_bwd")(sinks, qh, kh, kh, vh, vh, db)


def _qk_prep_bwd(dqh, dkh, dvh, qkv, pos, wq, wk, invf, bd, tm):
    s = qkv.shape[0]

    def fold_heads(row):
        out = row[:, 0:HEAD]
        for h in range(1, row.shape[1] // HEAD):
            out = out + row[:, HEAD * h:HEAD * (h + 1)]
        return out

    def body(dq_ref, dk_ref, dv_ref, qkv_ref, pos_ref, wq_ref, wk_ref, invf_ref, bd_ref,
             dz_ref, dwq_ref, dwk_ref):
        @pl.when(pl.program_id(0) == 0)
        def _():
            dwq_ref[...] = jnp.zeros_like(dwq_ref)
            dwk_ref[...] = jnp.zeros_like(dwk_ref)

        cos, sa, sb = _rope_tables(pos_ref, invf_ref)

        def norm_rope_bwd(dy, xin, w, bdm):
            dn = _rope_t(dy, cos, sa, sb)
            r = lax.rsqrt(_head_sum(xin * xin, bdm) * (1.0 / HEAD) + EPS)
            nh = xin * r
            gw = dn * w
            dx = r * (gw - nh * (_head_sum(gw * nh, bdm) * (1.0 / HEAD)))
            return dx, fold_heads(jnp.sum(dn * nh, axis=0, keepdims=True))

        dq = jnp.concatenate([dq_ref[h] for h in range(N_Q)], axis=1) * (HEAD ** -0.5)
        dk = jnp.concatenate([dk_ref[h] for h in range(N_KV)], axis=1)
        dxq, dwq = norm_rope_bwd(dq, qkv_ref[:, 0:Q_W], wq_ref[...], bd_ref[...])
        dxk, dwk = norm_rope_bwd(dk, qkv_ref[:, Q_W:Q_W + KV_W], wk_ref[...], bd_ref[0:KV_W, 0:KV_W])
        dz_ref[:, 0:Q_W] = dxq.astype(BF)
        dz_ref[:, Q_W:Q_W + KV_W] = dxk.astype(BF)
        dz_ref[:, Q_W + KV_W:QKV_W] = jnp.concatenate([dv_ref[h] for h in range(N_KV)], axis=1).astype(BF)
        dwq_ref[...] += dwq
        dwk_ref[...] += dwk

    heads = lambda n: pl.BlockSpec((n, tm, HEAD), lambda i: (0, i, 0))
    return pl.pallas_call(
        body, grid=(s // tm,),
        in_specs=[heads(N_Q), heads(N_KV), heads(N_KV), pl.BlockSpec((tm, QKV_W), lambda i: (i, 0)),
                  pl.BlockSpec((tm, 1), lambda i: (i, 0)), _full((1, Q_W)), _full((1, KV_W)),
                  _full((1, 2 * HEAD)), _full((Q_W, Q_W))],
        out_specs=[pl.BlockSpec((tm, QKV_W), lambda i: (i, 0)), _full((1, HEAD)), _full((1, HEAD))],
        out_shape=[_sds((s, QKV_W), BF), _sds((1, HEAD), F32), _sds((1, HEAD), F32)],
        compiler_params=_cp(("arbitrary",)), name="qk_prep_bwd")(dqh, dkh, dvh, qkv, pos, wq, wk, invf, bd)


def _in_proj_bwd(du, dzq, dzg, w_in, x, g1, dy1, tm):
    s = x.shape[0]

    def body(du_ref, dzq_ref, dzg_ref, w_ref, x_ref, g_ref, dy_ref, gx_ref, dgn_ref):
        @pl.when(pl.program_id(0) == 0)
        def _():
            dgn_ref[...] = jnp.zeros_like(dgn_ref)

        dh = _nt(du_ref[...], w_ref[:, 0:D])
        dh = dh + _nt(dzq_ref[...], w_ref[:, D:D + QKV_W])
        dh = dh + _nt(dzg_ref[...], w_ref[:, D + QKV_W:IN_W])
        dx, dgn = _rms_bwd(dh, x_ref[...], g_ref[...])
        gx_ref[...] = dy_ref[...] + dx
        dgn_ref[...] += jnp.sum(dgn, axis=0, keepdims=True)

    row = lambda w: pl.BlockSpec((tm, w), lambda i: (i, 0))
    return pl.pallas_call(
        body, grid=(s // tm,),
        in_specs=[row(D), row(QKV_W), row(2 * D), _full((D, IN_W)), row(D), _full((1, D)), row(D)],
        out_specs=[row(D), _full((1, D))],
        out_shape=[_sds((s, D), F32), _sds((1, D), F32)],
        compiler_params=_cp(("arbitrary",)), name="in_proj_bwd")(du, dzq, dzg, w_in, x, g1, dy1)


def _adamw(w, g, m, v, tr, name):
    r, c = w.shape

    def body(w_ref, g_ref, m_ref, v_ref, go_ref, d_ref, mo_ref, vo_ref):
        gv = g_ref[...]
        mn = B1 * m_ref[...] + (1.0 - B1) * gv
        vn = B2 * v_ref[...] + (1.0 - B2) * (gv * gv)
        m_hat = mn / (1.0 - B1 ** STEP)
        v_hat = vn / (1.0 - B2 ** STEP)
        go_ref[...] = gv
        d_ref[...] = -LR * (m_hat / (jnp.sqrt(v_hat) + ADAM_EPS) + WD * w_ref[...])
        mo_ref[...] = mn
        vo_ref[...] = vn

    blk = pl.BlockSpec((tr, c), lambda i: (i, 0))
    return pl.pallas_call(
        body, grid=(r // tr,), in_specs=[blk] * 4, out_specs=[blk] * 4, out_shape=[_sds((r, c), F32)] * 4,
        compiler_params=_cp(("parallel",)), name=name)(w, g, m, v)


def _place():
    x, y, c = lax.axis_index("x"), lax.axis_index("y"), lax.axis_index("c")
    chips = [(1 - x, y), (x, 1 - y), (1 - x, 1 - y)]
    return x, y, c, chips


def _rows(ref, lead, h, rh):
    sl = pl.ds(pl.multiple_of(h * rh, 16), rh)
    return ref.at[sl, :] if lead is None else ref.at[lead, sl, :]


def _all_gather_weights(halved, whole):
    nh, nw = len(halved), len(whole)
    na = nh + nw
    arrays = list(halved) + list(whole)
    out_dtypes = [BF] * nh + [a.dtype for a in whole]
    cast_rows = 128

    def body(*refs):
        ins, outs = refs[:na], refs[na:2 * na]
        raw, stage = refs[2 * na:3 * na], refs[3 * na:3 * na + nh]
        ici_send, ici_recv, fwd_send, fwd_recv, in_sem, loc_sem = refs[3 * na + nh:]
        x, y, c, chips = _place()
        me = 2 * x + y
        sibling = (x, y, 1 - c)
        loads = [pltpu.make_async_copy(ins[a], raw[a], in_sem.at[a]) for a in range(na)]
        for cp in loads:
            cp.start()

        def ici(a, j, src_chip, src=None):
            if a < nh:
                rh = arrays[a].shape[0] // 2
                dst = _rows(outs[a], src_chip, c, rh)
                src = dst if src is None else _rows(src, None, c, rh)
            else:
                dst = outs[a].at[src_chip]
                src = dst if src is None else src
            return pltpu.make_async_remote_copy(
                src_ref=src, dst_ref=dst, send_sem=ici_send.at[3 * a + j], recv_sem=ici_recv.at[3 * a + j],
                device_id=(*chips[j], c), device_id_type=MESH)

        def fwd(a, j, half):
            rh = arrays[a].shape[0] // 2
            kj = 2 * chips[j][0] + chips[j][1]
            blk = _rows(outs[a], kj, half, rh)
            return pltpu.make_async_remote_copy(
                src_ref=blk, dst_ref=blk, send_sem=fwd_send.at[3 * a + j], recv_sem=fwd_recv.at[3 * a + j],
                device_id=sibling, device_id_type=MESH)

        local, sends = [], []
        for a in range(na):
            loads[a].wait()
            if a < nh:
                r = arrays[a].shape[0]
                for r0 in range(0, r, cast_rows):
                    r1 = min(r0 + cast_rows, r)
                    stage[a][r0:r1, :] = raw[a][r0:r1, :].astype(BF)
                own = stage[a]
            else:
                own = raw[a]
            cp = pltpu.make_async_copy(own, outs[a].at[me], loc_sem.at[a])
            cp.start()
            local.append(cp)
            for j in range(3):
                cp = ici(a, j, me, src=own)
                cp.start()
                sends.append(cp)
        passed = []
        for a in range(na):
            for j in range(3):
                kj = 2 * chips[j][0] + chips[j][1]
                ici(a, j, kj).wait_recv()
                if a < nh:
                    cp = fwd(a, j, c)
                    cp.start()
                    passed.append(cp)
        for a in range(nh):
            for j in range(3):
                fwd(a, j, 1 - c).wait_recv()
        for cp in sends + passed:
            cp.wait_send()
        for cp in local:
            cp.wait()

    any_spec = pl.BlockSpec(memory_space=pl.ANY)
    return pl.pallas_call(
        body, in_specs=[any_spec] * na, out_specs=[any_spec] * na,
        out_shape=[_sds((N_CHIPS,) + a.shape, dt) for a, dt in zip(arrays, out_dtypes)],
        scratch_shapes=[pltpu.VMEM(a.shape, a.dtype) for a in arrays] + [pltpu.VMEM(a.shape, BF) for a in halved]
        + [pltpu.SemaphoreType.DMA((3 * na,)), pltpu.SemaphoreType.DMA((3 * na,)),
           pltpu.SemaphoreType.DMA((3 * nh,)), pltpu.SemaphoreType.DMA((3 * nh,)),
           pltpu.SemaphoreType.DMA((na,)), pltpu.SemaphoreType.DMA((na,))],
        compiler_params=pltpu.CompilerParams(vmem_limit_bytes=VMEM_LIMIT_MB << 20),
        name="all_gather_weights")(*arrays)


def _sibling_halves(grads):
    na = len(grads)

    def body(*refs):
        ins, outs = refs[:na], refs[na:2 * na]
        send_sem, recv_sem = refs[2 * na:]
        x, y, c, _ = _place()
        copies = []
        for a in range(na):
            rh = grads[a].shape[1] // 2
            src = ins[a].at[:, pl.ds(pl.multiple_of((1 - c) * rh, 8), rh), :]
            copies.append(pltpu.make_async_remote_copy(
                src_ref=src, dst_ref=outs[a], send_sem=send_sem.at[a], recv_sem=recv_sem.at[a],
                device_id=(x, y, 1 - c), device_id_type=MESH))
        for cp in copies:
            cp.start()
        for cp in copies:
            cp.wait()

    any_spec = pl.BlockSpec(memory_space=pl.ANY)
    return pl.pallas_call(
        body, in_specs=[any_spec] * na, out_specs=[any_spec] * na,
        out_shape=[_sds((N_CHIPS, g.shape[1] // 2, g.shape[2]), F32) for g in grads],
        scratch_shapes=[pltpu.SemaphoreType.DMA((na,)), pltpu.SemaphoreType.DMA((na,))],
        name="sibling_halves")(*grads)


def _pair_sum(g, recv, c, tr, name):
    _, r, cols = g.shape
    rh = r // 2
    nr = rh // tr

    def body(c_ref, g_ref, r_ref, o_ref):
        o_ref[...] = (g_ref[...] + r_ref[...]).astype(BF)

    grid_spec = pltpu.PrefetchScalarGridSpec(
        num_scalar_prefetch=1, grid=(N_CHIPS, nr),
        in_specs=[pl.BlockSpec((1, tr, cols), lambda k, i, c_ref: (k, c_ref[0] * nr + i, 0)),
                  pl.BlockSpec((1, tr, cols), lambda k, i, c_ref: (k, i, 0))],
        out_specs=pl.BlockSpec((1, tr, cols), lambda k, i, c_ref: (k, i, 0)))
    return pl.pallas_call(
        body, grid_spec=grid_spec, out_shape=_sds((N_CHIPS, rh, cols), BF),
        compiler_params=_cp(("parallel", "parallel")), name=name)(c, g, recv)


def _chip_exchange(halves, small):
    na = len(halves)
    srows = small.shape[0]

    def body(*refs):
        ins, small_ref = refs[:na], refs[na]
        outs, small_out = refs[na + 1:2 * na + 1], refs[2 * na + 1]
        send_sem, recv_sem, s_send, s_recv = refs[2 * na + 2:]
        x, y, c, chips = _place()
        me = 4 * x + 2 * y + c
        copies = []
        for a in range(na):
            for j in range(3):
                kj = 2 * chips[j][0] + chips[j][1]
                copies.append(pltpu.make_async_remote_copy(
                    src_ref=ins[a].at[kj], dst_ref=outs[a].at[j],
                    send_sem=send_sem.at[3 * a + j], recv_sem=recv_sem.at[3 * a + j],
                    device_id=(*chips[j], c), device_id_type=MESH))
        for r in range(1, N_DEV):
            peer = (x ^ (r >> 2), y ^ ((r >> 1) & 1), c ^ (r & 1))
            copies.append(pltpu.make_async_remote_copy(
                src_ref=small_ref, dst_ref=small_out.at[me],
                send_sem=s_send.at[r - 1], recv_sem=s_recv.at[r - 1], device_id=peer, device_id_type=MESH))
        for cp in copies:
            cp.start()
        small_out[pl.ds(me, 1)] = small_ref[...][None]
        for cp in copies:
            cp.wait()

    any_spec = pl.BlockSpec(memory_space=pl.ANY)
    vmem = pl.BlockSpec(memory_space=pltpu.VMEM)
    return pl.pallas_call(
        body, in_specs=[any_spec] * na + [vmem], out_specs=[any_spec] * na + [vmem],
        out_shape=[_sds((3,) + h.shape[1:], h.dtype) for h in halves] + [_sds((N_DEV, srows, 128), F32)],
        scratch_shapes=[pltpu.SemaphoreType.DMA((3 * na,)), pltpu.SemaphoreType.DMA((3 * na,)),
                        pltpu.SemaphoreType.DMA((N_DEV - 1,)), pltpu.SemaphoreType.DMA((N_DEV - 1,))],
        name="chip_exchange")(*halves, small)


def _chip_sum(g, sib, recv, place, tr, name):
    _, r, cols = g.shape
    rh = r // 2
    nr = rh // tr

    def body(p_ref, g_ref, s_ref, r0_ref, r1_ref, r2_ref, o_ref):
        own = g_ref[0] + s_ref[0]
        o_ref[...] = ((own + r0_ref[0].astype(F32)) + r1_ref[0].astype(F32)) + r2_ref[0].astype(F32)

    rspec = lambda j: pl.BlockSpec((1, tr, cols), lambda i, p: (j, i, 0))
    grid_spec = pltpu.PrefetchScalarGridSpec(
        num_scalar_prefetch=1, grid=(nr,),
        in_specs=[pl.BlockSpec((1, tr, cols), lambda i, p: (p[0], p[1] * nr + i, 0)),
                  pl.BlockSpec((1, tr, cols), lambda i, p: (p[0], i, 0)), rspec(0), rspec(1), rspec(2)],
        out_specs=pl.BlockSpec((tr, cols), lambda i, p: (p[1] * nr + i, 0)))
    return pl.pallas_call(
        body, grid_spec=grid_spec, out_shape=_sds((r, cols), F32),
        compiler_params=_cp(("parallel",)), name=name)(place, g, sib, recv, recv, recv)


def _sibling_exchange(shards):
    na = len(shards)

    def body(*refs):
        ins, outs = refs[:na], refs[na:2 * na]
        send_sem, recv_sem = refs[2 * na:]
        x, y, c, _ = _place()
        for a in range(na):
            rh = shards[a].shape[0] // 2
            pltpu.make_async_remote_copy(
                src_ref=_rows(ins[a], None, c, rh), dst_ref=_rows(outs[a], None, c, rh),
                send_sem=send_sem.at[a], recv_sem=recv_sem.at[a],
                device_id=(x, y, 1 - c), device_id_type=MESH).start()
        for a in range(na):
            rh = shards[a].shape[0] // 2
            pltpu.make_async_remote_copy(
                src_ref=_rows(ins[a], None, c, rh), dst_ref=_rows(outs[a], None, 1 - c, rh),
                send_sem=send_sem.at[a], recv_sem=recv_sem.at[a],
                device_id=(x, y, 1 - c), device_id_type=MESH).wait()

    any_spec = pl.BlockSpec(memory_space=pl.ANY)
    return pl.pallas_call(
        body, in_specs=[any_spec] * na, out_specs=[any_spec] * na,
        out_shape=[_sds(h.shape, F32) for h in shards],
        input_output_aliases={a: a for a in range(na)},
        scratch_shapes=[pltpu.SemaphoreType.DMA((na,)), pltpu.SemaphoreType.DMA((na,))],
        name="sibling_exchange")(*shards)


def _device_sum(stack):
    _, rows, _ = stack.shape

    def body(s_ref, o_ref):
        acc = s_ref[0]
        for d in range(1, N_DEV):
            acc = acc + s_ref[d]
        o_ref[...] = acc

    return pl.pallas_call(body, out_shape=_sds((rows, 128), F32), name="device_sum")(stack)


def _pack(parts, rows):
    flat = jnp.concatenate([p.reshape(-1) for p in parts])
    return jnp.pad(flat, (0, rows * 128 - flat.shape[0])).reshape(rows, 128)


def _unpack(buf, shapes):
    flat = buf.reshape(-1)
    out, off = [], 0
    for shp in shapes:
        n = 1
        for d in shp:
            n *= d
        out.append(flat[off:off + n].reshape(shp))
        off += n
    return out


def _rows_for(n):
    return -(-n // (8 * 128)) * 8


def _forward_backward(xs, pos, tgt, win, wup, wout, wdown, wpool, cw, attn_norm, b_gate, pool_scale, q_norm,
                      k_norm, sinks, ffn_norm, conv_b):
    s = xs.shape[0]
    tm = min(256, s)
    tk = min(512, s)
    inv_freq = ROPE_THETA ** (-jnp.arange(0, ROPE_DIM, 2, dtype=F32) / ROPE_DIM)
    lane = jnp.arange(2 * HEAD) % HEAD
    invf = jnp.where(lane < ROPE_DIM, inv_freq[lane % (ROPE_DIM // 2)], 0.0).reshape(1, 2 * HEAD)
    wq = jnp.tile(q_norm, (1, N_Q))
    wk = jnp.tile(k_norm, (1, N_KV))
    head_of = jnp.arange(Q_W) // HEAD
    bd = (head_of[:, None] == head_of[None, :]).astype(BF)
    sink = sinks[0]

    h1, u, qkv, gates = _attn_in_proj(xs, attn_norm, win, b_gate, tm)
    qh, kh, vh = _qk_prep(qkv, pos, wq, wk, invf, bd, tm)
    battn = _attn_fwd(qh, kh, vh, sink)
    apool, pooled = _pool_fwd(u, wpool, pool_scale, min(512, s))
    mix, y1, h2 = _mix_out_proj(apool, battn, gates, xs, wout, ffn_norm, tm)
    pre_g, pre_v, up_g, up_v, act = _ffn_up(h2, wup, cw, conv_b, tm)
    dy2, dy2b, loss_acc = _ffn_down_loss(act, wdown, y1, tgt, tm)

    d_wdown = _grad_matmul(act, dy2b, 512, tk, "grad_w_down")
    dp_g, dp_v, dcw_g, dcw_v, dcb_g, dcb_v = _ffn_act_bwd(dy2b, wdown, up_g, up_v, pre_g, pre_v, cw, tm)
    d_wup = _grad_matmul(h2, dp_g, UP_SHARD, tk, "grad_w_up_gate", lead=N_CHIPS)
    d_wup = _grad_matmul(h2, dp_v, UP_SHARD, tk, "grad_w_up_value", lead=N_CHIPS, prev=d_wup, lead_off=2)
    dy1, d_ffn_norm = _ffn_up_bwd(dp_g, dp_v, wup, y1, dy2, ffn_norm, tm)
    da, db, dzg, d_bgate, d_wout = _out_proj_bwd(dy1, wout, apool, battn, gates, mix, tm)
    du, d_wpool, d_pscale = _pool_bwd(da, pooled, wpool, pool_scale, min(512, s))
    dqh, dkh, dvh, dsink = _attn_bwd(qh, kh, vh, sink, db)
    dzq, d_qn, d_kn = _qk_prep_bwd(dqh, dkh, dvh, qkv, pos, wq, wk, invf, bd, tm)
    grad_x, d_attn_norm = _in_proj_bwd(du, dzq, dzg, win, xs, attn_norm, dy1, tm)
    d_win = jnp.concatenate([
        _grad_matmul(h1, du, D, tk, "grad_w_in_pool"),
        _grad_matmul(h1, dzq, QKV_W, tk, "grad_w_in_qkv"),
        _grad_matmul(h1, dzg, D, tk, "grad_w_in_gates")], axis=1)
    small_parts = [d_attn_norm, d_bgate, d_pscale, d_qn, d_kn, dsink[:, 0, 0:GQA], d_ffn_norm,
                   jnp.concatenate([dcb_g, dcb_v], axis=1), jnp.concatenate([dcw_g, dcw_v], axis=0)]
    return loss_acc, grad_x, d_win, d_wup, d_wout, d_wdown, d_wpool, small_parts


def kernel(x, positions, attn_norm, w_in, b_gate, w_pool, pool_scale, q_norm, k_norm, sinks, w_out, ffn_norm, w_up, conv_w, conv_b, w_down, loss_target, m_attn_norm, m_w_in, m_b_gate, m_w_pool, m_pool_scale, m_q_norm, m_k_norm, m_sinks, m_w_out, m_ffn_norm, m_w_up, m_conv_w, m_conv_b, m_w_down, v_attn_norm, v_w_in, v_b_gate, v_w_pool, v_pool_scale, v_q_norm, v_k_norm, v_sinks, v_w_out, v_ffn_norm, v_w_up, v_conv_w, v_conv_b, v_w_down):
    s = x.shape[1]
    xs = x[0]
    tgt = loss_target[0]
    pos = positions[0].reshape(s, 1)
    cx, cy, cc = lax.axis_index("x"), lax.axis_index("y"), lax.axis_index("c")
    chip = 2 * cx + cy

    shards = [w_in[0], w_up[0], w_out[0], w_down[0], w_pool[0].reshape(4 * 64, POOL_GROUP)]
    g_in, g_up, g_out, g_down, g_pool, g_cw = _all_gather_weights(shards, [conv_w[0]])
    win = g_in.transpose(1, 0, 2).reshape(D, IN_W)
    wout = g_out.reshape(D, D)
    wdown = g_down.reshape(D_FF, D)
    wpool = g_pool.reshape(N_CHIPS, 4, 64, POOL_GROUP).transpose(1, 0, 2, 3).reshape(4, POOL_GROUP, POOL_GROUP)

    loss_acc, grad_x, d_win, d_wup, d_wout, d_wdown, d_wpool, small_parts = _forward_backward(
        xs, pos, tgt, win, g_up, wout, wdown, wpool, g_cw, attn_norm, b_gate, pool_scale, q_norm, k_norm, sinks,
        ffn_norm, conv_b)
    loss = lax.psum(jnp.sum(loss_acc) * (0.5 / D), ("x", "y", "c"))

    big = [d_win.reshape(D, N_CHIPS, IN_W // N_CHIPS).transpose(1, 0, 2),
           d_wup,
           d_wout.reshape(N_CHIPS, D // N_CHIPS, D),
           d_wdown.reshape(N_CHIPS, D_FF // N_CHIPS, D),
           d_wpool.reshape(4, N_CHIPS, 64, POOL_GROUP).transpose(1, 0, 2, 3).reshape(N_CHIPS, 4 * 64, POOL_GROUP)]
    names = ["w_in", "w_up", "w_out", "w_down", "w_pool"]
    row_tile = [256, 256, 128, 176, 128]
    small_shapes = [(1, D), (1, 2 * D), (1, D), (1, HEAD), (1, HEAD), (1, N_Q), (1, D), (1, 2 * D_FF),
                    (N_CHIPS, 3, UP_SHARD)]
    n_small = sum(p.size for p in small_parts)
    small = _pack(small_parts, _rows_for(n_small))

    c_arr = cc.reshape(1).astype(jnp.int32)
    place = jnp.stack([chip, cc]).astype(jnp.int32)
    from_sibling = _sibling_halves(big)
    halves = [_pair_sum(g, r, c_arr, t, "pair_sum_" + nm) for g, r, t, nm in zip(big, from_sibling, row_tile, names)]
    *from_chips, small_all = _chip_exchange(halves, small)
    mine = [_chip_sum(g, sb, r, place, t, "chip_sum_" + nm)
            for g, sb, r, t, nm in zip(big, from_sibling, from_chips, row_tile, names)]
    g_shards = _sibling_exchange(mine)
    small_sum = _device_sum(small_all)
    (g_attn_norm, g_bgate, g_pscale, g_qn, g_kn, g_sinks, g_ffn_norm, g_convb, g_convw_all) = _unpack(
        small_sum, small_shapes)
    g_convw = lax.dynamic_index_in_dim(g_convw_all, chip, axis=0, keepdims=False)

    def two_d(a):
        return a.reshape(-1, a.shape[-1])

    big_w = [w_in, w_up, w_out, w_down, w_pool]
    big_m = [m_w_in, m_w_up, m_w_out, m_w_down, m_w_pool]
    big_v = [v_w_in, v_w_up, v_w_out, v_w_down, v_w_pool]
    big_out = {}
    for nm, w, g, m, v, t in zip(names, big_w, g_shards, big_m, big_v, [256, 256, 128, 176, 128]):
        res = _adamw(two_d(w), g, two_d(m), two_d(v), t, "adamw_" + nm)
        big_out[nm] = [r.reshape(w.shape) for r in res]

    small_names = ["attn_norm", "b_gate", "pool_scale", "q_norm", "k_norm", "sinks", "ffn_norm", "conv_b", "conv_w"]
    sm_w = [attn_norm, b_gate, pool_scale, q_norm, k_norm, sinks, ffn_norm, conv_b, conv_w]
    sm_m = [m_attn_norm, m_b_gate, m_pool_scale, m_q_norm, m_k_norm, m_sinks, m_ffn_norm, m_conv_b, m_conv_w]
    sm_v = [v_attn_norm, v_b_gate, v_pool_scale, v_q_norm, v_k_norm, v_sinks, v_ffn_norm, v_conv_b, v_conv_w]
    sm_g = [g_attn_norm, g_bgate, g_pscale, g_qn, g_kn, g_sinks, g_ffn_norm, g_convb, g_convw]
    sm_rows = _rows_for(sum(w.size for w in sm_w))
    res = _adamw(_pack(sm_w, sm_rows), _pack(sm_g, sm_rows), _pack(sm_m, sm_rows), _pack(sm_v, sm_rows),
                 sm_rows, "adamw_small")
    sm_out = [_unpack(r, [w.shape for w in sm_w]) for r in res]
    small_out = {nm: [sm_out[k][i] for k in range(4)] for i, nm in enumerate(small_names)}

    order = ["attn_norm", "w_in", "b_gate", "w_pool", "pool_scale", "q_norm", "k_norm", "sinks", "w_out",
             "ffn_norm", "w_up", "conv_w", "conv_b", "w_down"]
    allout = {**big_out, **small_out}
    outs = [loss, grad_x[None]]
    for k in range(4):
        outs += [allout[nm][k] for nm in order]
    return tuple(outs)
```

```python
import functools

import jax
import jax.numpy as jnp
from jax import lax
from jax.experimental import pallas as pl
from jax.experimental.pallas import tpu as pltpu

D = 1024
D_FF = 2816
HEAD = 64
N_Q = 16
N_KV = 2
GQA = 8
BLK = 128
ROPE_DIM = 16
ROPE_THETA = 500000.0
POOL_GROUP = 256
Q_W = 1024
KV_W = 128
QKV_W = Q_W + 2 * KV_W
IN_W = 4352
UP_SHARD = 1408
EPS = 1e-6
N_CHIPS = 4
N_DEV = 8

LR = 0.001
B1 = 0.9
B2 = 0.999
ADAM_EPS = 1e-08
WD = 0.01
STEP = 10

BF = jnp.bfloat16
F32 = jnp.float32
MESH = pl.DeviceIdType.MESH
VMEM_LIMIT_MB = 56


def _cp(sem, vmem_mb=VMEM_LIMIT_MB):
    return pltpu.CompilerParams(dimension_semantics=sem, vmem_limit_bytes=vmem_mb << 20)


def _full(shape):
    nd = len(shape)
    return pl.BlockSpec(shape, lambda *_: (0,) * nd)


def _sds(shape, dtype):
    return jax.ShapeDtypeStruct(shape, dtype)


def _after(body, n_in, deps):
    nd = len(deps)
    if nd == 0:
        return body

    def ordered(*refs):
        return body(*refs[:n_in], *refs[n_in + nd:])

    return ordered


def _any_specs(deps):
    return [pl.BlockSpec(memory_space=pl.ANY)] * len(deps)


def _nt(a, b):
    return lax.dot_general(a, b, (((1,), (1,)), ((), ())), preferred_element_type=F32)


def _tn(a, b):
    return lax.dot_general(a, b, (((0,), (0,)), ((), ())), preferred_element_type=F32)


def _mm(a, b):
    return jnp.dot(a, b, preferred_element_type=F32)


def _head_sum(v, bd):
    hi = v.astype(BF)
    lo = (v - hi.astype(F32)).astype(BF)
    return _mm(hi, bd) + _mm(lo, bd)


def _rope_tables(pos_ref, invf_ref):
    ang = pos_ref[...].astype(F32) * invf_ref[...]
    cos = jnp.cos(ang)
    sin = jnp.sin(ang)
    lane = lax.broadcasted_iota(jnp.int32, (1, 2 * HEAD), 1) % HEAD
    sa = jnp.where(lane < ROPE_DIM // 2, -sin, 0.0)
    sb = jnp.where(lane < ROPE_DIM // 2, 0.0, jnp.where(lane < ROPE_DIM, sin, 0.0))
    return cos, sa, sb


def _tile_lanes(t, reps):
    return t if reps == 1 else jnp.tile(t, (1, reps))


def _rope(v, cos, sa, sb):
    w = v.shape[1]
    reps = w // (2 * HEAD)
    half = ROPE_DIM // 2
    return (v * _tile_lanes(cos, reps) + pltpu.roll(v, w - half, 1) * _tile_lanes(sa, reps)
            + pltpu.roll(v, half, 1) * _tile_lanes(sb, reps))


def _rope_t(dy, cos, sa, sb):
    w = dy.shape[1]
    reps = w // (2 * HEAD)
    half = ROPE_DIM // 2
    return (dy * _tile_lanes(cos, reps) + pltpu.roll(dy * _tile_lanes(sa, reps), half, 1)
            + pltpu.roll(dy * _tile_lanes(sb, reps), w - half, 1))


def _attn_in_proj(x, g1, w_in, b_gate, tm, deps=()):
    s = x.shape[0]

    def body(x_ref, g_ref, w_ref, b_ref, h_ref, u_ref, qkv_ref, gate_ref):
        xv = x_ref[...]
        r = lax.rsqrt(jnp.mean(xv * xv, axis=-1, keepdims=True) + EPS)
        h = (xv * r * g_ref[...]).astype(BF)
        h_ref[...] = h
        u_ref[...] = _mm(h, w_ref[:, 0:D])
        qkv_ref[...] = _mm(h, w_ref[:, D:D + QKV_W])
        gate_ref[...] = jax.nn.sigmoid(_mm(h, w_ref[:, D + QKV_W:IN_W]) + b_ref[...])

    row = lambda w: pl.BlockSpec((tm, w), lambda i: (i, 0))
    return pl.pallas_call(
        _after(body, 4, deps), grid=(s // tm,),
        in_specs=[row(D), _full((1, D)), _full((D, IN_W)), _full((1, 2 * D))] + _any_specs(deps),
        out_specs=[row(D), row(D), row(QKV_W), row(2 * D)],
        out_shape=[_sds((s, D), BF), _sds((s, D), F32), _sds((s, QKV_W), F32), _sds((s, 2 * D), F32)],
        compiler_params=_cp(("parallel",)), name="attn_in_proj")(x, g1, w_in, b_gate, *deps)


def _qk_prep(qkv, pos, wq, wk, invf, bd, tm):
    s = qkv.shape[0]

    def body(qkv_ref, pos_ref, wq_ref, wk_ref, invf_ref, bd_ref, qh_ref, kh_ref, vh_ref):
        cos, sa, sb = _rope_tables(pos_ref, invf_ref)
        q = qkv_ref[:, 0:Q_W]
        k = qkv_ref[:, Q_W:Q_W + KV_W]
        v = qkv_ref[:, Q_W + KV_W:QKV_W]
        rq = lax.rsqrt(_head_sum(q * q, bd_ref[...]) * (1.0 / HEAD) + EPS)
        qr = _rope(q * rq * wq_ref[...], cos, sa, sb) * (HEAD ** -0.5)
        rk = lax.rsqrt(_head_sum(k * k, bd_ref[0:KV_W, 0:KV_W]) * (1.0 / HEAD) + EPS)
        kr = _rope(k * rk * wk_ref[...], cos, sa, sb)
        for h in range(N_Q):
            qh_ref[h] = qr[:, HEAD * h:HEAD * (h + 1)].astype(BF)
        for h in range(N_KV):
            kh_ref[h] = kr[:, HEAD * h:HEAD * (h + 1)].astype(BF)
            vh_ref[h] = v[:, HEAD * h:HEAD * (h + 1)].astype(BF)

    heads = lambda n: pl.BlockSpec((n, tm, HEAD), lambda i: (0, i, 0))
    return pl.pallas_call(
        body, grid=(s // tm,),
        in_specs=[pl.BlockSpec((tm, QKV_W), lambda i: (i, 0)), pl.BlockSpec((tm, 1), lambda i: (i, 0)),
                  _full((1, Q_W)), _full((1, KV_W)), _full((1, 2 * HEAD)), _full((Q_W, Q_W))],
        out_specs=[heads(N_Q), heads(N_KV), heads(N_KV)],
        out_shape=[_sds((N_Q, s, HEAD), BF), _sds((N_KV, s, HEAD), BF), _sds((N_KV, s, HEAD), BF)],
        compiler_params=_cp(("parallel",)), name="qk_prep")(qkv, pos, wq, wk, invf, bd)


def _sink_column(sink_ref, kh):
    row_g = lax.broadcasted_iota(jnp.int32, (GQA * BLK, 1), 0) // BLK
    col = jnp.zeros((GQA * BLK, 1), F32)
    for g in range(GQA):
        col = jnp.where(row_g == g, sink_ref[kh * GQA + g], col)
    return col


def _attn_probs(q, k, n, sink_col):
    sc = _nt(q, k)
    qi = lax.broadcasted_iota(jnp.int32, sc.shape, 0) % BLK + BLK
    ki = lax.broadcasted_iota(jnp.int32, sc.shape, 1)
    lo = jnp.where(n > 0, qi - BLK, BLK - 1)
    ok = (ki <= qi) & (ki > lo)
    sc = jnp.where(ok, sc, -jnp.inf)
    m = jnp.maximum(jnp.max(sc, axis=-1, keepdims=True), sink_col)
    p = jnp.exp(sc - m)
    es = jnp.exp(sink_col - m)
    inv = 1.0 / (jnp.sum(p, axis=-1, keepdims=True) + es)
    return p * inv, es * inv


def _attn_fwd(qh, kh, vh, sinks, deps=()):
    s = qh.shape[1]
    nb = s // BLK

    def body(sink_ref, q_ref, kp_ref, kc_ref, vp_ref, vc_ref, o_ref):
        khd = pl.program_id(0)
        n = pl.program_id(1)
        q = q_ref[...].reshape(GQA * BLK, HEAD)
        k = jnp.concatenate([kp_ref[0], kc_ref[0]], axis=0)
        v = jnp.concatenate([vp_ref[0], vc_ref[0]], axis=0)
        probs, _ = _attn_probs(q, k, n, _sink_column(sink_ref, khd))
        o = _mm(probs.astype(BF), v)
        for j in range(GQA // 2):
            o_ref[:, 2 * HEAD * j:2 * HEAD * (j + 1)] = jnp.concatenate(
                [o[2 * j * BLK:(2 * j + 1) * BLK], o[(2 * j + 1) * BLK:(2 * j + 2) * BLK]], axis=1)

    prev = pl.BlockSpec((1, BLK, HEAD), lambda h, n: (h, jnp.maximum(n - 1, 0), 0))
    cur = pl.BlockSpec((1, BLK, HEAD), lambda h, n: (h, n, 0))
    return pl.pallas_call(
        _after(body, 6, deps), grid=(N_KV, nb),
        in_specs=[pl.BlockSpec(memory_space=pltpu.SMEM),
                  pl.BlockSpec((GQA, BLK, HEAD), lambda h, n: (h, n, 0)), prev, cur, prev, cur] + _any_specs(deps),
        out_specs=pl.BlockSpec((BLK, GQA * HEAD), lambda h, n: (n, h)),
        out_shape=_sds((s, Q_W), F32),
        compiler_params=_cp(("parallel", "parallel")), name="attn_fwd")(sinks, qh, kh, kh, vh, vh, *deps)


def _pool_fwd(u, w_pool, pool_scale, ts):
    s = u.shape[0]
    halo = 16

    def body(u_ref, wp_ref, ps_ref, a_ref, pooled_ref, buf):
        g = pl.program_id(0)
        i = pl.program_id(1)

        @pl.when(i == 0)
        def _():
            buf[0:halo, :] = jnp.zeros((halo, POOL_GROUP), F32)

        buf[halo:halo + ts, :] = u_ref[...]
        t = (i * ts + lax.broadcasted_iota(jnp.int32, (ts, 1), 0)).astype(F32)
        for gi in range(4):
            @pl.when(g == gi)
            def _(gi=gi):
                w = 2 << gi
                cur = buf[pl.ds(halo, ts), :]
                acc = cur
                for j in range(1, w):
                    acc = acc + buf[pl.ds(halo - j, ts), :]
                pooled = (acc / jnp.minimum(t + 1.0, float(w)) - cur).astype(BF)
                pooled_ref[...] = pooled
                a_ref[...] = _mm(pooled, wp_ref[0]) * ps_ref[...]

        buf[0:halo, :] = buf[ts:ts + halo, :]

    col = pl.BlockSpec((ts, POOL_GROUP), lambda g, i: (i, g))
    return pl.pallas_call(
        body, grid=(4, s // ts),
        in_specs=[col, pl.BlockSpec((1, POOL_GROUP, POOL_GROUP), lambda g, i: (g, 0, 0)),
                  pl.BlockSpec((1, POOL_GROUP), lambda g, i: (0, g))],
        out_specs=[col, col],
        out_shape=[_sds((s, D), F32), _sds((s, D), BF)],
        scratch_shapes=[pltpu.VMEM((halo + ts, POOL_GROUP), F32)],
        compiler_params=_cp(("parallel", "arbitrary")), name="pool_fwd")(u, w_pool, pool_scale)


def _mix_out_proj(a, b, gates, x, w_out, g2, tm):
    s = x.shape[0]

    def body(a_ref, b_ref, gate_ref, x_ref, w_ref, g_ref, mix_ref, y_ref, h_ref):
        mix = (gate_ref[:, 0:D] * a_ref[...] + gate_ref[:, D:2 * D] * b_ref[...]).astype(BF)
        mix_ref[...] = mix
        y = x_ref[...] + _mm(mix, w_ref[...])
        y_ref[...] = y
        r = lax.rsqrt(jnp.mean(y * y, axis=-1, keepdims=True) + EPS)
        h_ref[...] = (y * r * g_ref[...]).astype(BF)

    row = lambda w: pl.BlockSpec((tm, w), lambda i: (i, 0))
    return pl.pallas_call(
        body, grid=(s // tm,),
        in_specs=[row(D), row(D), row(2 * D), row(D), _full((D, D)), _full((1, D))],
        out_specs=[row(D), row(D), row(D)],
        out_shape=[_sds((s, D), BF), _sds((s, D), F32), _sds((s, D), BF)],
        compiler_params=_cp(("parallel",)), name="mix_out_proj")(a, b, gates, x, w_out, g2)


def _ffn_up(h2, w_up, conv_w, conv_b, tm):
    s = h2.shape[0]

    def body(h_ref, wg_ref, wv_ref, cwg_ref, cwv_ref, cbg_ref, cbv_ref,
             preg_ref, prev_ref, upg_ref, upv_ref, act_ref, halog, halov):
        i = pl.program_id(1)

        @pl.when(i == 0)
        def _():
            halog[...] = jnp.zeros_like(halog)
            halov[...] = jnp.zeros_like(halov)

        h = h_ref[...]

        def conv_half(w_ref, cw_ref, cb_ref, halo, pre_ref, up_ref):
            pre = _mm(h, w_ref[0])
            pre_ref[...] = pre.astype(BF)
            ext = jnp.concatenate([halo[...], pre], axis=0)
            cw = cw_ref[0]
            up = cb_ref[...] + cw[0:1] * pltpu.roll(ext, 2, 0)[8:8 + tm]
            up = up + cw[1:2] * pltpu.roll(ext, 1, 0)[8:8 + tm]
            up = up + cw[2:3] * pre
            halo[...] = pre[tm - 8:tm]
            up_ref[...] = up.astype(BF)
            return up

        gate = conv_half(wg_ref, cwg_ref, cbg_ref, halog, preg_ref, upg_ref)
        val = conv_half(wv_ref, cwv_ref, cbv_ref, halov, prev_ref, upv_ref)
        act_ref[...] = (gate * jax.nn.sigmoid(gate) * val).astype(BF)

    tile = pl.BlockSpec((tm, UP_SHARD), lambda j, i: (i, j))
    wspec = lambda off: pl.BlockSpec((1, D, UP_SHARD), lambda j, i: (j + off, 0, 0))
    cwspec = lambda off: pl.BlockSpec((1, 3, UP_SHARD), lambda j, i: (j + off, 0, 0))
    cbspec = lambda off: pl.BlockSpec((1, UP_SHARD), lambda j, i: (0, j + off))
    half = _sds((s, D_FF), BF)
    return pl.pallas_call(
        body, grid=(2, s // tm),
        in_specs=[pl.BlockSpec((tm, D), lambda j, i: (i, 0)), wspec(0), wspec(2), cwspec(0), cwspec(2),
                  cbspec(0), cbspec(2)],
        out_specs=[tile] * 5, out_shape=[half] * 5,
        scratch_shapes=[pltpu.VMEM((8, UP_SHARD), F32), pltpu.VMEM((8, UP_SHARD), F32)],
        compiler_params=_cp(("parallel", "arbitrary")), name="ffn_up")(
            h2, w_up, w_up, conv_w, conv_w, conv_b, conv_b)


def _ffn_down_loss(act, w_down, y1, tgt, tm):
    s = y1.shape[0]

    def body(act_ref, w_ref, y_ref, t_ref, dy_ref, dyb_ref, loss_ref):
        @pl.when(pl.program_id(0) == 0)
        def _():
            loss_ref[...] = jnp.zeros_like(loss_ref)

        e = y_ref[...] + _mm(act_ref[...], w_ref[...]) - t_ref[...]
        dy = e * (1.0 / D)
        dy_ref[...] = dy
        dyb_ref[...] = dy.astype(BF)
        e2 = (e * e).reshape(tm // 8, 8, D).sum(axis=0)
        part = e2[:, 0:128]
        for j in range(1, D // 128):
            part = part + e2[:, 128 * j:128 * (j + 1)]
        loss_ref[...] += part

    row = lambda w: pl.BlockSpec((tm, w), lambda i: (i, 0))
    return pl.pallas_call(
        body, grid=(s // tm,),
        in_specs=[row(D_FF), _full((D_FF, D)), row(D), row(D)],
        out_specs=[row(D), row(D), _full((8, 128))],
        out_shape=[_sds((s, D), F32), _sds((s, D), BF), _sds((8, 128), F32)],
        compiler_params=_cp(("arbitrary",)), name="ffn_down_loss")(act, w_down, y1, tgt)


def _grad_matmul(a, b, tn, tk, name, lead=None, prev=None, lead_off=0):
    s, m = a.shape
    n = b.shape[1]
    nj = n // tn

    def body(*refs):
        a_ref, b_ref = refs[0], refs[1]
        o_ref = refs[-1]
        acc = _tn(a_ref[...], b_ref[...])
        acc = acc if lead is None else acc[None]

        @pl.when(pl.program_id(1) == 0)
        def _():
            o_ref[...] = acc

        @pl.when(pl.program_id(1) > 0)
        def _():
            o_ref[...] += acc

    in_specs = [pl.BlockSpec((tk, m), lambda j, k: (k, 0)), pl.BlockSpec((tk, tn), lambda j, k: (k, j))]
    args = [a, b]
    aliases = {}
    if lead is None:
        out_spec = pl.BlockSpec((m, tn), lambda j, k: (0, j))
        out_shape = _sds((m, n), F32)
    else:
        out_spec = pl.BlockSpec((1, m, tn), lambda j, k: (j + lead_off, 0, 0))
        out_shape = _sds((lead, m, tn), F32)
        if prev is not None:
            in_specs.append(pl.BlockSpec(memory_space=pl.ANY))
            args.append(prev)
            aliases = {2: 0}
    return pl.pallas_call(
        body, grid=(nj, s // tk), in_specs=in_specs, out_specs=out_spec, out_shape=out_shape,
        input_output_aliases=aliases,
        compiler_params=_cp(("parallel", "arbitrary")), name=name)(*args)


def _ffn_act_bwd(dyb, w_down, up_g, up_v, pre_g, pre_v, conv_w, tm):
    s = dyb.shape[0]
    nt = s // tm

    def body(dy_ref, wd_ref, upg_ref, upv_ref, preg_ref, prev_ref, cwg_ref, cwv_ref,
             dpg_ref, dpv_ref, dcwg_ref, dcwv_ref, dcbg_ref, dcbv_ref, nxg, nxv):
        i = pl.program_id(1)

        @pl.when(i == 0)
        def _():
            nxg[...] = jnp.zeros_like(nxg)
            nxv[...] = jnp.zeros_like(nxv)
            dcwg_ref[...] = jnp.zeros_like(dcwg_ref)
            dcwv_ref[...] = jnp.zeros_like(dcwv_ref)
            dcbg_ref[...] = jnp.zeros_like(dcbg_ref)
            dcbv_ref[...] = jnp.zeros_like(dcbv_ref)

        dact = _nt(dy_ref[...], wd_ref[...])
        g = upg_ref[...].astype(F32)
        v = upv_ref[...].astype(F32)
        sg = jax.nn.sigmoid(g)
        d_v = dact * (g * sg)
        d_g = dact * v * (sg * (1.0 + g * (1.0 - sg)))

        def conv_bwd(d_up, nx, pre_ref, cw_ref, dp_ref, dcw_ref, dcb_ref):
            ext = jnp.concatenate([d_up, nx[...]], axis=0)
            s1 = pltpu.roll(ext, tm + 8 - 1, 0)[0:tm]
            s2 = pltpu.roll(ext, tm + 8 - 2, 0)[0:tm]
            cw = cw_ref[0]
            dp_ref[...] = (cw[2:3] * d_up + cw[1:2] * s1 + cw[0:1] * s2).astype(BF)
            nx[...] = d_up[0:8]
            pre = pre_ref[...].astype(F32)
            dcw_ref[0, 0:1, :] += jnp.sum(s2 * pre, axis=0, keepdims=True)
            dcw_ref[0, 1:2, :] += jnp.sum(s1 * pre, axis=0, keepdims=True)
            dcw_ref[0, 2:3, :] += jnp.sum(d_up * pre, axis=0, keepdims=True)
            dcb_ref[...] += jnp.sum(d_up, axis=0, keepdims=True)

        conv_bwd(d_g, nxg, preg_ref, cwg_ref, dpg_ref, dcwg_ref, dcbg_ref)
        conv_bwd(d_v, nxv, prev_ref, cwv_ref, dpv_ref, dcwv_ref, dcbv_ref)

    tile = pl.BlockSpec((tm, UP_SHARD), lambda j, i: (nt - 1 - i, j))
    cwspec = lambda off: pl.BlockSpec((1, 3, UP_SHARD), lambda j, i: (j + off, 0, 0))
    acc_cw = pl.BlockSpec((1, 3, UP_SHARD), lambda j, i: (j, 0, 0))
    acc_cb = pl.BlockSpec((1, UP_SHARD), lambda j, i: (0, j))
    buf = pltpu.VMEM((8, UP_SHARD), F32)
    return pl.pallas_call(
        body, grid=(2, nt),
        in_specs=[pl.BlockSpec((tm, D), lambda j, i: (nt - 1 - i, 0)),
                  pl.BlockSpec((UP_SHARD, D), lambda j, i: (j, 0)),
                  tile, tile, tile, tile, cwspec(0), cwspec(2)],
        out_specs=[tile, tile, acc_cw, acc_cw, acc_cb, acc_cb],
        out_shape=[_sds((s, D_FF), BF), _sds((s, D_FF), BF), _sds((2, 3, UP_SHARD), F32),
                   _sds((2, 3, UP_SHARD), F32), _sds((1, D_FF), F32), _sds((1, D_FF), F32)],
        scratch_shapes=[buf, buf],
        compiler_params=_cp(("parallel", "arbitrary")), name="ffn_act_bwd")(
            dyb, w_down, up_g, up_v, pre_g, pre_v, conv_w, conv_w)


def _rms_bwd(dh, y, g):
    r = lax.rsqrt(jnp.mean(y * y, axis=-1, keepdims=True) + EPS)
    n = y * r
    dn = dh * g
    return r * (dn - n * jnp.mean(dn * n, axis=-1, keepdims=True)), dh * n


def _ffn_up_bwd(dp_g, dp_v, w_up, y1, dy2, g2, tm, deps=()):
    s = y1.shape[0]

    def body(dg_ref, dv_ref, w_ref, y_ref, dy2_ref, g_ref, dy1_ref, dgn_ref):
        @pl.when(pl.program_id(0) == 0)
        def _():
            dgn_ref[...] = jnp.zeros_like(dgn_ref)

        dh = _nt(dg_ref[:, 0:UP_SHARD], w_ref[0])
        dh = dh + _nt(dg_ref[:, UP_SHARD:D_FF], w_ref[1])
        dh = dh + _nt(dv_ref[:, 0:UP_SHARD], w_ref[2])
        dh = dh + _nt(dv_ref[:, UP_SHARD:D_FF], w_ref[3])
        dy, dgn = _rms_bwd(dh, y_ref[...], g_ref[...])
        dy1_ref[...] = dy2_ref[...] + dy
        dgn_ref[...] += jnp.sum(dgn, axis=0, keepdims=True)

    row = lambda w: pl.BlockSpec((tm, w), lambda i: (i, 0))
    return pl.pallas_call(
        _after(body, 6, deps), grid=(s // tm,),
        in_specs=[row(D_FF), row(D_FF), _full((4, D, UP_SHARD)), row(D), row(D), _full((1, D))] + _any_specs(deps),
        out_specs=[row(D), _full((1, D))],
        out_shape=[_sds((s, D), F32), _sds((1, D), F32)],
        compiler_params=_cp(("arbitrary",)), name="ffn_up_bwd")(dp_g, dp_v, w_up, y1, dy2, g2, *deps)


def _out_proj_bwd(dy1, w_out, a, b, gates, mix, tm, deps=()):
    s = dy1.shape[0]

    def body(dy_ref, w_ref, a_ref, b_ref, gate_ref, mix_ref, da_ref, db_ref, dzg_ref, dbg_ref, dw_ref):
        @pl.when(pl.program_id(0) == 0)
        def _():
            dbg_ref[...] = jnp.zeros_like(dbg_ref)
            dw_ref[...] = jnp.zeros_like(dw_ref)

        dyb = dy_ref[...].astype(BF)
        dmix = _nt(dyb, w_ref[...])
        gp = gate_ref[:, 0:D]
        ga = gate_ref[:, D:2 * D]
        da_ref[...] = dmix * gp
        db_ref[...] = dmix * ga
        dzp = dmix * a_ref[...] * (gp * (1.0 - gp))
        dza = dmix * b_ref[...] * (ga * (1.0 - ga))
        dzg_ref[:, 0:D] = dzp.astype(BF)
        dzg_ref[:, D:2 * D] = dza.astype(BF)
        dbg_ref[:, 0:D] += jnp.sum(dzp, axis=0, keepdims=True)
        dbg_ref[:, D:2 * D] += jnp.sum(dza, axis=0, keepdims=True)
        dw_ref[...] += _tn(mix_ref[...], dyb)

    row = lambda w: pl.BlockSpec((tm, w), lambda i: (i, 0))
    return pl.pallas_call(
        _after(body, 6, deps), grid=(s // tm,),
        in_specs=[row(D), _full((D, D)), row(D), row(D), row(2 * D), row(D)] + _any_specs(deps),
        out_specs=[row(D), row(D), row(2 * D), _full((1, 2 * D)), _full((D, D))],
        out_shape=[_sds((s, D), F32), _sds((s, D), F32), _sds((s, 2 * D), BF), _sds((1, 2 * D), F32),
                   _sds((D, D), F32)],
        compiler_params=_cp(("arbitrary",)), name="out_proj_bwd")(dy1, w_out, a, b, gates, mix, *deps)


def _pool_bwd(da, pooled, w_pool, pool_scale, ts):
    s = da.shape[0]
    nt = s // ts
    halo = 16

    def body(da_ref, pooled_ref, wp_ref, ps_ref, du_ref, dwp_ref, dps_ref, buf):
        g = pl.program_id(0)
        i = pl.program_id(1)
        ti = nt - 1 - i

        @pl.when(i == 0)
        def _():
            buf[ts:ts + halo, :] = jnp.zeros((halo, POOL_GROUP), F32)
            dwp_ref[...] = jnp.zeros_like(dwp_ref)
            dps_ref[...] = jnp.zeros_like(dps_ref)

        pooled = pooled_ref[...]
        dav = da_ref[...]
        dps_ref[...] += jnp.sum(dav * _mm(pooled, wp_ref[0]), axis=0, keepdims=True)
        dm = (dav * ps_ref[...]).astype(BF)
        dwp_ref[0] += _tn(pooled, dm)
        dpool = _nt(dm, wp_ref[0])
        t = (ti * ts + lax.broadcasted_iota(jnp.int32, (ts, 1), 0)).astype(F32)
        for gi in range(4):
            @pl.when(g == gi)
            def _(gi=gi):
                w = 2 << gi
                buf[0:ts, :] = dpool / jnp.minimum(t + 1.0, float(w))
                acc = buf[pl.ds(0, ts), :]
                for j in range(1, w):
                    acc = acc + buf[pl.ds(j, ts), :]
                du_ref[...] = (acc - dpool).astype(BF)

        buf[ts:ts + halo, :] = buf[0:halo, :]

    col = pl.BlockSpec((ts, POOL_GROUP), lambda g, i: (nt - 1 - i, g))
    return pl.pallas_call(
        body, grid=(4, nt),
        in_specs=[col, col, pl.BlockSpec((1, POOL_GROUP, POOL_GROUP), lambda g, i: (g, 0, 0)),
                  pl.BlockSpec((1, POOL_GROUP), lambda g, i: (0, g))],
        out_specs=[col, pl.BlockSpec((1, POOL_GROUP, POOL_GROUP), lambda g, i: (g, 0, 0)),
                   pl.BlockSpec((1, POOL_GROUP), lambda g, i: (0, g))],
        out_shape=[_sds((s, D), BF), _sds((4, POOL_GROUP, POOL_GROUP), F32), _sds((1, D), F32)],
        scratch_shapes=[pltpu.VMEM((ts + halo, POOL_GROUP), F32)],
        compiler_params=_cp(("parallel", "arbitrary")), name="pool_bwd")(da, pooled, w_pool, pool_scale)


def _attn_bwd(qh, kh, vh, sinks, db, deps=()):
    s = qh.shape[1]
    nb = s // BLK

    def body(sink_ref, q_ref, kp_ref, kc_ref, vp_ref, vc_ref, do_ref,
             dq_ref, dk_ref, dv_ref, dsink_ref, ck, cv):
        khd = pl.program_id(0)
        n = pl.program_id(1)

        @pl.when(n == 0)
        def _():
            ck[...] = jnp.zeros_like(ck)
            cv[...] = jnp.zeros_like(cv)
            dsink_ref[...] = jnp.zeros_like(dsink_ref)

        @pl.when(n < nb)
        def _():
            q = q_ref[...].reshape(GQA * BLK, HEAD)
            k = jnp.concatenate([kp_ref[0], kc_ref[0]], axis=0)
            v = jnp.concatenate([vp_ref[0], vc_ref[0]], axis=0)
            dov = do_ref[...]
            do = jnp.concatenate([dov[:, HEAD * g:HEAD * (g + 1)] for g in range(GQA)], axis=0).astype(BF)
            probs, psink = _attn_probs(q, k, n, _sink_column(sink_ref, khd))
            dp = _nt(do, v)
            delta = jnp.sum(probs * dp, axis=-1, keepdims=True)
            ds = (probs * (dp - delta)).astype(BF)
            dq_ref[...] = _mm(ds, k).reshape(GQA, BLK, HEAD)
            dk = _tn(ds, q)
            dv = _tn(probs.astype(BF), do)
            dk_ref[0] = ck[...] + dk[0:BLK]
            dv_ref[0] = cv[...] + dv[0:BLK]
            ck[...] = dk[BLK:2 * BLK]
            cv[...] = dv[BLK:2 * BLK]
            dsk = psink * delta
            lane = lax.broadcasted_iota(jnp.int32, (1, 128), 1)
            acc = jnp.zeros((1, 128), F32)
            for g in range(GQA):
                acc = acc - jnp.where(lane == g, jnp.sum(dsk[g * BLK:(g + 1) * BLK], axis=0, keepdims=True), 0.0)
            dsink_ref[0] += acc

        @pl.when(n == nb)
        def _():
            dk_ref[0] = ck[...]
            dv_ref[0] = cv[...]

    last = nb - 1
    prev = pl.BlockSpec((1, BLK, HEAD), lambda h, n: (h, jnp.maximum(jnp.minimum(n, last) - 1, 0), 0))
    cur = pl.BlockSpec((1, BLK, HEAD), lambda h, n: (h, jnp.minimum(n, last), 0))
    kv_out = pl.BlockSpec((1, BLK, HEAD), lambda h, n: (h, jnp.maximum(n - 1, 0), 0))
    return pl.pallas_call(
        _after(body, 7, deps), grid=(N_KV, nb + 1),
        in_specs=[pl.BlockSpec(memory_space=pltpu.SMEM),
                  pl.BlockSpec((GQA, BLK, HEAD), lambda h, n: (h, jnp.minimum(n, last), 0)),
                  prev, cur, prev, cur,
                  pl.BlockSpec((BLK, GQA * HEAD), lambda h, n: (jnp.minimum(n, last), h))] + _any_specs(deps),
        out_specs=[pl.BlockSpec((GQA, BLK, HEAD), lambda h, n: (h, jnp.minimum(n, last), 0)), kv_out, kv_out,
                   pl.BlockSpec((1, 1, 128), lambda h, n: (h, 0, 0))],
        out_shape=[_sds((N_Q, s, HEAD), F32), _sds((N_KV, s, HEAD), F32), _sds((N_KV, s, HEAD), F32),
                   _sds((N_KV, 1, 128), F32)],
        scratch_shapes=[pltpu.VMEM((BLK, HEAD), F32), pltpu.VMEM((BLK, HEAD), F32)],
        compiler_params=_cp(("parallel", "arbitrary")), name="attn_bwd")(sinks, qh, kh, kh, vh, vh, db, *deps)


def _qk_prep_bwd(dqh, dkh, dvh, qkv, pos, wq, wk, invf, bd, tm):
    s = qkv.shape[0]

    def fold_heads(row):
        out = row[:, 0:HEAD]
        for h in range(1, row.shape[1] // HEAD):
            out = out + row[:, HEAD * h:HEAD * (h + 1)]
        return out

    def body(dq_ref, dk_ref, dv_ref, qkv_ref, pos_ref, wq_ref, wk_ref, invf_ref, bd_ref,
             dz_ref, dwq_ref, dwk_ref):
        @pl.when(pl.program_id(0) == 0)
        def _():
            dwq_ref[...] = jnp.zeros_like(dwq_ref)
            dwk_ref[...] = jnp.zeros_like(dwk_ref)

        cos, sa, sb = _rope_tables(pos_ref, invf_ref)

        def norm_rope_bwd(dy, xin, w, bdm):
            dn = _rope_t(dy, cos, sa, sb)
            r = lax.rsqrt(_head_sum(xin * xin, bdm) * (1.0 / HEAD) + EPS)
            nh = xin * r
            gw = dn * w
            dx = r * (gw - nh * (_head_sum(gw * nh, bdm) * (1.0 / HEAD)))
            return dx, fold_heads(jnp.sum(dn * nh, axis=0, keepdims=True))

        dq = jnp.concatenate([dq_ref[h] for h in range(N_Q)], axis=1) * (HEAD ** -0.5)
        dk = jnp.concatenate([dk_ref[h] for h in range(N_KV)], axis=1)
        dxq, dwq = norm_rope_bwd(dq, qkv_ref[:, 0:Q_W], wq_ref[...], bd_ref[...])
        dxk, dwk = norm_rope_bwd(dk, qkv_ref[:, Q_W:Q_W + KV_W], wk_ref[...], bd_ref[0:KV_W, 0:KV_W])
        dz_ref[:, 0:Q_W] = dxq.astype(BF)
        dz_ref[:, Q_W:Q_W + KV_W] = dxk.astype(BF)
        dz_ref[:, Q_W + KV_W:QKV_W] = jnp.concatenate([dv_ref[h] for h in range(N_KV)], axis=1).astype(BF)
        dwq_ref[...] += dwq
        dwk_ref[...] += dwk

    heads = lambda n: pl.BlockSpec((n, tm, HEAD), lambda i: (0, i, 0))
    return pl.pallas_call(
        body, grid=(s // tm,),
        in_specs=[heads(N_Q), heads(N_KV), heads(N_KV), pl.BlockSpec((tm, QKV_W), lambda i: (i, 0)),
                  pl.BlockSpec((tm, 1), lambda i: (i, 0)), _full((1, Q_W)), _full((1, KV_W)),
                  _full((1, 2 * HEAD)), _full((Q_W, Q_W))],
        out_specs=[pl.BlockSpec((tm, QKV_W), lambda i: (i, 0)), _full((1, HEAD)), _full((1, HEAD))],
        out_shape=[_sds((s, QKV_W), BF), _sds((1, HEAD), F32), _sds((1, HEAD), F32)],
        compiler_params=_cp(("arbitrary",)), name="qk_prep_bwd")(dqh, dkh, dvh, qkv, pos, wq, wk, invf, bd)


def _in_proj_bwd(du, dzq, dzg, w_in, x, g1, dy1, tm):
    s = x.shape[0]

    def body(du_ref, dzq_ref, dzg_ref, w_ref, x_ref, g_ref, dy_ref, gx_ref, dgn_ref):
        @pl.when(pl.program_id(0) == 0)
        def _():
            dgn_ref[...] = jnp.zeros_like(dgn_ref)

        dh = _nt(du_ref[...], w_ref[:, 0:D])
        dh = dh + _nt(dzq_ref[...], w_ref[:, D:D + QKV_W])
        dh = dh + _nt(dzg_ref[...], w_ref[:, D + QKV_W:IN_W])
        dx, dgn = _rms_bwd(dh, x_ref[...], g_ref[...])
        gx_ref[...] = dy_ref[...] + dx
        dgn_ref[...] += jnp.sum(dgn, axis=0, keepdims=True)

    row = lambda w: pl.BlockSpec((tm, w), lambda i: (i, 0))
    return pl.pallas_call(
        body, grid=(s // tm,),
        in_specs=[row(D), row(QKV_W), row(2 * D), _full((D, IN_W)), row(D), _full((1, D)), row(D)],
        out_specs=[row(D), _full((1, D))],
        out_shape=[_sds((s, D), F32), _sds((1, D), F32)],
        compiler_params=_cp(("arbitrary",)), name="in_proj_bwd")(du, dzq, dzg, w_in, x, g1, dy1)


def _adamw(w, g, m, v, tr, name):
    r, c = w.shape

    def body(w_ref, g_ref, m_ref, v_ref, go_ref, d_ref, mo_ref, vo_ref):
        gv = g_ref[...]
        mn = B1 * m_ref[...] + (1.0 - B1) * gv
        vn = B2 * v_ref[...] + (1.0 - B2) * (gv * gv)
        m_hat = mn / (1.0 - B1 ** STEP)
        v_hat = vn / (1.0 - B2 ** STEP)
        go_ref[...] = gv
        d_ref[...] = -LR * (m_hat / (jnp.sqrt(v_hat) + ADAM_EPS) + WD * w_ref[...])
        mo_ref[...] = mn
        vo_ref[...] = vn

    blk = pl.BlockSpec((tr, c), lambda i: (i, 0))
    return pl.pallas_call(
        body, grid=(r // tr,), in_specs=[blk] * 4, out_specs=[blk] * 4, out_shape=[_sds((r, c), F32)] * 4,
        compiler_params=_cp(("parallel",)), name=name)(w, g, m, v)


def _place():
    x, y, c = lax.axis_index("x"), lax.axis_index("y"), lax.axis_index("c")
    chips = [(1 - x, y), (x, 1 - y), (1 - x, 1 - y)]
    return x, y, c, chips


def _rows(ref, lead, h, rh):
    sl = pl.ds(pl.multiple_of(h * rh, 16), rh)
    return ref.at[sl, :] if lead is None else ref.at[lead, sl, :]


def _all_gather_weights(halved, whole):
    nh, nw = len(halved), len(whole)
    na = nh + nw
    arrays = list(halved) + list(whole)
    out_dtypes = [BF] * nh + [a.dtype for a in whole]
    cast_rows = 128

    def body(*refs):
        ins, outs = refs[:na], refs[na:2 * na]
        raw, stage = refs[2 * na:3 * na], refs[3 * na:3 * na + nh]
        ici_send, ici_recv, fwd_send, fwd_recv, in_sem, loc_sem = refs[3 * na + nh:]
        x, y, c, chips = _place()
        me = 2 * x + y
        sibling = (x, y, 1 - c)
        loads = [pltpu.make_async_copy(ins[a], raw[a], in_sem.at[a]) for a in range(na)]
        for cp in loads:
            cp.start()

        def ici(a, j, src_chip, src=None):
            if a < nh:
                rh = arrays[a].shape[0] // 2
                dst = _rows(outs[a], src_chip, c, rh)
                src = dst if src is None else _rows(src, None, c, rh)
            else:
                dst = outs[a].at[src_chip]
                src = dst if src is None else src
            return pltpu.make_async_remote_copy(
                src_ref=src, dst_ref=dst, send_sem=ici_send.at[3 * a + j], recv_sem=ici_recv.at[3 * a + j],
                device_id=(*chips[j], c), device_id_type=MESH)

        def fwd(a, j, half):
            rh = arrays[a].shape[0] // 2
            kj = 2 * chips[j][0] + chips[j][1]
            blk = _rows(outs[a], kj, half, rh)
            return pltpu.make_async_remote_copy(
                src_ref=blk, dst_ref=blk, send_sem=fwd_send.at[3 * a + j], recv_sem=fwd_recv.at[3 * a + j],
                device_id=sibling, device_id_type=MESH)

        local, sends = [], []
        for a in range(na):
            loads[a].wait()
            if a < nh:
                r = arrays[a].shape[0]
                for r0 in range(0, r, cast_rows):
                    r1 = min(r0 + cast_rows, r)
                    stage[a][r0:r1, :] = raw[a][r0:r1, :].astype(BF)
                own = stage[a]
            else:
                own = raw[a]
            cp = pltpu.make_async_copy(own, outs[a].at[me], loc_sem.at[a])
            cp.start()
            local.append(cp)
            for j in range(3):
                cp = ici(a, j, me, src=own)
                cp.start()
                sends.append(cp)
        passed = []
        for a in range(na):
            for j in range(3):
                kj = 2 * chips[j][0] + chips[j][1]
                ici(a, j, kj).wait_recv()
                if a < nh:
                    cp = fwd(a, j, c)
                    cp.start()
                    passed.append(cp)
        for a in range(nh):
            for j in range(3):
                fwd(a, j, 1 - c).wait_recv()
        for cp in sends + passed:
            cp.wait_send()
        for cp in local:
            cp.wait()

    any_spec = pl.BlockSpec(memory_space=pl.ANY)
    return pl.pallas_call(
        body, in_specs=[any_spec] * na, out_specs=[any_spec] * na,
        out_shape=[_sds((N_CHIPS,) + a.shape, dt) for a, dt in zip(arrays, out_dtypes)],
        scratch_shapes=[pltpu.VMEM(a.shape, a.dtype) for a in arrays] + [pltpu.VMEM(a.shape, BF) for a in halved]
        + [pltpu.SemaphoreType.DMA((3 * na,)), pltpu.SemaphoreType.DMA((3 * na,)),
           pltpu.SemaphoreType.DMA((3 * nh,)), pltpu.SemaphoreType.DMA((3 * nh,)),
           pltpu.SemaphoreType.DMA((na,)), pltpu.SemaphoreType.DMA((na,))],
        compiler_params=pltpu.CompilerParams(vmem_limit_bytes=VMEM_LIMIT_MB << 20),
        name="all_gather_weights")(*arrays)


def _sibling_halves(grads):
    na = len(grads)

    def body(*refs):
        ins, outs = refs[:na], refs[na:2 * na]
        send_sem, recv_sem = refs[2 * na:]
        x, y, c, _ = _place()
        copies = []
        for a in range(na):
            rh = grads[a].shape[1] // 2
            src = ins[a].at[:, pl.ds(pl.multiple_of((1 - c) * rh, 8), rh), :]
            copies.append(pltpu.make_async_remote_copy(
                src_ref=src, dst_ref=outs[a], send_sem=send_sem.at[a], recv_sem=recv_sem.at[a],
                device_id=(x, y, 1 - c), device_id_type=MESH))
        for cp in copies:
            cp.start()
        for cp in copies:
            cp.wait()

    any_spec = pl.BlockSpec(memory_space=pl.ANY)
    return pl.pallas_call(
        body, in_specs=[any_spec] * na, out_specs=[any_spec] * na,
        out_shape=[_sds((N_CHIPS, g.shape[1] // 2, g.shape[2]), F32) for g in grads],
        scratch_shapes=[pltpu.SemaphoreType.DMA((na,)), pltpu.SemaphoreType.DMA((na,))],
        name="sibling_halves")(*grads)


def _pair_sum(g, recv, c, tr, name):
    _, r, cols = g.shape
    rh = r // 2
    nr = rh // tr

    def body(c_ref, g_ref, r_ref, o_ref):
        o_ref[...] = (g_ref[...] + r_ref[...]).astype(BF)

    grid_spec = pltpu.PrefetchScalarGridSpec(
        num_scalar_prefetch=1, grid=(N_CHIPS, nr),
        in_specs=[pl.BlockSpec((1, tr, cols), lambda k, i, c_ref: (k, c_ref[0] * nr + i, 0)),
                  pl.BlockSpec((1, tr, cols), lambda k, i, c_ref: (k, i, 0))],
        out_specs=pl.BlockSpec((1, tr, cols), lambda k, i, c_ref: (k, i, 0)))
    return pl.pallas_call(
        body, grid_spec=grid_spec, out_shape=_sds((N_CHIPS, rh, cols), BF),
        compiler_params=_cp(("parallel", "parallel")), name=name)(c, g, recv)


def _chip_exchange(halves, small):
    na = len(halves)
    srows = small.shape[0]

    def body(*refs):
        ins, small_ref = refs[:na], refs[na]
        outs, small_out = refs[na + 1:2 * na + 1], refs[2 * na + 1]
        send_sem, recv_sem, s_send, s_recv = refs[2 * na + 2:]
        x, y, c, chips = _place()
        me = 4 * x + 2 * y + c
        copies = []
        for a in range(na):
            for j in range(3):
                kj = 2 * chips[j][0] + chips[j][1]
                copies.append(pltpu.make_async_remote_copy(
                    src_ref=ins[a].at[kj], dst_ref=outs[a].at[j],
                    send_sem=send_sem.at[3 * a + j], recv_sem=recv_sem.at[3 * a + j],
                    device_id=(*chips[j], c), device_id_type=MESH))
        for r in range(1, N_DEV):
            peer = (x ^ (r >> 2), y ^ ((r >> 1) & 1), c ^ (r & 1))
            copies.append(pltpu.make_async_remote_copy(
                src_ref=small_ref, dst_ref=small_out.at[me],
                send_sem=s_send.at[r - 1], recv_sem=s_recv.at[r - 1], device_id=peer, device_id_type=MESH))
        for cp in copies:
            cp.start()
        small_out[pl.ds(me, 1)] = small_ref[...][None]
        for cp in copies:
            cp.wait()

    any_spec = pl.BlockSpec(memory_space=pl.ANY)
    vmem = pl.BlockSpec(memory_space=pltpu.VMEM)
    return pl.pallas_call(
        body, in_specs=[any_spec] * na + [vmem], out_specs=[any_spec] * na + [vmem],
        out_shape=[_sds((3,) + h.shape[1:], h.dtype) for h in halves] + [_sds((N_DEV, srows, 128), F32)],
        scratch_shapes=[pltpu.SemaphoreType.DMA((3 * na,)), pltpu.SemaphoreType.DMA((3 * na,)),
                        pltpu.SemaphoreType.DMA((N_DEV - 1,)), pltpu.SemaphoreType.DMA((N_DEV - 1,))],
        name="chip_exchange")(*halves, small)


def _chip_sum(g, sib, recv, place, tr, name):
    _, r, cols = g.shape
    rh = r // 2
    nr = rh // tr

    def body(p_ref, g_ref, s_ref, r0_ref, r1_ref, r2_ref, o_ref):
        own = g_ref[0] + s_ref[0]
        o_ref[...] = ((own + r0_ref[0].astype(F32)) + r1_ref[0].astype(F32)) + r2_ref[0].astype(F32)

    rspec = lambda j: pl.BlockSpec((1, tr, cols), lambda i, p: (j, i, 0))
    grid_spec = pltpu.PrefetchScalarGridSpec(
        num_scalar_prefetch=1, grid=(nr,),
        in_specs=[pl.BlockSpec((1, tr, cols), lambda i, p: (p[0], p[1] * nr + i, 0)),
                  pl.BlockSpec((1, tr, cols), lambda i, p: (p[0], i, 0)), rspec(0), rspec(1), rspec(2)],
        out_specs=pl.BlockSpec((tr, cols), lambda i, p: (p[1] * nr + i, 0)))
    return pl.pallas_call(
        body, grid_spec=grid_spec, out_shape=_sds((r, cols), F32),
        compiler_params=_cp(("parallel",)), name=name)(place, g, sib, recv, recv, recv)


def _sibling_exchange(shards):
    na = len(shards)

    def body(*refs):
        ins, outs = refs[:na], refs[na:2 * na]
        send_sem, recv_sem = refs[2 * na:]
        x, y, c, _ = _place()
        for a in range(na):
            rh = shards[a].shape[0] // 2
            pltpu.make_async_remote_copy(
                src_ref=_rows(ins[a], None, c, rh), dst_ref=_rows(outs[a], None, c, rh),
                send_sem=send_sem.at[a], recv_sem=recv_sem.at[a],
                device_id=(x, y, 1 - c), device_id_type=MESH).start()
        for a in range(na):
            rh = shards[a].shape[0] // 2
            pltpu.make_async_remote_copy(
                src_ref=_rows(ins[a], None, c, rh), dst_ref=_rows(outs[a], None, 1 - c, rh),
                send_sem=send_sem.at[a], recv_sem=recv_sem.at[a],
                device_id=(x, y, 1 - c), device_id_type=MESH).wait()

    any_spec = pl.BlockSpec(memory_space=pl.ANY)
    return pl.pallas_call(
        body, in_specs=[any_spec] * na, out_specs=[any_spec] * na,
        out_shape=[_sds(h.shape, F32) for h in shards],
        input_output_aliases={a: a for a in range(na)},
        scratch_shapes=[pltpu.SemaphoreType.DMA((na,)), pltpu.SemaphoreType.DMA((na,))],
        name="sibling_exchange")(*shards)


def _device_sum(stack):
    _, rows, _ = stack.shape

    def body(s_ref, o_ref):
        acc = s_ref[0]
        for d in range(1, N_DEV):
            acc = acc + s_ref[d]
        o_ref[...] = acc

    return pl.pallas_call(body, out_shape=_sds((rows, 128), F32), name="device_sum")(stack)


_HBM = pl.BlockSpec(memory_space=pltpu.HBM)
_SEM = pl.BlockSpec(memory_space=pltpu.SEMAPHORE)
_EFFECT = pltpu.SideEffectType.DATAFLOW_SIDE_EFFECTING


def _remote(src, dst, ssem, rsem, k, device):
    return pltpu.make_async_remote_copy(src_ref=src, dst_ref=dst, send_sem=ssem.at[k], recv_sem=rsem.at[k],
                                        device_id=device, device_id_type=MESH)


def _split_start(name, bufs, plan, n):
    nb = len(bufs)

    def body(*refs):
        sends, _ = plan(refs[:nb], refs[nb], refs[nb + 1])
        for cp in sends:
            cp.start()
        refs[-1][...] = jnp.zeros_like(refs[-1])

    res = pl.pallas_call(
        body, name=name,
        out_shape=(pltpu.SemaphoreType.DMA((n,)), pltpu.SemaphoreType.DMA((n,)))
        + tuple(pltpu.HBM(b.shape, b.dtype) for b in bufs) + (_sds((8, 128), F32),),
        in_specs=[_HBM] * nb,
        out_specs=(_SEM, _SEM) + (_HBM,) * nb + (pl.BlockSpec(memory_space=pltpu.VMEM),),
        input_output_aliases={i: i + 2 for i in range(nb)},
        compiler_params=pltpu.CompilerParams(has_side_effects=_EFFECT),
    )(*[pltpu.with_memory_space_constraint(b, pltpu.HBM) for b in bufs])
    return res[0], res[1], list(res[2:2 + nb]), res[2 + nb]


def _split_wait(name, send_sem, recv_sem, bufs, plan, after):
    nb = len(bufs)

    def body(*refs):
        sends, arrivals = plan(refs[:nb], refs[nb], refs[nb + 1])
        for cp in sends:
            cp.wait_send()
        for cp in arrivals:
            cp.wait_recv()

    res = pl.pallas_call(
        body, name=name, out_shape=tuple(pltpu.HBM(b.shape, b.dtype) for b in bufs),
        in_specs=[_HBM] * nb + [_SEM, _SEM, pl.BlockSpec(memory_space=pl.ANY)],
        out_specs=(_HBM,) * nb, input_output_aliases={i: i for i in range(nb)},
        compiler_params=pltpu.CompilerParams(has_side_effects=_EFFECT),
    )(*bufs, send_sem, recv_sem, after)
    return list(res)


def _plan_sibling_halves(shapes):
    na = len(shapes)

    def plan(refs, ssem, rsem):
        x, y, c, _ = _place()
        cps = []
        for a in range(na):
            rh = shapes[a][1] // 2
            src = refs[a].at[:, pl.ds(pl.multiple_of((1 - c) * rh, 8), rh), :]
            cps.append(_remote(src, refs[na + a], ssem, rsem, a, (x, y, 1 - c)))
        return cps, cps

    return plan


def _plan_chip_exchange(na):
    def plan(refs, ssem, rsem):
        _, _, c, chips = _place()
        cps = []
        for a in range(na):
            for j in range(3):
                kj = 2 * chips[j][0] + chips[j][1]
                cps.append(_remote(refs[a].at[kj], refs[na + a].at[j], ssem, rsem, 3 * a + j, (*chips[j], c)))
        return cps, cps

    return plan


def _plan_sibling_swap(shapes):
    def plan(refs, ssem, rsem):
        x, y, c, _ = _place()
        sends, arrivals = [], []
        for a, shp in enumerate(shapes):
            rh = shp[0] // 2
            mine, other = _rows(refs[a], None, c, rh), _rows(refs[a], None, 1 - c, rh)
            sends.append(_remote(mine, mine, ssem, rsem, a, (x, y, 1 - c)))
            arrivals.append(_remote(mine, other, ssem, rsem, a, (x, y, 1 - c)))
        return sends, arrivals

    return plan


def _plan_gather_chips(shapes):
    def plan(refs, ssem, rsem):
        x, y, c, chips = _place()
        me = 2 * x + y
        sends, arrivals = [], []
        for a, shp in enumerate(shapes):
            rh = shp[1] // 2
            mine = _rows(refs[a], me, c, rh)
            for j in range(3):
                land = _rows(refs[a], 2 * chips[j][0] + chips[j][1], c, rh)
                sends.append(_remote(mine, mine, ssem, rsem, 3 * a + j, (*chips[j], c)))
                arrivals.append(_remote(land, land, ssem, rsem, 3 * a + j, (*chips[j], c)))
        return sends, arrivals

    return plan


def _plan_gather_sibling(shapes):
    def plan(refs, ssem, rsem):
        x, y, c, chips = _place()
        sends, arrivals = [], []
        for a, shp in enumerate(shapes):
            rh = shp[1] // 2
            for j in range(3):
                kj = 2 * chips[j][0] + chips[j][1]
                got, land = _rows(refs[a], kj, c, rh), _rows(refs[a], kj, 1 - c, rh)
                sends.append(_remote(got, got, ssem, rsem, 3 * a + j, (x, y, 1 - c)))
                arrivals.append(_remote(got, land, ssem, rsem, 3 * a + j, (x, y, 1 - c)))
        return sends, arrivals

    return plan


def _cast_into_slice(w, chip, tr, name):
    r, cols = w.shape

    def body(k_ref, w_ref, o_ref):
        o_ref[0] = w_ref[...].astype(BF)

    grid_spec = pltpu.PrefetchScalarGridSpec(
        num_scalar_prefetch=1, grid=(r // tr,),
        in_specs=[pl.BlockSpec((tr, cols), lambda i, k: (i, 0))],
        out_specs=pl.BlockSpec((1, tr, cols), lambda i, k: (k[0], i, 0)))
    return pl.pallas_call(body, grid_spec=grid_spec, out_shape=_sds((N_CHIPS, r, cols), BF),
                          compiler_params=_cp(("parallel",)), name=name)(chip, w)


class _LateWeights:
    def __init__(self, shards, chip, names, tiles):
        bufs = [_cast_into_slice(w, chip, t, "own_" + nm) for w, nm, t in zip(shards, names, tiles)]
        self.n = 3 * len(bufs)
        self.chips, self.sibling = _plan_gather_chips([b.shape for b in bufs]), _plan_gather_sibling([b.shape for b in bufs])
        self.ssem, self.rsem, self.bufs, token = _split_start("gather_chips_start", bufs, self.chips, self.n)
        self.first = (token,)

    def middle(self, after):
        bufs = _split_wait("gather_chips_wait", self.ssem, self.rsem, self.bufs, self.chips, after)
        self.ssem, self.rsem, self.bufs, token = _split_start("gather_sibling_start", bufs, self.sibling, self.n)
        return (token,)

    def last(self, after):
        return _split_wait("gather_sibling_wait", self.ssem, self.rsem, self.bufs, self.sibling, after)


class _EarlyGrads:
    def __init__(self, place, names, tiles):
        self.place, self.names, self.tiles = place, names, tiles

    def first(self, grads):
        self.na = len(grads)
        self.p1 = _plan_sibling_halves([g.shape for g in grads])
        lands = [lax.empty((N_CHIPS, g.shape[1] // 2, g.shape[2]), F32) for g in grads]
        self.ssem, self.rsem, self.bufs, token = _split_start("early_halves_start", list(grads) + lands, self.p1, self.na)
        return (token,)

    def second(self, after):
        bufs = _split_wait("early_halves_wait", self.ssem, self.rsem, self.bufs, self.p1, after)
        self.grads, self.sib = bufs[:self.na], bufs[self.na:]
        halves = [_pair_sum(g, r, self.place[1:2], t, "pair_sum_" + nm)
                  for g, r, t, nm in zip(self.grads, self.sib, self.tiles, self.names)]
        lands = [lax.empty((3,) + h.shape[1:], h.dtype) for h in halves]
        self.p2 = _plan_chip_exchange(self.na)
        self.ssem, self.rsem, self.bufs, token = _split_start("early_chips_start", halves + lands, self.p2, 3 * self.na)
        return (token,)

    def third(self, after):
        bufs = _split_wait("early_chips_wait", self.ssem, self.rsem, self.bufs, self.p2, after)
        mine = [_chip_sum(g, sb, r, self.place, t, "chip_sum_" + nm)
                for g, sb, r, t, nm in zip(self.grads, self.sib, bufs[self.na:], self.tiles, self.names)]
        self.p3 = _plan_sibling_swap([m.shape for m in mine])
        self.ssem, self.rsem, self.bufs, token = _split_start("early_swap_start", mine, self.p3, self.na)
        return (token,)

    def last(self, after):
        return _split_wait("early_swap_wait", self.ssem, self.rsem, self.bufs, self.p3, after)


def _pack(parts, rows):
    flat = jnp.concatenate([p.reshape(-1) for p in parts])
    return jnp.pad(flat, (0, rows * 128 - flat.shape[0])).reshape(rows, 128)


def _unpack(buf, shapes):
    flat = buf.reshape(-1)
    out, off = [], 0
    for shp in shapes:
        n = 1
        for d in shp:
            n *= d
        out.append(flat[off:off + n].reshape(shp))
        off += n
    return out


def _rows_for(n):
    return -(-n // (8 * 128)) * 8


class _WeightsAtHand:
    def __init__(self, wup, wout, wdown):
        self.first, self.weights = (), [wup, wout, wdown]

    def middle(self, after):
        return ()

    def last(self, after):
        return self.weights


class _GradsKept:
    def first(self, grads):
        self.grads = list(grads)
        return ()

    def second(self, after):
        return ()

    def third(self, after):
        return ()

    def last(self, after):
        return self.grads


def _forward_backward(xs, pos, tgt, win, wpool, cw, attn_norm, b_gate, pool_scale, q_norm, k_norm, sinks,
                      ffn_norm, conv_b, late, early):
    s = xs.shape[0]
    tm = min(256, s)
    tk = min(512, s)
    inv_freq = ROPE_THETA ** (-jnp.arange(0, ROPE_DIM, 2, dtype=F32) / ROPE_DIM)
    lane = jnp.arange(2 * HEAD) % HEAD
    invf = jnp.where(lane < ROPE_DIM, inv_freq[lane % (ROPE_DIM // 2)], 0.0).reshape(1, 2 * HEAD)
    wq = jnp.tile(q_norm, (1, N_Q))
    wk = jnp.tile(k_norm, (1, N_KV))
    head_of = jnp.arange(Q_W) // HEAD
    bd = (head_of[:, None] == head_of[None, :]).astype(BF)
    sink = sinks[0]

    h1, u, qkv, gates = _attn_in_proj(xs, attn_norm, win, b_gate, tm, deps=late.first)
    qh, kh, vh = _qk_prep(qkv, pos, wq, wk, invf, bd, tm)
    battn = _attn_fwd(qh, kh, vh, sink, deps=late.middle(qh))
    apool, pooled = _pool_fwd(u, wpool, pool_scale, min(512, s))
    wup, wout, wdown = late.last(apool)
    wout = wout.reshape(D, D)
    wdown = wdown.reshape(D_FF, D)
    mix, y1, h2 = _mix_out_proj(apool, battn, gates, xs, wout, ffn_norm, tm)
    pre_g, pre_v, up_g, up_v, act = _ffn_up(h2, wup, cw, conv_b, tm)
    dy2, dy2b, loss_acc = _ffn_down_loss(act, wdown, y1, tgt, tm)

    d_wdown = _grad_matmul(act, dy2b, 512, tk, "grad_w_down")
    dp_g, dp_v, dcw_g, dcw_v, dcb_g, dcb_v = _ffn_act_bwd(dy2b, wdown, up_g, up_v, pre_g, pre_v, cw, tm)
    d_wup = _grad_matmul(h2, dp_g, UP_SHARD, tk, "grad_w_up_gate", lead=N_CHIPS)
    d_wup = _grad_matmul(h2, dp_v, UP_SHARD, tk, "grad_w_up_value", lead=N_CHIPS, prev=d_wup, lead_off=2)
    token = early.first([d_wdown.reshape(N_CHIPS, D_FF // N_CHIPS, D), d_wup])
    dy1, d_ffn_norm = _ffn_up_bwd(dp_g, dp_v, wup, y1, dy2, ffn_norm, tm, deps=token)
    token = early.second(dy1)
    da, db, dzg, d_bgate, d_wout = _out_proj_bwd(dy1, wout, apool, battn, gates, mix, tm, deps=token)
    du, d_wpool, d_pscale = _pool_bwd(da, pooled, wpool, pool_scale, min(512, s))
    token = early.third(du)
    dqh, dkh, dvh, dsink = _attn_bwd(qh, kh, vh, sink, db, deps=token)
    dzq, d_qn, d_kn = _qk_prep_bwd(dqh, dkh, dvh, qkv, pos, wq, wk, invf, bd, tm)
    grad_x, d_attn_norm = _in_proj_bwd(du, dzq, dzg, win, xs, attn_norm, dy1, tm)
    d_win = jnp.concatenate([
        _grad_matmul(h1, du, D, tk, "grad_w_in_pool"),
        _grad_matmul(h1, dzq, QKV_W, tk, "grad_w_in_qkv"),
        _grad_matmul(h1, dzg, D, tk, "grad_w_in_gates")], axis=1)
    small_parts = [d_attn_norm, d_bgate, d_pscale, d_qn, d_kn, dsink[:, 0, 0:GQA], d_ffn_norm,
                   jnp.concatenate([dcb_g, dcb_v], axis=1), jnp.concatenate([dcw_g, dcw_v], axis=0)]
    return loss_acc, grad_x, d_win, d_wout, d_wpool, small_parts, early.last(grad_x)


def kernel(x, positions, attn_norm, w_in, b_gate, w_pool, pool_scale, q_norm, k_norm, sinks, w_out, ffn_norm, w_up, conv_w, conv_b, w_down, loss_target, m_attn_norm, m_w_in, m_b_gate, m_w_pool, m_pool_scale, m_q_norm, m_k_norm, m_sinks, m_w_out, m_ffn_norm, m_w_up, m_conv_w, m_conv_b, m_w_down, v_attn_norm, v_w_in, v_b_gate, v_w_pool, v_pool_scale, v_q_norm, v_k_norm, v_sinks, v_w_out, v_ffn_norm, v_w_up, v_conv_w, v_conv_b, v_w_down):
    s = x.shape[1]
    xs = x[0]
    tgt = loss_target[0]
    pos = positions[0].reshape(s, 1)
    cx, cy, cc = lax.axis_index("x"), lax.axis_index("y"), lax.axis_index("c")
    chip = 2 * cx + cy

    c_arr = cc.reshape(1).astype(jnp.int32)
    chip_arr = chip.reshape(1).astype(jnp.int32)
    place = jnp.stack([chip, cc]).astype(jnp.int32)

    g_in, g_pool, g_cw = _all_gather_weights([w_in[0], w_pool[0].reshape(4 * 64, POOL_GROUP)], [conv_w[0]])
    win = g_in.transpose(1, 0, 2).reshape(D, IN_W)
    wpool = g_pool.reshape(N_CHIPS, 4, 64, POOL_GROUP).transpose(1, 0, 2, 3).reshape(4, POOL_GROUP, POOL_GROUP)
    late = _LateWeights([w_up[0], w_out[0], w_down[0]], chip_arr, ["w_up", "w_out", "w_down"], [256, 256, 352])
    early = _EarlyGrads(place, ["w_down", "w_up"], [176, 256])

    loss_acc, grad_x, d_win, d_wout, d_wpool, small_parts, (g_wdown, g_wup) = _forward_backward(
        xs, pos, tgt, win, wpool, g_cw, attn_norm, b_gate, pool_scale, q_norm, k_norm, sinks, ffn_norm, conv_b,
        late, early)
    loss = lax.psum(jnp.sum(loss_acc) * (0.5 / D), ("x", "y", "c"))

    big = [d_win.reshape(D, N_CHIPS, IN_W // N_CHIPS).transpose(1, 0, 2),
           d_wout.reshape(N_CHIPS, D // N_CHIPS, D),
           d_wpool.reshape(4, N_CHIPS, 64, POOL_GROUP).transpose(1, 0, 2, 3).reshape(N_CHIPS, 4 * 64, POOL_GROUP)]
    names = ["w_in", "w_out", "w_pool"]
    row_tile = [256, 128, 128]
    small_shapes = [(1, D), (1, 2 * D), (1, D), (1, HEAD), (1, HEAD), (1, N_Q), (1, D), (1, 2 * D_FF),
                    (N_CHIPS, 3, UP_SHARD)]
    n_small = sum(p.size for p in small_parts)
    small = _pack(small_parts, _rows_for(n_small))

    from_sibling = _sibling_halves(big)
    halves = [_pair_sum(g, r, c_arr, t, "pair_sum_" + nm) for g, r, t, nm in zip(big, from_sibling, row_tile, names)]
    *from_chips, small_all = _chip_exchange(halves, small)
    mine = [_chip_sum(g, sb, r, place, t, "chip_sum_" + nm)
            for g, sb, r, t, nm in zip(big, from_sibling, from_chips, row_tile, names)]
    g_win, g_wout, g_wpool = _sibling_exchange(mine)
    names = ["w_in", "w_up", "w_out", "w_down", "w_pool"]
    g_shards = [g_win, g_wup, g_wout, g_wdown, g_wpool]
    small_sum = _device_sum(small_all)
    (g_attn_norm, g_bgate, g_pscale, g_qn, g_kn, g_sinks, g_ffn_norm, g_convb, g_convw_all) = _unpack(
        small_sum, small_shapes)
    g_convw = lax.dynamic_index_in_dim(g_convw_all, chip, axis=0, keepdims=False)

    def two_d(a):
        return a.reshape(-1, a.shape[-1])

    big_w = [w_in, w_up, w_out, w_down, w_pool]
    big_m = [m_w_in, m_w_up, m_w_out, m_w_down, m_w_pool]
    big_v = [v_w_in, v_w_up, v_w_out, v_w_down, v_w_pool]
    big_out = {}
    for nm, w, g, m, v, t in zip(names, big_w, g_shards, big_m, big_v, [256, 256, 128, 176, 128]):
        res = _adamw(two_d(w), g, two_d(m), two_d(v), t, "adamw_" + nm)
        big_out[nm] = [r.reshape(w.shape) for r in res]

    small_names = ["attn_norm", "b_gate", "pool_scale", "q_norm", "k_norm", "sinks", "ffn_norm", "conv_b", "conv_w"]
    sm_w = [attn_norm, b_gate, pool_scale, q_norm, k_norm, sinks, ffn_norm, conv_b, conv_w]
    sm_m = [m_attn_norm, m_b_gate, m_pool_scale, m_q_norm, m_k_norm, m_sinks, m_ffn_norm, m_conv_b, m_conv_w]
    sm_v = [v_attn_norm, v_b_gate, v_pool_scale, v_q_norm, v_k_norm, v_sinks, v_ffn_norm, v_conv_b, v_conv_w]
    sm_g = [g_attn_norm, g_bgate, g_pscale, g_qn, g_kn, g_sinks, g_ffn_norm, g_convb, g_convw]
    sm_rows = _rows_for(sum(w.size for w in sm_w))
    res = _adamw(_pack(sm_w, sm_rows), _pack(sm_g, sm_rows), _pack(sm_m, sm_rows), _pack(sm_v, sm_rows),
                 sm_rows, "adamw_small")
    sm_out = [_unpack(r, [w.shape for w in sm_w]) for r in res]
    small_out = {nm: [sm_out[k][i] for k in range(4)] for i, nm in enumerate(small_names)}

    order = ["attn_norm", "w_in", "b_gate", "w_pool", "pool_scale", "q_norm", "k_norm", "sinks", "w_out",
             "ffn_norm", "w_up", "conv_w", "conv_b", "w_down"]
    allout = {**big_out, **small_out}
    outs = [loss, grad_x[None]]
    for k in range(4):
        outs += [allout[nm][k] for nm in order]
    return tuple(outs)
```

```python
import functools

import jax
import jax.numpy as jnp
from jax import lax
from jax.experimental import pallas as pl
from jax.experimental.pallas import tpu as pltpu

D = 1024
D_FF = 2816
HEAD = 64
N_Q = 16
N_KV = 2
GQA = 8
BLK = 128
ROPE_DIM = 16
ROPE_THETA = 500000.0
POOL_GROUP = 256
Q_W = 1024
KV_W = 128
QKV_W = Q_W + 2 * KV_W
IN_W = 4352
UP_SHARD = 1408
EPS = 1e-6
N_CHIPS = 4
N_DEV = 8

LR = 0.001
B1 = 0.9
B2 = 0.999
ADAM_EPS = 1e-08
WD = 0.01
STEP = 10

BF = jnp.bfloat16
F32 = jnp.float32
MESH = pl.DeviceIdType.MESH
VMEM_LIMIT_MB = 56


def _cp(sem, vmem_mb=VMEM_LIMIT_MB):
    return pltpu.CompilerParams(dimension_semantics=sem, vmem_limit_bytes=vmem_mb << 20)


def _full(shape):
    nd = len(shape)
    return pl.BlockSpec(shape, lambda *_: (0,) * nd)


def _sds(shape, dtype):
    return jax.ShapeDtypeStruct(shape, dtype)


def _after(body, n_in, deps):
    nd = len(deps)
    if nd == 0:
        return body

    def ordered(*refs):
        return body(*refs[:n_in], *refs[n_in + nd:])

    return ordered


def _any_specs(deps):
    return [pl.BlockSpec(memory_space=pl.ANY)] * len(deps)


def _nt(a, b):
    return lax.dot_general(a, b, (((1,), (1,)), ((), ())), preferred_element_type=F32)


def _tn(a, b):
    return lax.dot_general(a, b, (((0,), (0,)), ((), ())), preferred_element_type=F32)


def _mm(a, b):
    return jnp.dot(a, b, preferred_element_type=F32)


def _head_sum(v, bd):
    hi = v.astype(BF)
    lo = (v - hi.astype(F32)).astype(BF)
    return _mm(hi, bd) + _mm(lo, bd)


def _rope_tables(pos_ref, invf_ref):
    ang = pos_ref[...].astype(F32) * invf_ref[...]
    cos = jnp.cos(ang)
    sin = jnp.sin(ang)
    lane = lax.broadcasted_iota(jnp.int32, (1, 2 * HEAD), 1) % HEAD
    sa = jnp.where(lane < ROPE_DIM // 2, -sin, 0.0)
    sb = jnp.where(lane < ROPE_DIM // 2, 0.0, jnp.where(lane < ROPE_DIM, sin, 0.0))
    return cos, sa, sb


def _tile_lanes(t, reps):
    return t if reps == 1 else jnp.tile(t, (1, reps))


def _rope(v, cos, sa, sb):
    w = v.shape[1]
    reps = w // (2 * HEAD)
    half = ROPE_DIM // 2
    return (v * _tile_lanes(cos, reps) + pltpu.roll(v, w - half, 1) * _tile_lanes(sa, reps)
            + pltpu.roll(v, half, 1) * _tile_lanes(sb, reps))


def _rope_t(dy, cos, sa, sb):
    w = dy.shape[1]
    reps = w // (2 * HEAD)
    half = ROPE_DIM // 2
    return (dy * _tile_lanes(cos, reps) + pltpu.roll(dy * _tile_lanes(sa, reps), half, 1)
            + pltpu.roll(dy * _tile_lanes(sb, reps), w - half, 1))


def _attn_in_proj(x, g1, w_in, b_gate, tm, deps=()):
    s = x.shape[0]

    def body(x_ref, g_ref, w_ref, b_ref, h_ref, u_ref, qkv_ref, gate_ref):
        xv = x_ref[...]
        r = lax.rsqrt(jnp.mean(xv * xv, axis=-1, keepdims=True) + EPS)
        h = (xv * r * g_ref[...]).astype(BF)
        h_ref[...] = h
        u_ref[...] = _mm(h, w_ref[:, 0:D])
        qkv_ref[...] = _mm(h, w_ref[:, D:D + QKV_W])
        gate_ref[...] = jax.nn.sigmoid(_mm(h, w_ref[:, D + QKV_W:IN_W]) + b_ref[...])

    row = lambda w: pl.BlockSpec((tm, w), lambda i: (i, 0))
    return pl.pallas_call(
        _after(body, 4, deps), grid=(s // tm,),
        in_specs=[row(D), _full((1, D)), _full((D, IN_W)), _full((1, 2 * D))] + _any_specs(deps),
        out_specs=[row(D), row(D), row(QKV_W), row(2 * D)],
        out_shape=[_sds((s, D), BF), _sds((s, D), F32), _sds((s, QKV_W), F32), _sds((s, 2 * D), F32)],
        compiler_params=_cp(("parallel",)), name="attn_in_proj")(x, g1, w_in, b_gate, *deps)


def _qk_prep(qkv, pos, wq, wk, invf, bd, tm):
    s = qkv.shape[0]

    def body(qkv_ref, pos_ref, wq_ref, wk_ref, invf_ref, bd_ref, qh_ref, kh_ref, vh_ref):
        cos, sa, sb = _rope_tables(pos_ref, invf_ref)
        q = qkv_ref[:, 0:Q_W]
        k = qkv_ref[:, Q_W:Q_W + KV_W]
        v = qkv_ref[:, Q_W + KV_W:QKV_W]
        rq = lax.rsqrt(_head_sum(q * q, bd_ref[...]) * (1.0 / HEAD) + EPS)
        qr = _rope(q * rq * wq_ref[...], cos, sa, sb) * (HEAD ** -0.5)
        rk = lax.rsqrt(_head_sum(k * k, bd_ref[0:KV_W, 0:KV_W]) * (1.0 / HEAD) + EPS)
        kr = _rope(k * rk * wk_ref[...], cos, sa, sb)
        for h in range(N_Q):
            qh_ref[h] = qr[:, HEAD * h:HEAD * (h + 1)].astype(BF)
        for h in range(N_KV):
            kh_ref[h] = kr[:, HEAD * h:HEAD * (h + 1)].astype(BF)
            vh_ref[h] = v[:, HEAD * h:HEAD * (h + 1)].astype(BF)

    heads = lambda n: pl.BlockSpec((n, tm, HEAD), lambda i: (0, i, 0))
    return pl.pallas_call(
        body, grid=(s // tm,),
        in_specs=[pl.BlockSpec((tm, QKV_W), lambda i: (i, 0)), pl.BlockSpec((tm, 1), lambda i: (i, 0)),
                  _full((1, Q_W)), _full((1, KV_W)), _full((1, 2 * HEAD)), _full((Q_W, Q_W))],
        out_specs=[heads(N_Q), heads(N_KV), heads(N_KV)],
        out_shape=[_sds((N_Q, s, HEAD), BF), _sds((N_KV, s, HEAD), BF), _sds((N_KV, s, HEAD), BF)],
        compiler_params=_cp(("parallel",)), name="qk_prep")(qkv, pos, wq, wk, invf, bd)


def _sink_column(sink_ref, kh):
    row_g = lax.broadcasted_iota(jnp.int32, (GQA * BLK, 1), 0) // BLK
    col = jnp.zeros((GQA * BLK, 1), F32)
    for g in range(GQA):
        col = jnp.where(row_g == g, sink_ref[kh * GQA + g], col)
    return col


def _attn_probs(q, k, n, sink_col):
    sc = _nt(q, k)
    qi = lax.broadcasted_iota(jnp.int32, sc.shape, 0) % BLK + BLK
    ki = lax.broadcasted_iota(jnp.int32, sc.shape, 1)
    lo = jnp.where(n > 0, qi - BLK, BLK - 1)
    ok = (ki <= qi) & (ki > lo)
    sc = jnp.where(ok, sc, -jnp.inf)
    m = jnp.maximum(jnp.max(sc, axis=-1, keepdims=True), sink_col)
    p = jnp.exp(sc - m)
    es = jnp.exp(sink_col - m)
    inv = 1.0 / (jnp.sum(p, axis=-1, keepdims=True) + es)
    return p * inv, es * inv


def _attn_fwd(qh, kh, vh, sinks, deps=()):
    s = qh.shape[1]
    nb = s // BLK

    def body(sink_ref, q_ref, kp_ref, kc_ref, vp_ref, vc_ref, o_ref):
        khd = pl.program_id(0)
        n = pl.program_id(1)
        q = q_ref[...].reshape(GQA * BLK, HEAD)
        k = jnp.concatenate([kp_ref[0], kc_ref[0]], axis=0)
        v = jnp.concatenate([vp_ref[0], vc_ref[0]], axis=0)
        probs, _ = _attn_probs(q, k, n, _sink_column(sink_ref, khd))
        o = _mm(probs.astype(BF), v)
        for j in range(GQA // 2):
            o_ref[:, 2 * HEAD * j:2 * HEAD * (j + 1)] = jnp.concatenate(
                [o[2 * j * BLK:(2 * j + 1) * BLK], o[(2 * j + 1) * BLK:(2 * j + 2) * BLK]], axis=1)

    prev = pl.BlockSpec((1, BLK, HEAD), lambda h, n: (h, jnp.maximum(n - 1, 0), 0))
    cur = pl.BlockSpec((1, BLK, HEAD), lambda h, n: (h, n, 0))
    return pl.pallas_call(
        _after(body, 6, deps), grid=(N_KV, nb),
        in_specs=[pl.BlockSpec(memory_space=pltpu.SMEM),
                  pl.BlockSpec((GQA, BLK, HEAD), lambda h, n: (h, n, 0)), prev, cur, prev, cur] + _any_specs(deps),
        out_specs=pl.BlockSpec((BLK, GQA * HEAD), lambda h, n: (n, h)),
        out_shape=_sds((s, Q_W), F32),
        compiler_params=_cp(("parallel", "parallel")), name="attn_fwd")(sinks, qh, kh, kh, vh, vh, *deps)


def _pool_fwd(u, w_pool, pool_scale, ts):
    s = u.shape[0]
    halo = 16

    def body(u_ref, wp_ref, ps_ref, a_ref, pooled_ref, buf):
        g = pl.program_id(0)
        i = pl.program_id(1)

        @pl.when(i == 0)
        def _():
            buf[0:halo, :] = jnp.zeros((halo, POOL_GROUP), F32)

        buf[halo:halo + ts, :] = u_ref[...]
        t = (i * ts + lax.broadcasted_iota(jnp.int32, (ts, 1), 0)).astype(F32)
        for gi in range(4):
            @pl.when(g == gi)
            def _(gi=gi):
                w = 2 << gi
                cur = buf[pl.ds(halo, ts), :]
                acc = cur
                for j in range(1, w):
                    acc = acc + buf[pl.ds(halo - j, ts), :]
                pooled = (acc / jnp.minimum(t + 1.0, float(w)) - cur).astype(BF)
                pooled_ref[...] = pooled
                a_ref[...] = _mm(pooled, wp_ref[0]) * ps_ref[...]

        buf[0:halo, :] = buf[ts:ts + halo, :]

    col = pl.BlockSpec((ts, POOL_GROUP), lambda g, i: (i, g))
    return pl.pallas_call(
        body, grid=(4, s // ts),
        in_specs=[col, pl.BlockSpec((1, POOL_GROUP, POOL_GROUP), lambda g, i: (g, 0, 0)),
                  pl.BlockSpec((1, POOL_GROUP), lambda g, i: (0, g))],
        out_specs=[col, col],
        out_shape=[_sds((s, D), F32), _sds((s, D), BF)],
        scratch_shapes=[pltpu.VMEM((halo + ts, POOL_GROUP), F32)],
        compiler_params=_cp(("parallel", "arbitrary")), name="pool_fwd")(u, w_pool, pool_scale)


def _mix_out_proj(a, b, gates, x, w_out, g2, tm):
    s = x.shape[0]

    def body(a_ref, b_ref, gate_ref, x_ref, w_ref, g_ref, mix_ref, y_ref, h_ref):
        mix = (gate_ref[:, 0:D] * a_ref[...] + gate_ref[:, D:2 * D] * b_ref[...]).astype(BF)
        mix_ref[...] = mix
        y = x_ref[...] + _mm(mix, w_ref[...])
        y_ref[...] = y
        r = lax.rsqrt(jnp.mean(y * y, axis=-1, keepdims=True) + EPS)
        h_ref[...] = (y * r * g_ref[...]).astype(BF)

    row = lambda w: pl.BlockSpec((tm, w), lambda i: (i, 0))
    return pl.pallas_call(
        body, grid=(s // tm,),
        in_specs=[row(D), row(D), row(2 * D), row(D), _full((D, D)), _full((1, D))],
        out_specs=[row(D), row(D), row(D)],
        out_shape=[_sds((s, D), BF), _sds((s, D), F32), _sds((s, D), BF)],
        compiler_params=_cp(("parallel",)), name="mix_out_proj")(a, b, gates, x, w_out, g2)


def _ffn_up(h2, w_up, conv_w, conv_b, tm):
    s = h2.shape[0]

    def body(h_ref, wg_ref, wv_ref, cwg_ref, cwv_ref, cbg_ref, cbv_ref,
             preg_ref, prev_ref, upg_ref, upv_ref, act_ref, halog, halov):
        i = pl.program_id(1)

        @pl.when(i == 0)
        def _():
            halog[...] = jnp.zeros_like(halog)
            halov[...] = jnp.zeros_like(halov)

        h = h_ref[...]

        def conv_half(w_ref, cw_ref, cb_ref, halo, pre_ref, up_ref):
            pre = _mm(h, w_ref[0])
            pre_ref[...] = pre.astype(BF)
            ext = jnp.concatenate([halo[...], pre], axis=0)
            cw = cw_ref[0]
            up = cb_ref[...] + cw[0:1] * pltpu.roll(ext, 2, 0)[8:8 + tm]
            up = up + cw[1:2] * pltpu.roll(ext, 1, 0)[8:8 + tm]
            up = up + cw[2:3] * pre
            halo[...] = pre[tm - 8:tm]
            up_ref[...] = up.astype(BF)
            return up

        gate = conv_half(wg_ref, cwg_ref, cbg_ref, halog, preg_ref, upg_ref)
        val = conv_half(wv_ref, cwv_ref, cbv_ref, halov, prev_ref, upv_ref)
        act_ref[...] = (gate * jax.nn.sigmoid(gate) * val).astype(BF)

    tile = pl.BlockSpec((tm, UP_SHARD), lambda j, i: (i, j))
    wspec = lambda off: pl.BlockSpec((1, D, UP_SHARD), lambda j, i: (j + off, 0, 0))
    cwspec = lambda off: pl.BlockSpec((1, 3, UP_SHARD), lambda j, i: (j + off, 0, 0))
    cbspec = lambda off: pl.BlockSpec((1, UP_SHARD), lambda j, i: (0, j + off))
    half = _sds((s, D_FF), BF)
    return pl.pallas_call(
        body, grid=(2, s // tm),
        in_specs=[pl.BlockSpec((tm, D), lambda j, i: (i, 0)), wspec(0), wspec(2), cwspec(0), cwspec(2),
                  cbspec(0), cbspec(2)],
        out_specs=[tile] * 5, out_shape=[half] * 5,
        scratch_shapes=[pltpu.VMEM((8, UP_SHARD), F32), pltpu.VMEM((8, UP_SHARD), F32)],
        compiler_params=_cp(("parallel", "arbitrary")), name="ffn_up")(
            h2, w_up, w_up, conv_w, conv_w, conv_b, conv_b)


def _ffn_down_loss(act, w_down, y1, tgt, tm):
    s = y1.shape[0]

    def body(act_ref, w_ref, y_ref, t_ref, dy_ref, dyb_ref, loss_ref):
        @pl.when(pl.program_id(0) == 0)
        def _():
            loss_ref[...] = jnp.zeros_like(loss_ref)

        e = y_ref[...] + _mm(act_ref[...], w_ref[...]) - t_ref[...]
        dy = e * (1.0 / D)
        dy_ref[...] = dy
        dyb_ref[...] = dy.astype(BF)
        e2 = (e * e).reshape(tm // 8, 8, D).sum(axis=0)
        part = e2[:, 0:128]
        for j in range(1, D // 128):
            part = part + e2[:, 128 * j:128 * (j + 1)]
        loss_ref[...] += part

    row = lambda w: pl.BlockSpec((tm, w), lambda i: (i, 0))
    return pl.pallas_call(
        body, grid=(s // tm,),
        in_specs=[row(D_FF), _full((D_FF, D)), row(D), row(D)],
        out_specs=[row(D), row(D), _full((8, 128))],
        out_shape=[_sds((s, D), F32), _sds((s, D), BF), _sds((8, 128), F32)],
        compiler_params=_cp(("arbitrary",)), name="ffn_down_loss")(act, w_down, y1, tgt)


def _grad_matmul(a, b, tn, tk, name, lead=None, prev=None, lead_off=0):
    s, m = a.shape
    n = b.shape[1]
    nj = n // tn

    def body(*refs):
        a_ref, b_ref = refs[0], refs[1]
        o_ref = refs[-1]
        acc = _tn(a_ref[...], b_ref[...])
        acc = acc if lead is None else acc[None]

        @pl.when(pl.program_id(1) == 0)
        def _():
            o_ref[...] = acc

        @pl.when(pl.program_id(1) > 0)
        def _():
            o_ref[...] += acc

    in_specs = [pl.BlockSpec((tk, m), lambda j, k: (k, 0)), pl.BlockSpec((tk, tn), lambda j, k: (k, j))]
    args = [a, b]
    aliases = {}
    if lead is None:
        out_spec = pl.BlockSpec((m, tn), lambda j, k: (0, j))
        out_shape = _sds((m, n), F32)
    else:
        out_spec = pl.BlockSpec((1, m, tn), lambda j, k: (j + lead_off, 0, 0))
        out_shape = _sds((lead, m, tn), F32)
        if prev is not None:
            in_specs.append(pl.BlockSpec(memory_space=pl.ANY))
            args.append(prev)
            aliases = {2: 0}
    return pl.pallas_call(
        body, grid=(nj, s // tk), in_specs=in_specs, out_specs=out_spec, out_shape=out_shape,
        input_output_aliases=aliases,
        compiler_params=_cp(("parallel", "arbitrary")), name=name)(*args)


def _ffn_act_bwd(dyb, w_down, up_g, up_v, pre_g, pre_v, conv_w, tm):
    s = dyb.shape[0]
    nt = s // tm

    def body(dy_ref, wd_ref, upg_ref, upv_ref, preg_ref, prev_ref, cwg_ref, cwv_ref,
             dpg_ref, dpv_ref, dcwg_ref, dcwv_ref, dcbg_ref, dcbv_ref, nxg, nxv):
        i = pl.program_id(1)

        @pl.when(i == 0)
        def _():
            nxg[...] = jnp.zeros_like(nxg)
            nxv[...] = jnp.zeros_like(nxv)
            dcwg_ref[...] = jnp.zeros_like(dcwg_ref)
            dcwv_ref[...] = jnp.zeros_like(dcwv_ref)
            dcbg_ref[...] = jnp.zeros_like(dcbg_ref)
            dcbv_ref[...] = jnp.zeros_like(dcbv_ref)

        dact = _nt(dy_ref[...], wd_ref[...])
        g = upg_ref[...].astype(F32)
        v = upv_ref[...].astype(F32)
        sg = jax.nn.sigmoid(g)
        d_v = dact * (g * sg)
        d_g = dact * v * (sg * (1.0 + g * (1.0 - sg)))

        def conv_bwd(d_up, nx, pre_ref, cw_ref, dp_ref, dcw_ref, dcb_ref):
            ext = jnp.concatenate([d_up, nx[...]], axis=0)
            s1 = pltpu.roll(ext, tm + 8 - 1, 0)[0:tm]
            s2 = pltpu.roll(ext, tm + 8 - 2, 0)[0:tm]
            cw = cw_ref[0]
            dp_ref[...] = (cw[2:3] * d_up + cw[1:2] * s1 + cw[0:1] * s2).astype(BF)
            nx[...] = d_up[0:8]
            pre = pre_ref[...].astype(F32)
            dcw_ref[0, 0:1, :] += jnp.sum(s2 * pre, axis=0, keepdims=True)
            dcw_ref[0, 1:2, :] += jnp.sum(s1 * pre, axis=0, keepdims=True)
            dcw_ref[0, 2:3, :] += jnp.sum(d_up * pre, axis=0, keepdims=True)
            dcb_ref[...] += jnp.sum(d_up, axis=0, keepdims=True)

        conv_bwd(d_g, nxg, preg_ref, cwg_ref, dpg_ref, dcwg_ref, dcbg_ref)
        conv_bwd(d_v, nxv, prev_ref, cwv_ref, dpv_ref, dcwv_ref, dcbv_ref)

    tile = pl.BlockSpec((tm, UP_SHARD), lambda j, i: (nt - 1 - i, j))
    cwspec = lambda off: pl.BlockSpec((1, 3, UP_SHARD), lambda j, i: (j + off, 0, 0))
    acc_cw = pl.BlockSpec((1, 3, UP_SHARD), lambda j, i: (j, 0, 0))
    acc_cb = pl.BlockSpec((1, UP_SHARD), lambda j, i: (0, j))
    buf = pltpu.VMEM((8, UP_SHARD), F32)
    return pl.pallas_call(
        body, grid=(2, nt),
        in_specs=[pl.BlockSpec((tm, D), lambda j, i: (nt - 1 - i, 0)),
                  pl.BlockSpec((UP_SHARD, D), lambda j, i: (j, 0)),
                  tile, tile, tile, tile, cwspec(0), cwspec(2)],
        out_specs=[tile, tile, acc_cw, acc_cw, acc_cb, acc_cb],
        out_shape=[_sds((s, D_FF), BF), _sds((s, D_FF), BF), _sds((2, 3, UP_SHARD), F32),
                   _sds((2, 3, UP_SHARD), F32), _sds((1, D_FF), F32), _sds((1, D_FF), F32)],
        scratch_shapes=[buf, buf],
        compiler_params=_cp(("parallel", "arbitrary")), name="ffn_act_bwd")(
            dyb, w_down, up_g, up_v, pre_g, pre_v, conv_w, conv_w)


def _rms_bwd(dh, y, g):
    r = lax.rsqrt(jnp.mean(y * y, axis=-1, keepdims=True) + EPS)
    n = y * r
    dn = dh * g
    return r * (dn - n * jnp.mean(dn * n, axis=-1, keepdims=True)), dh * n


def _ffn_up_bwd(dp_g, dp_v, w_up, y1, dy2, g2, tm, deps=()):
    s = y1.shape[0]

    def body(dg_ref, dv_ref, w_ref, y_ref, dy2_ref, g_ref, dy1_ref, dgn_ref):
        @pl.when(pl.program_id(0) == 0)
        def _():
            dgn_ref[...] = jnp.zeros_like(dgn_ref)

        dh = _nt(dg_ref[:, 0:UP_SHARD], w_ref[0])
        dh = dh + _nt(dg_ref[:, UP_SHARD:D_FF], w_ref[1])
        dh = dh + _nt(dv_ref[:, 0:UP_SHARD], w_ref[2])
        dh = dh + _nt(dv_ref[:, UP_SHARD:D_FF], w_ref[3])
        dy, dgn = _rms_bwd(dh, y_ref[...], g_ref[...])
        dy1_ref[...] = dy2_ref[...] + dy
        dgn_ref[...] += jnp.sum(dgn, axis=0, keepdims=True)

    row = lambda w: pl.BlockSpec((tm, w), lambda i: (i, 0))
    return pl.pallas_call(
        _after(body, 6, deps), grid=(s // tm,),
        in_specs=[row(D_FF), row(D_FF), _full((4, D, UP_SHARD)), row(D), row(D), _full((1, D))] + _any_specs(deps),
        out_specs=[row(D), _full((1, D))],
        out_shape=[_sds((s, D), F32), _sds((1, D), F32)],
        compiler_params=_cp(("arbitrary",)), name="ffn_up_bwd")(dp_g, dp_v, w_up, y1, dy2, g2, *deps)


def _out_proj_bwd(dy1, w_out, a, b, gates, mix, tm, deps=()):
    s = dy1.shape[0]

    def body(dy_ref, w_ref, a_ref, b_ref, gate_ref, mix_ref, da_ref, db_ref, dzg_ref, dbg_ref, dw_ref):
        @pl.when(pl.program_id(0) == 0)
        def _():
            dbg_ref[...] = jnp.zeros_like(dbg_ref)
            dw_ref[...] = jnp.zeros_like(dw_ref)

        dyb = dy_ref[...].astype(BF)
        dmix = _nt(dyb, w_ref[...])
        gp = gate_ref[:, 0:D]
        ga = gate_ref[:, D:2 * D]
        da_ref[...] = dmix * gp
        db_ref[...] = dmix * ga
        dzp = dmix * a_ref[...] * (gp * (1.0 - gp))
        dza = dmix * b_ref[...] * (ga * (1.0 - ga))
        dzg_ref[:, 0:D] = dzp.astype(BF)
        dzg_ref[:, D:2 * D] = dza.astype(BF)
        dbg_ref[:, 0:D] += jnp.sum(dzp, axis=0, keepdims=True)
        dbg_ref[:, D:2 * D] += jnp.sum(dza, axis=0, keepdims=True)
        dw_ref[...] += _tn(mix_ref[...], dyb)

    row = lambda w: pl.BlockSpec((tm, w), lambda i: (i, 0))
    return pl.pallas_call(
        _after(body, 6, deps), grid=(s // tm,),
        in_specs=[row(D), _full((D, D)), row(D), row(D), row(2 * D), row(D)] + _any_specs(deps),
        out_specs=[row(D), row(D), row(2 * D), _full((1, 2 * D)), _full((D, D))],
        out_shape=[_sds((s, D), F32), _sds((s, D), F32), _sds((s, 2 * D), BF), _sds((1, 2 * D), F32),
                   _sds((D, D), F32)],
        compiler_params=_cp(("arbitrary",)), name="out_proj_bwd")(dy1, w_out, a, b, gates, mix, *deps)


def _pool_bwd(da, pooled, w_pool, pool_scale, ts):
    s = da.shape[0]
    nt = s // ts
    halo = 16

    def body(da_ref, pooled_ref, wp_ref, ps_ref, du_ref, dwp_ref, dps_ref, buf):
        g = pl.program_id(0)
        i = pl.program_id(1)
        ti = nt - 1 - i

        @pl.when(i == 0)
        def _():
            buf[ts:ts + halo, :] = jnp.zeros((halo, POOL_GROUP), F32)
            dwp_ref[...] = jnp.zeros_like(dwp_ref)
            dps_ref[...] = jnp.zeros_like(dps_ref)

        pooled = pooled_ref[...]
        dav = da_ref[...]
        dps_ref[...] += jnp.sum(dav * _mm(pooled, wp_ref[0]), axis=0, keepdims=True)
        dm = (dav * ps_ref[...]).astype(BF)
        dwp_ref[0] += _tn(pooled, dm)
        dpool = _nt(dm, wp_ref[0])
        t = (ti * ts + lax.broadcasted_iota(jnp.int32, (ts, 1), 0)).astype(F32)
        for gi in range(4):
            @pl.when(g == gi)
            def _(gi=gi):
                w = 2 << gi
                buf[0:ts, :] = dpool / jnp.minimum(t + 1.0, float(w))
                acc = buf[pl.ds(0, ts), :]
                for j in range(1, w):
                    acc = acc + buf[pl.ds(j, ts), :]
                du_ref[...] = (acc - dpool).astype(BF)

        buf[ts:ts + halo, :] = buf[0:halo, :]

    col = pl.BlockSpec((ts, POOL_GROUP), lambda g, i: (nt - 1 - i, g))
    return pl.pallas_call(
        body, grid=(4, nt),
        in_specs=[col, col, pl.BlockSpec((1, POOL_GROUP, POOL_GROUP), lambda g, i: (g, 0, 0)),
                  pl.BlockSpec((1, POOL_GROUP), lambda g, i: (0, g))],
        out_specs=[col, pl.BlockSpec((1, POOL_GROUP, POOL_GROUP), lambda g, i: (g, 0, 0)),
                   pl.BlockSpec((1, POOL_GROUP), lambda g, i: (0, g))],
        out_shape=[_sds((s, D), BF), _sds((4, POOL_GROUP, POOL_GROUP), F32), _sds((1, D), F32)],
        scratch_shapes=[pltpu.VMEM((ts + halo, POOL_GROUP), F32)],
        compiler_params=_cp(("parallel", "arbitrary")), name="pool_bwd")(da, pooled, w_pool, pool_scale)


def _attn_bwd(qh, kh, vh, sinks, db, deps=()):
    s = qh.shape[1]
    nb = s // BLK

    def body(sink_ref, q_ref, kp_ref, kc_ref, vp_ref, vc_ref, do_ref,
             dq_ref, dk_ref, dv_ref, dsink_ref, ck, cv):
        khd = pl.program_id(0)
        n = pl.program_id(1)

        @pl.when(n == 0)
        def _():
            ck[...] = jnp.zeros_like(ck)
            cv[...] = jnp.zeros_like(cv)
            dsink_ref[...] = jnp.zeros_like(dsink_ref)

        @pl.when(n < nb)
        def _():
            q = q_ref[...].reshape(GQA * BLK, HEAD)
            k = jnp.concatenate([kp_ref[0], kc_ref[0]], axis=0)
            v = jnp.concatenate([vp_ref[0], vc_ref[0]], axis=0)
            dov = do_ref[...]
            do = jnp.concatenate([dov[:, HEAD * g:HEAD * (g + 1)] for g in range(GQA)], axis=0).astype(BF)
            probs, psink = _attn_probs(q, k, n, _sink_column(sink_ref, khd))
            dp = _nt(do, v)
            delta = jnp.sum(probs * dp, axis=-1, keepdims=True)
            ds = (probs * (dp - delta)).astype(BF)
            dq_ref[...] = _mm(ds, k).reshape(GQA, BLK, HEAD)
            dk = _tn(ds, q)
            dv = _tn(probs.astype(BF), do)
            dk_ref[0] = ck[...] + dk[0:BLK]
            dv_ref[0] = cv[...] + dv[0:BLK]
            ck[...] = dk[BLK:2 * BLK]
            cv[...] = dv[BLK:2 * BLK]
            dsk = psink * delta
            lane = lax.broadcasted_iota(jnp.int32, (1, 128), 1)
            acc = jnp.zeros((1, 128), F32)
            for g in range(GQA):
                acc = acc - jnp.where(lane == g, jnp.sum(dsk[g * BLK:(g + 1) * BLK], axis=0, keepdims=True), 0.0)
            dsink_ref[0] += acc

        @pl.when(n == nb)
        def _():
            dk_ref[0] = ck[...]
            dv_ref[0] = cv[...]

    last = nb - 1
    prev = pl.BlockSpec((1, BLK, HEAD), lambda h, n: (h, jnp.maximum(jnp.minimum(n, last) - 1, 0), 0))
    cur = pl.BlockSpec((1, BLK, HEAD), lambda h, n: (h, jnp.minimum(n, last), 0))
    kv_out = pl.BlockSpec((1, BLK, HEAD), lambda h, n: (h, jnp.maximum(n - 1, 0), 0))
    return pl.pallas_call(
        _after(body, 7, deps), grid=(N_KV, nb + 1),
        in_specs=[pl.BlockSpec(memory_space=pltpu.SMEM),
                  pl.BlockSpec((GQA, BLK, HEAD), lambda h, n: (h, jnp.minimum(n, last), 0)),
                  prev, cur, prev, cur,
                  pl.BlockSpec((BLK, GQA * HEAD), lambda h, n: (jnp.minimum(n, last), h))] + _any_specs(deps),
        out_specs=[pl.BlockSpec((GQA, BLK, HEAD), lambda h, n: (h, jnp.minimum(n, last), 0)), kv_out, kv_out,
                   pl.BlockSpec((1, 1, 128), lambda h, n: (h, 0, 0))],
        out_shape=[_sds((N_Q, s, HEAD), F32), _sds((N_KV, s, HEAD), F32), _sds((N_KV, s, HEAD), F32),
                   _sds((N_KV, 1, 128), F32)],
        scratch_shapes=[pltpu.VMEM((BLK, HEAD), F32), pltpu.VMEM((BLK, HEAD), F32)],
        compiler_params=_cp(("parallel", "arbitrary")), name="attn_bwd")(sinks, qh, kh, kh, vh, vh, db, *deps)


def _qk_prep_bwd(dqh, dkh, dvh, qkv, pos, wq, wk, invf, bd, tm):
    s = qkv.shape[0]

    def fold_heads(row):
        out = row[:, 0:HEAD]
        for h in range(1, row.shape[1] // HEAD):
            out = out + row[:, HEAD * h:HEAD * (h + 1)]
        return out

    def body(dq_ref, dk_ref, dv_ref, qkv_ref, pos_ref, wq_ref, wk_ref, invf_ref, bd_ref,
             dz_ref, dwq_ref, dwk_ref):
        @pl.when(pl.program_id(0) == 0)
        def _():
            dwq_ref[...] = jnp.zeros_like(dwq_ref)
            dwk_ref[...] = jnp.zeros_like(dwk_ref)

        cos, sa, sb = _rope_tables(pos_ref, invf_ref)

        def norm_rope_bwd(dy, xin, w, bdm):
            dn = _rope_t(dy, cos, sa, sb)
            r = lax.rsqrt(_head_sum(xin * xin, bdm) * (1.0 / HEAD) + EPS)
            nh = xin * r
            gw = dn * w
            dx = r * (gw - nh * (_head_sum(gw * nh, bdm) * (1.0 / HEAD)))
            return dx, fold_heads(jnp.sum(dn * nh, axis=0, keepdims=True))

        dq = jnp.concatenate([dq_ref[h] for h in range(N_Q)], axis=1) * (HEAD ** -0.5)
        dk = jnp.concatenate([dk_ref[h] for h in range(N_KV)], axis=1)
        dxq, dwq = norm_rope_bwd(dq, qkv_ref[:, 0:Q_W], wq_ref[...], bd_ref[...])
        dxk, dwk = norm_rope_bwd(dk, qkv_ref[:, Q_W:Q_W + KV_W], wk_ref[...], bd_ref[0:KV_W, 0:KV_W])
        dz_ref[:, 0:Q_W] = dxq.astype(BF)
        dz_ref[:, Q_W:Q_W + KV_W] = dxk.astype(BF)
        dz_ref[:, Q_W + KV_W:QKV_W] = jnp.concatenate([dv_ref[h] for h in range(N_KV)], axis=1).astype(BF)
        dwq_ref[...] += dwq
        dwk_ref[...] += dwk

    heads = lambda n: pl.BlockSpec((n, tm, HEAD), lambda i: (0, i, 0))
    return pl.pallas_call(
        body, grid=(s // tm,),
        in_specs=[heads(N_Q), heads(N_KV), heads(N_KV), pl.BlockSpec((tm, QKV_W), lambda i: (i, 0)),
                  pl.BlockSpec((tm, 1), lambda i: (i, 0)), _full((1, Q_W)), _full((1, KV_W)),
                  _full((1, 2 * HEAD)), _full((Q_W, Q_W))],
        out_specs=[pl.BlockSpec((tm, QKV_W), lambda i: (i, 0)), _full((1, HEAD)), _full((1, HEAD))],
        out_shape=[_sds((s, QKV_W), BF), _sds((1, HEAD), F32), _sds((1, HEAD), F32)],
        compiler_params=_cp(("arbitrary",)), name="qk_prep_bwd")(dqh, dkh, dvh, qkv, pos, wq, wk, invf, bd)


def _in_proj_bwd(du, dzq, dzg, w_in, x, g1, dy1, tm, deps=()):
    s = x.shape[0]

    def body(du_ref, dzq_ref, dzg_ref, w_ref, x_ref, g_ref, dy_ref, gx_ref, dgn_ref):
        @pl.when(pl.program_id(0) == 0)
        def _():
            dgn_ref[...] = jnp.zeros_like(dgn_ref)

        dh = _nt(du_ref[...], w_ref[:, 0:D])
        dh = dh + _nt(dzq_ref[...], w_ref[:, D:D + QKV_W])
        dh = dh + _nt(dzg_ref[...], w_ref[:, D + QKV_W:IN_W])
        dx, dgn = _rms_bwd(dh, x_ref[...], g_ref[...])
        gx_ref[...] = dy_ref[...] + dx
        dgn_ref[...] += jnp.sum(dgn, axis=0, keepdims=True)

    row = lambda w: pl.BlockSpec((tm, w), lambda i: (i, 0))
    return pl.pallas_call(
        _after(body, 7, deps), grid=(s // tm,),
        in_specs=[row(D), row(QKV_W), row(2 * D), _full((D, IN_W)), row(D), _full((1, D)), row(D)] + _any_specs(deps),
        out_specs=[row(D), _full((1, D))],
        out_shape=[_sds((s, D), F32), _sds((1, D), F32)],
        compiler_params=_cp(("arbitrary",)), name="in_proj_bwd")(du, dzq, dzg, w_in, x, g1, dy1, *deps)


def _adamw(w, g, m, v, tr, name, deps=()):
    r, c = w.shape

    def body(w_ref, g_ref, m_ref, v_ref, go_ref, d_ref, mo_ref, vo_ref):
        gv = g_ref[...]
        mn = B1 * m_ref[...] + (1.0 - B1) * gv
        vn = B2 * v_ref[...] + (1.0 - B2) * (gv * gv)
        m_hat = mn / (1.0 - B1 ** STEP)
        v_hat = vn / (1.0 - B2 ** STEP)
        go_ref[...] = gv
        d_ref[...] = -LR * (m_hat / (jnp.sqrt(v_hat) + ADAM_EPS) + WD * w_ref[...])
        mo_ref[...] = mn
        vo_ref[...] = vn

    blk = pl.BlockSpec((tr, c), lambda i: (i, 0))
    return pl.pallas_call(
        _after(body, 4, deps), grid=(r // tr,), in_specs=[blk] * 4 + _any_specs(deps), out_specs=[blk] * 4,
        out_shape=[_sds((r, c), F32)] * 4, compiler_params=_cp(("parallel",)), name=name)(w, g, m, v, *deps)


def _place():
    x, y, c = lax.axis_index("x"), lax.axis_index("y"), lax.axis_index("c")
    chips = [(1 - x, y), (x, 1 - y), (1 - x, 1 - y)]
    return x, y, c, chips


def _rows(ref, lead, h, rh):
    sl = pl.ds(pl.multiple_of(h * rh, 16), rh)
    return ref.at[sl, :] if lead is None else ref.at[lead, sl, :]


def _all_gather_weights(halved, whole):
    nh, nw = len(halved), len(whole)
    na = nh + nw
    arrays = list(halved) + list(whole)
    out_dtypes = [BF] * nh + [a.dtype for a in whole]
    cast_rows = 128

    def body(*refs):
        ins, outs = refs[:na], refs[na:2 * na]
        raw, stage = refs[2 * na:3 * na], refs[3 * na:3 * na + nh]
        ici_send, ici_recv, fwd_send, fwd_recv, in_sem, loc_sem = refs[3 * na + nh:]
        x, y, c, chips = _place()
        me = 2 * x + y
        sibling = (x, y, 1 - c)
        loads = [pltpu.make_async_copy(ins[a], raw[a], in_sem.at[a]) for a in range(na)]
        for cp in loads:
            cp.start()

        def ici(a, j, src_chip, src=None):
            if a < nh:
                rh = arrays[a].shape[0] // 2
                dst = _rows(outs[a], src_chip, c, rh)
                src = dst if src is None else _rows(src, None, c, rh)
            else:
                dst = outs[a].at[src_chip]
                src = dst if src is None else src
            return pltpu.make_async_remote_copy(
                src_ref=src, dst_ref=dst, send_sem=ici_send.at[3 * a + j], recv_sem=ici_recv.at[3 * a + j],
                device_id=(*chips[j], c), device_id_type=MESH)

        def fwd(a, j, half):
            rh = arrays[a].shape[0] // 2
            kj = 2 * chips[j][0] + chips[j][1]
            blk = _rows(outs[a], kj, half, rh)
            return pltpu.make_async_remote_copy(
                src_ref=blk, dst_ref=blk, send_sem=fwd_send.at[3 * a + j], recv_sem=fwd_recv.at[3 * a + j],
                device_id=sibling, device_id_type=MESH)

        local, sends = [], []
        for a in range(na):
            loads[a].wait()
            if a < nh:
                r = arrays[a].shape[0]
                for r0 in range(0, r, cast_rows):
                    r1 = min(r0 + cast_rows, r)
                    stage[a][r0:r1, :] = raw[a][r0:r1, :].astype(BF)
                own = stage[a]
            else:
                own = raw[a]
            cp = pltpu.make_async_copy(own, outs[a].at[me], loc_sem.at[a])
            cp.start()
            local.append(cp)
            for j in range(3):
                cp = ici(a, j, me, src=own)
                cp.start()
                sends.append(cp)
        passed = []
        for a in range(na):
            for j in range(3):
                kj = 2 * chips[j][0] + chips[j][1]
                ici(a, j, kj).wait_recv()
                if a < nh:
                    cp = fwd(a, j, c)
                    cp.start()
                    passed.append(cp)
        for a in range(nh):
            for j in range(3):
                fwd(a, j, 1 - c).wait_recv()
        for cp in sends + passed:
            cp.wait_send()
        for cp in local:
            cp.wait()

    any_spec = pl.BlockSpec(memory_space=pl.ANY)
    return pl.pallas_call(
        body, in_specs=[any_spec] * na, out_specs=[any_spec] * na,
        out_shape=[_sds((N_CHIPS,) + a.shape, dt) for a, dt in zip(arrays, out_dtypes)],
        scratch_shapes=[pltpu.VMEM(a.shape, a.dtype) for a in arrays] + [pltpu.VMEM(a.shape, BF) for a in halved]
        + [pltpu.SemaphoreType.DMA((3 * na,)), pltpu.SemaphoreType.DMA((3 * na,)),
           pltpu.SemaphoreType.DMA((3 * nh,)), pltpu.SemaphoreType.DMA((3 * nh,)),
           pltpu.SemaphoreType.DMA((na,)), pltpu.SemaphoreType.DMA((na,))],
        compiler_params=pltpu.CompilerParams(vmem_limit_bytes=VMEM_LIMIT_MB << 20),
        name="all_gather_weights")(*arrays)


def _sibling_halves(grads):
    na = len(grads)

    def body(*refs):
        ins, outs = refs[:na], refs[na:2 * na]
        send_sem, recv_sem = refs[2 * na:]
        x, y, c, _ = _place()
        copies = []
        for a in range(na):
            rh = grads[a].shape[1] // 2
            src = ins[a].at[:, pl.ds(pl.multiple_of((1 - c) * rh, 8), rh), :]
            copies.append(pltpu.make_async_remote_copy(
                src_ref=src, dst_ref=outs[a], send_sem=send_sem.at[a], recv_sem=recv_sem.at[a],
                device_id=(x, y, 1 - c), device_id_type=MESH))
        for cp in copies:
            cp.start()
        for cp in copies:
            cp.wait()

    any_spec = pl.BlockSpec(memory_space=pl.ANY)
    return pl.pallas_call(
        body, in_specs=[any_spec] * na, out_specs=[any_spec] * na,
        out_shape=[_sds((N_CHIPS, g.shape[1] // 2, g.shape[2]), F32) for g in grads],
        scratch_shapes=[pltpu.SemaphoreType.DMA((na,)), pltpu.SemaphoreType.DMA((na,))],
        name="sibling_halves")(*grads)


def _pair_sum(g, recv, c, tr, name):
    _, r, cols = g.shape
    rh = r // 2
    nr = rh // tr

    def body(c_ref, g_ref, r_ref, o_ref):
        o_ref[...] = (g_ref[...] + r_ref[...]).astype(BF)

    grid_spec = pltpu.PrefetchScalarGridSpec(
        num_scalar_prefetch=1, grid=(N_CHIPS, nr),
        in_specs=[pl.BlockSpec((1, tr, cols), lambda k, i, c_ref: (k, c_ref[0] * nr + i, 0)),
                  pl.BlockSpec((1, tr, cols), lambda k, i, c_ref: (k, i, 0))],
        out_specs=pl.BlockSpec((1, tr, cols), lambda k, i, c_ref: (k, i, 0)))
    return pl.pallas_call(
        body, grid_spec=grid_spec, out_shape=_sds((N_CHIPS, rh, cols), BF),
        compiler_params=_cp(("parallel", "parallel")), name=name)(c, g, recv)


def _chip_exchange(halves, small):
    na = len(halves)
    srows = small.shape[0]

    def body(*refs):
        ins, small_ref = refs[:na], refs[na]
        outs, small_out = refs[na + 1:2 * na + 1], refs[2 * na + 1]
        send_sem, recv_sem, s_send, s_recv = refs[2 * na + 2:]
        x, y, c, chips = _place()
        me = 4 * x + 2 * y + c
        copies = []
        for a in range(na):
            for j in range(3):
                kj = 2 * chips[j][0] + chips[j][1]
                copies.append(pltpu.make_async_remote_copy(
                    src_ref=ins[a].at[kj], dst_ref=outs[a].at[j],
                    send_sem=send_sem.at[3 * a + j], recv_sem=recv_sem.at[3 * a + j],
                    device_id=(*chips[j], c), device_id_type=MESH))
        for r in range(1, N_DEV):
            peer = (x ^ (r >> 2), y ^ ((r >> 1) & 1), c ^ (r & 1))
            copies.append(pltpu.make_async_remote_copy(
                src_ref=small_ref, dst_ref=small_out.at[me],
                send_sem=s_send.at[r - 1], recv_sem=s_recv.at[r - 1], device_id=peer, device_id_type=MESH))
        for cp in copies:
            cp.start()
        small_out[pl.ds(me, 1)] = small_ref[...][None]
        for cp in copies:
            cp.wait()

    any_spec = pl.BlockSpec(memory_space=pl.ANY)
    vmem = pl.BlockSpec(memory_space=pltpu.VMEM)
    return pl.pallas_call(
        body, in_specs=[any_spec] * na + [vmem], out_specs=[any_spec] * na + [vmem],
        out_shape=[_sds((3,) + h.shape[1:], h.dtype) for h in halves] + [_sds((N_DEV, srows, 128), F32)],
        scratch_shapes=[pltpu.SemaphoreType.DMA((3 * na,)), pltpu.SemaphoreType.DMA((3 * na,)),
                        pltpu.SemaphoreType.DMA((N_DEV - 1,)), pltpu.SemaphoreType.DMA((N_DEV - 1,))],
        name="chip_exchange")(*halves, small)


def _chip_sum(g, sib, recv, place, tr, name):
    _, r, cols = g.shape
    rh = r // 2
    nr = rh // tr

    def body(p_ref, g_ref, s_ref, r0_ref, r1_ref, r2_ref, o_ref):
        own = g_ref[0] + s_ref[0]
        o_ref[...] = ((own + r0_ref[0].astype(F32)) + r1_ref[0].astype(F32)) + r2_ref[0].astype(F32)

    rspec = lambda j: pl.BlockSpec((1, tr, cols), lambda i, p: (j, i, 0))
    grid_spec = pltpu.PrefetchScalarGridSpec(
        num_scalar_prefetch=1, grid=(nr,),
        in_specs=[pl.BlockSpec((1, tr, cols), lambda i, p: (p[0], p[1] * nr + i, 0)),
                  pl.BlockSpec((1, tr, cols), lambda i, p: (p[0], i, 0)), rspec(0), rspec(1), rspec(2)],
        out_specs=pl.BlockSpec((tr, cols), lambda i, p: (p[1] * nr + i, 0)))
    return pl.pallas_call(
        body, grid_spec=grid_spec, out_shape=_sds((r, cols), F32),
        compiler_params=_cp(("parallel",)), name=name)(place, g, sib, recv, recv, recv)


def _sibling_exchange(shards):
    na = len(shards)

    def body(*refs):
        ins, outs = refs[:na], refs[na:2 * na]
        send_sem, recv_sem = refs[2 * na:]
        x, y, c, _ = _place()
        for a in range(na):
            rh = shards[a].shape[0] // 2
            pltpu.make_async_remote_copy(
                src_ref=_rows(ins[a], None, c, rh), dst_ref=_rows(outs[a], None, c, rh),
                send_sem=send_sem.at[a], recv_sem=recv_sem.at[a],
                device_id=(x, y, 1 - c), device_id_type=MESH).start()
        for a in range(na):
            rh = shards[a].shape[0] // 2
            pltpu.make_async_remote_copy(
                src_ref=_rows(ins[a], None, c, rh), dst_ref=_rows(outs[a], None, 1 - c, rh),
                send_sem=send_sem.at[a], recv_sem=recv_sem.at[a],
                device_id=(x, y, 1 - c), device_id_type=MESH).wait()

    any_spec = pl.BlockSpec(memory_space=pl.ANY)
    return pl.pallas_call(
        body, in_specs=[any_spec] * na, out_specs=[any_spec] * na,
        out_shape=[_sds(h.shape, F32) for h in shards],
        input_output_aliases={a: a for a in range(na)},
        scratch_shapes=[pltpu.SemaphoreType.DMA((na,)), pltpu.SemaphoreType.DMA((na,))],
        name="sibling_exchange")(*shards)


def _device_sum(stack, deps=()):
    _, rows, _ = stack.shape

    def body(s_ref, o_ref):
        acc = s_ref[0]
        for d in range(1, N_DEV):
            acc = acc + s_ref[d]
        o_ref[...] = acc

    vmem = pl.BlockSpec(memory_space=pltpu.VMEM)
    return pl.pallas_call(_after(body, 1, deps), in_specs=[vmem] + _any_specs(deps), out_specs=vmem,
                          out_shape=_sds((rows, 128), F32), name="device_sum")(stack, *deps)


_HBM = pl.BlockSpec(memory_space=pltpu.HBM)
_SEM = pl.BlockSpec(memory_space=pltpu.SEMAPHORE)
_EFFECT = pltpu.SideEffectType.DATAFLOW_SIDE_EFFECTING


def _remote(src, dst, ssem, rsem, k, device):
    return pltpu.make_async_remote_copy(src_ref=src, dst_ref=dst, send_sem=ssem.at[k], recv_sem=rsem.at[k],
                                        device_id=device, device_id_type=MESH)


def _split_start(name, bufs, plan, n):
    nb = len(bufs)

    def body(*refs):
        sends, _ = plan(refs[:nb], refs[nb], refs[nb + 1])
        for cp in sends:
            cp.start()
        refs[-1][...] = jnp.zeros_like(refs[-1])

    res = pl.pallas_call(
        body, name=name,
        out_shape=(pltpu.SemaphoreType.DMA((n,)), pltpu.SemaphoreType.DMA((n,)))
        + tuple(pltpu.HBM(b.shape, b.dtype) for b in bufs) + (_sds((8, 128), F32),),
        in_specs=[_HBM] * nb,
        out_specs=(_SEM, _SEM) + (_HBM,) * nb + (pl.BlockSpec(memory_space=pltpu.VMEM),),
        input_output_aliases={i: i + 2 for i in range(nb)},
        compiler_params=pltpu.CompilerParams(has_side_effects=_EFFECT),
    )(*[pltpu.with_memory_space_constraint(b, pltpu.HBM) for b in bufs])
    return res[0], res[1], list(res[2:2 + nb]), res[2 + nb]


def _split_wait(name, send_sem, recv_sem, bufs, plan, after):
    nb = len(bufs)

    def body(*refs):
        sends, arrivals = plan(refs[:nb], refs[nb], refs[nb + 1])
        for cp in sends:
            cp.wait_send()
        for cp in arrivals:
            cp.wait_recv()

    res = pl.pallas_call(
        body, name=name, out_shape=tuple(pltpu.HBM(b.shape, b.dtype) for b in bufs),
        in_specs=[_HBM] * nb + [_SEM, _SEM, pl.BlockSpec(memory_space=pl.ANY)],
        out_specs=(_HBM,) * nb, input_output_aliases={i: i for i in range(nb)},
        compiler_params=pltpu.CompilerParams(has_side_effects=_EFFECT),
    )(*bufs, send_sem, recv_sem, after)
    return list(res)


def _plan_sibling_halves(shapes):
    na = len(shapes)

    def plan(refs, ssem, rsem):
        x, y, c, _ = _place()
        cps = []
        for a in range(na):
            rh = shapes[a][1] // 2
            src = refs[a].at[:, pl.ds(pl.multiple_of((1 - c) * rh, 8), rh), :]
            cps.append(_remote(src, refs[na + a], ssem, rsem, a, (x, y, 1 - c)))
        return cps, cps

    return plan


def _plan_chip_exchange(na, with_small):
    def plan(refs, ssem, rsem):
        x, y, c, chips = _place()
        cps = []
        for a in range(na):
            for j in range(3):
                kj = 2 * chips[j][0] + chips[j][1]
                cps.append(_remote(refs[a].at[kj], refs[na + a].at[j], ssem, rsem, 3 * a + j, (*chips[j], c)))
        if with_small:
            mine = refs[2 * na].at[4 * x + 2 * y + c]
            for r in range(1, N_DEV):
                peer = (x ^ (r >> 2), y ^ ((r >> 1) & 1), c ^ (r & 1))
                cps.append(_remote(mine, mine, ssem, rsem, 3 * na + r - 1, peer))
        return cps, cps

    return plan


def _plan_sibling_swap(shapes):
    def plan(refs, ssem, rsem):
        x, y, c, _ = _place()
        sends, arrivals = [], []
        for a, shp in enumerate(shapes):
            rh = shp[0] // 2
            mine, other = _rows(refs[a], None, c, rh), _rows(refs[a], None, 1 - c, rh)
            sends.append(_remote(mine, mine, ssem, rsem, a, (x, y, 1 - c)))
            arrivals.append(_remote(mine, other, ssem, rsem, a, (x, y, 1 - c)))
        return sends, arrivals

    return plan


def _plan_gather_chips(shapes):
    def plan(refs, ssem, rsem):
        x, y, c, chips = _place()
        me = 2 * x + y
        sends, arrivals = [], []
        for a, shp in enumerate(shapes):
            rh = shp[1] // 2
            mine = _rows(refs[a], me, c, rh)
            for j in range(3):
                land = _rows(refs[a], 2 * chips[j][0] + chips[j][1], c, rh)
                sends.append(_remote(mine, mine, ssem, rsem, 3 * a + j, (*chips[j], c)))
                arrivals.append(_remote(land, land, ssem, rsem, 3 * a + j, (*chips[j], c)))
        return sends, arrivals

    return plan


def _plan_gather_sibling(shapes):
    def plan(refs, ssem, rsem):
        x, y, c, chips = _place()
        sends, arrivals = [], []
        for a, shp in enumerate(shapes):
            rh = shp[1] // 2
            for j in range(3):
                kj = 2 * chips[j][0] + chips[j][1]
                got, land = _rows(refs[a], kj, c, rh), _rows(refs[a], kj, 1 - c, rh)
                sends.append(_remote(got, got, ssem, rsem, 3 * a + j, (x, y, 1 - c)))
                arrivals.append(_remote(got, land, ssem, rsem, 3 * a + j, (x, y, 1 - c)))
        return sends, arrivals

    return plan


def _into_slice(w, k, n, tr, dtype, name):
    r, cols = w.shape

    def body(k_ref, w_ref, o_ref):
        o_ref[0] = w_ref[...].astype(dtype)

    grid_spec = pltpu.PrefetchScalarGridSpec(
        num_scalar_prefetch=1, grid=(r // tr,),
        in_specs=[pl.BlockSpec((tr, cols), lambda i, k: (i, 0))],
        out_specs=pl.BlockSpec((1, tr, cols), lambda i, k: (k[0], i, 0)))
    return pl.pallas_call(body, grid_spec=grid_spec, out_shape=_sds((n, r, cols), dtype),
                          compiler_params=_cp(("parallel",)), name=name)(k, w)


class _LateWeights:
    def __init__(self, shards, chip, names, tiles):
        bufs = [_into_slice(w, chip, N_CHIPS, t, BF, "own_" + nm) for w, nm, t in zip(shards, names, tiles)]
        self.n = 3 * len(bufs)
        self.chips, self.sibling = _plan_gather_chips([b.shape for b in bufs]), _plan_gather_sibling([b.shape for b in bufs])
        self.ssem, self.rsem, self.bufs, token = _split_start("gather_chips_start", bufs, self.chips, self.n)
        self.first = (token,)

    def middle(self, after):
        bufs = _split_wait("gather_chips_wait", self.ssem, self.rsem, self.bufs, self.chips, after)
        self.ssem, self.rsem, self.bufs, token = _split_start("gather_sibling_start", bufs, self.sibling, self.n)
        return (token,)

    def last(self, after):
        return _split_wait("gather_sibling_wait", self.ssem, self.rsem, self.bufs, self.sibling, after)


class _GradReduce:
    def __init__(self, tag, place, names, tiles):
        self.tag, self.place, self.names, self.tiles = tag, place, names, tiles
        self.small_all = None

    def first(self, grads):
        self.na = len(grads)
        self.p1 = _plan_sibling_halves([g.shape for g in grads])
        lands = [lax.empty((N_CHIPS, g.shape[1] // 2, g.shape[2]), F32) for g in grads]
        self.ssem, self.rsem, self.bufs, token = _split_start(
            self.tag + "_halves_start", list(grads) + lands, self.p1, self.na)
        return (token,)

    def second(self, after, small=None):
        bufs = _split_wait(self.tag + "_halves_wait", self.ssem, self.rsem, self.bufs, self.p1, after)
        self.grads, self.sib = bufs[:self.na], bufs[self.na:]
        halves = [_pair_sum(g, r, self.place[1:2], t, "pair_sum_" + nm)
                  for g, r, t, nm in zip(self.grads, self.sib, self.tiles, self.names)]
        lands = [lax.empty((3,) + h.shape[1:], h.dtype) for h in halves]
        extra = [] if small is None else [small]
        self.p2 = _plan_chip_exchange(self.na, small is not None)
        self.ssem, self.rsem, self.bufs, token = _split_start(
            self.tag + "_chips_start", halves + lands + extra, self.p2, 3 * self.na + (N_DEV - 1) * len(extra))
        return (token,)

    def third(self, after):
        bufs = _split_wait(self.tag + "_chips_wait", self.ssem, self.rsem, self.bufs, self.p2, after)
        if len(bufs) > 2 * self.na:
            self.small_all = bufs[2 * self.na]
        mine = [_chip_sum(g, sb, r, self.place, t, "chip_sum_" + nm)
                for g, sb, r, t, nm in zip(self.grads, self.sib, bufs[self.na:2 * self.na], self.tiles, self.names)]
        self.p3 = _plan_sibling_swap([m.shape for m in mine])
        self.ssem, self.rsem, self.bufs, token = _split_start(self.tag + "_swap_start", mine, self.p3, self.na)
        return (token,)

    def last(self, after):
        return _split_wait(self.tag + "_swap_wait", self.ssem, self.rsem, self.bufs, self.p3, after)


def _pack(parts, rows):
    flat = jnp.concatenate([p.reshape(-1) for p in parts])
    return jnp.pad(flat, (0, rows * 128 - flat.shape[0])).reshape(rows, 128)


def _unpack(buf, shapes):
    flat = buf.reshape(-1)
    out, off = [], 0
    for shp in shapes:
        n = 1
        for d in shp:
            n *= d
        out.append(flat[off:off + n].reshape(shp))
        off += n
    return out


def _rows_for(n):
    return -(-n // (8 * 128)) * 8


class _WeightsAtHand:
    def __init__(self, wup, wout, wdown):
        self.first, self.weights = (), [wup, wout, wdown]

    def middle(self, after):
        return ()

    def last(self, after):
        return self.weights


class _GradsKept:
    def first(self, grads):
        self.grads = list(grads)
        return ()

    def second(self, after, small=None):
        return ()

    def third(self, after):
        return ()

    def last(self, after):
        return self.grads


def _forward_backward(xs, pos, tgt, win, wpool, cw, attn_norm, b_gate, pool_scale, q_norm, k_norm, sinks,
                      ffn_norm, conv_b, late, early, rest):
    s = xs.shape[0]
    tm = min(256, s)
    tk = min(1024, s)
    inv_freq = ROPE_THETA ** (-jnp.arange(0, ROPE_DIM, 2, dtype=F32) / ROPE_DIM)
    lane = jnp.arange(2 * HEAD) % HEAD
    invf = jnp.where(lane < ROPE_DIM, inv_freq[lane % (ROPE_DIM // 2)], 0.0).reshape(1, 2 * HEAD)
    wq = jnp.tile(q_norm, (1, N_Q))
    wk = jnp.tile(k_norm, (1, N_KV))
    head_of = jnp.arange(Q_W) // HEAD
    bd = (head_of[:, None] == head_of[None, :]).astype(BF)
    sink = sinks[0]

    h1, u, qkv, gates = _attn_in_proj(xs, attn_norm, win, b_gate, tm, deps=late.first)
    qh, kh, vh = _qk_prep(qkv, pos, wq, wk, invf, bd, tm)
    battn = _attn_fwd(qh, kh, vh, sink, deps=late.middle(qh))
    apool, pooled = _pool_fwd(u, wpool, pool_scale, min(512, s))
    wup, wout, wdown = late.last(apool)
    wout = wout.reshape(D, D)
    wdown = wdown.reshape(D_FF, D)
    mix, y1, h2 = _mix_out_proj(apool, battn, gates, xs, wout, ffn_norm, tm)
    pre_g, pre_v, up_g, up_v, act = _ffn_up(h2, wup, cw, conv_b, tm)
    dy2, dy2b, loss_acc = _ffn_down_loss(act, wdown, y1, tgt, tm)

    d_wdown = _grad_matmul(act, dy2b, 512, tk, "grad_w_down")
    dp_g, dp_v, dcw_g, dcw_v, dcb_g, dcb_v = _ffn_act_bwd(dy2b, wdown, up_g, up_v, pre_g, pre_v, cw, tm)
    d_wup = _grad_matmul(h2, dp_g, UP_SHARD, tk, "grad_w_up_gate", lead=N_CHIPS)
    d_wup = _grad_matmul(h2, dp_v, UP_SHARD, tk, "grad_w_up_value", lead=N_CHIPS, prev=d_wup, lead_off=2)
    token = early.first([d_wdown.reshape(N_CHIPS, D_FF // N_CHIPS, D), d_wup])
    dy1, d_ffn_norm = _ffn_up_bwd(dp_g, dp_v, wup, y1, dy2, ffn_norm, tm, deps=token)
    token = early.second(dy1)
    da, db, dzg, d_bgate, d_wout = _out_proj_bwd(dy1, wout, apool, battn, gates, mix, tm, deps=token)
    du, d_wpool, d_pscale = _pool_bwd(da, pooled, wpool, pool_scale, min(512, s))
    token = early.third(du)
    dqh, dkh, dvh, dsink = _attn_bwd(qh, kh, vh, sink, db, deps=token)
    dzq, d_qn, d_kn = _qk_prep_bwd(dqh, dkh, dvh, qkv, pos, wq, wk, invf, bd, tm)
    d_win = jnp.concatenate([
        _grad_matmul(h1, du, D, tk, "grad_w_in_pool"),
        _grad_matmul(h1, dzq, QKV_W, tk, "grad_w_in_qkv"),
        _grad_matmul(h1, dzg, D, tk, "grad_w_in_gates")], axis=1)
    token = rest.first([
        d_win.reshape(D, N_CHIPS, IN_W // N_CHIPS).transpose(1, 0, 2),
        d_wout.reshape(N_CHIPS, D // N_CHIPS, D),
        d_wpool.reshape(4, N_CHIPS, 64, POOL_GROUP).transpose(1, 0, 2, 3).reshape(N_CHIPS, 4 * 64, POOL_GROUP)])
    grad_x, d_attn_norm = _in_proj_bwd(du, dzq, dzg, win, xs, attn_norm, dy1, tm, deps=token)
    small_parts = [d_attn_norm, d_bgate, d_pscale, d_qn, d_kn, dsink[:, 0, 0:GQA], d_ffn_norm,
                   jnp.concatenate([dcb_g, dcb_v], axis=1), jnp.concatenate([dcw_g, dcw_v], axis=0)]
    return loss_acc, grad_x, small_parts


def kernel(x, positions, attn_norm, w_in, b_gate, w_pool, pool_scale, q_norm, k_norm, sinks, w_out, ffn_norm, w_up, conv_w, conv_b, w_down, loss_target, m_attn_norm, m_w_in, m_b_gate, m_w_pool, m_pool_scale, m_q_norm, m_k_norm, m_sinks, m_w_out, m_ffn_norm, m_w_up, m_conv_w, m_conv_b, m_w_down, v_attn_norm, v_w_in, v_b_gate, v_w_pool, v_pool_scale, v_q_norm, v_k_norm, v_sinks, v_w_out, v_ffn_norm, v_w_up, v_conv_w, v_conv_b, v_w_down):
    s = x.shape[1]
    xs = x[0]
    tgt = loss_target[0]
    pos = positions[0].reshape(s, 1)
    cx, cy, cc = lax.axis_index("x"), lax.axis_index("y"), lax.axis_index("c")
    chip = 2 * cx + cy

    chip_arr = chip.reshape(1).astype(jnp.int32)
    dev_arr = (2 * chip + cc).reshape(1).astype(jnp.int32)
    place = jnp.stack([chip, cc]).astype(jnp.int32)

    g_in, g_pool, g_cw = _all_gather_weights([w_in[0], w_pool[0].reshape(4 * 64, POOL_GROUP)], [conv_w[0]])
    win = g_in.transpose(1, 0, 2).reshape(D, IN_W)
    wpool = g_pool.reshape(N_CHIPS, 4, 64, POOL_GROUP).transpose(1, 0, 2, 3).reshape(4, POOL_GROUP, POOL_GROUP)
    late = _LateWeights([w_up[0], w_out[0], w_down[0]], chip_arr, ["w_up", "w_out", "w_down"], [256, 256, 352])
    early = _GradReduce("early", place, ["w_down", "w_up"], [176, 256])
    rest = _GradReduce("rest", place, ["w_in", "w_out", "w_pool"], [256, 128, 128])

    loss_acc, grad_x, small_parts = _forward_backward(
        xs, pos, tgt, win, wpool, g_cw, attn_norm, b_gate, pool_scale, q_norm, k_norm, sinks, ffn_norm, conv_b,
        late, early, rest)
    loss = lax.psum(jnp.sum(loss_acc) * (0.5 / D), ("x", "y", "c"))

    def two_d(a):
        return a.reshape(-1, a.shape[-1])

    def update(nm, w, g, m, v, tr, deps=()):
        res = _adamw(two_d(w), g, two_d(m), two_d(v), tr, "adamw_" + nm, deps=deps)
        return [r.reshape(w.shape) for r in res]

    small_shapes = [(1, D), (1, 2 * D), (1, D), (1, HEAD), (1, HEAD), (1, N_Q), (1, D), (1, 2 * D_FF),
                    (N_CHIPS, 3, UP_SHARD)]
    n_small = sum(p.size for p in small_parts)
    small = _into_slice(_pack(small_parts, _rows_for(n_small)), dev_arr, N_DEV, _rows_for(n_small), F32, "own_small")
    g_wdown, g_wup = early.last(grad_x)
    token = rest.second(grad_x, small=small)
    big_out = {"w_up": update("w_up", w_up, g_wup, m_w_up, v_w_up, 256, deps=token),
               "w_down": update("w_down", w_down, g_wdown, m_w_down, v_w_down, 176, deps=token)}
    token = rest.third(big_out["w_down"][1])
    small_sum = _device_sum(rest.small_all, deps=token)
    (g_attn_norm, g_bgate, g_pscale, g_qn, g_kn, g_sinks, g_ffn_norm, g_convb, g_convw_all) = _unpack(
        small_sum, small_shapes)
    g_convw = lax.dynamic_index_in_dim(g_convw_all, chip, axis=0, keepdims=False)

    small_names = ["attn_norm", "b_gate", "pool_scale", "q_norm", "k_norm", "sinks", "ffn_norm", "conv_b", "conv_w"]
    sm_w = [attn_norm, b_gate, pool_scale, q_norm, k_norm, sinks, ffn_norm, conv_b, conv_w]
    sm_m = [m_attn_norm, m_b_gate, m_pool_scale, m_q_norm, m_k_norm, m_sinks, m_ffn_norm, m_conv_b, m_conv_w]
    sm_v = [v_attn_norm, v_b_gate, v_pool_scale, v_q_norm, v_k_norm, v_sinks, v_ffn_norm, v_conv_b, v_conv_w]
    sm_g = [g_attn_norm, g_bgate, g_pscale, g_qn, g_kn, g_sinks, g_ffn_norm, g_convb, g_convw]
    sm_rows = _rows_for(sum(w.size for w in sm_w))
    res = _adamw(_pack(sm_w, sm_rows), _pack(sm_g, sm_rows), _pack(sm_m, sm_rows), _pack(sm_v, sm_rows),
                 sm_rows, "adamw_small")
    sm_out = [_unpack(r, [w.shape for w in sm_w]) for r in res]
    small_out = {nm: [sm_out[k][i] for k in range(4)] for i, nm in enumerate(small_names)}
    g_win, g_wout, g_wpool = rest.last(res[1])
    big_out["w_in"] = update("w_in", w_in, g_win, m_w_in, v_w_in, 256)
    big_out["w_out"] = update("w_out", w_out, g_wout, m_w_out, v_w_out, 128)
    big_out["w_pool"] = update("w_pool", w_pool, g_wpool, m_w_pool, v_w_pool, 128)

    order = ["attn_norm", "w_in", "b_gate", "w_pool", "pool_scale", "q_norm", "k_norm", "sinks", "w_out",
             "ffn_norm", "w_up", "conv_w", "conv_b", "w_down"]
    allout = {**big_out, **small_out}
    outs = [loss, grad_x[None]]
    for k in range(4):
        outs += [allout[nm][k] for nm in order]
    return tuple(outs)
```

```python
import functools

import jax
import jax.numpy as jnp
from jax import lax
from jax.experimental import pallas as pl
from jax.experimental.pallas import tpu as pltpu

D = 1024
D_FF = 2816
HEAD = 64
N_Q = 16
N_KV = 2
GQA = 8
BLK = 128
ROPE_DIM = 16
ROPE_THETA = 500000.0
POOL_GROUP = 256
Q_W = 1024
KV_W = 128
QKV_W = Q_W + 2 * KV_W
IN_W = 4352
UP_SHARD = 1408
EPS = 1e-6
N_CHIPS = 4
N_DEV = 8

LR = 0.001
B1 = 0.9
B2 = 0.999
ADAM_EPS = 1e-08
WD = 0.01
STEP = 10

BF = jnp.bfloat16
F32 = jnp.float32
MESH = pl.DeviceIdType.MESH
VMEM_LIMIT_MB = 56


def _cp(sem, vmem_mb=VMEM_LIMIT_MB):
    return pltpu.CompilerParams(dimension_semantics=sem, vmem_limit_bytes=vmem_mb << 20)


def _full(shape):
    nd = len(shape)
    return pl.BlockSpec(shape, lambda *_: (0,) * nd)


def _sds(shape, dtype):
    return jax.ShapeDtypeStruct(shape, dtype)


def _after(body, n_in, deps):
    nd = len(deps)
    if nd == 0:
        return body

    def ordered(*refs):
        return body(*refs[:n_in], *refs[n_in + nd:])

    return ordered


def _any_specs(deps):
    return [pl.BlockSpec(memory_space=pl.ANY)] * len(deps)


def _nt(a, b):
    return lax.dot_general(a, b, (((1,), (1,)), ((), ())), preferred_element_type=F32)


def _tn(a, b):
    return lax.dot_general(a, b, (((0,), (0,)), ((), ())), preferred_element_type=F32)


def _mm(a, b):
    return jnp.dot(a, b, preferred_element_type=F32)


def _head_sum(v, bd):
    hi = v.astype(BF)
    lo = (v - hi.astype(F32)).astype(BF)
    return _mm(hi, bd) + _mm(lo, bd)


def _rope_tables(pos_ref, invf_ref):
    ang = pos_ref[...].astype(F32) * invf_ref[...]
    cos = jnp.cos(ang)
    sin = jnp.sin(ang)
    lane = lax.broadcasted_iota(jnp.int32, (1, 2 * HEAD), 1) % HEAD
    sa = jnp.where(lane < ROPE_DIM // 2, -sin, 0.0)
    sb = jnp.where(lane < ROPE_DIM // 2, 0.0, jnp.where(lane < ROPE_DIM, sin, 0.0))
    return cos, sa, sb


def _tile_lanes(t, reps):
    return t if reps == 1 else jnp.tile(t, (1, reps))


def _rope(v, cos, sa, sb):
    w = v.shape[1]
    reps = w // (2 * HEAD)
    half = ROPE_DIM // 2
    return (v * _tile_lanes(cos, reps) + pltpu.roll(v, w - half, 1) * _tile_lanes(sa, reps)
            + pltpu.roll(v, half, 1) * _tile_lanes(sb, reps))


def _rope_t(dy, cos, sa, sb):
    w = dy.shape[1]
    reps = w // (2 * HEAD)
    half = ROPE_DIM // 2
    return (dy * _tile_lanes(cos, reps) + pltpu.roll(dy * _tile_lanes(sa, reps), half, 1)
            + pltpu.roll(dy * _tile_lanes(sb, reps), w - half, 1))


def _attn_in_proj(x, g1, w_in, b_gate, tm, deps=()):
    s = x.shape[0]

    def body(x_ref, g_ref, w_ref, b_ref, h_ref, u_ref, qkv_ref, gate_ref):
        xv = x_ref[...]
        r = lax.rsqrt(jnp.mean(xv * xv, axis=-1, keepdims=True) + EPS)
        h = (xv * r * g_ref[...]).astype(BF)
        h_ref[...] = h
        u_ref[...] = _mm(h, w_ref[:, 0:D])
        qkv_ref[...] = _mm(h, w_ref[:, D:D + QKV_W])
        gate_ref[...] = jax.nn.sigmoid(_mm(h, w_ref[:, D + QKV_W:IN_W]) + b_ref[...]).astype(BF)

    row = lambda w: pl.BlockSpec((tm, w), lambda i: (i, 0))
    return pl.pallas_call(
        _after(body, 4, deps), grid=(s // tm,),
        in_specs=[row(D), _full((1, D)), _full((D, IN_W)), _full((1, 2 * D))] + _any_specs(deps),
        out_specs=[row(D), row(D), row(QKV_W), row(2 * D)],
        out_shape=[_sds((s, D), BF), _sds((s, D), F32), _sds((s, QKV_W), F32), _sds((s, 2 * D), BF)],
        compiler_params=_cp(("parallel",)), name="attn_in_proj")(x, g1, w_in, b_gate, *deps)


def _qk_prep(qkv, pos, wq, wk, invf, bd, tm):
    s = qkv.shape[0]

    def body(qkv_ref, pos_ref, wq_ref, wk_ref, invf_ref, bd_ref, qh_ref, kh_ref, vh_ref):
        cos, sa, sb = _rope_tables(pos_ref, invf_ref)
        q = qkv_ref[:, 0:Q_W]
        k = qkv_ref[:, Q_W:Q_W + KV_W]
        v = qkv_ref[:, Q_W + KV_W:QKV_W]
        rq = lax.rsqrt(_head_sum(q * q, bd_ref[...]) * (1.0 / HEAD) + EPS)
        qr = _rope(q * rq * wq_ref[...], cos, sa, sb) * (HEAD ** -0.5)
        rk = lax.rsqrt(_head_sum(k * k, bd_ref[0:KV_W, 0:KV_W]) * (1.0 / HEAD) + EPS)
        kr = _rope(k * rk * wk_ref[...], cos, sa, sb)
        for h in range(N_Q):
            qh_ref[h] = qr[:, HEAD * h:HEAD * (h + 1)].astype(BF)
        for h in range(N_KV):
            kh_ref[h] = kr[:, HEAD * h:HEAD * (h + 1)].astype(BF)
            vh_ref[h] = v[:, HEAD * h:HEAD * (h + 1)].astype(BF)

    heads = lambda n: pl.BlockSpec((n, tm, HEAD), lambda i: (0, i, 0))
    return pl.pallas_call(
        body, grid=(s // tm,),
        in_specs=[pl.BlockSpec((tm, QKV_W), lambda i: (i, 0)), pl.BlockSpec((tm, 1), lambda i: (i, 0)),
                  _full((1, Q_W)), _full((1, KV_W)), _full((1, 2 * HEAD)), _full((Q_W, Q_W))],
        out_specs=[heads(N_Q), heads(N_KV), heads(N_KV)],
        out_shape=[_sds((N_Q, s, HEAD), BF), _sds((N_KV, s, HEAD), BF), _sds((N_KV, s, HEAD), BF)],
        compiler_params=_cp(("parallel",)), name="qk_prep")(qkv, pos, wq, wk, invf, bd)


def _sink_column(sink_ref, kh):
    row_g = lax.broadcasted_iota(jnp.int32, (GQA * BLK, 1), 0) // BLK
    col = jnp.zeros((GQA * BLK, 1), F32)
    for g in range(GQA):
        col = jnp.where(row_g == g, sink_ref[kh * GQA + g], col)
    return col


def _attn_probs(q, k, n, sink_col):
    sc = _nt(q, k)
    qi = lax.broadcasted_iota(jnp.int32, sc.shape, 0) % BLK + BLK
    ki = lax.broadcasted_iota(jnp.int32, sc.shape, 1)
    lo = jnp.where(n > 0, qi - BLK, BLK - 1)
    ok = (ki <= qi) & (ki > lo)
    sc = jnp.where(ok, sc, -jnp.inf)
    m = jnp.maximum(jnp.max(sc, axis=-1, keepdims=True), sink_col)
    p = jnp.exp(sc - m)
    es = jnp.exp(sink_col - m)
    inv = 1.0 / (jnp.sum(p, axis=-1, keepdims=True) + es)
    return p * inv, es * inv


def _attn_fwd(qh, kh, vh, sinks, deps=()):
    s = qh.shape[1]
    nb = s // BLK

    def body(sink_ref, q_ref, kp_ref, kc_ref, vp_ref, vc_ref, o_ref):
        n = pl.program_id(0)
        for khd in range(N_KV):
            q = q_ref[khd * GQA:(khd + 1) * GQA].reshape(GQA * BLK, HEAD)
            k = jnp.concatenate([kp_ref[khd], kc_ref[khd]], axis=0)
            v = jnp.concatenate([vp_ref[khd], vc_ref[khd]], axis=0)
            probs, _ = _attn_probs(q, k, n, _sink_column(sink_ref, khd))
            o = _mm(probs.astype(BF), v)
            for j in range(GQA // 2):
                c0 = khd * GQA * HEAD + 2 * HEAD * j
                o_ref[:, c0:c0 + 2 * HEAD] = jnp.concatenate(
                    [o[2 * j * BLK:(2 * j + 1) * BLK], o[(2 * j + 1) * BLK:(2 * j + 2) * BLK]], axis=1).astype(BF)

    prev = pl.BlockSpec((N_KV, BLK, HEAD), lambda n: (0, jnp.maximum(n - 1, 0), 0))
    cur = pl.BlockSpec((N_KV, BLK, HEAD), lambda n: (0, n, 0))
    return pl.pallas_call(
        _after(body, 6, deps), grid=(nb,),
        in_specs=[pl.BlockSpec(memory_space=pltpu.SMEM),
                  pl.BlockSpec((N_Q, BLK, HEAD), lambda n: (0, n, 0)), prev, cur, prev, cur] + _any_specs(deps),
        out_specs=pl.BlockSpec((BLK, Q_W), lambda n: (n, 0)),
        out_shape=_sds((s, Q_W), BF),
        compiler_params=_cp(("parallel",)), name="attn_fwd")(sinks, qh, kh, kh, vh, vh, *deps)


def _pool_fwd(u, w_pool, pool_scale, ts):
    s = u.shape[0]
    halo = 16

    def body(u_ref, wp_ref, ps_ref, a_ref, pooled_ref, prev):
        g = pl.program_id(0)
        i = pl.program_id(1)

        @pl.when(i == 0)
        def _():
            prev[...] = jnp.zeros_like(prev)

        cur = u_ref[...]
        ext = jnp.concatenate([prev[...], cur], axis=0)
        t = (i * ts + lax.broadcasted_iota(jnp.int32, (ts, 1), 0)).astype(F32)
        for gi in range(4):
            @pl.when(g == gi)
            def _(gi=gi):
                w = 2 << gi
                acc, span = ext, 1
                while span < w:
                    acc = acc + pltpu.roll(acc, span, 0)
                    span *= 2
                inv = 1.0 / jnp.minimum(t + 1.0, float(w))
                pooled = (acc[halo:halo + ts] * inv - cur).astype(BF)
                pooled_ref[...] = pooled
                a_ref[...] = (_mm(pooled, wp_ref[0]) * ps_ref[...]).astype(BF)

        prev[...] = cur[ts - halo:ts]

    col = pl.BlockSpec((ts, POOL_GROUP), lambda g, i: (i, g))
    return pl.pallas_call(
        body, grid=(4, s // ts),
        in_specs=[col, pl.BlockSpec((1, POOL_GROUP, POOL_GROUP), lambda g, i: (g, 0, 0)),
                  pl.BlockSpec((1, POOL_GROUP), lambda g, i: (0, g))],
        out_specs=[col, col],
        out_shape=[_sds((s, D), BF), _sds((s, D), BF)],
        scratch_shapes=[pltpu.VMEM((halo, POOL_GROUP), F32)],
        compiler_params=_cp(("parallel", "arbitrary")), name="pool_fwd")(u, w_pool, pool_scale)


def _mix_out_proj(a, b, gates, x, w_out, g2, tm):
    s = x.shape[0]

    def body(a_ref, b_ref, gate_ref, x_ref, w_ref, g_ref, mix_ref, y_ref, h_ref):
        mix = (gate_ref[:, 0:D].astype(F32) * a_ref[...].astype(F32)
               + gate_ref[:, D:2 * D].astype(F32) * b_ref[...].astype(F32)).astype(BF)
        mix_ref[...] = mix
        y = x_ref[...] + _mm(mix, w_ref[...])
        y_ref[...] = y
        r = lax.rsqrt(jnp.mean(y * y, axis=-1, keepdims=True) + EPS)
        h_ref[...] = (y * r * g_ref[...]).astype(BF)

    row = lambda w: pl.BlockSpec((tm, w), lambda i: (i, 0))
    return pl.pallas_call(
        body, grid=(s // tm,),
        in_specs=[row(D), row(D), row(2 * D), row(D), _full((D, D)), _full((1, D))],
        out_specs=[row(D), row(D), row(D)],
        out_shape=[_sds((s, D), BF), _sds((s, D), F32), _sds((s, D), BF)],
        compiler_params=_cp(("parallel",)), name="mix_out_proj")(a, b, gates, x, w_out, g2)


def _ffn_up(h2, w_up, conv_w, conv_b, tm):
    s = h2.shape[0]

    def body(h_ref, wg_ref, wv_ref, cwg_ref, cwv_ref, cbg_ref, cbv_ref,
             preg_ref, prev_ref, upg_ref, upv_ref, act_ref, halog, halov):
        i = pl.program_id(1)

        @pl.when(i == 0)
        def _():
            halog[...] = jnp.zeros_like(halog)
            halov[...] = jnp.zeros_like(halov)

        h = h_ref[...]

        def conv_half(w_ref, cw_ref, cb_ref, halo, pre_ref, up_ref):
            pre = _mm(h, w_ref[0])
            pre_ref[...] = pre.astype(BF)
            ext = jnp.concatenate([halo[...], pre], axis=0)
            cw = cw_ref[0]
            up = cb_ref[...] + cw[0:1] * pltpu.roll(ext, 2, 0)[8:8 + tm]
            up = up + cw[1:2] * pltpu.roll(ext, 1, 0)[8:8 + tm]
            up = up + cw[2:3] * pre
            halo[...] = pre[tm - 8:tm]
            up_ref[...] = up.astype(BF)
            return up

        gate = conv_half(wg_ref, cwg_ref, cbg_ref, halog, preg_ref, upg_ref)
        val = conv_half(wv_ref, cwv_ref, cbv_ref, halov, prev_ref, upv_ref)
        act_ref[...] = (gate * jax.nn.sigmoid(gate) * val).astype(BF)

    tile = pl.BlockSpec((tm, UP_SHARD), lambda j, i: (i, j))
    wspec = lambda off: pl.BlockSpec((1, D, UP_SHARD), lambda j, i: (j + off, 0, 0))
    cwspec = lambda off: pl.BlockSpec((1, 3, UP_SHARD), lambda j, i: (j + off, 0, 0))
    cbspec = lambda off: pl.BlockSpec((1, UP_SHARD), lambda j, i: (0, j + off))
    half = _sds((s, D_FF), BF)
    return pl.pallas_call(
        body, grid=(2, s // tm),
        in_specs=[pl.BlockSpec((tm, D), lambda j, i: (i, 0)), wspec(0), wspec(2), cwspec(0), cwspec(2),
                  cbspec(0), cbspec(2)],
        out_specs=[tile] * 5, out_shape=[half] * 5,
        scratch_shapes=[pltpu.VMEM((8, UP_SHARD), F32), pltpu.VMEM((8, UP_SHARD), F32)],
        compiler_params=_cp(("parallel", "arbitrary")), name="ffn_up")(
            h2, w_up, w_up, conv_w, conv_w, conv_b, conv_b)


def _ffn_down_loss(act, w_down, y1, tgt, tm):
    s = y1.shape[0]

    def body(act_ref, w_ref, y_ref, t_ref, dy_ref, dyb_ref, loss_ref):
        @pl.when(pl.program_id(0) == 0)
        def _():
            loss_ref[...] = jnp.zeros_like(loss_ref)

        e = y_ref[...] + _mm(act_ref[...], w_ref[...]) - t_ref[...]
        dy = e * (1.0 / D)
        dy_ref[...] = dy
        dyb_ref[...] = dy.astype(BF)
        e2 = (e * e).reshape(tm // 8, 8, D).sum(axis=0)
        part = e2[:, 0:128]
        for j in range(1, D // 128):
            part = part + e2[:, 128 * j:128 * (j + 1)]
        loss_ref[...] += part

    row = lambda w: pl.BlockSpec((tm, w), lambda i: (i, 0))
    return pl.pallas_call(
        body, grid=(s // tm,),
        in_specs=[row(D_FF), _full((D_FF, D)), row(D), row(D)],
        out_specs=[row(D), row(D), _full((8, 128))],
        out_shape=[_sds((s, D), F32), _sds((s, D), BF), _sds((8, 128), F32)],
        compiler_params=_cp(("arbitrary",)), name="ffn_down_loss")(act, w_down, y1, tgt)


def _grad_matmul(a, b, tn, tk, name, lead=None, prev=None, lead_off=0):
    s, m = a.shape
    n = b.shape[1]
    nj = n // tn

    def body(*refs):
        a_ref, b_ref = refs[0], refs[1]
        o_ref = refs[-1]
        acc = _tn(a_ref[...], b_ref[...])
        acc = acc if lead is None else acc[None]

        @pl.when(pl.program_id(1) == 0)
        def _():
            o_ref[...] = acc

        @pl.when(pl.program_id(1) > 0)
        def _():
            o_ref[...] += acc

    in_specs = [pl.BlockSpec((tk, m), lambda j, k: (k, 0)), pl.BlockSpec((tk, tn), lambda j, k: (k, j))]
    args = [a, b]
    aliases = {}
    if lead is None:
        out_spec = pl.BlockSpec((m, tn), lambda j, k: (0, j))
        out_shape = _sds((m, n), F32)
    else:
        out_spec = pl.BlockSpec((1, m, tn), lambda j, k: (j + lead_off, 0, 0))
        out_shape = _sds((lead, m, tn), F32)
        if prev is not None:
            in_specs.append(pl.BlockSpec(memory_space=pl.ANY))
            args.append(prev)
            aliases = {2: 0}
    return pl.pallas_call(
        body, grid=(nj, s // tk), in_specs=in_specs, out_specs=out_spec, out_shape=out_shape,
        input_output_aliases=aliases,
        compiler_params=_cp(("parallel", "arbitrary")), name=name)(*args)


def _ffn_act_bwd(dyb, w_down, up_g, up_v, pre_g, pre_v, conv_w, tm):
    s = dyb.shape[0]
    nt = s // tm

    def body(dy_ref, wd_ref, upg_ref, upv_ref, preg_ref, prev_ref, cwg_ref, cwv_ref,
             dpg_ref, dpv_ref, dcwg_ref, dcwv_ref, dcbg_ref, dcbv_ref, nxg, nxv):
        i = pl.program_id(1)

        @pl.when(i == 0)
        def _():
            nxg[...] = jnp.zeros_like(nxg)
            nxv[...] = jnp.zeros_like(nxv)
            dcwg_ref[...] = jnp.zeros_like(dcwg_ref)
            dcwv_ref[...] = jnp.zeros_like(dcwv_ref)
            dcbg_ref[...] = jnp.zeros_like(dcbg_ref)
            dcbv_ref[...] = jnp.zeros_like(dcbv_ref)

        dact = _nt(dy_ref[...], wd_ref[...])
        g = upg_ref[...].astype(F32)
        v = upv_ref[...].astype(F32)
        sg = jax.nn.sigmoid(g)
        d_v = dact * (g * sg)
        d_g = dact * v * (sg * (1.0 + g * (1.0 - sg)))

        def conv_bwd(d_up, nx, pre_ref, cw_ref, dp_ref, dcw_ref, dcb_ref):
            ext = jnp.concatenate([d_up, nx[...]], axis=0)
            s1 = pltpu.roll(ext, tm + 8 - 1, 0)[0:tm]
            s2 = pltpu.roll(ext, tm + 8 - 2, 0)[0:tm]
            cw = cw_ref[0]
            dp_ref[...] = (cw[2:3] * d_up + cw[1:2] * s1 + cw[0:1] * s2).astype(BF)
            nx[...] = d_up[0:8]
            pre = pre_ref[...].astype(F32)
            dcw_ref[0, 0:1, :] += jnp.sum(s2 * pre, axis=0, keepdims=True)
            dcw_ref[0, 1:2, :] += jnp.sum(s1 * pre, axis=0, keepdims=True)
            dcw_ref[0, 2:3, :] += jnp.sum(d_up * pre, axis=0, keepdims=True)
            dcb_ref[...] += jnp.sum(d_up, axis=0, keepdims=True)

        conv_bwd(d_g, nxg, preg_ref, cwg_ref, dpg_ref, dcwg_ref, dcbg_ref)
        conv_bwd(d_v, nxv, prev_ref, cwv_ref, dpv_ref, dcwv_ref, dcbv_ref)

    tile = pl.BlockSpec((tm, UP_SHARD), lambda j, i: (nt - 1 - i, j))
    cwspec = lambda off: pl.BlockSpec((1, 3, UP_SHARD), lambda j, i: (j + off, 0, 0))
    acc_cw = pl.BlockSpec((1, 3, UP_SHARD), lambda j, i: (j, 0, 0))
    acc_cb = pl.BlockSpec((1, UP_SHARD), lambda j, i: (0, j))
    buf = pltpu.VMEM((8, UP_SHARD), F32)
    return pl.pallas_call(
        body, grid=(2, nt),
        in_specs=[pl.BlockSpec((tm, D), lambda j, i: (nt - 1 - i, 0)),
                  pl.BlockSpec((UP_SHARD, D), lambda j, i: (j, 0)),
                  tile, tile, tile, tile, cwspec(0), cwspec(2)],
        out_specs=[tile, tile, acc_cw, acc_cw, acc_cb, acc_cb],
        out_shape=[_sds((s, D_FF), BF), _sds((s, D_FF), BF), _sds((2, 3, UP_SHARD), F32),
                   _sds((2, 3, UP_SHARD), F32), _sds((1, D_FF), F32), _sds((1, D_FF), F32)],
        scratch_shapes=[buf, buf],
        compiler_params=_cp(("parallel", "arbitrary")), name="ffn_act_bwd")(
            dyb, w_down, up_g, up_v, pre_g, pre_v, conv_w, conv_w)


def _rms_bwd(dh, y, g):
    r = lax.rsqrt(jnp.mean(y * y, axis=-1, keepdims=True) + EPS)
    n = y * r
    dn = dh * g
    return r * (dn - n * jnp.mean(dn * n, axis=-1, keepdims=True)), dh * n


def _ffn_up_bwd(dp_g, dp_v, w_up, y1, dy2, g2, tm, deps=()):
    s = y1.shape[0]

    def body(dg_ref, dv_ref, w_ref, y_ref, dy2_ref, g_ref, dy1_ref, dgn_ref):
        @pl.when(pl.program_id(0) == 0)
        def _():
            dgn_ref[...] = jnp.zeros_like(dgn_ref)

        dh = _nt(dg_ref[:, 0:UP_SHARD], w_ref[0])
        dh = dh + _nt(dg_ref[:, UP_SHARD:D_FF], w_ref[1])
        dh = dh + _nt(dv_ref[:, 0:UP_SHARD], w_ref[2])
        dh = dh + _nt(dv_ref[:, UP_SHARD:D_FF], w_ref[3])
        dy, dgn = _rms_bwd(dh, y_ref[...], g_ref[...])
        dy1_ref[...] = dy2_ref[...] + dy
        dgn_ref[...] += jnp.sum(dgn, axis=0, keepdims=True)

    row = lambda w: pl.BlockSpec((tm, w), lambda i: (i, 0))
    return pl.pallas_call(
        _after(body, 6, deps), grid=(s // tm,),
        in_specs=[row(D_FF), row(D_FF), _full((4, D, UP_SHARD)), row(D), row(D), _full((1, D))] + _any_specs(deps),
        out_specs=[row(D), _full((1, D))],
        out_shape=[_sds((s, D), F32), _sds((1, D), F32)],
        compiler_params=_cp(("arbitrary",)), name="ffn_up_bwd")(dp_g, dp_v, w_up, y1, dy2, g2, *deps)


def _out_proj_bwd(dy1, w_out, a, b, gates, mix, tm, deps=()):
    s = dy1.shape[0]

    def body(dy_ref, w_ref, a_ref, b_ref, gate_ref, mix_ref, da_ref, db_ref, dzg_ref, dbg_ref, dw_ref):
        @pl.when(pl.program_id(0) == 0)
        def _():
            dbg_ref[...] = jnp.zeros_like(dbg_ref)
            dw_ref[...] = jnp.zeros_like(dw_ref)

        dyb = dy_ref[...].astype(BF)
        dmix = _nt(dyb, w_ref[...])
        gp = gate_ref[:, 0:D].astype(F32)
        ga = gate_ref[:, D:2 * D].astype(F32)
        da_ref[...] = (dmix * gp).astype(BF)
        db_ref[...] = (dmix * ga).astype(BF)
        dzp = dmix * a_ref[...].astype(F32) * (gp * (1.0 - gp))
        dza = dmix * b_ref[...].astype(F32) * (ga * (1.0 - ga))
        dzg_ref[:, 0:D] = dzp.astype(BF)
        dzg_ref[:, D:2 * D] = dza.astype(BF)
        dbg_ref[:, 0:D] += jnp.sum(dzp, axis=0, keepdims=True)
        dbg_ref[:, D:2 * D] += jnp.sum(dza, axis=0, keepdims=True)
        dw_ref[...] += _tn(mix_ref[...], dyb)

    row = lambda w: pl.BlockSpec((tm, w), lambda i: (i, 0))
    return pl.pallas_call(
        _after(body, 6, deps), grid=(s // tm,),
        in_specs=[row(D), _full((D, D)), row(D), row(D), row(2 * D), row(D)] + _any_specs(deps),
        out_specs=[row(D), row(D), row(2 * D), _full((1, 2 * D)), _full((D, D))],
        out_shape=[_sds((s, D), BF), _sds((s, D), BF), _sds((s, 2 * D), BF), _sds((1, 2 * D), F32),
                   _sds((D, D), F32)],
        compiler_params=_cp(("arbitrary",)), name="out_proj_bwd")(dy1, w_out, a, b, gates, mix, *deps)


def _pool_bwd(da, pooled, w_pool, pool_scale, ts):
    s = da.shape[0]
    nt = s // ts
    halo = 16

    def body(da_ref, pooled_ref, wp_ref, ps_ref, du_ref, dwp_ref, dps_ref, nxt):
        g = pl.program_id(0)
        i = pl.program_id(1)
        ti = nt - 1 - i

        @pl.when(i == 0)
        def _():
            nxt[...] = jnp.zeros_like(nxt)
            dwp_ref[...] = jnp.zeros_like(dwp_ref)
            dps_ref[...] = jnp.zeros_like(dps_ref)

        pooled = pooled_ref[...]
        dav = da_ref[...].astype(F32)
        dps_ref[...] += jnp.sum(dav * _mm(pooled, wp_ref[0]), axis=0, keepdims=True)
        dm = (dav * ps_ref[...]).astype(BF)
        dwp_ref[0] += _tn(pooled, dm)
        dpool = _nt(dm, wp_ref[0])
        t = (ti * ts + lax.broadcasted_iota(jnp.int32, (ts, 1), 0)).astype(F32)
        for gi in range(4):
            @pl.when(g == gi)
            def _(gi=gi):
                w = 2 << gi
                e = dpool * (1.0 / jnp.minimum(t + 1.0, float(w)))
                acc, span = jnp.concatenate([e, nxt[...]], axis=0), 1
                while span < w:
                    acc = acc + pltpu.roll(acc, ts + halo - span, 0)
                    span *= 2
                du_ref[...] = (acc[0:ts] - dpool).astype(BF)
                nxt[...] = e[0:halo]

    col = pl.BlockSpec((ts, POOL_GROUP), lambda g, i: (nt - 1 - i, g))
    return pl.pallas_call(
        body, grid=(4, nt),
        in_specs=[col, col, pl.BlockSpec((1, POOL_GROUP, POOL_GROUP), lambda g, i: (g, 0, 0)),
                  pl.BlockSpec((1, POOL_GROUP), lambda g, i: (0, g))],
        out_specs=[col, pl.BlockSpec((1, POOL_GROUP, POOL_GROUP), lambda g, i: (g, 0, 0)),
                   pl.BlockSpec((1, POOL_GROUP), lambda g, i: (0, g))],
        out_shape=[_sds((s, D), BF), _sds((4, POOL_GROUP, POOL_GROUP), F32), _sds((1, D), F32)],
        scratch_shapes=[pltpu.VMEM((halo, POOL_GROUP), F32)],
        compiler_params=_cp(("parallel", "arbitrary")), name="pool_bwd")(da, pooled, w_pool, pool_scale)


def _attn_bwd(qh, kh, vh, sinks, db, deps=()):
    s = qh.shape[1]
    nb = s // BLK

    def body(sink_ref, q_ref, kp_ref, kc_ref, vp_ref, vc_ref, do_ref,
             dq_ref, dk_ref, dv_ref, dsink_ref, ck, cv):
        n = pl.program_id(0)

        @pl.when(n == 0)
        def _():
            ck[...] = jnp.zeros_like(ck)
            cv[...] = jnp.zeros_like(cv)
            dsink_ref[...] = jnp.zeros_like(dsink_ref)

        @pl.when(n < nb)
        def _():
            dov = do_ref[...]
            for khd in range(N_KV):
                q = q_ref[khd * GQA:(khd + 1) * GQA].reshape(GQA * BLK, HEAD)
                k = jnp.concatenate([kp_ref[khd], kc_ref[khd]], axis=0)
                v = jnp.concatenate([vp_ref[khd], vc_ref[khd]], axis=0)
                c0 = khd * GQA * HEAD
                do = jnp.concatenate([dov[:, c0 + HEAD * g:c0 + HEAD * (g + 1)] for g in range(GQA)],
                                     axis=0).astype(BF)
                probs, psink = _attn_probs(q, k, n, _sink_column(sink_ref, khd))
                dp = _nt(do, v)
                delta = jnp.sum(probs * dp, axis=-1, keepdims=True)
                ds = (probs * (dp - delta)).astype(BF)
                dq_ref[khd * GQA:(khd + 1) * GQA] = _mm(ds, k).reshape(GQA, BLK, HEAD)
                dk = _tn(ds, q)
                dv = _tn(probs.astype(BF), do)
                dk_ref[khd] = ck[khd] + dk[0:BLK]
                dv_ref[khd] = cv[khd] + dv[0:BLK]
                ck[khd] = dk[BLK:2 * BLK]
                cv[khd] = dv[BLK:2 * BLK]
                dsk = psink * delta
                lane = lax.broadcasted_iota(jnp.int32, (1, 128), 1)
                acc = jnp.zeros((1, 128), F32)
                for g in range(GQA):
                    acc = acc - jnp.where(lane == g, jnp.sum(dsk[g * BLK:(g + 1) * BLK], axis=0, keepdims=True), 0.0)
                dsink_ref[khd] += acc

        @pl.when(n == nb)
        def _():
            dk_ref[...] = ck[...]
            dv_ref[...] = cv[...]

    last = nb - 1
    prev = pl.BlockSpec((N_KV, BLK, HEAD), lambda n: (0, jnp.maximum(jnp.minimum(n, last) - 1, 0), 0))
    cur = pl.BlockSpec((N_KV, BLK, HEAD), lambda n: (0, jnp.minimum(n, last), 0))
    kv_out = pl.BlockSpec((N_KV, BLK, HEAD), lambda n: (0, jnp.maximum(n - 1, 0), 0))
    return pl.pallas_call(
        _after(body, 7, deps), grid=(nb + 1,),
        in_specs=[pl.BlockSpec(memory_space=pltpu.SMEM),
                  pl.BlockSpec((N_Q, BLK, HEAD), lambda n: (0, jnp.minimum(n, last), 0)),
                  prev, cur, prev, cur,
                  pl.BlockSpec((BLK, Q_W), lambda n: (jnp.minimum(n, last), 0))] + _any_specs(deps),
        out_specs=[pl.BlockSpec((N_Q, BLK, HEAD), lambda n: (0, jnp.minimum(n, last), 0)), kv_out, kv_out,
                   _full((N_KV, 1, 128))],
        out_shape=[_sds((N_Q, s, HEAD), F32), _sds((N_KV, s, HEAD), F32), _sds((N_KV, s, HEAD), F32),
                   _sds((N_KV, 1, 128), F32)],
        scratch_shapes=[pltpu.VMEM((N_KV, BLK, HEAD), F32), pltpu.VMEM((N_KV, BLK, HEAD), F32)],
        compiler_params=_cp(("arbitrary",)), name="attn_bwd")(sinks, qh, kh, kh, vh, vh, db, *deps)


def _qk_prep_bwd(dqh, dkh, dvh, qkv, pos, wq, wk, invf, bd, tm):
    s = qkv.shape[0]

    def fold_heads(row):
        out = row[:, 0:HEAD]
        for h in range(1, row.shape[1] // HEAD):
            out = out + row[:, HEAD * h:HEAD * (h + 1)]
        return out

    def body(dq_ref, dk_ref, dv_ref, qkv_ref, pos_ref, wq_ref, wk_ref, invf_ref, bd_ref,
             dz_ref, dwq_ref, dwk_ref):
        @pl.when(pl.program_id(0) == 0)
        def _():
            dwq_ref[...] = jnp.zeros_like(dwq_ref)
            dwk_ref[...] = jnp.zeros_like(dwk_ref)

        cos, sa, sb = _rope_tables(pos_ref, invf_ref)

        def norm_rope_bwd(dy, xin, w, bdm):
            dn = _rope_t(dy, cos, sa, sb)
            r = lax.rsqrt(_head_sum(xin * xin, bdm) * (1.0 / HEAD) + EPS)
            nh = xin * r
            gw = dn * w
            dx = r * (gw - nh * (_head_sum(gw * nh, bdm) * (1.0 / HEAD)))
            return dx, fold_heads(jnp.sum(dn * nh, axis=0, keepdims=True))

        dq = jnp.concatenate([dq_ref[h] for h in range(N_Q)], axis=1) * (HEAD ** -0.5)
        dk = jnp.concatenate([dk_ref[h] for h in range(N_KV)], axis=1)
        dxq, dwq = norm_rope_bwd(dq, qkv_ref[:, 0:Q_W], wq_ref[...], bd_ref[...])
        dxk, dwk = norm_rope_bwd(dk, qkv_ref[:, Q_W:Q_W + KV_W], wk_ref[...], bd_ref[0:KV_W, 0:KV_W])
        dz_ref[:, 0:Q_W] = dxq.astype(BF)
        dz_ref[:, Q_W:Q_W + KV_W] = dxk.astype(BF)
        dz_ref[:, Q_W + KV_W:QKV_W] = jnp.concatenate([dv_ref[h] for h in range(N_KV)], axis=1).astype(BF)
        dwq_ref[...] += dwq
        dwk_ref[...] += dwk

    heads = lambda n: pl.BlockSpec((n, tm, HEAD), lambda i: (0, i, 0))
    return pl.pallas_call(
        body, grid=(s // tm,),
        in_specs=[heads(N_Q), heads(N_KV), heads(N_KV), pl.BlockSpec((tm, QKV_W), lambda i: (i, 0)),
                  pl.BlockSpec((tm, 1), lambda i: (i, 0)), _full((1, Q_W)), _full((1, KV_W)),
                  _full((1, 2 * HEAD)), _full((Q_W, Q_W))],
        out_specs=[pl.BlockSpec((tm, QKV_W), lambda i: (i, 0)), _full((1, HEAD)), _full((1, HEAD))],
        out_shape=[_sds((s, QKV_W), BF), _sds((1, HEAD), F32), _sds((1, HEAD), F32)],
        compiler_params=_cp(("arbitrary",)), name="qk_prep_bwd")(dqh, dkh, dvh, qkv, pos, wq, wk, invf, bd)


def _in_proj_bwd(du, dzq, dzg, w_in, x, g1, dy1, tm, deps=()):
    s = x.shape[0]

    def body(du_ref, dzq_ref, dzg_ref, w_ref, x_ref, g_ref, dy_ref, gx_ref, dgn_ref):
        @pl.when(pl.program_id(0) == 0)
        def _():
            dgn_ref[...] = jnp.zeros_like(dgn_ref)

        dh = _nt(du_ref[...], w_ref[:, 0:D])
        dh = dh + _nt(dzq_ref[...], w_ref[:, D:D + QKV_W])
        dh = dh + _nt(dzg_ref[...], w_ref[:, D + QKV_W:IN_W])
        dx, dgn = _rms_bwd(dh, x_ref[...], g_ref[...])
        gx_ref[...] = dy_ref[...] + dx
        dgn_ref[...] += jnp.sum(dgn, axis=0, keepdims=True)

    row = lambda w: pl.BlockSpec((tm, w), lambda i: (i, 0))
    return pl.pallas_call(
        _after(body, 7, deps), grid=(s // tm,),
        in_specs=[row(D), row(QKV_W), row(2 * D), _full((D, IN_W)), row(D), _full((1, D)), row(D)] + _any_specs(deps),
        out_specs=[row(D), _full((1, D))],
        out_shape=[_sds((s, D), F32), _sds((1, D), F32)],
        compiler_params=_cp(("arbitrary",)), name="in_proj_bwd")(du, dzq, dzg, w_in, x, g1, dy1, *deps)


def _adamw(w, g, m, v, tr, name, deps=()):
    r, c = w.shape

    def body(w_ref, g_ref, m_ref, v_ref, go_ref, d_ref, mo_ref, vo_ref):
        gv = g_ref[...]
        mn = B1 * m_ref[...] + (1.0 - B1) * gv
        vn = B2 * v_ref[...] + (1.0 - B2) * (gv * gv)
        m_hat = mn / (1.0 - B1 ** STEP)
        v_hat = vn / (1.0 - B2 ** STEP)
        go_ref[...] = gv
        d_ref[...] = -LR * (m_hat / (jnp.sqrt(v_hat) + ADAM_EPS) + WD * w_ref[...])
        mo_ref[...] = mn
        vo_ref[...] = vn

    blk = pl.BlockSpec((tr, c), lambda i: (i, 0))
    return pl.pallas_call(
        _after(body, 4, deps), grid=(r // tr,), in_specs=[blk] * 4 + _any_specs(deps), out_specs=[blk] * 4,
        out_shape=[_sds((r, c), F32)] * 4, compiler_params=_cp(("parallel",)), name=name)(w, g, m, v, *deps)


def _place():
    x, y, c = lax.axis_index("x"), lax.axis_index("y"), lax.axis_index("c")
    chips = [(1 - x, y), (x, 1 - y), (1 - x, 1 - y)]
    return x, y, c, chips


def _rows(ref, lead, h, rh):
    sl = pl.ds(pl.multiple_of(h * rh, 16), rh)
    return ref.at[sl, :] if lead is None else ref.at[lead, sl, :]


def _all_gather_weights(halved, whole):
    nh, nw = len(halved), len(whole)
    na = nh + nw
    arrays = list(halved) + list(whole)
    out_dtypes = [BF] * nh + [a.dtype for a in whole]
    cast_rows = 128

    def body(*refs):
        ins, outs = refs[:na], refs[na:2 * na]
        raw, stage = refs[2 * na:3 * na], refs[3 * na:3 * na + nh]
        ici_send, ici_recv, fwd_send, fwd_recv, in_sem, loc_sem = refs[3 * na + nh:]
        x, y, c, chips = _place()
        me = 2 * x + y
        sibling = (x, y, 1 - c)
        loads = [pltpu.make_async_copy(ins[a], raw[a], in_sem.at[a]) for a in range(na)]
        for cp in loads:
            cp.start()

        def ici(a, j, src_chip, src=None):
            if a < nh:
                rh = arrays[a].shape[0] // 2
                dst = _rows(outs[a], src_chip, c, rh)
                src = dst if src is None else _rows(src, None, c, rh)
            else:
                dst = outs[a].at[src_chip]
                src = dst if src is None else src
            return pltpu.make_async_remote_copy(
                src_ref=src, dst_ref=dst, send_sem=ici_send.at[3 * a + j], recv_sem=ici_recv.at[3 * a + j],
                device_id=(*chips[j], c), device_id_type=MESH)

        def fwd(a, j, half):
            rh = arrays[a].shape[0] // 2
            kj = 2 * chips[j][0] + chips[j][1]
            blk = _rows(outs[a], kj, half, rh)
            return pltpu.make_async_remote_copy(
                src_ref=blk, dst_ref=blk, send_sem=fwd_send.at[3 * a + j], recv_sem=fwd_recv.at[3 * a + j],
                device_id=sibling, device_id_type=MESH)

        local, sends = [], []
        for a in range(na):
            loads[a].wait()
            if a < nh:
                r = arrays[a].shape[0]
                for r0 in range(0, r, cast_rows):
                    r1 = min(r0 + cast_rows, r)
                    stage[a][r0:r1, :] = raw[a][r0:r1, :].astype(BF)
                own = stage[a]
            else:
                own = raw[a]
            cp = pltpu.make_async_copy(own, outs[a].at[me], loc_sem.at[a])
            cp.start()
            local.append(cp)
            for j in range(3):
                cp = ici(a, j, me, src=own)
                cp.start()
                sends.append(cp)
        passed = []
        for a in range(na):
            for j in range(3):
                kj = 2 * chips[j][0] + chips[j][1]
                ici(a, j, kj).wait_recv()
                if a < nh:
                    cp = fwd(a, j, c)
                    cp.start()
                    passed.append(cp)
        for a in range(nh):
            for j in range(3):
                fwd(a, j, 1 - c).wait_recv()
        for cp in sends + passed:
            cp.wait_send()
        for cp in local:
            cp.wait()

    any_spec = pl.BlockSpec(memory_space=pl.ANY)
    return pl.pallas_call(
        body, in_specs=[any_spec] * na, out_specs=[any_spec] * na,
        out_shape=[_sds((N_CHIPS,) + a.shape, dt) for a, dt in zip(arrays, out_dtypes)],
        scratch_shapes=[pltpu.VMEM(a.shape, a.dtype) for a in arrays] + [pltpu.VMEM(a.shape, BF) for a in halved]
        + [pltpu.SemaphoreType.DMA((3 * na,)), pltpu.SemaphoreType.DMA((3 * na,)),
           pltpu.SemaphoreType.DMA((3 * nh,)), pltpu.SemaphoreType.DMA((3 * nh,)),
           pltpu.SemaphoreType.DMA((na,)), pltpu.SemaphoreType.DMA((na,))],
        compiler_params=pltpu.CompilerParams(vmem_limit_bytes=VMEM_LIMIT_MB << 20),
        name="all_gather_weights")(*arrays)


def _sibling_halves(grads):
    na = len(grads)

    def body(*refs):
        ins, outs = refs[:na], refs[na:2 * na]
        send_sem, recv_sem = refs[2 * na:]
        x, y, c, _ = _place()
        copies = []
        for a in range(na):
            rh = grads[a].shape[1] // 2
            src = ins[a].at[:, pl.ds(pl.multiple_of((1 - c) * rh, 8), rh), :]
            copies.append(pltpu.make_async_remote_copy(
                src_ref=src, dst_ref=outs[a], send_sem=send_sem.at[a], recv_sem=recv_sem.at[a],
                device_id=(x, y, 1 - c), device_id_type=MESH))
        for cp in copies:
            cp.start()
        for cp in copies:
            cp.wait()

    any_spec = pl.BlockSpec(memory_space=pl.ANY)
    return pl.pallas_call(
        body, in_specs=[any_spec] * na, out_specs=[any_spec] * na,
        out_shape=[_sds((N_CHIPS, g.shape[1] // 2, g.shape[2]), F32) for g in grads],
        scratch_shapes=[pltpu.SemaphoreType.DMA((na,)), pltpu.SemaphoreType.DMA((na,))],
        name="sibling_halves")(*grads)


def _pair_sum(g, recv, c, tr, name):
    _, r, cols = g.shape
    rh = r // 2
    nr = rh // tr

    def body(c_ref, g_ref, r_ref, o_ref):
        o_ref[...] = (g_ref[...] + r_ref[...]).astype(BF)

    grid_spec = pltpu.PrefetchScalarGridSpec(
        num_scalar_prefetch=1, grid=(N_CHIPS, nr),
        in_specs=[pl.BlockSpec((1, tr, cols), lambda k, i, c_ref: (k, c_ref[0] * nr + i, 0)),
                  pl.BlockSpec((1, tr, cols), lambda k, i, c_ref: (k, i, 0))],
        out_specs=pl.BlockSpec((1, tr, cols), lambda k, i, c_ref: (k, i, 0)))
    return pl.pallas_call(
        body, grid_spec=grid_spec, out_shape=_sds((N_CHIPS, rh, cols), BF),
        compiler_params=_cp(("parallel", "parallel")), name=name)(c, g, recv)


def _chip_exchange(halves, small):
    na = len(halves)
    srows = small.shape[0]

    def body(*refs):
        ins, small_ref = refs[:na], refs[na]
        outs, small_out = refs[na + 1:2 * na + 1], refs[2 * na + 1]
        send_sem, recv_sem, s_send, s_recv = refs[2 * na + 2:]
        x, y, c, chips = _place()
        me = 4 * x + 2 * y + c
        copies = []
        for a in range(na):
            for j in range(3):
                kj = 2 * chips[j][0] + chips[j][1]
                copies.append(pltpu.make_async_remote_copy(
                    src_ref=ins[a].at[kj], dst_ref=outs[a].at[j],
                    send_sem=send_sem.at[3 * a + j], recv_sem=recv_sem.at[3 * a + j],
                    device_id=(*chips[j], c), device_id_type=MESH))
        for r in range(1, N_DEV):
            peer = (x ^ (r >> 2), y ^ ((r >> 1) & 1), c ^ (r & 1))
            copies.append(pltpu.make_async_remote_copy(
                src_ref=small_ref, dst_ref=small_out.at[me],
                send_sem=s_send.at[r - 1], recv_sem=s_recv.at[r - 1], device_id=peer, device_id_type=MESH))
        for cp in copies:
            cp.start()
        small_out[pl.ds(me, 1)] = small_ref[...][None]
        for cp in copies:
            cp.wait()

    any_spec = pl.BlockSpec(memory_space=pl.ANY)
    vmem = pl.BlockSpec(memory_space=pltpu.VMEM)
    return pl.pallas_call(
        body, in_specs=[any_spec] * na + [vmem], out_specs=[any_spec] * na + [vmem],
        out_shape=[_sds((3,) + h.shape[1:], h.dtype) for h in halves] + [_sds((N_DEV, srows, 128), F32)],
        scratch_shapes=[pltpu.SemaphoreType.DMA((3 * na,)), pltpu.SemaphoreType.DMA((3 * na,)),
                        pltpu.SemaphoreType.DMA((N_DEV - 1,)), pltpu.SemaphoreType.DMA((N_DEV - 1,))],
        name="chip_exchange")(*halves, small)


def _chip_sum(g, sib, recv, place, tr, name):
    _, r, cols = g.shape
    rh = r // 2
    nr = rh // tr

    def body(p_ref, g_ref, s_ref, r0_ref, r1_ref, r2_ref, o_ref):
        own = g_ref[0] + s_ref[0]
        o_ref[...] = ((own + r0_ref[0].astype(F32)) + r1_ref[0].astype(F32)) + r2_ref[0].astype(F32)

    rspec = lambda j: pl.BlockSpec((1, tr, cols), lambda i, p: (j, i, 0))
    grid_spec = pltpu.PrefetchScalarGridSpec(
        num_scalar_prefetch=1, grid=(nr,),
        in_specs=[pl.BlockSpec((1, tr, cols), lambda i, p: (p[0], p[1] * nr + i, 0)),
                  pl.BlockSpec((1, tr, cols), lambda i, p: (p[0], i, 0)), rspec(0), rspec(1), rspec(2)],
        out_specs=pl.BlockSpec((tr, cols), lambda i, p: (p[1] * nr + i, 0)))
    return pl.pallas_call(
        body, grid_spec=grid_spec, out_shape=_sds((r, cols), F32),
        compiler_params=_cp(("parallel",)), name=name)(place, g, sib, recv, recv, recv)


def _sibling_exchange(shards):
    na = len(shards)

    def body(*refs):
        ins, outs = refs[:na], refs[na:2 * na]
        send_sem, recv_sem = refs[2 * na:]
        x, y, c, _ = _place()
        for a in range(na):
            rh = shards[a].shape[0] // 2
            pltpu.make_async_remote_copy(
                src_ref=_rows(ins[a], None, c, rh), dst_ref=_rows(outs[a], None, c, rh),
                send_sem=send_sem.at[a], recv_sem=recv_sem.at[a],
                device_id=(x, y, 1 - c), device_id_type=MESH).start()
        for a in range(na):
            rh = shards[a].shape[0] // 2
            pltpu.make_async_remote_copy(
                src_ref=_rows(ins[a], None, c, rh), dst_ref=_rows(outs[a], None, 1 - c, rh),
                send_sem=send_sem.at[a], recv_sem=recv_sem.at[a],
                device_id=(x, y, 1 - c), device_id_type=MESH).wait()

    any_spec = pl.BlockSpec(memory_space=pl.ANY)
    return pl.pallas_call(
        body, in_specs=[any_spec] * na, out_specs=[any_spec] * na,
        out_shape=[_sds(h.shape, F32) for h in shards],
        input_output_aliases={a: a for a in range(na)},
        scratch_shapes=[pltpu.SemaphoreType.DMA((na,)), pltpu.SemaphoreType.DMA((na,))],
        name="sibling_exchange")(*shards)


def _device_sum(stack, deps=()):
    _, rows, _ = stack.shape

    def body(s_ref, o_ref):
        acc = s_ref[0]
        for d in range(1, N_DEV):
            acc = acc + s_ref[d]
        o_ref[...] = acc

    vmem = pl.BlockSpec(memory_space=pltpu.VMEM)
    return pl.pallas_call(_after(body, 1, deps), in_specs=[vmem] + _any_specs(deps), out_specs=vmem,
                          out_shape=_sds((rows, 128), F32), name="device_sum")(stack, *deps)


_HBM = pl.BlockSpec(memory_space=pltpu.HBM)
_SEM = pl.BlockSpec(memory_space=pltpu.SEMAPHORE)
_EFFECT = pltpu.SideEffectType.DATAFLOW_SIDE_EFFECTING


def _remote(src, dst, ssem, rsem, k, device):
    return pltpu.make_async_remote_copy(src_ref=src, dst_ref=dst, send_sem=ssem.at[k], recv_sem=rsem.at[k],
                                        device_id=device, device_id_type=MESH)


def _split_start(name, bufs, plan, n):
    nb = len(bufs)

    def body(*refs):
        sends, _ = plan(refs[:nb], refs[nb], refs[nb + 1])
        for cp in sends:
            cp.start()
        refs[-1][...] = jnp.zeros_like(refs[-1])

    res = pl.pallas_call(
        body, name=name,
        out_shape=(pltpu.SemaphoreType.DMA((n,)), pltpu.SemaphoreType.DMA((n,)))
        + tuple(pltpu.HBM(b.shape, b.dtype) for b in bufs) + (_sds((8, 128), F32),),
        in_specs=[_HBM] * nb,
        out_specs=(_SEM, _SEM) + (_HBM,) * nb + (pl.BlockSpec(memory_space=pltpu.VMEM),),
        input_output_aliases={i: i + 2 for i in range(nb)},
        compiler_params=pltpu.CompilerParams(has_side_effects=_EFFECT),
    )(*[pltpu.with_memory_space_constraint(b, pltpu.HBM) for b in bufs])
    return res[0], res[1], list(res[2:2 + nb]), res[2 + nb]


def _split_wait(name, send_sem, recv_sem, bufs, plan, after):
    nb = len(bufs)

    def body(*refs):
        sends, arrivals = plan(refs[:nb], refs[nb], refs[nb + 1])
        for cp in sends:
            cp.wait_send()
        for cp in arrivals:
            cp.wait_recv()

    res = pl.pallas_call(
        body, name=name, out_shape=tuple(pltpu.HBM(b.shape, b.dtype) for b in bufs),
        in_specs=[_HBM] * nb + [_SEM, _SEM, pl.BlockSpec(memory_space=pl.ANY)],
        out_specs=(_HBM,) * nb, input_output_aliases={i: i for i in range(nb)},
        compiler_params=pltpu.CompilerParams(has_side_effects=_EFFECT),
    )(*bufs, send_sem, recv_sem, after)
    return list(res)


def _plan_sibling_halves(shapes):
    na = len(shapes)

    def plan(refs, ssem, rsem):
        x, y, c, _ = _place()
        cps = []
        for a in range(na):
            rh = shapes[a][1] // 2
            src = refs[a].at[:, pl.ds(pl.multiple_of((1 - c) * rh, 8), rh), :]
            cps.append(_remote(src, refs[na + a], ssem, rsem, a, (x, y, 1 - c)))
        return cps, cps

    return plan


def _plan_chip_exchange(na, with_small):
    def plan(refs, ssem, rsem):
        x, y, c, chips = _place()
        cps = []
        for a in range(na):
            for j in range(3):
                kj = 2 * chips[j][0] + chips[j][1]
                cps.append(_remote(refs[a].at[kj], refs[na + a].at[j], ssem, rsem, 3 * a + j, (*chips[j], c)))
        if with_small:
            mine = refs[2 * na].at[4 * x + 2 * y + c]
            for r in range(1, N_DEV):
                peer = (x ^ (r >> 2), y ^ ((r >> 1) & 1), c ^ (r & 1))
                cps.append(_remote(mine, mine, ssem, rsem, 3 * na + r - 1, peer))
        return cps, cps

    return plan


def _plan_sibling_swap(shapes):
    def plan(refs, ssem, rsem):
        x, y, c, _ = _place()
        sends, arrivals = [], []
        for a, shp in enumerate(shapes):
            rh = shp[0] // 2
            mine, other = _rows(refs[a], None, c, rh), _rows(refs[a], None, 1 - c, rh)
            sends.append(_remote(mine, mine, ssem, rsem, a, (x, y, 1 - c)))
            arrivals.append(_remote(mine, other, ssem, rsem, a, (x, y, 1 - c)))
        return sends, arrivals

    return plan


def _plan_gather_chips(shapes):
    def plan(refs, ssem, rsem):
        x, y, c, chips = _place()
        me = 2 * x + y
        sends, arrivals = [], []
        for a, shp in enumerate(shapes):
            rh = shp[1] // 2
            mine = _rows(refs[a], me, c, rh)
            for j in range(3):
                land = _rows(refs[a], 2 * chips[j][0] + chips[j][1], c, rh)
                sends.append(_remote(mine, mine, ssem, rsem, 3 * a + j, (*chips[j], c)))
                arrivals.append(_remote(land, land, ssem, rsem, 3 * a + j, (*chips[j], c)))
        return sends, arrivals

    return plan


def _plan_gather_sibling(shapes):
    def plan(refs, ssem, rsem):
        x, y, c, chips = _place()
        sends, arrivals = [], []
        for a, shp in enumerate(shapes):
            rh = shp[1] // 2
            for j in range(3):
                kj = 2 * chips[j][0] + chips[j][1]
                got, land = _rows(refs[a], kj, c, rh), _rows(refs[a], kj, 1 - c, rh)
                sends.append(_remote(got, got, ssem, rsem, 3 * a + j, (x, y, 1 - c)))
                arrivals.append(_remote(got, land, ssem, rsem, 3 * a + j, (x, y, 1 - c)))
        return sends, arrivals

    return plan


def _into_slice(w, k, n, tr, dtype, name):
    r, cols = w.shape

    def body(k_ref, w_ref, o_ref):
        o_ref[0] = w_ref[...].astype(dtype)

    grid_spec = pltpu.PrefetchScalarGridSpec(
        num_scalar_prefetch=1, grid=(r // tr,),
        in_specs=[pl.BlockSpec((tr, cols), lambda i, k: (i, 0))],
        out_specs=pl.BlockSpec((1, tr, cols), lambda i, k: (k[0], i, 0)))
    return pl.pallas_call(body, grid_spec=grid_spec, out_shape=_sds((n, r, cols), dtype),
                          compiler_params=_cp(("parallel",)), name=name)(k, w)


class _LateWeights:
    def __init__(self, shards, chip, names, tiles):
        bufs = [_into_slice(w, chip, N_CHIPS, t, BF, "own_" + nm) for w, nm, t in zip(shards, names, tiles)]
        self.n = 3 * len(bufs)
        self.chips, self.sibling = _plan_gather_chips([b.shape for b in bufs]), _plan_gather_sibling([b.shape for b in bufs])
        self.ssem, self.rsem, self.bufs, token = _split_start("gather_chips_start", bufs, self.chips, self.n)
        self.first = (token,)

    def middle(self, after):
        bufs = _split_wait("gather_chips_wait", self.ssem, self.rsem, self.bufs, self.chips, after)
        self.ssem, self.rsem, self.bufs, token = _split_start("gather_sibling_start", bufs, self.sibling, self.n)
        return (token,)

    def last(self, after):
        return _split_wait("gather_sibling_wait", self.ssem, self.rsem, self.bufs, self.sibling, after)


class _GradReduce:
    def __init__(self, tag, place, names, tiles):
        self.tag, self.place, self.names, self.tiles = tag, place, names, tiles
        self.small_all = None

    def first(self, grads):
        self.na = len(grads)
        self.p1 = _plan_sibling_halves([g.shape for g in grads])
        lands = [lax.empty((N_CHIPS, g.shape[1] // 2, g.shape[2]), F32) for g in grads]
        self.ssem, self.rsem, self.bufs, token = _split_start(
            self.tag + "_halves_start", list(grads) + lands, self.p1, self.na)
        return (token,)

    def second(self, after, small=None):
        bufs = _split_wait(self.tag + "_halves_wait", self.ssem, self.rsem, self.bufs, self.p1, after)
        self.grads, self.sib = bufs[:self.na], bufs[self.na:]
        halves = [_pair_sum(g, r, self.place[1:2], t, "pair_sum_" + nm)
                  for g, r, t, nm in zip(self.grads, self.sib, self.tiles, self.names)]
        lands = [lax.empty((3,) + h.shape[1:], h.dtype) for h in halves]
        extra = [] if small is None else [small]
        self.p2 = _plan_chip_exchange(self.na, small is not None)
        self.ssem, self.rsem, self.bufs, token = _split_start(
            self.tag + "_chips_start", halves + lands + extra, self.p2, 3 * self.na + (N_DEV - 1) * len(extra))
        return (token,)

    def third(self, after):
        bufs = _split_wait(self.tag + "_chips_wait", self.ssem, self.rsem, self.bufs, self.p2, after)
        if len(bufs) > 2 * self.na:
            self.small_all = bufs[2 * self.na]
        mine = [_chip_sum(g, sb, r, self.place, t, "chip_sum_" + nm)
                for g, sb, r, t, nm in zip(self.grads, self.sib, bufs[self.na:2 * self.na], self.tiles, self.names)]
        self.p3 = _plan_sibling_swap([m.shape for m in mine])
        self.ssem, self.rsem, self.bufs, token = _split_start(self.tag + "_swap_start", mine, self.p3, self.na)
        return (token,)

    def last(self, after):
        return _split_wait(self.tag + "_swap_wait", self.ssem, self.rsem, self.bufs, self.p3, after)


def _pack(parts, rows):
    flat = jnp.concatenate([p.reshape(-1) for p in parts])
    return jnp.pad(flat, (0, rows * 128 - flat.shape[0])).reshape(rows, 128)


def _unpack(buf, shapes):
    flat = buf.reshape(-1)
    out, off = [], 0
    for shp in shapes:
        n = 1
        for d in shp:
            n *= d
        out.append(flat[off:off + n].reshape(shp))
        off += n
    return out


def _rows_for(n):
    return -(-n // (8 * 128)) * 8


class _WeightsAtHand:
    def __init__(self, wup, wout, wdown):
        self.first, self.weights = (), [wup, wout, wdown]

    def middle(self, after):
        return ()

    def last(self, after):
        return self.weights


class _GradsKept:
    def first(self, grads):
        self.grads = list(grads)
        return ()

    def second(self, after, small=None):
        return ()

    def third(self, after):
        return ()

    def last(self, after):
        return self.grads


def _forward_backward(xs, pos, tgt, win, wpool, cw, attn_norm, b_gate, pool_scale, q_norm, k_norm, sinks,
                      ffn_norm, conv_b, late, early, rest):
    s = xs.shape[0]
    tm = min(512, s)
    tk = min(1024, s)
    inv_freq = ROPE_THETA ** (-jnp.arange(0, ROPE_DIM, 2, dtype=F32) / ROPE_DIM)
    lane = jnp.arange(2 * HEAD) % HEAD
    invf = jnp.where(lane < ROPE_DIM, inv_freq[lane % (ROPE_DIM // 2)], 0.0).reshape(1, 2 * HEAD)
    wq = jnp.tile(q_norm, (1, N_Q))
    wk = jnp.tile(k_norm, (1, N_KV))
    head_of = jnp.arange(Q_W) // HEAD
    bd = (head_of[:, None] == head_of[None, :]).astype(BF)
    sink = sinks[0]

    h1, u, qkv, gates = _attn_in_proj(xs, attn_norm, win, b_gate, tm, deps=late.first)
    qh, kh, vh = _qk_prep(qkv, pos, wq, wk, invf, bd, tm)
    battn = _attn_fwd(qh, kh, vh, sink, deps=late.middle(qh))
    apool, pooled = _pool_fwd(u, wpool, pool_scale, min(512, s))
    wup, wout, wdown = late.last(apool)
    wout = wout.reshape(D, D)
    wdown = wdown.reshape(D_FF, D)
    mix, y1, h2 = _mix_out_proj(apool, battn, gates, xs, wout, ffn_norm, tm)
    pre_g, pre_v, up_g, up_v, act = _ffn_up(h2, wup, cw, conv_b, tm)
    dy2, dy2b, loss_acc = _ffn_down_loss(act, wdown, y1, tgt, tm)

    d_wdown = _grad_matmul(act, dy2b, 512, tk, "grad_w_down")
    dp_g, dp_v, dcw_g, dcw_v, dcb_g, dcb_v = _ffn_act_bwd(dy2b, wdown, up_g, up_v, pre_g, pre_v, cw, tm)
    d_wup = _grad_matmul(h2, dp_g, UP_SHARD, tk, "grad_w_up_gate", lead=N_CHIPS)
    d_wup = _grad_matmul(h2, dp_v, UP_SHARD, tk, "grad_w_up_value", lead=N_CHIPS, prev=d_wup, lead_off=2)
    token = early.first([d_wdown.reshape(N_CHIPS, D_FF // N_CHIPS, D), d_wup])
    dy1, d_ffn_norm = _ffn_up_bwd(dp_g, dp_v, wup, y1, dy2, ffn_norm, tm, deps=token)
    token = early.second(dy1)
    da, db, dzg, d_bgate, d_wout = _out_proj_bwd(dy1, wout, apool, battn, gates, mix, tm, deps=token)
    du, d_wpool, d_pscale = _pool_bwd(da, pooled, wpool, pool_scale, min(512, s))
    token = early.third(du)
    dqh, dkh, dvh, dsink = _attn_bwd(qh, kh, vh, sink, db, deps=token)
    dzq, d_qn, d_kn = _qk_prep_bwd(dqh, dkh, dvh, qkv, pos, wq, wk, invf, bd, tm)
    d_win = jnp.concatenate([
        _grad_matmul(h1, du, D, tk, "grad_w_in_pool"),
        _grad_matmul(h1, dzq, QKV_W, tk, "grad_w_in_qkv"),
        _grad_matmul(h1, dzg, D, tk, "grad_w_in_gates")], axis=1)
    token = rest.first([
        d_win.reshape(D, N_CHIPS, IN_W // N_CHIPS).transpose(1, 0, 2),
        d_wout.reshape(N_CHIPS, D // N_CHIPS, D),
        d_wpool.reshape(4, N_CHIPS, 64, POOL_GROUP).transpose(1, 0, 2, 3).reshape(N_CHIPS, 4 * 64, POOL_GROUP)])
    grad_x, d_attn_norm = _in_proj_bwd(du, dzq, dzg, win, xs, attn_norm, dy1, tm, deps=token)
    small_parts = [d_attn_norm, d_bgate, d_pscale, d_qn, d_kn, dsink[:, 0, 0:GQA], d_ffn_norm,
                   jnp.concatenate([dcb_g, dcb_v], axis=1), jnp.concatenate([dcw_g, dcw_v], axis=0)]
    return loss_acc, grad_x, small_parts


def kernel(x, positions, attn_norm, w_in, b_gate, w_pool, pool_scale, q_norm, k_norm, sinks, w_out, ffn_norm, w_up, conv_w, conv_b, w_down, loss_target, m_attn_norm, m_w_in, m_b_gate, m_w_pool, m_pool_scale, m_q_norm, m_k_norm, m_sinks, m_w_out, m_ffn_norm, m_w_up, m_conv_w, m_conv_b, m_w_down, v_attn_norm, v_w_in, v_b_gate, v_w_pool, v_pool_scale, v_q_norm, v_k_norm, v_sinks, v_w_out, v_ffn_norm, v_w_up, v_conv_w, v_conv_b, v_w_down):
    s = x.shape[1]
    xs = x[0]
    tgt = loss_target[0]
    pos = positions[0].reshape(s, 1)
    cx, cy, cc = lax.axis_index("x"), lax.axis_index("y"), lax.axis_index("c")
    chip = 2 * cx + cy

    chip_arr = chip.reshape(1).astype(jnp.int32)
    dev_arr = (2 * chip + cc).reshape(1).astype(jnp.int32)
    place = jnp.stack([chip, cc]).astype(jnp.int32)

    g_in, g_pool, g_cw = _all_gather_weights([w_in[0], w_pool[0].reshape(4 * 64, POOL_GROUP)], [conv_w[0]])
    win = g_in.transpose(1, 0, 2).reshape(D, IN_W)
    wpool = g_pool.reshape(N_CHIPS, 4, 64, POOL_GROUP).transpose(1, 0, 2, 3).reshape(4, POOL_GROUP, POOL_GROUP)
    late = _LateWeights([w_up[0], w_out[0], w_down[0]], chip_arr, ["w_up", "w_out", "w_down"], [256, 256, 352])
    early = _GradReduce("early", place, ["w_down", "w_up"], [176, 256])
    rest = _GradReduce("rest", place, ["w_in", "w_out", "w_pool"], [256, 128, 128])

    loss_acc, grad_x, small_parts = _forward_backward(
        xs, pos, tgt, win, wpool, g_cw, attn_norm, b_gate, pool_scale, q_norm, k_norm, sinks, ffn_norm, conv_b,
        late, early, rest)
    loss = lax.psum(jnp.sum(loss_acc) * (0.5 / D), ("x", "y", "c"))

    def two_d(a):
        return a.reshape(-1, a.shape[-1])

    def update(nm, w, g, m, v, tr, deps=()):
        res = _adamw(two_d(w), g, two_d(m), two_d(v), tr, "adamw_" + nm, deps=deps)
        return [r.reshape(w.shape) for r in res]

    small_shapes = [(1, D), (1, 2 * D), (1, D), (1, HEAD), (1, HEAD), (1, N_Q), (1, D), (1, 2 * D_FF),
                    (N_CHIPS, 3, UP_SHARD)]
    n_small = sum(p.size for p in small_parts)
    small = _into_slice(_pack(small_parts, _rows_for(n_small)), dev_arr, N_DEV, _rows_for(n_small), F32, "own_small")
    g_wdown, g_wup = early.last(grad_x)
    token = rest.second(grad_x, small=small)
    big_out = {"w_up": update("w_up", w_up, g_wup, m_w_up, v_w_up, 256, deps=token),
               "w_down": update("w_down", w_down, g_wdown, m_w_down, v_w_down, 176, deps=token)}
    token = rest.third(big_out["w_down"][1])
    small_sum = _device_sum(rest.small_all, deps=token)
    (g_attn_norm, g_bgate, g_pscale, g_qn, g_kn, g_sinks, g_ffn_norm, g_convb, g_convw_all) = _unpack(
        small_sum, small_shapes)
    g_convw = lax.dynamic_index_in_dim(g_convw_all, chip, axis=0, keepdims=False)

    small_names = ["attn_norm", "b_gate", "pool_scale", "q_norm", "k_norm", "sinks", "ffn_norm", "conv_b", "conv_w"]
    sm_w = [attn_norm, b_gate, pool_scale, q_norm, k_norm, sinks, ffn_norm, conv_b, conv_w]
    sm_m = [m_attn_norm, m_b_gate, m_pool_scale, m_q_norm, m_k_norm, m_sinks, m_ffn_norm, m_conv_b, m_conv_w]
    sm_v = [v_attn_norm, v_b_gate, v_pool_scale, v_q_norm, v_k_norm, v_sinks, v_ffn_norm, v_conv_b, v_conv_w]
    sm_g = [g_attn_norm, g_bgate, g_pscale, g_qn, g_kn, g_sinks, g_ffn_norm, g_convb, g_convw]
    sm_rows = _rows_for(sum(w.size for w in sm_w))
    res = _adamw(_pack(sm_w, sm_rows), _pack(sm_g, sm_rows), _pack(sm_m, sm_rows), _pack(sm_v, sm_rows),
                 sm_rows, "adamw_small")
    sm_out = [_unpack(r, [w.shape for w in sm_w]) for r in res]
    small_out = {nm: [sm_out[k][i] for k in range(4)] for i, nm in enumerate(small_names)}
    g_win, g_wout, g_wpool = rest.last(res[1])
    big_out["w_in"] = update("w_in", w_in, g_win, m_w_in, v_w_in, 256)
    big_out["w_out"] = update("w_out", w_out, g_wout, m_w_out, v_w_out, 128)
    big_out["w_pool"] = update("w_pool", w_pool, g_wpool, m_w_pool, v_w_pool, 128)

    order = ["attn_norm", "w_in", "b_gate", "w_pool", "pool_scale", "q_norm", "k_norm", "sinks", "w_out",
             "ffn_norm", "w_up", "conv_w", "conv_b", "w_down"]
    allout = {**big_out, **small_out}
    outs = [loss, grad_x[None]]
    for k in range(4):
        outs += [allout[nm][k] for nm in order]
    return tuple(outs)
```

```python
import functools

import jax
import jax.numpy as jnp
from jax import lax
from jax.experimental import pallas as pl
from jax.experimental.pallas import tpu as pltpu

D = 1024
D_FF = 2816
HEAD = 64
N_Q = 16
N_KV = 2
GQA = 8
BLK = 128
ROPE_DIM = 16
ROPE_THETA = 500000.0
POOL_GROUP = 256
Q_W = 1024
KV_W = 128
QKV_W = Q_W + 2 * KV_W
IN_W = 4352
UP_SHARD = 1408
EPS = 1e-6
N_CHIPS = 4
N_DEV = 8

LR = 0.001
B1 = 0.9
B2 = 0.999
ADAM_EPS = 1e-08
WD = 0.01
STEP = 10

BF = jnp.bfloat16
F32 = jnp.float32
MESH = pl.DeviceIdType.MESH
VMEM_LIMIT_MB = 56


def _cp(sem, vmem_mb=VMEM_LIMIT_MB):
    return pltpu.CompilerParams(dimension_semantics=sem, vmem_limit_bytes=vmem_mb << 20)


def _full(shape):
    nd = len(shape)
    return pl.BlockSpec(shape, lambda *_: (0,) * nd)


def _sds(shape, dtype):
    return jax.ShapeDtypeStruct(shape, dtype)


def _after(body, n_in, deps):
    nd = len(deps)
    if nd == 0:
        return body

    def ordered(*refs):
        return body(*refs[:n_in], *refs[n_in + nd:])

    return ordered


def _any_specs(deps):
    return [pl.BlockSpec(memory_space=pl.ANY)] * len(deps)


def _nt(a, b):
    return lax.dot_general(a, b, (((1,), (1,)), ((), ())), preferred_element_type=F32)


def _tn(a, b):
    return lax.dot_general(a, b, (((0,), (0,)), ((), ())), preferred_element_type=F32)


def _mm(a, b):
    return jnp.dot(a, b, preferred_element_type=F32)


def _head_sum(v, bd):
    hi = v.astype(BF)
    lo = (v - hi.astype(F32)).astype(BF)
    return _mm(hi, bd) + _mm(lo, bd)


def _rope_tables(pos_ref, invf_ref):
    ang = pos_ref[...].astype(F32) * invf_ref[...]
    cos = jnp.cos(ang)
    sin = jnp.sin(ang)
    lane = lax.broadcasted_iota(jnp.int32, (1, 2 * HEAD), 1) % HEAD
    sa = jnp.where(lane < ROPE_DIM // 2, -sin, 0.0)
    sb = jnp.where(lane < ROPE_DIM // 2, 0.0, jnp.where(lane < ROPE_DIM, sin, 0.0))
    return cos, sa, sb


def _tile_lanes(t, reps):
    return t if reps == 1 else jnp.tile(t, (1, reps))


def _rope(v, cos, sa, sb):
    w = v.shape[1]
    reps = w // (2 * HEAD)
    half = ROPE_DIM // 2
    return (v * _tile_lanes(cos, reps) + pltpu.roll(v, w - half, 1) * _tile_lanes(sa, reps)
            + pltpu.roll(v, half, 1) * _tile_lanes(sb, reps))


def _rope_t(dy, cos, sa, sb):
    w = dy.shape[1]
    reps = w // (2 * HEAD)
    half = ROPE_DIM // 2
    return (dy * _tile_lanes(cos, reps) + pltpu.roll(dy * _tile_lanes(sa, reps), half, 1)
            + pltpu.roll(dy * _tile_lanes(sb, reps), w - half, 1))


def _attn_in_proj(x, g1, w_in, b_gate, tm, deps=()):
    s = x.shape[0]

    def body(x_ref, g_ref, w_ref, b_ref, h_ref, u_ref, qkv_ref, gate_ref):
        xv = x_ref[...]
        r = lax.rsqrt(jnp.mean(xv * xv, axis=-1, keepdims=True) + EPS)
        h = (xv * r * g_ref[...]).astype(BF)
        h_ref[...] = h
        u_ref[...] = _nt(h, w_ref[0:D, :])
        qkv_ref[...] = _nt(h, w_ref[D:D + QKV_W, :])
        gate_ref[...] = jax.nn.sigmoid(_nt(h, w_ref[D + QKV_W:IN_W, :]) + b_ref[...]).astype(BF)

    row = lambda w: pl.BlockSpec((tm, w), lambda i: (i, 0))
    return pl.pallas_call(
        _after(body, 4, deps), grid=(s // tm,),
        in_specs=[row(D), _full((1, D)), _full((IN_W, D)), _full((1, 2 * D))] + _any_specs(deps),
        out_specs=[row(D), row(D), row(QKV_W), row(2 * D)],
        out_shape=[_sds((s, D), BF), _sds((s, D), F32), _sds((s, QKV_W), F32), _sds((s, 2 * D), BF)],
        compiler_params=_cp(("parallel",)), name="attn_in_proj")(x, g1, w_in, b_gate, *deps)


def _qk_prep(qkv, pos, wq, wk, invf, bd, tm):
    s = qkv.shape[0]

    def body(qkv_ref, pos_ref, wq_ref, wk_ref, invf_ref, bd_ref, qh_ref, kh_ref, vh_ref):
        cos, sa, sb = _rope_tables(pos_ref, invf_ref)
        q = qkv_ref[:, 0:Q_W]
        k = qkv_ref[:, Q_W:Q_W + KV_W]
        v = qkv_ref[:, Q_W + KV_W:QKV_W]
        rq = lax.rsqrt(_head_sum(q * q, bd_ref[...]) * (1.0 / HEAD) + EPS)
        qr = _rope(q * rq * wq_ref[...], cos, sa, sb) * (HEAD ** -0.5)
        rk = lax.rsqrt(_head_sum(k * k, bd_ref[0:KV_W, 0:KV_W]) * (1.0 / HEAD) + EPS)
        kr = _rope(k * rk * wk_ref[...], cos, sa, sb)
        for h in range(N_Q):
            qh_ref[h] = qr[:, HEAD * h:HEAD * (h + 1)].astype(BF)
        for h in range(N_KV):
            kh_ref[h] = kr[:, HEAD * h:HEAD * (h + 1)].astype(BF)
            vh_ref[h] = v[:, HEAD * h:HEAD * (h + 1)].astype(BF)

    heads = lambda n: pl.BlockSpec((n, tm, HEAD), lambda i: (0, i, 0))
    return pl.pallas_call(
        body, grid=(s // tm,),
        in_specs=[pl.BlockSpec((tm, QKV_W), lambda i: (i, 0)), pl.BlockSpec((tm, 1), lambda i: (i, 0)),
                  _full((1, Q_W)), _full((1, KV_W)), _full((1, 2 * HEAD)), _full((Q_W, Q_W))],
        out_specs=[heads(N_Q), heads(N_KV), heads(N_KV)],
        out_shape=[_sds((N_Q, s, HEAD), BF), _sds((N_KV, s, HEAD), BF), _sds((N_KV, s, HEAD), BF)],
        compiler_params=_cp(("parallel",)), name="qk_prep")(qkv, pos, wq, wk, invf, bd)


def _sink_column(sink_ref, kh):
    row_g = lax.broadcasted_iota(jnp.int32, (GQA * BLK, 1), 0) // BLK
    col = jnp.zeros((GQA * BLK, 1), F32)
    for g in range(GQA):
        col = jnp.where(row_g == g, sink_ref[kh * GQA + g], col)
    return col


def _attn_probs(q, k, n, sink_col):
    sc = _nt(q, k)
    qi = lax.broadcasted_iota(jnp.int32, sc.shape, 0) % BLK + BLK
    ki = lax.broadcasted_iota(jnp.int32, sc.shape, 1)
    lo = jnp.where(n > 0, qi - BLK, BLK - 1)
    ok = (ki <= qi) & (ki > lo)
    sc = jnp.where(ok, sc, -jnp.inf)
    m = jnp.maximum(jnp.max(sc, axis=-1, keepdims=True), sink_col)
    p = jnp.exp(sc - m)
    es = jnp.exp(sink_col - m)
    inv = 1.0 / (jnp.sum(p, axis=-1, keepdims=True) + es)
    return p * inv, es * inv


def _attn_fwd(qh, kh, vh, sinks, deps=()):
    s = qh.shape[1]
    nb = s // BLK

    def body(sink_ref, q_ref, kp_ref, kc_ref, vp_ref, vc_ref, o_ref):
        n = pl.program_id(0)
        for khd in range(N_KV):
            q = q_ref[khd * GQA:(khd + 1) * GQA].reshape(GQA * BLK, HEAD)
            k = jnp.concatenate([kp_ref[khd], kc_ref[khd]], axis=0)
            v = jnp.concatenate([vp_ref[khd], vc_ref[khd]], axis=0)
            probs, _ = _attn_probs(q, k, n, _sink_column(sink_ref, khd))
            o = _mm(probs.astype(BF), v)
            for j in range(GQA // 2):
                c0 = khd * GQA * HEAD + 2 * HEAD * j
                o_ref[:, c0:c0 + 2 * HEAD] = jnp.concatenate(
                    [o[2 * j * BLK:(2 * j + 1) * BLK], o[(2 * j + 1) * BLK:(2 * j + 2) * BLK]], axis=1).astype(BF)

    prev = pl.BlockSpec((N_KV, BLK, HEAD), lambda n: (0, jnp.maximum(n - 1, 0), 0))
    cur = pl.BlockSpec((N_KV, BLK, HEAD), lambda n: (0, n, 0))
    return pl.pallas_call(
        _after(body, 6, deps), grid=(nb,),
        in_specs=[pl.BlockSpec(memory_space=pltpu.SMEM),
                  pl.BlockSpec((N_Q, BLK, HEAD), lambda n: (0, n, 0)), prev, cur, prev, cur] + _any_specs(deps),
        out_specs=pl.BlockSpec((BLK, Q_W), lambda n: (n, 0)),
        out_shape=_sds((s, Q_W), BF),
        compiler_params=_cp(("parallel",)), name="attn_fwd")(sinks, qh, kh, kh, vh, vh, *deps)


def _pool_fwd(u, w_pool, pool_scale, ts):
    s = u.shape[0]
    halo = 16

    def body(u_ref, wp_ref, ps_ref, a_ref, pooled_ref, prev):
        g = pl.program_id(0)
        i = pl.program_id(1)

        @pl.when(i == 0)
        def _():
            prev[...] = jnp.zeros_like(prev)

        cur = u_ref[...]
        ext = jnp.concatenate([prev[...], cur], axis=0)
        t = (i * ts + lax.broadcasted_iota(jnp.int32, (ts, 1), 0)).astype(F32)
        for gi in range(4):
            @pl.when(g == gi)
            def _(gi=gi):
                w = 2 << gi
                acc, span = ext, 1
                while span < w:
                    acc = acc + pltpu.roll(acc, span, 0)
                    span *= 2
                inv = 1.0 / jnp.minimum(t + 1.0, float(w))
                pooled = (acc[halo:halo + ts] * inv - cur).astype(BF)
                pooled_ref[...] = pooled
                a_ref[...] = (_mm(pooled, wp_ref[0]) * ps_ref[...]).astype(BF)

        prev[...] = cur[ts - halo:ts]

    col = pl.BlockSpec((ts, POOL_GROUP), lambda g, i: (i, g))
    return pl.pallas_call(
        body, grid=(4, s // ts),
        in_specs=[col, pl.BlockSpec((1, POOL_GROUP, POOL_GROUP), lambda g, i: (g, 0, 0)),
                  pl.BlockSpec((1, POOL_GROUP), lambda g, i: (0, g))],
        out_specs=[col, col],
        out_shape=[_sds((s, D), BF), _sds((s, D), BF)],
        scratch_shapes=[pltpu.VMEM((halo, POOL_GROUP), F32)],
        compiler_params=_cp(("parallel", "arbitrary")), name="pool_fwd")(u, w_pool, pool_scale)


def _mix_out_proj(a, b, gates, x, w_out, g2, tm):
    s = x.shape[0]

    def body(a_ref, b_ref, gate_ref, x_ref, w_ref, g_ref, mix_ref, y_ref, h_ref):
        mix = (gate_ref[:, 0:D].astype(F32) * a_ref[...].astype(F32)
               + gate_ref[:, D:2 * D].astype(F32) * b_ref[...].astype(F32)).astype(BF)
        mix_ref[...] = mix
        y = x_ref[...] + _mm(mix, w_ref[...])
        y_ref[...] = y
        r = lax.rsqrt(jnp.mean(y * y, axis=-1, keepdims=True) + EPS)
        h_ref[...] = (y * r * g_ref[...]).astype(BF)

    row = lambda w: pl.BlockSpec((tm, w), lambda i: (i, 0))
    return pl.pallas_call(
        body, grid=(s // tm,),
        in_specs=[row(D), row(D), row(2 * D), row(D), _full((D, D)), _full((1, D))],
        out_specs=[row(D), row(D), row(D)],
        out_shape=[_sds((s, D), BF), _sds((s, D), F32), _sds((s, D), BF)],
        compiler_params=_cp(("parallel",)), name="mix_out_proj")(a, b, gates, x, w_out, g2)


def _ffn_up(h2, w_up, conv_w, conv_b, tm):
    s = h2.shape[0]

    def body(h_ref, wg_ref, wv_ref, cwg_ref, cwv_ref, cbg_ref, cbv_ref,
             preg_ref, prev_ref, upg_ref, upv_ref, act_ref, halog, halov):
        i = pl.program_id(1)

        @pl.when(i == 0)
        def _():
            halog[...] = jnp.zeros_like(halog)
            halov[...] = jnp.zeros_like(halov)

        h = h_ref[...]

        def conv_half(w_ref, cw_ref, cb_ref, halo, pre_ref, up_ref):
            pre = _mm(h, w_ref[0])
            pre_ref[...] = pre.astype(BF)
            ext = jnp.concatenate([halo[...], pre], axis=0)
            cw = cw_ref[0]
            up = cb_ref[...] + cw[0:1] * pltpu.roll(ext, 2, 0)[8:8 + tm]
            up = up + cw[1:2] * pltpu.roll(ext, 1, 0)[8:8 + tm]
            up = up + cw[2:3] * pre
            halo[...] = pre[tm - 8:tm]
            up_ref[...] = up.astype(BF)
            return up

        gate = conv_half(wg_ref, cwg_ref, cbg_ref, halog, preg_ref, upg_ref)
        val = conv_half(wv_ref, cwv_ref, cbv_ref, halov, prev_ref, upv_ref)
        act_ref[...] = (gate * jax.nn.sigmoid(gate) * val).astype(BF)

    tile = pl.BlockSpec((tm, UP_SHARD), lambda j, i: (i, j))
    wspec = lambda off: pl.BlockSpec((1, D, UP_SHARD), lambda j, i: (j + off, 0, 0))
    cwspec = lambda off: pl.BlockSpec((1, 3, UP_SHARD), lambda j, i: (j + off, 0, 0))
    cbspec = lambda off: pl.BlockSpec((1, UP_SHARD), lambda j, i: (0, j + off))
    half = _sds((s, D_FF), BF)
    return pl.pallas_call(
        body, grid=(2, s // tm),
        in_specs=[pl.BlockSpec((tm, D), lambda j, i: (i, 0)), wspec(0), wspec(2), cwspec(0), cwspec(2),
                  cbspec(0), cbspec(2)],
        out_specs=[tile] * 5, out_shape=[half] * 5,
        scratch_shapes=[pltpu.VMEM((8, UP_SHARD), F32), pltpu.VMEM((8, UP_SHARD), F32)],
        compiler_params=_cp(("parallel", "arbitrary")), name="ffn_up")(
            h2, w_up, w_up, conv_w, conv_w, conv_b, conv_b)


def _ffn_down_loss(act, w_down, y1, tgt, tm):
    s = y1.shape[0]

    def body(act_ref, w_ref, y_ref, t_ref, dy_ref, dyb_ref, loss_ref):
        @pl.when(pl.program_id(0) == 0)
        def _():
            loss_ref[...] = jnp.zeros_like(loss_ref)

        e = y_ref[...] + _mm(act_ref[...], w_ref[...]) - t_ref[...]
        dy = e * (1.0 / D)
        dy_ref[...] = dy
        dyb_ref[...] = dy.astype(BF)
        e2 = (e * e).reshape(tm // 8, 8, D).sum(axis=0)
        part = e2[:, 0:128]
        for j in range(1, D // 128):
            part = part + e2[:, 128 * j:128 * (j + 1)]
        loss_ref[...] += part

    row = lambda w: pl.BlockSpec((tm, w), lambda i: (i, 0))
    return pl.pallas_call(
        body, grid=(s // tm,),
        in_specs=[row(D_FF), _full((D_FF, D)), row(D), row(D)],
        out_specs=[row(D), row(D), _full((8, 128))],
        out_shape=[_sds((s, D), F32), _sds((s, D), BF), _sds((8, 128), F32)],
        compiler_params=_cp(("arbitrary",)), name="ffn_down_loss")(act, w_down, y1, tgt)


def _grad_matmul(a, b, tn, tk, name, lead=None, prev=None, lead_off=0):
    s, m = a.shape
    n = b.shape[1]
    nj = n // tn

    def body(*refs):
        a_ref, b_ref = refs[0], refs[1]
        o_ref = refs[-1]
        acc = _tn(a_ref[...], b_ref[...])
        acc = acc if lead is None else acc[None]

        @pl.when(pl.program_id(1) == 0)
        def _():
            o_ref[...] = acc

        @pl.when(pl.program_id(1) > 0)
        def _():
            o_ref[...] += acc

    in_specs = [pl.BlockSpec((tk, m), lambda j, k: (k, 0)), pl.BlockSpec((tk, tn), lambda j, k: (k, j))]
    args = [a, b]
    aliases = {}
    if lead is None:
        out_spec = pl.BlockSpec((m, tn), lambda j, k: (0, j))
        out_shape = _sds((m, n), F32)
    else:
        out_spec = pl.BlockSpec((1, m, tn), lambda j, k: (j + lead_off, 0, 0))
        out_shape = _sds((lead, m, tn), F32)
        if prev is not None:
            in_specs.append(pl.BlockSpec(memory_space=pl.ANY))
            args.append(prev)
            aliases = {2: 0}
    return pl.pallas_call(
        body, grid=(nj, s // tk), in_specs=in_specs, out_specs=out_spec, out_shape=out_shape,
        input_output_aliases=aliases,
        compiler_params=_cp(("parallel", "arbitrary")), name=name)(*args)


def _ffn_act_bwd(dyb, w_down, up_g, up_v, pre_g, pre_v, conv_w, tm):
    s = dyb.shape[0]
    nt = s // tm

    def body(dy_ref, wd_ref, upg_ref, upv_ref, preg_ref, prev_ref, cwg_ref, cwv_ref,
             dpg_ref, dpv_ref, dcwg_ref, dcwv_ref, dcbg_ref, dcbv_ref, nxg, nxv):
        i = pl.program_id(1)

        @pl.when(i == 0)
        def _():
            nxg[...] = jnp.zeros_like(nxg)
            nxv[...] = jnp.zeros_like(nxv)
            dcwg_ref[...] = jnp.zeros_like(dcwg_ref)
            dcwv_ref[...] = jnp.zeros_like(dcwv_ref)
            dcbg_ref[...] = jnp.zeros_like(dcbg_ref)
            dcbv_ref[...] = jnp.zeros_like(dcbv_ref)

        dact = _nt(dy_ref[...], wd_ref[...])
        g = upg_ref[...].astype(F32)
        v = upv_ref[...].astype(F32)
        sg = jax.nn.sigmoid(g)
        d_v = dact * (g * sg)
        d_g = dact * v * (sg * (1.0 + g * (1.0 - sg)))

        def conv_bwd(d_up, nx, pre_ref, cw_ref, dp_ref, dcw_ref, dcb_ref):
            ext = jnp.concatenate([d_up, nx[...]], axis=0)
            s1 = pltpu.roll(ext, tm + 8 - 1, 0)[0:tm]
            s2 = pltpu.roll(ext, tm + 8 - 2, 0)[0:tm]
            cw = cw_ref[0]
            dp_ref[...] = (cw[2:3] * d_up + cw[1:2] * s1 + cw[0:1] * s2).astype(BF)
            nx[...] = d_up[0:8]
            pre = pre_ref[...].astype(F32)
            dcw_ref[0, 0:1, :] += jnp.sum(s2 * pre, axis=0, keepdims=True)
            dcw_ref[0, 1:2, :] += jnp.sum(s1 * pre, axis=0, keepdims=True)
            dcw_ref[0, 2:3, :] += jnp.sum(d_up * pre, axis=0, keepdims=True)
            dcb_ref[...] += jnp.sum(d_up, axis=0, keepdims=True)

        conv_bwd(d_g, nxg, preg_ref, cwg_ref, dpg_ref, dcwg_ref, dcbg_ref)
        conv_bwd(d_v, nxv, prev_ref, cwv_ref, dpv_ref, dcwv_ref, dcbv_ref)

    tile = pl.BlockSpec((tm, UP_SHARD), lambda j, i: (nt - 1 - i, j))
    cwspec = lambda off: pl.BlockSpec((1, 3, UP_SHARD), lambda j, i: (j + off, 0, 0))
    acc_cw = pl.BlockSpec((1, 3, UP_SHARD), lambda j, i: (j, 0, 0))
    acc_cb = pl.BlockSpec((1, UP_SHARD), lambda j, i: (0, j))
    buf = pltpu.VMEM((8, UP_SHARD), F32)
    return pl.pallas_call(
        body, grid=(2, nt),
        in_specs=[pl.BlockSpec((tm, D), lambda j, i: (nt - 1 - i, 0)),
                  pl.BlockSpec((UP_SHARD, D), lambda j, i: (j, 0)),
                  tile, tile, tile, tile, cwspec(0), cwspec(2)],
        out_specs=[tile, tile, acc_cw, acc_cw, acc_cb, acc_cb],
        out_shape=[_sds((s, D_FF), BF), _sds((s, D_FF), BF), _sds((2, 3, UP_SHARD), F32),
                   _sds((2, 3, UP_SHARD), F32), _sds((1, D_FF), F32), _sds((1, D_FF), F32)],
        scratch_shapes=[buf, buf],
        compiler_params=_cp(("parallel", "arbitrary")), name="ffn_act_bwd")(
            dyb, w_down, up_g, up_v, pre_g, pre_v, conv_w, conv_w)


def _rms_bwd(dh, y, g):
    r = lax.rsqrt(jnp.mean(y * y, axis=-1, keepdims=True) + EPS)
    n = y * r
    dn = dh * g
    return r * (dn - n * jnp.mean(dn * n, axis=-1, keepdims=True)), dh * n


def _ffn_up_bwd(dp_g, dp_v, w_up, y1, dy2, g2, tm, deps=()):
    s = y1.shape[0]

    def body(dg_ref, dv_ref, w_ref, y_ref, dy2_ref, g_ref, dy1_ref, dgn_ref):
        @pl.when(pl.program_id(0) == 0)
        def _():
            dgn_ref[...] = jnp.zeros_like(dgn_ref)

        dh = _nt(dg_ref[:, 0:UP_SHARD], w_ref[0])
        dh = dh + _nt(dg_ref[:, UP_SHARD:D_FF], w_ref[1])
        dh = dh + _nt(dv_ref[:, 0:UP_SHARD], w_ref[2])
        dh = dh + _nt(dv_ref[:, UP_SHARD:D_FF], w_ref[3])
        dy, dgn = _rms_bwd(dh, y_ref[...], g_ref[...])
        dy1_ref[...] = dy2_ref[...] + dy
        dgn_ref[...] += jnp.sum(dgn, axis=0, keepdims=True)

    row = lambda w: pl.BlockSpec((tm, w), lambda i: (i, 0))
    return pl.pallas_call(
        _after(body, 6, deps), grid=(s // tm,),
        in_specs=[row(D_FF), row(D_FF), _full((4, D, UP_SHARD)), row(D), row(D), _full((1, D))] + _any_specs(deps),
        out_specs=[row(D), _full((1, D))],
        out_shape=[_sds((s, D), F32), _sds((1, D), F32)],
        compiler_params=_cp(("arbitrary",)), name="ffn_up_bwd")(dp_g, dp_v, w_up, y1, dy2, g2, *deps)


def _out_proj_bwd(dy1, w_out, a, b, gates, mix, tm, deps=()):
    s = dy1.shape[0]

    def body(dy_ref, w_ref, a_ref, b_ref, gate_ref, mix_ref, da_ref, db_ref, dzg_ref, dbg_ref, dw_ref):
        @pl.when(pl.program_id(0) == 0)
        def _():
            dbg_ref[...] = jnp.zeros_like(dbg_ref)
            dw_ref[...] = jnp.zeros_like(dw_ref)

        dyb = dy_ref[...].astype(BF)
        dmix = _nt(dyb, w_ref[...])
        gp = gate_ref[:, 0:D].astype(F32)
        ga = gate_ref[:, D:2 * D].astype(F32)
        da_ref[...] = (dmix * gp).astype(BF)
        db_ref[...] = (dmix * ga).astype(BF)
        dzp = dmix * a_ref[...].astype(F32) * (gp * (1.0 - gp))
        dza = dmix * b_ref[...].astype(F32) * (ga * (1.0 - ga))
        dzg_ref[:, 0:D] = dzp.astype(BF)
        dzg_ref[:, D:2 * D] = dza.astype(BF)
        dbg_ref[:, 0:D] += jnp.sum(dzp, axis=0, keepdims=True)
        dbg_ref[:, D:2 * D] += jnp.sum(dza, axis=0, keepdims=True)
        dw_ref[...] += _tn(mix_ref[...], dyb)

    row = lambda w: pl.BlockSpec((tm, w), lambda i: (i, 0))
    return pl.pallas_call(
        _after(body, 6, deps), grid=(s // tm,),
        in_specs=[row(D), _full((D, D)), row(D), row(D), row(2 * D), row(D)] + _any_specs(deps),
        out_specs=[row(D), row(D), row(2 * D), _full((1, 2 * D)), _full((D, D))],
        out_shape=[_sds((s, D), BF), _sds((s, D), BF), _sds((s, 2 * D), BF), _sds((1, 2 * D), F32),
                   _sds((D, D), F32)],
        compiler_params=_cp(("arbitrary",)), name="out_proj_bwd")(dy1, w_out, a, b, gates, mix, *deps)


def _pool_bwd(da, pooled, w_pool, pool_scale, ts):
    s = da.shape[0]
    nt = s // ts
    halo = 16

    def body(da_ref, pooled_ref, wp_ref, ps_ref, du_ref, dwp_ref, dps_ref, nxt):
        g = pl.program_id(0)
        i = pl.program_id(1)
        ti = nt - 1 - i

        @pl.when(i == 0)
        def _():
            nxt[...] = jnp.zeros_like(nxt)
            dwp_ref[...] = jnp.zeros_like(dwp_ref)
            dps_ref[...] = jnp.zeros_like(dps_ref)

        pooled = pooled_ref[...]
        dav = da_ref[...].astype(F32)
        dps_ref[...] += jnp.sum(dav * _mm(pooled, wp_ref[0]), axis=0, keepdims=True)
        dm = (dav * ps_ref[...]).astype(BF)
        dwp_ref[0] += _tn(pooled, dm)
        dpool = _nt(dm, wp_ref[0])
        t = (ti * ts + lax.broadcasted_iota(jnp.int32, (ts, 1), 0)).astype(F32)
        for gi in range(4):
            @pl.when(g == gi)
            def _(gi=gi):
                w = 2 << gi
                e = dpool * (1.0 / jnp.minimum(t + 1.0, float(w)))
                acc, span = jnp.concatenate([e, nxt[...]], axis=0), 1
                while span < w:
                    acc = acc + pltpu.roll(acc, ts + halo - span, 0)
                    span *= 2
                du_ref[...] = (acc[0:ts] - dpool).astype(BF)
                nxt[...] = e[0:halo]

    col = pl.BlockSpec((ts, POOL_GROUP), lambda g, i: (nt - 1 - i, g))
    return pl.pallas_call(
        body, grid=(4, nt),
        in_specs=[col, col, pl.BlockSpec((1, POOL_GROUP, POOL_GROUP), lambda g, i: (g, 0, 0)),
                  pl.BlockSpec((1, POOL_GROUP), lambda g, i: (0, g))],
        out_specs=[col, pl.BlockSpec((1, POOL_GROUP, POOL_GROUP), lambda g, i: (g, 0, 0)),
                   pl.BlockSpec((1, POOL_GROUP), lambda g, i: (0, g))],
        out_shape=[_sds((s, D), BF), _sds((4, POOL_GROUP, POOL_GROUP), F32), _sds((1, D), F32)],
        scratch_shapes=[pltpu.VMEM((halo, POOL_GROUP), F32)],
        compiler_params=_cp(("parallel", "arbitrary")), name="pool_bwd")(da, pooled, w_pool, pool_scale)


def _attn_bwd(qh, kh, vh, sinks, db, deps=()):
    s = qh.shape[1]
    nb = s // BLK

    def body(sink_ref, q_ref, kp_ref, kc_ref, vp_ref, vc_ref, do_ref,
             dq_ref, dk_ref, dv_ref, dsink_ref, ck, cv):
        n = pl.program_id(0)

        @pl.when(n == 0)
        def _():
            ck[...] = jnp.zeros_like(ck)
            cv[...] = jnp.zeros_like(cv)
            dsink_ref[...] = jnp.zeros_like(dsink_ref)

        @pl.when(n < nb)
        def _():
            dov = do_ref[...]
            for khd in range(N_KV):
                q = q_ref[khd * GQA:(khd + 1) * GQA].reshape(GQA * BLK, HEAD)
                k = jnp.concatenate([kp_ref[khd], kc_ref[khd]], axis=0)
                v = jnp.concatenate([vp_ref[khd], vc_ref[khd]], axis=0)
                c0 = khd * GQA * HEAD
                do = jnp.concatenate([dov[:, c0 + HEAD * g:c0 + HEAD * (g + 1)] for g in range(GQA)],
                                     axis=0).astype(BF)
                probs, psink = _attn_probs(q, k, n, _sink_column(sink_ref, khd))
                dp = _nt(do, v)
                delta = jnp.sum(probs * dp, axis=-1, keepdims=True)
                ds = (probs * (dp - delta)).astype(BF)
                dq_ref[khd * GQA:(khd + 1) * GQA] = _mm(ds, k).reshape(GQA, BLK, HEAD)
                dk = _tn(ds, q)
                dv = _tn(probs.astype(BF), do)
                dk_ref[khd] = ck[khd] + dk[0:BLK]
                dv_ref[khd] = cv[khd] + dv[0:BLK]
                ck[khd] = dk[BLK:2 * BLK]
                cv[khd] = dv[BLK:2 * BLK]
                dsk = psink * delta
                lane = lax.broadcasted_iota(jnp.int32, (1, 128), 1)
                acc = jnp.zeros((1, 128), F32)
                for g in range(GQA):
                    acc = acc - jnp.where(lane == g, jnp.sum(dsk[g * BLK:(g + 1) * BLK], axis=0, keepdims=True), 0.0)
                dsink_ref[khd] += acc

        @pl.when(n == nb)
        def _():
            dk_ref[...] = ck[...]
            dv_ref[...] = cv[...]

    last = nb - 1
    prev = pl.BlockSpec((N_KV, BLK, HEAD), lambda n: (0, jnp.maximum(jnp.minimum(n, last) - 1, 0), 0))
    cur = pl.BlockSpec((N_KV, BLK, HEAD), lambda n: (0, jnp.minimum(n, last), 0))
    kv_out = pl.BlockSpec((N_KV, BLK, HEAD), lambda n: (0, jnp.maximum(n - 1, 0), 0))
    return pl.pallas_call(
        _after(body, 7, deps), grid=(nb + 1,),
        in_specs=[pl.BlockSpec(memory_space=pltpu.SMEM),
                  pl.BlockSpec((N_Q, BLK, HEAD), lambda n: (0, jnp.minimum(n, last), 0)),
                  prev, cur, prev, cur,
                  pl.BlockSpec((BLK, Q_W), lambda n: (jnp.minimum(n, last), 0))] + _any_specs(deps),
        out_specs=[pl.BlockSpec((N_Q, BLK, HEAD), lambda n: (0, jnp.minimum(n, last), 0)), kv_out, kv_out,
                   _full((N_KV, 1, 128))],
        out_shape=[_sds((N_Q, s, HEAD), F32), _sds((N_KV, s, HEAD), F32), _sds((N_KV, s, HEAD), F32),
                   _sds((N_KV, 1, 128), F32)],
        scratch_shapes=[pltpu.VMEM((N_KV, BLK, HEAD), F32), pltpu.VMEM((N_KV, BLK, HEAD), F32)],
        compiler_params=_cp(("arbitrary",)), name="attn_bwd")(sinks, qh, kh, kh, vh, vh, db, *deps)


def _qk_prep_bwd(dqh, dkh, dvh, qkv, pos, wq, wk, invf, bd, tm):
    s = qkv.shape[0]

    def fold_heads(row):
        out = row[:, 0:HEAD]
        for h in range(1, row.shape[1] // HEAD):
            out = out + row[:, HEAD * h:HEAD * (h + 1)]
        return out

    def body(dq_ref, dk_ref, dv_ref, qkv_ref, pos_ref, wq_ref, wk_ref, invf_ref, bd_ref,
             dz_ref, dwq_ref, dwk_ref):
        @pl.when(pl.program_id(0) == 0)
        def _():
            dwq_ref[...] = jnp.zeros_like(dwq_ref)
            dwk_ref[...] = jnp.zeros_like(dwk_ref)

        cos, sa, sb = _rope_tables(pos_ref, invf_ref)

        def norm_rope_bwd(dy, xin, w, bdm):
            dn = _rope_t(dy, cos, sa, sb)
            r = lax.rsqrt(_head_sum(xin * xin, bdm) * (1.0 / HEAD) + EPS)
            nh = xin * r
            gw = dn * w
            dx = r * (gw - nh * (_head_sum(gw * nh, bdm) * (1.0 / HEAD)))
            return dx, fold_heads(jnp.sum(dn * nh, axis=0, keepdims=True))

        dq = jnp.concatenate([dq_ref[h] for h in range(N_Q)], axis=1) * (HEAD ** -0.5)
        dk = jnp.concatenate([dk_ref[h] for h in range(N_KV)], axis=1)
        dxq, dwq = norm_rope_bwd(dq, qkv_ref[:, 0:Q_W], wq_ref[...], bd_ref[...])
        dxk, dwk = norm_rope_bwd(dk, qkv_ref[:, Q_W:Q_W + KV_W], wk_ref[...], bd_ref[0:KV_W, 0:KV_W])
        dz_ref[:, 0:Q_W] = dxq.astype(BF)
        dz_ref[:, Q_W:Q_W + KV_W] = dxk.astype(BF)
        dz_ref[:, Q_W + KV_W:QKV_W] = jnp.concatenate([dv_ref[h] for h in range(N_KV)], axis=1).astype(BF)
        dwq_ref[...] += dwq
        dwk_ref[...] += dwk

    heads = lambda n: pl.BlockSpec((n, tm, HEAD), lambda i: (0, i, 0))
    return pl.pallas_call(
        body, grid=(s // tm,),
        in_specs=[heads(N_Q), heads(N_KV), heads(N_KV), pl.BlockSpec((tm, QKV_W), lambda i: (i, 0)),
                  pl.BlockSpec((tm, 1), lambda i: (i, 0)), _full((1, Q_W)), _full((1, KV_W)),
                  _full((1, 2 * HEAD)), _full((Q_W, Q_W))],
        out_specs=[pl.BlockSpec((tm, QKV_W), lambda i: (i, 0)), _full((1, HEAD)), _full((1, HEAD))],
        out_shape=[_sds((s, QKV_W), BF), _sds((1, HEAD), F32), _sds((1, HEAD), F32)],
        compiler_params=_cp(("arbitrary",)), name="qk_prep_bwd")(dqh, dkh, dvh, qkv, pos, wq, wk, invf, bd)


def _in_proj_bwd(du, dzq, dzg, w_in, x, g1, dy1, tm, deps=()):
    s = x.shape[0]

    def body(du_ref, dzq_ref, dzg_ref, w_ref, x_ref, g_ref, dy_ref, gx_ref, dgn_ref):
        @pl.when(pl.program_id(0) == 0)
        def _():
            dgn_ref[...] = jnp.zeros_like(dgn_ref)

        dh = _mm(du_ref[...], w_ref[0:D, :])
        dh = dh + _mm(dzq_ref[...], w_ref[D:D + QKV_W, :])
        dh = dh + _mm(dzg_ref[...], w_ref[D + QKV_W:IN_W, :])
        dx, dgn = _rms_bwd(dh, x_ref[...], g_ref[...])
        gx_ref[...] = dy_ref[...] + dx
        dgn_ref[...] += jnp.sum(dgn, axis=0, keepdims=True)

    row = lambda w: pl.BlockSpec((tm, w), lambda i: (i, 0))
    return pl.pallas_call(
        _after(body, 7, deps), grid=(s // tm,),
        in_specs=[row(D), row(QKV_W), row(2 * D), _full((IN_W, D)), row(D), _full((1, D)), row(D)] + _any_specs(deps),
        out_specs=[row(D), _full((1, D))],
        out_shape=[_sds((s, D), F32), _sds((1, D), F32)],
        compiler_params=_cp(("arbitrary",)), name="in_proj_bwd")(du, dzq, dzg, w_in, x, g1, dy1, *deps)


def _adamw(w, g, m, v, tr, name, deps=()):
    r, c = w.shape

    def body(w_ref, g_ref, m_ref, v_ref, go_ref, d_ref, mo_ref, vo_ref):
        gv = g_ref[...]
        mn = B1 * m_ref[...] + (1.0 - B1) * gv
        vn = B2 * v_ref[...] + (1.0 - B2) * (gv * gv)
        m_hat = mn / (1.0 - B1 ** STEP)
        v_hat = vn / (1.0 - B2 ** STEP)
        go_ref[...] = gv
        d_ref[...] = -LR * (m_hat / (jnp.sqrt(v_hat) + ADAM_EPS) + WD * w_ref[...])
        mo_ref[...] = mn
        vo_ref[...] = vn

    blk = pl.BlockSpec((tr, c), lambda i: (i, 0))
    return pl.pallas_call(
        _after(body, 4, deps), grid=(r // tr,), in_specs=[blk] * 4 + _any_specs(deps), out_specs=[blk] * 4,
        out_shape=[_sds((r, c), F32)] * 4, compiler_params=_cp(("parallel",)), name=name)(w, g, m, v, *deps)


def _place():
    x, y, c = lax.axis_index("x"), lax.axis_index("y"), lax.axis_index("c")
    chips = [(1 - x, y), (x, 1 - y), (1 - x, 1 - y)]
    return x, y, c, chips


def _rows(ref, lead, h, rh):
    sl = pl.ds(pl.multiple_of(h * rh, 16), rh)
    return ref.at[sl, :] if lead is None else ref.at[lead, sl, :]


def _all_gather_weights(halved, whole):
    nh, nw = len(halved), len(whole)
    na = nh + nw
    arrays = list(halved) + list(whole)
    out_dtypes = [BF] * nh + [a.dtype for a in whole]
    cast_rows = 128

    def body(*refs):
        ins, outs = refs[:na], refs[na:2 * na]
        raw, stage = refs[2 * na:3 * na], refs[3 * na:3 * na + nh]
        ici_send, ici_recv, fwd_send, fwd_recv, in_sem, loc_sem = refs[3 * na + nh:]
        x, y, c, chips = _place()
        me = 2 * x + y
        sibling = (x, y, 1 - c)
        loads = [pltpu.make_async_copy(ins[a], raw[a], in_sem.at[a]) for a in range(na)]
        for cp in loads:
            cp.start()

        def ici(a, j, src_chip, src=None):
            if a < nh:
                rh = arrays[a].shape[0] // 2
                dst = _rows(outs[a], src_chip, c, rh)
                src = dst if src is None else _rows(src, None, c, rh)
            else:
                dst = outs[a].at[src_chip]
                src = dst if src is None else src
            return pltpu.make_async_remote_copy(
                src_ref=src, dst_ref=dst, send_sem=ici_send.at[3 * a + j], recv_sem=ici_recv.at[3 * a + j],
                device_id=(*chips[j], c), device_id_type=MESH)

        def fwd(a, j, half):
            rh = arrays[a].shape[0] // 2
            kj = 2 * chips[j][0] + chips[j][1]
            blk = _rows(outs[a], kj, half, rh)
            return pltpu.make_async_remote_copy(
                src_ref=blk, dst_ref=blk, send_sem=fwd_send.at[3 * a + j], recv_sem=fwd_recv.at[3 * a + j],
                device_id=sibling, device_id_type=MESH)

        local, sends = [], []
        for a in range(na):
            loads[a].wait()
            if a < nh:
                r = arrays[a].shape[0]
                for r0 in range(0, r, cast_rows):
                    r1 = min(r0 + cast_rows, r)
                    stage[a][r0:r1, :] = raw[a][r0:r1, :].astype(BF)
                own = stage[a]
            else:
                own = raw[a]
            cp = pltpu.make_async_copy(own, outs[a].at[me], loc_sem.at[a])
            cp.start()
            local.append(cp)
            for j in range(3):
                cp = ici(a, j, me, src=own)
                cp.start()
                sends.append(cp)
        passed = []
        for a in range(na):
            for j in range(3):
                kj = 2 * chips[j][0] + chips[j][1]
                ici(a, j, kj).wait_recv()
                if a < nh:
                    cp = fwd(a, j, c)
                    cp.start()
                    passed.append(cp)
        for a in range(nh):
            for j in range(3):
                fwd(a, j, 1 - c).wait_recv()
        for cp in sends + passed:
            cp.wait_send()
        for cp in local:
            cp.wait()

    any_spec = pl.BlockSpec(memory_space=pl.ANY)
    return pl.pallas_call(
        body, in_specs=[any_spec] * na, out_specs=[any_spec] * na,
        out_shape=[_sds((N_CHIPS,) + a.shape, dt) for a, dt in zip(arrays, out_dtypes)],
        scratch_shapes=[pltpu.VMEM(a.shape, a.dtype) for a in arrays] + [pltpu.VMEM(a.shape, BF) for a in halved]
        + [pltpu.SemaphoreType.DMA((3 * na,)), pltpu.SemaphoreType.DMA((3 * na,)),
           pltpu.SemaphoreType.DMA((3 * nh,)), pltpu.SemaphoreType.DMA((3 * nh,)),
           pltpu.SemaphoreType.DMA((na,)), pltpu.SemaphoreType.DMA((na,))],
        compiler_params=pltpu.CompilerParams(vmem_limit_bytes=VMEM_LIMIT_MB << 20),
        name="all_gather_weights")(*arrays)


def _sibling_halves(grads):
    na = len(grads)

    def body(*refs):
        ins, outs = refs[:na], refs[na:2 * na]
        send_sem, recv_sem = refs[2 * na:]
        x, y, c, _ = _place()
        copies = []
        for a in range(na):
            rh = grads[a].shape[1] // 2
            src = ins[a].at[:, pl.ds(pl.multiple_of((1 - c) * rh, 8), rh), :]
            copies.append(pltpu.make_async_remote_copy(
                src_ref=src, dst_ref=outs[a], send_sem=send_sem.at[a], recv_sem=recv_sem.at[a],
                device_id=(x, y, 1 - c), device_id_type=MESH))
        for cp in copies:
            cp.start()
        for cp in copies:
            cp.wait()

    any_spec = pl.BlockSpec(memory_space=pl.ANY)
    return pl.pallas_call(
        body, in_specs=[any_spec] * na, out_specs=[any_spec] * na,
        out_shape=[_sds((N_CHIPS, g.shape[1] // 2, g.shape[2]), F32) for g in grads],
        scratch_shapes=[pltpu.SemaphoreType.DMA((na,)), pltpu.SemaphoreType.DMA((na,))],
        name="sibling_halves")(*grads)


def _pair_sum(g, recv, c, tr, name):
    _, r, cols = g.shape
    rh = r // 2
    nr = rh // tr

    def body(c_ref, g_ref, r_ref, o_ref):
        o_ref[...] = (g_ref[...] + r_ref[...]).astype(BF)

    grid_spec = pltpu.PrefetchScalarGridSpec(
        num_scalar_prefetch=1, grid=(N_CHIPS, nr),
        in_specs=[pl.BlockSpec((1, tr, cols), lambda k, i, c_ref: (k, c_ref[0] * nr + i, 0)),
                  pl.BlockSpec((1, tr, cols), lambda k, i, c_ref: (k, i, 0))],
        out_specs=pl.BlockSpec((1, tr, cols), lambda k, i, c_ref: (k, i, 0)))
    return pl.pallas_call(
        body, grid_spec=grid_spec, out_shape=_sds((N_CHIPS, rh, cols), BF),
        compiler_params=_cp(("parallel", "parallel")), name=name)(c, g, recv)


def _chip_exchange(halves, small):
    na = len(halves)
    srows = small.shape[0]

    def body(*refs):
        ins, small_ref = refs[:na], refs[na]
        outs, small_out = refs[na + 1:2 * na + 1], refs[2 * na + 1]
        send_sem, recv_sem, s_send, s_recv = refs[2 * na + 2:]
        x, y, c, chips = _place()
        me = 4 * x + 2 * y + c
        copies = []
        for a in range(na):
            for j in range(3):
                kj = 2 * chips[j][0] + chips[j][1]
                copies.append(pltpu.make_async_remote_copy(
                    src_ref=ins[a].at[kj], dst_ref=outs[a].at[j],
                    send_sem=send_sem.at[3 * a + j], recv_sem=recv_sem.at[3 * a + j],
                    device_id=(*chips[j], c), device_id_type=MESH))
        for r in range(1, N_DEV):
            peer = (x ^ (r >> 2), y ^ ((r >> 1) & 1), c ^ (r & 1))
            copies.append(pltpu.make_async_remote_copy(
                src_ref=small_ref, dst_ref=small_out.at[me],
                send_sem=s_send.at[r - 1], recv_sem=s_recv.at[r - 1], device_id=peer, device_id_type=MESH))
        for cp in copies:
            cp.start()
        small_out[pl.ds(me, 1)] = small_ref[...][None]
        for cp in copies:
            cp.wait()

    any_spec = pl.BlockSpec(memory_space=pl.ANY)
    vmem = pl.BlockSpec(memory_space=pltpu.VMEM)
    return pl.pallas_call(
        body, in_specs=[any_spec] * na + [vmem], out_specs=[any_spec] * na + [vmem],
        out_shape=[_sds((3,) + h.shape[1:], h.dtype) for h in halves] + [_sds((N_DEV, srows, 128), F32)],
        scratch_shapes=[pltpu.SemaphoreType.DMA((3 * na,)), pltpu.SemaphoreType.DMA((3 * na,)),
                        pltpu.SemaphoreType.DMA((N_DEV - 1,)), pltpu.SemaphoreType.DMA((N_DEV - 1,))],
        name="chip_exchange")(*halves, small)


def _chip_sum(g, sib, recv, place, tr, name):
    _, r, cols = g.shape
    rh = r // 2
    nr = rh // tr

    def body(p_ref, g_ref, s_ref, r0_ref, r1_ref, r2_ref, o_ref):
        own = g_ref[0] + s_ref[0]
        o_ref[...] = ((own + r0_ref[0].astype(F32)) + r1_ref[0].astype(F32)) + r2_ref[0].astype(F32)

    rspec = lambda j: pl.BlockSpec((1, tr, cols), lambda i, p: (j, i, 0))
    grid_spec = pltpu.PrefetchScalarGridSpec(
        num_scalar_prefetch=1, grid=(nr,),
        in_specs=[pl.BlockSpec((1, tr, cols), lambda i, p: (p[0], p[1] * nr + i, 0)),
                  pl.BlockSpec((1, tr, cols), lambda i, p: (p[0], i, 0)), rspec(0), rspec(1), rspec(2)],
        out_specs=pl.BlockSpec((tr, cols), lambda i, p: (p[1] * nr + i, 0)))
    return pl.pallas_call(
        body, grid_spec=grid_spec, out_shape=_sds((r, cols), F32),
        compiler_params=_cp(("parallel",)), name=name)(place, g, sib, recv, recv, recv)


def _sibling_exchange(shards):
    na = len(shards)

    def body(*refs):
        ins, outs = refs[:na], refs[na:2 * na]
        send_sem, recv_sem = refs[2 * na:]
        x, y, c, _ = _place()
        for a in range(na):
            rh = shards[a].shape[0] // 2
            pltpu.make_async_remote_copy(
                src_ref=_rows(ins[a], None, c, rh), dst_ref=_rows(outs[a], None, c, rh),
                send_sem=send_sem.at[a], recv_sem=recv_sem.at[a],
                device_id=(x, y, 1 - c), device_id_type=MESH).start()
        for a in range(na):
            rh = shards[a].shape[0] // 2
            pltpu.make_async_remote_copy(
                src_ref=_rows(ins[a], None, c, rh), dst_ref=_rows(outs[a], None, 1 - c, rh),
                send_sem=send_sem.at[a], recv_sem=recv_sem.at[a],
                device_id=(x, y, 1 - c), device_id_type=MESH).wait()

    any_spec = pl.BlockSpec(memory_space=pl.ANY)
    return pl.pallas_call(
        body, in_specs=[any_spec] * na, out_specs=[any_spec] * na,
        out_shape=[_sds(h.shape, F32) for h in shards],
        input_output_aliases={a: a for a in range(na)},
        scratch_shapes=[pltpu.SemaphoreType.DMA((na,)), pltpu.SemaphoreType.DMA((na,))],
        name="sibling_exchange")(*shards)


def _device_sum(stack, deps=()):
    _, rows, _ = stack.shape

    def body(s_ref, o_ref):
        acc = s_ref[0]
        for d in range(1, N_DEV):
            acc = acc + s_ref[d]
        o_ref[...] = acc

    vmem = pl.BlockSpec(memory_space=pltpu.VMEM)
    return pl.pallas_call(_after(body, 1, deps), in_specs=[vmem] + _any_specs(deps), out_specs=vmem,
                          out_shape=_sds((rows, 128), F32), name="device_sum")(stack, *deps)


_HBM = pl.BlockSpec(memory_space=pltpu.HBM)
_SEM = pl.BlockSpec(memory_space=pltpu.SEMAPHORE)
_EFFECT = pltpu.SideEffectType.DATAFLOW_SIDE_EFFECTING


def _remote(src, dst, ssem, rsem, k, device):
    return pltpu.make_async_remote_copy(src_ref=src, dst_ref=dst, send_sem=ssem.at[k], recv_sem=rsem.at[k],
                                        device_id=device, device_id_type=MESH)


def _split_start(name, bufs, plan, n):
    nb = len(bufs)

    def body(*refs):
        sends, _ = plan(refs[:nb], refs[nb], refs[nb + 1])
        for cp in sends:
            cp.start()
        refs[-1][...] = jnp.zeros_like(refs[-1])

    res = pl.pallas_call(
        body, name=name,
        out_shape=(pltpu.SemaphoreType.DMA((n,)), pltpu.SemaphoreType.DMA((n,)))
        + tuple(pltpu.HBM(b.shape, b.dtype) for b in bufs) + (_sds((8, 128), F32),),
        in_specs=[_HBM] * nb,
        out_specs=(_SEM, _SEM) + (_HBM,) * nb + (pl.BlockSpec(memory_space=pltpu.VMEM),),
        input_output_aliases={i: i + 2 for i in range(nb)},
        compiler_params=pltpu.CompilerParams(has_side_effects=_EFFECT),
    )(*[pltpu.with_memory_space_constraint(b, pltpu.HBM) for b in bufs])
    return res[0], res[1], list(res[2:2 + nb]), res[2 + nb]


def _split_wait(name, send_sem, recv_sem, bufs, plan, after):
    nb = len(bufs)

    def body(*refs):
        sends, arrivals = plan(refs[:nb], refs[nb], refs[nb + 1])
        for cp in sends:
            cp.wait_send()
        for cp in arrivals:
            cp.wait_recv()

    res = pl.pallas_call(
        body, name=name, out_shape=tuple(pltpu.HBM(b.shape, b.dtype) for b in bufs),
        in_specs=[_HBM] * nb + [_SEM, _SEM, pl.BlockSpec(memory_space=pl.ANY)],
        out_specs=(_HBM,) * nb, input_output_aliases={i: i for i in range(nb)},
        compiler_params=pltpu.CompilerParams(has_side_effects=_EFFECT),
    )(*bufs, send_sem, recv_sem, after)
    return list(res)


def _plan_sibling_halves(shapes):
    na = len(shapes)

    def plan(refs, ssem, rsem):
        x, y, c, _ = _place()
        cps = []
        for a in range(na):
            rh = shapes[a][1] // 2
            src = refs[a].at[:, pl.ds(pl.multiple_of((1 - c) * rh, 8), rh), :]
            cps.append(_remote(src, refs[na + a], ssem, rsem, a, (x, y, 1 - c)))
        return cps, cps

    return plan


def _plan_chip_exchange(na):
    def plan(refs, ssem, rsem):
        _, _, c, chips = _place()
        cps = []
        for a in range(na):
            for j in range(3):
                kj = 2 * chips[j][0] + chips[j][1]
                cps.append(_remote(refs[a].at[kj], refs[na + a].at[j], ssem, rsem, 3 * a + j, (*chips[j], c)))
        return cps, cps

    return plan


def _plan_sibling_swap(shapes, with_small):
    def plan(refs, ssem, rsem):
        x, y, c, _ = _place()
        sends, arrivals = [], []
        for a, shp in enumerate(shapes):
            rh = shp[0] // 2
            mine, other = _rows(refs[a], None, c, rh), _rows(refs[a], None, 1 - c, rh)
            sends.append(_remote(mine, mine, ssem, rsem, a, (x, y, 1 - c)))
            arrivals.append(_remote(mine, other, ssem, rsem, a, (x, y, 1 - c)))
        if with_small:
            mine = refs[len(shapes)].at[4 * x + 2 * y + c]
            for r in range(1, N_DEV):
                peer = (x ^ (r >> 2), y ^ ((r >> 1) & 1), c ^ (r & 1))
                cp = _remote(mine, mine, ssem, rsem, len(shapes) + r - 1, peer)
                sends.append(cp)
                arrivals.append(cp)
        return sends, arrivals

    return plan


def _plan_gather_chips(shapes):
    def plan(refs, ssem, rsem):
        x, y, c, chips = _place()
        me = 2 * x + y
        sends, arrivals = [], []
        for a, shp in enumerate(shapes):
            rh = shp[1] // 2
            mine = _rows(refs[a], me, c, rh)
            for j in range(3):
                land = _rows(refs[a], 2 * chips[j][0] + chips[j][1], c, rh)
                sends.append(_remote(mine, mine, ssem, rsem, 3 * a + j, (*chips[j], c)))
                arrivals.append(_remote(land, land, ssem, rsem, 3 * a + j, (*chips[j], c)))
        return sends, arrivals

    return plan


def _plan_gather_sibling(shapes):
    def plan(refs, ssem, rsem):
        x, y, c, chips = _place()
        sends, arrivals = [], []
        for a, shp in enumerate(shapes):
            rh = shp[1] // 2
            for j in range(3):
                kj = 2 * chips[j][0] + chips[j][1]
                got, land = _rows(refs[a], kj, c, rh), _rows(refs[a], kj, 1 - c, rh)
                sends.append(_remote(got, got, ssem, rsem, 3 * a + j, (x, y, 1 - c)))
                arrivals.append(_remote(got, land, ssem, rsem, 3 * a + j, (x, y, 1 - c)))
        return sends, arrivals

    return plan


def _into_slice(w, k, n, tr, dtype, name):
    r, cols = w.shape

    def body(k_ref, w_ref, o_ref):
        o_ref[0] = w_ref[...].astype(dtype)

    grid_spec = pltpu.PrefetchScalarGridSpec(
        num_scalar_prefetch=1, grid=(r // tr,),
        in_specs=[pl.BlockSpec((tr, cols), lambda i, k: (i, 0))],
        out_specs=pl.BlockSpec((1, tr, cols), lambda i, k: (k[0], i, 0)))
    return pl.pallas_call(body, grid_spec=grid_spec, out_shape=_sds((n, r, cols), dtype),
                          compiler_params=_cp(("parallel",)), name=name)(k, w)


class _LateWeights:
    def __init__(self, shards, chip, names, tiles):
        bufs = [_into_slice(w, chip, N_CHIPS, t, BF, "own_" + nm) for w, nm, t in zip(shards, names, tiles)]
        self.n = 3 * len(bufs)
        self.chips, self.sibling = _plan_gather_chips([b.shape for b in bufs]), _plan_gather_sibling([b.shape for b in bufs])
        self.ssem, self.rsem, self.bufs, token = _split_start("gather_chips_start", bufs, self.chips, self.n)
        self.first = (token,)

    def middle(self, after):
        bufs = _split_wait("gather_chips_wait", self.ssem, self.rsem, self.bufs, self.chips, after)
        self.ssem, self.rsem, self.bufs, token = _split_start("gather_sibling_start", bufs, self.sibling, self.n)
        return (token,)

    def last(self, after):
        return _split_wait("gather_sibling_wait", self.ssem, self.rsem, self.bufs, self.sibling, after)


class _GradReduce:
    def __init__(self, tag, place, names, tiles):
        self.tag, self.place, self.names, self.tiles = tag, place, names, tiles
        self.small_all = None

    def first(self, grads):
        self.na = len(grads)
        self.p1 = _plan_sibling_halves([g.shape for g in grads])
        lands = [lax.empty((N_CHIPS, g.shape[1] // 2, g.shape[2]), F32) for g in grads]
        self.ssem, self.rsem, self.bufs, token = _split_start(
            self.tag + "_halves_start", list(grads) + lands, self.p1, self.na)
        return (token,)

    def second(self, after):
        bufs = _split_wait(self.tag + "_halves_wait", self.ssem, self.rsem, self.bufs, self.p1, after)
        self.grads, self.sib = bufs[:self.na], bufs[self.na:]
        halves = [_pair_sum(g, r, self.place[1:2], t, "pair_sum_" + nm)
                  for g, r, t, nm in zip(self.grads, self.sib, self.tiles, self.names)]
        lands = [lax.empty((3,) + h.shape[1:], h.dtype) for h in halves]
        self.p2 = _plan_chip_exchange(self.na)
        self.ssem, self.rsem, self.bufs, token = _split_start(
            self.tag + "_chips_start", halves + lands, self.p2, 3 * self.na)
        return (token,)

    def third(self, after, small=None):
        bufs = _split_wait(self.tag + "_chips_wait", self.ssem, self.rsem, self.bufs, self.p2, after)
        mine = [_chip_sum(g, sb, r, self.place, t, "chip_sum_" + nm)
                for g, sb, r, t, nm in zip(self.grads, self.sib, bufs[self.na:], self.tiles, self.names)]
        extra = [] if small is None else [small]
        self.p3 = _plan_sibling_swap([m.shape for m in mine], small is not None)
        self.ssem, self.rsem, self.bufs, token = _split_start(
            self.tag + "_swap_start", mine + extra, self.p3, self.na + (N_DEV - 1) * len(extra))
        return (token,)

    def last(self, after):
        bufs = _split_wait(self.tag + "_swap_wait", self.ssem, self.rsem, self.bufs, self.p3, after)
        if len(bufs) > self.na:
            self.small_all = bufs[self.na]
        return bufs[:self.na]


def _pack(parts, rows):
    flat = jnp.concatenate([p.reshape(-1) for p in parts])
    return jnp.pad(flat, (0, rows * 128 - flat.shape[0])).reshape(rows, 128)


def _unpack(buf, shapes):
    flat = buf.reshape(-1)
    out, off = [], 0
    for shp in shapes:
        n = 1
        for d in shp:
            n *= d
        out.append(flat[off:off + n].reshape(shp))
        off += n
    return out


def _rows_for(n):
    return -(-n // (8 * 128)) * 8


class _WeightsAtHand:
    def __init__(self, wup, wout, wdown):
        self.first, self.weights = (), [wup, wout, wdown]

    def middle(self, after):
        return ()

    def last(self, after):
        return self.weights


class _GradsKept:
    def first(self, grads):
        self.grads = list(grads)
        return ()

    def second(self, after):
        return ()

    def third(self, after, small=None):
        return ()

    def last(self, after):
        return self.grads


def _forward_backward(xs, pos, tgt, win, wpool, cw, attn_norm, b_gate, pool_scale, q_norm, k_norm, sinks,
                      ffn_norm, conv_b, late, early, rest):
    s = xs.shape[0]
    tm = min(512, s)
    tk = min(1024, s)
    inv_freq = ROPE_THETA ** (-jnp.arange(0, ROPE_DIM, 2, dtype=F32) / ROPE_DIM)
    lane = jnp.arange(2 * HEAD) % HEAD
    invf = jnp.where(lane < ROPE_DIM, inv_freq[lane % (ROPE_DIM // 2)], 0.0).reshape(1, 2 * HEAD)
    wq = jnp.tile(q_norm, (1, N_Q))
    wk = jnp.tile(k_norm, (1, N_KV))
    head_of = jnp.arange(Q_W) // HEAD
    bd = (head_of[:, None] == head_of[None, :]).astype(BF)
    sink = sinks[0]

    h1, u, qkv, gates = _attn_in_proj(xs, attn_norm, win, b_gate, tm, deps=late.first)
    qh, kh, vh = _qk_prep(qkv, pos, wq, wk, invf, bd, tm)
    battn = _attn_fwd(qh, kh, vh, sink, deps=late.middle(qh))
    apool, pooled = _pool_fwd(u, wpool, pool_scale, min(512, s))
    wup, wout, wdown = late.last(apool)
    wout = wout.reshape(D, D)
    wdown = wdown.reshape(D_FF, D)
    mix, y1, h2 = _mix_out_proj(apool, battn, gates, xs, wout, ffn_norm, tm)
    pre_g, pre_v, up_g, up_v, act = _ffn_up(h2, wup, cw, conv_b, tm)
    dy2, dy2b, loss_acc = _ffn_down_loss(act, wdown, y1, tgt, tm)

    d_wdown = _grad_matmul(act, dy2b, 512, tk, "grad_w_down")
    dp_g, dp_v, dcw_g, dcw_v, dcb_g, dcb_v = _ffn_act_bwd(dy2b, wdown, up_g, up_v, pre_g, pre_v, cw, tm)
    d_wup = _grad_matmul(h2, dp_g, UP_SHARD, tk, "grad_w_up_gate", lead=N_CHIPS)
    d_wup = _grad_matmul(h2, dp_v, UP_SHARD, tk, "grad_w_up_value", lead=N_CHIPS, prev=d_wup, lead_off=2)
    token = early.first([d_wdown.reshape(N_CHIPS, D_FF // N_CHIPS, D), d_wup])
    dy1, d_ffn_norm = _ffn_up_bwd(dp_g, dp_v, wup, y1, dy2, ffn_norm, tm, deps=token)
    token = early.second(dy1)
    da, db, dzg, d_bgate, d_wout = _out_proj_bwd(dy1, wout, apool, battn, gates, mix, tm, deps=token)
    du, d_wpool, d_pscale = _pool_bwd(da, pooled, wpool, pool_scale, min(512, s))
    token = early.third(du)
    dqh, dkh, dvh, dsink = _attn_bwd(qh, kh, vh, sink, db, deps=token)
    dzq, d_qn, d_kn = _qk_prep_bwd(dqh, dkh, dvh, qkv, pos, wq, wk, invf, bd, tm)
    d_win_t = jnp.concatenate([
        _grad_matmul(du, h1, D, tk, "grad_w_in_pool"),
        _grad_matmul(dzq, h1, D, tk, "grad_w_in_qkv"),
        _grad_matmul(dzg, h1, D, tk, "grad_w_in_gates")], axis=0)
    token = rest.first([
        d_win_t.reshape(N_CHIPS, IN_W // N_CHIPS, D),
        d_wout.reshape(N_CHIPS, D // N_CHIPS, D),
        d_wpool.reshape(4, N_CHIPS, 64, POOL_GROUP).transpose(1, 0, 2, 3).reshape(N_CHIPS, 4 * 64, POOL_GROUP)])
    token = rest.second(token[0] if token else None)
    grad_x, d_attn_norm = _in_proj_bwd(du, dzq, dzg, win, xs, attn_norm, dy1, tm, deps=token)
    small_parts = [d_attn_norm, d_bgate, d_pscale, d_qn, d_kn, dsink[:, 0, 0:GQA], d_ffn_norm,
                   jnp.concatenate([dcb_g, dcb_v], axis=1), jnp.concatenate([dcw_g, dcw_v], axis=0)]
    return loss_acc, grad_x, small_parts


def kernel(x, positions, attn_norm, w_in, b_gate, w_pool, pool_scale, q_norm, k_norm, sinks, w_out, ffn_norm, w_up, conv_w, conv_b, w_down, loss_target, m_attn_norm, m_w_in, m_b_gate, m_w_pool, m_pool_scale, m_q_norm, m_k_norm, m_sinks, m_w_out, m_ffn_norm, m_w_up, m_conv_w, m_conv_b, m_w_down, v_attn_norm, v_w_in, v_b_gate, v_w_pool, v_pool_scale, v_q_norm, v_k_norm, v_sinks, v_w_out, v_ffn_norm, v_w_up, v_conv_w, v_conv_b, v_w_down):
    s = x.shape[1]
    xs = x[0]
    tgt = loss_target[0]
    pos = positions[0].reshape(s, 1)
    cx, cy, cc = lax.axis_index("x"), lax.axis_index("y"), lax.axis_index("c")
    chip = 2 * cx + cy

    chip_arr = chip.reshape(1).astype(jnp.int32)
    dev_arr = (2 * chip + cc).reshape(1).astype(jnp.int32)
    place = jnp.stack([chip, cc]).astype(jnp.int32)

    g_in, g_pool, g_cw = _all_gather_weights(
        [jnp.swapaxes(w_in[0], 0, 1), w_pool[0].reshape(4 * 64, POOL_GROUP)], [conv_w[0]])
    win = g_in.reshape(IN_W, D)
    wpool = g_pool.reshape(N_CHIPS, 4, 64, POOL_GROUP).transpose(1, 0, 2, 3).reshape(4, POOL_GROUP, POOL_GROUP)
    late = _LateWeights([w_up[0], w_out[0], w_down[0]], chip_arr, ["w_up", "w_out", "w_down"], [256, 256, 352])
    early = _GradReduce("early", place, ["w_down", "w_up"], [176, 256])
    rest = _GradReduce("rest", place, ["w_in", "w_out", "w_pool"], [272, 128, 128])

    loss_acc, grad_x, small_parts = _forward_backward(
        xs, pos, tgt, win, wpool, g_cw, attn_norm, b_gate, pool_scale, q_norm, k_norm, sinks, ffn_norm, conv_b,
        late, early, rest)
    loss = lax.psum(jnp.sum(loss_acc) * (0.5 / D), ("x", "y", "c"))

    def two_d(a):
        return a.reshape(-1, a.shape[-1])

    def update(nm, w, g, m, v, tr, deps=()):
        res = _adamw(two_d(w), g, two_d(m), two_d(v), tr, "adamw_" + nm, deps=deps)
        return [r.reshape(w.shape) for r in res]

    small_shapes = [(1, D), (1, 2 * D), (1, D), (1, HEAD), (1, HEAD), (1, N_Q), (1, D), (1, 2 * D_FF),
                    (N_CHIPS, 3, UP_SHARD)]
    n_small = sum(p.size for p in small_parts)
    small = _into_slice(_pack(small_parts, _rows_for(n_small)), dev_arr, N_DEV, _rows_for(n_small), F32, "own_small")
    g_wdown, g_wup = early.last(grad_x)
    big_out = {"w_up": update("w_up", w_up, g_wup, m_w_up, v_w_up, 256)}
    big_out["w_down"] = update("w_down", w_down, g_wdown, m_w_down, v_w_down, 176, deps=(big_out["w_up"][1],))
    token = rest.third(big_out["w_down"][1], small=small)
    g_win_t, g_wout, g_wpool = rest.last(token[0])
    small_sum = _device_sum(rest.small_all)
    (g_attn_norm, g_bgate, g_pscale, g_qn, g_kn, g_sinks, g_ffn_norm, g_convb, g_convw_all) = _unpack(
        small_sum, small_shapes)
    g_convw = lax.dynamic_index_in_dim(g_convw_all, chip, axis=0, keepdims=False)

    small_names = ["attn_norm", "b_gate", "pool_scale", "q_norm", "k_norm", "sinks", "ffn_norm", "conv_b", "conv_w"]
    sm_w = [attn_norm, b_gate, pool_scale, q_norm, k_norm, sinks, ffn_norm, conv_b, conv_w]
    sm_m = [m_attn_norm, m_b_gate, m_pool_scale, m_q_norm, m_k_norm, m_sinks, m_ffn_norm, m_conv_b, m_conv_w]
    sm_v = [v_attn_norm, v_b_gate, v_pool_scale, v_q_norm, v_k_norm, v_sinks, v_ffn_norm, v_conv_b, v_conv_w]
    sm_g = [g_attn_norm, g_bgate, g_pscale, g_qn, g_kn, g_sinks, g_ffn_norm, g_convb, g_convw]
    sm_rows = _rows_for(sum(w.size for w in sm_w))
    res = _adamw(_pack(sm_w, sm_rows), _pack(sm_g, sm_rows), _pack(sm_m, sm_rows), _pack(sm_v, sm_rows),
                 sm_rows, "adamw_small")
    sm_out = [_unpack(r, [w.shape for w in sm_w]) for r in res]
    small_out = {nm: [sm_out[k][i] for k in range(4)] for i, nm in enumerate(small_names)}
    flip = lambda a: jnp.swapaxes(a[0], 0, 1)
    res = _adamw(flip(w_in), g_win_t, flip(m_w_in), flip(v_w_in), 272, "adamw_w_in")
    big_out["w_in"] = [jnp.swapaxes(r, 0, 1)[None] for r in res]
    big_out["w_out"] = update("w_out", w_out, g_wout, m_w_out, v_w_out, 128)
    big_out["w_pool"] = update("w_pool", w_pool, g_wpool, m_w_pool, v_w_pool, 128)

    order = ["attn_norm", "w_in", "b_gate", "w_pool", "pool_scale", "q_norm", "k_norm", "sinks", "w_out",
             "ffn_norm", "w_up", "conv_w", "conv_b", "w_down"]
    allout = {**big_out, **small_out}
    outs = [loss, grad_x[None]]
    for k in range(4):
        outs += [allout[nm][k] for nm in order]
    return tuple(outs)
```

```python
import functools

import jax
import jax.numpy as jnp
from jax import lax
from jax.experimental import pallas as pl
from jax.experimental.pallas import tpu as pltpu

D = 1024
D_FF = 2816
HEAD = 64
N_Q = 16
N_KV = 2
GQA = 8
BLK = 128
ROPE_DIM = 16
ROPE_THETA = 500000.0
POOL_GROUP = 256
Q_W = 1024
KV_W = 128
QKV_W = Q_W + 2 * KV_W
IN_W = 4352
UP_SHARD = 1408
EPS = 1e-6
N_CHIPS = 4
N_DEV = 8

LR = 0.001
B1 = 0.9
B2 = 0.999
ADAM_EPS = 1e-08
WD = 0.01
STEP = 10

BF = jnp.bfloat16
F32 = jnp.float32
MESH = pl.DeviceIdType.MESH
VMEM_LIMIT_MB = 56


def _cp(sem, vmem_mb=VMEM_LIMIT_MB):
    return pltpu.CompilerParams(dimension_semantics=sem, vmem_limit_bytes=vmem_mb << 20)


def _full(shape):
    nd = len(shape)
    return pl.BlockSpec(shape, lambda *_: (0,) * nd)


def _sds(shape, dtype):
    return jax.ShapeDtypeStruct(shape, dtype)


def _after(body, n_in, deps):
    nd = len(deps)
    if nd == 0:
        return body

    def ordered(*refs):
        return body(*refs[:n_in], *refs[n_in + nd:])

    return ordered


def _any_specs(deps):
    return [pl.BlockSpec(memory_space=pl.ANY)] * len(deps)


def _nt(a, b):
    return lax.dot_general(a, b, (((1,), (1,)), ((), ())), preferred_element_type=F32)


def _tn(a, b):
    return lax.dot_general(a, b, (((0,), (0,)), ((), ())), preferred_element_type=F32)


def _mm(a, b):
    return jnp.dot(a, b, preferred_element_type=F32)


def _head_sum(v, bd):
    return _mm(v.astype(BF), bd)


def _rope_tables(pos_ref, invf_ref):
    ang = pos_ref[...].astype(F32) * invf_ref[...]
    cos = jnp.cos(ang)
    sin = jnp.sin(ang)
    lane = lax.broadcasted_iota(jnp.int32, (1, 2 * HEAD), 1) % HEAD
    sa = jnp.where(lane < ROPE_DIM // 2, -sin, 0.0)
    sb = jnp.where(lane < ROPE_DIM // 2, 0.0, jnp.where(lane < ROPE_DIM, sin, 0.0))
    return cos, sa, sb


def _tile_lanes(t, reps):
    return t if reps == 1 else jnp.tile(t, (1, reps))


def _rope(v, cos, sa, sb):
    w = v.shape[1]
    reps = w // (2 * HEAD)
    half = ROPE_DIM // 2
    return (v * _tile_lanes(cos, reps) + pltpu.roll(v, w - half, 1) * _tile_lanes(sa, reps)
            + pltpu.roll(v, half, 1) * _tile_lanes(sb, reps))


def _rope_t(dy, cos, sa, sb):
    w = dy.shape[1]
    reps = w // (2 * HEAD)
    half = ROPE_DIM // 2
    return (dy * _tile_lanes(cos, reps) + pltpu.roll(dy * _tile_lanes(sa, reps), half, 1)
            + pltpu.roll(dy * _tile_lanes(sb, reps), w - half, 1))


def _attn_in_proj(x, g1, w_in, b_gate, tm, deps=()):
    s = x.shape[0]

    def body(x_ref, g_ref, w_ref, b_ref, h_ref, u_ref, qkv_ref, gate_ref):
        xv = x_ref[...]
        r = lax.rsqrt(jnp.mean(xv * xv, axis=-1, keepdims=True) + EPS)
        h = (xv * r * g_ref[...]).astype(BF)
        h_ref[...] = h
        u_ref[...] = _nt(h, w_ref[0:D, :])
        qkv_ref[...] = _nt(h, w_ref[D:D + QKV_W, :])
        gate_ref[...] = jax.nn.sigmoid(_nt(h, w_ref[D + QKV_W:IN_W, :]) + b_ref[...]).astype(BF)

    row = lambda w: pl.BlockSpec((tm, w), lambda i: (i, 0))
    return pl.pallas_call(
        _after(body, 4, deps), grid=(s // tm,),
        in_specs=[row(D), _full((1, D)), _full((IN_W, D)), _full((1, 2 * D))] + _any_specs(deps),
        out_specs=[row(D), row(D), row(QKV_W), row(2 * D)],
        out_shape=[_sds((s, D), BF), _sds((s, D), F32), _sds((s, QKV_W), F32), _sds((s, 2 * D), BF)],
        compiler_params=_cp(("parallel",)), name="attn_in_proj")(x, g1, w_in, b_gate, *deps)


def _qk_prep(qkv, pos, wq, wk, invf, bd, tm):
    s = qkv.shape[0]

    def body(qkv_ref, pos_ref, wq_ref, wk_ref, invf_ref, bd_ref, qh_ref, kh_ref, vh_ref):
        cos, sa, sb = _rope_tables(pos_ref, invf_ref)
        q = qkv_ref[:, 0:Q_W]
        k = qkv_ref[:, Q_W:Q_W + KV_W]
        v = qkv_ref[:, Q_W + KV_W:QKV_W]
        rq = lax.rsqrt(_head_sum(q * q, bd_ref[...]) * (1.0 / HEAD) + EPS)
        qr = _rope(q * rq * wq_ref[...], cos, sa, sb) * (HEAD ** -0.5)
        rk = lax.rsqrt(_head_sum(k * k, bd_ref[0:KV_W, 0:KV_W]) * (1.0 / HEAD) + EPS)
        kr = _rope(k * rk * wk_ref[...], cos, sa, sb)
        for h in range(N_Q):
            qh_ref[h] = qr[:, HEAD * h:HEAD * (h + 1)].astype(BF)
        for h in range(N_KV):
            kh_ref[h] = kr[:, HEAD * h:HEAD * (h + 1)].astype(BF)
            vh_ref[h] = v[:, HEAD * h:HEAD * (h + 1)].astype(BF)

    heads = lambda n: pl.BlockSpec((n, tm, HEAD), lambda i: (0, i, 0))
    return pl.pallas_call(
        body, grid=(s // tm,),
        in_specs=[pl.BlockSpec((tm, QKV_W), lambda i: (i, 0)), pl.BlockSpec((tm, 1), lambda i: (i, 0)),
                  _full((1, Q_W)), _full((1, KV_W)), _full((1, 2 * HEAD)), _full((Q_W, Q_W))],
        out_specs=[heads(N_Q), heads(N_KV), heads(N_KV)],
        out_shape=[_sds((N_Q, s, HEAD), BF), _sds((N_KV, s, HEAD), BF), _sds((N_KV, s, HEAD), BF)],
        compiler_params=_cp(("parallel",)), name="qk_prep")(qkv, pos, wq, wk, invf, bd)


def _sink_column(sink_ref, kh):
    row_g = lax.broadcasted_iota(jnp.int32, (GQA * BLK, 1), 0) // BLK
    col = jnp.zeros((GQA * BLK, 1), F32)
    for g in range(GQA):
        col = jnp.where(row_g == g, sink_ref[kh * GQA + g], col)
    return col


def _attn_probs(q, k, n, sink_col):
    sc = _nt(q, k)
    qi = lax.broadcasted_iota(jnp.int32, sc.shape, 0) % BLK + BLK
    ki = lax.broadcasted_iota(jnp.int32, sc.shape, 1)
    lo = jnp.where(n > 0, qi - BLK, BLK - 1)
    ok = (ki <= qi) & (ki > lo)
    sc = jnp.where(ok, sc, -jnp.inf)
    m = jnp.maximum(jnp.max(sc, axis=-1, keepdims=True), sink_col)
    p = jnp.exp(sc - m)
    es = jnp.exp(sink_col - m)
    inv = 1.0 / (jnp.sum(p, axis=-1, keepdims=True) + es)
    return p * inv, es * inv


def _attn_fwd(qh, kh, vh, sinks, deps=()):
    s = qh.shape[1]
    nb = s // BLK

    def body(sink_ref, q_ref, kp_ref, kc_ref, vp_ref, vc_ref, o_ref):
        n = pl.program_id(0)
        for khd in range(N_KV):
            q = q_ref[khd * GQA:(khd + 1) * GQA].reshape(GQA * BLK, HEAD)
            k = jnp.concatenate([kp_ref[khd], kc_ref[khd]], axis=0)
            v = jnp.concatenate([vp_ref[khd], vc_ref[khd]], axis=0)
            probs, _ = _attn_probs(q, k, n, _sink_column(sink_ref, khd))
            o = _mm(probs.astype(BF), v)
            for j in range(GQA // 2):
                c0 = khd * GQA * HEAD + 2 * HEAD * j
                o_ref[:, c0:c0 + 2 * HEAD] = jnp.concatenate(
                    [o[2 * j * BLK:(2 * j + 1) * BLK], o[(2 * j + 1) * BLK:(2 * j + 2) * BLK]], axis=1).astype(BF)

    prev = pl.BlockSpec((N_KV, BLK, HEAD), lambda n: (0, jnp.maximum(n - 1, 0), 0))
    cur = pl.BlockSpec((N_KV, BLK, HEAD), lambda n: (0, n, 0))
    return pl.pallas_call(
        _after(body, 6, deps), grid=(nb,),
        in_specs=[pl.BlockSpec(memory_space=pltpu.SMEM),
                  pl.BlockSpec((N_Q, BLK, HEAD), lambda n: (0, n, 0)), prev, cur, prev, cur] + _any_specs(deps),
        out_specs=pl.BlockSpec((BLK, Q_W), lambda n: (n, 0)),
        out_shape=_sds((s, Q_W), BF),
        compiler_params=_cp(("parallel",)), name="attn_fwd")(sinks, qh, kh, kh, vh, vh, *deps)


def _pool_fwd(u, w_pool, pool_scale, ts):
    s = u.shape[0]
    halo = 16

    def body(u_ref, wp_ref, ps_ref, a_ref, pooled_ref, prev):
        g = pl.program_id(0)
        i = pl.program_id(1)

        @pl.when(i == 0)
        def _():
            prev[...] = jnp.zeros_like(prev)

        cur = u_ref[...]
        ext = jnp.concatenate([prev[...], cur], axis=0)
        t = (i * ts + lax.broadcasted_iota(jnp.int32, (ts, 1), 0)).astype(F32)
        for gi in range(4):
            @pl.when(g == gi)
            def _(gi=gi):
                w = 2 << gi
                acc, span = ext, 1
                while span < w:
                    acc = acc + pltpu.roll(acc, span, 0)
                    span *= 2
                inv = 1.0 / jnp.minimum(t + 1.0, float(w))
                pooled = (acc[halo:halo + ts] * inv - cur).astype(BF)
                pooled_ref[...] = pooled
                a_ref[...] = (_mm(pooled, wp_ref[0]) * ps_ref[...]).astype(BF)

        prev[...] = cur[ts - halo:ts]

    col = pl.BlockSpec((ts, POOL_GROUP), lambda g, i: (i, g))
    return pl.pallas_call(
        body, grid=(4, s // ts),
        in_specs=[col, pl.BlockSpec((1, POOL_GROUP, POOL_GROUP), lambda g, i: (g, 0, 0)),
                  pl.BlockSpec((1, POOL_GROUP), lambda g, i: (0, g))],
        out_specs=[col, col],
        out_shape=[_sds((s, D), BF), _sds((s, D), BF)],
        scratch_shapes=[pltpu.VMEM((halo, POOL_GROUP), F32)],
        compiler_params=_cp(("parallel", "arbitrary")), name="pool_fwd")(u, w_pool, pool_scale)


def _mix_out_proj(a, b, gates, x, w_out, g2, tm):
    s = x.shape[0]

    def body(a_ref, b_ref, gate_ref, x_ref, w_ref, g_ref, mix_ref, y_ref, h_ref):
        mix = (gate_ref[:, 0:D].astype(F32) * a_ref[...].astype(F32)
               + gate_ref[:, D:2 * D].astype(F32) * b_ref[...].astype(F32)).astype(BF)
        mix_ref[...] = mix
        y = x_ref[...] + _mm(mix, w_ref[...])
        y_ref[...] = y
        r = lax.rsqrt(jnp.mean(y * y, axis=-1, keepdims=True) + EPS)
        h_ref[...] = (y * r * g_ref[...]).astype(BF)

    row = lambda w: pl.BlockSpec((tm, w), lambda i: (i, 0))
    return pl.pallas_call(
        body, grid=(s // tm,),
        in_specs=[row(D), row(D), row(2 * D), row(D), _full((D, D)), _full((1, D))],
        out_specs=[row(D), row(D), row(D)],
        out_shape=[_sds((s, D), BF), _sds((s, D), F32), _sds((s, D), BF)],
        compiler_params=_cp(("parallel",)), name="mix_out_proj")(a, b, gates, x, w_out, g2)


def _ffn_up(h2, w_up, conv_w, conv_b, tm):
    s = h2.shape[0]

    def body(h_ref, wg_ref, wv_ref, cwg_ref, cwv_ref, cbg_ref, cbv_ref,
             preg_ref, prev_ref, upg_ref, upv_ref, act_ref, halog, halov):
        i = pl.program_id(1)

        @pl.when(i == 0)
        def _():
            halog[...] = jnp.zeros_like(halog)
            halov[...] = jnp.zeros_like(halov)

        h = h_ref[...]

        def conv_half(w_ref, cw_ref, cb_ref, halo, pre_ref, up_ref):
            pre = _mm(h, w_ref[0])
            pre_ref[...] = pre.astype(BF)
            ext = jnp.concatenate([halo[...], pre], axis=0)
            cw = cw_ref[0]
            up = cb_ref[...] + cw[0:1] * pltpu.roll(ext, 2, 0)[8:8 + tm]
            up = up + cw[1:2] * pltpu.roll(ext, 1, 0)[8:8 + tm]
            up = up + cw[2:3] * pre
            halo[...] = pre[tm - 8:tm]
            up_ref[...] = up.astype(BF)
            return up

        gate = conv_half(wg_ref, cwg_ref, cbg_ref, halog, preg_ref, upg_ref)
        val = conv_half(wv_ref, cwv_ref, cbv_ref, halov, prev_ref, upv_ref)
        act_ref[...] = (gate * jax.nn.sigmoid(gate) * val).astype(BF)

    tile = pl.BlockSpec((tm, UP_SHARD), lambda j, i: (i, j))
    wspec = lambda off: pl.BlockSpec((1, D, UP_SHARD), lambda j, i: (j + off, 0, 0))
    cwspec = lambda off: pl.BlockSpec((1, 3, UP_SHARD), lambda j, i: (j + off, 0, 0))
    cbspec = lambda off: pl.BlockSpec((1, UP_SHARD), lambda j, i: (0, j + off))
    half = _sds((s, D_FF), BF)
    return pl.pallas_call(
        body, grid=(2, s // tm),
        in_specs=[pl.BlockSpec((tm, D), lambda j, i: (i, 0)), wspec(0), wspec(2), cwspec(0), cwspec(2),
                  cbspec(0), cbspec(2)],
        out_specs=[tile] * 5, out_shape=[half] * 5,
        scratch_shapes=[pltpu.VMEM((8, UP_SHARD), F32), pltpu.VMEM((8, UP_SHARD), F32)],
        compiler_params=_cp(("parallel", "arbitrary")), name="ffn_up")(
            h2, w_up, w_up, conv_w, conv_w, conv_b, conv_b)


def _ffn_down_loss(act, w_down, y1, tgt, tm):
    s = y1.shape[0]

    def body(act_ref, w_ref, y_ref, t_ref, dy_ref, dyb_ref, loss_ref):
        @pl.when(pl.program_id(0) == 0)
        def _():
            loss_ref[...] = jnp.zeros_like(loss_ref)

        e = y_ref[...] + _mm(act_ref[...], w_ref[...]) - t_ref[...]
        dy = e * (1.0 / D)
        dy_ref[...] = dy
        dyb_ref[...] = dy.astype(BF)
        e2 = (e * e).reshape(tm // 8, 8, D).sum(axis=0)
        part = e2[:, 0:128]
        for j in range(1, D // 128):
            part = part + e2[:, 128 * j:128 * (j + 1)]
        loss_ref[...] += part

    row = lambda w: pl.BlockSpec((tm, w), lambda i: (i, 0))
    return pl.pallas_call(
        body, grid=(s // tm,),
        in_specs=[row(D_FF), _full((D_FF, D)), row(D), row(D)],
        out_specs=[row(D), row(D), _full((8, 128))],
        out_shape=[_sds((s, D), F32), _sds((s, D), BF), _sds((8, 128), F32)],
        compiler_params=_cp(("arbitrary",)), name="ffn_down_loss")(act, w_down, y1, tgt)


def _grad_matmul(a, b, tn, tk, name, lead=None, prev=None, lead_off=0):
    s, m = a.shape
    n = b.shape[1]
    nj = n // tn

    def body(*refs):
        a_ref, b_ref = refs[0], refs[1]
        o_ref = refs[-1]
        acc = _tn(a_ref[...], b_ref[...])
        acc = acc if lead is None else acc[None]

        @pl.when(pl.program_id(1) == 0)
        def _():
            o_ref[...] = acc

        @pl.when(pl.program_id(1) > 0)
        def _():
            o_ref[...] += acc

    in_specs = [pl.BlockSpec((tk, m), lambda j, k: (k, 0)), pl.BlockSpec((tk, tn), lambda j, k: (k, j))]
    args = [a, b]
    aliases = {}
    if lead is None:
        out_spec = pl.BlockSpec((m, tn), lambda j, k: (0, j))
        out_shape = _sds((m, n), F32)
    else:
        out_spec = pl.BlockSpec((1, m, tn), lambda j, k: (j + lead_off, 0, 0))
        out_shape = _sds((lead, m, tn), F32)
        if prev is not None:
            in_specs.append(pl.BlockSpec(memory_space=pl.ANY))
            args.append(prev)
            aliases = {2: 0}
    return pl.pallas_call(
        body, grid=(nj, s // tk), in_specs=in_specs, out_specs=out_spec, out_shape=out_shape,
        input_output_aliases=aliases,
        compiler_params=_cp(("parallel", "arbitrary")), name=name)(*args)


def _ffn_act_bwd(dyb, w_down, up_g, up_v, pre_g, pre_v, conv_w, tm):
    s = dyb.shape[0]
    nt = s // tm

    def body(dy_ref, wd_ref, upg_ref, upv_ref, preg_ref, prev_ref, cwg_ref, cwv_ref,
             dpg_ref, dpv_ref, dcwg_ref, dcwv_ref, dcbg_ref, dcbv_ref, nxg, nxv):
        i = pl.program_id(1)

        @pl.when(i == 0)
        def _():
            nxg[...] = jnp.zeros_like(nxg)
            nxv[...] = jnp.zeros_like(nxv)
            dcwg_ref[...] = jnp.zeros_like(dcwg_ref)
            dcwv_ref[...] = jnp.zeros_like(dcwv_ref)
            dcbg_ref[...] = jnp.zeros_like(dcbg_ref)
            dcbv_ref[...] = jnp.zeros_like(dcbv_ref)

        dact = _nt(dy_ref[...], wd_ref[...])
        g = upg_ref[...].astype(F32)
        v = upv_ref[...].astype(F32)
        sg = jax.nn.sigmoid(g)
        d_v = dact * (g * sg)
        d_g = dact * v * (sg * (1.0 + g * (1.0 - sg)))

        def conv_bwd(d_up, nx, pre_ref, cw_ref, dp_ref, dcw_ref, dcb_ref):
            ext = jnp.concatenate([d_up, nx[...]], axis=0)
            s1 = pltpu.roll(ext, tm + 8 - 1, 0)[0:tm]
            s2 = pltpu.roll(ext, tm + 8 - 2, 0)[0:tm]
            cw = cw_ref[0]
            dp_ref[...] = (cw[2:3] * d_up + cw[1:2] * s1 + cw[0:1] * s2).astype(BF)
            nx[...] = d_up[0:8]
            pre = pre_ref[...].astype(F32)
            dcw_ref[0, 0:1, :] += jnp.sum(s2 * pre, axis=0, keepdims=True)
            dcw_ref[0, 1:2, :] += jnp.sum(s1 * pre, axis=0, keepdims=True)
            dcw_ref[0, 2:3, :] += jnp.sum(d_up * pre, axis=0, keepdims=True)
            dcb_ref[...] += jnp.sum(d_up, axis=0, keepdims=True)

        conv_bwd(d_g, nxg, preg_ref, cwg_ref, dpg_ref, dcwg_ref, dcbg_ref)
        conv_bwd(d_v, nxv, prev_ref, cwv_ref, dpv_ref, dcwv_ref, dcbv_ref)

    tile = pl.BlockSpec((tm, UP_SHARD), lambda j, i: (nt - 1 - i, j))
    cwspec = lambda off: pl.BlockSpec((1, 3, UP_SHARD), lambda j, i: (j + off, 0, 0))
    acc_cw = pl.BlockSpec((1, 3, UP_SHARD), lambda j, i: (j, 0, 0))
    acc_cb = pl.BlockSpec((1, UP_SHARD), lambda j, i: (0, j))
    buf = pltpu.VMEM((8, UP_SHARD), F32)
    return pl.pallas_call(
        body, grid=(2, nt),
        in_specs=[pl.BlockSpec((tm, D), lambda j, i: (nt - 1 - i, 0)),
                  pl.BlockSpec((UP_SHARD, D), lambda j, i: (j, 0)),
                  tile, tile, tile, tile, cwspec(0), cwspec(2)],
        out_specs=[tile, tile, acc_cw, acc_cw, acc_cb, acc_cb],
        out_shape=[_sds((s, D_FF), BF), _sds((s, D_FF), BF), _sds((2, 3, UP_SHARD), F32),
                   _sds((2, 3, UP_SHARD), F32), _sds((1, D_FF), F32), _sds((1, D_FF), F32)],
        scratch_shapes=[buf, buf],
        compiler_params=_cp(("parallel", "arbitrary")), name="ffn_act_bwd")(
            dyb, w_down, up_g, up_v, pre_g, pre_v, conv_w, conv_w)


def _rms_bwd(dh, y, g):
    r = lax.rsqrt(jnp.mean(y * y, axis=-1, keepdims=True) + EPS)
    n = y * r
    dn = dh * g
    return r * (dn - n * jnp.mean(dn * n, axis=-1, keepdims=True)), dh * n


def _ffn_up_bwd(dp_g, dp_v, w_up, y1, dy2, g2, tm, deps=()):
    s = y1.shape[0]

    def body(dg_ref, dv_ref, w_ref, y_ref, dy2_ref, g_ref, dy1_ref, dgn_ref):
        @pl.when(pl.program_id(0) == 0)
        def _():
            dgn_ref[...] = jnp.zeros_like(dgn_ref)

        dh = _nt(dg_ref[:, 0:UP_SHARD], w_ref[0])
        dh = dh + _nt(dg_ref[:, UP_SHARD:D_FF], w_ref[1])
        dh = dh + _nt(dv_ref[:, 0:UP_SHARD], w_ref[2])
        dh = dh + _nt(dv_ref[:, UP_SHARD:D_FF], w_ref[3])
        dy, dgn = _rms_bwd(dh, y_ref[...], g_ref[...])
        dy1_ref[...] = dy2_ref[...] + dy
        dgn_ref[...] += jnp.sum(dgn, axis=0, keepdims=True)

    row = lambda w: pl.BlockSpec((tm, w), lambda i: (i, 0))
    return pl.pallas_call(
        _after(body, 6, deps), grid=(s // tm,),
        in_specs=[row(D_FF), row(D_FF), _full((4, D, UP_SHARD)), row(D), row(D), _full((1, D))] + _any_specs(deps),
        out_specs=[row(D), _full((1, D))],
        out_shape=[_sds((s, D), F32), _sds((1, D), F32)],
        compiler_params=_cp(("arbitrary",)), name="ffn_up_bwd")(dp_g, dp_v, w_up, y1, dy2, g2, *deps)


def _out_proj_bwd(dy1, w_out, a, b, gates, mix, tm, deps=()):
    s = dy1.shape[0]

    def body(dy_ref, w_ref, a_ref, b_ref, gate_ref, mix_ref, da_ref, db_ref, dzg_ref, dbg_ref, dw_ref):
        @pl.when(pl.program_id(0) == 0)
        def _():
            dbg_ref[...] = jnp.zeros_like(dbg_ref)
            dw_ref[...] = jnp.zeros_like(dw_ref)

        dyb = dy_ref[...].astype(BF)
        dmix = _nt(dyb, w_ref[...])
        gp = gate_ref[:, 0:D].astype(F32)
        ga = gate_ref[:, D:2 * D].astype(F32)
        da_ref[...] = (dmix * gp).astype(BF)
        db_ref[...] = (dmix * ga).astype(BF)
        dzp = dmix * a_ref[...].astype(F32) * (gp * (1.0 - gp))
        dza = dmix * b_ref[...].astype(F32) * (ga * (1.0 - ga))
        dzg_ref[:, 0:D] = dzp.astype(BF)
        dzg_ref[:, D:2 * D] = dza.astype(BF)
        dbg_ref[:, 0:D] += jnp.sum(dzp, axis=0, keepdims=True)
        dbg_ref[:, D:2 * D] += jnp.sum(dza, axis=0, keepdims=True)
        dw_ref[...] += _tn(mix_ref[...], dyb)

    row = lambda w: pl.BlockSpec((tm, w), lambda i: (i, 0))
    return pl.pallas_call(
        _after(body, 6, deps), grid=(s // tm,),
        in_specs=[row(D), _full((D, D)), row(D), row(D), row(2 * D), row(D)] + _any_specs(deps),
        out_specs=[row(D), row(D), row(2 * D), _full((1, 2 * D)), _full((D, D))],
        out_shape=[_sds((s, D), BF), _sds((s, D), BF), _sds((s, 2 * D), BF), _sds((1, 2 * D), F32),
                   _sds((D, D), F32)],
        compiler_params=_cp(("arbitrary",)), name="out_proj_bwd")(dy1, w_out, a, b, gates, mix, *deps)


def _pool_bwd(da, pooled, w_pool, pool_scale, h1, ts):
    s = da.shape[0]
    nt = s // ts
    halo = 16

    def body(da_ref, pooled_ref, wp_ref, ps_ref, h_ref, du_ref, dwp_ref, dps_ref, dwi_ref, nxt):
        g = pl.program_id(0)
        i = pl.program_id(1)
        ti = nt - 1 - i

        @pl.when(i == 0)
        def _():
            nxt[...] = jnp.zeros_like(nxt)
            dwp_ref[...] = jnp.zeros_like(dwp_ref)
            dps_ref[...] = jnp.zeros_like(dps_ref)
            dwi_ref[...] = jnp.zeros_like(dwi_ref)

        pooled = pooled_ref[...]
        dav = da_ref[...].astype(F32)
        dps_ref[...] += jnp.sum(dav * _mm(pooled, wp_ref[0]), axis=0, keepdims=True)
        dm = (dav * ps_ref[...]).astype(BF)
        dwp_ref[0] += _tn(pooled, dm)
        dpool = _nt(dm, wp_ref[0])
        t = (ti * ts + lax.broadcasted_iota(jnp.int32, (ts, 1), 0)).astype(F32)
        for gi in range(4):
            @pl.when(g == gi)
            def _(gi=gi):
                w = 2 << gi
                e = dpool * (1.0 / jnp.minimum(t + 1.0, float(w)))
                acc, span = jnp.concatenate([e, nxt[...]], axis=0), 1
                while span < w:
                    acc = acc + pltpu.roll(acc, ts + halo - span, 0)
                    span *= 2
                du = (acc[0:ts] - dpool).astype(BF)
                du_ref[...] = du
                dwi_ref[...] += _tn(du, h_ref[...])
                nxt[...] = e[0:halo]

    col = pl.BlockSpec((ts, POOL_GROUP), lambda g, i: (nt - 1 - i, g))
    return pl.pallas_call(
        body, grid=(4, nt),
        in_specs=[col, col, pl.BlockSpec((1, POOL_GROUP, POOL_GROUP), lambda g, i: (g, 0, 0)),
                  pl.BlockSpec((1, POOL_GROUP), lambda g, i: (0, g)),
                  pl.BlockSpec((ts, D), lambda g, i: (nt - 1 - i, 0))],
        out_specs=[col, pl.BlockSpec((1, POOL_GROUP, POOL_GROUP), lambda g, i: (g, 0, 0)),
                   pl.BlockSpec((1, POOL_GROUP), lambda g, i: (0, g)),
                   pl.BlockSpec((POOL_GROUP, D), lambda g, i: (g, 0))],
        out_shape=[_sds((s, D), BF), _sds((4, POOL_GROUP, POOL_GROUP), F32), _sds((1, D), F32), _sds((D, D), F32)],
        scratch_shapes=[pltpu.VMEM((halo, POOL_GROUP), F32)],
        compiler_params=_cp(("parallel", "arbitrary")), name="pool_bwd")(da, pooled, w_pool, pool_scale, h1)


def _attn_bwd(qh, kh, vh, sinks, db, deps=()):
    s = qh.shape[1]
    nb = s // BLK

    def body(sink_ref, q_ref, kp_ref, kc_ref, vp_ref, vc_ref, do_ref,
             dq_ref, dk_ref, dv_ref, dsink_ref, ck, cv):
        n = pl.program_id(0)

        @pl.when(n == 0)
        def _():
            ck[...] = jnp.zeros_like(ck)
            cv[...] = jnp.zeros_like(cv)
            dsink_ref[...] = jnp.zeros_like(dsink_ref)

        @pl.when(n < nb)
        def _():
            dov = do_ref[...]
            for khd in range(N_KV):
                q = q_ref[khd * GQA:(khd + 1) * GQA].reshape(GQA * BLK, HEAD)
                k = jnp.concatenate([kp_ref[khd], kc_ref[khd]], axis=0)
                v = jnp.concatenate([vp_ref[khd], vc_ref[khd]], axis=0)
                c0 = khd * GQA * HEAD
                do = jnp.concatenate([dov[:, c0 + HEAD * g:c0 + HEAD * (g + 1)] for g in range(GQA)],
                                     axis=0).astype(BF)
                probs, psink = _attn_probs(q, k, n, _sink_column(sink_ref, khd))
                dp = _nt(do, v)
                delta = jnp.sum(probs * dp, axis=-1, keepdims=True)
                ds = (probs * (dp - delta)).astype(BF)
                dq_ref[khd * GQA:(khd + 1) * GQA] = _mm(ds, k).reshape(GQA, BLK, HEAD)
                dk = _tn(ds, q)
                dv = _tn(probs.astype(BF), do)
                dk_ref[khd] = ck[khd] + dk[0:BLK]
                dv_ref[khd] = cv[khd] + dv[0:BLK]
                ck[khd] = dk[BLK:2 * BLK]
                cv[khd] = dv[BLK:2 * BLK]
                dsk = psink * delta
                lane = lax.broadcasted_iota(jnp.int32, (1, 128), 1)
                acc = jnp.zeros((1, 128), F32)
                for g in range(GQA):
                    acc = acc - jnp.where(lane == khd * GQA + g,
                                          jnp.sum(dsk[g * BLK:(g + 1) * BLK], axis=0, keepdims=True), 0.0)
                dsink_ref[...] += acc

        @pl.when(n == nb)
        def _():
            dk_ref[...] = ck[...]
            dv_ref[...] = cv[...]

    last = nb - 1
    prev = pl.BlockSpec((N_KV, BLK, HEAD), lambda n: (0, jnp.maximum(jnp.minimum(n, last) - 1, 0), 0))
    cur = pl.BlockSpec((N_KV, BLK, HEAD), lambda n: (0, jnp.minimum(n, last), 0))
    kv_out = pl.BlockSpec((N_KV, BLK, HEAD), lambda n: (0, jnp.maximum(n - 1, 0), 0))
    return pl.pallas_call(
        _after(body, 7, deps), grid=(nb + 1,),
        in_specs=[pl.BlockSpec(memory_space=pltpu.SMEM),
                  pl.BlockSpec((N_Q, BLK, HEAD), lambda n: (0, jnp.minimum(n, last), 0)),
                  prev, cur, prev, cur,
                  pl.BlockSpec((BLK, Q_W), lambda n: (jnp.minimum(n, last), 0))] + _any_specs(deps),
        out_specs=[pl.BlockSpec((N_Q, BLK, HEAD), lambda n: (0, jnp.minimum(n, last), 0)), kv_out, kv_out,
                   _full((1, 128))],
        out_shape=[_sds((N_Q, s, HEAD), F32), _sds((N_KV, s, HEAD), F32), _sds((N_KV, s, HEAD), F32),
                   _sds((1, 128), F32)],
        scratch_shapes=[pltpu.VMEM((N_KV, BLK, HEAD), F32), pltpu.VMEM((N_KV, BLK, HEAD), F32)],
        compiler_params=_cp(("arbitrary",)), name="attn_bwd")(sinks, qh, kh, kh, vh, vh, db, *deps)


def _qk_prep_bwd(dqh, dkh, dvh, qkv, pos, wq, wk, invf, bd, tm, deps=()):
    s = qkv.shape[0]

    def fold_heads(row):
        out = row[:, 0:HEAD]
        for h in range(1, row.shape[1] // HEAD):
            out = out + row[:, HEAD * h:HEAD * (h + 1)]
        return out

    def body(dq_ref, dk_ref, dv_ref, qkv_ref, pos_ref, wq_ref, wk_ref, invf_ref, bd_ref,
             dz_ref, dwq_ref, dwk_ref):
        @pl.when(pl.program_id(0) == 0)
        def _():
            dwq_ref[...] = jnp.zeros_like(dwq_ref)
            dwk_ref[...] = jnp.zeros_like(dwk_ref)

        cos, sa, sb = _rope_tables(pos_ref, invf_ref)

        def norm_rope_bwd(dy, xin, w, bdm):
            dn = _rope_t(dy, cos, sa, sb)
            r = lax.rsqrt(_head_sum(xin * xin, bdm) * (1.0 / HEAD) + EPS)
            nh = xin * r
            gw = dn * w
            dx = r * (gw - nh * (_head_sum(gw * nh, bdm) * (1.0 / HEAD)))
            return dx, fold_heads(jnp.sum(dn * nh, axis=0, keepdims=True))

        dq = jnp.concatenate([dq_ref[h] for h in range(N_Q)], axis=1) * (HEAD ** -0.5)
        dk = jnp.concatenate([dk_ref[h] for h in range(N_KV)], axis=1)
        dxq, dwq = norm_rope_bwd(dq, qkv_ref[:, 0:Q_W], wq_ref[...], bd_ref[...])
        dxk, dwk = norm_rope_bwd(dk, qkv_ref[:, Q_W:Q_W + KV_W], wk_ref[...], bd_ref[0:KV_W, 0:KV_W])
        dz_ref[:, 0:Q_W] = dxq.astype(BF)
        dz_ref[:, Q_W:Q_W + KV_W] = dxk.astype(BF)
        dz_ref[:, Q_W + KV_W:QKV_W] = jnp.concatenate([dv_ref[h] for h in range(N_KV)], axis=1).astype(BF)
        dwq_ref[...] += dwq
        dwk_ref[...] += dwk

    heads = lambda n: pl.BlockSpec((n, tm, HEAD), lambda i: (0, i, 0))
    return pl.pallas_call(
        _after(body, 9, deps), grid=(s // tm,),
        in_specs=[heads(N_Q), heads(N_KV), heads(N_KV), pl.BlockSpec((tm, QKV_W), lambda i: (i, 0)),
                  pl.BlockSpec((tm, 1), lambda i: (i, 0)), _full((1, Q_W)), _full((1, KV_W)),
                  _full((1, 2 * HEAD)), _full((Q_W, Q_W))] + _any_specs(deps),
        out_specs=[pl.BlockSpec((tm, QKV_W), lambda i: (i, 0)), _full((1, HEAD)), _full((1, HEAD))],
        out_shape=[_sds((s, QKV_W), BF), _sds((1, HEAD), F32), _sds((1, HEAD), F32)],
        compiler_params=_cp(("arbitrary",)), name="qk_prep_bwd")(
            dqh, dkh, dvh, qkv, pos, wq, wk, invf, bd, *deps)


def _in_proj_bwd(du, dzq, dzg, w_in, x, g1, dy1, tm, deps=()):
    s = x.shape[0]

    def body(du_ref, dzq_ref, dzg_ref, w_ref, x_ref, g_ref, dy_ref, gx_ref, dgn_ref):
        @pl.when(pl.program_id(0) == 0)
        def _():
            dgn_ref[...] = jnp.zeros_like(dgn_ref)

        dh = _mm(du_ref[...], w_ref[0:D, :])
        dh = dh + _mm(dzq_ref[...], w_ref[D:D + QKV_W, :])
        dh = dh + _mm(dzg_ref[...], w_ref[D + QKV_W:IN_W, :])
        dx, dgn = _rms_bwd(dh, x_ref[...], g_ref[...])
        gx_ref[...] = dy_ref[...] + dx
        dgn_ref[...] += jnp.sum(dgn, axis=0, keepdims=True)

    row = lambda w: pl.BlockSpec((tm, w), lambda i: (i, 0))
    return pl.pallas_call(
        _after(body, 7, deps), grid=(s // tm,),
        in_specs=[row(D), row(QKV_W), row(2 * D), _full((IN_W, D)), row(D), _full((1, D)), row(D)] + _any_specs(deps),
        out_specs=[row(D), _full((1, D))],
        out_shape=[_sds((s, D), F32), _sds((1, D), F32)],
        compiler_params=_cp(("arbitrary",)), name="in_proj_bwd")(du, dzq, dzg, w_in, x, g1, dy1, *deps)


def _adamw_step(w, g, m, v):
    mn = B1 * m + (1.0 - B1) * g
    vn = B2 * v + (1.0 - B2) * (g * g)
    m_hat = mn / (1.0 - B1 ** STEP)
    v_hat = vn / (1.0 - B2 ** STEP)
    return -LR * (m_hat / (jnp.sqrt(v_hat) + ADAM_EPS) + WD * w), mn, vn


SMALL_ROWS = 16
SMALL_COLS = 2 * D_FF
SMALL_AT = {"attn_norm": (0, D), "b_gate": (1, 2 * D), "pool_scale": (2, D), "q_norm": (3, HEAD),
            "k_norm": (4, HEAD), "sinks": (5, N_Q), "ffn_norm": (6, D), "conv_b": (7, 2 * D_FF)}
SMALL_CONV_W_ROW = 8


def _pack_small(d_attn_norm, d_bgate, d_pscale, d_qn, d_kn, dsink, d_ffn_norm, dcb_g, dcb_v, dcw_g, dcw_v, dev):
    def body(k_ref, an_ref, bg_ref, ps_ref, qn_ref, kn_ref, sk_ref, fn_ref, cbg_ref, cbv_ref, cwg_ref, cwv_ref,
             o_ref):
        o_ref[...] = jnp.zeros_like(o_ref)
        for nm, ref in (("attn_norm", an_ref), ("b_gate", bg_ref), ("pool_scale", ps_ref), ("q_norm", qn_ref),
                        ("k_norm", kn_ref), ("ffn_norm", fn_ref)):
            row, n = SMALL_AT[nm]
            o_ref[0, row:row + 1, 0:n] = ref[...]
        row, _ = SMALL_AT["sinks"]
        o_ref[0, row:row + 1, 0:128] = sk_ref[...]
        row, _ = SMALL_AT["conv_b"]
        o_ref[0, row:row + 1, 0:D_FF] = cbg_ref[...]
        o_ref[0, row:row + 1, D_FF:2 * D_FF] = cbv_ref[...]
        for k in range(3):
            row = SMALL_CONV_W_ROW + k
            for half in range(2):
                o_ref[0, row:row + 1, half * UP_SHARD:(half + 1) * UP_SHARD] = cwg_ref[half, k:k + 1, :]
                o_ref[0, row:row + 1, (2 + half) * UP_SHARD:(3 + half) * UP_SHARD] = cwv_ref[half, k:k + 1, :]

    args = [d_attn_norm, d_bgate, d_pscale, d_qn, d_kn, dsink, d_ffn_norm, dcb_g, dcb_v, dcw_g, dcw_v]
    grid_spec = pltpu.PrefetchScalarGridSpec(
        num_scalar_prefetch=1, grid=(1,),
        in_specs=[pl.BlockSpec(a.shape, functools.partial(lambda nd, i, k: (0,) * nd, a.ndim)) for a in args],
        out_specs=pl.BlockSpec((1, SMALL_ROWS, SMALL_COLS), lambda i, k: (k[0], 0, 0)))
    return pl.pallas_call(body, grid_spec=grid_spec, out_shape=_sds((N_DEV, SMALL_ROWS, SMALL_COLS), F32),
                          name="pack_small")(dev, *args)


def _small_update(stack, params):
    names = list(params)

    def body(*refs):
        s_ref = refs[0]
        ins = refs[1:1 + 3 * len(names)]
        outs = refs[1 + 3 * len(names):]
        tot = s_ref[0]
        for d in range(1, N_DEV):
            tot = tot + s_ref[d]
        for i, nm in enumerate(names):
            row, n = SMALL_AT[nm]
            g = tot[row:row + 1, 0:n]
            delta, mn, vn = _adamw_step(ins[3 * i][...], g, ins[3 * i + 1][...], ins[3 * i + 2][...])
            outs[4 * i][...] = g
            outs[4 * i + 1][...] = delta
            outs[4 * i + 2][...] = mn
            outs[4 * i + 3][...] = vn
        outs[-1][...] = tot[SMALL_CONV_W_ROW:SMALL_CONV_W_ROW + 3, :]

    flat = [a for nm in names for a in params[nm]]
    out_shape = [_sds(params[nm][0].shape, F32) for nm in names for _ in range(4)] + [_sds((3, SMALL_COLS), F32)]
    res = pl.pallas_call(body, out_shape=out_shape, name="small_update")(stack, *flat)
    return {nm: list(res[4 * i:4 * i + 4]) for i, nm in enumerate(names)}, res[-1]


def _adamw(w, g, m, v, tr, name, deps=()):
    r, c = w.shape

    def body(w_ref, g_ref, m_ref, v_ref, go_ref, d_ref, mo_ref, vo_ref):
        gv = g_ref[...]
        go_ref[...] = gv
        d_ref[...], mo_ref[...], vo_ref[...] = _adamw_step(w_ref[...], gv, m_ref[...], v_ref[...])

    blk = pl.BlockSpec((tr, c), lambda i: (i, 0))
    return pl.pallas_call(
        _after(body, 4, deps), grid=(r // tr,), in_specs=[blk] * 4 + _any_specs(deps), out_specs=[blk] * 4,
        out_shape=[_sds((r, c), F32)] * 4, compiler_params=_cp(("parallel",)), name=name)(w, g, m, v, *deps)


def _place():
    x, y, c = lax.axis_index("x"), lax.axis_index("y"), lax.axis_index("c")
    chips = [(1 - x, y), (x, 1 - y), (1 - x, 1 - y)]
    return x, y, c, chips


def _rows(ref, lead, h, rh):
    sl = pl.ds(pl.multiple_of(h * rh, 16), rh)
    return ref.at[sl, :] if lead is None else ref.at[lead, sl, :]


def _all_gather_weights(halved, whole):
    nh, nw = len(halved), len(whole)
    na = nh + nw
    arrays = list(halved) + list(whole)
    out_dtypes = [BF] * nh + [a.dtype for a in whole]
    cast_rows = 128

    def body(*refs):
        ins, outs = refs[:na], refs[na:2 * na]
        raw, stage = refs[2 * na:3 * na], refs[3 * na:3 * na + nh]
        ici_send, ici_recv, fwd_send, fwd_recv, in_sem, loc_sem = refs[3 * na + nh:]
        x, y, c, chips = _place()
        me = 2 * x + y
        sibling = (x, y, 1 - c)
        loads = [pltpu.make_async_copy(ins[a], raw[a], in_sem.at[a]) for a in range(na)]
        for cp in loads:
            cp.start()

        def ici(a, j, src_chip, src=None):
            if a < nh:
                rh = arrays[a].shape[0] // 2
                dst = _rows(outs[a], src_chip, c, rh)
                src = dst if src is None else _rows(src, None, c, rh)
            else:
                dst = outs[a].at[src_chip]
                src = dst if src is None else src
            return pltpu.make_async_remote_copy(
                src_ref=src, dst_ref=dst, send_sem=ici_send.at[3 * a + j], recv_sem=ici_recv.at[3 * a + j],
                device_id=(*chips[j], c), device_id_type=MESH)

        def fwd(a, j, half):
            rh = arrays[a].shape[0] // 2
            kj = 2 * chips[j][0] + chips[j][1]
            blk = _rows(outs[a], kj, half, rh)
            return pltpu.make_async_remote_copy(
                src_ref=blk, dst_ref=blk, send_sem=fwd_send.at[3 * a + j], recv_sem=fwd_recv.at[3 * a + j],
                device_id=sibling, device_id_type=MESH)

        local, sends = [], []
        for a in range(na):
            loads[a].wait()
            if a < nh:
                r = arrays[a].shape[0]
                for r0 in range(0, r, cast_rows):
                    r1 = min(r0 + cast_rows, r)
                    stage[a][r0:r1, :] = raw[a][r0:r1, :].astype(BF)
                own = stage[a]
            else:
                own = raw[a]
            cp = pltpu.make_async_copy(own, outs[a].at[me], loc_sem.at[a])
            cp.start()
            local.append(cp)
            for j in range(3):
                cp = ici(a, j, me, src=own)
                cp.start()
                sends.append(cp)
        passed = []
        for a in range(na):
            for j in range(3):
                kj = 2 * chips[j][0] + chips[j][1]
                ici(a, j, kj).wait_recv()
                if a < nh:
                    cp = fwd(a, j, c)
                    cp.start()
                    passed.append(cp)
        for a in range(nh):
            for j in range(3):
                fwd(a, j, 1 - c).wait_recv()
        for cp in sends + passed:
            cp.wait_send()
        for cp in local:
            cp.wait()

    any_spec = pl.BlockSpec(memory_space=pl.ANY)
    return pl.pallas_call(
        body, in_specs=[any_spec] * na, out_specs=[any_spec] * na,
        out_shape=[_sds((N_CHIPS,) + a.shape, dt) for a, dt in zip(arrays, out_dtypes)],
        scratch_shapes=[pltpu.VMEM(a.shape, a.dtype) for a in arrays] + [pltpu.VMEM(a.shape, BF) for a in halved]
        + [pltpu.SemaphoreType.DMA((3 * na,)), pltpu.SemaphoreType.DMA((3 * na,)),
           pltpu.SemaphoreType.DMA((3 * nh,)), pltpu.SemaphoreType.DMA((3 * nh,)),
           pltpu.SemaphoreType.DMA((na,)), pltpu.SemaphoreType.DMA((na,))],
        compiler_params=pltpu.CompilerParams(vmem_limit_bytes=VMEM_LIMIT_MB << 20),
        name="all_gather_weights")(*arrays)


def _sibling_halves(grads):
    na = len(grads)

    def body(*refs):
        ins, outs = refs[:na], refs[na:2 * na]
        send_sem, recv_sem = refs[2 * na:]
        x, y, c, _ = _place()
        copies = []
        for a in range(na):
            rh = grads[a].shape[1] // 2
            src = ins[a].at[:, pl.ds(pl.multiple_of((1 - c) * rh, 8), rh), :]
            copies.append(pltpu.make_async_remote_copy(
                src_ref=src, dst_ref=outs[a], send_sem=send_sem.at[a], recv_sem=recv_sem.at[a],
                device_id=(x, y, 1 - c), device_id_type=MESH))
        for cp in copies:
            cp.start()
        for cp in copies:
            cp.wait()

    any_spec = pl.BlockSpec(memory_space=pl.ANY)
    return pl.pallas_call(
        body, in_specs=[any_spec] * na, out_specs=[any_spec] * na,
        out_shape=[_sds((N_CHIPS, g.shape[1] // 2, g.shape[2]), F32) for g in grads],
        scratch_shapes=[pltpu.SemaphoreType.DMA((na,)), pltpu.SemaphoreType.DMA((na,))],
        name="sibling_halves")(*grads)


def _pair_sum(g, recv, c, tr, name):
    _, r, cols = g.shape
    rh = r // 2
    nr = rh // tr

    def body(c_ref, g_ref, r_ref, o_ref):
        o_ref[...] = (g_ref[...] + r_ref[...]).astype(BF)

    grid_spec = pltpu.PrefetchScalarGridSpec(
        num_scalar_prefetch=1, grid=(N_CHIPS, nr),
        in_specs=[pl.BlockSpec((1, tr, cols), lambda k, i, c_ref: (k, c_ref[0] * nr + i, 0)),
                  pl.BlockSpec((1, tr, cols), lambda k, i, c_ref: (k, i, 0))],
        out_specs=pl.BlockSpec((1, tr, cols), lambda k, i, c_ref: (k, i, 0)))
    return pl.pallas_call(
        body, grid_spec=grid_spec, out_shape=_sds((N_CHIPS, rh, cols), BF),
        compiler_params=_cp(("parallel", "parallel")), name=name)(c, g, recv)


def _chip_exchange(halves, small):
    na = len(halves)
    srows = small.shape[0]

    def body(*refs):
        ins, small_ref = refs[:na], refs[na]
        outs, small_out = refs[na + 1:2 * na + 1], refs[2 * na + 1]
        send_sem, recv_sem, s_send, s_recv = refs[2 * na + 2:]
        x, y, c, chips = _place()
        me = 4 * x + 2 * y + c
        copies = []
        for a in range(na):
            for j in range(3):
                kj = 2 * chips[j][0] + chips[j][1]
                copies.append(pltpu.make_async_remote_copy(
                    src_ref=ins[a].at[kj], dst_ref=outs[a].at[j],
                    send_sem=send_sem.at[3 * a + j], recv_sem=recv_sem.at[3 * a + j],
                    device_id=(*chips[j], c), device_id_type=MESH))
        for r in range(1, N_DEV):
            peer = (x ^ (r >> 2), y ^ ((r >> 1) & 1), c ^ (r & 1))
            copies.append(pltpu.make_async_remote_copy(
                src_ref=small_ref, dst_ref=small_out.at[me],
                send_sem=s_send.at[r - 1], recv_sem=s_recv.at[r - 1], device_id=peer, device_id_type=MESH))
        for cp in copies:
            cp.start()
        small_out[pl.ds(me, 1)] = small_ref[...][None]
        for cp in copies:
            cp.wait()

    any_spec = pl.BlockSpec(memory_space=pl.ANY)
    vmem = pl.BlockSpec(memory_space=pltpu.VMEM)
    return pl.pallas_call(
        body, in_specs=[any_spec] * na + [vmem], out_specs=[any_spec] * na + [vmem],
        out_shape=[_sds((3,) + h.shape[1:], h.dtype) for h in halves] + [_sds((N_DEV, srows, 128), F32)],
        scratch_shapes=[pltpu.SemaphoreType.DMA((3 * na,)), pltpu.SemaphoreType.DMA((3 * na,)),
                        pltpu.SemaphoreType.DMA((N_DEV - 1,)), pltpu.SemaphoreType.DMA((N_DEV - 1,))],
        name="chip_exchange")(*halves, small)


def _chip_sum(g, sib, recv, place, tr, name):
    _, r, cols = g.shape
    rh = r // 2
    nr = rh // tr

    def body(p_ref, g_ref, s_ref, r0_ref, r1_ref, r2_ref, o_ref):
        own = g_ref[0] + s_ref[0]
        o_ref[...] = ((own + r0_ref[0].astype(F32)) + r1_ref[0].astype(F32)) + r2_ref[0].astype(F32)

    rspec = lambda j: pl.BlockSpec((1, tr, cols), lambda i, p: (j, i, 0))
    grid_spec = pltpu.PrefetchScalarGridSpec(
        num_scalar_prefetch=1, grid=(nr,),
        in_specs=[pl.BlockSpec((1, tr, cols), lambda i, p: (p[0], p[1] * nr + i, 0)),
                  pl.BlockSpec((1, tr, cols), lambda i, p: (p[0], i, 0)), rspec(0), rspec(1), rspec(2)],
        out_specs=pl.BlockSpec((tr, cols), lambda i, p: (p[1] * nr + i, 0)))
    return pl.pallas_call(
        body, grid_spec=grid_spec, out_shape=_sds((r, cols), F32),
        compiler_params=_cp(("parallel",)), name=name)(place, g, sib, recv, recv, recv)


def _sibling_exchange(shards):
    na = len(shards)

    def body(*refs):
        ins, outs = refs[:na], refs[na:2 * na]
        send_sem, recv_sem = refs[2 * na:]
        x, y, c, _ = _place()
        for a in range(na):
            rh = shards[a].shape[0] // 2
            pltpu.make_async_remote_copy(
                src_ref=_rows(ins[a], None, c, rh), dst_ref=_rows(outs[a], None, c, rh),
                send_sem=send_sem.at[a], recv_sem=recv_sem.at[a],
                device_id=(x, y, 1 - c), device_id_type=MESH).start()
        for a in range(na):
            rh = shards[a].shape[0] // 2
            pltpu.make_async_remote_copy(
                src_ref=_rows(ins[a], None, c, rh), dst_ref=_rows(outs[a], None, 1 - c, rh),
                send_sem=send_sem.at[a], recv_sem=recv_sem.at[a],
                device_id=(x, y, 1 - c), device_id_type=MESH).wait()

    any_spec = pl.BlockSpec(memory_space=pl.ANY)
    return pl.pallas_call(
        body, in_specs=[any_spec] * na, out_specs=[any_spec] * na,
        out_shape=[_sds(h.shape, F32) for h in shards],
        input_output_aliases={a: a for a in range(na)},
        scratch_shapes=[pltpu.SemaphoreType.DMA((na,)), pltpu.SemaphoreType.DMA((na,))],
        name="sibling_exchange")(*shards)


def _device_sum(stack, deps=()):
    _, rows, _ = stack.shape

    def body(s_ref, o_ref):
        acc = s_ref[0]
        for d in range(1, N_DEV):
            acc = acc + s_ref[d]
        o_ref[...] = acc

    vmem = pl.BlockSpec(memory_space=pltpu.VMEM)
    return pl.pallas_call(_after(body, 1, deps), in_specs=[vmem] + _any_specs(deps), out_specs=vmem,
                          out_shape=_sds((rows, 128), F32), name="device_sum")(stack, *deps)


_HBM = pl.BlockSpec(memory_space=pltpu.HBM)
_SEM = pl.BlockSpec(memory_space=pltpu.SEMAPHORE)
_EFFECT = pltpu.SideEffectType.DATAFLOW_SIDE_EFFECTING


def _remote(src, dst, ssem, rsem, k, device):
    return pltpu.make_async_remote_copy(src_ref=src, dst_ref=dst, send_sem=ssem.at[k], recv_sem=rsem.at[k],
                                        device_id=device, device_id_type=MESH)


def _split_start(name, bufs, plan, n):
    nb = len(bufs)

    def body(*refs):
        sends, _ = plan(refs[:nb], refs[nb], refs[nb + 1])
        for cp in sends:
            cp.start()
        refs[-1][...] = jnp.zeros_like(refs[-1])

    res = pl.pallas_call(
        body, name=name,
        out_shape=(pltpu.SemaphoreType.DMA((n,)), pltpu.SemaphoreType.DMA((n,)))
        + tuple(pltpu.HBM(b.shape, b.dtype) for b in bufs) + (_sds((8, 128), F32),),
        in_specs=[_HBM] * nb,
        out_specs=(_SEM, _SEM) + (_HBM,) * nb + (pl.BlockSpec(memory_space=pltpu.VMEM),),
        input_output_aliases={i: i + 2 for i in range(nb)},
        compiler_params=pltpu.CompilerParams(has_side_effects=_EFFECT),
    )(*[pltpu.with_memory_space_constraint(b, pltpu.HBM) for b in bufs])
    return res[0], res[1], list(res[2:2 + nb]), res[2 + nb]


def _split_wait(name, send_sem, recv_sem, bufs, plan, after):
    nb = len(bufs)

    def body(*refs):
        sends, arrivals = plan(refs[:nb], refs[nb], refs[nb + 1])
        for cp in sends:
            cp.wait_send()
        for cp in arrivals:
            cp.wait_recv()

    res = pl.pallas_call(
        body, name=name, out_shape=tuple(pltpu.HBM(b.shape, b.dtype) for b in bufs),
        in_specs=[_HBM] * nb + [_SEM, _SEM, pl.BlockSpec(memory_space=pl.ANY)],
        out_specs=(_HBM,) * nb, input_output_aliases={i: i for i in range(nb)},
        compiler_params=pltpu.CompilerParams(has_side_effects=_EFFECT),
    )(*bufs, send_sem, recv_sem, after)
    return list(res)


def _plan_sibling_halves(shapes):
    na = len(shapes)

    def plan(refs, ssem, rsem):
        x, y, c, _ = _place()
        cps = []
        for a in range(na):
            rh = shapes[a][1] // 2
            src = refs[a].at[:, pl.ds(pl.multiple_of((1 - c) * rh, 8), rh), :]
            cps.append(_remote(src, refs[na + a], ssem, rsem, a, (x, y, 1 - c)))
        return cps, cps

    return plan


def _plan_chip_exchange(na):
    def plan(refs, ssem, rsem):
        _, _, c, chips = _place()
        cps = []
        for a in range(na):
            for j in range(3):
                kj = 2 * chips[j][0] + chips[j][1]
                cps.append(_remote(refs[a].at[kj], refs[na + a].at[j], ssem, rsem, 3 * a + j, (*chips[j], c)))
        return cps, cps

    return plan


def _plan_sibling_swap(shapes, with_small):
    def plan(refs, ssem, rsem):
        x, y, c, _ = _place()
        sends, arrivals = [], []
        for a, shp in enumerate(shapes):
            rh = shp[0] // 2
            mine, other = _rows(refs[a], None, c, rh), _rows(refs[a], None, 1 - c, rh)
            sends.append(_remote(mine, mine, ssem, rsem, a, (x, y, 1 - c)))
            arrivals.append(_remote(mine, other, ssem, rsem, a, (x, y, 1 - c)))
        if with_small:
            mine = refs[len(shapes)].at[4 * x + 2 * y + c]
            for r in range(1, N_DEV):
                peer = (x ^ (r >> 2), y ^ ((r >> 1) & 1), c ^ (r & 1))
                cp = _remote(mine, mine, ssem, rsem, len(shapes) + r - 1, peer)
                sends.append(cp)
                arrivals.append(cp)
        return sends, arrivals

    return plan


def _plan_gather_chips(shapes):
    def plan(refs, ssem, rsem):
        x, y, c, chips = _place()
        me = 2 * x + y
        sends, arrivals = [], []
        for a, shp in enumerate(shapes):
            rh = shp[1] // 2
            mine = _rows(refs[a], me, c, rh)
            for j in range(3):
                land = _rows(refs[a], 2 * chips[j][0] + chips[j][1], c, rh)
                sends.append(_remote(mine, mine, ssem, rsem, 3 * a + j, (*chips[j], c)))
                arrivals.append(_remote(land, land, ssem, rsem, 3 * a + j, (*chips[j], c)))
        return sends, arrivals

    return plan


def _plan_gather_sibling(shapes):
    def plan(refs, ssem, rsem):
        x, y, c, chips = _place()
        sends, arrivals = [], []
        for a, shp in enumerate(shapes):
            rh = shp[1] // 2
            for j in range(3):
                kj = 2 * chips[j][0] + chips[j][1]
                got, land = _rows(refs[a], kj, c, rh), _rows(refs[a], kj, 1 - c, rh)
                sends.append(_remote(got, got, ssem, rsem, 3 * a + j, (x, y, 1 - c)))
                arrivals.append(_remote(got, land, ssem, rsem, 3 * a + j, (x, y, 1 - c)))
        return sends, arrivals

    return plan


def _into_slice(w, k, n, tr, dtype, name):
    r, cols = w.shape

    def body(k_ref, w_ref, o_ref):
        o_ref[0] = w_ref[...].astype(dtype)

    grid_spec = pltpu.PrefetchScalarGridSpec(
        num_scalar_prefetch=1, grid=(r // tr,),
        in_specs=[pl.BlockSpec((tr, cols), lambda i, k: (i, 0))],
        out_specs=pl.BlockSpec((1, tr, cols), lambda i, k: (k[0], i, 0)))
    return pl.pallas_call(body, grid_spec=grid_spec, out_shape=_sds((n, r, cols), dtype),
                          compiler_params=_cp(("parallel",)), name=name)(k, w)


class _LateWeights:
    def __init__(self, shards, chip, names, tiles):
        bufs = [_into_slice(w, chip, N_CHIPS, t, BF, "own_" + nm) for w, nm, t in zip(shards, names, tiles)]
        self.n = 3 * len(bufs)
        self.chips, self.sibling = _plan_gather_chips([b.shape for b in bufs]), _plan_gather_sibling([b.shape for b in bufs])
        self.ssem, self.rsem, self.bufs, token = _split_start("gather_chips_start", bufs, self.chips, self.n)
        self.first = (token,)

    def middle(self, after):
        bufs = _split_wait("gather_chips_wait", self.ssem, self.rsem, self.bufs, self.chips, after)
        self.ssem, self.rsem, self.bufs, token = _split_start("gather_sibling_start", bufs, self.sibling, self.n)
        return (token,)

    def last(self, after):
        return _split_wait("gather_sibling_wait", self.ssem, self.rsem, self.bufs, self.sibling, after)


class _GradReduce:
    def __init__(self, tag, place, names, tiles):
        self.tag, self.place, self.names, self.tiles = tag, place, names, tiles
        self.small_all = None

    def first(self, grads):
        self.na = len(grads)
        self.p1 = _plan_sibling_halves([g.shape for g in grads])
        lands = [lax.empty((N_CHIPS, g.shape[1] // 2, g.shape[2]), F32) for g in grads]
        self.ssem, self.rsem, self.bufs, token = _split_start(
            self.tag + "_halves_start", list(grads) + lands, self.p1, self.na)
        return (token,)

    def second(self, after):
        bufs = _split_wait(self.tag + "_halves_wait", self.ssem, self.rsem, self.bufs, self.p1, after)
        self.grads, self.sib = bufs[:self.na], bufs[self.na:]
        halves = [_pair_sum(g, r, self.place[1:2], t, "pair_sum_" + nm)
                  for g, r, t, nm in zip(self.grads, self.sib, self.tiles, self.names)]
        lands = [lax.empty((3,) + h.shape[1:], h.dtype) for h in halves]
        self.p2 = _plan_chip_exchange(self.na)
        self.ssem, self.rsem, self.bufs, token = _split_start(
            self.tag + "_chips_start", halves + lands, self.p2, 3 * self.na)
        return (token,)

    def third(self, after, small=None):
        bufs = _split_wait(self.tag + "_chips_wait", self.ssem, self.rsem, self.bufs, self.p2, after)
        mine = [_chip_sum(g, sb, r, self.place, t, "chip_sum_" + nm)
                for g, sb, r, t, nm in zip(self.grads, self.sib, bufs[self.na:], self.tiles, self.names)]
        extra = [] if small is None else [small]
        self.p3 = _plan_sibling_swap([m.shape for m in mine], small is not None)
        self.ssem, self.rsem, self.bufs, token = _split_start(
            self.tag + "_swap_start", mine + extra, self.p3, self.na + (N_DEV - 1) * len(extra))
        return (token,)

    def last(self, after):
        bufs = _split_wait(self.tag + "_swap_wait", self.ssem, self.rsem, self.bufs, self.p3, after)
        if len(bufs) > self.na:
            self.small_all = bufs[self.na]
        return bufs[:self.na]


def _pack(parts, rows):
    flat = jnp.concatenate([p.reshape(-1) for p in parts])
    return jnp.pad(flat, (0, rows * 128 - flat.shape[0])).reshape(rows, 128)


def _unpack(buf, shapes):
    flat = buf.reshape(-1)
    out, off = [], 0
    for shp in shapes:
        n = 1
        for d in shp:
            n *= d
        out.append(flat[off:off + n].reshape(shp))
        off += n
    return out


def _rows_for(n):
    return -(-n // (8 * 128)) * 8


class _WeightsAtHand:
    def __init__(self, wup, wout, wdown):
        self.first, self.weights = (), [wup, wout, wdown]

    def middle(self, after):
        return ()

    def last(self, after):
        return self.weights


class _GradsKept:
    def first(self, grads):
        self.grads = list(grads)
        return ()

    def second(self, after):
        return ()

    def third(self, after, small=None):
        return ()

    def last(self, after):
        return self.grads


def _forward_backward(xs, pos, tgt, win, wpool, cw, attn_norm, b_gate, pool_scale, q_norm, k_norm, sinks,
                      ffn_norm, conv_b, late, early, rest):
    s = xs.shape[0]
    tm = min(512, s)
    tk = min(1024, s)
    inv_freq = ROPE_THETA ** (-jnp.arange(0, ROPE_DIM, 2, dtype=F32) / ROPE_DIM)
    lane = jnp.arange(2 * HEAD) % HEAD
    invf = jnp.where(lane < ROPE_DIM, inv_freq[lane % (ROPE_DIM // 2)], 0.0).reshape(1, 2 * HEAD)
    wq = jnp.tile(q_norm, (1, N_Q))
    wk = jnp.tile(k_norm, (1, N_KV))
    head_of = jnp.arange(Q_W) // HEAD
    bd = (head_of[:, None] == head_of[None, :]).astype(BF)
    sink = sinks[0]

    h1, u, qkv, gates = _attn_in_proj(xs, attn_norm, win, b_gate, tm, deps=late.first)
    qh, kh, vh = _qk_prep(qkv, pos, wq, wk, invf, bd, tm)
    battn = _attn_fwd(qh, kh, vh, sink, deps=late.middle(qh))
    apool, pooled = _pool_fwd(u, wpool, pool_scale, min(512, s))
    wup, wout, wdown = late.last(apool)
    wout = wout.reshape(D, D)
    wdown = wdown.reshape(D_FF, D)
    mix, y1, h2 = _mix_out_proj(apool, battn, gates, xs, wout, ffn_norm, tm)
    pre_g, pre_v, up_g, up_v, act = _ffn_up(h2, wup, cw, conv_b, tm)
    dy2, dy2b, loss_acc = _ffn_down_loss(act, wdown, y1, tgt, tm)

    d_wdown = _grad_matmul(act, dy2b, 512, tk, "grad_w_down")
    dp_g, dp_v, dcw_g, dcw_v, dcb_g, dcb_v = _ffn_act_bwd(dy2b, wdown, up_g, up_v, pre_g, pre_v, cw, tm)
    d_wup = _grad_matmul(h2, dp_g, UP_SHARD, tk, "grad_w_up_gate", lead=N_CHIPS)
    d_wup = _grad_matmul(h2, dp_v, UP_SHARD, tk, "grad_w_up_value", lead=N_CHIPS, prev=d_wup, lead_off=2)
    token = early.first([d_wdown.reshape(N_CHIPS, D_FF // N_CHIPS, D), d_wup])
    dy1, d_ffn_norm = _ffn_up_bwd(dp_g, dp_v, wup, y1, dy2, ffn_norm, tm, deps=token)
    token = early.second(dy1)
    da, db, dzg, d_bgate, d_wout = _out_proj_bwd(dy1, wout, apool, battn, gates, mix, tm, deps=token)
    du, d_wpool, d_pscale, d_win_pool = _pool_bwd(da, pooled, wpool, pool_scale, h1, min(512, s))
    dqh, dkh, dvh, dsink = _attn_bwd(qh, kh, vh, sink, db)
    token = early.third(dqh)
    dzq, d_qn, d_kn = _qk_prep_bwd(dqh, dkh, dvh, qkv, pos, wq, wk, invf, bd, tm, deps=token)
    d_win_t = jnp.concatenate([
        d_win_pool,
        _grad_matmul(dzq, h1, D, tk, "grad_w_in_qkv"),
        _grad_matmul(dzg, h1, D, tk, "grad_w_in_gates")], axis=0)
    token = rest.first([
        d_win_t.reshape(N_CHIPS, IN_W // N_CHIPS, D),
        d_wout.reshape(N_CHIPS, D // N_CHIPS, D),
        d_wpool.reshape(4, N_CHIPS, 64, POOL_GROUP).transpose(1, 0, 2, 3).reshape(N_CHIPS, 4 * 64, POOL_GROUP)])
    token = rest.second(token[0] if token else None)
    grad_x, d_attn_norm = _in_proj_bwd(du, dzq, dzg, win, xs, attn_norm, dy1, tm, deps=token)
    small_parts = [d_attn_norm, d_bgate, d_pscale, d_qn, d_kn, dsink, d_ffn_norm, dcb_g, dcb_v, dcw_g, dcw_v]
    return loss_acc, grad_x, small_parts


def kernel(x, positions, attn_norm, w_in, b_gate, w_pool, pool_scale, q_norm, k_norm, sinks, w_out, ffn_norm, w_up, conv_w, conv_b, w_down, loss_target, m_attn_norm, m_w_in, m_b_gate, m_w_pool, m_pool_scale, m_q_norm, m_k_norm, m_sinks, m_w_out, m_ffn_norm, m_w_up, m_conv_w, m_conv_b, m_w_down, v_attn_norm, v_w_in, v_b_gate, v_w_pool, v_pool_scale, v_q_norm, v_k_norm, v_sinks, v_w_out, v_ffn_norm, v_w_up, v_conv_w, v_conv_b, v_w_down):
    s = x.shape[1]
    xs = x[0]
    tgt = loss_target[0]
    pos = positions[0].reshape(s, 1)
    cx, cy, cc = lax.axis_index("x"), lax.axis_index("y"), lax.axis_index("c")
    chip = 2 * cx + cy

    chip_arr = chip.reshape(1).astype(jnp.int32)
    dev_arr = (2 * chip + cc).reshape(1).astype(jnp.int32)
    place = jnp.stack([chip, cc]).astype(jnp.int32)

    g_in, g_pool, g_cw = _all_gather_weights(
        [jnp.swapaxes(w_in[0], 0, 1), w_pool[0].reshape(4 * 64, POOL_GROUP)], [conv_w[0]])
    win = g_in.reshape(IN_W, D)
    wpool = g_pool.reshape(N_CHIPS, 4, 64, POOL_GROUP).transpose(1, 0, 2, 3).reshape(4, POOL_GROUP, POOL_GROUP)
    late = _LateWeights([w_up[0], w_out[0], w_down[0]], chip_arr, ["w_up", "w_out", "w_down"], [256, 256, 352])
    early = _GradReduce("early", place, ["w_down", "w_up"], [176, 256])
    rest = _GradReduce("rest", place, ["w_in", "w_out", "w_pool"], [272, 128, 128])

    loss_acc, grad_x, small_parts = _forward_backward(
        xs, pos, tgt, win, wpool, g_cw, attn_norm, b_gate, pool_scale, q_norm, k_norm, sinks, ffn_norm, conv_b,
        late, early, rest)
    loss = lax.psum(jnp.sum(loss_acc) * (0.5 / D), ("x", "y", "c"))

    def two_d(a):
        return a.reshape(-1, a.shape[-1])

    def update(nm, w, g, m, v, tr, deps=()):
        res = _adamw(two_d(w), g, two_d(m), two_d(v), tr, "adamw_" + nm, deps=deps)
        return [r.reshape(w.shape) for r in res]

    small = _pack_small(*small_parts, dev_arr)
    g_wdown, g_wup = early.last(grad_x)
    big_out = {"w_up": update("w_up", w_up, g_wup, m_w_up, v_w_up, 256)}
    big_out["w_down"] = update("w_down", w_down, g_wdown, m_w_down, v_w_down, 176, deps=(big_out["w_up"][1],))
    token = rest.third(big_out["w_down"][1], small=small)
    g_win_t, g_wout, g_wpool = rest.last(token[0])
    small_out, g_convw_all = _small_update(rest.small_all, {
        "attn_norm": (attn_norm, m_attn_norm, v_attn_norm), "b_gate": (b_gate, m_b_gate, v_b_gate),
        "pool_scale": (pool_scale, m_pool_scale, v_pool_scale), "q_norm": (q_norm, m_q_norm, v_q_norm),
        "k_norm": (k_norm, m_k_norm, v_k_norm), "sinks": (sinks, m_sinks, v_sinks),
        "ffn_norm": (ffn_norm, m_ffn_norm, v_ffn_norm), "conv_b": (conv_b, m_conv_b, v_conv_b)})
    g_convw = lax.dynamic_slice_in_dim(g_convw_all, chip * UP_SHARD, UP_SHARD, axis=1)
    small_out["conv_w"] = update("conv_w", conv_w, g_convw, m_conv_w, v_conv_w, 3)
    flip = lambda a: jnp.swapaxes(a[0], 0, 1)
    res = _adamw(flip(w_in), g_win_t, flip(m_w_in), flip(v_w_in), 272, "adamw_w_in")
    big_out["w_in"] = [jnp.swapaxes(r, 0, 1)[None] for r in res]
    big_out["w_out"] = update("w_out", w_out, g_wout, m_w_out, v_w_out, 128)
    big_out["w_pool"] = update("w_pool", w_pool, g_wpool, m_w_pool, v_w_pool, 128)

    order = ["attn_norm", "w_in", "b_gate", "w_pool", "pool_scale", "q_norm", "k_norm", "sinks", "w_out",
             "ffn_norm", "w_up", "conv_w", "conv_b", "w_down"]
    allout = {**big_out, **small_out}
    outs = [loss, grad_x[None]]
    for k in range(4):
        outs += [allout[nm][k] for nm in order]
    return tuple(outs)
```

```python
import functools

import jax
import jax.numpy as jnp
from jax import lax
from jax.experimental import pallas as pl
from jax.experimental.pallas import tpu as pltpu

D = 1024
D_FF = 2816
HEAD = 64
N_Q = 16
N_KV = 2
GQA = 8
BLK = 128
ROPE_DIM = 16
ROPE_THETA = 500000.0
POOL_GROUP = 256
Q_W = 1024
KV_W = 128
QKV_W = Q_W + 2 * KV_W
IN_W = 4352
UP_SHARD = 1408
EPS = 1e-6
N_CHIPS = 4
N_DEV = 8

LR = 0.001
B1 = 0.9
B2 = 0.999
ADAM_EPS = 1e-08
WD = 0.01
STEP = 10

BF = jnp.bfloat16
F32 = jnp.float32
MESH = pl.DeviceIdType.MESH
VMEM_LIMIT_MB = 56


def _cp(sem, vmem_mb=VMEM_LIMIT_MB):
    return pltpu.CompilerParams(dimension_semantics=sem, vmem_limit_bytes=vmem_mb << 20)


def _full(shape):
    nd = len(shape)
    return pl.BlockSpec(shape, lambda *_: (0,) * nd)


def _sds(shape, dtype):
    return jax.ShapeDtypeStruct(shape, dtype)


def _after(body, n_in, deps):
    nd = len(deps)
    if nd == 0:
        return body

    def ordered(*refs):
        return body(*refs[:n_in], *refs[n_in + nd:])

    return ordered


def _any_specs(deps):
    return [pl.BlockSpec(memory_space=pl.ANY)] * len(deps)


def _nt(a, b):
    return lax.dot_general(a, b, (((1,), (1,)), ((), ())), preferred_element_type=F32)


def _tn(a, b):
    return lax.dot_general(a, b, (((0,), (0,)), ((), ())), preferred_element_type=F32)


def _mm(a, b):
    return jnp.dot(a, b, preferred_element_type=F32)


def _head_sum(v, bd):
    return _mm(v.astype(BF), bd)


def _rope_tables(pos_ref, invf_ref):
    ang = pos_ref[...].astype(F32) * invf_ref[...]
    cos = jnp.cos(ang)
    sin = jnp.sin(ang)
    lane = lax.broadcasted_iota(jnp.int32, (1, 2 * HEAD), 1) % HEAD
    sa = jnp.where(lane < ROPE_DIM // 2, -sin, 0.0)
    sb = jnp.where(lane < ROPE_DIM // 2, 0.0, jnp.where(lane < ROPE_DIM, sin, 0.0))
    return cos, sa, sb


def _tile_lanes(t, reps):
    return t if reps == 1 else jnp.tile(t, (1, reps))


def _rope(v, cos, sa, sb):
    w = v.shape[1]
    reps = w // (2 * HEAD)
    half = ROPE_DIM // 2
    return (v * _tile_lanes(cos, reps) + pltpu.roll(v, w - half, 1) * _tile_lanes(sa, reps)
            + pltpu.roll(v, half, 1) * _tile_lanes(sb, reps))


def _rope_t(dy, cos, sa, sb):
    w = dy.shape[1]
    reps = w // (2 * HEAD)
    half = ROPE_DIM // 2
    return (dy * _tile_lanes(cos, reps) + pltpu.roll(dy * _tile_lanes(sa, reps), half, 1)
            + pltpu.roll(dy * _tile_lanes(sb, reps), w - half, 1))


def _attn_in_proj(x, g1, w_in, b_gate, tm, deps=()):
    s = x.shape[0]

    def body(x_ref, g_ref, w_ref, b_ref, h_ref, u_ref, qkv_ref, gate_ref):
        xv = x_ref[...]
        r = lax.rsqrt(jnp.mean(xv * xv, axis=-1, keepdims=True) + EPS)
        h = (xv * r * g_ref[...]).astype(BF)
        h_ref[...] = h
        u_ref[...] = _nt(h, w_ref[0:D, :])
        qkv_ref[...] = _nt(h, w_ref[D:D + QKV_W, :])
        gate_ref[...] = jax.nn.sigmoid(_nt(h, w_ref[D + QKV_W:IN_W, :]) + b_ref[...]).astype(BF)

    row = lambda w: pl.BlockSpec((tm, w), lambda i: (i, 0))
    return pl.pallas_call(
        _after(body, 4, deps), grid=(s // tm,),
        in_specs=[row(D), _full((1, D)), _full((IN_W, D)), _full((1, 2 * D))] + _any_specs(deps),
        out_specs=[row(D), row(D), row(QKV_W), row(2 * D)],
        out_shape=[_sds((s, D), BF), _sds((s, D), F32), _sds((s, QKV_W), F32), _sds((s, 2 * D), BF)],
        compiler_params=_cp(("parallel",)), name="attn_in_proj")(x, g1, w_in, b_gate, *deps)


def _qk_prep(qkv, pos, wq, wk, invf, bd, tm):
    s = qkv.shape[0]

    def body(qkv_ref, pos_ref, wq_ref, wk_ref, invf_ref, bd_ref, qh_ref, kh_ref, vh_ref):
        cos, sa, sb = _rope_tables(pos_ref, invf_ref)
        q = qkv_ref[:, 0:Q_W]
        k = qkv_ref[:, Q_W:Q_W + KV_W]
        v = qkv_ref[:, Q_W + KV_W:QKV_W]
        rq = lax.rsqrt(_head_sum(q * q, bd_ref[...]) * (1.0 / HEAD) + EPS)
        qr = _rope(q * rq * wq_ref[...], cos, sa, sb) * (HEAD ** -0.5)
        rk = lax.rsqrt(_head_sum(k * k, bd_ref[0:KV_W, 0:KV_W]) * (1.0 / HEAD) + EPS)
        kr = _rope(k * rk * wk_ref[...], cos, sa, sb)
        for h in range(N_Q):
            qh_ref[h] = qr[:, HEAD * h:HEAD * (h + 1)].astype(BF)
        for h in range(N_KV):
            kh_ref[h] = kr[:, HEAD * h:HEAD * (h + 1)].astype(BF)
            vh_ref[h] = v[:, HEAD * h:HEAD * (h + 1)].astype(BF)

    heads = lambda n: pl.BlockSpec((n, tm, HEAD), lambda i: (0, i, 0))
    return pl.pallas_call(
        body, grid=(s // tm,),
        in_specs=[pl.BlockSpec((tm, QKV_W), lambda i: (i, 0)), pl.BlockSpec((tm, 1), lambda i: (i, 0)),
                  _full((1, Q_W)), _full((1, KV_W)), _full((1, 2 * HEAD)), _full((Q_W, Q_W))],
        out_specs=[heads(N_Q), heads(N_KV), heads(N_KV)],
        out_shape=[_sds((N_Q, s, HEAD), BF), _sds((N_KV, s, HEAD), BF), _sds((N_KV, s, HEAD), BF)],
        compiler_params=_cp(("parallel",)), name="qk_prep")(qkv, pos, wq, wk, invf, bd)


def _sink_column(sink_ref, kh):
    row_g = lax.broadcasted_iota(jnp.int32, (GQA * BLK, 1), 0) // BLK
    col = jnp.zeros((GQA * BLK, 1), F32)
    for g in range(GQA):
        col = jnp.where(row_g == g, sink_ref[kh * GQA + g], col)
    return col


def _attn_probs(q, k, n, sink_col):
    sc = _nt(q, k)
    qi = lax.broadcasted_iota(jnp.int32, sc.shape, 0) % BLK + BLK
    ki = lax.broadcasted_iota(jnp.int32, sc.shape, 1)
    lo = jnp.where(n > 0, qi - BLK, BLK - 1)
    ok = (ki <= qi) & (ki > lo)
    sc = jnp.where(ok, sc, -jnp.inf)
    m = jnp.maximum(jnp.max(sc, axis=-1, keepdims=True), sink_col)
    p = jnp.exp(sc - m)
    es = jnp.exp(sink_col - m)
    inv = 1.0 / (jnp.sum(p, axis=-1, keepdims=True) + es)
    return p * inv, es * inv


def _attn_fwd(qh, kh, vh, sinks, deps=()):
    s = qh.shape[1]
    nb = s // BLK

    def body(sink_ref, q_ref, kp_ref, kc_ref, vp_ref, vc_ref, o_ref):
        n = pl.program_id(0)
        for khd in range(N_KV):
            q = q_ref[khd * GQA:(khd + 1) * GQA].reshape(GQA * BLK, HEAD)
            k = jnp.concatenate([kp_ref[khd], kc_ref[khd]], axis=0)
            v = jnp.concatenate([vp_ref[khd], vc_ref[khd]], axis=0)
            probs, _ = _attn_probs(q, k, n, _sink_column(sink_ref, khd))
            o = _mm(probs.astype(BF), v)
            for j in range(GQA // 2):
                c0 = khd * GQA * HEAD + 2 * HEAD * j
                o_ref[:, c0:c0 + 2 * HEAD] = jnp.concatenate(
                    [o[2 * j * BLK:(2 * j + 1) * BLK], o[(2 * j + 1) * BLK:(2 * j + 2) * BLK]], axis=1).astype(BF)

    prev = pl.BlockSpec((N_KV, BLK, HEAD), lambda n: (0, jnp.maximum(n - 1, 0), 0))
    cur = pl.BlockSpec((N_KV, BLK, HEAD), lambda n: (0, n, 0))
    return pl.pallas_call(
        _after(body, 6, deps), grid=(nb,),
        in_specs=[pl.BlockSpec(memory_space=pltpu.SMEM),
                  pl.BlockSpec((N_Q, BLK, HEAD), lambda n: (0, n, 0)), prev, cur, prev, cur] + _any_specs(deps),
        out_specs=pl.BlockSpec((BLK, Q_W), lambda n: (n, 0)),
        out_shape=_sds((s, Q_W), BF),
        compiler_params=_cp(("parallel",)), name="attn_fwd")(sinks, qh, kh, kh, vh, vh, *deps)


def _pool_fwd(u, w_pool, pool_scale, ts):
    s = u.shape[0]
    halo = 16

    def body(u_ref, wp_ref, ps_ref, a_ref, pooled_ref, prev):
        g = pl.program_id(0)
        i = pl.program_id(1)

        @pl.when(i == 0)
        def _():
            prev[...] = jnp.zeros_like(prev)

        cur = u_ref[...]
        ext = jnp.concatenate([prev[...], cur], axis=0)
        t = (i * ts + lax.broadcasted_iota(jnp.int32, (ts, 1), 0)).astype(F32)
        for gi in range(4):
            @pl.when(g == gi)
            def _(gi=gi):
                w = 2 << gi
                acc, span = ext, 1
                while span < w:
                    acc = acc + pltpu.roll(acc, span, 0)
                    span *= 2
                inv = 1.0 / jnp.minimum(t + 1.0, float(w))
                pooled = (acc[halo:halo + ts] * inv - cur).astype(BF)
                pooled_ref[...] = pooled
                a_ref[...] = (_mm(pooled, wp_ref[0]) * ps_ref[...]).astype(BF)

        prev[...] = cur[ts - halo:ts]

    col = pl.BlockSpec((ts, POOL_GROUP), lambda g, i: (i, g))
    return pl.pallas_call(
        body, grid=(4, s // ts),
        in_specs=[col, pl.BlockSpec((1, POOL_GROUP, POOL_GROUP), lambda g, i: (g, 0, 0)),
                  pl.BlockSpec((1, POOL_GROUP), lambda g, i: (0, g))],
        out_specs=[col, col],
        out_shape=[_sds((s, D), BF), _sds((s, D), BF)],
        scratch_shapes=[pltpu.VMEM((halo, POOL_GROUP), F32)],
        compiler_params=_cp(("parallel", "arbitrary")), name="pool_fwd")(u, w_pool, pool_scale)


def _mix_out_proj(a, b, gates, x, w_out, g2, tm):
    s = x.shape[0]

    def body(a_ref, b_ref, gate_ref, x_ref, w_ref, g_ref, mix_ref, y_ref, h_ref):
        mix = (gate_ref[:, 0:D].astype(F32) * a_ref[...].astype(F32)
               + gate_ref[:, D:2 * D].astype(F32) * b_ref[...].astype(F32)).astype(BF)
        mix_ref[...] = mix
        y = x_ref[...] + _mm(mix, w_ref[...])
        y_ref[...] = y
        r = lax.rsqrt(jnp.mean(y * y, axis=-1, keepdims=True) + EPS)
        h_ref[...] = (y * r * g_ref[...]).astype(BF)

    row = lambda w: pl.BlockSpec((tm, w), lambda i: (i, 0))
    return pl.pallas_call(
        body, grid=(s // tm,),
        in_specs=[row(D), row(D), row(2 * D), row(D), _full((D, D)), _full((1, D))],
        out_specs=[row(D), row(D), row(D)],
        out_shape=[_sds((s, D), BF), _sds((s, D), F32), _sds((s, D), BF)],
        compiler_params=_cp(("parallel",)), name="mix_out_proj")(a, b, gates, x, w_out, g2)


def _ffn_up(h2, w_up, conv_w, conv_b, tm):
    s = h2.shape[0]

    def body(h_ref, wg_ref, wv_ref, cwg_ref, cwv_ref, cbg_ref, cbv_ref,
             preg_ref, prev_ref, upg_ref, upv_ref, act_ref, halog, halov):
        i = pl.program_id(1)

        @pl.when(i == 0)
        def _():
            halog[...] = jnp.zeros_like(halog)
            halov[...] = jnp.zeros_like(halov)

        h = h_ref[...]

        def conv_half(w_ref, cw_ref, cb_ref, halo, pre_ref, up_ref):
            pre = _mm(h, w_ref[0])
            pre_ref[...] = pre.astype(BF)
            ext = jnp.concatenate([halo[...], pre], axis=0)
            cw = cw_ref[0]
            up = cb_ref[...] + cw[0:1] * pltpu.roll(ext, 2, 0)[8:8 + tm]
            up = up + cw[1:2] * pltpu.roll(ext, 1, 0)[8:8 + tm]
            up = up + cw[2:3] * pre
            halo[...] = pre[tm - 8:tm]
            up_ref[...] = up.astype(BF)
            return up

        gate = conv_half(wg_ref, cwg_ref, cbg_ref, halog, preg_ref, upg_ref)
        val = conv_half(wv_ref, cwv_ref, cbv_ref, halov, prev_ref, upv_ref)
        act_ref[...] = (gate * jax.nn.sigmoid(gate) * val).astype(BF)

    tile = pl.BlockSpec((tm, UP_SHARD), lambda j, i: (i, j))
    wspec = lambda off: pl.BlockSpec((1, D, UP_SHARD), lambda j, i: (j + off, 0, 0))
    cwspec = lambda off: pl.BlockSpec((1, 3, UP_SHARD), lambda j, i: (j + off, 0, 0))
    cbspec = lambda off: pl.BlockSpec((1, UP_SHARD), lambda j, i: (0, j + off))
    half = _sds((s, D_FF), BF)
    return pl.pallas_call(
        body, grid=(2, s // tm),
        in_specs=[pl.BlockSpec((tm, D), lambda j, i: (i, 0)), wspec(0), wspec(2), cwspec(0), cwspec(2),
                  cbspec(0), cbspec(2)],
        out_specs=[tile] * 5, out_shape=[half] * 5,
        scratch_shapes=[pltpu.VMEM((8, UP_SHARD), F32), pltpu.VMEM((8, UP_SHARD), F32)],
        compiler_params=_cp(("parallel", "arbitrary")), name="ffn_up")(
            h2, w_up, w_up, conv_w, conv_w, conv_b, conv_b)


def _ffn_down_loss(act, w_down, y1, tgt, tm):
    s = y1.shape[0]

    def body(act_ref, w_ref, y_ref, t_ref, dy_ref, dyb_ref, loss_ref):
        @pl.when(pl.program_id(0) == 0)
        def _():
            loss_ref[...] = jnp.zeros_like(loss_ref)

        e = y_ref[...] + _mm(act_ref[...], w_ref[...]) - t_ref[...]
        dy = e * (1.0 / D)
        dy_ref[...] = dy
        dyb_ref[...] = dy.astype(BF)
        e2 = (e * e).reshape(tm // 8, 8, D).sum(axis=0)
        part = e2[:, 0:128]
        for j in range(1, D // 128):
            part = part + e2[:, 128 * j:128 * (j + 1)]
        loss_ref[...] += part

    row = lambda w: pl.BlockSpec((tm, w), lambda i: (i, 0))
    return pl.pallas_call(
        body, grid=(s // tm,),
        in_specs=[row(D_FF), _full((D_FF, D)), row(D), row(D)],
        out_specs=[row(D), row(D), _full((8, 128))],
        out_shape=[_sds((s, D), F32), _sds((s, D), BF), _sds((8, 128), F32)],
        compiler_params=_cp(("arbitrary",)), name="ffn_down_loss")(act, w_down, y1, tgt)


def _grad_matmul(a, b, tn, tk, name, lead=None, prev=None, lead_off=0):
    s, m = a.shape
    n = b.shape[1]
    nj = n // tn

    def body(*refs):
        a_ref, b_ref = refs[0], refs[1]
        o_ref = refs[-1]
        acc = _tn(a_ref[...], b_ref[...])
        acc = acc if lead is None else acc[None]

        @pl.when(pl.program_id(1) == 0)
        def _():
            o_ref[...] = acc

        @pl.when(pl.program_id(1) > 0)
        def _():
            o_ref[...] += acc

    in_specs = [pl.BlockSpec((tk, m), lambda j, k: (k, 0)), pl.BlockSpec((tk, tn), lambda j, k: (k, j))]
    args = [a, b]
    aliases = {}
    if lead is None:
        out_spec = pl.BlockSpec((m, tn), lambda j, k: (0, j))
        out_shape = _sds((m, n), F32)
    else:
        out_spec = pl.BlockSpec((1, m, tn), lambda j, k: (j + lead_off, 0, 0))
        out_shape = _sds((lead, m, tn), F32)
        if prev is not None:
            in_specs.append(pl.BlockSpec(memory_space=pl.ANY))
            args.append(prev)
            aliases = {2: 0}
    return pl.pallas_call(
        body, grid=(nj, s // tk), in_specs=in_specs, out_specs=out_spec, out_shape=out_shape,
        input_output_aliases=aliases,
        compiler_params=_cp(("parallel", "arbitrary")), name=name)(*args)


def _ffn_act_bwd(dyb, w_down, up_g, up_v, pre_g, pre_v, conv_w, tm):
    s = dyb.shape[0]
    nt = s // tm

    def body(dy_ref, wd_ref, upg_ref, upv_ref, preg_ref, prev_ref, cwg_ref, cwv_ref,
             dpg_ref, dpv_ref, dcwg_ref, dcwv_ref, dcbg_ref, dcbv_ref, nxg, nxv):
        i = pl.program_id(1)

        @pl.when(i == 0)
        def _():
            nxg[...] = jnp.zeros_like(nxg)
            nxv[...] = jnp.zeros_like(nxv)
            dcwg_ref[...] = jnp.zeros_like(dcwg_ref)
            dcwv_ref[...] = jnp.zeros_like(dcwv_ref)
            dcbg_ref[...] = jnp.zeros_like(dcbg_ref)
            dcbv_ref[...] = jnp.zeros_like(dcbv_ref)

        dact = _nt(dy_ref[...], wd_ref[...])
        g = upg_ref[...].astype(F32)
        v = upv_ref[...].astype(F32)
        sg = jax.nn.sigmoid(g)
        d_v = dact * (g * sg)
        d_g = dact * v * (sg * (1.0 + g * (1.0 - sg)))

        def conv_bwd(d_up, nx, pre_ref, cw_ref, dp_ref, dcw_ref, dcb_ref):
            ext = jnp.concatenate([d_up, nx[...]], axis=0)
            s1 = pltpu.roll(ext, tm + 8 - 1, 0)[0:tm]
            s2 = pltpu.roll(ext, tm + 8 - 2, 0)[0:tm]
            cw = cw_ref[0]
            dp_ref[...] = (cw[2:3] * d_up + cw[1:2] * s1 + cw[0:1] * s2).astype(BF)
            nx[...] = d_up[0:8]
            pre = pre_ref[...].astype(F32)
            dcw_ref[0, 0:1, :] += jnp.sum(s2 * pre, axis=0, keepdims=True)
            dcw_ref[0, 1:2, :] += jnp.sum(s1 * pre, axis=0, keepdims=True)
            dcw_ref[0, 2:3, :] += jnp.sum(d_up * pre, axis=0, keepdims=True)
            dcb_ref[...] += jnp.sum(d_up, axis=0, keepdims=True)

        conv_bwd(d_g, nxg, preg_ref, cwg_ref, dpg_ref, dcwg_ref, dcbg_ref)
        conv_bwd(d_v, nxv, prev_ref, cwv_ref, dpv_ref, dcwv_ref, dcbv_ref)

    tile = pl.BlockSpec((tm, UP_SHARD), lambda j, i: (nt - 1 - i, j))
    cwspec = lambda off: pl.BlockSpec((1, 3, UP_SHARD), lambda j, i: (j + off, 0, 0))
    acc_cw = pl.BlockSpec((1, 3, UP_SHARD), lambda j, i: (j, 0, 0))
    acc_cb = pl.BlockSpec((1, UP_SHARD), lambda j, i: (0, j))
    buf = pltpu.VMEM((8, UP_SHARD), F32)
    return pl.pallas_call(
        body, grid=(2, nt),
        in_specs=[pl.BlockSpec((tm, D), lambda j, i: (nt - 1 - i, 0)),
                  pl.BlockSpec((UP_SHARD, D), lambda j, i: (j, 0)),
                  tile, tile, tile, tile, cwspec(0), cwspec(2)],
        out_specs=[tile, tile, acc_cw, acc_cw, acc_cb, acc_cb],
        out_shape=[_sds((s, D_FF), BF), _sds((s, D_FF), BF), _sds((2, 3, UP_SHARD), F32),
                   _sds((2, 3, UP_SHARD), F32), _sds((1, D_FF), F32), _sds((1, D_FF), F32)],
        scratch_shapes=[buf, buf],
        compiler_params=_cp(("parallel", "arbitrary")), name="ffn_act_bwd")(
            dyb, w_down, up_g, up_v, pre_g, pre_v, conv_w, conv_w)


def _rms_bwd(dh, y, g):
    r = lax.rsqrt(jnp.mean(y * y, axis=-1, keepdims=True) + EPS)
    n = y * r
    dn = dh * g
    return r * (dn - n * jnp.mean(dn * n, axis=-1, keepdims=True)), dh * n


def _ffn_up_bwd(dp_g, dp_v, w_up, y1, dy2, g2, tm, deps=()):
    s = y1.shape[0]

    def body(dg_ref, dv_ref, w_ref, y_ref, dy2_ref, g_ref, dy1_ref, dgn_ref):
        @pl.when(pl.program_id(0) == 0)
        def _():
            dgn_ref[...] = jnp.zeros_like(dgn_ref)

        dh = _nt(dg_ref[:, 0:UP_SHARD], w_ref[0])
        dh = dh + _nt(dg_ref[:, UP_SHARD:D_FF], w_ref[1])
        dh = dh + _nt(dv_ref[:, 0:UP_SHARD], w_ref[2])
        dh = dh + _nt(dv_ref[:, UP_SHARD:D_FF], w_ref[3])
        dy, dgn = _rms_bwd(dh, y_ref[...], g_ref[...])
        dy1_ref[...] = dy2_ref[...] + dy
        dgn_ref[...] += jnp.sum(dgn, axis=0, keepdims=True)

    row = lambda w: pl.BlockSpec((tm, w), lambda i: (i, 0))
    return pl.pallas_call(
        _after(body, 6, deps), grid=(s // tm,),
        in_specs=[row(D_FF), row(D_FF), _full((4, D, UP_SHARD)), row(D), row(D), _full((1, D))] + _any_specs(deps),
        out_specs=[row(D), _full((1, D))],
        out_shape=[_sds((s, D), F32), _sds((1, D), F32)],
        compiler_params=_cp(("arbitrary",)), name="ffn_up_bwd")(dp_g, dp_v, w_up, y1, dy2, g2, *deps)


def _out_proj_bwd(dy1, w_out, a, b, gates, mix, tm, deps=()):
    s = dy1.shape[0]

    def body(dy_ref, w_ref, a_ref, b_ref, gate_ref, mix_ref, da_ref, db_ref, dzg_ref, dbg_ref, dw_ref):
        @pl.when(pl.program_id(0) == 0)
        def _():
            dbg_ref[...] = jnp.zeros_like(dbg_ref)
            dw_ref[...] = jnp.zeros_like(dw_ref)

        dyb = dy_ref[...].astype(BF)
        dmix = _nt(dyb, w_ref[...])
        gp = gate_ref[:, 0:D].astype(F32)
        ga = gate_ref[:, D:2 * D].astype(F32)
        da_ref[...] = (dmix * gp).astype(BF)
        db_ref[...] = (dmix * ga).astype(BF)
        dzp = dmix * a_ref[...].astype(F32) * (gp * (1.0 - gp))
        dza = dmix * b_ref[...].astype(F32) * (ga * (1.0 - ga))
        dzg_ref[:, 0:D] = dzp.astype(BF)
        dzg_ref[:, D:2 * D] = dza.astype(BF)
        dbg_ref[:, 0:D] += jnp.sum(dzp, axis=0, keepdims=True)
        dbg_ref[:, D:2 * D] += jnp.sum(dza, axis=0, keepdims=True)
        dw_ref[...] += _tn(mix_ref[...], dyb)

    row = lambda w: pl.BlockSpec((tm, w), lambda i: (i, 0))
    return pl.pallas_call(
        _after(body, 6, deps), grid=(s // tm,),
        in_specs=[row(D), _full((D, D)), row(D), row(D), row(2 * D), row(D)] + _any_specs(deps),
        out_specs=[row(D), row(D), row(2 * D), _full((1, 2 * D)), _full((D, D))],
        out_shape=[_sds((s, D), BF), _sds((s, D), BF), _sds((s, 2 * D), BF), _sds((1, 2 * D), F32),
                   _sds((D, D), F32)],
        compiler_params=_cp(("arbitrary",)), name="out_proj_bwd")(dy1, w_out, a, b, gates, mix, *deps)


def _pool_bwd(da, pooled, w_pool, pool_scale, h1, ts):
    s = da.shape[0]
    nt = s // ts
    halo = 16

    def body(da_ref, pooled_ref, wp_ref, ps_ref, h_ref, du_ref, dwp_ref, dps_ref, dwi_ref, nxt):
        g = pl.program_id(0)
        i = pl.program_id(1)
        ti = nt - 1 - i

        @pl.when(i == 0)
        def _():
            nxt[...] = jnp.zeros_like(nxt)
            dwp_ref[...] = jnp.zeros_like(dwp_ref)
            dps_ref[...] = jnp.zeros_like(dps_ref)
            dwi_ref[...] = jnp.zeros_like(dwi_ref)

        pooled = pooled_ref[...]
        dav = da_ref[...].astype(F32)
        dps_ref[...] += jnp.sum(dav * _mm(pooled, wp_ref[0]), axis=0, keepdims=True)
        dm = (dav * ps_ref[...]).astype(BF)
        dwp_ref[0] += _tn(pooled, dm)
        dpool = _nt(dm, wp_ref[0])
        t = (ti * ts + lax.broadcasted_iota(jnp.int32, (ts, 1), 0)).astype(F32)
        for gi in range(4):
            @pl.when(g == gi)
            def _(gi=gi):
                w = 2 << gi
                e = dpool * (1.0 / jnp.minimum(t + 1.0, float(w)))
                acc, span = jnp.concatenate([e, nxt[...]], axis=0), 1
                while span < w:
                    acc = acc + pltpu.roll(acc, ts + halo - span, 0)
                    span *= 2
                du = (acc[0:ts] - dpool).astype(BF)
                du_ref[...] = du
                dwi_ref[...] += _tn(du, h_ref[...])
                nxt[...] = e[0:halo]

    col = pl.BlockSpec((ts, POOL_GROUP), lambda g, i: (nt - 1 - i, g))
    return pl.pallas_call(
        body, grid=(4, nt),
        in_specs=[col, col, pl.BlockSpec((1, POOL_GROUP, POOL_GROUP), lambda g, i: (g, 0, 0)),
                  pl.BlockSpec((1, POOL_GROUP), lambda g, i: (0, g)),
                  pl.BlockSpec((ts, D), lambda g, i: (nt - 1 - i, 0))],
        out_specs=[col, pl.BlockSpec((1, POOL_GROUP, POOL_GROUP), lambda g, i: (g, 0, 0)),
                   pl.BlockSpec((1, POOL_GROUP), lambda g, i: (0, g)),
                   pl.BlockSpec((POOL_GROUP, D), lambda g, i: (g, 0))],
        out_shape=[_sds((s, D), BF), _sds((4, POOL_GROUP, POOL_GROUP), F32), _sds((1, D), F32), _sds((D, D), F32)],
        scratch_shapes=[pltpu.VMEM((halo, POOL_GROUP), F32)],
        compiler_params=_cp(("parallel", "arbitrary")), name="pool_bwd")(da, pooled, w_pool, pool_scale, h1)


def _attn_bwd(qh, kh, vh, sinks, db, deps=()):
    s = qh.shape[1]
    nb = s // BLK

    def body(sink_ref, q_ref, kp_ref, kc_ref, vp_ref, vc_ref, do_ref,
             dq_ref, dk_ref, dv_ref, dsink_ref, ck, cv):
        n = pl.program_id(0)

        @pl.when(n == 0)
        def _():
            ck[...] = jnp.zeros_like(ck)
            cv[...] = jnp.zeros_like(cv)
            dsink_ref[...] = jnp.zeros_like(dsink_ref)

        @pl.when(n < nb)
        def _():
            dov = do_ref[...]
            for khd in range(N_KV):
                q = q_ref[khd * GQA:(khd + 1) * GQA].reshape(GQA * BLK, HEAD)
                k = jnp.concatenate([kp_ref[khd], kc_ref[khd]], axis=0)
                v = jnp.concatenate([vp_ref[khd], vc_ref[khd]], axis=0)
                c0 = khd * GQA * HEAD
                do = jnp.concatenate([dov[:, c0 + HEAD * g:c0 + HEAD * (g + 1)] for g in range(GQA)],
                                     axis=0).astype(BF)
                probs, psink = _attn_probs(q, k, n, _sink_column(sink_ref, khd))
                dp = _nt(do, v)
                delta = jnp.sum(probs * dp, axis=-1, keepdims=True)
                ds = (probs * (dp - delta)).astype(BF)
                dq_ref[khd * GQA:(khd + 1) * GQA] = _mm(ds, k).reshape(GQA, BLK, HEAD)
                dk = _tn(ds, q)
                dv = _tn(probs.astype(BF), do)
                dk_ref[khd] = ck[khd] + dk[0:BLK]
                dv_ref[khd] = cv[khd] + dv[0:BLK]
                ck[khd] = dk[BLK:2 * BLK]
                cv[khd] = dv[BLK:2 * BLK]
                dsk = psink * delta
                lane = lax.broadcasted_iota(jnp.int32, (1, 128), 1)
                acc = jnp.zeros((1, 128), F32)
                for g in range(GQA):
                    acc = acc - jnp.where(lane == khd * GQA + g,
                                          jnp.sum(dsk[g * BLK:(g + 1) * BLK], axis=0, keepdims=True), 0.0)
                dsink_ref[...] += acc

        @pl.when(n == nb)
        def _():
            dk_ref[...] = ck[...]
            dv_ref[...] = cv[...]

    last = nb - 1
    prev = pl.BlockSpec((N_KV, BLK, HEAD), lambda n: (0, jnp.maximum(jnp.minimum(n, last) - 1, 0), 0))
    cur = pl.BlockSpec((N_KV, BLK, HEAD), lambda n: (0, jnp.minimum(n, last), 0))
    kv_out = pl.BlockSpec((N_KV, BLK, HEAD), lambda n: (0, jnp.maximum(n - 1, 0), 0))
    return pl.pallas_call(
        _after(body, 7, deps), grid=(nb + 1,),
        in_specs=[pl.BlockSpec(memory_space=pltpu.SMEM),
                  pl.BlockSpec((N_Q, BLK, HEAD), lambda n: (0, jnp.minimum(n, last), 0)),
                  prev, cur, prev, cur,
                  pl.BlockSpec((BLK, Q_W), lambda n: (jnp.minimum(n, last), 0))] + _any_specs(deps),
        out_specs=[pl.BlockSpec((N_Q, BLK, HEAD), lambda n: (0, jnp.minimum(n, last), 0)), kv_out, kv_out,
                   _full((1, 128))],
        out_shape=[_sds((N_Q, s, HEAD), F32), _sds((N_KV, s, HEAD), F32), _sds((N_KV, s, HEAD), F32),
                   _sds((1, 128), F32)],
        scratch_shapes=[pltpu.VMEM((N_KV, BLK, HEAD), F32), pltpu.VMEM((N_KV, BLK, HEAD), F32)],
        compiler_params=_cp(("arbitrary",)), name="attn_bwd")(sinks, qh, kh, kh, vh, vh, db, *deps)


def _qk_prep_bwd(dqh, dkh, dvh, qkv, pos, wq, wk, invf, bd, tm, deps=()):
    s = qkv.shape[0]

    def fold_heads(row):
        out = row[:, 0:HEAD]
        for h in range(1, row.shape[1] // HEAD):
            out = out + row[:, HEAD * h:HEAD * (h + 1)]
        return out

    def body(dq_ref, dk_ref, dv_ref, qkv_ref, pos_ref, wq_ref, wk_ref, invf_ref, bd_ref,
             dz_ref, dwq_ref, dwk_ref):
        @pl.when(pl.program_id(0) == 0)
        def _():
            dwq_ref[...] = jnp.zeros_like(dwq_ref)
            dwk_ref[...] = jnp.zeros_like(dwk_ref)

        cos, sa, sb = _rope_tables(pos_ref, invf_ref)

        def norm_rope_bwd(dy, xin, w, bdm):
            dn = _rope_t(dy, cos, sa, sb)
            r = lax.rsqrt(_head_sum(xin * xin, bdm) * (1.0 / HEAD) + EPS)
            nh = xin * r
            gw = dn * w
            dx = r * (gw - nh * (_head_sum(gw * nh, bdm) * (1.0 / HEAD)))
            return dx, fold_heads(jnp.sum(dn * nh, axis=0, keepdims=True))

        dq = jnp.concatenate([dq_ref[h] for h in range(N_Q)], axis=1) * (HEAD ** -0.5)
        dk = jnp.concatenate([dk_ref[h] for h in range(N_KV)], axis=1)
        dxq, dwq = norm_rope_bwd(dq, qkv_ref[:, 0:Q_W], wq_ref[...], bd_ref[...])
        dxk, dwk = norm_rope_bwd(dk, qkv_ref[:, Q_W:Q_W + KV_W], wk_ref[...], bd_ref[0:KV_W, 0:KV_W])
        dz_ref[:, 0:Q_W] = dxq.astype(BF)
        dz_ref[:, Q_W:Q_W + KV_W] = dxk.astype(BF)
        dz_ref[:, Q_W + KV_W:QKV_W] = jnp.concatenate([dv_ref[h] for h in range(N_KV)], axis=1).astype(BF)
        dwq_ref[...] += dwq
        dwk_ref[...] += dwk

    heads = lambda n: pl.BlockSpec((n, tm, HEAD), lambda i: (0, i, 0))
    return pl.pallas_call(
        _after(body, 9, deps), grid=(s // tm,),
        in_specs=[heads(N_Q), heads(N_KV), heads(N_KV), pl.BlockSpec((tm, QKV_W), lambda i: (i, 0)),
                  pl.BlockSpec((tm, 1), lambda i: (i, 0)), _full((1, Q_W)), _full((1, KV_W)),
                  _full((1, 2 * HEAD)), _full((Q_W, Q_W))] + _any_specs(deps),
        out_specs=[pl.BlockSpec((tm, QKV_W), lambda i: (i, 0)), _full((1, HEAD)), _full((1, HEAD))],
        out_shape=[_sds((s, QKV_W), BF), _sds((1, HEAD), F32), _sds((1, HEAD), F32)],
        compiler_params=_cp(("arbitrary",)), name="qk_prep_bwd")(
            dqh, dkh, dvh, qkv, pos, wq, wk, invf, bd, *deps)


def _in_proj_bwd(du, dzq, dzg, w_in, x, g1, dy1, tm, deps=()):
    s = x.shape[0]

    def body(du_ref, dzq_ref, dzg_ref, w_ref, x_ref, g_ref, dy_ref, gx_ref, dgn_ref):
        @pl.when(pl.program_id(0) == 0)
        def _():
            dgn_ref[...] = jnp.zeros_like(dgn_ref)

        dh = _mm(du_ref[...], w_ref[0:D, :])
        dh = dh + _mm(dzq_ref[...], w_ref[D:D + QKV_W, :])
        dh = dh + _mm(dzg_ref[...], w_ref[D + QKV_W:IN_W, :])
        dx, dgn = _rms_bwd(dh, x_ref[...], g_ref[...])
        gx_ref[...] = dy_ref[...] + dx
        dgn_ref[...] += jnp.sum(dgn, axis=0, keepdims=True)

    row = lambda w: pl.BlockSpec((tm, w), lambda i: (i, 0))
    return pl.pallas_call(
        _after(body, 7, deps), grid=(s // tm,),
        in_specs=[row(D), row(QKV_W), row(2 * D), _full((IN_W, D)), row(D), _full((1, D)), row(D)] + _any_specs(deps),
        out_specs=[row(D), _full((1, D))],
        out_shape=[_sds((s, D), F32), _sds((1, D), F32)],
        compiler_params=_cp(("arbitrary",)), name="in_proj_bwd")(du, dzq, dzg, w_in, x, g1, dy1, *deps)


def _adamw_step(w, g, m, v):
    mn = B1 * m + (1.0 - B1) * g
    vn = B2 * v + (1.0 - B2) * (g * g)
    m_hat = mn / (1.0 - B1 ** STEP)
    v_hat = vn / (1.0 - B2 ** STEP)
    return -LR * (m_hat / (jnp.sqrt(v_hat) + ADAM_EPS) + WD * w), mn, vn


SMALL_ROWS = 16
SMALL_COLS = 2 * D_FF
SMALL_AT = {"b_gate": (1, 2 * D), "pool_scale": (2, D), "q_norm": (3, HEAD),
            "k_norm": (4, HEAD), "sinks": (5, N_Q), "ffn_norm": (6, D), "conv_b": (7, 2 * D_FF)}
SMALL_LOSS_ROW = 0
SMALL_CONV_W_ROW = 8


def _pack_small(loss_acc, d_bgate, d_pscale, d_qn, d_kn, dsink, d_ffn_norm, dcb_g, dcb_v, dcw_g, dcw_v, dev):
    def body(k_ref, ls_ref, bg_ref, ps_ref, qn_ref, kn_ref, sk_ref, fn_ref, cbg_ref, cbv_ref, cwg_ref, cwv_ref,
             o_ref):
        o_ref[...] = jnp.zeros_like(o_ref)
        o_ref[0, SMALL_LOSS_ROW:SMALL_LOSS_ROW + 1, 0:128] = jnp.sum(ls_ref[...], axis=0, keepdims=True)
        for nm, ref in (("b_gate", bg_ref), ("pool_scale", ps_ref), ("q_norm", qn_ref),
                        ("k_norm", kn_ref), ("ffn_norm", fn_ref)):
            row, n = SMALL_AT[nm]
            o_ref[0, row:row + 1, 0:n] = ref[...]
        row, _ = SMALL_AT["sinks"]
        o_ref[0, row:row + 1, 0:128] = sk_ref[...]
        row, _ = SMALL_AT["conv_b"]
        o_ref[0, row:row + 1, 0:D_FF] = cbg_ref[...]
        o_ref[0, row:row + 1, D_FF:2 * D_FF] = cbv_ref[...]
        for k in range(3):
            row = SMALL_CONV_W_ROW + k
            for half in range(2):
                o_ref[0, row:row + 1, half * UP_SHARD:(half + 1) * UP_SHARD] = cwg_ref[half, k:k + 1, :]
                o_ref[0, row:row + 1, (2 + half) * UP_SHARD:(3 + half) * UP_SHARD] = cwv_ref[half, k:k + 1, :]

    args = [loss_acc, d_bgate, d_pscale, d_qn, d_kn, dsink, d_ffn_norm, dcb_g, dcb_v, dcw_g, dcw_v]
    grid_spec = pltpu.PrefetchScalarGridSpec(
        num_scalar_prefetch=1, grid=(1,),
        in_specs=[pl.BlockSpec(a.shape, functools.partial(lambda nd, i, k: (0,) * nd, a.ndim)) for a in args],
        out_specs=pl.BlockSpec((1, SMALL_ROWS, SMALL_COLS), lambda i, k: (k[0], 0, 0)))
    return pl.pallas_call(body, grid_spec=grid_spec, out_shape=_sds((N_DEV, SMALL_ROWS, SMALL_COLS), F32),
                          name="pack_small")(dev, *args)


def _small_update(stack, attn_stack, params):
    names = list(params)

    def body(*refs):
        s_ref, a_ref = refs[0], refs[1]
        ins = refs[2:2 + 3 * len(names)]
        outs = refs[2 + 3 * len(names):]
        tot, tot_a = s_ref[0], a_ref[0]
        for d in range(1, N_DEV):
            tot = tot + s_ref[d]
            tot_a = tot_a + a_ref[d]
        for i, nm in enumerate(names):
            if nm == "attn_norm":
                g = tot_a
            else:
                row, n = SMALL_AT[nm]
                g = tot[row:row + 1, 0:n]
            delta, mn, vn = _adamw_step(ins[3 * i][...], g, ins[3 * i + 1][...], ins[3 * i + 2][...])
            outs[4 * i][...] = g
            outs[4 * i + 1][...] = delta
            outs[4 * i + 2][...] = mn
            outs[4 * i + 3][...] = vn
        outs[-2][...] = tot[SMALL_CONV_W_ROW:SMALL_CONV_W_ROW + 3, :]
        outs[-1][...] = jnp.sum(tot[SMALL_LOSS_ROW:SMALL_LOSS_ROW + 1, 0:128], axis=1, keepdims=True) * (0.5 / D)

    flat = [a for nm in names for a in params[nm]]
    out_shape = ([_sds(params[nm][0].shape, F32) for nm in names for _ in range(4)]
                 + [_sds((3, SMALL_COLS), F32), _sds((1, 1), F32)])
    res = pl.pallas_call(body, out_shape=out_shape, name="small_update")(stack, attn_stack, *flat)
    return {nm: list(res[4 * i:4 * i + 4]) for i, nm in enumerate(names)}, res[-2], res[-1]


def _adamw(w, g, m, v, tr, name, deps=()):
    r, c = w.shape

    def body(w_ref, g_ref, m_ref, v_ref, go_ref, d_ref, mo_ref, vo_ref):
        gv = g_ref[...]
        go_ref[...] = gv
        d_ref[...], mo_ref[...], vo_ref[...] = _adamw_step(w_ref[...], gv, m_ref[...], v_ref[...])

    blk = pl.BlockSpec((tr, c), lambda i: (i, 0))
    return pl.pallas_call(
        _after(body, 4, deps), grid=(r // tr,), in_specs=[blk] * 4 + _any_specs(deps), out_specs=[blk] * 4,
        out_shape=[_sds((r, c), F32)] * 4, compiler_params=_cp(("parallel",)), name=name)(w, g, m, v, *deps)


def _place():
    x, y, c = lax.axis_index("x"), lax.axis_index("y"), lax.axis_index("c")
    chips = [(1 - x, y), (x, 1 - y), (1 - x, 1 - y)]
    return x, y, c, chips


def _rows(ref, lead, h, rh):
    sl = pl.ds(pl.multiple_of(h * rh, 16), rh)
    return ref.at[sl, :] if lead is None else ref.at[lead, sl, :]


def _all_gather_weights(halved, whole):
    nh, nw = len(halved), len(whole)
    na = nh + nw
    arrays = list(halved) + list(whole)
    out_dtypes = [BF] * nh + [a.dtype for a in whole]
    cast_rows = 128

    def body(*refs):
        ins, outs = refs[:na], refs[na:2 * na]
        raw, stage = refs[2 * na:3 * na], refs[3 * na:3 * na + nh]
        ici_send, ici_recv, fwd_send, fwd_recv, in_sem, loc_sem = refs[3 * na + nh:]
        x, y, c, chips = _place()
        me = 2 * x + y
        sibling = (x, y, 1 - c)
        loads = [pltpu.make_async_copy(ins[a], raw[a], in_sem.at[a]) for a in range(na)]
        for cp in loads:
            cp.start()

        def ici(a, j, src_chip, src=None):
            if a < nh:
                rh = arrays[a].shape[0] // 2
                dst = _rows(outs[a], src_chip, c, rh)
                src = dst if src is None else _rows(src, None, c, rh)
            else:
                dst = outs[a].at[src_chip]
                src = dst if src is None else src
            return pltpu.make_async_remote_copy(
                src_ref=src, dst_ref=dst, send_sem=ici_send.at[3 * a + j], recv_sem=ici_recv.at[3 * a + j],
                device_id=(*chips[j], c), device_id_type=MESH)

        def fwd(a, j, half):
            rh = arrays[a].shape[0] // 2
            kj = 2 * chips[j][0] + chips[j][1]
            blk = _rows(outs[a], kj, half, rh)
            return pltpu.make_async_remote_copy(
                src_ref=blk, dst_ref=blk, send_sem=fwd_send.at[3 * a + j], recv_sem=fwd_recv.at[3 * a + j],
                device_id=sibling, device_id_type=MESH)

        local, sends = [], []
        for a in range(na):
            loads[a].wait()
            if a < nh:
                r = arrays[a].shape[0]
                for r0 in range(0, r, cast_rows):
                    r1 = min(r0 + cast_rows, r)
                    stage[a][r0:r1, :] = raw[a][r0:r1, :].astype(BF)
                own = stage[a]
            else:
                own = raw[a]
            cp = pltpu.make_async_copy(own, outs[a].at[me], loc_sem.at[a])
            cp.start()
            local.append(cp)
            for j in range(3):
                cp = ici(a, j, me, src=own)
                cp.start()
                sends.append(cp)
        passed = []
        for a in range(na):
            for j in range(3):
                kj = 2 * chips[j][0] + chips[j][1]
                ici(a, j, kj).wait_recv()
                if a < nh:
                    cp = fwd(a, j, c)
                    cp.start()
                    passed.append(cp)
        for a in range(nh):
            for j in range(3):
                fwd(a, j, 1 - c).wait_recv()
        for cp in sends + passed:
            cp.wait_send()
        for cp in local:
            cp.wait()

    any_spec = pl.BlockSpec(memory_space=pl.ANY)
    return pl.pallas_call(
        body, in_specs=[any_spec] * na, out_specs=[any_spec] * na,
        out_shape=[_sds((N_CHIPS,) + a.shape, dt) for a, dt in zip(arrays, out_dtypes)],
        scratch_shapes=[pltpu.VMEM(a.shape, a.dtype) for a in arrays] + [pltpu.VMEM(a.shape, BF) for a in halved]
        + [pltpu.SemaphoreType.DMA((3 * na,)), pltpu.SemaphoreType.DMA((3 * na,)),
           pltpu.SemaphoreType.DMA((3 * nh,)), pltpu.SemaphoreType.DMA((3 * nh,)),
           pltpu.SemaphoreType.DMA((na,)), pltpu.SemaphoreType.DMA((na,))],
        compiler_params=pltpu.CompilerParams(vmem_limit_bytes=VMEM_LIMIT_MB << 20),
        name="all_gather_weights")(*arrays)


def _sibling_halves(grads):
    na = len(grads)

    def body(*refs):
        ins, outs = refs[:na], refs[na:2 * na]
        send_sem, recv_sem = refs[2 * na:]
        x, y, c, _ = _place()
        copies = []
        for a in range(na):
            rh = grads[a].shape[1] // 2
            src = ins[a].at[:, pl.ds(pl.multiple_of((1 - c) * rh, 8), rh), :]
            copies.append(pltpu.make_async_remote_copy(
                src_ref=src, dst_ref=outs[a], send_sem=send_sem.at[a], recv_sem=recv_sem.at[a],
                device_id=(x, y, 1 - c), device_id_type=MESH))
        for cp in copies:
            cp.start()
        for cp in copies:
            cp.wait()

    any_spec = pl.BlockSpec(memory_space=pl.ANY)
    return pl.pallas_call(
        body, in_specs=[any_spec] * na, out_specs=[any_spec] * na,
        out_shape=[_sds((N_CHIPS, g.shape[1] // 2, g.shape[2]), F32) for g in grads],
        scratch_shapes=[pltpu.SemaphoreType.DMA((na,)), pltpu.SemaphoreType.DMA((na,))],
        name="sibling_halves")(*grads)


def _pair_sum(g, recv, c, tr, name):
    _, r, cols = g.shape
    rh = r // 2
    nr = rh // tr

    def body(c_ref, g_ref, r_ref, o_ref):
        o_ref[...] = (g_ref[...] + r_ref[...]).astype(BF)

    grid_spec = pltpu.PrefetchScalarGridSpec(
        num_scalar_prefetch=1, grid=(N_CHIPS, nr),
        in_specs=[pl.BlockSpec((1, tr, cols), lambda k, i, c_ref: (k, c_ref[0] * nr + i, 0)),
                  pl.BlockSpec((1, tr, cols), lambda k, i, c_ref: (k, i, 0))],
        out_specs=pl.BlockSpec((1, tr, cols), lambda k, i, c_ref: (k, i, 0)))
    return pl.pallas_call(
        body, grid_spec=grid_spec, out_shape=_sds((N_CHIPS, rh, cols), BF),
        compiler_params=_cp(("parallel", "parallel")), name=name)(c, g, recv)


def _chip_exchange(halves, small):
    na = len(halves)
    srows = small.shape[0]

    def body(*refs):
        ins, small_ref = refs[:na], refs[na]
        outs, small_out = refs[na + 1:2 * na + 1], refs[2 * na + 1]
        send_sem, recv_sem, s_send, s_recv = refs[2 * na + 2:]
        x, y, c, chips = _place()
        me = 4 * x + 2 * y + c
        copies = []
        for a in range(na):
            for j in range(3):
                kj = 2 * chips[j][0] + chips[j][1]
                copies.append(pltpu.make_async_remote_copy(
                    src_ref=ins[a].at[kj], dst_ref=outs[a].at[j],
                    send_sem=send_sem.at[3 * a + j], recv_sem=recv_sem.at[3 * a + j],
                    device_id=(*chips[j], c), device_id_type=MESH))
        for r in range(1, N_DEV):
            peer = (x ^ (r >> 2), y ^ ((r >> 1) & 1), c ^ (r & 1))
            copies.append(pltpu.make_async_remote_copy(
                src_ref=small_ref, dst_ref=small_out.at[me],
                send_sem=s_send.at[r - 1], recv_sem=s_recv.at[r - 1], device_id=peer, device_id_type=MESH))
        for cp in copies:
            cp.start()
        small_out[pl.ds(me, 1)] = small_ref[...][None]
        for cp in copies:
            cp.wait()

    any_spec = pl.BlockSpec(memory_space=pl.ANY)
    vmem = pl.BlockSpec(memory_space=pltpu.VMEM)
    return pl.pallas_call(
        body, in_specs=[any_spec] * na + [vmem], out_specs=[any_spec] * na + [vmem],
        out_shape=[_sds((3,) + h.shape[1:], h.dtype) for h in halves] + [_sds((N_DEV, srows, 128), F32)],
        scratch_shapes=[pltpu.SemaphoreType.DMA((3 * na,)), pltpu.SemaphoreType.DMA((3 * na,)),
                        pltpu.SemaphoreType.DMA((N_DEV - 1,)), pltpu.SemaphoreType.DMA((N_DEV - 1,))],
        name="chip_exchange")(*halves, small)


def _chip_sum(g, sib, recv, place, tr, name):
    _, r, cols = g.shape
    rh = r // 2
    nr = rh // tr

    def body(p_ref, g_ref, s_ref, r0_ref, r1_ref, r2_ref, o_ref):
        own = g_ref[0] + s_ref[0]
        o_ref[...] = ((own + r0_ref[0].astype(F32)) + r1_ref[0].astype(F32)) + r2_ref[0].astype(F32)

    rspec = lambda j: pl.BlockSpec((1, tr, cols), lambda i, p: (j, i, 0))
    grid_spec = pltpu.PrefetchScalarGridSpec(
        num_scalar_prefetch=1, grid=(nr,),
        in_specs=[pl.BlockSpec((1, tr, cols), lambda i, p: (p[0], p[1] * nr + i, 0)),
                  pl.BlockSpec((1, tr, cols), lambda i, p: (p[0], i, 0)), rspec(0), rspec(1), rspec(2)],
        out_specs=pl.BlockSpec((tr, cols), lambda i, p: (p[1] * nr + i, 0)))
    return pl.pallas_call(
        body, grid_spec=grid_spec, out_shape=_sds((r, cols), F32),
        compiler_params=_cp(("parallel",)), name=name)(place, g, sib, recv, recv, recv)


def _sibling_exchange(shards):
    na = len(shards)

    def body(*refs):
        ins, outs = refs[:na], refs[na:2 * na]
        send_sem, recv_sem = refs[2 * na:]
        x, y, c, _ = _place()
        for a in range(na):
            rh = shards[a].shape[0] // 2
            pltpu.make_async_remote_copy(
                src_ref=_rows(ins[a], None, c, rh), dst_ref=_rows(outs[a], None, c, rh),
                send_sem=send_sem.at[a], recv_sem=recv_sem.at[a],
                device_id=(x, y, 1 - c), device_id_type=MESH).start()
        for a in range(na):
            rh = shards[a].shape[0] // 2
            pltpu.make_async_remote_copy(
                src_ref=_rows(ins[a], None, c, rh), dst_ref=_rows(outs[a], None, 1 - c, rh),
                send_sem=send_sem.at[a], recv_sem=recv_sem.at[a],
                device_id=(x, y, 1 - c), device_id_type=MESH).wait()

    any_spec = pl.BlockSpec(memory_space=pl.ANY)
    return pl.pallas_call(
        body, in_specs=[any_spec] * na, out_specs=[any_spec] * na,
        out_shape=[_sds(h.shape, F32) for h in shards],
        input_output_aliases={a: a for a in range(na)},
        scratch_shapes=[pltpu.SemaphoreType.DMA((na,)), pltpu.SemaphoreType.DMA((na,))],
        name="sibling_exchange")(*shards)


def _device_sum(stack, deps=()):
    _, rows, _ = stack.shape

    def body(s_ref, o_ref):
        acc = s_ref[0]
        for d in range(1, N_DEV):
            acc = acc + s_ref[d]
        o_ref[...] = acc

    vmem = pl.BlockSpec(memory_space=pltpu.VMEM)
    return pl.pallas_call(_after(body, 1, deps), in_specs=[vmem] + _any_specs(deps), out_specs=vmem,
                          out_shape=_sds((rows, 128), F32), name="device_sum")(stack, *deps)


_HBM = pl.BlockSpec(memory_space=pltpu.HBM)
_SEM = pl.BlockSpec(memory_space=pltpu.SEMAPHORE)
_EFFECT = pltpu.SideEffectType.DATAFLOW_SIDE_EFFECTING


def _remote(src, dst, ssem, rsem, k, device):
    return pltpu.make_async_remote_copy(src_ref=src, dst_ref=dst, send_sem=ssem.at[k], recv_sem=rsem.at[k],
                                        device_id=device, device_id_type=MESH)


def _split_start(name, bufs, plan, n):
    nb = len(bufs)

    def body(*refs):
        sends, _ = plan(refs[:nb], refs[nb], refs[nb + 1])
        for cp in sends:
            cp.start()
        refs[-1][...] = jnp.zeros_like(refs[-1])

    res = pl.pallas_call(
        body, name=name,
        out_shape=(pltpu.SemaphoreType.DMA((n,)), pltpu.SemaphoreType.DMA((n,)))
        + tuple(pltpu.HBM(b.shape, b.dtype) for b in bufs) + (_sds((8, 128), F32),),
        in_specs=[_HBM] * nb,
        out_specs=(_SEM, _SEM) + (_HBM,) * nb + (pl.BlockSpec(memory_space=pltpu.VMEM),),
        input_output_aliases={i: i + 2 for i in range(nb)},
        compiler_params=pltpu.CompilerParams(has_side_effects=_EFFECT),
    )(*[pltpu.with_memory_space_constraint(b, pltpu.HBM) for b in bufs])
    return res[0], res[1], list(res[2:2 + nb]), res[2 + nb]


def _split_wait(name, send_sem, recv_sem, bufs, plan, after):
    nb = len(bufs)

    def body(*refs):
        sends, arrivals = plan(refs[:nb], refs[nb], refs[nb + 1])
        for cp in sends:
            cp.wait_send()
        for cp in arrivals:
            cp.wait_recv()

    res = pl.pallas_call(
        body, name=name, out_shape=tuple(pltpu.HBM(b.shape, b.dtype) for b in bufs),
        in_specs=[_HBM] * nb + [_SEM, _SEM, pl.BlockSpec(memory_space=pl.ANY)],
        out_specs=(_HBM,) * nb, input_output_aliases={i: i for i in range(nb)},
        compiler_params=pltpu.CompilerParams(has_side_effects=_EFFECT),
    )(*bufs, send_sem, recv_sem, after)
    return list(res)


def _plan_sibling_halves(shapes):
    na = len(shapes)

    def plan(refs, ssem, rsem):
        x, y, c, _ = _place()
        cps = []
        for a in range(na):
            rh = shapes[a][1] // 2
            src = refs[a].at[:, pl.ds(pl.multiple_of((1 - c) * rh, 8), rh), :]
            cps.append(_remote(src, refs[na + a], ssem, rsem, a, (x, y, 1 - c)))
        return cps, cps

    return plan


def _to_all(ref, ssem, rsem, base):
    x, y, c, _ = _place()
    mine = ref.at[4 * x + 2 * y + c]
    return [_remote(mine, mine, ssem, rsem, base + r - 1, (x ^ (r >> 2), y ^ ((r >> 1) & 1), c ^ (r & 1)))
            for r in range(1, N_DEV)]


def _plan_chip_exchange(na, with_small):
    def plan(refs, ssem, rsem):
        _, _, c, chips = _place()
        cps = []
        for a in range(na):
            for j in range(3):
                kj = 2 * chips[j][0] + chips[j][1]
                cps.append(_remote(refs[a].at[kj], refs[na + a].at[j], ssem, rsem, 3 * a + j, (*chips[j], c)))
        if with_small:
            cps += _to_all(refs[2 * na], ssem, rsem, 3 * na)
        return cps, cps

    return plan


def _plan_sibling_swap(shapes, with_small):
    def plan(refs, ssem, rsem):
        x, y, c, _ = _place()
        sends, arrivals = [], []
        for a, shp in enumerate(shapes):
            rh = shp[0] // 2
            mine, other = _rows(refs[a], None, c, rh), _rows(refs[a], None, 1 - c, rh)
            sends.append(_remote(mine, mine, ssem, rsem, a, (x, y, 1 - c)))
            arrivals.append(_remote(mine, other, ssem, rsem, a, (x, y, 1 - c)))
        if with_small:
            cps = _to_all(refs[len(shapes)], ssem, rsem, len(shapes))
            sends += cps
            arrivals += cps
        return sends, arrivals

    return plan


def _plan_gather_chips(shapes):
    def plan(refs, ssem, rsem):
        x, y, c, chips = _place()
        me = 2 * x + y
        sends, arrivals = [], []
        for a, shp in enumerate(shapes):
            rh = shp[1] // 2
            mine = _rows(refs[a], me, c, rh)
            for j in range(3):
                land = _rows(refs[a], 2 * chips[j][0] + chips[j][1], c, rh)
                sends.append(_remote(mine, mine, ssem, rsem, 3 * a + j, (*chips[j], c)))
                arrivals.append(_remote(land, land, ssem, rsem, 3 * a + j, (*chips[j], c)))
        return sends, arrivals

    return plan


def _plan_gather_sibling(shapes):
    def plan(refs, ssem, rsem):
        x, y, c, chips = _place()
        sends, arrivals = [], []
        for a, shp in enumerate(shapes):
            rh = shp[1] // 2
            for j in range(3):
                kj = 2 * chips[j][0] + chips[j][1]
                got, land = _rows(refs[a], kj, c, rh), _rows(refs[a], kj, 1 - c, rh)
                sends.append(_remote(got, got, ssem, rsem, 3 * a + j, (x, y, 1 - c)))
                arrivals.append(_remote(got, land, ssem, rsem, 3 * a + j, (x, y, 1 - c)))
        return sends, arrivals

    return plan


def _into_slice(w, k, n, tr, dtype, name):
    r, cols = w.shape

    def body(k_ref, w_ref, o_ref):
        o_ref[0] = w_ref[...].astype(dtype)

    grid_spec = pltpu.PrefetchScalarGridSpec(
        num_scalar_prefetch=1, grid=(r // tr,),
        in_specs=[pl.BlockSpec((tr, cols), lambda i, k: (i, 0))],
        out_specs=pl.BlockSpec((1, tr, cols), lambda i, k: (k[0], i, 0)))
    return pl.pallas_call(body, grid_spec=grid_spec, out_shape=_sds((n, r, cols), dtype),
                          compiler_params=_cp(("parallel",)), name=name)(k, w)


class _LateWeights:
    def __init__(self, shards, chip, names, tiles):
        bufs = [_into_slice(w, chip, N_CHIPS, t, BF, "own_" + nm) for w, nm, t in zip(shards, names, tiles)]
        self.n = 3 * len(bufs)
        self.chips, self.sibling = _plan_gather_chips([b.shape for b in bufs]), _plan_gather_sibling([b.shape for b in bufs])
        self.ssem, self.rsem, self.bufs, token = _split_start("gather_chips_start", bufs, self.chips, self.n)
        self.first = (token,)

    def middle(self, after):
        bufs = _split_wait("gather_chips_wait", self.ssem, self.rsem, self.bufs, self.chips, after)
        self.ssem, self.rsem, self.bufs, token = _split_start("gather_sibling_start", bufs, self.sibling, self.n)
        return (token,)

    def last(self, after):
        return _split_wait("gather_sibling_wait", self.ssem, self.rsem, self.bufs, self.sibling, after)


class _GradReduce:
    def __init__(self, tag, place, names, tiles):
        self.tag, self.place, self.names, self.tiles = tag, place, names, tiles
        self.small_all = None

    def first(self, grads):
        self.na = len(grads)
        self.p1 = _plan_sibling_halves([g.shape for g in grads])
        lands = [lax.empty((N_CHIPS, g.shape[1] // 2, g.shape[2]), F32) for g in grads]
        self.ssem, self.rsem, self.bufs, token = _split_start(
            self.tag + "_halves_start", list(grads) + lands, self.p1, self.na)
        return (token,)

    def second(self, after, small=None):
        bufs = _split_wait(self.tag + "_halves_wait", self.ssem, self.rsem, self.bufs, self.p1, after)
        self.grads, self.sib = bufs[:self.na], bufs[self.na:]
        halves = [_pair_sum(g, r, self.place[1:2], t, "pair_sum_" + nm)
                  for g, r, t, nm in zip(self.grads, self.sib, self.tiles, self.names)]
        lands = [lax.empty((3,) + h.shape[1:], h.dtype) for h in halves]
        extra = [] if small is None else [small]
        self.p2 = _plan_chip_exchange(self.na, small is not None)
        self.ssem, self.rsem, self.bufs, token = _split_start(
            self.tag + "_chips_start", halves + lands + extra, self.p2, 3 * self.na + (N_DEV - 1) * len(extra))
        return (token,)

    def third(self, after, small=None):
        bufs = _split_wait(self.tag + "_chips_wait", self.ssem, self.rsem, self.bufs, self.p2, after)
        if len(bufs) > 2 * self.na:
            self.small_chips = bufs[2 * self.na]
        mine = [_chip_sum(g, sb, r, self.place, t, "chip_sum_" + nm)
                for g, sb, r, t, nm in zip(self.grads, self.sib, bufs[self.na:2 * self.na], self.tiles, self.names)]
        extra = [] if small is None else [small]
        self.p3 = _plan_sibling_swap([m.shape for m in mine], small is not None)
        self.ssem, self.rsem, self.bufs, token = _split_start(
            self.tag + "_swap_start", mine + extra, self.p3, self.na + (N_DEV - 1) * len(extra))
        return (token,)

    def last(self, after):
        bufs = _split_wait(self.tag + "_swap_wait", self.ssem, self.rsem, self.bufs, self.p3, after)
        if len(bufs) > self.na:
            self.small_swap = bufs[self.na]
        return bufs[:self.na]


def _pack(parts, rows):
    flat = jnp.concatenate([p.reshape(-1) for p in parts])
    return jnp.pad(flat, (0, rows * 128 - flat.shape[0])).reshape(rows, 128)


def _unpack(buf, shapes):
    flat = buf.reshape(-1)
    out, off = [], 0
    for shp in shapes:
        n = 1
        for d in shp:
            n *= d
        out.append(flat[off:off + n].reshape(shp))
        off += n
    return out


def _rows_for(n):
    return -(-n // (8 * 128)) * 8


class _WeightsAtHand:
    def __init__(self, wup, wout, wdown):
        self.first, self.weights = (), [wup, wout, wdown]

    def middle(self, after):
        return ()

    def last(self, after):
        return self.weights


class _GradsKept:
    def first(self, grads):
        self.grads = list(grads)
        return ()

    def second(self, after, small=None):
        return ()

    def third(self, after, small=None):
        return ()

    def last(self, after):
        return self.grads


def _forward_backward(xs, pos, tgt, win, wpool, cw, attn_norm, b_gate, pool_scale, q_norm, k_norm, sinks,
                      ffn_norm, conv_b, late, early, rest, dev):
    s = xs.shape[0]
    tm = min(512, s)
    tk = min(1024, s)
    inv_freq = ROPE_THETA ** (-jnp.arange(0, ROPE_DIM, 2, dtype=F32) / ROPE_DIM)
    lane = jnp.arange(2 * HEAD) % HEAD
    invf = jnp.where(lane < ROPE_DIM, inv_freq[lane % (ROPE_DIM // 2)], 0.0).reshape(1, 2 * HEAD)
    wq = jnp.tile(q_norm, (1, N_Q))
    wk = jnp.tile(k_norm, (1, N_KV))
    head_of = jnp.arange(Q_W) // HEAD
    bd = (head_of[:, None] == head_of[None, :]).astype(BF)
    sink = sinks[0]

    h1, u, qkv, gates = _attn_in_proj(xs, attn_norm, win, b_gate, tm, deps=late.first)
    qh, kh, vh = _qk_prep(qkv, pos, wq, wk, invf, bd, tm)
    battn = _attn_fwd(qh, kh, vh, sink, deps=late.middle(qh))
    apool, pooled = _pool_fwd(u, wpool, pool_scale, min(512, s))
    wup, wout, wdown = late.last(apool)
    wout = wout.reshape(D, D)
    wdown = wdown.reshape(D_FF, D)
    mix, y1, h2 = _mix_out_proj(apool, battn, gates, xs, wout, ffn_norm, tm)
    pre_g, pre_v, up_g, up_v, act = _ffn_up(h2, wup, cw, conv_b, tm)
    dy2, dy2b, loss_acc = _ffn_down_loss(act, wdown, y1, tgt, tm)

    d_wdown = _grad_matmul(act, dy2b, 512, tk, "grad_w_down")
    dp_g, dp_v, dcw_g, dcw_v, dcb_g, dcb_v = _ffn_act_bwd(dy2b, wdown, up_g, up_v, pre_g, pre_v, cw, tm)
    d_wup = _grad_matmul(h2, dp_g, UP_SHARD, tk, "grad_w_up_gate", lead=N_CHIPS)
    d_wup = _grad_matmul(h2, dp_v, UP_SHARD, tk, "grad_w_up_value", lead=N_CHIPS, prev=d_wup, lead_off=2)
    token = early.first([d_wdown.reshape(N_CHIPS, D_FF // N_CHIPS, D), d_wup])
    dy1, d_ffn_norm = _ffn_up_bwd(dp_g, dp_v, wup, y1, dy2, ffn_norm, tm, deps=token)
    token = early.second(dy1)
    da, db, dzg, d_bgate, d_wout = _out_proj_bwd(dy1, wout, apool, battn, gates, mix, tm, deps=token)
    du, d_wpool, d_pscale, d_win_pool = _pool_bwd(da, pooled, wpool, pool_scale, h1, min(512, s))
    dqh, dkh, dvh, dsink = _attn_bwd(qh, kh, vh, sink, db)
    token = early.third(dqh)
    dzq, d_qn, d_kn = _qk_prep_bwd(dqh, dkh, dvh, qkv, pos, wq, wk, invf, bd, tm, deps=token)
    d_win_t = jnp.concatenate([
        d_win_pool,
        _grad_matmul(dzq, h1, D, tk, "grad_w_in_qkv"),
        _grad_matmul(dzg, h1, D, tk, "grad_w_in_gates")], axis=0)
    token = rest.first([
        d_win_t.reshape(N_CHIPS, IN_W // N_CHIPS, D),
        d_wout.reshape(N_CHIPS, D // N_CHIPS, D),
        d_wpool.reshape(4, N_CHIPS, 64, POOL_GROUP).transpose(1, 0, 2, 3).reshape(N_CHIPS, 4 * 64, POOL_GROUP)])
    small = _pack_small(loss_acc, d_bgate, d_pscale, d_qn, d_kn, dsink, d_ffn_norm, dcb_g, dcb_v, dcw_g, dcw_v, dev)
    token = rest.second(token[0] if token else None, small=small)
    grad_x, d_attn_norm = _in_proj_bwd(du, dzq, dzg, win, xs, attn_norm, dy1, tm, deps=token)
    return grad_x, d_attn_norm, small


def kernel(x, positions, attn_norm, w_in, b_gate, w_pool, pool_scale, q_norm, k_norm, sinks, w_out, ffn_norm, w_up, conv_w, conv_b, w_down, loss_target, m_attn_norm, m_w_in, m_b_gate, m_w_pool, m_pool_scale, m_q_norm, m_k_norm, m_sinks, m_w_out, m_ffn_norm, m_w_up, m_conv_w, m_conv_b, m_w_down, v_attn_norm, v_w_in, v_b_gate, v_w_pool, v_pool_scale, v_q_norm, v_k_norm, v_sinks, v_w_out, v_ffn_norm, v_w_up, v_conv_w, v_conv_b, v_w_down):
    s = x.shape[1]
    xs = x[0]
    tgt = loss_target[0]
    pos = positions[0].reshape(s, 1)
    cx, cy, cc = lax.axis_index("x"), lax.axis_index("y"), lax.axis_index("c")
    chip = 2 * cx + cy

    chip_arr = chip.reshape(1).astype(jnp.int32)
    dev_arr = (2 * chip + cc).reshape(1).astype(jnp.int32)
    place = jnp.stack([chip, cc]).astype(jnp.int32)

    g_in, g_pool, g_cw = _all_gather_weights(
        [jnp.swapaxes(w_in[0], 0, 1), w_pool[0].reshape(4 * 64, POOL_GROUP)], [conv_w[0]])
    win = g_in.reshape(IN_W, D)
    wpool = g_pool.reshape(N_CHIPS, 4, 64, POOL_GROUP).transpose(1, 0, 2, 3).reshape(4, POOL_GROUP, POOL_GROUP)
    late = _LateWeights([w_up[0], w_out[0], w_down[0]], chip_arr, ["w_up", "w_out", "w_down"], [256, 256, 352])
    early = _GradReduce("early", place, ["w_down", "w_up"], [176, 256])
    rest = _GradReduce("rest", place, ["w_in", "w_out", "w_pool"], [272, 128, 128])

    grad_x, d_attn_norm, _ = _forward_backward(
        xs, pos, tgt, win, wpool, g_cw, attn_norm, b_gate, pool_scale, q_norm, k_norm, sinks, ffn_norm, conv_b,
        late, early, rest, dev_arr)

    def two_d(a):
        return a.reshape(-1, a.shape[-1])

    def update(nm, w, g, m, v, tr, deps=()):
        res = _adamw(two_d(w), g, two_d(m), two_d(v), tr, "adamw_" + nm, deps=deps)
        return [r.reshape(w.shape) for r in res]

    attn_stack = _into_slice(d_attn_norm, dev_arr, N_DEV, 1, F32, "own_attn_norm")
    g_wdown, g_wup = early.last(grad_x)
    big_out = {"w_up": update("w_up", w_up, g_wup, m_w_up, v_w_up, 256)}
    big_out["w_down"] = update("w_down", w_down, g_wdown, m_w_down, v_w_down, 176, deps=(big_out["w_up"][1],))
    token = rest.third(big_out["w_down"][1], small=attn_stack)
    g_win_t, g_wout, g_wpool = rest.last(token[0])
    small_out, g_convw_all, loss = _small_update(rest.small_chips, rest.small_swap, {
        "attn_norm": (attn_norm, m_attn_norm, v_attn_norm), "b_gate": (b_gate, m_b_gate, v_b_gate),
        "pool_scale": (pool_scale, m_pool_scale, v_pool_scale), "q_norm": (q_norm, m_q_norm, v_q_norm),
        "k_norm": (k_norm, m_k_norm, v_k_norm), "sinks": (sinks, m_sinks, v_sinks),
        "ffn_norm": (ffn_norm, m_ffn_norm, v_ffn_norm), "conv_b": (conv_b, m_conv_b, v_conv_b)})
    g_convw = lax.dynamic_slice_in_dim(g_convw_all, chip * UP_SHARD, UP_SHARD, axis=1)
    small_out["conv_w"] = update("conv_w", conv_w, g_convw, m_conv_w, v_conv_w, 3)
    flip = lambda a: jnp.swapaxes(a[0], 0, 1)
    res = _adamw(flip(w_in), g_win_t, flip(m_w_in), flip(v_w_in), 272, "adamw_w_in")
    big_out["w_in"] = [jnp.swapaxes(r, 0, 1)[None] for r in res]
    big_out["w_out"] = update("w_out", w_out, g_wout, m_w_out, v_w_out, 128)
    big_out["w_pool"] = update("w_pool", w_pool, g_wpool, m_w_pool, v_w_pool, 128)

    order = ["attn_norm", "w_in", "b_gate", "w_pool", "pool_scale", "q_norm", "k_norm", "sinks", "w_out",
             "ffn_norm", "w_up", "conv_w", "conv_b", "w_down"]
    allout = {**big_out, **small_out}
    outs = [loss.reshape(()), grad_x[None]]
    for k in range(4):
        outs += [allout[nm][k] for nm in order]
    return tuple(outs)
```

```python
import functools

import jax
import jax.numpy as jnp
from jax import lax
from jax.experimental import pallas as pl
from jax.experimental.pallas import tpu as pltpu

D = 1024
D_FF = 2816
HEAD = 64
N_Q = 16
N_KV = 2
GQA = 8
BLK = 128
ROPE_DIM = 16
ROPE_THETA = 500000.0
POOL_GROUP = 256
Q_W = 1024
KV_W = 128
QKV_W = Q_W + 2 * KV_W
IN_W = 4352
UP_SHARD = 1408
EPS = 1e-6
N_CHIPS = 4
N_DEV = 8

LR = 0.001
B1 = 0.9
B2 = 0.999
ADAM_EPS = 1e-08
WD = 0.01
STEP = 10

BF = jnp.bfloat16
F32 = jnp.float32
MESH = pl.DeviceIdType.MESH
VMEM_LIMIT_MB = 56


def _cp(sem, vmem_mb=VMEM_LIMIT_MB):
    return pltpu.CompilerParams(dimension_semantics=sem, vmem_limit_bytes=vmem_mb << 20)


def _full(shape):
    nd = len(shape)
    return pl.BlockSpec(shape, lambda *_: (0,) * nd)


def _sds(shape, dtype):
    return jax.ShapeDtypeStruct(shape, dtype)


def _after(body, n_in, deps):
    nd = len(deps)
    if nd == 0:
        return body

    def ordered(*refs):
        return body(*refs[:n_in], *refs[n_in + nd:])

    return ordered


def _any_specs(deps):
    return [pl.BlockSpec(memory_space=pl.ANY)] * len(deps)


def _nt(a, b):
    return lax.dot_general(a, b, (((1,), (1,)), ((), ())), preferred_element_type=F32)


def _tn(a, b):
    return lax.dot_general(a, b, (((0,), (0,)), ((), ())), preferred_element_type=F32)


def _mm(a, b):
    return jnp.dot(a, b, preferred_element_type=F32)


def _head_sum(v, bd):
    return _mm(v.astype(BF), bd)


def _rope_tables(pos_ref, invf_ref):
    ang = pos_ref[...].astype(F32) * invf_ref[...]
    cos = jnp.cos(ang)
    sin = jnp.sin(ang)
    lane = lax.broadcasted_iota(jnp.int32, (1, 2 * HEAD), 1) % HEAD
    sa = jnp.where(lane < ROPE_DIM // 2, -sin, 0.0)
    sb = jnp.where(lane < ROPE_DIM // 2, 0.0, jnp.where(lane < ROPE_DIM, sin, 0.0))
    return cos, sa, sb


def _tile_lanes(t, reps):
    return t if reps == 1 else jnp.tile(t, (1, reps))


def _rope(v, cos, sa, sb):
    w = v.shape[1]
    reps = w // (2 * HEAD)
    half = ROPE_DIM // 2
    return (v * _tile_lanes(cos, reps) + pltpu.roll(v, w - half, 1) * _tile_lanes(sa, reps)
            + pltpu.roll(v, half, 1) * _tile_lanes(sb, reps))


def _rope_t(dy, cos, sa, sb):
    w = dy.shape[1]
    reps = w // (2 * HEAD)
    half = ROPE_DIM // 2
    return (dy * _tile_lanes(cos, reps) + pltpu.roll(dy * _tile_lanes(sa, reps), half, 1)
            + pltpu.roll(dy * _tile_lanes(sb, reps), w - half, 1))


def _attn_in_proj(x, g1, w_in, b_gate, tm, deps=()):
    s = x.shape[0]

    def body(x_ref, g_ref, w_ref, b_ref, h_ref, u_ref, qkv_ref, gate_ref):
        xv = x_ref[...]
        r = lax.rsqrt(jnp.mean(xv * xv, axis=-1, keepdims=True) + EPS)
        h = (xv * r * g_ref[...]).astype(BF)
        h_ref[...] = h
        u_ref[...] = _nt(h, w_ref[0:D, :])
        qkv_ref[...] = _nt(h, w_ref[D:D + QKV_W, :])
        gate_ref[...] = jax.nn.sigmoid(_nt(h, w_ref[D + QKV_W:IN_W, :]) + b_ref[...]).astype(BF)

    row = lambda w: pl.BlockSpec((tm, w), lambda i: (i, 0))
    return pl.pallas_call(
        _after(body, 4, deps), grid=(s // tm,),
        in_specs=[row(D), _full((1, D)), _full((IN_W, D)), _full((1, 2 * D))] + _any_specs(deps),
        out_specs=[row(D), row(D), row(QKV_W), row(2 * D)],
        out_shape=[_sds((s, D), BF), _sds((s, D), F32), _sds((s, QKV_W), F32), _sds((s, 2 * D), BF)],
        compiler_params=_cp(("parallel",)), name="attn_in_proj")(x, g1, w_in, b_gate, *deps)


def _qk_prep(qkv, pos, wq, wk, invf, bd, tm):
    s = qkv.shape[0]

    def body(qkv_ref, pos_ref, wq_ref, wk_ref, invf_ref, bd_ref, qh_ref, kh_ref, vh_ref):
        cos, sa, sb = _rope_tables(pos_ref, invf_ref)
        q = qkv_ref[:, 0:Q_W]
        k = qkv_ref[:, Q_W:Q_W + KV_W]
        v = qkv_ref[:, Q_W + KV_W:QKV_W]
        rq = lax.rsqrt(_head_sum(q * q, bd_ref[...]) * (1.0 / HEAD) + EPS)
        qr = _rope(q * rq * wq_ref[...], cos, sa, sb) * (HEAD ** -0.5)
        rk = lax.rsqrt(_head_sum(k * k, bd_ref[0:KV_W, 0:KV_W]) * (1.0 / HEAD) + EPS)
        kr = _rope(k * rk * wk_ref[...], cos, sa, sb)
        for h in range(N_Q):
            qh_ref[h] = qr[:, HEAD * h:HEAD * (h + 1)].astype(BF)
        for h in range(N_KV):
            kh_ref[h] = kr[:, HEAD * h:HEAD * (h + 1)].astype(BF)
            vh_ref[h] = v[:, HEAD * h:HEAD * (h + 1)].astype(BF)

    heads = lambda n: pl.BlockSpec((n, tm, HEAD), lambda i: (0, i, 0))
    return pl.pallas_call(
        body, grid=(s // tm,),
        in_specs=[pl.BlockSpec((tm, QKV_W), lambda i: (i, 0)), pl.BlockSpec((tm, 1), lambda i: (i, 0)),
                  _full((1, Q_W)), _full((1, KV_W)), _full((1, 2 * HEAD)), _full((Q_W, Q_W))],
        out_specs=[heads(N_Q), heads(N_KV), heads(N_KV)],
        out_shape=[_sds((N_Q, s, HEAD), BF), _sds((N_KV, s, HEAD), BF), _sds((N_KV, s, HEAD), BF)],
        compiler_params=_cp(("parallel",)), name="qk_prep")(qkv, pos, wq, wk, invf, bd)


def _sink_column(sink_ref, kh):
    row_g = lax.broadcasted_iota(jnp.int32, (GQA * BLK, 1), 0) // BLK
    col = jnp.zeros((GQA * BLK, 1), F32)
    for g in range(GQA):
        col = jnp.where(row_g == g, sink_ref[kh * GQA + g], col)
    return col


def _fold_band(band, lower, first=None):
    prev, cur = band[:, 0:BLK], band[:, BLK:2 * BLK]
    if first is not None:
        prev = jnp.where(first, -jnp.inf, prev)
    return jnp.where(lower, cur, prev)


def _unfold_band(x, lower):
    return jnp.concatenate([jnp.where(lower, 0.0, x), jnp.where(lower, x, 0.0)], axis=1)


def _attn_probs(q, k, n, sink_col):
    sc = _nt(q, k)
    qi = lax.broadcasted_iota(jnp.int32, (sc.shape[0], BLK), 0) % BLK
    ki = lax.broadcasted_iota(jnp.int32, (sc.shape[0], BLK), 1)
    lower = ki <= qi
    sc = _fold_band(sc, lower, first=n == 0)
    m = jnp.maximum(jnp.max(sc, axis=-1, keepdims=True), sink_col)
    p = jnp.exp(sc - m)
    es = jnp.exp(sink_col - m)
    inv = 1.0 / (jnp.sum(p, axis=-1, keepdims=True) + es)
    return p * inv, es * inv, lower


def _attn_fwd(qh, kh, vh, sinks, deps=()):
    s = qh.shape[1]
    nb = s // BLK

    def body(sink_ref, q_ref, kp_ref, kc_ref, vp_ref, vc_ref, o_ref):
        n = pl.program_id(0)
        for khd in range(N_KV):
            q = q_ref[khd * GQA:(khd + 1) * GQA].reshape(GQA * BLK, HEAD)
            k = jnp.concatenate([kp_ref[khd], kc_ref[khd]], axis=0)
            v = jnp.concatenate([vp_ref[khd], vc_ref[khd]], axis=0)
            probs, _, lower = _attn_probs(q, k, n, _sink_column(sink_ref, khd))
            o = _mm(_unfold_band(probs, lower).astype(BF), v)
            for j in range(GQA // 2):
                c0 = khd * GQA * HEAD + 2 * HEAD * j
                o_ref[:, c0:c0 + 2 * HEAD] = jnp.concatenate(
                    [o[2 * j * BLK:(2 * j + 1) * BLK], o[(2 * j + 1) * BLK:(2 * j + 2) * BLK]], axis=1).astype(BF)

    prev = pl.BlockSpec((N_KV, BLK, HEAD), lambda n: (0, jnp.maximum(n - 1, 0), 0))
    cur = pl.BlockSpec((N_KV, BLK, HEAD), lambda n: (0, n, 0))
    return pl.pallas_call(
        _after(body, 6, deps), grid=(nb,),
        in_specs=[pl.BlockSpec(memory_space=pltpu.SMEM),
                  pl.BlockSpec((N_Q, BLK, HEAD), lambda n: (0, n, 0)), prev, cur, prev, cur] + _any_specs(deps),
        out_specs=pl.BlockSpec((BLK, Q_W), lambda n: (n, 0)),
        out_shape=_sds((s, Q_W), BF),
        compiler_params=_cp(("parallel",)), name="attn_fwd")(sinks, qh, kh, kh, vh, vh, *deps)


def _pool_fwd(u, w_pool, pool_scale, ts):
    s = u.shape[0]
    halo = 16

    def body(u_ref, wp_ref, ps_ref, a_ref, pooled_ref, prev):
        g = pl.program_id(0)
        i = pl.program_id(1)

        @pl.when(i == 0)
        def _():
            prev[...] = jnp.zeros_like(prev)

        cur = u_ref[...]
        ext = jnp.concatenate([prev[...], cur], axis=0)
        t = (i * ts + lax.broadcasted_iota(jnp.int32, (ts, 1), 0)).astype(F32)
        for gi in range(4):
            @pl.when(g == gi)
            def _(gi=gi):
                w = 2 << gi
                acc, span = ext, 1
                while span < w:
                    acc = acc + pltpu.roll(acc, span, 0)
                    span *= 2
                inv = 1.0 / jnp.minimum(t + 1.0, float(w))
                pooled = (acc[halo:halo + ts] * inv - cur).astype(BF)
                pooled_ref[...] = pooled
                a_ref[...] = (_mm(pooled, wp_ref[0]) * ps_ref[...]).astype(BF)

        prev[...] = cur[ts - halo:ts]

    col = pl.BlockSpec((ts, POOL_GROUP), lambda g, i: (i, g))
    return pl.pallas_call(
        body, grid=(4, s // ts),
        in_specs=[col, pl.BlockSpec((1, POOL_GROUP, POOL_GROUP), lambda g, i: (g, 0, 0)),
                  pl.BlockSpec((1, POOL_GROUP), lambda g, i: (0, g))],
        out_specs=[col, col],
        out_shape=[_sds((s, D), BF), _sds((s, D), BF)],
        scratch_shapes=[pltpu.VMEM((halo, POOL_GROUP), F32)],
        compiler_params=_cp(("parallel", "arbitrary")), name="pool_fwd")(u, w_pool, pool_scale)


def _mix_out_proj(a, b, gates, x, w_out, g2, tm):
    s = x.shape[0]

    def body(a_ref, b_ref, gate_ref, x_ref, w_ref, g_ref, mix_ref, y_ref, h_ref):
        mix = (gate_ref[:, 0:D].astype(F32) * a_ref[...].astype(F32)
               + gate_ref[:, D:2 * D].astype(F32) * b_ref[...].astype(F32)).astype(BF)
        mix_ref[...] = mix
        y = x_ref[...] + _mm(mix, w_ref[...])
        y_ref[...] = y
        r = lax.rsqrt(jnp.mean(y * y, axis=-1, keepdims=True) + EPS)
        h_ref[...] = (y * r * g_ref[...]).astype(BF)

    row = lambda w: pl.BlockSpec((tm, w), lambda i: (i, 0))
    return pl.pallas_call(
        body, grid=(s // tm,),
        in_specs=[row(D), row(D), row(2 * D), row(D), _full((D, D)), _full((1, D))],
        out_specs=[row(D), row(D), row(D)],
        out_shape=[_sds((s, D), BF), _sds((s, D), F32), _sds((s, D), BF)],
        compiler_params=_cp(("parallel",)), name="mix_out_proj")(a, b, gates, x, w_out, g2)


def _ffn_up(h2, w_up, conv_w, conv_b, tm):
    s = h2.shape[0]

    def body(h_ref, wg_ref, wv_ref, cwg_ref, cwv_ref, cbg_ref, cbv_ref,
             preg_ref, prev_ref, upg_ref, upv_ref, act_ref, halog, halov):
        i = pl.program_id(1)

        @pl.when(i == 0)
        def _():
            halog[...] = jnp.zeros_like(halog)
            halov[...] = jnp.zeros_like(halov)

        h = h_ref[...]

        def conv_half(w_ref, cw_ref, cb_ref, halo, pre_ref, up_ref):
            pre = _mm(h, w_ref[0])
            pre_ref[...] = pre.astype(BF)
            ext = jnp.concatenate([halo[...], pre], axis=0)
            cw = cw_ref[0]
            up = cb_ref[...] + cw[0:1] * pltpu.roll(ext, 2, 0)[8:8 + tm]
            up = up + cw[1:2] * pltpu.roll(ext, 1, 0)[8:8 + tm]
            up = up + cw[2:3] * pre
            halo[...] = pre[tm - 8:tm]
            up_ref[...] = up.astype(BF)
            return up

        gate = conv_half(wg_ref, cwg_ref, cbg_ref, halog, preg_ref, upg_ref)
        val = conv_half(wv_ref, cwv_ref, cbv_ref, halov, prev_ref, upv_ref)
        act_ref[...] = (gate * jax.nn.sigmoid(gate) * val).astype(BF)

    tile = pl.BlockSpec((tm, UP_SHARD), lambda j, i: (i, j))
    wspec = lambda off: pl.BlockSpec((1, D, UP_SHARD), lambda j, i: (j + off, 0, 0))
    cwspec = lambda off: pl.BlockSpec((1, 3, UP_SHARD), lambda j, i: (j + off, 0, 0))
    cbspec = lambda off: pl.BlockSpec((1, UP_SHARD), lambda j, i: (0, j + off))
    half = _sds((s, D_FF), BF)
    return pl.pallas_call(
        body, grid=(2, s // tm),
        in_specs=[pl.BlockSpec((tm, D), lambda j, i: (i, 0)), wspec(0), wspec(2), cwspec(0), cwspec(2),
                  cbspec(0), cbspec(2)],
        out_specs=[tile] * 5, out_shape=[half] * 5,
        scratch_shapes=[pltpu.VMEM((8, UP_SHARD), F32), pltpu.VMEM((8, UP_SHARD), F32)],
        compiler_params=_cp(("parallel", "arbitrary")), name="ffn_up")(
            h2, w_up, w_up, conv_w, conv_w, conv_b, conv_b)


def _ffn_down_loss(act, w_down, y1, tgt, tm):
    s = y1.shape[0]

    def body(act_ref, w_ref, y_ref, t_ref, dy_ref, dyb_ref, loss_ref):
        @pl.when(pl.program_id(0) == 0)
        def _():
            loss_ref[...] = jnp.zeros_like(loss_ref)

        e = y_ref[...] + _mm(act_ref[...], w_ref[...]) - t_ref[...]
        dy = e * (1.0 / D)
        dy_ref[...] = dy
        dyb_ref[...] = dy.astype(BF)
        e2 = (e * e).reshape(tm // 8, 8, D).sum(axis=0)
        part = e2[:, 0:128]
        for j in range(1, D // 128):
            part = part + e2[:, 128 * j:128 * (j + 1)]
        loss_ref[...] += part

    row = lambda w: pl.BlockSpec((tm, w), lambda i: (i, 0))
    return pl.pallas_call(
        body, grid=(s // tm,),
        in_specs=[row(D_FF), _full((D_FF, D)), row(D), row(D)],
        out_specs=[row(D), row(D), _full((8, 128))],
        out_shape=[_sds((s, D), F32), _sds((s, D), BF), _sds((8, 128), F32)],
        compiler_params=_cp(("arbitrary",)), name="ffn_down_loss")(act, w_down, y1, tgt)


def _grad_matmul(a, b, tn, tk, name, lead=None, prev=None, lead_off=0):
    s, m = a.shape
    n = b.shape[1]
    nj = n // tn

    def body(*refs):
        a_ref, b_ref = refs[0], refs[1]
        o_ref = refs[-1]
        acc = _tn(a_ref[...], b_ref[...])
        acc = acc if lead is None else acc[None]

        @pl.when(pl.program_id(1) == 0)
        def _():
            o_ref[...] = acc

        @pl.when(pl.program_id(1) > 0)
        def _():
            o_ref[...] += acc

    in_specs = [pl.BlockSpec((tk, m), lambda j, k: (k, 0)), pl.BlockSpec((tk, tn), lambda j, k: (k, j))]
    args = [a, b]
    aliases = {}
    if lead is None:
        out_spec = pl.BlockSpec((m, tn), lambda j, k: (0, j))
        out_shape = _sds((m, n), F32)
    else:
        out_spec = pl.BlockSpec((1, m, tn), lambda j, k: (j + lead_off, 0, 0))
        out_shape = _sds((lead, m, tn), F32)
        if prev is not None:
            in_specs.append(pl.BlockSpec(memory_space=pl.ANY))
            args.append(prev)
            aliases = {2: 0}
    return pl.pallas_call(
        body, grid=(nj, s // tk), in_specs=in_specs, out_specs=out_spec, out_shape=out_shape,
        input_output_aliases=aliases,
        compiler_params=_cp(("parallel", "arbitrary")), name=name)(*args)


def _ffn_act_bwd(dyb, w_down, up_g, up_v, pre_g, pre_v, conv_w, tm):
    s = dyb.shape[0]
    nt = s // tm

    def body(dy_ref, wd_ref, upg_ref, upv_ref, preg_ref, prev_ref, cwg_ref, cwv_ref,
             dpg_ref, dpv_ref, dcwg_ref, dcwv_ref, dcbg_ref, dcbv_ref, nxg, nxv):
        i = pl.program_id(1)

        @pl.when(i == 0)
        def _():
            nxg[...] = jnp.zeros_like(nxg)
            nxv[...] = jnp.zeros_like(nxv)
            dcwg_ref[...] = jnp.zeros_like(dcwg_ref)
            dcwv_ref[...] = jnp.zeros_like(dcwv_ref)
            dcbg_ref[...] = jnp.zeros_like(dcbg_ref)
            dcbv_ref[...] = jnp.zeros_like(dcbv_ref)

        dact = _nt(dy_ref[...], wd_ref[...])
        g = upg_ref[...].astype(F32)
        v = upv_ref[...].astype(F32)
        sg = jax.nn.sigmoid(g)
        d_v = dact * (g * sg)
        d_g = dact * v * (sg * (1.0 + g * (1.0 - sg)))

        def conv_bwd(d_up, nx, pre_ref, cw_ref, dp_ref, dcw_ref, dcb_ref):
            ext = jnp.concatenate([d_up, nx[...]], axis=0)
            s1 = pltpu.roll(ext, tm + 8 - 1, 0)[0:tm]
            s2 = pltpu.roll(ext, tm + 8 - 2, 0)[0:tm]
            cw = cw_ref[0]
            dp_ref[...] = (cw[2:3] * d_up + cw[1:2] * s1 + cw[0:1] * s2).astype(BF)
            nx[...] = d_up[0:8]
            pre = pre_ref[...].astype(F32)
            dcw_ref[0, 0:1, :] += jnp.sum(s2 * pre, axis=0, keepdims=True)
            dcw_ref[0, 1:2, :] += jnp.sum(s1 * pre, axis=0, keepdims=True)
            dcw_ref[0, 2:3, :] += jnp.sum(d_up * pre, axis=0, keepdims=True)
            dcb_ref[...] += jnp.sum(d_up, axis=0, keepdims=True)

        conv_bwd(d_g, nxg, preg_ref, cwg_ref, dpg_ref, dcwg_ref, dcbg_ref)
        conv_bwd(d_v, nxv, prev_ref, cwv_ref, dpv_ref, dcwv_ref, dcbv_ref)

    tile = pl.BlockSpec((tm, UP_SHARD), lambda j, i: (nt - 1 - i, j))
    cwspec = lambda off: pl.BlockSpec((1, 3, UP_SHARD), lambda j, i: (j + off, 0, 0))
    acc_cw = pl.BlockSpec((1, 3, UP_SHARD), lambda j, i: (j, 0, 0))
    acc_cb = pl.BlockSpec((1, UP_SHARD), lambda j, i: (0, j))
    buf = pltpu.VMEM((8, UP_SHARD), F32)
    return pl.pallas_call(
        body, grid=(2, nt),
        in_specs=[pl.BlockSpec((tm, D), lambda j, i: (nt - 1 - i, 0)),
                  pl.BlockSpec((UP_SHARD, D), lambda j, i: (j, 0)),
                  tile, tile, tile, tile, cwspec(0), cwspec(2)],
        out_specs=[tile, tile, acc_cw, acc_cw, acc_cb, acc_cb],
        out_shape=[_sds((s, D_FF), BF), _sds((s, D_FF), BF), _sds((2, 3, UP_SHARD), F32),
                   _sds((2, 3, UP_SHARD), F32), _sds((1, D_FF), F32), _sds((1, D_FF), F32)],
        scratch_shapes=[buf, buf],
        compiler_params=_cp(("parallel", "arbitrary")), name="ffn_act_bwd")(
            dyb, w_down, up_g, up_v, pre_g, pre_v, conv_w, conv_w)


def _rms_bwd(dh, y, g):
    r = lax.rsqrt(jnp.mean(y * y, axis=-1, keepdims=True) + EPS)
    n = y * r
    dn = dh * g
    return r * (dn - n * jnp.mean(dn * n, axis=-1, keepdims=True)), dh * n


def _ffn_up_bwd(dp_g, dp_v, w_up, y1, dy2, g2, tm, deps=()):
    s = y1.shape[0]

    def body(dg_ref, dv_ref, w_ref, y_ref, dy2_ref, g_ref, dy1_ref, dgn_ref):
        @pl.when(pl.program_id(0) == 0)
        def _():
            dgn_ref[...] = jnp.zeros_like(dgn_ref)

        dh = _nt(dg_ref[:, 0:UP_SHARD], w_ref[0])
        dh = dh + _nt(dg_ref[:, UP_SHARD:D_FF], w_ref[1])
        dh = dh + _nt(dv_ref[:, 0:UP_SHARD], w_ref[2])
        dh = dh + _nt(dv_ref[:, UP_SHARD:D_FF], w_ref[3])
        dy, dgn = _rms_bwd(dh, y_ref[...], g_ref[...])
        dy1_ref[...] = dy2_ref[...] + dy
        dgn_ref[...] += jnp.sum(dgn, axis=0, keepdims=True)

    row = lambda w: pl.BlockSpec((tm, w), lambda i: (i, 0))
    return pl.pallas_call(
        _after(body, 6, deps), grid=(s // tm,),
        in_specs=[row(D_FF), row(D_FF), _full((4, D, UP_SHARD)), row(D), row(D), _full((1, D))] + _any_specs(deps),
        out_specs=[row(D), _full((1, D))],
        out_shape=[_sds((s, D), F32), _sds((1, D), F32)],
        compiler_params=_cp(("arbitrary",)), name="ffn_up_bwd")(dp_g, dp_v, w_up, y1, dy2, g2, *deps)


def _out_proj_bwd(dy1, w_out, a, b, gates, mix, tm, deps=()):
    s = dy1.shape[0]

    def body(dy_ref, w_ref, a_ref, b_ref, gate_ref, mix_ref, da_ref, db_ref, dzg_ref, dbg_ref, dw_ref):
        @pl.when(pl.program_id(0) == 0)
        def _():
            dbg_ref[...] = jnp.zeros_like(dbg_ref)
            dw_ref[...] = jnp.zeros_like(dw_ref)

        dyb = dy_ref[...].astype(BF)
        dmix = _nt(dyb, w_ref[...])
        gp = gate_ref[:, 0:D].astype(F32)
        ga = gate_ref[:, D:2 * D].astype(F32)
        da_ref[...] = (dmix * gp).astype(BF)
        db_ref[...] = (dmix * ga).astype(BF)
        dzp = dmix * a_ref[...].astype(F32) * (gp * (1.0 - gp))
        dza = dmix * b_ref[...].astype(F32) * (ga * (1.0 - ga))
        dzg_ref[:, 0:D] = dzp.astype(BF)
        dzg_ref[:, D:2 * D] = dza.astype(BF)
        dbg_ref[:, 0:D] += jnp.sum(dzp, axis=0, keepdims=True)
        dbg_ref[:, D:2 * D] += jnp.sum(dza, axis=0, keepdims=True)
        dw_ref[...] += _tn(mix_ref[...], dyb)

    row = lambda w: pl.BlockSpec((tm, w), lambda i: (i, 0))
    return pl.pallas_call(
        _after(body, 6, deps), grid=(s // tm,),
        in_specs=[row(D), _full((D, D)), row(D), row(D), row(2 * D), row(D)] + _any_specs(deps),
        out_specs=[row(D), row(D), row(2 * D), _full((1, 2 * D)), _full((D, D))],
        out_shape=[_sds((s, D), BF), _sds((s, D), BF), _sds((s, 2 * D), BF), _sds((1, 2 * D), F32),
                   _sds((D, D), F32)],
        compiler_params=_cp(("arbitrary",)), name="out_proj_bwd")(dy1, w_out, a, b, gates, mix, *deps)


def _pool_bwd(da, pooled, w_pool, pool_scale, h1, ts):
    s = da.shape[0]
    nt = s // ts
    halo = 16

    def body(da_ref, pooled_ref, wp_ref, ps_ref, h_ref, du_ref, dwp_ref, dps_ref, dwi_ref, nxt):
        g = pl.program_id(0)
        i = pl.program_id(1)
        ti = nt - 1 - i

        @pl.when(i == 0)
        def _():
            nxt[...] = jnp.zeros_like(nxt)
            dwp_ref[...] = jnp.zeros_like(dwp_ref)
            dps_ref[...] = jnp.zeros_like(dps_ref)
            dwi_ref[...] = jnp.zeros_like(dwi_ref)

        pooled = pooled_ref[...]
        dav = da_ref[...].astype(F32)
        dps_ref[...] += jnp.sum(dav * _mm(pooled, wp_ref[0]), axis=0, keepdims=True)
        dm = (dav * ps_ref[...]).astype(BF)
        dwp_ref[0] += _tn(pooled, dm)
        dpool = _nt(dm, wp_ref[0])
        t = (ti * ts + lax.broadcasted_iota(jnp.int32, (ts, 1), 0)).astype(F32)
        for gi in range(4):
            @pl.when(g == gi)
            def _(gi=gi):
                w = 2 << gi
                e = dpool * (1.0 / jnp.minimum(t + 1.0, float(w)))
                acc, span = jnp.concatenate([e, nxt[...]], axis=0), 1
                while span < w:
                    acc = acc + pltpu.roll(acc, ts + halo - span, 0)
                    span *= 2
                du = (acc[0:ts] - dpool).astype(BF)
                du_ref[...] = du
                dwi_ref[...] += _tn(du, h_ref[...])
                nxt[...] = e[0:halo]

    col = pl.BlockSpec((ts, POOL_GROUP), lambda g, i: (nt - 1 - i, g))
    return pl.pallas_call(
        body, grid=(4, nt),
        in_specs=[col, col, pl.BlockSpec((1, POOL_GROUP, POOL_GROUP), lambda g, i: (g, 0, 0)),
                  pl.BlockSpec((1, POOL_GROUP), lambda g, i: (0, g)),
                  pl.BlockSpec((ts, D), lambda g, i: (nt - 1 - i, 0))],
        out_specs=[col, pl.BlockSpec((1, POOL_GROUP, POOL_GROUP), lambda g, i: (g, 0, 0)),
                   pl.BlockSpec((1, POOL_GROUP), lambda g, i: (0, g)),
                   pl.BlockSpec((POOL_GROUP, D), lambda g, i: (g, 0))],
        out_shape=[_sds((s, D), BF), _sds((4, POOL_GROUP, POOL_GROUP), F32), _sds((1, D), F32), _sds((D, D), F32)],
        scratch_shapes=[pltpu.VMEM((halo, POOL_GROUP), F32)],
        compiler_params=_cp(("parallel", "arbitrary")), name="pool_bwd")(da, pooled, w_pool, pool_scale, h1)


def _attn_bwd(qh, kh, vh, sinks, db, deps=()):
    s = qh.shape[1]
    nb = s // BLK

    def body(sink_ref, q_ref, kp_ref, kc_ref, vp_ref, vc_ref, do_ref,
             dq_ref, dk_ref, dv_ref, dsink_ref, ck, cv):
        n = pl.program_id(0)

        @pl.when(n == 0)
        def _():
            ck[...] = jnp.zeros_like(ck)
            cv[...] = jnp.zeros_like(cv)
            dsink_ref[...] = jnp.zeros_like(dsink_ref)

        @pl.when(n < nb)
        def _():
            dov = do_ref[...]
            for khd in range(N_KV):
                q = q_ref[khd * GQA:(khd + 1) * GQA].reshape(GQA * BLK, HEAD)
                k = jnp.concatenate([kp_ref[khd], kc_ref[khd]], axis=0)
                v = jnp.concatenate([vp_ref[khd], vc_ref[khd]], axis=0)
                c0 = khd * GQA * HEAD
                do = jnp.concatenate([dov[:, c0 + HEAD * g:c0 + HEAD * (g + 1)] for g in range(GQA)],
                                     axis=0).astype(BF)
                probs, psink, lower = _attn_probs(q, k, n, _sink_column(sink_ref, khd))
                dp = _fold_band(_nt(do, v), lower)
                delta = jnp.sum(probs * dp, axis=-1, keepdims=True)
                ds = _unfold_band(probs * (dp - delta), lower).astype(BF)
                dq_ref[khd * GQA:(khd + 1) * GQA] = _mm(ds, k).reshape(GQA, BLK, HEAD)
                dk = _tn(ds, q)
                dv = _tn(_unfold_band(probs, lower).astype(BF), do)
                dk_ref[khd] = ck[khd] + dk[0:BLK]
                dv_ref[khd] = cv[khd] + dv[0:BLK]
                ck[khd] = dk[BLK:2 * BLK]
                cv[khd] = dv[BLK:2 * BLK]
                dsk = psink * delta
                lane = lax.broadcasted_iota(jnp.int32, (1, 128), 1)
                acc = jnp.zeros((1, 128), F32)
                for g in range(GQA):
                    acc = acc - jnp.where(lane == khd * GQA + g,
                                          jnp.sum(dsk[g * BLK:(g + 1) * BLK], axis=0, keepdims=True), 0.0)
                dsink_ref[...] += acc

        @pl.when(n == nb)
        def _():
            dk_ref[...] = ck[...]
            dv_ref[...] = cv[...]

    last = nb - 1
    prev = pl.BlockSpec((N_KV, BLK, HEAD), lambda n: (0, jnp.maximum(jnp.minimum(n, last) - 1, 0), 0))
    cur = pl.BlockSpec((N_KV, BLK, HEAD), lambda n: (0, jnp.minimum(n, last), 0))
    kv_out = pl.BlockSpec((N_KV, BLK, HEAD), lambda n: (0, jnp.maximum(n - 1, 0), 0))
    return pl.pallas_call(
        _after(body, 7, deps), grid=(nb + 1,),
        in_specs=[pl.BlockSpec(memory_space=pltpu.SMEM),
                  pl.BlockSpec((N_Q, BLK, HEAD), lambda n: (0, jnp.minimum(n, last), 0)),
                  prev, cur, prev, cur,
                  pl.BlockSpec((BLK, Q_W), lambda n: (jnp.minimum(n, last), 0))] + _any_specs(deps),
        out_specs=[pl.BlockSpec((N_Q, BLK, HEAD), lambda n: (0, jnp.minimum(n, last), 0)), kv_out, kv_out,
                   _full((1, 128))],
        out_shape=[_sds((N_Q, s, HEAD), F32), _sds((N_KV, s, HEAD), F32), _sds((N_KV, s, HEAD), F32),
                   _sds((1, 128), F32)],
        scratch_shapes=[pltpu.VMEM((N_KV, BLK, HEAD), F32), pltpu.VMEM((N_KV, BLK, HEAD), F32)],
        compiler_params=_cp(("arbitrary",)), name="attn_bwd")(sinks, qh, kh, kh, vh, vh, db, *deps)


def _qk_prep_bwd(dqh, dkh, dvh, qkv, pos, wq, wk, invf, bd, tm, deps=()):
    s = qkv.shape[0]

    def fold_heads(row):
        out = row[:, 0:HEAD]
        for h in range(1, row.shape[1] // HEAD):
            out = out + row[:, HEAD * h:HEAD * (h + 1)]
        return out

    def body(dq_ref, dk_ref, dv_ref, qkv_ref, pos_ref, wq_ref, wk_ref, invf_ref, bd_ref,
             dz_ref, dwq_ref, dwk_ref):
        @pl.when(pl.program_id(0) == 0)
        def _():
            dwq_ref[...] = jnp.zeros_like(dwq_ref)
            dwk_ref[...] = jnp.zeros_like(dwk_ref)

        cos, sa, sb = _rope_tables(pos_ref, invf_ref)

        def norm_rope_bwd(dy, xin, w, bdm):
            dn = _rope_t(dy, cos, sa, sb)
            r = lax.rsqrt(_head_sum(xin * xin, bdm) * (1.0 / HEAD) + EPS)
            nh = xin * r
            gw = dn * w
            dx = r * (gw - nh * (_head_sum(gw * nh, bdm) * (1.0 / HEAD)))
            return dx, fold_heads(jnp.sum(dn * nh, axis=0, keepdims=True))

        dq = jnp.concatenate([dq_ref[h] for h in range(N_Q)], axis=1) * (HEAD ** -0.5)
        dk = jnp.concatenate([dk_ref[h] for h in range(N_KV)], axis=1)
        dxq, dwq = norm_rope_bwd(dq, qkv_ref[:, 0:Q_W], wq_ref[...], bd_ref[...])
        dxk, dwk = norm_rope_bwd(dk, qkv_ref[:, Q_W:Q_W + KV_W], wk_ref[...], bd_ref[0:KV_W, 0:KV_W])
        dz_ref[:, 0:Q_W] = dxq.astype(BF)
        dz_ref[:, Q_W:Q_W + KV_W] = dxk.astype(BF)
        dz_ref[:, Q_W + KV_W:QKV_W] = jnp.concatenate([dv_ref[h] for h in range(N_KV)], axis=1).astype(BF)
        dwq_ref[...] += dwq
        dwk_ref[...] += dwk

    heads = lambda n: pl.BlockSpec((n, tm, HEAD), lambda i: (0, i, 0))
    return pl.pallas_call(
        _after(body, 9, deps), grid=(s // tm,),
        in_specs=[heads(N_Q), heads(N_KV), heads(N_KV), pl.BlockSpec((tm, QKV_W), lambda i: (i, 0)),
                  pl.BlockSpec((tm, 1), lambda i: (i, 0)), _full((1, Q_W)), _full((1, KV_W)),
                  _full((1, 2 * HEAD)), _full((Q_W, Q_W))] + _any_specs(deps),
        out_specs=[pl.BlockSpec((tm, QKV_W), lambda i: (i, 0)), _full((1, HEAD)), _full((1, HEAD))],
        out_shape=[_sds((s, QKV_W), BF), _sds((1, HEAD), F32), _sds((1, HEAD), F32)],
        compiler_params=_cp(("arbitrary",)), name="qk_prep_bwd")(
            dqh, dkh, dvh, qkv, pos, wq, wk, invf, bd, *deps)


def _in_proj_bwd(du, dzq, dzg, w_in, x, g1, dy1, tm, deps=()):
    s = x.shape[0]

    def body(du_ref, dzq_ref, dzg_ref, w_ref, x_ref, g_ref, dy_ref, gx_ref, dgn_ref):
        @pl.when(pl.program_id(0) == 0)
        def _():
            dgn_ref[...] = jnp.zeros_like(dgn_ref)

        dh = _mm(du_ref[...], w_ref[0:D, :])
        dh = dh + _mm(dzq_ref[...], w_ref[D:D + QKV_W, :])
        dh = dh + _mm(dzg_ref[...], w_ref[D + QKV_W:IN_W, :])
        dx, dgn = _rms_bwd(dh, x_ref[...], g_ref[...])
        gx_ref[...] = dy_ref[...] + dx
        dgn_ref[...] += jnp.sum(dgn, axis=0, keepdims=True)

    row = lambda w: pl.BlockSpec((tm, w), lambda i: (i, 0))
    return pl.pallas_call(
        _after(body, 7, deps), grid=(s // tm,),
        in_specs=[row(D), row(QKV_W), row(2 * D), _full((IN_W, D)), row(D), _full((1, D)), row(D)] + _any_specs(deps),
        out_specs=[row(D), _full((1, D))],
        out_shape=[_sds((s, D), F32), _sds((1, D), F32)],
        compiler_params=_cp(("arbitrary",)), name="in_proj_bwd")(du, dzq, dzg, w_in, x, g1, dy1, *deps)


def _adamw_step(w, g, m, v):
    mn = B1 * m + (1.0 - B1) * g
    vn = B2 * v + (1.0 - B2) * (g * g)
    m_hat = mn / (1.0 - B1 ** STEP)
    v_hat = vn / (1.0 - B2 ** STEP)
    return -LR * (m_hat / (jnp.sqrt(v_hat) + ADAM_EPS) + WD * w), mn, vn


SMALL_ROWS = 16
SMALL_COLS = 2 * D_FF
SMALL_AT = {"b_gate": (1, 2 * D), "pool_scale": (2, D), "q_norm": (3, HEAD),
            "k_norm": (4, HEAD), "sinks": (5, N_Q), "ffn_norm": (6, D), "conv_b": (7, 2 * D_FF)}
SMALL_LOSS_ROW = 0
SMALL_CONV_W_ROW = 8


def _pack_small(loss_acc, d_bgate, d_pscale, d_qn, d_kn, dsink, d_ffn_norm, dcb_g, dcb_v, dcw_g, dcw_v, dev):
    def body(k_ref, ls_ref, bg_ref, ps_ref, qn_ref, kn_ref, sk_ref, fn_ref, cbg_ref, cbv_ref, cwg_ref, cwv_ref,
             o_ref):
        o_ref[...] = jnp.zeros_like(o_ref)
        o_ref[0, SMALL_LOSS_ROW:SMALL_LOSS_ROW + 1, 0:128] = jnp.sum(ls_ref[...], axis=0, keepdims=True)
        for nm, ref in (("b_gate", bg_ref), ("pool_scale", ps_ref), ("q_norm", qn_ref),
                        ("k_norm", kn_ref), ("ffn_norm", fn_ref)):
            row, n = SMALL_AT[nm]
            o_ref[0, row:row + 1, 0:n] = ref[...]
        row, _ = SMALL_AT["sinks"]
        o_ref[0, row:row + 1, 0:128] = sk_ref[...]
        row, _ = SMALL_AT["conv_b"]
        o_ref[0, row:row + 1, 0:D_FF] = cbg_ref[...]
        o_ref[0, row:row + 1, D_FF:2 * D_FF] = cbv_ref[...]
        for k in range(3):
            row = SMALL_CONV_W_ROW + k
            for half in range(2):
                o_ref[0, row:row + 1, half * UP_SHARD:(half + 1) * UP_SHARD] = cwg_ref[half, k:k + 1, :]
                o_ref[0, row:row + 1, (2 + half) * UP_SHARD:(3 + half) * UP_SHARD] = cwv_ref[half, k:k + 1, :]

    args = [loss_acc, d_bgate, d_pscale, d_qn, d_kn, dsink, d_ffn_norm, dcb_g, dcb_v, dcw_g, dcw_v]
    grid_spec = pltpu.PrefetchScalarGridSpec(
        num_scalar_prefetch=1, grid=(1,),
        in_specs=[pl.BlockSpec(a.shape, functools.partial(lambda nd, i, k: (0,) * nd, a.ndim)) for a in args],
        out_specs=pl.BlockSpec((1, SMALL_ROWS, SMALL_COLS), lambda i, k: (k[0], 0, 0)))
    return pl.pallas_call(body, grid_spec=grid_spec, out_shape=_sds((N_DEV, SMALL_ROWS, SMALL_COLS), F32),
                          name="pack_small")(dev, *args)


def _small_update(stack, attn_stack, params):
    names = list(params)

    def body(*refs):
        s_ref, a_ref = refs[0], refs[1]
        ins = refs[2:2 + 3 * len(names)]
        outs = refs[2 + 3 * len(names):]
        tot, tot_a = s_ref[0], a_ref[0]
        for d in range(1, N_DEV):
            tot = tot + s_ref[d]
            tot_a = tot_a + a_ref[d]
        for i, nm in enumerate(names):
            if nm == "attn_norm":
                g = tot_a
            else:
                row, n = SMALL_AT[nm]
                g = tot[row:row + 1, 0:n]
            delta, mn, vn = _adamw_step(ins[3 * i][...], g, ins[3 * i + 1][...], ins[3 * i + 2][...])
            outs[4 * i][...] = g
            outs[4 * i + 1][...] = delta
            outs[4 * i + 2][...] = mn
            outs[4 * i + 3][...] = vn
        outs[-2][...] = tot[SMALL_CONV_W_ROW:SMALL_CONV_W_ROW + 3, :]
        outs[-1][...] = jnp.sum(tot[SMALL_LOSS_ROW:SMALL_LOSS_ROW + 1, 0:128], axis=1, keepdims=True) * (0.5 / D)

    flat = [a for nm in names for a in params[nm]]
    out_shape = ([_sds(params[nm][0].shape, F32) for nm in names for _ in range(4)]
                 + [_sds((3, SMALL_COLS), F32), _sds((1, 1), F32)])
    res = pl.pallas_call(body, out_shape=out_shape, name="small_update")(stack, attn_stack, *flat)
    return {nm: list(res[4 * i:4 * i + 4]) for i, nm in enumerate(names)}, res[-2], res[-1]


def _adamw(w, g, m, v, tr, name, deps=()):
    r, c = w.shape

    def body(w_ref, g_ref, m_ref, v_ref, go_ref, d_ref, mo_ref, vo_ref):
        gv = g_ref[...]
        go_ref[...] = gv
        d_ref[...], mo_ref[...], vo_ref[...] = _adamw_step(w_ref[...], gv, m_ref[...], v_ref[...])

    blk = pl.BlockSpec((tr, c), lambda i: (i, 0))
    return pl.pallas_call(
        _after(body, 4, deps), grid=(r // tr,), in_specs=[blk] * 4 + _any_specs(deps), out_specs=[blk] * 4,
        out_shape=[_sds((r, c), F32)] * 4, compiler_params=_cp(("parallel",)), name=name)(w, g, m, v, *deps)


def _place():
    x, y, c = lax.axis_index("x"), lax.axis_index("y"), lax.axis_index("c")
    chips = [(1 - x, y), (x, 1 - y), (1 - x, 1 - y)]
    return x, y, c, chips


def _rows(ref, lead, h, rh):
    sl = pl.ds(pl.multiple_of(h * rh, 16), rh)
    return ref.at[sl, :] if lead is None else ref.at[lead, sl, :]


def _all_gather_weights(halved, whole):
    nh, nw = len(halved), len(whole)
    na = nh + nw
    arrays = list(halved) + list(whole)
    out_dtypes = [BF] * nh + [a.dtype for a in whole]
    cast_rows = 128

    def body(*refs):
        ins, outs = refs[:na], refs[na:2 * na]
        raw, stage = refs[2 * na:3 * na], refs[3 * na:3 * na + nh]
        ici_send, ici_recv, fwd_send, fwd_recv, in_sem, loc_sem = refs[3 * na + nh:]
        x, y, c, chips = _place()
        me = 2 * x + y
        sibling = (x, y, 1 - c)
        loads = [pltpu.make_async_copy(ins[a], raw[a], in_sem.at[a]) for a in range(na)]
        for cp in loads:
            cp.start()

        def ici(a, j, src_chip, src=None):
            if a < nh:
                rh = arrays[a].shape[0] // 2
                dst = _rows(outs[a], src_chip, c, rh)
                src = dst if src is None else _rows(src, None, c, rh)
            else:
                dst = outs[a].at[src_chip]
                src = dst if src is None else src
            return pltpu.make_async_remote_copy(
                src_ref=src, dst_ref=dst, send_sem=ici_send.at[3 * a + j], recv_sem=ici_recv.at[3 * a + j],
                device_id=(*chips[j], c), device_id_type=MESH)

        def fwd(a, j, half):
            rh = arrays[a].shape[0] // 2
            kj = 2 * chips[j][0] + chips[j][1]
            blk = _rows(outs[a], kj, half, rh)
            return pltpu.make_async_remote_copy(
                src_ref=blk, dst_ref=blk, send_sem=fwd_send.at[3 * a + j], recv_sem=fwd_recv.at[3 * a + j],
                device_id=sibling, device_id_type=MESH)

        local, sends = [], []
        for a in range(na):
            loads[a].wait()
            if a < nh:
                r = arrays[a].shape[0]
                for r0 in range(0, r, cast_rows):
                    r1 = min(r0 + cast_rows, r)
                    stage[a][r0:r1, :] = raw[a][r0:r1, :].astype(BF)
                own = stage[a]
            else:
                own = raw[a]
            cp = pltpu.make_async_copy(own, outs[a].at[me], loc_sem.at[a])
            cp.start()
            local.append(cp)
            for j in range(3):
                cp = ici(a, j, me, src=own)
                cp.start()
                sends.append(cp)
        passed = []
        for a in range(na):
            for j in range(3):
                kj = 2 * chips[j][0] + chips[j][1]
                ici(a, j, kj).wait_recv()
                if a < nh:
                    cp = fwd(a, j, c)
                    cp.start()
                    passed.append(cp)
        for a in range(nh):
            for j in range(3):
                fwd(a, j, 1 - c).wait_recv()
        for cp in sends + passed:
            cp.wait_send()
        for cp in local:
            cp.wait()

    any_spec = pl.BlockSpec(memory_space=pl.ANY)
    return pl.pallas_call(
        body, in_specs=[any_spec] * na, out_specs=[any_spec] * na,
        out_shape=[_sds((N_CHIPS,) + a.shape, dt) for a, dt in zip(arrays, out_dtypes)],
        scratch_shapes=[pltpu.VMEM(a.shape, a.dtype) for a in arrays] + [pltpu.VMEM(a.shape, BF) for a in halved]
        + [pltpu.SemaphoreType.DMA((3 * na,)), pltpu.SemaphoreType.DMA((3 * na,)),
           pltpu.SemaphoreType.DMA((3 * nh,)), pltpu.SemaphoreType.DMA((3 * nh,)),
           pltpu.SemaphoreType.DMA((na,)), pltpu.SemaphoreType.DMA((na,))],
        compiler_params=pltpu.CompilerParams(vmem_limit_bytes=VMEM_LIMIT_MB << 20),
        name="all_gather_weights")(*arrays)


def _sibling_halves(grads):
    na = len(grads)

    def body(*refs):
        ins, outs = refs[:na], refs[na:2 * na]
        send_sem, recv_sem = refs[2 * na:]
        x, y, c, _ = _place()
        copies = []
        for a in range(na):
            rh = grads[a].shape[1] // 2
            src = ins[a].at[:, pl.ds(pl.multiple_of((1 - c) * rh, 8), rh), :]
            copies.append(pltpu.make_async_remote_copy(
                src_ref=src, dst_ref=outs[a], send_sem=send_sem.at[a], recv_sem=recv_sem.at[a],
                device_id=(x, y, 1 - c), device_id_type=MESH))
        for cp in copies:
            cp.start()
        for cp in copies:
            cp.wait()

    any_spec = pl.BlockSpec(memory_space=pl.ANY)
    return pl.pallas_call(
        body, in_specs=[any_spec] * na, out_specs=[any_spec] * na,
        out_shape=[_sds((N_CHIPS, g.shape[1] // 2, g.shape[2]), F32) for g in grads],
        scratch_shapes=[pltpu.SemaphoreType.DMA((na,)), pltpu.SemaphoreType.DMA((na,))],
        name="sibling_halves")(*grads)


def _pair_sum(g, recv, c, tr, name):
    _, r, cols = g.shape
    rh = r // 2
    nr = rh // tr

    def body(c_ref, g_ref, r_ref, o_ref):
        o_ref[...] = (g_ref[...] + r_ref[...]).astype(BF)

    grid_spec = pltpu.PrefetchScalarGridSpec(
        num_scalar_prefetch=1, grid=(N_CHIPS, nr),
        in_specs=[pl.BlockSpec((1, tr, cols), lambda k, i, c_ref: (k, c_ref[0] * nr + i, 0)),
                  pl.BlockSpec((1, tr, cols), lambda k, i, c_ref: (k, i, 0))],
        out_specs=pl.BlockSpec((1, tr, cols), lambda k, i, c_ref: (k, i, 0)))
    return pl.pallas_call(
        body, grid_spec=grid_spec, out_shape=_sds((N_CHIPS, rh, cols), BF),
        compiler_params=_cp(("parallel", "parallel")), name=name)(c, g, recv)


def _chip_exchange(halves, small):
    na = len(halves)
    srows = small.shape[0]

    def body(*refs):
        ins, small_ref = refs[:na], refs[na]
        outs, small_out = refs[na + 1:2 * na + 1], refs[2 * na + 1]
        send_sem, recv_sem, s_send, s_recv = refs[2 * na + 2:]
        x, y, c, chips = _place()
        me = 4 * x + 2 * y + c
        copies = []
        for a in range(na):
            for j in range(3):
                kj = 2 * chips[j][0] + chips[j][1]
                copies.append(pltpu.make_async_remote_copy(
                    src_ref=ins[a].at[kj], dst_ref=outs[a].at[j],
                    send_sem=send_sem.at[3 * a + j], recv_sem=recv_sem.at[3 * a + j],
                    device_id=(*chips[j], c), device_id_type=MESH))
        for r in range(1, N_DEV):
            peer = (x ^ (r >> 2), y ^ ((r >> 1) & 1), c ^ (r & 1))
            copies.append(pltpu.make_async_remote_copy(
                src_ref=small_ref, dst_ref=small_out.at[me],
                send_sem=s_send.at[r - 1], recv_sem=s_recv.at[r - 1], device_id=peer, device_id_type=MESH))
        for cp in copies:
            cp.start()
        small_out[pl.ds(me, 1)] = small_ref[...][None]
        for cp in copies:
            cp.wait()

    any_spec = pl.BlockSpec(memory_space=pl.ANY)
    vmem = pl.BlockSpec(memory_space=pltpu.VMEM)
    return pl.pallas_call(
        body, in_specs=[any_spec] * na + [vmem], out_specs=[any_spec] * na + [vmem],
        out_shape=[_sds((3,) + h.shape[1:], h.dtype) for h in halves] + [_sds((N_DEV, srows, 128), F32)],
        scratch_shapes=[pltpu.SemaphoreType.DMA((3 * na,)), pltpu.SemaphoreType.DMA((3 * na,)),
                        pltpu.SemaphoreType.DMA((N_DEV - 1,)), pltpu.SemaphoreType.DMA((N_DEV - 1,))],
        name="chip_exchange")(*halves, small)


def _chip_sum(g, sib, recv, place, tr, name):
    _, r, cols = g.shape
    rh = r // 2
    nr = rh // tr

    def body(p_ref, g_ref, s_ref, r0_ref, r1_ref, r2_ref, o_ref):
        own = g_ref[0] + s_ref[0]
        o_ref[...] = ((own + r0_ref[0].astype(F32)) + r1_ref[0].astype(F32)) + r2_ref[0].astype(F32)

    rspec = lambda j: pl.BlockSpec((1, tr, cols), lambda i, p: (j, i, 0))
    grid_spec = pltpu.PrefetchScalarGridSpec(
        num_scalar_prefetch=1, grid=(nr,),
        in_specs=[pl.BlockSpec((1, tr, cols), lambda i, p: (p[0], p[1] * nr + i, 0)),
                  pl.BlockSpec((1, tr, cols), lambda i, p: (p[0], i, 0)), rspec(0), rspec(1), rspec(2)],
        out_specs=pl.BlockSpec((tr, cols), lambda i, p: (p[1] * nr + i, 0)))
    return pl.pallas_call(
        body, grid_spec=grid_spec, out_shape=_sds((r, cols), F32),
        compiler_params=_cp(("parallel",)), name=name)(place, g, sib, recv, recv, recv)


def _sibling_exchange(shards):
    na = len(shards)

    def body(*refs):
        ins, outs = refs[:na], refs[na:2 * na]
        send_sem, recv_sem = refs[2 * na:]
        x, y, c, _ = _place()
        for a in range(na):
            rh = shards[a].shape[0] // 2
            pltpu.make_async_remote_copy(
                src_ref=_rows(ins[a], None, c, rh), dst_ref=_rows(outs[a], None, c, rh),
                send_sem=send_sem.at[a], recv_sem=recv_sem.at[a],
                device_id=(x, y, 1 - c), device_id_type=MESH).start()
        for a in range(na):
            rh = shards[a].shape[0] // 2
            pltpu.make_async_remote_copy(
                src_ref=_rows(ins[a], None, c, rh), dst_ref=_rows(outs[a], None, 1 - c, rh),
                send_sem=send_sem.at[a], recv_sem=recv_sem.at[a],
                device_id=(x, y, 1 - c), device_id_type=MESH).wait()

    any_spec = pl.BlockSpec(memory_space=pl.ANY)
    return pl.pallas_call(
        body, in_specs=[any_spec] * na, out_specs=[any_spec] * na,
        out_shape=[_sds(h.shape, F32) for h in shards],
        input_output_aliases={a: a for a in range(na)},
        scratch_shapes=[pltpu.SemaphoreType.DMA((na,)), pltpu.SemaphoreType.DMA((na,))],
        name="sibling_exchange")(*shards)


def _device_sum(stack, deps=()):
    _, rows, _ = stack.shape

    def body(s_ref, o_ref):
        acc = s_ref[0]
        for d in range(1, N_DEV):
            acc = acc + s_ref[d]
        o_ref[...] = acc

    vmem = pl.BlockSpec(memory_space=pltpu.VMEM)
    return pl.pallas_call(_after(body, 1, deps), in_specs=[vmem] + _any_specs(deps), out_specs=vmem,
                          out_shape=_sds((rows, 128), F32), name="device_sum")(stack, *deps)


_HBM = pl.BlockSpec(memory_space=pltpu.HBM)
_SEM = pl.BlockSpec(memory_space=pltpu.SEMAPHORE)
_EFFECT = pltpu.SideEffectType.DATAFLOW_SIDE_EFFECTING


def _remote(src, dst, ssem, rsem, k, device):
    return pltpu.make_async_remote_copy(src_ref=src, dst_ref=dst, send_sem=ssem.at[k], recv_sem=rsem.at[k],
                                        device_id=device, device_id_type=MESH)


def _split_start(name, bufs, plan, n):
    nb = len(bufs)

    def body(*refs):
        sends, _ = plan(refs[:nb], refs[nb], refs[nb + 1])
        for cp in sends:
            cp.start()
        refs[-1][...] = jnp.zeros_like(refs[-1])

    res = pl.pallas_call(
        body, name=name,
        out_shape=(pltpu.SemaphoreType.DMA((n,)), pltpu.SemaphoreType.DMA((n,)))
        + tuple(pltpu.HBM(b.shape, b.dtype) for b in bufs) + (_sds((8, 128), F32),),
        in_specs=[_HBM] * nb,
        out_specs=(_SEM, _SEM) + (_HBM,) * nb + (pl.BlockSpec(memory_space=pltpu.VMEM),),
        input_output_aliases={i: i + 2 for i in range(nb)},
        compiler_params=pltpu.CompilerParams(has_side_effects=_EFFECT),
    )(*[pltpu.with_memory_space_constraint(b, pltpu.HBM) for b in bufs])
    return res[0], res[1], list(res[2:2 + nb]), res[2 + nb]


def _split_wait(name, send_sem, recv_sem, bufs, plan, after):
    nb = len(bufs)

    def body(*refs):
        sends, arrivals = plan(refs[:nb], refs[nb], refs[nb + 1])
        for cp in sends:
            cp.wait_send()
        for cp in arrivals:
            cp.wait_recv()

    res = pl.pallas_call(
        body, name=name, out_shape=tuple(pltpu.HBM(b.shape, b.dtype) for b in bufs),
        in_specs=[_HBM] * nb + [_SEM, _SEM, pl.BlockSpec(memory_space=pl.ANY)],
        out_specs=(_HBM,) * nb, input_output_aliases={i: i for i in range(nb)},
        compiler_params=pltpu.CompilerParams(has_side_effects=_EFFECT),
    )(*bufs, send_sem, recv_sem, after)
    return list(res)


def _plan_sibling_halves(shapes):
    na = len(shapes)

    def plan(refs, ssem, rsem):
        x, y, c, _ = _place()
        cps = []
        for a in range(na):
            rh = shapes[a][1] // 2
            src = refs[a].at[:, pl.ds(pl.multiple_of((1 - c) * rh, 8), rh), :]
            cps.append(_remote(src, refs[na + a], ssem, rsem, a, (x, y, 1 - c)))
        return cps, cps

    return plan


def _to_all(ref, ssem, rsem, base):
    x, y, c, _ = _place()
    mine = ref.at[4 * x + 2 * y + c]
    return [_remote(mine, mine, ssem, rsem, base + r - 1, (x ^ (r >> 2), y ^ ((r >> 1) & 1), c ^ (r & 1)))
            for r in range(1, N_DEV)]


def _plan_chip_exchange(na, with_small):
    def plan(refs, ssem, rsem):
        _, _, c, chips = _place()
        cps = []
        for a in range(na):
            for j in range(3):
                kj = 2 * chips[j][0] + chips[j][1]
                cps.append(_remote(refs[a].at[kj], refs[na + a].at[j], ssem, rsem, 3 * a + j, (*chips[j], c)))
        if with_small:
            cps += _to_all(refs[2 * na], ssem, rsem, 3 * na)
        return cps, cps

    return plan


def _plan_sibling_swap(shapes, with_small):
    def plan(refs, ssem, rsem):
        x, y, c, _ = _place()
        sends, arrivals = [], []
        for a, shp in enumerate(shapes):
            rh = shp[0] // 2
            mine, other = _rows(refs[a], None, c, rh), _rows(refs[a], None, 1 - c, rh)
            sends.append(_remote(mine, mine, ssem, rsem, a, (x, y, 1 - c)))
            arrivals.append(_remote(mine, other, ssem, rsem, a, (x, y, 1 - c)))
        if with_small:
            cps = _to_all(refs[len(shapes)], ssem, rsem, len(shapes))
            sends += cps
            arrivals += cps
        return sends, arrivals

    return plan


def _plan_gather_chips(shapes):
    def plan(refs, ssem, rsem):
        x, y, c, chips = _place()
        me = 2 * x + y
        sends, arrivals = [], []
        for a, shp in enumerate(shapes):
            rh = shp[1] // 2
            mine = _rows(refs[a], me, c, rh)
            for j in range(3):
                land = _rows(refs[a], 2 * chips[j][0] + chips[j][1], c, rh)
                sends.append(_remote(mine, mine, ssem, rsem, 3 * a + j, (*chips[j], c)))
                arrivals.append(_remote(land, land, ssem, rsem, 3 * a + j, (*chips[j], c)))
        return sends, arrivals

    return plan


def _plan_gather_sibling(shapes):
    def plan(refs, ssem, rsem):
        x, y, c, chips = _place()
        sends, arrivals = [], []
        for a, shp in enumerate(shapes):
            rh = shp[1] // 2
            for j in range(3):
                kj = 2 * chips[j][0] + chips[j][1]
                got, land = _rows(refs[a], kj, c, rh), _rows(refs[a], kj, 1 - c, rh)
                sends.append(_remote(got, got, ssem, rsem, 3 * a + j, (x, y, 1 - c)))
                arrivals.append(_remote(got, land, ssem, rsem, 3 * a + j, (x, y, 1 - c)))
        return sends, arrivals

    return plan


def _into_slice(w, k, n, tr, dtype, name, deps=()):
    r, cols = w.shape

    def body(k_ref, w_ref, o_ref):
        o_ref[0] = w_ref[...].astype(dtype)

    grid_spec = pltpu.PrefetchScalarGridSpec(
        num_scalar_prefetch=1, grid=(r // tr,),
        in_specs=[pl.BlockSpec((tr, cols), lambda i, k: (i, 0))] + _any_specs(deps),
        out_specs=pl.BlockSpec((1, tr, cols), lambda i, k: (k[0], i, 0)))
    return pl.pallas_call(_after(body, 2, deps), grid_spec=grid_spec, out_shape=_sds((n, r, cols), dtype),
                          compiler_params=_cp(("parallel",)), name=name)(k, w, *deps)


class _LateWeights:
    def __init__(self, shards, chip, names, tiles, deps):
        bufs = [_into_slice(w, chip, N_CHIPS, t, BF, "own_" + nm, deps=deps)
                for w, nm, t in zip(shards, names, tiles)]
        self.n = 3 * len(bufs)
        self.chips, self.sibling = _plan_gather_chips([b.shape for b in bufs]), _plan_gather_sibling([b.shape for b in bufs])
        self.ssem, self.rsem, self.bufs, token = _split_start("gather_chips_start", bufs, self.chips, self.n)
        self.first = (token,)

    def middle(self, after):
        bufs = _split_wait("gather_chips_wait", self.ssem, self.rsem, self.bufs, self.chips, after)
        self.ssem, self.rsem, self.bufs, token = _split_start("gather_sibling_start", bufs, self.sibling, self.n)
        return (token,)

    def last(self, after):
        return _split_wait("gather_sibling_wait", self.ssem, self.rsem, self.bufs, self.sibling, after)


class _GradReduce:
    def __init__(self, tag, place, names, tiles):
        self.tag, self.place, self.names, self.tiles = tag, place, names, tiles
        self.small_all = None

    def first(self, grads):
        self.na = len(grads)
        self.p1 = _plan_sibling_halves([g.shape for g in grads])
        lands = [lax.empty((N_CHIPS, g.shape[1] // 2, g.shape[2]), F32) for g in grads]
        self.ssem, self.rsem, self.bufs, token = _split_start(
            self.tag + "_halves_start", list(grads) + lands, self.p1, self.na)
        return (token,)

    def second(self, after, small=None):
        bufs = _split_wait(self.tag + "_halves_wait", self.ssem, self.rsem, self.bufs, self.p1, after)
        self.grads, self.sib = bufs[:self.na], bufs[self.na:]
        halves = [_pair_sum(g, r, self.place[1:2], t, "pair_sum_" + nm)
                  for g, r, t, nm in zip(self.grads, self.sib, self.tiles, self.names)]
        lands = [lax.empty((3,) + h.shape[1:], h.dtype) for h in halves]
        extra = [] if small is None else [small]
        self.p2 = _plan_chip_exchange(self.na, small is not None)
        self.ssem, self.rsem, self.bufs, token = _split_start(
            self.tag + "_chips_start", halves + lands + extra, self.p2, 3 * self.na + (N_DEV - 1) * len(extra))
        return (token,)

    def third(self, after, small=None):
        bufs = _split_wait(self.tag + "_chips_wait", self.ssem, self.rsem, self.bufs, self.p2, after)
        if len(bufs) > 2 * self.na:
            self.small_chips = bufs[2 * self.na]
        mine = [_chip_sum(g, sb, r, self.place, t, "chip_sum_" + nm)
                for g, sb, r, t, nm in zip(self.grads, self.sib, bufs[self.na:2 * self.na], self.tiles, self.names)]
        extra = [] if small is None else [small]
        self.p3 = _plan_sibling_swap([m.shape for m in mine], small is not None)
        self.ssem, self.rsem, self.bufs, token = _split_start(
            self.tag + "_swap_start", mine + extra, self.p3, self.na + (N_DEV - 1) * len(extra))
        return (token,)

    def last(self, after):
        bufs = _split_wait(self.tag + "_swap_wait", self.ssem, self.rsem, self.bufs, self.p3, after)
        if len(bufs) > self.na:
            self.small_swap = bufs[self.na]
        return bufs[:self.na]


def _pack(parts, rows):
    flat = jnp.concatenate([p.reshape(-1) for p in parts])
    return jnp.pad(flat, (0, rows * 128 - flat.shape[0])).reshape(rows, 128)


def _unpack(buf, shapes):
    flat = buf.reshape(-1)
    out, off = [], 0
    for shp in shapes:
        n = 1
        for d in shp:
            n *= d
        out.append(flat[off:off + n].reshape(shp))
        off += n
    return out


def _rows_for(n):
    return -(-n // (8 * 128)) * 8


class _WeightsAtHand:
    def __init__(self, wup, wout, wdown):
        self.first, self.weights = (), [wup, wout, wdown]

    def middle(self, after):
        return ()

    def last(self, after):
        return self.weights


class _GradsKept:
    def first(self, grads):
        self.grads = list(grads)
        return ()

    def second(self, after, small=None):
        return ()

    def third(self, after, small=None):
        return ()

    def last(self, after):
        return self.grads


def _forward_backward(xs, pos, tgt, win, wpool, cw, attn_norm, b_gate, pool_scale, q_norm, k_norm, sinks,
                      ffn_norm, conv_b, late, early, rest, dev):
    s = xs.shape[0]
    tm = min(512, s)
    tk = min(1024, s)
    inv_freq = ROPE_THETA ** (-jnp.arange(0, ROPE_DIM, 2, dtype=F32) / ROPE_DIM)
    lane = jnp.arange(2 * HEAD) % HEAD
    invf = jnp.where(lane < ROPE_DIM, inv_freq[lane % (ROPE_DIM // 2)], 0.0).reshape(1, 2 * HEAD)
    wq = jnp.tile(q_norm, (1, N_Q))
    wk = jnp.tile(k_norm, (1, N_KV))
    head_of = jnp.arange(Q_W) // HEAD
    bd = (head_of[:, None] == head_of[None, :]).astype(BF)
    sink = sinks[0]

    h1, u, qkv, gates = _attn_in_proj(xs, attn_norm, win, b_gate, tm, deps=late.first)
    qh, kh, vh = _qk_prep(qkv, pos, wq, wk, invf, bd, tm)
    battn = _attn_fwd(qh, kh, vh, sink, deps=late.middle(qh))
    apool, pooled = _pool_fwd(u, wpool, pool_scale, min(512, s))
    wup, wout, wdown = late.last(apool)
    wout = wout.reshape(D, D)
    wdown = wdown.reshape(D_FF, D)
    mix, y1, h2 = _mix_out_proj(apool, battn, gates, xs, wout, ffn_norm, tm)
    pre_g, pre_v, up_g, up_v, act = _ffn_up(h2, wup, cw, conv_b, tm)
    dy2, dy2b, loss_acc = _ffn_down_loss(act, wdown, y1, tgt, tm)

    d_wdown = _grad_matmul(act, dy2b, 512, tk, "grad_w_down")
    dp_g, dp_v, dcw_g, dcw_v, dcb_g, dcb_v = _ffn_act_bwd(dy2b, wdown, up_g, up_v, pre_g, pre_v, cw, tm)
    d_wup = _grad_matmul(h2, dp_g, UP_SHARD, tk, "grad_w_up_gate", lead=N_CHIPS)
    d_wup = _grad_matmul(h2, dp_v, UP_SHARD, tk, "grad_w_up_value", lead=N_CHIPS, prev=d_wup, lead_off=2)
    token = early.first([d_wdown.reshape(N_CHIPS, D_FF // N_CHIPS, D), d_wup])
    dy1, d_ffn_norm = _ffn_up_bwd(dp_g, dp_v, wup, y1, dy2, ffn_norm, tm, deps=token)
    token = early.second(dy1)
    da, db, dzg, d_bgate, d_wout = _out_proj_bwd(dy1, wout, apool, battn, gates, mix, tm, deps=token)
    du, d_wpool, d_pscale, d_win_pool = _pool_bwd(da, pooled, wpool, pool_scale, h1, min(512, s))
    dqh, dkh, dvh, dsink = _attn_bwd(qh, kh, vh, sink, db)
    token = early.third(dqh)
    dzq, d_qn, d_kn = _qk_prep_bwd(dqh, dkh, dvh, qkv, pos, wq, wk, invf, bd, tm, deps=token)
    d_win_t = jnp.concatenate([
        d_win_pool,
        _grad_matmul(dzq, h1, D, tk, "grad_w_in_qkv"),
        _grad_matmul(dzg, h1, D, tk, "grad_w_in_gates")], axis=0)
    token = rest.first([
        d_win_t.reshape(N_CHIPS, IN_W // N_CHIPS, D),
        d_wout.reshape(N_CHIPS, D // N_CHIPS, D),
        d_wpool.reshape(4, N_CHIPS, 64, POOL_GROUP).transpose(1, 0, 2, 3).reshape(N_CHIPS, 4 * 64, POOL_GROUP)])
    small = _pack_small(loss_acc, d_bgate, d_pscale, d_qn, d_kn, dsink, d_ffn_norm, dcb_g, dcb_v, dcw_g, dcw_v, dev)
    token = rest.second(token[0] if token else None, small=small)
    grad_x, d_attn_norm = _in_proj_bwd(du, dzq, dzg, win, xs, attn_norm, dy1, tm, deps=token)
    return grad_x, d_attn_norm, small


def kernel(x, positions, attn_norm, w_in, b_gate, w_pool, pool_scale, q_norm, k_norm, sinks, w_out, ffn_norm, w_up, conv_w, conv_b, w_down, loss_target, m_attn_norm, m_w_in, m_b_gate, m_w_pool, m_pool_scale, m_q_norm, m_k_norm, m_sinks, m_w_out, m_ffn_norm, m_w_up, m_conv_w, m_conv_b, m_w_down, v_attn_norm, v_w_in, v_b_gate, v_w_pool, v_pool_scale, v_q_norm, v_k_norm, v_sinks, v_w_out, v_ffn_norm, v_w_up, v_conv_w, v_conv_b, v_w_down):
    s = x.shape[1]
    xs = x[0]
    tgt = loss_target[0]
    pos = positions[0].reshape(s, 1)
    cx, cy, cc = lax.axis_index("x"), lax.axis_index("y"), lax.axis_index("c")
    chip = 2 * cx + cy

    chip_arr = chip.reshape(1).astype(jnp.int32)
    dev_arr = (2 * chip + cc).reshape(1).astype(jnp.int32)
    place = jnp.stack([chip, cc]).astype(jnp.int32)

    g_in, g_pool, g_cw = _all_gather_weights(
        [jnp.swapaxes(w_in[0], 0, 1), w_pool[0].reshape(4 * 64, POOL_GROUP)], [conv_w[0]])
    win = g_in.reshape(IN_W, D)
    wpool = g_pool.reshape(N_CHIPS, 4, 64, POOL_GROUP).transpose(1, 0, 2, 3).reshape(4, POOL_GROUP, POOL_GROUP)
    late = _LateWeights([w_up[0], w_out[0], w_down[0]], chip_arr, ["w_up", "w_out", "w_down"], [256, 256, 352],
                        deps=(g_in,))
    early = _GradReduce("early", place, ["w_down", "w_up"], [176, 256])
    rest = _GradReduce("rest", place, ["w_in", "w_out", "w_pool"], [272, 128, 128])

    grad_x, d_attn_norm, _ = _forward_backward(
        xs, pos, tgt, win, wpool, g_cw, attn_norm, b_gate, pool_scale, q_norm, k_norm, sinks, ffn_norm, conv_b,
        late, early, rest, dev_arr)

    def two_d(a):
        return a.reshape(-1, a.shape[-1])

    def update(nm, w, g, m, v, tr, deps=()):
        res = _adamw(two_d(w), g, two_d(m), two_d(v), tr, "adamw_" + nm, deps=deps)
        return [r.reshape(w.shape) for r in res]

    attn_stack = _into_slice(d_attn_norm, dev_arr, N_DEV, 1, F32, "own_attn_norm")
    g_wdown, g_wup = early.last(grad_x)
    big_out = {"w_up": update("w_up", w_up, g_wup, m_w_up, v_w_up, 256)}
    big_out["w_down"] = update("w_down", w_down, g_wdown, m_w_down, v_w_down, 176, deps=(big_out["w_up"][1],))
    token = rest.third(big_out["w_down"][1], small=attn_stack)
    g_win_t, g_wout, g_wpool = rest.last(token[0])
    small_out, g_convw_all, loss = _small_update(rest.small_chips, rest.small_swap, {
        "attn_norm": (attn_norm, m_attn_norm, v_attn_norm), "b_gate": (b_gate, m_b_gate, v_b_gate),
        "pool_scale": (pool_scale, m_pool_scale, v_pool_scale), "q_norm": (q_norm, m_q_norm, v_q_norm),
        "k_norm": (k_norm, m_k_norm, v_k_norm), "sinks": (sinks, m_sinks, v_sinks),
        "ffn_norm": (ffn_norm, m_ffn_norm, v_ffn_norm), "conv_b": (conv_b, m_conv_b, v_conv_b)})
    g_convw = lax.dynamic_slice_in_dim(g_convw_all, chip * UP_SHARD, UP_SHARD, axis=1)
    small_out["conv_w"] = update("conv_w", conv_w, g_convw, m_conv_w, v_conv_w, 3)
    flip = lambda a: jnp.swapaxes(a[0], 0, 1)
    res = _adamw(flip(w_in), g_win_t, flip(m_w_in), flip(v_w_in), 272, "adamw_w_in")
    big_out["w_in"] = [jnp.swapaxes(r, 0, 1)[None] for r in res]
    big_out["w_out"] = update("w_out", w_out, g_wout, m_w_out, v_w_out, 128)
    big_out["w_pool"] = update("w_pool", w_pool, g_wpool, m_w_pool, v_w_pool, 128)

    order = ["attn_norm", "w_in", "b_gate", "w_pool", "pool_scale", "q_norm", "k_norm", "sinks", "w_out",
             "ffn_norm", "w_up", "conv_w", "conv_b", "w_down"]
    allout = {**big_out, **small_out}
    outs = [loss.reshape(()), grad_x[None]]
    for k in range(4):
        outs += [allout[nm][k] for nm in order]
    return tuple(outs)
```

```python
import functools

import jax
import jax.numpy as jnp
from jax import lax
from jax.experimental import pallas as pl
from jax.experimental.pallas import tpu as pltpu

D = 1024
D_FF = 2816
HEAD = 64
N_Q = 16
N_KV = 2
GQA = 8
BLK = 128
ROPE_DIM = 16
ROPE_THETA = 500000.0
POOL_GROUP = 256
Q_W = 1024
KV_W = 128
QKV_W = Q_W + 2 * KV_W
IN_W = 4352
UP_SHARD = 1408
EPS = 1e-6
N_CHIPS = 4
N_DEV = 8

LR = 0.001
B1 = 0.9
B2 = 0.999
ADAM_EPS = 1e-08
WD = 0.01
STEP = 10

BF = jnp.bfloat16
F32 = jnp.float32
MESH = pl.DeviceIdType.MESH
VMEM_LIMIT_MB = 56


def _cp(sem, vmem_mb=VMEM_LIMIT_MB):
    return pltpu.CompilerParams(dimension_semantics=sem, vmem_limit_bytes=vmem_mb << 20)


def _full(shape):
    nd = len(shape)
    return pl.BlockSpec(shape, lambda *_: (0,) * nd)


def _sds(shape, dtype):
    return jax.ShapeDtypeStruct(shape, dtype)


def _after(body, n_in, deps):
    nd = len(deps)
    if nd == 0:
        return body

    def ordered(*refs):
        return body(*refs[:n_in], *refs[n_in + nd:])

    return ordered


def _any_specs(deps):
    return [pl.BlockSpec(memory_space=pl.ANY)] * len(deps)


def _nt(a, b):
    return lax.dot_general(a, b, (((1,), (1,)), ((), ())), preferred_element_type=F32)


def _tn(a, b):
    return lax.dot_general(a, b, (((0,), (0,)), ((), ())), preferred_element_type=F32)


def _mm(a, b):
    return jnp.dot(a, b, preferred_element_type=F32)


def _head_sum(v, bd):
    return _mm(v.astype(BF), bd)


def _rope_tables(pos_ref, invf_ref):
    ang = pos_ref[...].astype(F32) * invf_ref[...]
    cos = jnp.cos(ang)
    sin = jnp.sin(ang)
    lane = lax.broadcasted_iota(jnp.int32, (1, 2 * HEAD), 1) % HEAD
    sa = jnp.where(lane < ROPE_DIM // 2, -sin, 0.0)
    sb = jnp.where(lane < ROPE_DIM // 2, 0.0, jnp.where(lane < ROPE_DIM, sin, 0.0))
    return cos, sa, sb


def _tile_lanes(t, reps):
    return t if reps == 1 else jnp.tile(t, (1, reps))


def _rope(v, cos, sa, sb):
    w = v.shape[1]
    reps = w // (2 * HEAD)
    half = ROPE_DIM // 2
    return (v * _tile_lanes(cos, reps) + pltpu.roll(v, w - half, 1) * _tile_lanes(sa, reps)
            + pltpu.roll(v, half, 1) * _tile_lanes(sb, reps))


def _rope_t(dy, cos, sa, sb):
    w = dy.shape[1]
    reps = w // (2 * HEAD)
    half = ROPE_DIM // 2
    return (dy * _tile_lanes(cos, reps) + pltpu.roll(dy * _tile_lanes(sa, reps), half, 1)
            + pltpu.roll(dy * _tile_lanes(sb, reps), w - half, 1))


def _attn_in_proj(x, g1, w_in, b_gate, tm, deps=()):
    s = x.shape[0]

    def body(x_ref, g_ref, w_ref, b_ref, h_ref, u_ref, qkv_ref, gate_ref):
        xv = x_ref[...]
        r = lax.rsqrt(jnp.mean(xv * xv, axis=-1, keepdims=True) + EPS)
        h = (xv * r * g_ref[...]).astype(BF)
        h_ref[...] = h
        u_ref[...] = _nt(h, w_ref[0:D, :])
        qkv_ref[...] = _nt(h, w_ref[D:D + QKV_W, :])
        gate_ref[...] = jax.nn.sigmoid(_nt(h, w_ref[D + QKV_W:IN_W, :]) + b_ref[...]).astype(BF)

    row = lambda w: pl.BlockSpec((tm, w), lambda i: (i, 0))
    return pl.pallas_call(
        _after(body, 4, deps), grid=(s // tm,),
        in_specs=[row(D), _full((1, D)), _full((IN_W, D)), _full((1, 2 * D))] + _any_specs(deps),
        out_specs=[row(D), row(D), row(QKV_W), row(2 * D)],
        out_shape=[_sds((s, D), BF), _sds((s, D), F32), _sds((s, QKV_W), F32), _sds((s, 2 * D), BF)],
        compiler_params=_cp(("parallel",)), name="attn_in_proj")(x, g1, w_in, b_gate, *deps)


def _qk_prep(qkv, pos, wq, wk, invf, bd, tm):
    s = qkv.shape[0]

    def body(qkv_ref, pos_ref, wq_ref, wk_ref, invf_ref, bd_ref, qh_ref, kh_ref, vh_ref):
        cos, sa, sb = _rope_tables(pos_ref, invf_ref)
        q = qkv_ref[:, 0:Q_W]
        k = qkv_ref[:, Q_W:Q_W + KV_W]
        v = qkv_ref[:, Q_W + KV_W:QKV_W]
        rq = lax.rsqrt(_head_sum(q * q, bd_ref[...]) * (1.0 / HEAD) + EPS)
        qr = _rope(q * rq * wq_ref[...], cos, sa, sb) * (HEAD ** -0.5)
        rk = lax.rsqrt(_head_sum(k * k, bd_ref[0:KV_W, 0:KV_W]) * (1.0 / HEAD) + EPS)
        kr = _rope(k * rk * wk_ref[...], cos, sa, sb)
        for h in range(N_Q):
            qh_ref[h] = qr[:, HEAD * h:HEAD * (h + 1)].astype(BF)
        for h in range(N_KV):
            kh_ref[h] = kr[:, HEAD * h:HEAD * (h + 1)].astype(BF)
            vh_ref[h] = v[:, HEAD * h:HEAD * (h + 1)].astype(BF)

    heads = lambda n: pl.BlockSpec((n, tm, HEAD), lambda i: (0, i, 0))
    return pl.pallas_call(
        body, grid=(s // tm,),
        in_specs=[pl.BlockSpec((tm, QKV_W), lambda i: (i, 0)), pl.BlockSpec((tm, 1), lambda i: (i, 0)),
                  _full((1, Q_W)), _full((1, KV_W)), _full((1, 2 * HEAD)), _full((Q_W, Q_W))],
        out_specs=[heads(N_Q), heads(N_KV), heads(N_KV)],
        out_shape=[_sds((N_Q, s, HEAD), BF), _sds((N_KV, s, HEAD), BF), _sds((N_KV, s, HEAD), BF)],
        compiler_params=_cp(("parallel",)), name="qk_prep")(qkv, pos, wq, wk, invf, bd)


def _sink_column(sink_ref, kh):
    row_g = lax.broadcasted_iota(jnp.int32, (GQA * BLK, 1), 0) // BLK
    col = jnp.zeros((GQA * BLK, 1), F32)
    for g in range(GQA):
        col = jnp.where(row_g == g, sink_ref[kh * GQA + g], col)
    return col


def _fold_band(band, lower, first=None):
    prev, cur = band[:, 0:BLK], band[:, BLK:2 * BLK]
    if first is not None:
        prev = jnp.where(first, -jnp.inf, prev)
    return jnp.where(lower, cur, prev)


def _unfold_band(x, lower):
    return jnp.concatenate([jnp.where(lower, 0.0, x), jnp.where(lower, x, 0.0)], axis=1)


def _attn_probs(q, k, n, sink_col):
    sc = _nt(q, k)
    qi = lax.broadcasted_iota(jnp.int32, (sc.shape[0], BLK), 0) % BLK
    ki = lax.broadcasted_iota(jnp.int32, (sc.shape[0], BLK), 1)
    lower = ki <= qi
    sc = _fold_band(sc, lower, first=n == 0)
    m = jnp.maximum(jnp.max(sc, axis=-1, keepdims=True), sink_col)
    p = jnp.exp(sc - m)
    es = jnp.exp(sink_col - m)
    inv = 1.0 / (jnp.sum(p, axis=-1, keepdims=True) + es)
    return p * inv, es * inv, lower


def _attn_fwd(qh, kh, vh, sinks, deps=()):
    s = qh.shape[1]
    nb = s // BLK

    def body(sink_ref, q_ref, kp_ref, kc_ref, vp_ref, vc_ref, o_ref):
        n = pl.program_id(0)
        for khd in range(N_KV):
            q = q_ref[khd * GQA:(khd + 1) * GQA].reshape(GQA * BLK, HEAD)
            k = jnp.concatenate([kp_ref[khd], kc_ref[khd]], axis=0)
            v = jnp.concatenate([vp_ref[khd], vc_ref[khd]], axis=0)
            probs, _, lower = _attn_probs(q, k, n, _sink_column(sink_ref, khd))
            o = _mm(_unfold_band(probs, lower).astype(BF), v)
            for j in range(GQA // 2):
                c0 = khd * GQA * HEAD + 2 * HEAD * j
                o_ref[:, c0:c0 + 2 * HEAD] = jnp.concatenate(
                    [o[2 * j * BLK:(2 * j + 1) * BLK], o[(2 * j + 1) * BLK:(2 * j + 2) * BLK]], axis=1).astype(BF)

    prev = pl.BlockSpec((N_KV, BLK, HEAD), lambda n: (0, jnp.maximum(n - 1, 0), 0))
    cur = pl.BlockSpec((N_KV, BLK, HEAD), lambda n: (0, n, 0))
    return pl.pallas_call(
        _after(body, 6, deps), grid=(nb,),
        in_specs=[pl.BlockSpec(memory_space=pltpu.SMEM),
                  pl.BlockSpec((N_Q, BLK, HEAD), lambda n: (0, n, 0)), prev, cur, prev, cur] + _any_specs(deps),
        out_specs=pl.BlockSpec((BLK, Q_W), lambda n: (n, 0)),
        out_shape=_sds((s, Q_W), BF),
        compiler_params=_cp(("parallel",)), name="attn_fwd")(sinks, qh, kh, kh, vh, vh, *deps)


def _pool_fwd(u, w_pool, pool_scale, ts, deps=()):
    s = u.shape[0]
    halo = 16

    def body(u_ref, wp_ref, ps_ref, a_ref, pooled_ref, prev):
        g = pl.program_id(0)
        i = pl.program_id(1)

        @pl.when(i == 0)
        def _():
            prev[...] = jnp.zeros_like(prev)

        cur = u_ref[...]
        ext = jnp.concatenate([prev[...], cur], axis=0)
        t = (i * ts + lax.broadcasted_iota(jnp.int32, (ts, 1), 0)).astype(F32)
        for gi in range(4):
            @pl.when(g == gi)
            def _(gi=gi):
                w = 2 << gi
                acc, span = ext, 1
                while span < w:
                    acc = acc + pltpu.roll(acc, span, 0)
                    span *= 2
                inv = 1.0 / jnp.minimum(t + 1.0, float(w))
                pooled = (acc[halo:halo + ts] * inv - cur).astype(BF)
                pooled_ref[...] = pooled
                a_ref[...] = (_mm(pooled, wp_ref[0]) * ps_ref[...]).astype(BF)

        prev[...] = cur[ts - halo:ts]

    col = pl.BlockSpec((ts, POOL_GROUP), lambda g, i: (i, g))
    return pl.pallas_call(
        _after(body, 3, deps), grid=(4, s // ts),
        in_specs=[col, pl.BlockSpec((1, POOL_GROUP, POOL_GROUP), lambda g, i: (g, 0, 0)),
                  pl.BlockSpec((1, POOL_GROUP), lambda g, i: (0, g))] + _any_specs(deps),
        out_specs=[col, col],
        out_shape=[_sds((s, D), BF), _sds((s, D), BF)],
        scratch_shapes=[pltpu.VMEM((halo, POOL_GROUP), F32)],
        compiler_params=_cp(("parallel", "arbitrary")), name="pool_fwd")(u, w_pool, pool_scale, *deps)


def _mix_out_proj(a, b, gates, x, w_out, g2, tm):
    s = x.shape[0]

    def body(a_ref, b_ref, gate_ref, x_ref, w_ref, g_ref, mix_ref, y_ref, h_ref):
        mix = (gate_ref[:, 0:D].astype(F32) * a_ref[...].astype(F32)
               + gate_ref[:, D:2 * D].astype(F32) * b_ref[...].astype(F32)).astype(BF)
        mix_ref[...] = mix
        y = x_ref[...] + _mm(mix, w_ref[...])
        y_ref[...] = y
        r = lax.rsqrt(jnp.mean(y * y, axis=-1, keepdims=True) + EPS)
        h_ref[...] = (y * r * g_ref[...]).astype(BF)

    row = lambda w: pl.BlockSpec((tm, w), lambda i: (i, 0))
    return pl.pallas_call(
        body, grid=(s // tm,),
        in_specs=[row(D), row(D), row(2 * D), row(D), _full((D, D)), _full((1, D))],
        out_specs=[row(D), row(D), row(D)],
        out_shape=[_sds((s, D), BF), _sds((s, D), F32), _sds((s, D), BF)],
        compiler_params=_cp(("parallel",)), name="mix_out_proj")(a, b, gates, x, w_out, g2)


def _ffn_up(h2, w_up, conv_w, conv_b, tm):
    s = h2.shape[0]

    def body(h_ref, wg_ref, wv_ref, cwg_ref, cwv_ref, cbg_ref, cbv_ref,
             preg_ref, prev_ref, upg_ref, upv_ref, act_ref, halog, halov):
        i = pl.program_id(1)

        @pl.when(i == 0)
        def _():
            halog[...] = jnp.zeros_like(halog)
            halov[...] = jnp.zeros_like(halov)

        h = h_ref[...]

        def conv_half(w_ref, cw_ref, cb_ref, halo, pre_ref, up_ref):
            pre = _mm(h, w_ref[0])
            pre_ref[...] = pre.astype(BF)
            ext = jnp.concatenate([halo[...], pre], axis=0)
            cw = cw_ref[0]
            up = cb_ref[...] + cw[0:1] * pltpu.roll(ext, 2, 0)[8:8 + tm]
            up = up + cw[1:2] * pltpu.roll(ext, 1, 0)[8:8 + tm]
            up = up + cw[2:3] * pre
            halo[...] = pre[tm - 8:tm]
            up_ref[...] = up.astype(BF)
            return up

        gate = conv_half(wg_ref, cwg_ref, cbg_ref, halog, preg_ref, upg_ref)
        val = conv_half(wv_ref, cwv_ref, cbv_ref, halov, prev_ref, upv_ref)
        act_ref[...] = (gate * jax.nn.sigmoid(gate) * val).astype(BF)

    tile = pl.BlockSpec((tm, UP_SHARD), lambda j, i: (i, j))
    wspec = lambda off: pl.BlockSpec((1, D, UP_SHARD), lambda j, i: (j + off, 0, 0))
    cwspec = lambda off: pl.BlockSpec((1, 3, UP_SHARD), lambda j, i: (j + off, 0, 0))
    cbspec = lambda off: pl.BlockSpec((1, UP_SHARD), lambda j, i: (0, j + off))
    half = _sds((s, D_FF), BF)
    return pl.pallas_call(
        body, grid=(2, s // tm),
        in_specs=[pl.BlockSpec((tm, D), lambda j, i: (i, 0)), wspec(0), wspec(2), cwspec(0), cwspec(2),
                  cbspec(0), cbspec(2)],
        out_specs=[tile] * 5, out_shape=[half] * 5,
        scratch_shapes=[pltpu.VMEM((8, UP_SHARD), F32), pltpu.VMEM((8, UP_SHARD), F32)],
        compiler_params=_cp(("parallel", "arbitrary")), name="ffn_up")(
            h2, w_up, w_up, conv_w, conv_w, conv_b, conv_b)


def _ffn_down_loss(act, w_down, y1, tgt, tm):
    s = y1.shape[0]

    def body(act_ref, w_ref, y_ref, t_ref, dy_ref, dyb_ref, loss_ref):
        @pl.when(pl.program_id(0) == 0)
        def _():
            loss_ref[...] = jnp.zeros_like(loss_ref)

        e = y_ref[...] + _mm(act_ref[...], w_ref[...]) - t_ref[...]
        dy = e * (1.0 / D)
        dy_ref[...] = dy
        dyb_ref[...] = dy.astype(BF)
        e2 = (e * e).reshape(tm // 8, 8, D).sum(axis=0)
        part = e2[:, 0:128]
        for j in range(1, D // 128):
            part = part + e2[:, 128 * j:128 * (j + 1)]
        loss_ref[...] += part

    row = lambda w: pl.BlockSpec((tm, w), lambda i: (i, 0))
    return pl.pallas_call(
        body, grid=(s // tm,),
        in_specs=[row(D_FF), _full((D_FF, D)), row(D), row(D)],
        out_specs=[row(D), row(D), _full((8, 128))],
        out_shape=[_sds((s, D), F32), _sds((s, D), BF), _sds((8, 128), F32)],
        compiler_params=_cp(("arbitrary",)), name="ffn_down_loss")(act, w_down, y1, tgt)


def _grad_matmul(a, b, tn, tk, name, lead=None, prev=None, lead_off=0):
    s, m = a.shape
    n = b.shape[1]
    nj = n // tn

    def body(*refs):
        a_ref, b_ref = refs[0], refs[1]
        o_ref = refs[-1]
        acc = _tn(a_ref[...], b_ref[...])
        acc = acc if lead is None else acc[None]

        @pl.when(pl.program_id(1) == 0)
        def _():
            o_ref[...] = acc

        @pl.when(pl.program_id(1) > 0)
        def _():
            o_ref[...] += acc

    in_specs = [pl.BlockSpec((tk, m), lambda j, k: (k, 0)), pl.BlockSpec((tk, tn), lambda j, k: (k, j))]
    args = [a, b]
    aliases = {}
    if lead is None:
        out_spec = pl.BlockSpec((m, tn), lambda j, k: (0, j))
        out_shape = _sds((m, n), F32)
    else:
        out_spec = pl.BlockSpec((1, m, tn), lambda j, k: (j + lead_off, 0, 0))
        out_shape = _sds((lead, m, tn), F32)
        if prev is not None:
            in_specs.append(pl.BlockSpec(memory_space=pl.ANY))
            args.append(prev)
            aliases = {2: 0}
    return pl.pallas_call(
        body, grid=(nj, s // tk), in_specs=in_specs, out_specs=out_spec, out_shape=out_shape,
        input_output_aliases=aliases,
        compiler_params=_cp(("parallel", "arbitrary")), name=name)(*args)


def _ffn_act_bwd(dyb, w_down, up_g, up_v, pre_g, pre_v, conv_w, tm):
    s = dyb.shape[0]
    nt = s // tm

    def body(dy_ref, wd_ref, upg_ref, upv_ref, preg_ref, prev_ref, cwg_ref, cwv_ref,
             dpg_ref, dpv_ref, dcwg_ref, dcwv_ref, dcbg_ref, dcbv_ref, nxg, nxv):
        i = pl.program_id(1)

        @pl.when(i == 0)
        def _():
            nxg[...] = jnp.zeros_like(nxg)
            nxv[...] = jnp.zeros_like(nxv)
            dcwg_ref[...] = jnp.zeros_like(dcwg_ref)
            dcwv_ref[...] = jnp.zeros_like(dcwv_ref)
            dcbg_ref[...] = jnp.zeros_like(dcbg_ref)
            dcbv_ref[...] = jnp.zeros_like(dcbv_ref)

        dact = _nt(dy_ref[...], wd_ref[...])
        g = upg_ref[...].astype(F32)
        v = upv_ref[...].astype(F32)
        sg = jax.nn.sigmoid(g)
        d_v = dact * (g * sg)
        d_g = dact * v * (sg * (1.0 + g * (1.0 - sg)))

        def conv_bwd(d_up, nx, pre_ref, cw_ref, dp_ref, dcw_ref, dcb_ref):
            ext = jnp.concatenate([d_up, nx[...]], axis=0)
            s1 = pltpu.roll(ext, tm + 8 - 1, 0)[0:tm]
            s2 = pltpu.roll(ext, tm + 8 - 2, 0)[0:tm]
            cw = cw_ref[0]
            dp_ref[...] = (cw[2:3] * d_up + cw[1:2] * s1 + cw[0:1] * s2).astype(BF)
            nx[...] = d_up[0:8]
            pre = pre_ref[...].astype(F32)
            dcw_ref[0, 0:1, :] += jnp.sum(s2 * pre, axis=0, keepdims=True)
            dcw_ref[0, 1:2, :] += jnp.sum(s1 * pre, axis=0, keepdims=True)
            dcw_ref[0, 2:3, :] += jnp.sum(d_up * pre, axis=0, keepdims=True)
            dcb_ref[...] += jnp.sum(d_up, axis=0, keepdims=True)

        conv_bwd(d_g, nxg, preg_ref, cwg_ref, dpg_ref, dcwg_ref, dcbg_ref)
        conv_bwd(d_v, nxv, prev_ref, cwv_ref, dpv_ref, dcwv_ref, dcbv_ref)

    tile = pl.BlockSpec((tm, UP_SHARD), lambda j, i: (nt - 1 - i, j))
    cwspec = lambda off: pl.BlockSpec((1, 3, UP_SHARD), lambda j, i: (j + off, 0, 0))
    acc_cw = pl.BlockSpec((1, 3, UP_SHARD), lambda j, i: (j, 0, 0))
    acc_cb = pl.BlockSpec((1, UP_SHARD), lambda j, i: (0, j))
    buf = pltpu.VMEM((8, UP_SHARD), F32)
    return pl.pallas_call(
        body, grid=(2, nt),
        in_specs=[pl.BlockSpec((tm, D), lambda j, i: (nt - 1 - i, 0)),
                  pl.BlockSpec((UP_SHARD, D), lambda j, i: (j, 0)),
                  tile, tile, tile, tile, cwspec(0), cwspec(2)],
        out_specs=[tile, tile, acc_cw, acc_cw, acc_cb, acc_cb],
        out_shape=[_sds((s, D_FF), BF), _sds((s, D_FF), BF), _sds((2, 3, UP_SHARD), F32),
                   _sds((2, 3, UP_SHARD), F32), _sds((1, D_FF), F32), _sds((1, D_FF), F32)],
        scratch_shapes=[buf, buf],
        compiler_params=_cp(("parallel", "arbitrary")), name="ffn_act_bwd")(
            dyb, w_down, up_g, up_v, pre_g, pre_v, conv_w, conv_w)


def _rms_bwd(dh, y, g):
    r = lax.rsqrt(jnp.mean(y * y, axis=-1, keepdims=True) + EPS)
    n = y * r
    dn = dh * g
    return r * (dn - n * jnp.mean(dn * n, axis=-1, keepdims=True)), dh * n


def _ffn_up_bwd(dp_g, dp_v, w_up, y1, dy2, g2, tm, deps=()):
    s = y1.shape[0]

    def body(dg_ref, dv_ref, w_ref, y_ref, dy2_ref, g_ref, dy1_ref, dgn_ref):
        @pl.when(pl.program_id(0) == 0)
        def _():
            dgn_ref[...] = jnp.zeros_like(dgn_ref)

        dh = _nt(dg_ref[:, 0:UP_SHARD], w_ref[0])
        dh = dh + _nt(dg_ref[:, UP_SHARD:D_FF], w_ref[1])
        dh = dh + _nt(dv_ref[:, 0:UP_SHARD], w_ref[2])
        dh = dh + _nt(dv_ref[:, UP_SHARD:D_FF], w_ref[3])
        dy, dgn = _rms_bwd(dh, y_ref[...], g_ref[...])
        dy1_ref[...] = dy2_ref[...] + dy
        dgn_ref[...] += jnp.sum(dgn, axis=0, keepdims=True)

    row = lambda w: pl.BlockSpec((tm, w), lambda i: (i, 0))
    return pl.pallas_call(
        _after(body, 6, deps), grid=(s // tm,),
        in_specs=[row(D_FF), row(D_FF), _full((4, D, UP_SHARD)), row(D), row(D), _full((1, D))] + _any_specs(deps),
        out_specs=[row(D), _full((1, D))],
        out_shape=[_sds((s, D), F32), _sds((1, D), F32)],
        compiler_params=_cp(("arbitrary",)), name="ffn_up_bwd")(dp_g, dp_v, w_up, y1, dy2, g2, *deps)


def _out_proj_bwd(dy1, w_out, a, b, gates, mix, tm, deps=()):
    s = dy1.shape[0]

    def body(dy_ref, w_ref, a_ref, b_ref, gate_ref, mix_ref, da_ref, db_ref, dzg_ref, dbg_ref, dw_ref):
        @pl.when(pl.program_id(0) == 0)
        def _():
            dbg_ref[...] = jnp.zeros_like(dbg_ref)
            dw_ref[...] = jnp.zeros_like(dw_ref)

        dyb = dy_ref[...].astype(BF)
        dmix = _nt(dyb, w_ref[...])
        gp = gate_ref[:, 0:D].astype(F32)
        ga = gate_ref[:, D:2 * D].astype(F32)
        da_ref[...] = (dmix * gp).astype(BF)
        db_ref[...] = (dmix * ga).astype(BF)
        dzp = dmix * a_ref[...].astype(F32) * (gp * (1.0 - gp))
        dza = dmix * b_ref[...].astype(F32) * (ga * (1.0 - ga))
        dzg_ref[:, 0:D] = dzp.astype(BF)
        dzg_ref[:, D:2 * D] = dza.astype(BF)
        dbg_ref[:, 0:D] += jnp.sum(dzp, axis=0, keepdims=True)
        dbg_ref[:, D:2 * D] += jnp.sum(dza, axis=0, keepdims=True)
        dw_ref[...] += _tn(mix_ref[...], dyb)

    row = lambda w: pl.BlockSpec((tm, w), lambda i: (i, 0))
    return pl.pallas_call(
        _after(body, 6, deps), grid=(s // tm,),
        in_specs=[row(D), _full((D, D)), row(D), row(D), row(2 * D), row(D)] + _any_specs(deps),
        out_specs=[row(D), row(D), row(2 * D), _full((1, 2 * D)), _full((D, D))],
        out_shape=[_sds((s, D), BF), _sds((s, D), BF), _sds((s, 2 * D), BF), _sds((1, 2 * D), F32),
                   _sds((D, D), F32)],
        compiler_params=_cp(("arbitrary",)), name="out_proj_bwd")(dy1, w_out, a, b, gates, mix, *deps)


def _pool_bwd(da, pooled, w_pool, pool_scale, h1, ts):
    s = da.shape[0]
    nt = s // ts
    halo = 16

    def body(da_ref, pooled_ref, wp_ref, ps_ref, h_ref, du_ref, dwp_ref, dps_ref, dwi_ref, nxt):
        g = pl.program_id(0)
        i = pl.program_id(1)
        ti = nt - 1 - i

        @pl.when(i == 0)
        def _():
            nxt[...] = jnp.zeros_like(nxt)
            dwp_ref[...] = jnp.zeros_like(dwp_ref)
            dps_ref[...] = jnp.zeros_like(dps_ref)
            dwi_ref[...] = jnp.zeros_like(dwi_ref)

        pooled = pooled_ref[...]
        dav = da_ref[...].astype(F32)
        dps_ref[...] += jnp.sum(dav * _mm(pooled, wp_ref[0]), axis=0, keepdims=True)
        dm = (dav * ps_ref[...]).astype(BF)
        dwp_ref[0] += _tn(pooled, dm)
        dpool = _nt(dm, wp_ref[0])
        t = (ti * ts + lax.broadcasted_iota(jnp.int32, (ts, 1), 0)).astype(F32)
        for gi in range(4):
            @pl.when(g == gi)
            def _(gi=gi):
                w = 2 << gi
                e = dpool * (1.0 / jnp.minimum(t + 1.0, float(w)))
                acc, span = jnp.concatenate([e, nxt[...]], axis=0), 1
                while span < w:
                    acc = acc + pltpu.roll(acc, ts + halo - span, 0)
                    span *= 2
                du = (acc[0:ts] - dpool).astype(BF)
                du_ref[...] = du
                dwi_ref[...] += _tn(du, h_ref[...])
                nxt[...] = e[0:halo]

    col = pl.BlockSpec((ts, POOL_GROUP), lambda g, i: (nt - 1 - i, g))
    return pl.pallas_call(
        body, grid=(4, nt),
        in_specs=[col, col, pl.BlockSpec((1, POOL_GROUP, POOL_GROUP), lambda g, i: (g, 0, 0)),
                  pl.BlockSpec((1, POOL_GROUP), lambda g, i: (0, g)),
                  pl.BlockSpec((ts, D), lambda g, i: (nt - 1 - i, 0))],
        out_specs=[col, pl.BlockSpec((1, POOL_GROUP, POOL_GROUP), lambda g, i: (g, 0, 0)),
                   pl.BlockSpec((1, POOL_GROUP), lambda g, i: (0, g)),
                   pl.BlockSpec((POOL_GROUP, D), lambda g, i: (g, 0))],
        out_shape=[_sds((s, D), BF), _sds((4, POOL_GROUP, POOL_GROUP), F32), _sds((1, D), F32), _sds((D, D), F32)],
        scratch_shapes=[pltpu.VMEM((halo, POOL_GROUP), F32)],
        compiler_params=_cp(("parallel", "arbitrary")), name="pool_bwd")(da, pooled, w_pool, pool_scale, h1)


def _attn_bwd(qh, kh, vh, sinks, db, deps=()):
    s = qh.shape[1]
    nb = s // BLK

    def body(sink_ref, q_ref, kp_ref, kc_ref, vp_ref, vc_ref, do_ref,
             dq_ref, dk_ref, dv_ref, dsink_ref, ck, cv):
        n = pl.program_id(0)

        @pl.when(n == 0)
        def _():
            ck[...] = jnp.zeros_like(ck)
            cv[...] = jnp.zeros_like(cv)
            dsink_ref[...] = jnp.zeros_like(dsink_ref)

        @pl.when(n < nb)
        def _():
            dov = do_ref[...]
            for khd in range(N_KV):
                q = q_ref[khd * GQA:(khd + 1) * GQA].reshape(GQA * BLK, HEAD)
                k = jnp.concatenate([kp_ref[khd], kc_ref[khd]], axis=0)
                v = jnp.concatenate([vp_ref[khd], vc_ref[khd]], axis=0)
                c0 = khd * GQA * HEAD
                do = jnp.concatenate([dov[:, c0 + HEAD * g:c0 + HEAD * (g + 1)] for g in range(GQA)],
                                     axis=0).astype(BF)
                probs, psink, lower = _attn_probs(q, k, n, _sink_column(sink_ref, khd))
                dp = _fold_band(_nt(do, v), lower)
                delta = jnp.sum(probs * dp, axis=-1, keepdims=True)
                ds = _unfold_band(probs * (dp - delta), lower).astype(BF)
                dq_ref[khd * GQA:(khd + 1) * GQA] = _mm(ds, k).reshape(GQA, BLK, HEAD)
                dk = _tn(ds, q)
                dv = _tn(_unfold_band(probs, lower).astype(BF), do)
                dk_ref[khd] = ck[khd] + dk[0:BLK]
                dv_ref[khd] = cv[khd] + dv[0:BLK]
                ck[khd] = dk[BLK:2 * BLK]
                cv[khd] = dv[BLK:2 * BLK]
                dsk = psink * delta
                lane = lax.broadcasted_iota(jnp.int32, (1, 128), 1)
                acc = jnp.zeros((1, 128), F32)
                for g in range(GQA):
                    acc = acc - jnp.where(lane == khd * GQA + g,
                                          jnp.sum(dsk[g * BLK:(g + 1) * BLK], axis=0, keepdims=True), 0.0)
                dsink_ref[...] += acc

        @pl.when(n == nb)
        def _():
            dk_ref[...] = ck[...]
            dv_ref[...] = cv[...]

    last = nb - 1
    prev = pl.BlockSpec((N_KV, BLK, HEAD), lambda n: (0, jnp.maximum(jnp.minimum(n, last) - 1, 0), 0))
    cur = pl.BlockSpec((N_KV, BLK, HEAD), lambda n: (0, jnp.minimum(n, last), 0))
    kv_out = pl.BlockSpec((N_KV, BLK, HEAD), lambda n: (0, jnp.maximum(n - 1, 0), 0))
    return pl.pallas_call(
        _after(body, 7, deps), grid=(nb + 1,),
        in_specs=[pl.BlockSpec(memory_space=pltpu.SMEM),
                  pl.BlockSpec((N_Q, BLK, HEAD), lambda n: (0, jnp.minimum(n, last), 0)),
                  prev, cur, prev, cur,
                  pl.BlockSpec((BLK, Q_W), lambda n: (jnp.minimum(n, last), 0))] + _any_specs(deps),
        out_specs=[pl.BlockSpec((N_Q, BLK, HEAD), lambda n: (0, jnp.minimum(n, last), 0)), kv_out, kv_out,
                   _full((1, 128))],
        out_shape=[_sds((N_Q, s, HEAD), F32), _sds((N_KV, s, HEAD), F32), _sds((N_KV, s, HEAD), F32),
                   _sds((1, 128), F32)],
        scratch_shapes=[pltpu.VMEM((N_KV, BLK, HEAD), F32), pltpu.VMEM((N_KV, BLK, HEAD), F32)],
        compiler_params=_cp(("arbitrary",)), name="attn_bwd")(sinks, qh, kh, kh, vh, vh, db, *deps)


def _qk_prep_bwd(dqh, dkh, dvh, qkv, pos, wq, wk, invf, bd, tm, deps=()):
    s = qkv.shape[0]

    def fold_heads(row):
        out = row[:, 0:HEAD]
        for h in range(1, row.shape[1] // HEAD):
            out = out + row[:, HEAD * h:HEAD * (h + 1)]
        return out

    def body(dq_ref, dk_ref, dv_ref, qkv_ref, pos_ref, wq_ref, wk_ref, invf_ref, bd_ref,
             dz_ref, dwq_ref, dwk_ref):
        @pl.when(pl.program_id(0) == 0)
        def _():
            dwq_ref[...] = jnp.zeros_like(dwq_ref)
            dwk_ref[...] = jnp.zeros_like(dwk_ref)

        cos, sa, sb = _rope_tables(pos_ref, invf_ref)

        def norm_rope_bwd(dy, xin, w, bdm):
            dn = _rope_t(dy, cos, sa, sb)
            r = lax.rsqrt(_head_sum(xin * xin, bdm) * (1.0 / HEAD) + EPS)
            nh = xin * r
            gw = dn * w
            dx = r * (gw - nh * (_head_sum(gw * nh, bdm) * (1.0 / HEAD)))
            return dx, fold_heads(jnp.sum(dn * nh, axis=0, keepdims=True))

        dq = jnp.concatenate([dq_ref[h] for h in range(N_Q)], axis=1) * (HEAD ** -0.5)
        dk = jnp.concatenate([dk_ref[h] for h in range(N_KV)], axis=1)
        dxq, dwq = norm_rope_bwd(dq, qkv_ref[:, 0:Q_W], wq_ref[...], bd_ref[...])
        dxk, dwk = norm_rope_bwd(dk, qkv_ref[:, Q_W:Q_W + KV_W], wk_ref[...], bd_ref[0:KV_W, 0:KV_W])
        dz_ref[:, 0:Q_W] = dxq.astype(BF)
        dz_ref[:, Q_W:Q_W + KV_W] = dxk.astype(BF)
        dz_ref[:, Q_W + KV_W:QKV_W] = jnp.concatenate([dv_ref[h] for h in range(N_KV)], axis=1).astype(BF)
        dwq_ref[...] += dwq
        dwk_ref[...] += dwk

    heads = lambda n: pl.BlockSpec((n, tm, HEAD), lambda i: (0, i, 0))
    return pl.pallas_call(
        _after(body, 9, deps), grid=(s // tm,),
        in_specs=[heads(N_Q), heads(N_KV), heads(N_KV), pl.BlockSpec((tm, QKV_W), lambda i: (i, 0)),
                  pl.BlockSpec((tm, 1), lambda i: (i, 0)), _full((1, Q_W)), _full((1, KV_W)),
                  _full((1, 2 * HEAD)), _full((Q_W, Q_W))] + _any_specs(deps),
        out_specs=[pl.BlockSpec((tm, QKV_W), lambda i: (i, 0)), _full((1, HEAD)), _full((1, HEAD))],
        out_shape=[_sds((s, QKV_W), BF), _sds((1, HEAD), F32), _sds((1, HEAD), F32)],
        compiler_params=_cp(("arbitrary",)), name="qk_prep_bwd")(
            dqh, dkh, dvh, qkv, pos, wq, wk, invf, bd, *deps)


def _in_proj_bwd(du, dzq, dzg, w_in, x, g1, dy1, tm, deps=()):
    s = x.shape[0]

    def body(du_ref, dzq_ref, dzg_ref, w_ref, x_ref, g_ref, dy_ref, gx_ref, dgn_ref):
        @pl.when(pl.program_id(0) == 0)
        def _():
            dgn_ref[...] = jnp.zeros_like(dgn_ref)

        dh = _mm(du_ref[...], w_ref[0:D, :])
        dh = dh + _mm(dzq_ref[...], w_ref[D:D + QKV_W, :])
        dh = dh + _mm(dzg_ref[...], w_ref[D + QKV_W:IN_W, :])
        dx, dgn = _rms_bwd(dh, x_ref[...], g_ref[...])
        gx_ref[...] = dy_ref[...] + dx
        dgn_ref[...] += jnp.sum(dgn, axis=0, keepdims=True)

    row = lambda w: pl.BlockSpec((tm, w), lambda i: (i, 0))
    return pl.pallas_call(
        _after(body, 7, deps), grid=(s // tm,),
        in_specs=[row(D), row(QKV_W), row(2 * D), _full((IN_W, D)), row(D), _full((1, D)), row(D)] + _any_specs(deps),
        out_specs=[row(D), _full((1, D))],
        out_shape=[_sds((s, D), F32), _sds((1, D), F32)],
        compiler_params=_cp(("arbitrary",)), name="in_proj_bwd")(du, dzq, dzg, w_in, x, g1, dy1, *deps)


def _adamw_step(w, g, m, v):
    mn = B1 * m + (1.0 - B1) * g
    vn = B2 * v + (1.0 - B2) * (g * g)
    m_hat = mn / (1.0 - B1 ** STEP)
    v_hat = vn / (1.0 - B2 ** STEP)
    return -LR * (m_hat / (jnp.sqrt(v_hat) + ADAM_EPS) + WD * w), mn, vn


SMALL_ROWS = 16
SMALL_COLS = D_FF
SMALL_AT = {"b_gate": (1, 2 * D), "pool_scale": (2, D), "q_norm": (3, HEAD),
            "k_norm": (4, HEAD), "sinks": (5, N_Q), "ffn_norm": (6, D)}
SMALL_LOSS_ROW = 0
SMALL_CONV_B_ROW = 7
SMALL_CONV_W_ROW = 9


def _pack_small(loss_acc, d_bgate, d_pscale, d_qn, d_kn, dsink, d_ffn_norm, dcb_g, dcb_v, dcw_g, dcw_v, dev):
    def body(k_ref, ls_ref, bg_ref, ps_ref, qn_ref, kn_ref, sk_ref, fn_ref, cbg_ref, cbv_ref, cwg_ref, cwv_ref,
             o_ref):
        o_ref[...] = jnp.zeros_like(o_ref)
        o_ref[0, SMALL_LOSS_ROW:SMALL_LOSS_ROW + 1, 0:128] = jnp.sum(ls_ref[...], axis=0, keepdims=True)
        for nm, ref in (("b_gate", bg_ref), ("pool_scale", ps_ref), ("q_norm", qn_ref),
                        ("k_norm", kn_ref), ("ffn_norm", fn_ref)):
            row, n = SMALL_AT[nm]
            o_ref[0, row:row + 1, 0:n] = ref[...]
        row, _ = SMALL_AT["sinks"]
        o_ref[0, row:row + 1, 0:128] = sk_ref[...]
        o_ref[0, SMALL_CONV_B_ROW:SMALL_CONV_B_ROW + 1, :] = cbg_ref[...]
        o_ref[0, SMALL_CONV_B_ROW + 1:SMALL_CONV_B_ROW + 2, :] = cbv_ref[...]
        for k in range(3):
            row = SMALL_CONV_W_ROW + 2 * k
            for half in range(2):
                o_ref[0, row:row + 1, half * UP_SHARD:(half + 1) * UP_SHARD] = cwg_ref[half, k:k + 1, :]
                o_ref[0, row + 1:row + 2, half * UP_SHARD:(half + 1) * UP_SHARD] = cwv_ref[half, k:k + 1, :]

    args = [loss_acc, d_bgate, d_pscale, d_qn, d_kn, dsink, d_ffn_norm, dcb_g, dcb_v, dcw_g, dcw_v]
    grid_spec = pltpu.PrefetchScalarGridSpec(
        num_scalar_prefetch=1, grid=(1,),
        in_specs=[pl.BlockSpec(a.shape, functools.partial(lambda nd, i, k: (0,) * nd, a.ndim)) for a in args],
        out_specs=pl.BlockSpec((1, SMALL_ROWS, SMALL_COLS), lambda i, k: (k[0], 0, 0)))
    return pl.pallas_call(body, grid_spec=grid_spec, out_shape=_sds((N_DEV, SMALL_ROWS, SMALL_COLS), F32),
                          name="pack_small")(dev, *args)


def _small_update(stack, attn_stack, params):
    names = list(params)

    def body(*refs):
        s_ref, a_ref = refs[0], refs[1]
        ins = refs[2:2 + 3 * len(names)]
        outs = refs[2 + 3 * len(names):]
        tot, tot_a = s_ref[0], a_ref[0]
        for d in range(1, N_DEV):
            tot = tot + s_ref[d]
            tot_a = tot_a + a_ref[d]
        for i, nm in enumerate(names):
            if nm == "attn_norm":
                g = tot_a
            elif nm == "conv_b":
                g = jnp.concatenate([tot[SMALL_CONV_B_ROW:SMALL_CONV_B_ROW + 1, :],
                                     tot[SMALL_CONV_B_ROW + 1:SMALL_CONV_B_ROW + 2, :]], axis=1)
            else:
                row, n = SMALL_AT[nm]
                g = tot[row:row + 1, 0:n]
            delta, mn, vn = _adamw_step(ins[3 * i][...], g, ins[3 * i + 1][...], ins[3 * i + 2][...])
            outs[4 * i][...] = g
            outs[4 * i + 1][...] = delta
            outs[4 * i + 2][...] = mn
            outs[4 * i + 3][...] = vn
        for k in range(3):
            row = SMALL_CONV_W_ROW + 2 * k
            outs[-2][k:k + 1, 0:D_FF] = tot[row:row + 1, :]
            outs[-2][k:k + 1, D_FF:2 * D_FF] = tot[row + 1:row + 2, :]
        outs[-1][...] = jnp.sum(tot[SMALL_LOSS_ROW:SMALL_LOSS_ROW + 1, 0:128], axis=1, keepdims=True) * (0.5 / D)

    flat = [a for nm in names for a in params[nm]]
    out_shape = ([_sds(params[nm][0].shape, F32) for nm in names for _ in range(4)]
                 + [_sds((3, 2 * D_FF), F32), _sds((1, 1), F32)])
    res = pl.pallas_call(body, out_shape=out_shape, name="small_update")(stack, attn_stack, *flat)
    return {nm: list(res[4 * i:4 * i + 4]) for i, nm in enumerate(names)}, res[-2], res[-1]


def _adamw(w, g, m, v, tr, name, deps=()):
    r, c = w.shape

    def body(w_ref, g_ref, m_ref, v_ref, go_ref, d_ref, mo_ref, vo_ref):
        gv = g_ref[...]
        go_ref[...] = gv
        d_ref[...], mo_ref[...], vo_ref[...] = _adamw_step(w_ref[...], gv, m_ref[...], v_ref[...])

    blk = pl.BlockSpec((tr, c), lambda i: (i, 0))
    return pl.pallas_call(
        _after(body, 4, deps), grid=(r // tr,), in_specs=[blk] * 4 + _any_specs(deps), out_specs=[blk] * 4,
        out_shape=[_sds((r, c), F32)] * 4, compiler_params=_cp(("parallel",)), name=name)(w, g, m, v, *deps)


def _place():
    x, y, c = lax.axis_index("x"), lax.axis_index("y"), lax.axis_index("c")
    chips = [(1 - x, y), (x, 1 - y), (1 - x, 1 - y)]
    return x, y, c, chips


def _rows(ref, lead, h, rh):
    sl = pl.ds(pl.multiple_of(h * rh, 16), rh)
    return ref.at[sl, :] if lead is None else ref.at[lead, sl, :]


def _all_gather_weights(halved, whole, placed):
    nh, nw, npl = len(halved), len(whole), len(placed)
    na = nh + nw
    nall = na + npl
    arrays = list(halved) + list(whole) + list(placed)
    out_dtypes = [BF] * nh + [a.dtype for a in whole] + [BF] * npl
    cast_rows = 128

    def body(*refs):
        ins, outs = refs[:nall], refs[nall:2 * nall]
        raw, stage = refs[2 * nall:3 * nall], refs[3 * nall:3 * nall + nh + npl]
        ici_send, ici_recv, fwd_send, fwd_recv, in_sem, loc_sem = refs[3 * nall + nh + npl:]
        x, y, c, chips = _place()
        me = 2 * x + y
        sibling = (x, y, 1 - c)
        loads = [pltpu.make_async_copy(ins[a], raw[a], in_sem.at[a]) for a in range(nall)]
        for cp in loads:
            cp.start()

        def cast(a, dst):
            r = arrays[a].shape[0]
            for r0 in range(0, r, cast_rows):
                r1 = min(r0 + cast_rows, r)
                dst[r0:r1, :] = raw[a][r0:r1, :].astype(BF)

        def ici(a, j, src_chip, src=None):
            if a < nh:
                rh = arrays[a].shape[0] // 2
                dst = _rows(outs[a], src_chip, c, rh)
                src = dst if src is None else _rows(src, None, c, rh)
            else:
                dst = outs[a].at[src_chip]
                src = dst if src is None else src
            return pltpu.make_async_remote_copy(
                src_ref=src, dst_ref=dst, send_sem=ici_send.at[3 * a + j], recv_sem=ici_recv.at[3 * a + j],
                device_id=(*chips[j], c), device_id_type=MESH)

        def fwd(a, j, half):
            rh = arrays[a].shape[0] // 2
            kj = 2 * chips[j][0] + chips[j][1]
            blk = _rows(outs[a], kj, half, rh)
            return pltpu.make_async_remote_copy(
                src_ref=blk, dst_ref=blk, send_sem=fwd_send.at[3 * a + j], recv_sem=fwd_recv.at[3 * a + j],
                device_id=sibling, device_id_type=MESH)

        local, sends = [], []
        for a in range(na):
            loads[a].wait()
            if a < nh:
                cast(a, stage[a])
                own = stage[a]
            else:
                own = raw[a]
            cp = pltpu.make_async_copy(own, outs[a].at[me], loc_sem.at[a])
            cp.start()
            local.append(cp)
            for j in range(3):
                cp = ici(a, j, me, src=own)
                cp.start()
                sends.append(cp)
        for i in range(npl):
            loads[na + i].wait()
            cast(na + i, stage[nh + i])
            cp = pltpu.make_async_copy(stage[nh + i], outs[na + i].at[me], loc_sem.at[na + i])
            cp.start()
            local.append(cp)
        passed = []
        for a in range(na):
            for j in range(3):
                kj = 2 * chips[j][0] + chips[j][1]
                ici(a, j, kj).wait_recv()
                if a < nh:
                    cp = fwd(a, j, c)
                    cp.start()
                    passed.append(cp)
        for a in range(nh):
            for j in range(3):
                fwd(a, j, 1 - c).wait_recv()
        for cp in sends + passed:
            cp.wait_send()
        for cp in local:
            cp.wait()

    any_spec = pl.BlockSpec(memory_space=pl.ANY)
    return pl.pallas_call(
        body, in_specs=[any_spec] * nall, out_specs=[any_spec] * nall,
        out_shape=[_sds((N_CHIPS,) + a.shape, dt) for a, dt in zip(arrays, out_dtypes)],
        scratch_shapes=[pltpu.VMEM(a.shape, a.dtype) for a in arrays]
        + [pltpu.VMEM(a.shape, BF) for a in list(halved) + list(placed)]
        + [pltpu.SemaphoreType.DMA((3 * na,)), pltpu.SemaphoreType.DMA((3 * na,)),
           pltpu.SemaphoreType.DMA((3 * nh,)), pltpu.SemaphoreType.DMA((3 * nh,)),
           pltpu.SemaphoreType.DMA((nall,)), pltpu.SemaphoreType.DMA((nall,))],
        compiler_params=pltpu.CompilerParams(vmem_limit_bytes=VMEM_LIMIT_MB << 20),
        name="all_gather_weights")(*arrays)


def _pair_sum(g, recv, c, tr, name):
    _, r, cols = g.shape
    rh = r // 2
    nr = rh // tr

    def body(c_ref, g_ref, r_ref, o_ref):
        o_ref[...] = (g_ref[...] + r_ref[...]).astype(BF)

    grid_spec = pltpu.PrefetchScalarGridSpec(
        num_scalar_prefetch=1, grid=(N_CHIPS, nr),
        in_specs=[pl.BlockSpec((1, tr, cols), lambda k, i, c_ref: (k, c_ref[0] * nr + i, 0)),
                  pl.BlockSpec((1, tr, cols), lambda k, i, c_ref: (k, i, 0))],
        out_specs=pl.BlockSpec((1, tr, cols), lambda k, i, c_ref: (k, i, 0)))
    return pl.pallas_call(
        body, grid_spec=grid_spec, out_shape=_sds((N_CHIPS, rh, cols), BF),
        compiler_params=_cp(("parallel", "parallel")), name=name)(c, g, recv)


def _chip_sum(g, sib, recv, place, tr, name):
    _, r, cols = g.shape
    rh = r // 2
    nr = rh // tr

    def body(p_ref, g_ref, s_ref, r0_ref, r1_ref, r2_ref, o_ref):
        own = g_ref[0] + s_ref[0]
        o_ref[...] = ((own + r0_ref[0].astype(F32)) + r1_ref[0].astype(F32)) + r2_ref[0].astype(F32)

    rspec = lambda j: pl.BlockSpec((1, tr, cols), lambda i, p: (j, i, 0))
    grid_spec = pltpu.PrefetchScalarGridSpec(
        num_scalar_prefetch=1, grid=(nr,),
        in_specs=[pl.BlockSpec((1, tr, cols), lambda i, p: (p[0], p[1] * nr + i, 0)),
                  pl.BlockSpec((1, tr, cols), lambda i, p: (p[0], i, 0)), rspec(0), rspec(1), rspec(2)],
        out_specs=pl.BlockSpec((tr, cols), lambda i, p: (p[1] * nr + i, 0)))
    return pl.pallas_call(
        body, grid_spec=grid_spec, out_shape=_sds((r, cols), F32),
        compiler_params=_cp(("parallel",)), name=name)(place, g, sib, recv, recv, recv)


_HBM = pl.BlockSpec(memory_space=pltpu.HBM)
_SEM = pl.BlockSpec(memory_space=pltpu.SEMAPHORE)
_EFFECT = pltpu.SideEffectType.DATAFLOW_SIDE_EFFECTING


def _remote(src, dst, ssem, rsem, k, device):
    return pltpu.make_async_remote_copy(src_ref=src, dst_ref=dst, send_sem=ssem.at[k], recv_sem=rsem.at[k],
                                        device_id=device, device_id_type=MESH)


def _split_start(name, bufs, plan, n):
    nb = len(bufs)

    def body(*refs):
        sends, _ = plan(refs[:nb], refs[nb], refs[nb + 1])
        for cp in sends:
            cp.start()
        refs[-1][...] = jnp.zeros_like(refs[-1])

    res = pl.pallas_call(
        body, name=name,
        out_shape=(pltpu.SemaphoreType.DMA((n,)), pltpu.SemaphoreType.DMA((n,)))
        + tuple(pltpu.HBM(b.shape, b.dtype) for b in bufs) + (_sds((8, 128), F32),),
        in_specs=[_HBM] * nb,
        out_specs=(_SEM, _SEM) + (_HBM,) * nb + (pl.BlockSpec(memory_space=pltpu.VMEM),),
        input_output_aliases={i: i + 2 for i in range(nb)},
        compiler_params=pltpu.CompilerParams(has_side_effects=_EFFECT),
    )(*[pltpu.with_memory_space_constraint(b, pltpu.HBM) for b in bufs])
    return res[0], res[1], list(res[2:2 + nb]), res[2 + nb]


def _split_wait(name, send_sem, recv_sem, bufs, plan, after):
    nb = len(bufs)

    def body(*refs):
        sends, arrivals = plan(refs[:nb], refs[nb], refs[nb + 1])
        for cp in sends:
            cp.wait_send()
        for cp in arrivals:
            cp.wait_recv()

    res = pl.pallas_call(
        body, name=name, out_shape=tuple(pltpu.HBM(b.shape, b.dtype) for b in bufs),
        in_specs=[_HBM] * nb + [_SEM, _SEM, pl.BlockSpec(memory_space=pl.ANY)],
        out_specs=(_HBM,) * nb, input_output_aliases={i: i for i in range(nb)},
        compiler_params=pltpu.CompilerParams(has_side_effects=_EFFECT),
    )(*bufs, send_sem, recv_sem, after)
    return list(res)


def _plan_sibling_halves(shapes):
    na = len(shapes)

    def plan(refs, ssem, rsem):
        x, y, c, _ = _place()
        cps = []
        for a in range(na):
            rh = shapes[a][1] // 2
            src = refs[a].at[:, pl.ds(pl.multiple_of((1 - c) * rh, 8), rh), :]
            cps.append(_remote(src, refs[na + a], ssem, rsem, a, (x, y, 1 - c)))
        return cps, cps

    return plan


def _to_all(ref, ssem, rsem, base):
    x, y, c, _ = _place()
    mine = ref.at[4 * x + 2 * y + c]
    return [_remote(mine, mine, ssem, rsem, base + r - 1, (x ^ (r >> 2), y ^ ((r >> 1) & 1), c ^ (r & 1)))
            for r in range(1, N_DEV)]


def _plan_chip_exchange(na, with_small):
    def plan(refs, ssem, rsem):
        _, _, c, chips = _place()
        cps = []
        for a in range(na):
            for j in range(3):
                kj = 2 * chips[j][0] + chips[j][1]
                cps.append(_remote(refs[a].at[kj], refs[na + a].at[j], ssem, rsem, 3 * a + j, (*chips[j], c)))
        if with_small:
            cps += _to_all(refs[2 * na], ssem, rsem, 3 * na)
        return cps, cps

    return plan


def _plan_sibling_swap(shapes, with_small):
    def plan(refs, ssem, rsem):
        x, y, c, _ = _place()
        sends, arrivals = [], []
        for a, shp in enumerate(shapes):
            rh = shp[0] // 2
            mine, other = _rows(refs[a], None, c, rh), _rows(refs[a], None, 1 - c, rh)
            sends.append(_remote(mine, mine, ssem, rsem, a, (x, y, 1 - c)))
            arrivals.append(_remote(mine, other, ssem, rsem, a, (x, y, 1 - c)))
        if with_small:
            cps = _to_all(refs[len(shapes)], ssem, rsem, len(shapes))
            sends += cps
            arrivals += cps
        return sends, arrivals

    return plan


def _plan_gather_chips(shapes):
    def plan(refs, ssem, rsem):
        x, y, c, chips = _place()
        me = 2 * x + y
        sends, arrivals = [], []
        for a, shp in enumerate(shapes):
            rh = shp[1] // 2
            mine = _rows(refs[a], me, c, rh)
            for j in range(3):
                land = _rows(refs[a], 2 * chips[j][0] + chips[j][1], c, rh)
                sends.append(_remote(mine, mine, ssem, rsem, 3 * a + j, (*chips[j], c)))
                arrivals.append(_remote(land, land, ssem, rsem, 3 * a + j, (*chips[j], c)))
        return sends, arrivals

    return plan


def _plan_gather_sibling(shapes):
    def plan(refs, ssem, rsem):
        x, y, c, chips = _place()
        sends, arrivals = [], []
        for a, shp in enumerate(shapes):
            rh = shp[1] // 2
            for j in range(3):
                kj = 2 * chips[j][0] + chips[j][1]
                got, land = _rows(refs[a], kj, c, rh), _rows(refs[a], kj, 1 - c, rh)
                sends.append(_remote(got, got, ssem, rsem, 3 * a + j, (x, y, 1 - c)))
                arrivals.append(_remote(got, land, ssem, rsem, 3 * a + j, (x, y, 1 - c)))
        return sends, arrivals

    return plan


def _into_slice(w, k, n, tr, dtype, name, deps=()):
    r, cols = w.shape

    def body(k_ref, w_ref, o_ref):
        o_ref[0] = w_ref[...].astype(dtype)

    grid_spec = pltpu.PrefetchScalarGridSpec(
        num_scalar_prefetch=1, grid=(r // tr,),
        in_specs=[pl.BlockSpec((tr, cols), lambda i, k: (i, 0))] + _any_specs(deps),
        out_specs=pl.BlockSpec((1, tr, cols), lambda i, k: (k[0], i, 0)))
    return pl.pallas_call(_after(body, 2, deps), grid_spec=grid_spec, out_shape=_sds((n, r, cols), dtype),
                          compiler_params=_cp(("parallel",)), name=name)(k, w, *deps)


class _LateWeights:
    def __init__(self, bufs):
        self.n = 3 * len(bufs)
        self.chips, self.sibling = _plan_gather_chips([b.shape for b in bufs]), _plan_gather_sibling([b.shape for b in bufs])
        self.ssem, self.rsem, self.bufs, token = _split_start("gather_chips_start", bufs, self.chips, self.n)
        self.first = (token,)

    def middle(self, after):
        bufs = _split_wait("gather_chips_wait", self.ssem, self.rsem, self.bufs, self.chips, after)
        self.ssem, self.rsem, self.bufs, token = _split_start("gather_sibling_start", bufs, self.sibling, self.n)
        return (token,)

    def last(self, after):
        return _split_wait("gather_sibling_wait", self.ssem, self.rsem, self.bufs, self.sibling, after)


class _GradReduce:
    def __init__(self, tag, place, names, tiles):
        self.tag, self.place, self.names, self.tiles = tag, place, names, tiles
        self.small_all = None

    def first(self, grads):
        self.na = len(grads)
        self.p1 = _plan_sibling_halves([g.shape for g in grads])
        lands = [lax.empty((N_CHIPS, g.shape[1] // 2, g.shape[2]), F32) for g in grads]
        self.ssem, self.rsem, self.bufs, token = _split_start(
            self.tag + "_halves_start", list(grads) + lands, self.p1, self.na)
        return (token,)

    def second(self, after, small=None):
        bufs = _split_wait(self.tag + "_halves_wait", self.ssem, self.rsem, self.bufs, self.p1, after)
        self.grads, self.sib = bufs[:self.na], bufs[self.na:]
        halves = [_pair_sum(g, r, self.place[1:2], t, "pair_sum_" + nm)
                  for g, r, t, nm in zip(self.grads, self.sib, self.tiles, self.names)]
        lands = [lax.empty((3,) + h.shape[1:], h.dtype) for h in halves]
        extra = [] if small is None else [small]
        self.p2 = _plan_chip_exchange(self.na, small is not None)
        self.ssem, self.rsem, self.bufs, token = _split_start(
            self.tag + "_chips_start", halves + lands + extra, self.p2, 3 * self.na + (N_DEV - 1) * len(extra))
        return (token,)

    def third(self, after, small=None):
        bufs = _split_wait(self.tag + "_chips_wait", self.ssem, self.rsem, self.bufs, self.p2, after)
        if len(bufs) > 2 * self.na:
            self.small_chips = bufs[2 * self.na]
        mine = [_chip_sum(g, sb, r, self.place, t, "chip_sum_" + nm)
                for g, sb, r, t, nm in zip(self.grads, self.sib, bufs[self.na:2 * self.na], self.tiles, self.names)]
        extra = [] if small is None else [small]
        self.p3 = _plan_sibling_swap([m.shape for m in mine], small is not None)
        self.ssem, self.rsem, self.bufs, token = _split_start(
            self.tag + "_swap_start", mine + extra, self.p3, self.na + (N_DEV - 1) * len(extra))
        return (token,)

    def last(self, after):
        bufs = _split_wait(self.tag + "_swap_wait", self.ssem, self.rsem, self.bufs, self.p3, after)
        if len(bufs) > self.na:
            self.small_swap = bufs[self.na]
        return bufs[:self.na]


class _WeightsAtHand:
    def __init__(self, wup, wout, wdown):
        self.first, self.weights = (), [wup, wout, wdown]

    def middle(self, after):
        return ()

    def last(self, after):
        return self.weights


class _GradsKept:
    def first(self, grads):
        self.grads = list(grads)
        return ()

    def second(self, after, small=None):
        return ()

    def third(self, after, small=None):
        return ()

    def last(self, after):
        return self.grads


def _forward_backward(xs, pos, tgt, win, wpool, cw, attn_norm, b_gate, pool_scale, q_norm, k_norm, sinks,
                      ffn_norm, conv_b, late, early, rest, dev):
    s = xs.shape[0]
    tm = min(512, s)
    tk = min(1024, s)
    inv_freq = ROPE_THETA ** (-jnp.arange(0, ROPE_DIM, 2, dtype=F32) / ROPE_DIM)
    lane = jnp.arange(2 * HEAD) % HEAD
    invf = jnp.where(lane < ROPE_DIM, inv_freq[lane % (ROPE_DIM // 2)], 0.0).reshape(1, 2 * HEAD)
    wq = jnp.tile(q_norm, (1, N_Q))
    wk = jnp.tile(k_norm, (1, N_KV))
    head_of = jnp.arange(Q_W) // HEAD
    bd = (head_of[:, None] == head_of[None, :]).astype(BF)
    sink = sinks[0]

    h1, u, qkv, gates = _attn_in_proj(xs, attn_norm, win, b_gate, tm, deps=late.first)
    qh, kh, vh = _qk_prep(qkv, pos, wq, wk, invf, bd, tm)
    apool, pooled = _pool_fwd(u, wpool, pool_scale, min(512, s), deps=(qh,))
    battn = _attn_fwd(qh, kh, vh, sink, deps=late.middle(apool))
    wup, wout, wdown = late.last(battn)
    wout = wout.reshape(D, D)
    wdown = wdown.reshape(D_FF, D)
    mix, y1, h2 = _mix_out_proj(apool, battn, gates, xs, wout, ffn_norm, tm)
    pre_g, pre_v, up_g, up_v, act = _ffn_up(h2, wup, cw, conv_b, tm)
    dy2, dy2b, loss_acc = _ffn_down_loss(act, wdown, y1, tgt, tm)

    d_wdown = _grad_matmul(act, dy2b, 512, tk, "grad_w_down")
    dp_g, dp_v, dcw_g, dcw_v, dcb_g, dcb_v = _ffn_act_bwd(dy2b, wdown, up_g, up_v, pre_g, pre_v, cw, tm)
    d_wup = _grad_matmul(h2, dp_g, UP_SHARD, tk, "grad_w_up_gate", lead=N_CHIPS)
    d_wup = _grad_matmul(h2, dp_v, UP_SHARD, tk, "grad_w_up_value", lead=N_CHIPS, prev=d_wup, lead_off=2)
    token = early.first([d_wdown.reshape(N_CHIPS, D_FF // N_CHIPS, D), d_wup])
    dy1, d_ffn_norm = _ffn_up_bwd(dp_g, dp_v, wup, y1, dy2, ffn_norm, tm, deps=token)
    token = early.second(dy1)
    da, db, dzg, d_bgate, d_wout = _out_proj_bwd(dy1, wout, apool, battn, gates, mix, tm, deps=token)
    du, d_wpool, d_pscale, d_win_pool = _pool_bwd(da, pooled, wpool, pool_scale, h1, min(512, s))
    dqh, dkh, dvh, dsink = _attn_bwd(qh, kh, vh, sink, db)
    token = early.third(dqh)
    dzq, d_qn, d_kn = _qk_prep_bwd(dqh, dkh, dvh, qkv, pos, wq, wk, invf, bd, tm, deps=token)
    d_win_t = jnp.concatenate([
        d_win_pool,
        _grad_matmul(dzq, h1, D, tk, "grad_w_in_qkv"),
        _grad_matmul(dzg, h1, D, tk, "grad_w_in_gates")], axis=0)
    token = rest.first([
        d_win_t.reshape(N_CHIPS, IN_W // N_CHIPS, D),
        d_wout.reshape(N_CHIPS, D // N_CHIPS, D),
        d_wpool.reshape(4, N_CHIPS, 64, POOL_GROUP).transpose(1, 0, 2, 3).reshape(N_CHIPS, 4 * 64, POOL_GROUP)])
    small = _pack_small(loss_acc, d_bgate, d_pscale, d_qn, d_kn, dsink, d_ffn_norm, dcb_g, dcb_v, dcw_g, dcw_v, dev)
    token = rest.second(token[0] if token else None, small=small)
    grad_x, d_attn_norm = _in_proj_bwd(du, dzq, dzg, win, xs, attn_norm, dy1, tm, deps=token)
    return grad_x, d_attn_norm, small


def kernel(x, positions, attn_norm, w_in, b_gate, w_pool, pool_scale, q_norm, k_norm, sinks, w_out, ffn_norm, w_up, conv_w, conv_b, w_down, loss_target, m_attn_norm, m_w_in, m_b_gate, m_w_pool, m_pool_scale, m_q_norm, m_k_norm, m_sinks, m_w_out, m_ffn_norm, m_w_up, m_conv_w, m_conv_b, m_w_down, v_attn_norm, v_w_in, v_b_gate, v_w_pool, v_pool_scale, v_q_norm, v_k_norm, v_sinks, v_w_out, v_ffn_norm, v_w_up, v_conv_w, v_conv_b, v_w_down):
    s = x.shape[1]
    xs = x[0]
    tgt = loss_target[0]
    pos = positions[0].reshape(s, 1)
    cx, cy, cc = lax.axis_index("x"), lax.axis_index("y"), lax.axis_index("c")
    chip = 2 * cx + cy

    dev_arr = (2 * chip + cc).reshape(1).astype(jnp.int32)
    place = jnp.stack([chip, cc]).astype(jnp.int32)

    g_in, g_pool, g_cw, *own_late = _all_gather_weights(
        [jnp.swapaxes(w_in[0], 0, 1), w_pool[0].reshape(4 * 64, POOL_GROUP)], [conv_w[0]],
        [w_up[0], w_out[0], w_down[0]])
    win = g_in.reshape(IN_W, D)
    wpool = g_pool.reshape(N_CHIPS, 4, 64, POOL_GROUP).transpose(1, 0, 2, 3).reshape(4, POOL_GROUP, POOL_GROUP)
    late = _LateWeights(own_late)
    early = _GradReduce("early", place, ["w_down", "w_up"], [176, 256])
    rest = _GradReduce("rest", place, ["w_in", "w_out", "w_pool"], [272, 128, 128])

    grad_x, d_attn_norm, _ = _forward_backward(
        xs, pos, tgt, win, wpool, g_cw, attn_norm, b_gate, pool_scale, q_norm, k_norm, sinks, ffn_norm, conv_b,
        late, early, rest, dev_arr)

    def two_d(a):
        return a.reshape(-1, a.shape[-1])

    def update(nm, w, g, m, v, tr, deps=()):
        res = _adamw(two_d(w), g, two_d(m), two_d(v), tr, "adamw_" + nm, deps=deps)
        return [r.reshape(w.shape) for r in res]

    attn_stack = _into_slice(d_attn_norm, dev_arr, N_DEV, 1, F32, "own_attn_norm")
    g_wdown, g_wup = early.last(grad_x)
    big_out = {"w_up": update("w_up", w_up, g_wup, m_w_up, v_w_up, 256)}
    big_out["w_down"] = update("w_down", w_down, g_wdown, m_w_down, v_w_down, 176, deps=(big_out["w_up"][1],))
    token = rest.third(big_out["w_down"][1], small=attn_stack)
    g_win_t, g_wout, g_wpool = rest.last(token[0])
    small_out, g_convw_all, loss = _small_update(rest.small_chips, rest.small_swap, {
        "attn_norm": (attn_norm, m_attn_norm, v_attn_norm), "b_gate": (b_gate, m_b_gate, v_b_gate),
        "pool_scale": (pool_scale, m_pool_scale, v_pool_scale), "q_norm": (q_norm, m_q_norm, v_q_norm),
        "k_norm": (k_norm, m_k_norm, v_k_norm), "sinks": (sinks, m_sinks, v_sinks),
        "ffn_norm": (ffn_norm, m_ffn_norm, v_ffn_norm), "conv_b": (conv_b, m_conv_b, v_conv_b)})
    g_convw = lax.dynamic_slice_in_dim(g_convw_all, chip * UP_SHARD, UP_SHARD, axis=1)
    small_out["conv_w"] = update("conv_w", conv_w, g_convw, m_conv_w, v_conv_w, 3)
    flip = lambda a: jnp.swapaxes(a[0], 0, 1)
    res = _adamw(flip(w_in), g_win_t, flip(m_w_in), flip(v_w_in), 272, "adamw_w_in")
    big_out["w_in"] = [jnp.swapaxes(r, 0, 1)[None] for r in res]
    big_out["w_out"] = update("w_out", w_out, g_wout, m_w_out, v_w_out, 128)
    big_out["w_pool"] = update("w_pool", w_pool, g_wpool, m_w_pool, v_w_pool, 128)

    order = ["attn_norm", "w_in", "b_gate", "w_pool", "pool_scale", "q_norm", "k_norm", "sinks", "w_out",
             "ffn_norm", "w_up", "conv_w", "conv_b", "w_down"]
    allout = {**big_out, **small_out}
    outs = [loss.reshape(()), grad_x[None]]
    for k in range(4):
        outs += [allout[nm][k] for nm in order]
    return tuple(outs)
```

```python
import functools

import jax
import jax.numpy as jnp
from jax import lax
from jax.experimental import pallas as pl
from jax.experimental.pallas import tpu as pltpu

D = 1024
D_FF = 2816
HEAD = 64
N_Q = 16
N_KV = 2
GQA = 8
BLK = 128
ROPE_DIM = 16
ROPE_THETA = 500000.0
POOL_GROUP = 256
Q_W = 1024
KV_W = 128
QKV_W = Q_W + 2 * KV_W
IN_W = 4352
UP_SHARD = 1408
EPS = 1e-6
N_CHIPS = 4
N_DEV = 8

LR = 0.001
B1 = 0.9
B2 = 0.999
ADAM_EPS = 1e-08
WD = 0.01
STEP = 10

BF = jnp.bfloat16
F32 = jnp.float32
MESH = pl.DeviceIdType.MESH
VMEM_LIMIT_MB = 56


def _cp(sem, vmem_mb=VMEM_LIMIT_MB):
    return pltpu.CompilerParams(dimension_semantics=sem, vmem_limit_bytes=vmem_mb << 20)


def _full(shape):
    nd = len(shape)
    return pl.BlockSpec(shape, lambda *_: (0,) * nd)


def _sds(shape, dtype):
    return jax.ShapeDtypeStruct(shape, dtype)


def _after(body, n_in, deps):
    nd = len(deps)
    if nd == 0:
        return body

    def ordered(*refs):
        return body(*refs[:n_in], *refs[n_in + nd:])

    return ordered


def _any_specs(deps):
    return [pl.BlockSpec(memory_space=pl.ANY)] * len(deps)


def _nt(a, b):
    return lax.dot_general(a, b, (((1,), (1,)), ((), ())), preferred_element_type=F32)


def _tn(a, b):
    return lax.dot_general(a, b, (((0,), (0,)), ((), ())), preferred_element_type=F32)


def _mm(a, b):
    return jnp.dot(a, b, preferred_element_type=F32)


def _head_sum(v, bd):
    return _mm(v.astype(BF), bd)


def _rope_tables(pos_ref, invf_ref):
    ang = pos_ref[...].astype(F32) * invf_ref[...]
    cos = jnp.cos(ang)
    sin = jnp.sin(ang)
    lane = lax.broadcasted_iota(jnp.int32, (1, 2 * HEAD), 1) % HEAD
    sa = jnp.where(lane < ROPE_DIM // 2, -sin, 0.0)
    sb = jnp.where(lane < ROPE_DIM // 2, 0.0, jnp.where(lane < ROPE_DIM, sin, 0.0))
    return cos, sa, sb


def _tile_lanes(t, reps):
    return t if reps == 1 else jnp.tile(t, (1, reps))


def _rope(v, cos, sa, sb):
    w = v.shape[1]
    reps = w // (2 * HEAD)
    half = ROPE_DIM // 2
    return (v * _tile_lanes(cos, reps) + pltpu.roll(v, w - half, 1) * _tile_lanes(sa, reps)
            + pltpu.roll(v, half, 1) * _tile_lanes(sb, reps))


def _rope_t(dy, cos, sa, sb):
    w = dy.shape[1]
    reps = w // (2 * HEAD)
    half = ROPE_DIM // 2
    return (dy * _tile_lanes(cos, reps) + pltpu.roll(dy * _tile_lanes(sa, reps), half, 1)
            + pltpu.roll(dy * _tile_lanes(sb, reps), w - half, 1))


def _attn_in_proj(x, g1, w_in, b_gate, tm, deps=()):
    s = x.shape[0]

    def body(x_ref, g_ref, w_ref, b_ref, h_ref, u_ref, qkv_ref, gate_ref):
        xv = x_ref[...]
        r = lax.rsqrt(jnp.mean(xv * xv, axis=-1, keepdims=True) + EPS)
        h = (xv * r * g_ref[...]).astype(BF)
        h_ref[...] = h
        u_ref[...] = _nt(h, w_ref[0:D, :])
        qkv_ref[...] = _nt(h, w_ref[D:D + QKV_W, :])
        gate_ref[...] = jax.nn.sigmoid(_nt(h, w_ref[D + QKV_W:IN_W, :]) + b_ref[...]).astype(BF)

    row = lambda w: pl.BlockSpec((tm, w), lambda i: (i, 0))
    return pl.pallas_call(
        _after(body, 4, deps), grid=(s // tm,),
        in_specs=[row(D), _full((1, D)), _full((IN_W, D)), _full((1, 2 * D))] + _any_specs(deps),
        out_specs=[row(D), row(D), row(QKV_W), row(2 * D)],
        out_shape=[_sds((s, D), BF), _sds((s, D), F32), _sds((s, QKV_W), F32), _sds((s, 2 * D), BF)],
        compiler_params=_cp(("parallel",)), name="attn_in_proj")(x, g1, w_in, b_gate, *deps)


def _qk_prep(qkv, pos, wq, wk, invf, bd, tm):
    s = qkv.shape[0]

    def body(qkv_ref, pos_ref, wq_ref, wk_ref, invf_ref, bd_ref, qh_ref, kh_ref, vh_ref):
        cos, sa, sb = _rope_tables(pos_ref, invf_ref)
        q = qkv_ref[:, 0:Q_W]
        k = qkv_ref[:, Q_W:Q_W + KV_W]
        v = qkv_ref[:, Q_W + KV_W:QKV_W]
        rq = lax.rsqrt(_head_sum(q * q, bd_ref[...]) * (1.0 / HEAD) + EPS)
        qr = _rope(q * rq * wq_ref[...], cos, sa, sb) * (HEAD ** -0.5)
        rk = lax.rsqrt(_head_sum(k * k, bd_ref[0:KV_W, 0:KV_W]) * (1.0 / HEAD) + EPS)
        kr = _rope(k * rk * wk_ref[...], cos, sa, sb)
        for h in range(N_Q):
            qh_ref[h] = qr[:, HEAD * h:HEAD * (h + 1)].astype(BF)
        for h in range(N_KV):
            kh_ref[h] = kr[:, HEAD * h:HEAD * (h + 1)].astype(BF)
            vh_ref[h] = v[:, HEAD * h:HEAD * (h + 1)].astype(BF)

    heads = lambda n: pl.BlockSpec((n, tm, HEAD), lambda i: (0, i, 0))
    return pl.pallas_call(
        body, grid=(s // tm,),
        in_specs=[pl.BlockSpec((tm, QKV_W), lambda i: (i, 0)), pl.BlockSpec((tm, 1), lambda i: (i, 0)),
                  _full((1, Q_W)), _full((1, KV_W)), _full((1, 2 * HEAD)), _full((Q_W, Q_W))],
        out_specs=[heads(N_Q), heads(N_KV), heads(N_KV)],
        out_shape=[_sds((N_Q, s, HEAD), BF), _sds((N_KV, s, HEAD), BF), _sds((N_KV, s, HEAD), BF)],
        compiler_params=_cp(("parallel",)), name="qk_prep")(qkv, pos, wq, wk, invf, bd)


def _sink_column(sink_ref, kh):
    row_g = lax.broadcasted_iota(jnp.int32, (GQA * BLK, 1), 0) // BLK
    col = jnp.zeros((GQA * BLK, 1), F32)
    for g in range(GQA):
        col = jnp.where(row_g == g, sink_ref[kh * GQA + g], col)
    return col


def _fold_band(band, lower, first=None):
    prev, cur = band[:, 0:BLK], band[:, BLK:2 * BLK]
    if first is not None:
        prev = jnp.where(first, -jnp.inf, prev)
    return jnp.where(lower, cur, prev)


def _unfold_band(x, lower):
    return jnp.concatenate([jnp.where(lower, 0.0, x), jnp.where(lower, x, 0.0)], axis=1)


def _attn_probs(q, k, n, sink_col):
    sc = _nt(q, k)
    qi = lax.broadcasted_iota(jnp.int32, (sc.shape[0], BLK), 0) % BLK
    ki = lax.broadcasted_iota(jnp.int32, (sc.shape[0], BLK), 1)
    lower = ki <= qi
    sc = _fold_band(sc, lower, first=n == 0)
    m = jnp.maximum(jnp.max(sc, axis=-1, keepdims=True), sink_col)
    p = jnp.exp(sc - m)
    es = jnp.exp(sink_col - m)
    inv = 1.0 / (jnp.sum(p, axis=-1, keepdims=True) + es)
    return p * inv, es * inv, lower


def _attn_fwd(qh, kh, vh, sinks, deps=()):
    s = qh.shape[1]
    nb = s // BLK

    def body(sink_ref, q_ref, kp_ref, kc_ref, vp_ref, vc_ref, o_ref):
        n = pl.program_id(0)
        for khd in range(N_KV):
            q = q_ref[khd * GQA:(khd + 1) * GQA].reshape(GQA * BLK, HEAD)
            k = jnp.concatenate([kp_ref[khd], kc_ref[khd]], axis=0)
            v = jnp.concatenate([vp_ref[khd], vc_ref[khd]], axis=0)
            probs, _, lower = _attn_probs(q, k, n, _sink_column(sink_ref, khd))
            o = _mm(_unfold_band(probs, lower).astype(BF), v)
            for j in range(GQA // 2):
                c0 = khd * GQA * HEAD + 2 * HEAD * j
                o_ref[:, c0:c0 + 2 * HEAD] = jnp.concatenate(
                    [o[2 * j * BLK:(2 * j + 1) * BLK], o[(2 * j + 1) * BLK:(2 * j + 2) * BLK]], axis=1).astype(BF)

    prev = pl.BlockSpec((N_KV, BLK, HEAD), lambda n: (0, jnp.maximum(n - 1, 0), 0))
    cur = pl.BlockSpec((N_KV, BLK, HEAD), lambda n: (0, n, 0))
    return pl.pallas_call(
        _after(body, 6, deps), grid=(nb,),
        in_specs=[pl.BlockSpec(memory_space=pltpu.SMEM),
                  pl.BlockSpec((N_Q, BLK, HEAD), lambda n: (0, n, 0)), prev, cur, prev, cur] + _any_specs(deps),
        out_specs=pl.BlockSpec((BLK, Q_W), lambda n: (n, 0)),
        out_shape=_sds((s, Q_W), BF),
        compiler_params=_cp(("parallel",)), name="attn_fwd")(sinks, qh, kh, kh, vh, vh, *deps)


def _pool_fwd(u, w_pool, pool_scale, ts, deps=()):
    s = u.shape[0]
    halo = 16

    def body(u_ref, wp_ref, ps_ref, a_ref, pooled_ref, prev):
        g = pl.program_id(0)
        i = pl.program_id(1)

        @pl.when(i == 0)
        def _():
            prev[...] = jnp.zeros_like(prev)

        cur = u_ref[...]
        ext = jnp.concatenate([prev[...], cur], axis=0)
        t = (i * ts + lax.broadcasted_iota(jnp.int32, (ts, 1), 0)).astype(F32)
        for gi in range(4):
            @pl.when(g == gi)
            def _(gi=gi):
                w = 2 << gi
                acc, span = ext, 1
                while span < w:
                    acc = acc + pltpu.roll(acc, span, 0)
                    span *= 2
                inv = 1.0 / jnp.minimum(t + 1.0, float(w))
                pooled = (acc[halo:halo + ts] * inv - cur).astype(BF)
                pooled_ref[...] = pooled
                a_ref[...] = (_mm(pooled, wp_ref[0]) * ps_ref[...]).astype(BF)

        prev[...] = cur[ts - halo:ts]

    col = pl.BlockSpec((ts, POOL_GROUP), lambda g, i: (i, g))
    return pl.pallas_call(
        _after(body, 3, deps), grid=(4, s // ts),
        in_specs=[col, pl.BlockSpec((1, POOL_GROUP, POOL_GROUP), lambda g, i: (g, 0, 0)),
                  pl.BlockSpec((1, POOL_GROUP), lambda g, i: (0, g))] + _any_specs(deps),
        out_specs=[col, col],
        out_shape=[_sds((s, D), BF), _sds((s, D), BF)],
        scratch_shapes=[pltpu.VMEM((halo, POOL_GROUP), F32)],
        compiler_params=_cp(("parallel", "arbitrary")), name="pool_fwd")(u, w_pool, pool_scale, *deps)


def _mix_out_proj(a, b, gates, x, w_out, g2, tm):
    s = x.shape[0]

    def body(a_ref, b_ref, gate_ref, x_ref, w_ref, g_ref, mix_ref, y_ref, h_ref):
        mix = (gate_ref[:, 0:D].astype(F32) * a_ref[...].astype(F32)
               + gate_ref[:, D:2 * D].astype(F32) * b_ref[...].astype(F32)).astype(BF)
        mix_ref[...] = mix
        y = x_ref[...] + _mm(mix, w_ref[...])
        y_ref[...] = y
        r = lax.rsqrt(jnp.mean(y * y, axis=-1, keepdims=True) + EPS)
        h_ref[...] = (y * r * g_ref[...]).astype(BF)

    row = lambda w: pl.BlockSpec((tm, w), lambda i: (i, 0))
    return pl.pallas_call(
        body, grid=(s // tm,),
        in_specs=[row(D), row(D), row(2 * D), row(D), _full((D, D)), _full((1, D))],
        out_specs=[row(D), row(D), row(D)],
        out_shape=[_sds((s, D), BF), _sds((s, D), F32), _sds((s, D), BF)],
        compiler_params=_cp(("parallel",)), name="mix_out_proj")(a, b, gates, x, w_out, g2)


def _ffn_up(h2, w_up, conv_w, conv_b, tm):
    s = h2.shape[0]

    def body(h_ref, wg_ref, wv_ref, cwg_ref, cwv_ref, cbg_ref, cbv_ref,
             preg_ref, prev_ref, upg_ref, upv_ref, act_ref, halog, halov):
        i = pl.program_id(1)

        @pl.when(i == 0)
        def _():
            halog[...] = jnp.zeros_like(halog)
            halov[...] = jnp.zeros_like(halov)

        h = h_ref[...]

        def conv_half(w_ref, cw_ref, cb_ref, halo, pre_ref, up_ref):
            pre = _mm(h, w_ref[0])
            pre_ref[...] = pre.astype(BF)
            ext = jnp.concatenate([halo[...], pre], axis=0)
            cw = cw_ref[0]
            up = cb_ref[...] + cw[0:1] * pltpu.roll(ext, 2, 0)[8:8 + tm]
            up = up + cw[1:2] * pltpu.roll(ext, 1, 0)[8:8 + tm]
            up = up + cw[2:3] * pre
            halo[...] = pre[tm - 8:tm]
            up_ref[...] = up.astype(BF)
            return up

        gate = conv_half(wg_ref, cwg_ref, cbg_ref, halog, preg_ref, upg_ref)
        val = conv_half(wv_ref, cwv_ref, cbv_ref, halov, prev_ref, upv_ref)
        act_ref[...] = (gate * jax.nn.sigmoid(gate) * val).astype(BF)

    tile = pl.BlockSpec((tm, UP_SHARD), lambda j, i: (i, j))
    wspec = lambda off: pl.BlockSpec((1, D, UP_SHARD), lambda j, i: (j + off, 0, 0))
    cwspec = lambda off: pl.BlockSpec((1, 3, UP_SHARD), lambda j, i: (j + off, 0, 0))
    cbspec = lambda off: pl.BlockSpec((1, UP_SHARD), lambda j, i: (0, j + off))
    half = _sds((s, D_FF), BF)
    return pl.pallas_call(
        body, grid=(2, s // tm),
        in_specs=[pl.BlockSpec((tm, D), lambda j, i: (i, 0)), wspec(0), wspec(2), cwspec(0), cwspec(2),
                  cbspec(0), cbspec(2)],
        out_specs=[tile] * 5, out_shape=[half] * 5,
        scratch_shapes=[pltpu.VMEM((8, UP_SHARD), F32), pltpu.VMEM((8, UP_SHARD), F32)],
        compiler_params=_cp(("parallel", "arbitrary")), name="ffn_up")(
            h2, w_up, w_up, conv_w, conv_w, conv_b, conv_b)


def _ffn_down_loss(act, w_down, y1, tgt, tm):
    s = y1.shape[0]

    def body(act_ref, w_ref, y_ref, t_ref, dy_ref, dyb_ref, loss_ref):
        @pl.when(pl.program_id(0) == 0)
        def _():
            loss_ref[...] = jnp.zeros_like(loss_ref)

        e = y_ref[...] + _mm(act_ref[...], w_ref[...]) - t_ref[...]
        dy = e * (1.0 / D)
        dy_ref[...] = dy
        dyb_ref[...] = dy.astype(BF)
        e2 = (e * e).reshape(tm // 8, 8, D).sum(axis=0)
        part = e2[:, 0:128]
        for j in range(1, D // 128):
            part = part + e2[:, 128 * j:128 * (j + 1)]
        loss_ref[...] += part

    row = lambda w: pl.BlockSpec((tm, w), lambda i: (i, 0))
    return pl.pallas_call(
        body, grid=(s // tm,),
        in_specs=[row(D_FF), _full((D_FF, D)), row(D), row(D)],
        out_specs=[row(D), row(D), _full((8, 128))],
        out_shape=[_sds((s, D), F32), _sds((s, D), BF), _sds((8, 128), F32)],
        compiler_params=_cp(("arbitrary",)), name="ffn_down_loss")(act, w_down, y1, tgt)


def _grad_matmul(a, b, tn, tk, name, lead=None, prev=None, lead_off=0):
    s, m = a.shape
    n = b.shape[1]
    nj = n // tn

    def body(*refs):
        a_ref, b_ref = refs[0], refs[1]
        o_ref = refs[-1]
        acc = _tn(a_ref[...], b_ref[...])
        acc = acc if lead is None else acc[None]

        @pl.when(pl.program_id(1) == 0)
        def _():
            o_ref[...] = acc

        @pl.when(pl.program_id(1) > 0)
        def _():
            o_ref[...] += acc

    in_specs = [pl.BlockSpec((tk, m), lambda j, k: (k, 0)), pl.BlockSpec((tk, tn), lambda j, k: (k, j))]
    args = [a, b]
    aliases = {}
    if lead is None:
        out_spec = pl.BlockSpec((m, tn), lambda j, k: (0, j))
        out_shape = _sds((m, n), F32)
    else:
        out_spec = pl.BlockSpec((1, m, tn), lambda j, k: (j + lead_off, 0, 0))
        out_shape = _sds((lead, m, tn), F32)
        if prev is not None:
            in_specs.append(pl.BlockSpec(memory_space=pl.ANY))
            args.append(prev)
            aliases = {2: 0}
    return pl.pallas_call(
        body, grid=(nj, s // tk), in_specs=in_specs, out_specs=out_spec, out_shape=out_shape,
        input_output_aliases=aliases,
        compiler_params=_cp(("parallel", "arbitrary")), name=name)(*args)


def _ffn_act_bwd(dyb, w_down, up_g, up_v, pre_g, pre_v, conv_w, tm):
    s = dyb.shape[0]
    nt = s // tm

    def body(dy_ref, wd_ref, upg_ref, upv_ref, preg_ref, prev_ref, cwg_ref, cwv_ref,
             dpg_ref, dpv_ref, dcwg_ref, dcwv_ref, dcbg_ref, dcbv_ref, nxg, nxv):
        i = pl.program_id(1)

        @pl.when(i == 0)
        def _():
            nxg[...] = jnp.zeros_like(nxg)
            nxv[...] = jnp.zeros_like(nxv)
            dcwg_ref[...] = jnp.zeros_like(dcwg_ref)
            dcwv_ref[...] = jnp.zeros_like(dcwv_ref)
            dcbg_ref[...] = jnp.zeros_like(dcbg_ref)
            dcbv_ref[...] = jnp.zeros_like(dcbv_ref)

        dact = _nt(dy_ref[...], wd_ref[...])
        g = upg_ref[...].astype(F32)
        v = upv_ref[...].astype(F32)
        sg = jax.nn.sigmoid(g)
        d_v = dact * (g * sg)
        d_g = dact * v * (sg * (1.0 + g * (1.0 - sg)))

        def conv_bwd(d_up, nx, pre_ref, cw_ref, dp_ref, dcw_ref, dcb_ref):
            ext = jnp.concatenate([d_up, nx[...]], axis=0)
            s1 = pltpu.roll(ext, tm + 8 - 1, 0)[0:tm]
            s2 = pltpu.roll(ext, tm + 8 - 2, 0)[0:tm]
            cw = cw_ref[0]
            dp_ref[...] = (cw[2:3] * d_up + cw[1:2] * s1 + cw[0:1] * s2).astype(BF)
            nx[...] = d_up[0:8]
            pre = pre_ref[...].astype(F32)
            dcw_ref[0, 0:1, :] += jnp.sum(s2 * pre, axis=0, keepdims=True)
            dcw_ref[0, 1:2, :] += jnp.sum(s1 * pre, axis=0, keepdims=True)
            dcw_ref[0, 2:3, :] += jnp.sum(d_up * pre, axis=0, keepdims=True)
            dcb_ref[...] += jnp.sum(d_up, axis=0, keepdims=True)

        conv_bwd(d_g, nxg, preg_ref, cwg_ref, dpg_ref, dcwg_ref, dcbg_ref)
        conv_bwd(d_v, nxv, prev_ref, cwv_ref, dpv_ref, dcwv_ref, dcbv_ref)

    tile = pl.BlockSpec((tm, UP_SHARD), lambda j, i: (nt - 1 - i, j))
    cwspec = lambda off: pl.BlockSpec((1, 3, UP_SHARD), lambda j, i: (j + off, 0, 0))
    acc_cw = pl.BlockSpec((1, 3, UP_SHARD), lambda j, i: (j, 0, 0))
    acc_cb = pl.BlockSpec((1, UP_SHARD), lambda j, i: (0, j))
    buf = pltpu.VMEM((8, UP_SHARD), F32)
    return pl.pallas_call(
        body, grid=(2, nt),
        in_specs=[pl.BlockSpec((tm, D), lambda j, i: (nt - 1 - i, 0)),
                  pl.BlockSpec((UP_SHARD, D), lambda j, i: (j, 0)),
                  tile, tile, tile, tile, cwspec(0), cwspec(2)],
        out_specs=[tile, tile, acc_cw, acc_cw, acc_cb, acc_cb],
        out_shape=[_sds((s, D_FF), BF), _sds((s, D_FF), BF), _sds((2, 3, UP_SHARD), F32),
                   _sds((2, 3, UP_SHARD), F32), _sds((1, D_FF), F32), _sds((1, D_FF), F32)],
        scratch_shapes=[buf, buf],
        compiler_params=_cp(("parallel", "arbitrary")), name="ffn_act_bwd")(
            dyb, w_down, up_g, up_v, pre_g, pre_v, conv_w, conv_w)


def _rms_bwd(dh, y, g):
    r = lax.rsqrt(jnp.mean(y * y, axis=-1, keepdims=True) + EPS)
    n = y * r
    dn = dh * g
    return r * (dn - n * jnp.mean(dn * n, axis=-1, keepdims=True)), dh * n


def _ffn_up_bwd(dp_g, dp_v, w_up, y1, dy2, g2, tm, deps=()):
    s = y1.shape[0]

    def body(dg_ref, dv_ref, w_ref, y_ref, dy2_ref, g_ref, dy1_ref, dgn_ref):
        @pl.when(pl.program_id(0) == 0)
        def _():
            dgn_ref[...] = jnp.zeros_like(dgn_ref)

        dh = _nt(dg_ref[:, 0:UP_SHARD], w_ref[0])
        dh = dh + _nt(dg_ref[:, UP_SHARD:D_FF], w_ref[1])
        dh = dh + _nt(dv_ref[:, 0:UP_SHARD], w_ref[2])
        dh = dh + _nt(dv_ref[:, UP_SHARD:D_FF], w_ref[3])
        dy, dgn = _rms_bwd(dh, y_ref[...], g_ref[...])
        dy1_ref[...] = dy2_ref[...] + dy
        dgn_ref[...] += jnp.sum(dgn, axis=0, keepdims=True)

    row = lambda w: pl.BlockSpec((tm, w), lambda i: (i, 0))
    return pl.pallas_call(
        _after(body, 6, deps), grid=(s // tm,),
        in_specs=[row(D_FF), row(D_FF), _full((4, D, UP_SHARD)), row(D), row(D), _full((1, D))] + _any_specs(deps),
        out_specs=[row(D), _full((1, D))],
        out_shape=[_sds((s, D), F32), _sds((1, D), F32)],
        compiler_params=_cp(("arbitrary",)), name="ffn_up_bwd")(dp_g, dp_v, w_up, y1, dy2, g2, *deps)


def _out_proj_bwd(dy1, w_out, a, b, gates, mix, tm, deps=()):
    s = dy1.shape[0]

    def body(dy_ref, w_ref, a_ref, b_ref, gate_ref, mix_ref, da_ref, db_ref, dzg_ref, dbg_ref, dw_ref):
        @pl.when(pl.program_id(0) == 0)
        def _():
            dbg_ref[...] = jnp.zeros_like(dbg_ref)
            dw_ref[...] = jnp.zeros_like(dw_ref)

        dyb = dy_ref[...].astype(BF)
        dmix = _nt(dyb, w_ref[...])
        gp = gate_ref[:, 0:D].astype(F32)
        ga = gate_ref[:, D:2 * D].astype(F32)
        da_ref[...] = (dmix * gp).astype(BF)
        db_ref[...] = (dmix * ga).astype(BF)
        dzp = dmix * a_ref[...].astype(F32) * (gp * (1.0 - gp))
        dza = dmix * b_ref[...].astype(F32) * (ga * (1.0 - ga))
        dzg_ref[:, 0:D] = dzp.astype(BF)
        dzg_ref[:, D:2 * D] = dza.astype(BF)
        dbg_ref[:, 0:D] += jnp.sum(dzp, axis=0, keepdims=True)
        dbg_ref[:, D:2 * D] += jnp.sum(dza, axis=0, keepdims=True)
        dw_ref[...] += _tn(mix_ref[...], dyb)

    row = lambda w: pl.BlockSpec((tm, w), lambda i: (i, 0))
    return pl.pallas_call(
        _after(body, 6, deps), grid=(s // tm,),
        in_specs=[row(D), _full((D, D)), row(D), row(D), row(2 * D), row(D)] + _any_specs(deps),
        out_specs=[row(D), row(D), row(2 * D), _full((1, 2 * D)), _full((D, D))],
        out_shape=[_sds((s, D), BF), _sds((s, D), BF), _sds((s, 2 * D), BF), _sds((1, 2 * D), F32),
                   _sds((D, D), F32)],
        compiler_params=_cp(("arbitrary",)), name="out_proj_bwd")(dy1, w_out, a, b, gates, mix, *deps)


def _pool_bwd(da, pooled, w_pool, pool_scale, h1, ts):
    s = da.shape[0]
    nt = s // ts
    halo = 16

    def body(da_ref, pooled_ref, wp_ref, ps_ref, h_ref, du_ref, dwp_ref, dps_ref, dwi_ref, nxt):
        g = pl.program_id(0)
        i = pl.program_id(1)
        ti = nt - 1 - i

        @pl.when(i == 0)
        def _():
            nxt[...] = jnp.zeros_like(nxt)
            dwp_ref[...] = jnp.zeros_like(dwp_ref)
            dps_ref[...] = jnp.zeros_like(dps_ref)
            dwi_ref[...] = jnp.zeros_like(dwi_ref)

        pooled = pooled_ref[...]
        dav = da_ref[...].astype(F32)
        dps_ref[...] += jnp.sum(dav * _mm(pooled, wp_ref[0]), axis=0, keepdims=True)
        dm = (dav * ps_ref[...]).astype(BF)
        dwp_ref[0] += _tn(pooled, dm)
        dpool = _nt(dm, wp_ref[0])
        t = (ti * ts + lax.broadcasted_iota(jnp.int32, (ts, 1), 0)).astype(F32)
        for gi in range(4):
            @pl.when(g == gi)
            def _(gi=gi):
                w = 2 << gi
                e = dpool * (1.0 / jnp.minimum(t + 1.0, float(w)))
                acc, span = jnp.concatenate([e, nxt[...]], axis=0), 1
                while span < w:
                    acc = acc + pltpu.roll(acc, ts + halo - span, 0)
                    span *= 2
                du = (acc[0:ts] - dpool).astype(BF)
                du_ref[...] = du
                dwi_ref[...] += _tn(du, h_ref[...])
                nxt[...] = e[0:halo]

    col = pl.BlockSpec((ts, POOL_GROUP), lambda g, i: (nt - 1 - i, g))
    return pl.pallas_call(
        body, grid=(4, nt),
        in_specs=[col, col, pl.BlockSpec((1, POOL_GROUP, POOL_GROUP), lambda g, i: (g, 0, 0)),
                  pl.BlockSpec((1, POOL_GROUP), lambda g, i: (0, g)),
                  pl.BlockSpec((ts, D), lambda g, i: (nt - 1 - i, 0))],
        out_specs=[col, pl.BlockSpec((1, POOL_GROUP, POOL_GROUP), lambda g, i: (g, 0, 0)),
                   pl.BlockSpec((1, POOL_GROUP), lambda g, i: (0, g)),
                   pl.BlockSpec((POOL_GROUP, D), lambda g, i: (g, 0))],
        out_shape=[_sds((s, D), BF), _sds((4, POOL_GROUP, POOL_GROUP), F32), _sds((1, D), F32), _sds((D, D), F32)],
        scratch_shapes=[pltpu.VMEM((halo, POOL_GROUP), F32)],
        compiler_params=_cp(("parallel", "arbitrary")), name="pool_bwd")(da, pooled, w_pool, pool_scale, h1)


def _attn_bwd(qh, kh, vh, sinks, db, deps=()):
    s = qh.shape[1]
    nb = s // BLK

    def body(sink_ref, q_ref, kp_ref, kc_ref, vp_ref, vc_ref, do_ref,
             dq_ref, dk_ref, dv_ref, dsink_ref, ck, cv):
        n = pl.program_id(0)

        @pl.when(n == 0)
        def _():
            ck[...] = jnp.zeros_like(ck)
            cv[...] = jnp.zeros_like(cv)
            dsink_ref[...] = jnp.zeros_like(dsink_ref)

        @pl.when(n < nb)
        def _():
            dov = do_ref[...]
            for khd in range(N_KV):
                q = q_ref[khd * GQA:(khd + 1) * GQA].reshape(GQA * BLK, HEAD)
                k = jnp.concatenate([kp_ref[khd], kc_ref[khd]], axis=0)
                v = jnp.concatenate([vp_ref[khd], vc_ref[khd]], axis=0)
                c0 = khd * GQA * HEAD
                do = jnp.concatenate([dov[:, c0 + HEAD * g:c0 + HEAD * (g + 1)] for g in range(GQA)],
                                     axis=0).astype(BF)
                probs, psink, lower = _attn_probs(q, k, n, _sink_column(sink_ref, khd))
                dp = _fold_band(_nt(do, v), lower)
                delta = jnp.sum(probs * dp, axis=-1, keepdims=True)
                ds = _unfold_band(probs * (dp - delta), lower).astype(BF)
                dq_ref[khd * GQA:(khd + 1) * GQA] = _mm(ds, k).reshape(GQA, BLK, HEAD)
                dk = _tn(ds, q)
                dv = _tn(_unfold_band(probs, lower).astype(BF), do)
                dk_ref[khd] = ck[khd] + dk[0:BLK]
                dv_ref[khd] = cv[khd] + dv[0:BLK]
                ck[khd] = dk[BLK:2 * BLK]
                cv[khd] = dv[BLK:2 * BLK]
                dsk = psink * delta
                lane = lax.broadcasted_iota(jnp.int32, (1, 128), 1)
                acc = jnp.zeros((1, 128), F32)
                for g in range(GQA):
                    acc = acc - jnp.where(lane == khd * GQA + g,
                                          jnp.sum(dsk[g * BLK:(g + 1) * BLK], axis=0, keepdims=True), 0.0)
                dsink_ref[...] += acc

        @pl.when(n == nb)
        def _():
            dk_ref[...] = ck[...]
            dv_ref[...] = cv[...]

    last = nb - 1
    prev = pl.BlockSpec((N_KV, BLK, HEAD), lambda n: (0, jnp.maximum(jnp.minimum(n, last) - 1, 0), 0))
    cur = pl.BlockSpec((N_KV, BLK, HEAD), lambda n: (0, jnp.minimum(n, last), 0))
    kv_out = pl.BlockSpec((N_KV, BLK, HEAD), lambda n: (0, jnp.maximum(n - 1, 0), 0))
    return pl.pallas_call(
        _after(body, 7, deps), grid=(nb + 1,),
        in_specs=[pl.BlockSpec(memory_space=pltpu.SMEM),
                  pl.BlockSpec((N_Q, BLK, HEAD), lambda n: (0, jnp.minimum(n, last), 0)),
                  prev, cur, prev, cur,
                  pl.BlockSpec((BLK, Q_W), lambda n: (jnp.minimum(n, last), 0))] + _any_specs(deps),
        out_specs=[pl.BlockSpec((N_Q, BLK, HEAD), lambda n: (0, jnp.minimum(n, last), 0)), kv_out, kv_out,
                   _full((1, 128))],
        out_shape=[_sds((N_Q, s, HEAD), F32), _sds((N_KV, s, HEAD), F32), _sds((N_KV, s, HEAD), F32),
                   _sds((1, 128), F32)],
        scratch_shapes=[pltpu.VMEM((N_KV, BLK, HEAD), F32), pltpu.VMEM((N_KV, BLK, HEAD), F32)],
        compiler_params=_cp(("arbitrary",)), name="attn_bwd")(sinks, qh, kh, kh, vh, vh, db, *deps)


def _qk_prep_bwd(dqh, dkh, dvh, qkv, pos, wq, wk, invf, bd, tm, deps=()):
    s = qkv.shape[0]

    def fold_heads(row):
        out = row[:, 0:HEAD]
        for h in range(1, row.shape[1] // HEAD):
            out = out + row[:, HEAD * h:HEAD * (h + 1)]
        return out

    def body(dq_ref, dk_ref, dv_ref, qkv_ref, pos_ref, wq_ref, wk_ref, invf_ref, bd_ref,
             dz_ref, dwq_ref, dwk_ref):
        @pl.when(pl.program_id(0) == 0)
        def _():
            dwq_ref[...] = jnp.zeros_like(dwq_ref)
            dwk_ref[...] = jnp.zeros_like(dwk_ref)

        cos, sa, sb = _rope_tables(pos_ref, invf_ref)

        def norm_rope_bwd(dy, xin, w, bdm):
            dn = _rope_t(dy, cos, sa, sb)
            r = lax.rsqrt(_head_sum(xin * xin, bdm) * (1.0 / HEAD) + EPS)
            nh = xin * r
            gw = dn * w
            dx = r * (gw - nh * (_head_sum(gw * nh, bdm) * (1.0 / HEAD)))
            return dx, fold_heads(jnp.sum(dn * nh, axis=0, keepdims=True))

        dq = jnp.concatenate([dq_ref[h] for h in range(N_Q)], axis=1) * (HEAD ** -0.5)
        dk = jnp.concatenate([dk_ref[h] for h in range(N_KV)], axis=1)
        dxq, dwq = norm_rope_bwd(dq, qkv_ref[:, 0:Q_W], wq_ref[...], bd_ref[...])
        dxk, dwk = norm_rope_bwd(dk, qkv_ref[:, Q_W:Q_W + KV_W], wk_ref[...], bd_ref[0:KV_W, 0:KV_W])
        dz_ref[:, 0:Q_W] = dxq.astype(BF)
        dz_ref[:, Q_W:Q_W + KV_W] = dxk.astype(BF)
        dz_ref[:, Q_W + KV_W:QKV_W] = jnp.concatenate([dv_ref[h] for h in range(N_KV)], axis=1).astype(BF)
        dwq_ref[...] += dwq
        dwk_ref[...] += dwk

    heads = lambda n: pl.BlockSpec((n, tm, HEAD), lambda i: (0, i, 0))
    return pl.pallas_call(
        _after(body, 9, deps), grid=(s // tm,),
        in_specs=[heads(N_Q), heads(N_KV), heads(N_KV), pl.BlockSpec((tm, QKV_W), lambda i: (i, 0)),
                  pl.BlockSpec((tm, 1), lambda i: (i, 0)), _full((1, Q_W)), _full((1, KV_W)),
                  _full((1, 2 * HEAD)), _full((Q_W, Q_W))] + _any_specs(deps),
        out_specs=[pl.BlockSpec((tm, QKV_W), lambda i: (i, 0)), _full((1, HEAD)), _full((1, HEAD))],
        out_shape=[_sds((s, QKV_W), BF), _sds((1, HEAD), F32), _sds((1, HEAD), F32)],
        compiler_params=_cp(("arbitrary",)), name="qk_prep_bwd")(
            dqh, dkh, dvh, qkv, pos, wq, wk, invf, bd, *deps)


def _in_proj_bwd(du, dzq, dzg, w_in, x, g1, dy1, tm, deps=()):
    s = x.shape[0]

    def body(du_ref, dzq_ref, dzg_ref, w_ref, x_ref, g_ref, dy_ref, gx_ref, dgn_ref):
        @pl.when(pl.program_id(0) == 0)
        def _():
            dgn_ref[...] = jnp.zeros_like(dgn_ref)

        dh = _mm(du_ref[...], w_ref[0:D, :])
        dh = dh + _mm(dzq_ref[...], w_ref[D:D + QKV_W, :])
        dh = dh + _mm(dzg_ref[...], w_ref[D + QKV_W:IN_W, :])
        dx, dgn = _rms_bwd(dh, x_ref[...], g_ref[...])
        gx_ref[...] = dy_ref[...] + dx
        dgn_ref[...] += jnp.sum(dgn, axis=0, keepdims=True)

    row = lambda w: pl.BlockSpec((tm, w), lambda i: (i, 0))
    return pl.pallas_call(
        _after(body, 7, deps), grid=(s // tm,),
        in_specs=[row(D), row(QKV_W), row(2 * D), _full((IN_W, D)), row(D), _full((1, D)), row(D)] + _any_specs(deps),
        out_specs=[row(D), _full((1, D))],
        out_shape=[_sds((s, D), F32), _sds((1, D), F32)],
        compiler_params=_cp(("arbitrary",)), name="in_proj_bwd")(du, dzq, dzg, w_in, x, g1, dy1, *deps)


def _adamw_step(w, g, m, v):
    mn = B1 * m + (1.0 - B1) * g
    vn = B2 * v + (1.0 - B2) * (g * g)
    m_hat = mn / (1.0 - B1 ** STEP)
    v_hat = vn / (1.0 - B2 ** STEP)
    return -LR * (m_hat / (jnp.sqrt(v_hat) + ADAM_EPS) + WD * w), mn, vn


SMALL_ROWS = 16
SMALL_COLS = D_FF
SMALL_AT = {"b_gate": (1, 2 * D), "pool_scale": (2, D), "q_norm": (3, HEAD),
            "k_norm": (4, HEAD), "sinks": (5, N_Q), "ffn_norm": (6, D)}
SMALL_LOSS_ROW = 0
SMALL_CONV_B_ROW = 7
SMALL_CONV_W_ROW = 9


def _pack_small(loss_acc, d_bgate, d_pscale, d_qn, d_kn, dsink, d_ffn_norm, dcb_g, dcb_v, dcw_g, dcw_v, dev):
    def body(k_ref, ls_ref, bg_ref, ps_ref, qn_ref, kn_ref, sk_ref, fn_ref, cbg_ref, cbv_ref, cwg_ref, cwv_ref,
             o_ref):
        o_ref[...] = jnp.zeros_like(o_ref)
        o_ref[0, SMALL_LOSS_ROW:SMALL_LOSS_ROW + 1, 0:128] = jnp.sum(ls_ref[...], axis=0, keepdims=True)
        for nm, ref in (("b_gate", bg_ref), ("pool_scale", ps_ref), ("q_norm", qn_ref),
                        ("k_norm", kn_ref), ("ffn_norm", fn_ref)):
            row, n = SMALL_AT[nm]
            o_ref[0, row:row + 1, 0:n] = ref[...]
        row, _ = SMALL_AT["sinks"]
        o_ref[0, row:row + 1, 0:128] = sk_ref[...]
        o_ref[0, SMALL_CONV_B_ROW:SMALL_CONV_B_ROW + 1, :] = cbg_ref[...]
        o_ref[0, SMALL_CONV_B_ROW + 1:SMALL_CONV_B_ROW + 2, :] = cbv_ref[...]
        for k in range(3):
            row = SMALL_CONV_W_ROW + 2 * k
            for half in range(2):
                o_ref[0, row:row + 1, half * UP_SHARD:(half + 1) * UP_SHARD] = cwg_ref[half, k:k + 1, :]
                o_ref[0, row + 1:row + 2, half * UP_SHARD:(half + 1) * UP_SHARD] = cwv_ref[half, k:k + 1, :]

    args = [loss_acc, d_bgate, d_pscale, d_qn, d_kn, dsink, d_ffn_norm, dcb_g, dcb_v, dcw_g, dcw_v]
    grid_spec = pltpu.PrefetchScalarGridSpec(
        num_scalar_prefetch=1, grid=(1,),
        in_specs=[pl.BlockSpec(a.shape, functools.partial(lambda nd, i, k: (0,) * nd, a.ndim)) for a in args],
        out_specs=pl.BlockSpec((1, SMALL_ROWS, SMALL_COLS), lambda i, k: (k[0], 0, 0)))
    return pl.pallas_call(body, grid_spec=grid_spec, out_shape=_sds((N_DEV, SMALL_ROWS, SMALL_COLS), F32),
                          name="pack_small")(dev, *args)


def _small_update(stack, attn_stack, params):
    names = list(params)

    def body(*refs):
        s_ref, a_ref = refs[0], refs[1]
        ins = refs[2:2 + 3 * len(names)]
        outs = refs[2 + 3 * len(names):]
        tot, tot_a = s_ref[0], a_ref[0]
        for d in range(1, N_DEV):
            tot = tot + s_ref[d]
            tot_a = tot_a + a_ref[d]
        for i, nm in enumerate(names):
            if nm == "attn_norm":
                g = tot_a
            elif nm == "conv_b":
                g = jnp.concatenate([tot[SMALL_CONV_B_ROW:SMALL_CONV_B_ROW + 1, :],
                                     tot[SMALL_CONV_B_ROW + 1:SMALL_CONV_B_ROW + 2, :]], axis=1)
            else:
                row, n = SMALL_AT[nm]
                g = tot[row:row + 1, 0:n]
            delta, mn, vn = _adamw_step(ins[3 * i][...], g, ins[3 * i + 1][...], ins[3 * i + 2][...])
            outs[4 * i][...] = g
            outs[4 * i + 1][...] = delta
            outs[4 * i + 2][...] = mn
            outs[4 * i + 3][...] = vn
        for k in range(3):
            row = SMALL_CONV_W_ROW + 2 * k
            outs[-2][k:k + 1, 0:D_FF] = tot[row:row + 1, :]
            outs[-2][k:k + 1, D_FF:2 * D_FF] = tot[row + 1:row + 2, :]
        outs[-1][...] = jnp.sum(tot[SMALL_LOSS_ROW:SMALL_LOSS_ROW + 1, 0:128], axis=1, keepdims=True) * (0.5 / D)

    flat = [a for nm in names for a in params[nm]]
    out_shape = ([_sds(params[nm][0].shape, F32) for nm in names for _ in range(4)]
                 + [_sds((3, 2 * D_FF), F32), _sds((1, 1), F32)])
    res = pl.pallas_call(body, out_shape=out_shape, name="small_update")(stack, attn_stack, *flat)
    return {nm: list(res[4 * i:4 * i + 4]) for i, nm in enumerate(names)}, res[-2], res[-1]


def _adamw(w, g, m, v, tr, name, deps=()):
    r, c = w.shape

    def body(w_ref, g_ref, m_ref, v_ref, go_ref, d_ref, mo_ref, vo_ref):
        gv = g_ref[...]
        go_ref[...] = gv
        d_ref[...], mo_ref[...], vo_ref[...] = _adamw_step(w_ref[...], gv, m_ref[...], v_ref[...])

    blk = pl.BlockSpec((tr, c), lambda i: (i, 0))
    return pl.pallas_call(
        _after(body, 4, deps), grid=(r // tr,), in_specs=[blk] * 4 + _any_specs(deps), out_specs=[blk] * 4,
        out_shape=[_sds((r, c), F32)] * 4, compiler_params=_cp(("parallel",)), name=name)(w, g, m, v, *deps)


def _place():
    x, y, c = lax.axis_index("x"), lax.axis_index("y"), lax.axis_index("c")
    chips = [(1 - x, y), (x, 1 - y), (1 - x, 1 - y)]
    return x, y, c, chips


def _rows(ref, lead, h, rh):
    sl = pl.ds(pl.multiple_of(h * rh, 16), rh)
    return ref.at[sl, :] if lead is None else ref.at[lead, sl, :]


def _all_gather_weights(halved, whole, placed):
    nh, nw, npl = len(halved), len(whole), len(placed)
    na = nh + nw
    nall = na + npl
    arrays = list(halved) + list(whole) + list(placed)
    out_dtypes = [BF] * nh + [a.dtype for a in whole] + [BF] * npl
    cast_rows = 128

    def body(*refs):
        ins, outs = refs[:nall], refs[nall:2 * nall]
        raw, stage = refs[2 * nall:3 * nall], refs[3 * nall:3 * nall + nh + npl]
        ici_send, ici_recv, fwd_send, fwd_recv, in_sem, loc_sem = refs[3 * nall + nh + npl:]
        x, y, c, chips = _place()
        me = 2 * x + y
        sibling = (x, y, 1 - c)
        loads = [pltpu.make_async_copy(ins[a], raw[a], in_sem.at[a]) for a in range(nall)]
        for cp in loads:
            cp.start()

        def cast(a, dst):
            r = arrays[a].shape[0]
            for r0 in range(0, r, cast_rows):
                r1 = min(r0 + cast_rows, r)
                dst[r0:r1, :] = raw[a][r0:r1, :].astype(BF)

        def ici(a, j, src_chip, src=None):
            if a < nh:
                rh = arrays[a].shape[0] // 2
                dst = _rows(outs[a], src_chip, c, rh)
                src = dst if src is None else _rows(src, None, c, rh)
            else:
                dst = outs[a].at[src_chip]
                src = dst if src is None else src
            return pltpu.make_async_remote_copy(
                src_ref=src, dst_ref=dst, send_sem=ici_send.at[3 * a + j], recv_sem=ici_recv.at[3 * a + j],
                device_id=(*chips[j], c), device_id_type=MESH)

        def fwd(a, j, half):
            rh = arrays[a].shape[0] // 2
            kj = 2 * chips[j][0] + chips[j][1]
            blk = _rows(outs[a], kj, half, rh)
            return pltpu.make_async_remote_copy(
                src_ref=blk, dst_ref=blk, send_sem=fwd_send.at[3 * a + j], recv_sem=fwd_recv.at[3 * a + j],
                device_id=sibling, device_id_type=MESH)

        local, sends = [], []
        for a in range(na):
            loads[a].wait()
            if a < nh:
                cast(a, stage[a])
                own = stage[a]
            else:
                own = raw[a]
            cp = pltpu.make_async_copy(own, outs[a].at[me], loc_sem.at[a])
            cp.start()
            local.append(cp)
            for j in range(3):
                cp = ici(a, j, me, src=own)
                cp.start()
                sends.append(cp)
        for i in range(npl):
            loads[na + i].wait()
            cast(na + i, stage[nh + i])
            cp = pltpu.make_async_copy(stage[nh + i], outs[na + i].at[me], loc_sem.at[na + i])
            cp.start()
            local.append(cp)
        passed = []
        for a in range(na):
            for j in range(3):
                kj = 2 * chips[j][0] + chips[j][1]
                ici(a, j, kj).wait_recv()
                if a < nh:
                    cp = fwd(a, j, c)
                    cp.start()
                    passed.append(cp)
        for a in range(nh):
            for j in range(3):
                fwd(a, j, 1 - c).wait_recv()
        for cp in sends + passed:
            cp.wait_send()
        for cp in local:
            cp.wait()

    any_spec = pl.BlockSpec(memory_space=pl.ANY)
    return pl.pallas_call(
        body, in_specs=[any_spec] * nall, out_specs=[any_spec] * nall,
        out_shape=[_sds((N_CHIPS,) + a.shape, dt) for a, dt in zip(arrays, out_dtypes)],
        scratch_shapes=[pltpu.VMEM(a.shape, a.dtype) for a in arrays]
        + [pltpu.VMEM(a.shape, BF) for a in list(halved) + list(placed)]
        + [pltpu.SemaphoreType.DMA((3 * na,)), pltpu.SemaphoreType.DMA((3 * na,)),
           pltpu.SemaphoreType.DMA((3 * nh,)), pltpu.SemaphoreType.DMA((3 * nh,)),
           pltpu.SemaphoreType.DMA((nall,)), pltpu.SemaphoreType.DMA((nall,))],
        compiler_params=pltpu.CompilerParams(vmem_limit_bytes=VMEM_LIMIT_MB << 20),
        name="all_gather_weights")(*arrays)


def _pair_sum(g, recv, c, tr, name):
    _, r, cols = g.shape
    rh = r // 2
    nr = rh // tr

    def body(c_ref, g_ref, r_ref, o_ref):
        o_ref[...] = (g_ref[...] + r_ref[...]).astype(BF)

    grid_spec = pltpu.PrefetchScalarGridSpec(
        num_scalar_prefetch=1, grid=(N_CHIPS, nr),
        in_specs=[pl.BlockSpec((1, tr, cols), lambda k, i, c_ref: (k, c_ref[0] * nr + i, 0)),
                  pl.BlockSpec((1, tr, cols), lambda k, i, c_ref: (k, i, 0))],
        out_specs=pl.BlockSpec((1, tr, cols), lambda k, i, c_ref: (k, i, 0)))
    return pl.pallas_call(
        body, grid_spec=grid_spec, out_shape=_sds((N_CHIPS, rh, cols), BF),
        compiler_params=_cp(("parallel", "parallel")), name=name)(c, g, recv)


def _chip_sum(g, sib, recv, place, tr, name):
    _, r, cols = g.shape
    rh = r // 2
    nr = rh // tr

    def body(p_ref, g_ref, s_ref, r0_ref, r1_ref, r2_ref, o_ref):
        own = g_ref[0] + s_ref[0]
        o_ref[...] = ((own + r0_ref[0].astype(F32)) + r1_ref[0].astype(F32)) + r2_ref[0].astype(F32)

    rspec = lambda j: pl.BlockSpec((1, tr, cols), lambda i, p: (j, i, 0))
    grid_spec = pltpu.PrefetchScalarGridSpec(
        num_scalar_prefetch=1, grid=(nr,),
        in_specs=[pl.BlockSpec((1, tr, cols), lambda i, p: (p[0], p[1] * nr + i, 0)),
                  pl.BlockSpec((1, tr, cols), lambda i, p: (p[0], i, 0)), rspec(0), rspec(1), rspec(2)],
        out_specs=pl.BlockSpec((tr, cols), lambda i, p: (p[1] * nr + i, 0)))
    return pl.pallas_call(
        body, grid_spec=grid_spec, out_shape=_sds((r, cols), F32),
        compiler_params=_cp(("parallel",)), name=name)(place, g, sib, recv, recv, recv)


_HBM = pl.BlockSpec(memory_space=pltpu.HBM)
_SEM = pl.BlockSpec(memory_space=pltpu.SEMAPHORE)
_EFFECT = pltpu.SideEffectType.DATAFLOW_SIDE_EFFECTING


def _remote(src, dst, ssem, rsem, k, device):
    return pltpu.make_async_remote_copy(src_ref=src, dst_ref=dst, send_sem=ssem.at[k], recv_sem=rsem.at[k],
                                        device_id=device, device_id_type=MESH)


def _split_start(name, bufs, plan, n):
    nb = len(bufs)

    def body(*refs):
        sends, _ = plan(refs[:nb], refs[nb], refs[nb + 1])
        for cp in sends:
            cp.start()
        refs[-1][...] = jnp.zeros_like(refs[-1])

    res = pl.pallas_call(
        body, name=name,
        out_shape=(pltpu.SemaphoreType.DMA((n,)), pltpu.SemaphoreType.DMA((n,)))
        + tuple(pltpu.HBM(b.shape, b.dtype) for b in bufs) + (_sds((8, 128), F32),),
        in_specs=[_HBM] * nb,
        out_specs=(_SEM, _SEM) + (_HBM,) * nb + (pl.BlockSpec(memory_space=pltpu.VMEM),),
        input_output_aliases={i: i + 2 for i in range(nb)},
        compiler_params=pltpu.CompilerParams(has_side_effects=_EFFECT),
    )(*[pltpu.with_memory_space_constraint(b, pltpu.HBM) for b in bufs])
    return res[0], res[1], list(res[2:2 + nb]), res[2 + nb]


def _split_wait(name, send_sem, recv_sem, bufs, plan, after):
    nb = len(bufs)

    def body(*refs):
        sends, arrivals = plan(refs[:nb], refs[nb], refs[nb + 1])
        for cp in sends:
            cp.wait_send()
        for cp in arrivals:
            cp.wait_recv()

    res = pl.pallas_call(
        body, name=name, out_shape=tuple(pltpu.HBM(b.shape, b.dtype) for b in bufs),
        in_specs=[_HBM] * nb + [_SEM, _SEM, pl.BlockSpec(memory_space=pl.ANY)],
        out_specs=(_HBM,) * nb, input_output_aliases={i: i for i in range(nb)},
        compiler_params=pltpu.CompilerParams(has_side_effects=_EFFECT),
    )(*bufs, send_sem, recv_sem, after)
    return list(res)


def _plan_sibling_halves(shapes):
    na = len(shapes)

    def plan(refs, ssem, rsem):
        x, y, c, _ = _place()
        cps = []
        for a in range(na):
            rh = shapes[a][1] // 2
            src = refs[a].at[:, pl.ds(pl.multiple_of((1 - c) * rh, 8), rh), :]
            cps.append(_remote(src, refs[na + a], ssem, rsem, a, (x, y, 1 - c)))
        return cps, cps

    return plan


def _to_all(ref, ssem, rsem, base):
    x, y, c, _ = _place()
    mine = ref.at[4 * x + 2 * y + c]
    return [_remote(mine, mine, ssem, rsem, base + r - 1, (x ^ (r >> 2), y ^ ((r >> 1) & 1), c ^ (r & 1)))
            for r in range(1, N_DEV)]


def _plan_chip_exchange(na, with_small):
    def plan(refs, ssem, rsem):
        _, _, c, chips = _place()
        cps = []
        for a in range(na):
            for j in range(3):
                kj = 2 * chips[j][0] + chips[j][1]
                cps.append(_remote(refs[a].at[kj], refs[na + a].at[j], ssem, rsem, 3 * a + j, (*chips[j], c)))
        if with_small:
            cps += _to_all(refs[2 * na], ssem, rsem, 3 * na)
        return cps, cps

    return plan


def _plan_sibling_swap(shapes, with_small):
    def plan(refs, ssem, rsem):
        x, y, c, _ = _place()
        sends, arrivals = [], []
        for a, shp in enumerate(shapes):
            rh = shp[0] // 2
            mine, other = _rows(refs[a], None, c, rh), _rows(refs[a], None, 1 - c, rh)
            sends.append(_remote(mine, mine, ssem, rsem, a, (x, y, 1 - c)))
            arrivals.append(_remote(mine, other, ssem, rsem, a, (x, y, 1 - c)))
        if with_small:
            cps = _to_all(refs[len(shapes)], ssem, rsem, len(shapes))
            sends += cps
            arrivals += cps
        return sends, arrivals

    return plan


def _plan_gather_chips(shapes):
    def plan(refs, ssem, rsem):
        x, y, c, chips = _place()
        me = 2 * x + y
        sends, arrivals = [], []
        for a, shp in enumerate(shapes):
            rh = shp[1] // 2
            mine = _rows(refs[a], me, c, rh)
            for j in range(3):
                land = _rows(refs[a], 2 * chips[j][0] + chips[j][1], c, rh)
                sends.append(_remote(mine, mine, ssem, rsem, 3 * a + j, (*chips[j], c)))
                arrivals.append(_remote(land, land, ssem, rsem, 3 * a + j, (*chips[j], c)))
        return sends, arrivals

    return plan


def _plan_gather_sibling(shapes):
    def plan(refs, ssem, rsem):
        x, y, c, chips = _place()
        sends, arrivals = [], []
        for a, shp in enumerate(shapes):
            rh = shp[1] // 2
            for j in range(3):
                kj = 2 * chips[j][0] + chips[j][1]
                got, land = _rows(refs[a], kj, c, rh), _rows(refs[a], kj, 1 - c, rh)
                sends.append(_remote(got, got, ssem, rsem, 3 * a + j, (x, y, 1 - c)))
                arrivals.append(_remote(got, land, ssem, rsem, 3 * a + j, (x, y, 1 - c)))
        return sends, arrivals

    return plan


def _into_slice(w, k, n, tr, dtype, name, deps=()):
    r, cols = w.shape

    def body(k_ref, w_ref, o_ref):
        o_ref[0] = w_ref[...].astype(dtype)

    grid_spec = pltpu.PrefetchScalarGridSpec(
        num_scalar_prefetch=1, grid=(r // tr,),
        in_specs=[pl.BlockSpec((tr, cols), lambda i, k: (i, 0))] + _any_specs(deps),
        out_specs=pl.BlockSpec((1, tr, cols), lambda i, k: (k[0], i, 0)))
    return pl.pallas_call(_after(body, 2, deps), grid_spec=grid_spec, out_shape=_sds((n, r, cols), dtype),
                          compiler_params=_cp(("parallel",)), name=name)(k, w, *deps)


class _LateWeights:
    def __init__(self, bufs):
        self.n = 3 * len(bufs)
        self.chips, self.sibling = _plan_gather_chips([b.shape for b in bufs]), _plan_gather_sibling([b.shape for b in bufs])
        self.ssem, self.rsem, self.bufs, token = _split_start("gather_chips_start", bufs, self.chips, self.n)
        self.first = (token,)

    def middle(self, after):
        bufs = _split_wait("gather_chips_wait", self.ssem, self.rsem, self.bufs, self.chips, after)
        self.ssem, self.rsem, self.bufs, token = _split_start("gather_sibling_start", bufs, self.sibling, self.n)
        return (token,)

    def last(self, after):
        return _split_wait("gather_sibling_wait", self.ssem, self.rsem, self.bufs, self.sibling, after)


class _GradReduce:
    def __init__(self, tag, place, names, tiles):
        self.tag, self.place, self.names, self.tiles = tag, place, names, tiles
        self.small_all = None

    def first(self, grads):
        self.na = len(grads)
        self.p1 = _plan_sibling_halves([g.shape for g in grads])
        lands = [lax.empty((N_CHIPS, g.shape[1] // 2, g.shape[2]), F32) for g in grads]
        self.ssem, self.rsem, self.bufs, token = _split_start(
            self.tag + "_halves_start", list(grads) + lands, self.p1, self.na)
        return (token,)

    def second(self, after, small=None):
        bufs = _split_wait(self.tag + "_halves_wait", self.ssem, self.rsem, self.bufs, self.p1, after)
        self.grads, self.sib = bufs[:self.na], bufs[self.na:]
        halves = [_pair_sum(g, r, self.place[1:2], t, "pair_sum_" + nm)
                  for g, r, t, nm in zip(self.grads, self.sib, self.tiles, self.names)]
        lands = [lax.empty((3,) + h.shape[1:], h.dtype) for h in halves]
        extra = [] if small is None else [small]
        self.p2 = _plan_chip_exchange(self.na, small is not None)
        self.ssem, self.rsem, self.bufs, token = _split_start(
            self.tag + "_chips_start", halves + lands + extra, self.p2, 3 * self.na + (N_DEV - 1) * len(extra))
        return (token,)

    def third(self, after, small=None):
        bufs = _split_wait(self.tag + "_chips_wait", self.ssem, self.rsem, self.bufs, self.p2, after)
        if len(bufs) > 2 * self.na:
            self.small_chips = bufs[2 * self.na]
        mine = [_chip_sum(g, sb, r, self.place, t, "chip_sum_" + nm)
                for g, sb, r, t, nm in zip(self.grads, self.sib, bufs[self.na:2 * self.na], self.tiles, self.names)]
        extra = [] if small is None else [small]
        self.p3 = _plan_sibling_swap([m.shape for m in mine], small is not None)
        self.ssem, self.rsem, self.bufs, token = _split_start(
            self.tag + "_swap_start", mine + extra, self.p3, self.na + (N_DEV - 1) * len(extra))
        return (token,)

    def last(self, after):
        bufs = _split_wait(self.tag + "_swap_wait", self.ssem, self.rsem, self.bufs, self.p3, after)
        if len(bufs) > self.na:
            self.small_swap = bufs[self.na]
        return bufs[:self.na]


class _WeightsAtHand:
    def __init__(self, wup, wout, wdown):
        self.first, self.weights = (), [wup, wout, wdown]

    def middle(self, after):
        return ()

    def last(self, after):
        return self.weights


class _GradsKept:
    def first(self, grads):
        self.grads = list(grads)
        return ()

    def second(self, after, small=None):
        return ()

    def third(self, after, small=None):
        return ()

    def last(self, after):
        return self.grads


def _forward_backward(xs, pos, tgt, win, wpool, cw, attn_norm, b_gate, pool_scale, q_norm, k_norm, sinks,
                      ffn_norm, conv_b, late, early, rest, dev):
    s = xs.shape[0]
    tm = min(512, s)
    tk = min(2048, s)
    inv_freq = ROPE_THETA ** (-jnp.arange(0, ROPE_DIM, 2, dtype=F32) / ROPE_DIM)
    lane = jnp.arange(2 * HEAD) % HEAD
    invf = jnp.where(lane < ROPE_DIM, inv_freq[lane % (ROPE_DIM // 2)], 0.0).reshape(1, 2 * HEAD)
    wq = jnp.tile(q_norm, (1, N_Q))
    wk = jnp.tile(k_norm, (1, N_KV))
    head_of = jnp.arange(Q_W) // HEAD
    bd = (head_of[:, None] == head_of[None, :]).astype(BF)
    sink = sinks[0]

    h1, u, qkv, gates = _attn_in_proj(xs, attn_norm, win, b_gate, tm, deps=late.first)
    qh, kh, vh = _qk_prep(qkv, pos, wq, wk, invf, bd, tm)
    apool, pooled = _pool_fwd(u, wpool, pool_scale, min(1024, s), deps=(qh,))
    battn = _attn_fwd(qh, kh, vh, sink, deps=late.middle(apool))
    wup, wout, wdown = late.last(battn)
    wout = wout.reshape(D, D)
    wdown = wdown.reshape(D_FF, D)
    mix, y1, h2 = _mix_out_proj(apool, battn, gates, xs, wout, ffn_norm, tm)
    pre_g, pre_v, up_g, up_v, act = _ffn_up(h2, wup, cw, conv_b, tm)
    dy2, dy2b, loss_acc = _ffn_down_loss(act, wdown, y1, tgt, tm)

    d_wdown = _grad_matmul(act, dy2b, 512, tk, "grad_w_down")
    dp_g, dp_v, dcw_g, dcw_v, dcb_g, dcb_v = _ffn_act_bwd(dy2b, wdown, up_g, up_v, pre_g, pre_v, cw, tm)
    d_wup = _grad_matmul(h2, dp_g, UP_SHARD, tk, "grad_w_up_gate", lead=N_CHIPS)
    d_wup = _grad_matmul(h2, dp_v, UP_SHARD, tk, "grad_w_up_value", lead=N_CHIPS, prev=d_wup, lead_off=2)
    token = early.first([d_wdown.reshape(N_CHIPS, D_FF // N_CHIPS, D), d_wup])
    dy1, d_ffn_norm = _ffn_up_bwd(dp_g, dp_v, wup, y1, dy2, ffn_norm, tm, deps=token)
    token = early.second(dy1)
    da, db, dzg, d_bgate, d_wout = _out_proj_bwd(dy1, wout, apool, battn, gates, mix, tm, deps=token)
    du, d_wpool, d_pscale, d_win_pool = _pool_bwd(da, pooled, wpool, pool_scale, h1, min(1024, s))
    dqh, dkh, dvh, dsink = _attn_bwd(qh, kh, vh, sink, db)
    token = early.third(dqh)
    dzq, d_qn, d_kn = _qk_prep_bwd(dqh, dkh, dvh, qkv, pos, wq, wk, invf, bd, tm, deps=token)
    d_win_t = jnp.concatenate([
        d_win_pool,
        _grad_matmul(dzq, h1, D, tk, "grad_w_in_qkv"),
        _grad_matmul(dzg, h1, D, tk, "grad_w_in_gates")], axis=0)
    token = rest.first([
        d_win_t.reshape(N_CHIPS, IN_W // N_CHIPS, D),
        d_wout.reshape(N_CHIPS, D // N_CHIPS, D),
        d_wpool.reshape(4, N_CHIPS, 64, POOL_GROUP).transpose(1, 0, 2, 3).reshape(N_CHIPS, 4 * 64, POOL_GROUP)])
    small = _pack_small(loss_acc, d_bgate, d_pscale, d_qn, d_kn, dsink, d_ffn_norm, dcb_g, dcb_v, dcw_g, dcw_v, dev)
    token = rest.second(token[0] if token else None, small=small)
    grad_x, d_attn_norm = _in_proj_bwd(du, dzq, dzg, win, xs, attn_norm, dy1, tm, deps=token)
    return grad_x, d_attn_norm, small


def kernel(x, positions, attn_norm, w_in, b_gate, w_pool, pool_scale, q_norm, k_norm, sinks, w_out, ffn_norm, w_up, conv_w, conv_b, w_down, loss_target, m_attn_norm, m_w_in, m_b_gate, m_w_pool, m_pool_scale, m_q_norm, m_k_norm, m_sinks, m_w_out, m_ffn_norm, m_w_up, m_conv_w, m_conv_b, m_w_down, v_attn_norm, v_w_in, v_b_gate, v_w_pool, v_pool_scale, v_q_norm, v_k_norm, v_sinks, v_w_out, v_ffn_norm, v_w_up, v_conv_w, v_conv_b, v_w_down):
    s = x.shape[1]
    xs = x[0]
    tgt = loss_target[0]
    pos = positions[0].reshape(s, 1)
    cx, cy, cc = lax.axis_index("x"), lax.axis_index("y"), lax.axis_index("c")
    chip = 2 * cx + cy

    dev_arr = (2 * chip + cc).reshape(1).astype(jnp.int32)
    place = jnp.stack([chip, cc]).astype(jnp.int32)

    g_in, g_pool, g_cw, *own_late = _all_gather_weights(
        [jnp.swapaxes(w_in[0], 0, 1), w_pool[0].reshape(4 * 64, POOL_GROUP)], [conv_w[0]],
        [w_up[0], w_out[0], w_down[0]])
    win = g_in.reshape(IN_W, D)
    wpool = g_pool.reshape(N_CHIPS, 4, 64, POOL_GROUP).transpose(1, 0, 2, 3).reshape(4, POOL_GROUP, POOL_GROUP)
    late = _LateWeights(own_late)
    early = _GradReduce("early", place, ["w_down", "w_up"], [176, 256])
    rest = _GradReduce("rest", place, ["w_in", "w_out", "w_pool"], [272, 128, 128])

    grad_x, d_attn_norm, _ = _forward_backward(
        xs, pos, tgt, win, wpool, g_cw, attn_norm, b_gate, pool_scale, q_norm, k_norm, sinks, ffn_norm, conv_b,
        late, early, rest, dev_arr)

    def two_d(a):
        return a.reshape(-1, a.shape[-1])

    def update(nm, w, g, m, v, tr, deps=()):
        res = _adamw(two_d(w), g, two_d(m), two_d(v), tr, "adamw_" + nm, deps=deps)
        return [r.reshape(w.shape) for r in res]

    attn_stack = _into_slice(d_attn_norm, dev_arr, N_DEV, 1, F32, "own_attn_norm")
    g_wdown, g_wup = early.last(grad_x)
    big_out = {"w_up": update("w_up", w_up, g_wup, m_w_up, v_w_up, 256)}
    big_out["w_down"] = update("w_down", w_down, g_wdown, m_w_down, v_w_down, 176, deps=(big_out["w_up"][1],))
    token = rest.third(big_out["w_down"][1], small=attn_stack)
    g_win_t, g_wout, g_wpool = rest.last(token[0])
    small_out, g_convw_all, loss = _small_update(rest.small_chips, rest.small_swap, {
        "attn_norm": (attn_norm, m_attn_norm, v_attn_norm), "b_gate": (b_gate, m_b_gate, v_b_gate),
        "pool_scale": (pool_scale, m_pool_scale, v_pool_scale), "q_norm": (q_norm, m_q_norm, v_q_norm),
        "k_norm": (k_norm, m_k_norm, v_k_norm), "sinks": (sinks, m_sinks, v_sinks),
        "ffn_norm": (ffn_norm, m_ffn_norm, v_ffn_norm), "conv_b": (conv_b, m_conv_b, v_conv_b)})
    g_convw = lax.dynamic_slice_in_dim(g_convw_all, chip * UP_SHARD, UP_SHARD, axis=1)
    small_out["conv_w"] = update("conv_w", conv_w, g_convw, m_conv_w, v_conv_w, 3)
    flip = lambda a: jnp.swapaxes(a[0], 0, 1)
    res = _adamw(flip(w_in), g_win_t, flip(m_w_in), flip(v_w_in), 272, "adamw_w_in")
    big_out["w_in"] = [jnp.swapaxes(r, 0, 1)[None] for r in res]
    big_out["w_out"] = update("w_out", w_out, g_wout, m_w_out, v_w_out, 128)
    big_out["w_pool"] = update("w_pool", w_pool, g_wpool, m_w_pool, v_w_pool, 128)

    order = ["attn_norm", "w_in", "b_gate", "w_pool", "pool_scale", "q_norm", "k_norm", "sinks", "w_out",
             "ffn_norm", "w_up", "conv_w", "conv_b", "w_down"]
    allout = {**big_out, **small_out}
    outs = [loss.reshape(()), grad_x[None]]
    for k in range(4):
        outs += [allout[nm][k] for nm in order]
    return tuple(outs)
```

```python
import functools

import jax
import jax.numpy as jnp
from jax import lax
from jax.experimental import pallas as pl
from jax.experimental.pallas import tpu as pltpu

D = 1024
D_FF = 2816
HEAD = 64
N_Q = 16
N_KV = 2
GQA = 8
BLK = 128
ROPE_DIM = 16
ROPE_THETA = 500000.0
POOL_GROUP = 256
Q_W = 1024
KV_W = 128
QKV_W = Q_W + 2 * KV_W
IN_W = 4352
UP_SHARD = 1408
EPS = 1e-6
N_CHIPS = 4
N_DEV = 8

LR = 0.001
B1 = 0.9
B2 = 0.999
ADAM_EPS = 1e-08
WD = 0.01
STEP = 10

BF = jnp.bfloat16
F32 = jnp.float32
MESH = pl.DeviceIdType.MESH
VMEM_LIMIT_MB = 56


def _cp(sem, vmem_mb=VMEM_LIMIT_MB):
    return pltpu.CompilerParams(dimension_semantics=sem, vmem_limit_bytes=vmem_mb << 20)


def _full(shape):
    nd = len(shape)
    return pl.BlockSpec(shape, lambda *_: (0,) * nd)


def _sds(shape, dtype):
    return jax.ShapeDtypeStruct(shape, dtype)


def _after(body, n_in, deps):
    nd = len(deps)
    if nd == 0:
        return body

    def ordered(*refs):
        return body(*refs[:n_in], *refs[n_in + nd:])

    return ordered


def _any_specs(deps):
    return [pl.BlockSpec(memory_space=pl.ANY)] * len(deps)


def _nt(a, b):
    return lax.dot_general(a, b, (((1,), (1,)), ((), ())), preferred_element_type=F32)


def _tn(a, b):
    return lax.dot_general(a, b, (((0,), (0,)), ((), ())), preferred_element_type=F32)


def _mm(a, b):
    return jnp.dot(a, b, preferred_element_type=F32)


def _head_sum(v, bd):
    return _mm(v.astype(BF), bd)


def _rope_tables(pos_ref, invf_ref):
    ang = pos_ref[...].astype(F32) * invf_ref[...]
    cos = jnp.cos(ang)
    sin = jnp.sin(ang)
    lane = lax.broadcasted_iota(jnp.int32, (1, 2 * HEAD), 1) % HEAD
    sa = jnp.where(lane < ROPE_DIM // 2, -sin, 0.0)
    sb = jnp.where(lane < ROPE_DIM // 2, 0.0, jnp.where(lane < ROPE_DIM, sin, 0.0))
    return cos, sa, sb


def _tile_lanes(t, reps):
    return t if reps == 1 else jnp.tile(t, (1, reps))


def _rope(v, cos, sa, sb):
    w = v.shape[1]
    reps = w // (2 * HEAD)
    half = ROPE_DIM // 2
    return (v * _tile_lanes(cos, reps) + pltpu.roll(v, w - half, 1) * _tile_lanes(sa, reps)
            + pltpu.roll(v, half, 1) * _tile_lanes(sb, reps))


def _rope_t(dy, cos, sa, sb):
    w = dy.shape[1]
    reps = w // (2 * HEAD)
    half = ROPE_DIM // 2
    return (dy * _tile_lanes(cos, reps) + pltpu.roll(dy * _tile_lanes(sa, reps), half, 1)
            + pltpu.roll(dy * _tile_lanes(sb, reps), w - half, 1))


def _attn_in_proj(x, g1, w_in, b_gate, tm, deps=()):
    s = x.shape[0]

    def body(x_ref, g_ref, w_ref, b_ref, h_ref, u_ref, qkv_ref, gate_ref):
        xv = x_ref[...]
        r = lax.rsqrt(jnp.mean(xv * xv, axis=-1, keepdims=True) + EPS)
        h = (xv * r * g_ref[...]).astype(BF)
        h_ref[...] = h
        u_ref[...] = _nt(h, w_ref[0:D, :])
        qkv_ref[...] = _nt(h, w_ref[D:D + QKV_W, :])
        gate_ref[...] = jax.nn.sigmoid(_nt(h, w_ref[D + QKV_W:IN_W, :]) + b_ref[...]).astype(BF)

    row = lambda w: pl.BlockSpec((tm, w), lambda i: (i, 0))
    return pl.pallas_call(
        _after(body, 4, deps), grid=(s // tm,),
        in_specs=[row(D), _full((1, D)), _full((IN_W, D)), _full((1, 2 * D))] + _any_specs(deps),
        out_specs=[row(D), row(D), row(QKV_W), row(2 * D)],
        out_shape=[_sds((s, D), BF), _sds((s, D), F32), _sds((s, QKV_W), F32), _sds((s, 2 * D), BF)],
        compiler_params=_cp(("parallel",)), name="attn_in_proj")(x, g1, w_in, b_gate, *deps)


def _qk_prep(qkv, pos, wq, wk, invf, bd, tm):
    s = qkv.shape[0]

    def body(qkv_ref, pos_ref, wq_ref, wk_ref, invf_ref, bd_ref, qh_ref, kh_ref, vh_ref):
        cos, sa, sb = _rope_tables(pos_ref, invf_ref)
        q = qkv_ref[:, 0:Q_W]
        k = qkv_ref[:, Q_W:Q_W + KV_W]
        v = qkv_ref[:, Q_W + KV_W:QKV_W]
        rq = lax.rsqrt(_head_sum(q * q, bd_ref[...]) * (1.0 / HEAD) + EPS)
        qr = _rope(q * rq * wq_ref[...], cos, sa, sb) * (HEAD ** -0.5)
        rk = lax.rsqrt(_head_sum(k * k, bd_ref[0:KV_W, 0:KV_W]) * (1.0 / HEAD) + EPS)
        kr = _rope(k * rk * wk_ref[...], cos, sa, sb)
        for h in range(N_Q):
            qh_ref[h] = qr[:, HEAD * h:HEAD * (h + 1)].astype(BF)
        for h in range(N_KV):
            kh_ref[h] = kr[:, HEAD * h:HEAD * (h + 1)].astype(BF)
            vh_ref[h] = v[:, HEAD * h:HEAD * (h + 1)].astype(BF)

    heads = lambda n: pl.BlockSpec((n, tm, HEAD), lambda i: (0, i, 0))
    return pl.pallas_call(
        body, grid=(s // tm,),
        in_specs=[pl.BlockSpec((tm, QKV_W), lambda i: (i, 0)), pl.BlockSpec((tm, 1), lambda i: (i, 0)),
                  _full((1, Q_W)), _full((1, KV_W)), _full((1, 2 * HEAD)), _full((Q_W, Q_W))],
        out_specs=[heads(N_Q), heads(N_KV), heads(N_KV)],
        out_shape=[_sds((N_Q, s, HEAD), BF), _sds((N_KV, s, HEAD), BF), _sds((N_KV, s, HEAD), BF)],
        compiler_params=_cp(("parallel",)), name="qk_prep")(qkv, pos, wq, wk, invf, bd)


def _sink_column(sink_ref, kh):
    row_g = lax.broadcasted_iota(jnp.int32, (GQA * BLK, 1), 0) // BLK
    col = jnp.zeros((GQA * BLK, 1), F32)
    for g in range(GQA):
        col = jnp.where(row_g == g, sink_ref[kh * GQA + g], col)
    return col


def _fold_band(band, lower, first=None):
    prev, cur = band[:, 0:BLK], band[:, BLK:2 * BLK]
    if first is not None:
        prev = jnp.where(first, -jnp.inf, prev)
    return jnp.where(lower, cur, prev)


def _unfold_band(x, lower):
    return jnp.concatenate([jnp.where(lower, 0.0, x), jnp.where(lower, x, 0.0)], axis=1)


def _attn_probs(q, k, n, sink_col):
    sc = _nt(q, k)
    qi = lax.broadcasted_iota(jnp.int32, (sc.shape[0], BLK), 0) % BLK
    ki = lax.broadcasted_iota(jnp.int32, (sc.shape[0], BLK), 1)
    lower = ki <= qi
    sc = _fold_band(sc, lower, first=n == 0)
    m = jnp.maximum(jnp.max(sc, axis=-1, keepdims=True), sink_col)
    p = jnp.exp(sc - m)
    es = jnp.exp(sink_col - m)
    inv = 1.0 / (jnp.sum(p, axis=-1, keepdims=True) + es)
    return p * inv, es * inv, lower


def _attn_fwd(qh, kh, vh, sinks, deps=()):
    s = qh.shape[1]
    assert s % (2 * BLK) == 0

    def body(sink_ref, q_ref, kp_ref, kc_ref, vp_ref, vc_ref, o_ref):
        i = pl.program_id(0)
        for sub in range(2):
            for khd in range(N_KV):
                q = q_ref[khd * GQA:(khd + 1) * GQA, sub * BLK:(sub + 1) * BLK].reshape(GQA * BLK, HEAD)
                if sub == 0:
                    k = jnp.concatenate([kp_ref[khd], kc_ref[khd, 0:BLK]], axis=0)
                    v = jnp.concatenate([vp_ref[khd], vc_ref[khd, 0:BLK]], axis=0)
                else:
                    k, v = kc_ref[khd], vc_ref[khd]
                probs, _, lower = _attn_probs(q, k, 2 * i + sub, _sink_column(sink_ref, khd))
                o = _mm(_unfold_band(probs, lower).astype(BF), v)
                for j in range(GQA // 2):
                    c0 = khd * GQA * HEAD + 2 * HEAD * j
                    o_ref[sub * BLK:(sub + 1) * BLK, c0:c0 + 2 * HEAD] = jnp.concatenate(
                        [o[2 * j * BLK:(2 * j + 1) * BLK], o[(2 * j + 1) * BLK:(2 * j + 2) * BLK]],
                        axis=1).astype(BF)

    prev = pl.BlockSpec((N_KV, BLK, HEAD), lambda i: (0, jnp.maximum(2 * i - 1, 0), 0))
    cur = pl.BlockSpec((N_KV, 2 * BLK, HEAD), lambda i: (0, i, 0))
    return pl.pallas_call(
        _after(body, 6, deps), grid=(s // (2 * BLK),),
        in_specs=[pl.BlockSpec(memory_space=pltpu.SMEM),
                  pl.BlockSpec((N_Q, 2 * BLK, HEAD), lambda i: (0, i, 0)), prev, cur, prev, cur] + _any_specs(deps),
        out_specs=pl.BlockSpec((2 * BLK, Q_W), lambda i: (i, 0)),
        out_shape=_sds((s, Q_W), BF),
        compiler_params=_cp(("parallel",)), name="attn_fwd")(sinks, qh, kh, kh, vh, vh, *deps)


def _pool_fwd(u, w_pool, pool_scale, ts, deps=()):
    s = u.shape[0]
    halo = 16

    def body(u_ref, wp_ref, ps_ref, a_ref, pooled_ref, prev):
        g = pl.program_id(0)
        i = pl.program_id(1)

        @pl.when(i == 0)
        def _():
            prev[...] = jnp.zeros_like(prev)

        cur = u_ref[...]
        ext = jnp.concatenate([prev[...], cur], axis=0)
        t = (i * ts + lax.broadcasted_iota(jnp.int32, (ts, 1), 0)).astype(F32)
        for gi in range(4):
            @pl.when(g == gi)
            def _(gi=gi):
                w = 2 << gi
                acc, span = ext, 1
                while span < w:
                    acc = acc + pltpu.roll(acc, span, 0)
                    span *= 2
                inv = 1.0 / jnp.minimum(t + 1.0, float(w))
                pooled = (acc[halo:halo + ts] * inv - cur).astype(BF)
                pooled_ref[...] = pooled
                a_ref[...] = (_mm(pooled, wp_ref[0]) * ps_ref[...]).astype(BF)

        prev[...] = cur[ts - halo:ts]

    col = pl.BlockSpec((ts, POOL_GROUP), lambda g, i: (i, g))
    return pl.pallas_call(
        _after(body, 3, deps), grid=(4, s // ts),
        in_specs=[col, pl.BlockSpec((1, POOL_GROUP, POOL_GROUP), lambda g, i: (g, 0, 0)),
                  pl.BlockSpec((1, POOL_GROUP), lambda g, i: (0, g))] + _any_specs(deps),
        out_specs=[col, col],
        out_shape=[_sds((s, D), BF), _sds((s, D), BF)],
        scratch_shapes=[pltpu.VMEM((halo, POOL_GROUP), F32)],
        compiler_params=_cp(("parallel", "arbitrary")), name="pool_fwd")(u, w_pool, pool_scale, *deps)


def _mix_out_proj(a, b, gates, x, w_out, g2, tm):
    s = x.shape[0]

    def body(a_ref, b_ref, gate_ref, x_ref, w_ref, g_ref, mix_ref, y_ref, h_ref):
        mix = (gate_ref[:, 0:D].astype(F32) * a_ref[...].astype(F32)
               + gate_ref[:, D:2 * D].astype(F32) * b_ref[...].astype(F32)).astype(BF)
        mix_ref[...] = mix
        y = x_ref[...] + _mm(mix, w_ref[...])
        y_ref[...] = y
        r = lax.rsqrt(jnp.mean(y * y, axis=-1, keepdims=True) + EPS)
        h_ref[...] = (y * r * g_ref[...]).astype(BF)

    row = lambda w: pl.BlockSpec((tm, w), lambda i: (i, 0))
    return pl.pallas_call(
        body, grid=(s // tm,),
        in_specs=[row(D), row(D), row(2 * D), row(D), _full((D, D)), _full((1, D))],
        out_specs=[row(D), row(D), row(D)],
        out_shape=[_sds((s, D), BF), _sds((s, D), F32), _sds((s, D), BF)],
        compiler_params=_cp(("parallel",)), name="mix_out_proj")(a, b, gates, x, w_out, g2)


def _ffn_up(h2, w_up, conv_w, conv_b, tm):
    s = h2.shape[0]

    def body(h_ref, wg_ref, wv_ref, cwg_ref, cwv_ref, cbg_ref, cbv_ref,
             preg_ref, prev_ref, upg_ref, upv_ref, act_ref, halog, halov):
        i = pl.program_id(1)

        @pl.when(i == 0)
        def _():
            halog[...] = jnp.zeros_like(halog)
            halov[...] = jnp.zeros_like(halov)

        h = h_ref[...]

        def conv_half(w_ref, cw_ref, cb_ref, halo, pre_ref, up_ref):
            pre = _mm(h, w_ref[0])
            pre_ref[...] = pre.astype(BF)
            ext = jnp.concatenate([halo[...], pre], axis=0)
            cw = cw_ref[0]
            up = cb_ref[...] + cw[0:1] * pltpu.roll(ext, 2, 0)[8:8 + tm]
            up = up + cw[1:2] * pltpu.roll(ext, 1, 0)[8:8 + tm]
            up = up + cw[2:3] * pre
            halo[...] = pre[tm - 8:tm]
            up_ref[...] = up.astype(BF)
            return up

        gate = conv_half(wg_ref, cwg_ref, cbg_ref, halog, preg_ref, upg_ref)
        val = conv_half(wv_ref, cwv_ref, cbv_ref, halov, prev_ref, upv_ref)
        act_ref[...] = (gate * jax.nn.sigmoid(gate) * val).astype(BF)

    tile = pl.BlockSpec((tm, UP_SHARD), lambda j, i: (i, j))
    wspec = lambda off: pl.BlockSpec((1, D, UP_SHARD), lambda j, i: (j + off, 0, 0))
    cwspec = lambda off: pl.BlockSpec((1, 3, UP_SHARD), lambda j, i: (j + off, 0, 0))
    cbspec = lambda off: pl.BlockSpec((1, UP_SHARD), lambda j, i: (0, j + off))
    half = _sds((s, D_FF), BF)
    return pl.pallas_call(
        body, grid=(2, s // tm),
        in_specs=[pl.BlockSpec((tm, D), lambda j, i: (i, 0)), wspec(0), wspec(2), cwspec(0), cwspec(2),
                  cbspec(0), cbspec(2)],
        out_specs=[tile] * 5, out_shape=[half] * 5,
        scratch_shapes=[pltpu.VMEM((8, UP_SHARD), F32), pltpu.VMEM((8, UP_SHARD), F32)],
        compiler_params=_cp(("parallel", "arbitrary")), name="ffn_up")(
            h2, w_up, w_up, conv_w, conv_w, conv_b, conv_b)


def _ffn_down_loss(act, w_down, y1, tgt, tm):
    s = y1.shape[0]

    def body(act_ref, w_ref, y_ref, t_ref, dy_ref, dyb_ref, loss_ref):
        @pl.when(pl.program_id(0) == 0)
        def _():
            loss_ref[...] = jnp.zeros_like(loss_ref)

        e = y_ref[...] + _mm(act_ref[...], w_ref[...]) - t_ref[...]
        dy = e * (1.0 / D)
        dy_ref[...] = dy
        dyb_ref[...] = dy.astype(BF)
        e2 = (e * e).reshape(tm // 8, 8, D).sum(axis=0)
        part = e2[:, 0:128]
        for j in range(1, D // 128):
            part = part + e2[:, 128 * j:128 * (j + 1)]
        loss_ref[...] += part

    row = lambda w: pl.BlockSpec((tm, w), lambda i: (i, 0))
    return pl.pallas_call(
        body, grid=(s // tm,),
        in_specs=[row(D_FF), _full((D_FF, D)), row(D), row(D)],
        out_specs=[row(D), row(D), _full((8, 128))],
        out_shape=[_sds((s, D), F32), _sds((s, D), BF), _sds((8, 128), F32)],
        compiler_params=_cp(("arbitrary",)), name="ffn_down_loss")(act, w_down, y1, tgt)


def _grad_matmul(a, b, tn, tk, name, lead=None, prev=None, lead_off=0):
    s, m = a.shape
    n = b.shape[1]
    nj = n // tn

    def body(*refs):
        a_ref, b_ref = refs[0], refs[1]
        o_ref = refs[-1]
        acc = _tn(a_ref[...], b_ref[...])
        acc = acc if lead is None else acc[None]

        @pl.when(pl.program_id(1) == 0)
        def _():
            o_ref[...] = acc

        @pl.when(pl.program_id(1) > 0)
        def _():
            o_ref[...] += acc

    in_specs = [pl.BlockSpec((tk, m), lambda j, k: (k, 0)), pl.BlockSpec((tk, tn), lambda j, k: (k, j))]
    args = [a, b]
    aliases = {}
    if lead is None:
        out_spec = pl.BlockSpec((m, tn), lambda j, k: (0, j))
        out_shape = _sds((m, n), F32)
    else:
        out_spec = pl.BlockSpec((1, m, tn), lambda j, k: (j + lead_off, 0, 0))
        out_shape = _sds((lead, m, tn), F32)
        if prev is not None:
            in_specs.append(pl.BlockSpec(memory_space=pl.ANY))
            args.append(prev)
            aliases = {2: 0}
    return pl.pallas_call(
        body, grid=(nj, s // tk), in_specs=in_specs, out_specs=out_spec, out_shape=out_shape,
        input_output_aliases=aliases,
        compiler_params=_cp(("parallel", "arbitrary")), name=name)(*args)


def _ffn_act_bwd(dyb, w_down, up_g, up_v, pre_g, pre_v, conv_w, tm):
    s = dyb.shape[0]
    nt = s // tm

    def body(dy_ref, wd_ref, upg_ref, upv_ref, preg_ref, prev_ref, cwg_ref, cwv_ref,
             dpg_ref, dpv_ref, dcwg_ref, dcwv_ref, dcbg_ref, dcbv_ref, nxg, nxv):
        i = pl.program_id(1)

        @pl.when(i == 0)
        def _():
            nxg[...] = jnp.zeros_like(nxg)
            nxv[...] = jnp.zeros_like(nxv)
            dcwg_ref[...] = jnp.zeros_like(dcwg_ref)
            dcwv_ref[...] = jnp.zeros_like(dcwv_ref)
            dcbg_ref[...] = jnp.zeros_like(dcbg_ref)
            dcbv_ref[...] = jnp.zeros_like(dcbv_ref)

        dact = _nt(dy_ref[...], wd_ref[...])
        g = upg_ref[...].astype(F32)
        v = upv_ref[...].astype(F32)
        sg = jax.nn.sigmoid(g)
        d_v = dact * (g * sg)
        d_g = dact * v * (sg * (1.0 + g * (1.0 - sg)))

        def conv_bwd(d_up, nx, pre_ref, cw_ref, dp_ref, dcw_ref, dcb_ref):
            ext = jnp.concatenate([d_up, nx[...]], axis=0)
            s1 = pltpu.roll(ext, tm + 8 - 1, 0)[0:tm]
            s2 = pltpu.roll(ext, tm + 8 - 2, 0)[0:tm]
            cw = cw_ref[0]
            dp_ref[...] = (cw[2:3] * d_up + cw[1:2] * s1 + cw[0:1] * s2).astype(BF)
            nx[...] = d_up[0:8]
            pre = pre_ref[...].astype(F32)
            dcw_ref[0, 0:1, :] += jnp.sum(s2 * pre, axis=0, keepdims=True)
            dcw_ref[0, 1:2, :] += jnp.sum(s1 * pre, axis=0, keepdims=True)
            dcw_ref[0, 2:3, :] += jnp.sum(d_up * pre, axis=0, keepdims=True)
            dcb_ref[...] += jnp.sum(d_up, axis=0, keepdims=True)

        conv_bwd(d_g, nxg, preg_ref, cwg_ref, dpg_ref, dcwg_ref, dcbg_ref)
        conv_bwd(d_v, nxv, prev_ref, cwv_ref, dpv_ref, dcwv_ref, dcbv_ref)

    tile = pl.BlockSpec((tm, UP_SHARD), lambda j, i: (nt - 1 - i, j))
    cwspec = lambda off: pl.BlockSpec((1, 3, UP_SHARD), lambda j, i: (j + off, 0, 0))
    acc_cw = pl.BlockSpec((1, 3, UP_SHARD), lambda j, i: (j, 0, 0))
    acc_cb = pl.BlockSpec((1, UP_SHARD), lambda j, i: (0, j))
    buf = pltpu.VMEM((8, UP_SHARD), F32)
    return pl.pallas_call(
        body, grid=(2, nt),
        in_specs=[pl.BlockSpec((tm, D), lambda j, i: (nt - 1 - i, 0)),
                  pl.BlockSpec((UP_SHARD, D), lambda j, i: (j, 0)),
                  tile, tile, tile, tile, cwspec(0), cwspec(2)],
        out_specs=[tile, tile, acc_cw, acc_cw, acc_cb, acc_cb],
        out_shape=[_sds((s, D_FF), BF), _sds((s, D_FF), BF), _sds((2, 3, UP_SHARD), F32),
                   _sds((2, 3, UP_SHARD), F32), _sds((1, D_FF), F32), _sds((1, D_FF), F32)],
        scratch_shapes=[buf, buf],
        compiler_params=_cp(("parallel", "arbitrary")), name="ffn_act_bwd")(
            dyb, w_down, up_g, up_v, pre_g, pre_v, conv_w, conv_w)


def _rms_bwd(dh, y, g):
    r = lax.rsqrt(jnp.mean(y * y, axis=-1, keepdims=True) + EPS)
    n = y * r
    dn = dh * g
    return r * (dn - n * jnp.mean(dn * n, axis=-1, keepdims=True)), dh * n


def _ffn_up_bwd(dp_g, dp_v, w_up, y1, dy2, g2, tm, deps=()):
    s = y1.shape[0]

    def body(dg_ref, dv_ref, w_ref, y_ref, dy2_ref, g_ref, dy1_ref, dgn_ref):
        @pl.when(pl.program_id(0) == 0)
        def _():
            dgn_ref[...] = jnp.zeros_like(dgn_ref)

        dh = _nt(dg_ref[:, 0:UP_SHARD], w_ref[0])
        dh = dh + _nt(dg_ref[:, UP_SHARD:D_FF], w_ref[1])
        dh = dh + _nt(dv_ref[:, 0:UP_SHARD], w_ref[2])
        dh = dh + _nt(dv_ref[:, UP_SHARD:D_FF], w_ref[3])
        dy, dgn = _rms_bwd(dh, y_ref[...], g_ref[...])
        dy1_ref[...] = dy2_ref[...] + dy
        dgn_ref[...] += jnp.sum(dgn, axis=0, keepdims=True)

    row = lambda w: pl.BlockSpec((tm, w), lambda i: (i, 0))
    return pl.pallas_call(
        _after(body, 6, deps), grid=(s // tm,),
        in_specs=[row(D_FF), row(D_FF), _full((4, D, UP_SHARD)), row(D), row(D), _full((1, D))] + _any_specs(deps),
        out_specs=[row(D), _full((1, D))],
        out_shape=[_sds((s, D), F32), _sds((1, D), F32)],
        compiler_params=_cp(("arbitrary",)), name="ffn_up_bwd")(dp_g, dp_v, w_up, y1, dy2, g2, *deps)


def _out_proj_bwd(dy1, w_out, a, b, gates, mix, tm, deps=()):
    s = dy1.shape[0]

    def body(dy_ref, w_ref, a_ref, b_ref, gate_ref, mix_ref, da_ref, db_ref, dzg_ref, dbg_ref, dw_ref):
        @pl.when(pl.program_id(0) == 0)
        def _():
            dbg_ref[...] = jnp.zeros_like(dbg_ref)
            dw_ref[...] = jnp.zeros_like(dw_ref)

        dyb = dy_ref[...].astype(BF)
        dmix = _nt(dyb, w_ref[...])
        gp = gate_ref[:, 0:D].astype(F32)
        ga = gate_ref[:, D:2 * D].astype(F32)
        da_ref[...] = (dmix * gp).astype(BF)
        db_ref[...] = (dmix * ga).astype(BF)
        dzp = dmix * a_ref[...].astype(F32) * (gp * (1.0 - gp))
        dza = dmix * b_ref[...].astype(F32) * (ga * (1.0 - ga))
        dzg_ref[:, 0:D] = dzp.astype(BF)
        dzg_ref[:, D:2 * D] = dza.astype(BF)
        dbg_ref[:, 0:D] += jnp.sum(dzp, axis=0, keepdims=True)
        dbg_ref[:, D:2 * D] += jnp.sum(dza, axis=0, keepdims=True)
        dw_ref[...] += _tn(mix_ref[...], dyb)

    row = lambda w: pl.BlockSpec((tm, w), lambda i: (i, 0))
    return pl.pallas_call(
        _after(body, 6, deps), grid=(s // tm,),
        in_specs=[row(D), _full((D, D)), row(D), row(D), row(2 * D), row(D)] + _any_specs(deps),
        out_specs=[row(D), row(D), row(2 * D), _full((1, 2 * D)), _full((D, D))],
        out_shape=[_sds((s, D), BF), _sds((s, D), BF), _sds((s, 2 * D), BF), _sds((1, 2 * D), F32),
                   _sds((D, D), F32)],
        compiler_params=_cp(("arbitrary",)), name="out_proj_bwd")(dy1, w_out, a, b, gates, mix, *deps)


def _pool_bwd(da, pooled, w_pool, pool_scale, h1, ts):
    s = da.shape[0]
    nt = s // ts
    halo = 16

    def body(da_ref, pooled_ref, wp_ref, ps_ref, h_ref, du_ref, dwp_ref, dps_ref, dwi_ref, nxt):
        g = pl.program_id(0)
        i = pl.program_id(1)
        ti = nt - 1 - i

        @pl.when(i == 0)
        def _():
            nxt[...] = jnp.zeros_like(nxt)
            dwp_ref[...] = jnp.zeros_like(dwp_ref)
            dps_ref[...] = jnp.zeros_like(dps_ref)
            dwi_ref[...] = jnp.zeros_like(dwi_ref)

        pooled = pooled_ref[...]
        dav = da_ref[...].astype(F32)
        dps_ref[...] += jnp.sum(dav * _mm(pooled, wp_ref[0]), axis=0, keepdims=True)
        dm = (dav * ps_ref[...]).astype(BF)
        dwp_ref[0] += _tn(pooled, dm)
        dpool = _nt(dm, wp_ref[0])
        t = (ti * ts + lax.broadcasted_iota(jnp.int32, (ts, 1), 0)).astype(F32)
        for gi in range(4):
            @pl.when(g == gi)
            def _(gi=gi):
                w = 2 << gi
                e = dpool * (1.0 / jnp.minimum(t + 1.0, float(w)))
                acc, span = jnp.concatenate([e, nxt[...]], axis=0), 1
                while span < w:
                    acc = acc + pltpu.roll(acc, ts + halo - span, 0)
                    span *= 2
                du = (acc[0:ts] - dpool).astype(BF)
                du_ref[...] = du
                dwi_ref[...] += _tn(du, h_ref[...])
                nxt[...] = e[0:halo]

    col = pl.BlockSpec((ts, POOL_GROUP), lambda g, i: (nt - 1 - i, g))
    return pl.pallas_call(
        body, grid=(4, nt),
        in_specs=[col, col, pl.BlockSpec((1, POOL_GROUP, POOL_GROUP), lambda g, i: (g, 0, 0)),
                  pl.BlockSpec((1, POOL_GROUP), lambda g, i: (0, g)),
                  pl.BlockSpec((ts, D), lambda g, i: (nt - 1 - i, 0))],
        out_specs=[col, pl.BlockSpec((1, POOL_GROUP, POOL_GROUP), lambda g, i: (g, 0, 0)),
                   pl.BlockSpec((1, POOL_GROUP), lambda g, i: (0, g)),
                   pl.BlockSpec((POOL_GROUP, D), lambda g, i: (g, 0))],
        out_shape=[_sds((s, D), BF), _sds((4, POOL_GROUP, POOL_GROUP), F32), _sds((1, D), F32), _sds((D, D), F32)],
        scratch_shapes=[pltpu.VMEM((halo, POOL_GROUP), F32)],
        compiler_params=_cp(("parallel", "arbitrary")), name="pool_bwd")(da, pooled, w_pool, pool_scale, h1)


def _attn_bwd(qh, kh, vh, sinks, db, deps=()):
    s = qh.shape[1]
    assert s % (2 * BLK) == 0
    pairs = s // (2 * BLK)

    def body(sink_ref, q_ref, kp_ref, kc_ref, vp_ref, vc_ref, do_ref,
             dq_ref, dke_ref, dko_ref, dve_ref, dvo_ref, dsink_ref, ck, cv):
        i = pl.program_id(0)

        @pl.when(i == 0)
        def _():
            ck[...] = jnp.zeros_like(ck)
            cv[...] = jnp.zeros_like(cv)
            dsink_ref[...] = jnp.zeros_like(dsink_ref)

        @pl.when(i < pairs)
        def _():
            for khd in range(N_KV):
                c0 = khd * GQA * HEAD
                band = []
                for sub in range(2):
                    q = q_ref[khd * GQA:(khd + 1) * GQA, sub * BLK:(sub + 1) * BLK].reshape(GQA * BLK, HEAD)
                    if sub == 0:
                        k = jnp.concatenate([kp_ref[khd], kc_ref[khd, 0:BLK]], axis=0)
                        v = jnp.concatenate([vp_ref[khd], vc_ref[khd, 0:BLK]], axis=0)
                    else:
                        k, v = kc_ref[khd], vc_ref[khd]
                    dov = do_ref[sub * BLK:(sub + 1) * BLK, :]
                    do = jnp.concatenate([dov[:, c0 + HEAD * g:c0 + HEAD * (g + 1)] for g in range(GQA)],
                                         axis=0).astype(BF)
                    probs, psink, lower = _attn_probs(q, k, 2 * i + sub, _sink_column(sink_ref, khd))
                    dp = _fold_band(_nt(do, v), lower)
                    delta = jnp.sum(probs * dp, axis=-1, keepdims=True)
                    ds = _unfold_band(probs * (dp - delta), lower).astype(BF)
                    dq_ref[khd * GQA:(khd + 1) * GQA, sub * BLK:(sub + 1) * BLK] = _mm(ds, k).reshape(
                        GQA, BLK, HEAD)
                    band.append((_tn(ds, q), _tn(_unfold_band(probs, lower).astype(BF), do)))
                    dsk = psink * delta
                    lane = lax.broadcasted_iota(jnp.int32, (1, 128), 1)
                    acc = jnp.zeros((1, 128), F32)
                    for g in range(GQA):
                        acc = acc - jnp.where(lane == khd * GQA + g,
                                              jnp.sum(dsk[g * BLK:(g + 1) * BLK], axis=0, keepdims=True), 0.0)
                    dsink_ref[...] += acc
                (dk0, dv0), (dk1, dv1) = band
                dko_ref[khd] = ck[khd] + dk0[0:BLK]
                dvo_ref[khd] = cv[khd] + dv0[0:BLK]
                dke_ref[khd] = dk0[BLK:2 * BLK] + dk1[0:BLK]
                dve_ref[khd] = dv0[BLK:2 * BLK] + dv1[0:BLK]
                ck[khd] = dk1[BLK:2 * BLK]
                cv[khd] = dv1[BLK:2 * BLK]

        @pl.when(i == pairs)
        def _():
            dko_ref[...] = ck[...]
            dvo_ref[...] = cv[...]

    last = pairs - 1
    at = lambda i: jnp.minimum(i, last)
    prev = pl.BlockSpec((N_KV, BLK, HEAD), lambda i: (0, jnp.maximum(2 * at(i) - 1, 0), 0))
    cur = pl.BlockSpec((N_KV, 2 * BLK, HEAD), lambda i: (0, at(i), 0))
    even = pl.BlockSpec((N_KV, BLK, HEAD), lambda i: (0, at(i), 0))
    odd = pl.BlockSpec((N_KV, BLK, HEAD), lambda i: (0, jnp.maximum(i - 1, 0), 0))
    halfkv = _sds((N_KV, s // 2, HEAD), F32)
    dq, dke, dko, dve, dvo, dsink = pl.pallas_call(
        _after(body, 7, deps), grid=(pairs + 1,),
        in_specs=[pl.BlockSpec(memory_space=pltpu.SMEM),
                  pl.BlockSpec((N_Q, 2 * BLK, HEAD), lambda i: (0, at(i), 0)),
                  prev, cur, prev, cur,
                  pl.BlockSpec((2 * BLK, Q_W), lambda i: (at(i), 0))] + _any_specs(deps),
        out_specs=[pl.BlockSpec((N_Q, 2 * BLK, HEAD), lambda i: (0, at(i), 0)), even, odd, even, odd,
                   _full((1, 128))],
        out_shape=[_sds((N_Q, s, HEAD), F32), halfkv, halfkv, halfkv, halfkv, _sds((1, 128), F32)],
        scratch_shapes=[pltpu.VMEM((N_KV, BLK, HEAD), F32), pltpu.VMEM((N_KV, BLK, HEAD), F32)],
        compiler_params=_cp(("arbitrary",)), name="attn_bwd")(sinks, qh, kh, kh, vh, vh, db, *deps)

    def interleave(ev, od):
        both = jnp.stack([ev.reshape(N_KV, pairs, BLK, HEAD), od.reshape(N_KV, pairs, BLK, HEAD)], axis=2)
        return both.reshape(N_KV, s, HEAD)

    return dq, interleave(dke, dko), interleave(dve, dvo), dsink


def _qk_prep_bwd(dqh, dkh, dvh, qkv, pos, wq, wk, invf, bd, tm, deps=()):
    s = qkv.shape[0]

    def fold_heads(row):
        out = row[:, 0:HEAD]
        for h in range(1, row.shape[1] // HEAD):
            out = out + row[:, HEAD * h:HEAD * (h + 1)]
        return out

    def body(dq_ref, dk_ref, dv_ref, qkv_ref, pos_ref, wq_ref, wk_ref, invf_ref, bd_ref,
             dz_ref, dwq_ref, dwk_ref):
        @pl.when(pl.program_id(0) == 0)
        def _():
            dwq_ref[...] = jnp.zeros_like(dwq_ref)
            dwk_ref[...] = jnp.zeros_like(dwk_ref)

        cos, sa, sb = _rope_tables(pos_ref, invf_ref)

        def norm_rope_bwd(dy, xin, w, bdm):
            dn = _rope_t(dy, cos, sa, sb)
            r = lax.rsqrt(_head_sum(xin * xin, bdm) * (1.0 / HEAD) + EPS)
            nh = xin * r
            gw = dn * w
            dx = r * (gw - nh * (_head_sum(gw * nh, bdm) * (1.0 / HEAD)))
            return dx, fold_heads(jnp.sum(dn * nh, axis=0, keepdims=True))

        dq = jnp.concatenate([dq_ref[h] for h in range(N_Q)], axis=1) * (HEAD ** -0.5)
        dk = jnp.concatenate([dk_ref[h] for h in range(N_KV)], axis=1)
        dxq, dwq = norm_rope_bwd(dq, qkv_ref[:, 0:Q_W], wq_ref[...], bd_ref[...])
        dxk, dwk = norm_rope_bwd(dk, qkv_ref[:, Q_W:Q_W + KV_W], wk_ref[...], bd_ref[0:KV_W, 0:KV_W])
        dz_ref[:, 0:Q_W] = dxq.astype(BF)
        dz_ref[:, Q_W:Q_W + KV_W] = dxk.astype(BF)
        dz_ref[:, Q_W + KV_W:QKV_W] = jnp.concatenate([dv_ref[h] for h in range(N_KV)], axis=1).astype(BF)
        dwq_ref[...] += dwq
        dwk_ref[...] += dwk

    heads = lambda n: pl.BlockSpec((n, tm, HEAD), lambda i: (0, i, 0))
    return pl.pallas_call(
        _after(body, 9, deps), grid=(s // tm,),
        in_specs=[heads(N_Q), heads(N_KV), heads(N_KV), pl.BlockSpec((tm, QKV_W), lambda i: (i, 0)),
                  pl.BlockSpec((tm, 1), lambda i: (i, 0)), _full((1, Q_W)), _full((1, KV_W)),
                  _full((1, 2 * HEAD)), _full((Q_W, Q_W))] + _any_specs(deps),
        out_specs=[pl.BlockSpec((tm, QKV_W), lambda i: (i, 0)), _full((1, HEAD)), _full((1, HEAD))],
        out_shape=[_sds((s, QKV_W), BF), _sds((1, HEAD), F32), _sds((1, HEAD), F32)],
        compiler_params=_cp(("arbitrary",)), name="qk_prep_bwd")(
            dqh, dkh, dvh, qkv, pos, wq, wk, invf, bd, *deps)


def _in_proj_bwd(du, dzq, dzg, w_in, x, g1, dy1, tm, deps=()):
    s = x.shape[0]

    def body(du_ref, dzq_ref, dzg_ref, w_ref, x_ref, g_ref, dy_ref, gx_ref, dgn_ref):
        @pl.when(pl.program_id(0) == 0)
        def _():
            dgn_ref[...] = jnp.zeros_like(dgn_ref)

        dh = _mm(du_ref[...], w_ref[0:D, :])
        dh = dh + _mm(dzq_ref[...], w_ref[D:D + QKV_W, :])
        dh = dh + _mm(dzg_ref[...], w_ref[D + QKV_W:IN_W, :])
        dx, dgn = _rms_bwd(dh, x_ref[...], g_ref[...])
        gx_ref[...] = dy_ref[...] + dx
        dgn_ref[...] += jnp.sum(dgn, axis=0, keepdims=True)

    row = lambda w: pl.BlockSpec((tm, w), lambda i: (i, 0))
    return pl.pallas_call(
        _after(body, 7, deps), grid=(s // tm,),
        in_specs=[row(D), row(QKV_W), row(2 * D), _full((IN_W, D)), row(D), _full((1, D)), row(D)] + _any_specs(deps),
        out_specs=[row(D), _full((1, D))],
        out_shape=[_sds((s, D), F32), _sds((1, D), F32)],
        compiler_params=_cp(("arbitrary",)), name="in_proj_bwd")(du, dzq, dzg, w_in, x, g1, dy1, *deps)


def _adamw_step(w, g, m, v):
    mn = B1 * m + (1.0 - B1) * g
    vn = B2 * v + (1.0 - B2) * (g * g)
    m_hat = mn / (1.0 - B1 ** STEP)
    v_hat = vn / (1.0 - B2 ** STEP)
    return -LR * (m_hat / (jnp.sqrt(v_hat) + ADAM_EPS) + WD * w), mn, vn


SMALL_ROWS = 16
SMALL_COLS = D_FF
SMALL_AT = {"b_gate": (1, 2 * D), "pool_scale": (2, D), "q_norm": (3, HEAD),
            "k_norm": (4, HEAD), "sinks": (5, N_Q), "ffn_norm": (6, D)}
SMALL_LOSS_ROW = 0
SMALL_CONV_B_ROW = 7
SMALL_CONV_W_ROW = 9


def _pack_small(loss_acc, d_bgate, d_pscale, d_qn, d_kn, dsink, d_ffn_norm, dcb_g, dcb_v, dcw_g, dcw_v, dev):
    def body(k_ref, ls_ref, bg_ref, ps_ref, qn_ref, kn_ref, sk_ref, fn_ref, cbg_ref, cbv_ref, cwg_ref, cwv_ref,
             o_ref):
        o_ref[...] = jnp.zeros_like(o_ref)
        o_ref[0, SMALL_LOSS_ROW:SMALL_LOSS_ROW + 1, 0:128] = jnp.sum(ls_ref[...], axis=0, keepdims=True)
        for nm, ref in (("b_gate", bg_ref), ("pool_scale", ps_ref), ("q_norm", qn_ref),
                        ("k_norm", kn_ref), ("ffn_norm", fn_ref)):
            row, n = SMALL_AT[nm]
            o_ref[0, row:row + 1, 0:n] = ref[...]
        row, _ = SMALL_AT["sinks"]
        o_ref[0, row:row + 1, 0:128] = sk_ref[...]
        o_ref[0, SMALL_CONV_B_ROW:SMALL_CONV_B_ROW + 1, :] = cbg_ref[...]
        o_ref[0, SMALL_CONV_B_ROW + 1:SMALL_CONV_B_ROW + 2, :] = cbv_ref[...]
        for k in range(3):
            row = SMALL_CONV_W_ROW + 2 * k
            for half in range(2):
                o_ref[0, row:row + 1, half * UP_SHARD:(half + 1) * UP_SHARD] = cwg_ref[half, k:k + 1, :]
                o_ref[0, row + 1:row + 2, half * UP_SHARD:(half + 1) * UP_SHARD] = cwv_ref[half, k:k + 1, :]

    args = [loss_acc, d_bgate, d_pscale, d_qn, d_kn, dsink, d_ffn_norm, dcb_g, dcb_v, dcw_g, dcw_v]
    grid_spec = pltpu.PrefetchScalarGridSpec(
        num_scalar_prefetch=1, grid=(1,),
        in_specs=[pl.BlockSpec(a.shape, functools.partial(lambda nd, i, k: (0,) * nd, a.ndim)) for a in args],
        out_specs=pl.BlockSpec((1, SMALL_ROWS, SMALL_COLS), lambda i, k: (k[0], 0, 0)))
    return pl.pallas_call(body, grid_spec=grid_spec, out_shape=_sds((N_DEV, SMALL_ROWS, SMALL_COLS), F32),
                          name="pack_small")(dev, *args)


def _small_update(stack, attn_stack, params):
    names = list(params)

    def body(*refs):
        s_ref, a_ref = refs[0], refs[1]
        ins = refs[2:2 + 3 * len(names)]
        outs = refs[2 + 3 * len(names):]
        tot, tot_a = s_ref[0], a_ref[0]
        for d in range(1, N_DEV):
            tot = tot + s_ref[d]
            tot_a = tot_a + a_ref[d]
        for i, nm in enumerate(names):
            if nm == "attn_norm":
                g = tot_a
            elif nm == "conv_b":
                g = jnp.concatenate([tot[SMALL_CONV_B_ROW:SMALL_CONV_B_ROW + 1, :],
                                     tot[SMALL_CONV_B_ROW + 1:SMALL_CONV_B_ROW + 2, :]], axis=1)
            else:
                row, n = SMALL_AT[nm]
                g = tot[row:row + 1, 0:n]
            delta, mn, vn = _adamw_step(ins[3 * i][...], g, ins[3 * i + 1][...], ins[3 * i + 2][...])
            outs[4 * i][...] = g
            outs[4 * i + 1][...] = delta
            outs[4 * i + 2][...] = mn
            outs[4 * i + 3][...] = vn
        for k in range(3):
            row = SMALL_CONV_W_ROW + 2 * k
            outs[-2][k:k + 1, 0:D_FF] = tot[row:row + 1, :]
            outs[-2][k:k + 1, D_FF:2 * D_FF] = tot[row + 1:row + 2, :]
        outs[-1][...] = jnp.sum(tot[SMALL_LOSS_ROW:SMALL_LOSS_ROW + 1, 0:128], axis=1, keepdims=True) * (0.5 / D)

    flat = [a for nm in names for a in params[nm]]
    out_shape = ([_sds(params[nm][0].shape, F32) for nm in names for _ in range(4)]
                 + [_sds((3, 2 * D_FF), F32), _sds((1, 1), F32)])
    res = pl.pallas_call(body, out_shape=out_shape, name="small_update")(stack, attn_stack, *flat)
    return {nm: list(res[4 * i:4 * i + 4]) for i, nm in enumerate(names)}, res[-2], res[-1]


def _adamw(w, g, m, v, tr, name, deps=()):
    r, c = w.shape

    def body(w_ref, g_ref, m_ref, v_ref, go_ref, d_ref, mo_ref, vo_ref):
        gv = g_ref[...]
        go_ref[...] = gv
        d_ref[...], mo_ref[...], vo_ref[...] = _adamw_step(w_ref[...], gv, m_ref[...], v_ref[...])

    blk = pl.BlockSpec((tr, c), lambda i: (i, 0))
    return pl.pallas_call(
        _after(body, 4, deps), grid=(r // tr,), in_specs=[blk] * 4 + _any_specs(deps), out_specs=[blk] * 4,
        out_shape=[_sds((r, c), F32)] * 4, compiler_params=_cp(("parallel",)), name=name)(w, g, m, v, *deps)


def _place():
    x, y, c = lax.axis_index("x"), lax.axis_index("y"), lax.axis_index("c")
    chips = [(1 - x, y), (x, 1 - y), (1 - x, 1 - y)]
    return x, y, c, chips


def _rows(ref, lead, h, rh):
    sl = pl.ds(pl.multiple_of(h * rh, 16), rh)
    return ref.at[sl, :] if lead is None else ref.at[lead, sl, :]


def _all_gather_weights(halved, whole, placed):
    nh, nw, npl = len(halved), len(whole), len(placed)
    na = nh + nw
    nall = na + npl
    arrays = list(halved) + list(whole) + list(placed)
    out_dtypes = [BF] * nh + [a.dtype for a in whole] + [BF] * npl
    cast_rows = 128

    def body(*refs):
        ins, outs = refs[:nall], refs[nall:2 * nall]
        raw, stage = refs[2 * nall:3 * nall], refs[3 * nall:3 * nall + nh + npl]
        ici_send, ici_recv, fwd_send, fwd_recv, in_sem, loc_sem = refs[3 * nall + nh + npl:]
        x, y, c, chips = _place()
        me = 2 * x + y
        sibling = (x, y, 1 - c)
        loads = [pltpu.make_async_copy(ins[a], raw[a], in_sem.at[a]) for a in range(nall)]
        for cp in loads:
            cp.start()

        def cast(a, dst):
            r = arrays[a].shape[0]
            for r0 in range(0, r, cast_rows):
                r1 = min(r0 + cast_rows, r)
                dst[r0:r1, :] = raw[a][r0:r1, :].astype(BF)

        def ici(a, j, src_chip, src=None):
            if a < nh:
                rh = arrays[a].shape[0] // 2
                dst = _rows(outs[a], src_chip, c, rh)
                src = dst if src is None else _rows(src, None, c, rh)
            else:
                dst = outs[a].at[src_chip]
                src = dst if src is None else src
            return pltpu.make_async_remote_copy(
                src_ref=src, dst_ref=dst, send_sem=ici_send.at[3 * a + j], recv_sem=ici_recv.at[3 * a + j],
                device_id=(*chips[j], c), device_id_type=MESH)

        def fwd(a, j, half):
            rh = arrays[a].shape[0] // 2
            kj = 2 * chips[j][0] + chips[j][1]
            blk = _rows(outs[a], kj, half, rh)
            return pltpu.make_async_remote_copy(
                src_ref=blk, dst_ref=blk, send_sem=fwd_send.at[3 * a + j], recv_sem=fwd_recv.at[3 * a + j],
                device_id=sibling, device_id_type=MESH)

        local, sends = [], []
        for a in range(na):
            loads[a].wait()
            if a < nh:
                cast(a, stage[a])
                own = stage[a]
            else:
                own = raw[a]
            cp = pltpu.make_async_copy(own, outs[a].at[me], loc_sem.at[a])
            cp.start()
            local.append(cp)
            for j in range(3):
                cp = ici(a, j, me, src=own)
                cp.start()
                sends.append(cp)
        for i in range(npl):
            loads[na + i].wait()
            cast(na + i, stage[nh + i])
            cp = pltpu.make_async_copy(stage[nh + i], outs[na + i].at[me], loc_sem.at[na + i])
            cp.start()
            local.append(cp)
        passed = []
        for a in range(na):
            for j in range(3):
                kj = 2 * chips[j][0] + chips[j][1]
                ici(a, j, kj).wait_recv()
                if a < nh:
                    cp = fwd(a, j, c)
                    cp.start()
                    passed.append(cp)
        for a in range(nh):
            for j in range(3):
                fwd(a, j, 1 - c).wait_recv()
        for cp in sends + passed:
            cp.wait_send()
        for cp in local:
            cp.wait()

    any_spec = pl.BlockSpec(memory_space=pl.ANY)
    return pl.pallas_call(
        body, in_specs=[any_spec] * nall, out_specs=[any_spec] * nall,
        out_shape=[_sds((N_CHIPS,) + a.shape, dt) for a, dt in zip(arrays, out_dtypes)],
        scratch_shapes=[pltpu.VMEM(a.shape, a.dtype) for a in arrays]
        + [pltpu.VMEM(a.shape, BF) for a in list(halved) + list(placed)]
        + [pltpu.SemaphoreType.DMA((3 * na,)), pltpu.SemaphoreType.DMA((3 * na,)),
           pltpu.SemaphoreType.DMA((3 * nh,)), pltpu.SemaphoreType.DMA((3 * nh,)),
           pltpu.SemaphoreType.DMA((nall,)), pltpu.SemaphoreType.DMA((nall,))],
        compiler_params=pltpu.CompilerParams(vmem_limit_bytes=VMEM_LIMIT_MB << 20),
        name="all_gather_weights")(*arrays)


def _pair_sum(g, recv, c, tr, name):
    _, r, cols = g.shape
    rh = r // 2
    nr = rh // tr

    def body(c_ref, g_ref, r_ref, o_ref):
        o_ref[...] = (g_ref[...] + r_ref[...]).astype(BF)

    grid_spec = pltpu.PrefetchScalarGridSpec(
        num_scalar_prefetch=1, grid=(N_CHIPS, nr),
        in_specs=[pl.BlockSpec((1, tr, cols), lambda k, i, c_ref: (k, c_ref[0] * nr + i, 0)),
                  pl.BlockSpec((1, tr, cols), lambda k, i, c_ref: (k, i, 0))],
        out_specs=pl.BlockSpec((1, tr, cols), lambda k, i, c_ref: (k, i, 0)))
    return pl.pallas_call(
        body, grid_spec=grid_spec, out_shape=_sds((N_CHIPS, rh, cols), BF),
        compiler_params=_cp(("parallel", "parallel")), name=name)(c, g, recv)


def _chip_sum(g, sib, recv, place, tr, name):
    _, r, cols = g.shape
    rh = r // 2
    nr = rh // tr

    def body(p_ref, g_ref, s_ref, r0_ref, r1_ref, r2_ref, o_ref):
        own = g_ref[0] + s_ref[0]
        o_ref[...] = ((own + r0_ref[0].astype(F32)) + r1_ref[0].astype(F32)) + r2_ref[0].astype(F32)

    rspec = lambda j: pl.BlockSpec((1, tr, cols), lambda i, p: (j, i, 0))
    grid_spec = pltpu.PrefetchScalarGridSpec(
        num_scalar_prefetch=1, grid=(nr,),
        in_specs=[pl.BlockSpec((1, tr, cols), lambda i, p: (p[0], p[1] * nr + i, 0)),
                  pl.BlockSpec((1, tr, cols), lambda i, p: (p[0], i, 0)), rspec(0), rspec(1), rspec(2)],
        out_specs=pl.BlockSpec((tr, cols), lambda i, p: (p[1] * nr + i, 0)))
    return pl.pallas_call(
        body, grid_spec=grid_spec, out_shape=_sds((r, cols), F32),
        compiler_params=_cp(("parallel",)), name=name)(place, g, sib, recv, recv, recv)


_HBM = pl.BlockSpec(memory_space=pltpu.HBM)
_SEM = pl.BlockSpec(memory_space=pltpu.SEMAPHORE)
_EFFECT = pltpu.SideEffectType.DATAFLOW_SIDE_EFFECTING


def _remote(src, dst, ssem, rsem, k, device):
    return pltpu.make_async_remote_copy(src_ref=src, dst_ref=dst, send_sem=ssem.at[k], recv_sem=rsem.at[k],
                                        device_id=device, device_id_type=MESH)


def _split_start(name, bufs, plan, n):
    nb = len(bufs)

    def body(*refs):
        sends, _ = plan(refs[:nb], refs[nb], refs[nb + 1])
        for cp in sends:
            cp.start()
        refs[-1][...] = jnp.zeros_like(refs[-1])

    res = pl.pallas_call(
        body, name=name,
        out_shape=(pltpu.SemaphoreType.DMA((n,)), pltpu.SemaphoreType.DMA((n,)))
        + tuple(pltpu.HBM(b.shape, b.dtype) for b in bufs) + (_sds((8, 128), F32),),
        in_specs=[_HBM] * nb,
        out_specs=(_SEM, _SEM) + (_HBM,) * nb + (pl.BlockSpec(memory_space=pltpu.VMEM),),
        input_output_aliases={i: i + 2 for i in range(nb)},
        compiler_params=pltpu.CompilerParams(has_side_effects=_EFFECT),
    )(*[pltpu.with_memory_space_constraint(b, pltpu.HBM) for b in bufs])
    return res[0], res[1], list(res[2:2 + nb]), res[2 + nb]


def _split_wait(name, send_sem, recv_sem, bufs, plan, after):
    nb = len(bufs)

    def body(*refs):
        sends, arrivals = plan(refs[:nb], refs[nb], refs[nb + 1])
        for cp in sends:
            cp.wait_send()
        for cp in arrivals:
            cp.wait_recv()

    res = pl.pallas_call(
        body, name=name, out_shape=tuple(pltpu.HBM(b.shape, b.dtype) for b in bufs),
        in_specs=[_HBM] * nb + [_SEM, _SEM, pl.BlockSpec(memory_space=pl.ANY)],
        out_specs=(_HBM,) * nb, input_output_aliases={i: i for i in range(nb)},
        compiler_params=pltpu.CompilerParams(has_side_effects=_EFFECT),
    )(*bufs, send_sem, recv_sem, after)
    return list(res)


def _plan_sibling_halves(shapes):
    na = len(shapes)

    def plan(refs, ssem, rsem):
        x, y, c, _ = _place()
        cps = []
        for a in range(na):
            rh = shapes[a][1] // 2
            src = refs[a].at[:, pl.ds(pl.multiple_of((1 - c) * rh, 8), rh), :]
            cps.append(_remote(src, refs[na + a], ssem, rsem, a, (x, y, 1 - c)))
        return cps, cps

    return plan


def _to_all(ref, ssem, rsem, base):
    x, y, c, _ = _place()
    mine = ref.at[4 * x + 2 * y + c]
    return [_remote(mine, mine, ssem, rsem, base + r - 1, (x ^ (r >> 2), y ^ ((r >> 1) & 1), c ^ (r & 1)))
            for r in range(1, N_DEV)]


def _plan_chip_exchange(na, with_small):
    def plan(refs, ssem, rsem):
        _, _, c, chips = _place()
        cps = []
        for a in range(na):
            for j in range(3):
                kj = 2 * chips[j][0] + chips[j][1]
                cps.append(_remote(refs[a].at[kj], refs[na + a].at[j], ssem, rsem, 3 * a + j, (*chips[j], c)))
        if with_small:
            cps += _to_all(refs[2 * na], ssem, rsem, 3 * na)
        return cps, cps

    return plan


def _plan_sibling_swap(shapes, with_small):
    def plan(refs, ssem, rsem):
        x, y, c, _ = _place()
        sends, arrivals = [], []
        for a, shp in enumerate(shapes):
            rh = shp[0] // 2
            mine, other = _rows(refs[a], None, c, rh), _rows(refs[a], None, 1 - c, rh)
            sends.append(_remote(mine, mine, ssem, rsem, a, (x, y, 1 - c)))
            arrivals.append(_remote(mine, other, ssem, rsem, a, (x, y, 1 - c)))
        if with_small:
            cps = _to_all(refs[len(shapes)], ssem, rsem, len(shapes))
            sends += cps
            arrivals += cps
        return sends, arrivals

    return plan


def _plan_gather_chips(shapes):
    def plan(refs, ssem, rsem):
        x, y, c, chips = _place()
        me = 2 * x + y
        sends, arrivals = [], []
        for a, shp in enumerate(shapes):
            rh = shp[1] // 2
            mine = _rows(refs[a], me, c, rh)
            for j in range(3):
                land = _rows(refs[a], 2 * chips[j][0] + chips[j][1], c, rh)
                sends.append(_remote(mine, mine, ssem, rsem, 3 * a + j, (*chips[j], c)))
                arrivals.append(_remote(land, land, ssem, rsem, 3 * a + j, (*chips[j], c)))
        return sends, arrivals

    return plan


def _plan_gather_sibling(shapes):
    def plan(refs, ssem, rsem):
        x, y, c, chips = _place()
        sends, arrivals = [], []
        for a, shp in enumerate(shapes):
            rh = shp[1] // 2
            for j in range(3):
                kj = 2 * chips[j][0] + chips[j][1]
                got, land = _rows(refs[a], kj, c, rh), _rows(refs[a], kj, 1 - c, rh)
                sends.append(_remote(got, got, ssem, rsem, 3 * a + j, (x, y, 1 - c)))
                arrivals.append(_remote(got, land, ssem, rsem, 3 * a + j, (x, y, 1 - c)))
        return sends, arrivals

    return plan


def _into_slice(w, k, n, tr, dtype, name, deps=()):
    r, cols = w.shape

    def body(k_ref, w_ref, o_ref):
        o_ref[0] = w_ref[...].astype(dtype)

    grid_spec = pltpu.PrefetchScalarGridSpec(
        num_scalar_prefetch=1, grid=(r // tr,),
        in_specs=[pl.BlockSpec((tr, cols), lambda i, k: (i, 0))] + _any_specs(deps),
        out_specs=pl.BlockSpec((1, tr, cols), lambda i, k: (k[0], i, 0)))
    return pl.pallas_call(_after(body, 2, deps), grid_spec=grid_spec, out_shape=_sds((n, r, cols), dtype),
                          compiler_params=_cp(("parallel",)), name=name)(k, w, *deps)


class _LateWeights:
    def __init__(self, bufs):
        self.n = 3 * len(bufs)
        self.chips, self.sibling = _plan_gather_chips([b.shape for b in bufs]), _plan_gather_sibling([b.shape for b in bufs])
        self.ssem, self.rsem, self.bufs, token = _split_start("gather_chips_start", bufs, self.chips, self.n)
        self.first = (token,)

    def middle(self, after):
        bufs = _split_wait("gather_chips_wait", self.ssem, self.rsem, self.bufs, self.chips, after)
        self.ssem, self.rsem, self.bufs, token = _split_start("gather_sibling_start", bufs, self.sibling, self.n)
        return (token,)

    def last(self, after):
        return _split_wait("gather_sibling_wait", self.ssem, self.rsem, self.bufs, self.sibling, after)


class _GradReduce:
    def __init__(self, tag, place, names, tiles):
        self.tag, self.place, self.names, self.tiles = tag, place, names, tiles
        self.small_all = None

    def first(self, grads):
        self.na = len(grads)
        self.p1 = _plan_sibling_halves([g.shape for g in grads])
        lands = [lax.empty((N_CHIPS, g.shape[1] // 2, g.shape[2]), F32) for g in grads]
        self.ssem, self.rsem, self.bufs, token = _split_start(
            self.tag + "_halves_start", list(grads) + lands, self.p1, self.na)
        return (token,)

    def second(self, after, small=None):
        bufs = _split_wait(self.tag + "_halves_wait", self.ssem, self.rsem, self.bufs, self.p1, after)
        self.grads, self.sib = bufs[:self.na], bufs[self.na:]
        halves = [_pair_sum(g, r, self.place[1:2], t, "pair_sum_" + nm)
                  for g, r, t, nm in zip(self.grads, self.sib, self.tiles, self.names)]
        lands = [lax.empty((3,) + h.shape[1:], h.dtype) for h in halves]
        extra = [] if small is None else [small]
        self.p2 = _plan_chip_exchange(self.na, small is not None)
        self.ssem, self.rsem, self.bufs, token = _split_start(
            self.tag + "_chips_start", halves + lands + extra, self.p2, 3 * self.na + (N_DEV - 1) * len(extra))
        return (token,)

    def third(self, after, small=None):
        bufs = _split_wait(self.tag + "_chips_wait", self.ssem, self.rsem, self.bufs, self.p2, after)
        if len(bufs) > 2 * self.na:
            self.small_chips = bufs[2 * self.na]
        mine = [_chip_sum(g, sb, r, self.place, t, "chip_sum_" + nm)
                for g, sb, r, t, nm in zip(self.grads, self.sib, bufs[self.na:2 * self.na], self.tiles, self.names)]
        extra = [] if small is None else [small]
        self.p3 = _plan_sibling_swap([m.shape for m in mine], small is not None)
        self.ssem, self.rsem, self.bufs, token = _split_start(
            self.tag + "_swap_start", mine + extra, self.p3, self.na + (N_DEV - 1) * len(extra))
        return (token,)

    def last(self, after):
        bufs = _split_wait(self.tag + "_swap_wait", self.ssem, self.rsem, self.bufs, self.p3, after)
        if len(bufs) > self.na:
            self.small_swap = bufs[self.na]
        return bufs[:self.na]


class _WeightsAtHand:
    def __init__(self, wup, wout, wdown):
        self.first, self.weights = (), [wup, wout, wdown]

    def middle(self, after):
        return ()

    def last(self, after):
        return self.weights


class _GradsKept:
    def first(self, grads):
        self.grads = list(grads)
        return ()

    def second(self, after, small=None):
        return ()

    def third(self, after, small=None):
        return ()

    def last(self, after):
        return self.grads


def _forward_backward(xs, pos, tgt, win, wpool, cw, attn_norm, b_gate, pool_scale, q_norm, k_norm, sinks,
                      ffn_norm, conv_b, late, early, rest, dev):
    s = xs.shape[0]
    tm = min(512, s)
    tk = min(2048, s)
    inv_freq = ROPE_THETA ** (-jnp.arange(0, ROPE_DIM, 2, dtype=F32) / ROPE_DIM)
    lane = jnp.arange(2 * HEAD) % HEAD
    invf = jnp.where(lane < ROPE_DIM, inv_freq[lane % (ROPE_DIM // 2)], 0.0).reshape(1, 2 * HEAD)
    wq = jnp.tile(q_norm, (1, N_Q))
    wk = jnp.tile(k_norm, (1, N_KV))
    head_of = jnp.arange(Q_W) // HEAD
    bd = (head_of[:, None] == head_of[None, :]).astype(BF)
    sink = sinks[0]

    h1, u, qkv, gates = _attn_in_proj(xs, attn_norm, win, b_gate, tm, deps=late.first)
    qh, kh, vh = _qk_prep(qkv, pos, wq, wk, invf, bd, tm)
    apool, pooled = _pool_fwd(u, wpool, pool_scale, min(2048, s), deps=(qh,))
    battn = _attn_fwd(qh, kh, vh, sink, deps=late.middle(apool))
    wup, wout, wdown = late.last(battn)
    wout = wout.reshape(D, D)
    wdown = wdown.reshape(D_FF, D)
    mix, y1, h2 = _mix_out_proj(apool, battn, gates, xs, wout, ffn_norm, tm)
    pre_g, pre_v, up_g, up_v, act = _ffn_up(h2, wup, cw, conv_b, tm)
    dy2, dy2b, loss_acc = _ffn_down_loss(act, wdown, y1, tgt, tm)

    d_wdown = _grad_matmul(act, dy2b, 512, tk, "grad_w_down")
    dp_g, dp_v, dcw_g, dcw_v, dcb_g, dcb_v = _ffn_act_bwd(dy2b, wdown, up_g, up_v, pre_g, pre_v, cw, tm)
    d_wup = _grad_matmul(h2, dp_g, UP_SHARD, tk, "grad_w_up_gate", lead=N_CHIPS)
    d_wup = _grad_matmul(h2, dp_v, UP_SHARD, tk, "grad_w_up_value", lead=N_CHIPS, prev=d_wup, lead_off=2)
    token = early.first([d_wdown.reshape(N_CHIPS, D_FF // N_CHIPS, D), d_wup])
    dy1, d_ffn_norm = _ffn_up_bwd(dp_g, dp_v, wup, y1, dy2, ffn_norm, tm, deps=token)
    token = early.second(dy1)
    da, db, dzg, d_bgate, d_wout = _out_proj_bwd(dy1, wout, apool, battn, gates, mix, tm, deps=token)
    du, d_wpool, d_pscale, d_win_pool = _pool_bwd(da, pooled, wpool, pool_scale, h1, min(2048, s))
    dqh, dkh, dvh, dsink = _attn_bwd(qh, kh, vh, sink, db)
    token = early.third(dqh)
    dzq, d_qn, d_kn = _qk_prep_bwd(dqh, dkh, dvh, qkv, pos, wq, wk, invf, bd, tm, deps=token)
    d_win_t = jnp.concatenate([
        d_win_pool,
        _grad_matmul(dzq, h1, D, tk, "grad_w_in_qkv"),
        _grad_matmul(dzg, h1, D, tk, "grad_w_in_gates")], axis=0)
    token = rest.first([
        d_win_t.reshape(N_CHIPS, IN_W // N_CHIPS, D),
        d_wout.reshape(N_CHIPS, D // N_CHIPS, D),
        d_wpool.reshape(4, N_CHIPS, 64, POOL_GROUP).transpose(1, 0, 2, 3).reshape(N_CHIPS, 4 * 64, POOL_GROUP)])
    small = _pack_small(loss_acc, d_bgate, d_pscale, d_qn, d_kn, dsink, d_ffn_norm, dcb_g, dcb_v, dcw_g, dcw_v, dev)
    token = rest.second(token[0] if token else None, small=small)
    grad_x, d_attn_norm = _in_proj_bwd(du, dzq, dzg, win, xs, attn_norm, dy1, tm, deps=token)
    return grad_x, d_attn_norm, small


def kernel(x, positions, attn_norm, w_in, b_gate, w_pool, pool_scale, q_norm, k_norm, sinks, w_out, ffn_norm, w_up, conv_w, conv_b, w_down, loss_target, m_attn_norm, m_w_in, m_b_gate, m_w_pool, m_pool_scale, m_q_norm, m_k_norm, m_sinks, m_w_out, m_ffn_norm, m_w_up, m_conv_w, m_conv_b, m_w_down, v_attn_norm, v_w_in, v_b_gate, v_w_pool, v_pool_scale, v_q_norm, v_k_norm, v_sinks, v_w_out, v_ffn_norm, v_w_up, v_conv_w, v_conv_b, v_w_down):
    s = x.shape[1]
    xs = x[0]
    tgt = loss_target[0]
    pos = positions[0].reshape(s, 1)
    cx, cy, cc = lax.axis_index("x"), lax.axis_index("y"), lax.axis_index("c")
    chip = 2 * cx + cy

    dev_arr = (2 * chip + cc).reshape(1).astype(jnp.int32)
    place = jnp.stack([chip, cc]).astype(jnp.int32)

    g_in, g_pool, g_cw, *own_late = _all_gather_weights(
        [jnp.swapaxes(w_in[0], 0, 1), w_pool[0].reshape(4 * 64, POOL_GROUP)], [conv_w[0]],
        [w_up[0], w_out[0], w_down[0]])
    win = g_in.reshape(IN_W, D)
    wpool = g_pool.reshape(N_CHIPS, 4, 64, POOL_GROUP).transpose(1, 0, 2, 3).reshape(4, POOL_GROUP, POOL_GROUP)
    late = _LateWeights(own_late)
    early = _GradReduce("early", place, ["w_down", "w_up"], [176, 256])
    rest = _GradReduce("rest", place, ["w_in", "w_out", "w_pool"], [272, 128, 128])

    grad_x, d_attn_norm, _ = _forward_backward(
        xs, pos, tgt, win, wpool, g_cw, attn_norm, b_gate, pool_scale, q_norm, k_norm, sinks, ffn_norm, conv_b,
        late, early, rest, dev_arr)

    def two_d(a):
        return a.reshape(-1, a.shape[-1])

    def update(nm, w, g, m, v, tr, deps=()):
        res = _adamw(two_d(w), g, two_d(m), two_d(v), tr, "adamw_" + nm, deps=deps)
        return [r.reshape(w.shape) for r in res]

    attn_stack = _into_slice(d_attn_norm, dev_arr, N_DEV, 1, F32, "own_attn_norm")
    g_wdown, g_wup = early.last(grad_x)
    big_out = {"w_up": update("w_up", w_up, g_wup, m_w_up, v_w_up, 256)}
    big_out["w_down"] = update("w_down", w_down, g_wdown, m_w_down, v_w_down, 176, deps=(big_out["w_up"][1],))
    token = rest.third(big_out["w_down"][1], small=attn_stack)
    g_win_t, g_wout, g_wpool = rest.last(token[0])
    small_out, g_convw_all, loss = _small_update(rest.small_chips, rest.small_swap, {
        "attn_norm": (attn_norm, m_attn_norm, v_attn_norm), "b_gate": (b_gate, m_b_gate, v_b_gate),
        "pool_scale": (pool_scale, m_pool_scale, v_pool_scale), "q_norm": (q_norm, m_q_norm, v_q_norm),
        "k_norm": (k_norm, m_k_norm, v_k_norm), "sinks": (sinks, m_sinks, v_sinks),
        "ffn_norm": (ffn_norm, m_ffn_norm, v_ffn_norm), "conv_b": (conv_b, m_conv_b, v_conv_b)})
    g_convw = lax.dynamic_slice_in_dim(g_convw_all, chip * UP_SHARD, UP_SHARD, axis=1)
    small_out["conv_w"] = update("conv_w", conv_w, g_convw, m_conv_w, v_conv_w, 3)
    flip = lambda a: jnp.swapaxes(a[0], 0, 1)
    res = _adamw(flip(w_in), g_win_t, flip(m_w_in), flip(v_w_in), 272, "adamw_w_in")
    big_out["w_in"] = [jnp.swapaxes(r, 0, 1)[None] for r in res]
    big_out["w_out"] = update("w_out", w_out, g_wout, m_w_out, v_w_out, 128)
    big_out["w_pool"] = update("w_pool", w_pool, g_wpool, m_w_pool, v_w_pool, 128)

    order = ["attn_norm", "w_in", "b_gate", "w_pool", "pool_scale", "q_norm", "k_norm", "sinks", "w_out",
             "ffn_norm", "w_up", "conv_w", "conv_b", "w_down"]
    allout = {**big_out, **small_out}
    outs = [loss.reshape(()), grad_x[None]]
    for k in range(4):
        outs += [allout[nm][k] for nm in order]
    return tuple(outs)
```

```python
import functools

import jax
import jax.numpy as jnp
from jax import lax
from jax.experimental import pallas as pl
from jax.experimental.pallas import tpu as pltpu

D = 1024
D_FF = 2816
HEAD = 64
N_Q = 16
N_KV = 2
GQA = 8
BLK = 128
ROPE_DIM = 16
ROPE_THETA = 500000.0
POOL_GROUP = 256
Q_W = 1024
KV_W = 128
QKV_W = Q_W + 2 * KV_W
IN_W = 4352
UP_SHARD = 1408
EPS = 1e-6
N_CHIPS = 4
N_DEV = 8

LR = 0.001
B1 = 0.9
B2 = 0.999
ADAM_EPS = 1e-08
WD = 0.01
STEP = 10

BF = jnp.bfloat16
F32 = jnp.float32
MESH = pl.DeviceIdType.MESH
VMEM_LIMIT_MB = 56


def _cp(sem, vmem_mb=VMEM_LIMIT_MB):
    return pltpu.CompilerParams(dimension_semantics=sem, vmem_limit_bytes=vmem_mb << 20)


def _full(shape):
    nd = len(shape)
    return pl.BlockSpec(shape, lambda *_: (0,) * nd)


def _sds(shape, dtype):
    return jax.ShapeDtypeStruct(shape, dtype)


def _after(body, n_in, deps):
    nd = len(deps)
    if nd == 0:
        return body

    def ordered(*refs):
        return body(*refs[:n_in], *refs[n_in + nd:])

    return ordered


def _any_specs(deps):
    return [pl.BlockSpec(memory_space=pl.ANY)] * len(deps)


def _nt(a, b):
    return lax.dot_general(a, b, (((1,), (1,)), ((), ())), preferred_element_type=F32)


def _tn(a, b):
    return lax.dot_general(a, b, (((0,), (0,)), ((), ())), preferred_element_type=F32)


def _mm(a, b):
    return jnp.dot(a, b, preferred_element_type=F32)


def _head_sum(v, bd):
    return _mm(v.astype(BF), bd)


def _rope_tables(pos_ref, invf_ref):
    ang = pos_ref[...].astype(F32) * invf_ref[...]
    cos = jnp.cos(ang)
    sin = jnp.sin(ang)
    lane = lax.broadcasted_iota(jnp.int32, (1, 2 * HEAD), 1) % HEAD
    sa = jnp.where(lane < ROPE_DIM // 2, -sin, 0.0)
    sb = jnp.where(lane < ROPE_DIM // 2, 0.0, jnp.where(lane < ROPE_DIM, sin, 0.0))
    return cos, sa, sb


def _tile_lanes(t, reps):
    return t if reps == 1 else jnp.tile(t, (1, reps))


def _rope(v, cos, sa, sb):
    w = v.shape[1]
    reps = w // (2 * HEAD)
    half = ROPE_DIM // 2
    return (v * _tile_lanes(cos, reps) + pltpu.roll(v, w - half, 1) * _tile_lanes(sa, reps)
            + pltpu.roll(v, half, 1) * _tile_lanes(sb, reps))


def _rope_t(dy, cos, sa, sb):
    w = dy.shape[1]
    reps = w // (2 * HEAD)
    half = ROPE_DIM // 2
    return (dy * _tile_lanes(cos, reps) + pltpu.roll(dy * _tile_lanes(sa, reps), half, 1)
            + pltpu.roll(dy * _tile_lanes(sb, reps), w - half, 1))


def _attn_in_proj(x, g1, w_in, b_gate, tm, deps=()):
    s = x.shape[0]

    def body(x_ref, g_ref, w_ref, b_ref, h_ref, u_ref, qkv_ref, gate_ref):
        xv = x_ref[...]
        r = lax.rsqrt(jnp.mean(xv * xv, axis=-1, keepdims=True) + EPS)
        h = (xv * r * g_ref[...]).astype(BF)
        h_ref[...] = h
        u_ref[...] = _nt(h, w_ref[0:D, :])
        qkv_ref[...] = _nt(h, w_ref[D:D + QKV_W, :])
        gate_ref[...] = jax.nn.sigmoid(_nt(h, w_ref[D + QKV_W:IN_W, :]) + b_ref[...]).astype(BF)

    row = lambda w: pl.BlockSpec((tm, w), lambda i: (i, 0))
    return pl.pallas_call(
        _after(body, 4, deps), grid=(s // tm,),
        in_specs=[row(D), _full((1, D)), _full((IN_W, D)), _full((1, 2 * D))] + _any_specs(deps),
        out_specs=[row(D), row(D), row(QKV_W), row(2 * D)],
        out_shape=[_sds((s, D), BF), _sds((s, D), F32), _sds((s, QKV_W), F32), _sds((s, 2 * D), BF)],
        compiler_params=_cp(("parallel",)), name="attn_in_proj")(x, g1, w_in, b_gate, *deps)


def _qk_prep(qkv, pos, wq, wk, invf, bd, tm):
    s = qkv.shape[0]

    def body(qkv_ref, pos_ref, wq_ref, wk_ref, invf_ref, bd_ref, qh_ref, kh_ref, vh_ref):
        cos, sa, sb = _rope_tables(pos_ref, invf_ref)
        q = qkv_ref[:, 0:Q_W]
        k = qkv_ref[:, Q_W:Q_W + KV_W]
        v = qkv_ref[:, Q_W + KV_W:QKV_W]
        rq = lax.rsqrt(_head_sum(q * q, bd_ref[...]) * (1.0 / HEAD) + EPS)
        qr = _rope(q * rq * wq_ref[...], cos, sa, sb) * (HEAD ** -0.5)
        rk = lax.rsqrt(_head_sum(k * k, bd_ref[0:KV_W, 0:KV_W]) * (1.0 / HEAD) + EPS)
        kr = _rope(k * rk * wk_ref[...], cos, sa, sb)
        for h in range(N_Q):
            qh_ref[h] = qr[:, HEAD * h:HEAD * (h + 1)].astype(BF)
        for h in range(N_KV):
            kh_ref[h] = kr[:, HEAD * h:HEAD * (h + 1)].astype(BF)
            vh_ref[h] = v[:, HEAD * h:HEAD * (h + 1)].astype(BF)

    heads = lambda n: pl.BlockSpec((n, tm, HEAD), lambda i: (0, i, 0))
    return pl.pallas_call(
        body, grid=(s // tm,),
        in_specs=[pl.BlockSpec((tm, QKV_W), lambda i: (i, 0)), pl.BlockSpec((tm, 1), lambda i: (i, 0)),
                  _full((1, Q_W)), _full((1, KV_W)), _full((1, 2 * HEAD)), _full((Q_W, Q_W))],
        out_specs=[heads(N_Q), heads(N_KV), heads(N_KV)],
        out_shape=[_sds((N_Q, s, HEAD), BF), _sds((N_KV, s, HEAD), BF), _sds((N_KV, s, HEAD), BF)],
        compiler_params=_cp(("parallel",)), name="qk_prep")(qkv, pos, wq, wk, invf, bd)


def _sink_column(sink_ref, kh):
    row_g = lax.broadcasted_iota(jnp.int32, (GQA * BLK, 1), 0) // BLK
    col = jnp.zeros((GQA * BLK, 1), F32)
    for g in range(GQA):
        col = jnp.where(row_g == g, sink_ref[kh * GQA + g], col)
    return col


def _fold_band(band, lower, first=None):
    prev, cur = band[:, 0:BLK], band[:, BLK:2 * BLK]
    if first is not None:
        prev = jnp.where(first, -jnp.inf, prev)
    return jnp.where(lower, cur, prev)


def _unfold_band(x, lower):
    return jnp.concatenate([jnp.where(lower, 0.0, x), jnp.where(lower, x, 0.0)], axis=1)


def _attn_probs(q, k, n, sink_col):
    sc = _nt(q, k)
    qi = lax.broadcasted_iota(jnp.int32, (sc.shape[0], BLK), 0) % BLK
    ki = lax.broadcasted_iota(jnp.int32, (sc.shape[0], BLK), 1)
    lower = ki <= qi
    sc = _fold_band(sc, lower, first=n == 0)
    m = jnp.maximum(jnp.max(sc, axis=-1, keepdims=True), sink_col)
    p = jnp.exp(sc - m)
    es = jnp.exp(sink_col - m)
    inv = 1.0 / (jnp.sum(p, axis=-1, keepdims=True) + es)
    return p * inv, es * inv, lower


def _attn_fwd(qh, kh, vh, sinks, deps=()):
    s = qh.shape[1]
    assert s % (2 * BLK) == 0

    def body(sink_ref, q_ref, kp_ref, kc_ref, vp_ref, vc_ref, o_ref):
        i = pl.program_id(0)
        for sub in range(2):
            for khd in range(N_KV):
                q = q_ref[khd * GQA:(khd + 1) * GQA, sub * BLK:(sub + 1) * BLK].reshape(GQA * BLK, HEAD)
                if sub == 0:
                    k = jnp.concatenate([kp_ref[khd], kc_ref[khd, 0:BLK]], axis=0)
                    v = jnp.concatenate([vp_ref[khd], vc_ref[khd, 0:BLK]], axis=0)
                else:
                    k, v = kc_ref[khd], vc_ref[khd]
                probs, _, lower = _attn_probs(q, k, 2 * i + sub, _sink_column(sink_ref, khd))
                o = _mm(_unfold_band(probs, lower).astype(BF), v)
                for j in range(GQA // 2):
                    c0 = khd * GQA * HEAD + 2 * HEAD * j
                    o_ref[sub * BLK:(sub + 1) * BLK, c0:c0 + 2 * HEAD] = jnp.concatenate(
                        [o[2 * j * BLK:(2 * j + 1) * BLK], o[(2 * j + 1) * BLK:(2 * j + 2) * BLK]],
                        axis=1).astype(BF)

    prev = pl.BlockSpec((N_KV, BLK, HEAD), lambda i: (0, jnp.maximum(2 * i - 1, 0), 0))
    cur = pl.BlockSpec((N_KV, 2 * BLK, HEAD), lambda i: (0, i, 0))
    return pl.pallas_call(
        _after(body, 6, deps), grid=(s // (2 * BLK),),
        in_specs=[pl.BlockSpec(memory_space=pltpu.SMEM),
                  pl.BlockSpec((N_Q, 2 * BLK, HEAD), lambda i: (0, i, 0)), prev, cur, prev, cur] + _any_specs(deps),
        out_specs=pl.BlockSpec((2 * BLK, Q_W), lambda i: (i, 0)),
        out_shape=_sds((s, Q_W), BF),
        compiler_params=_cp(("parallel",)), name="attn_fwd")(sinks, qh, kh, kh, vh, vh, *deps)


def _pool_fwd(u, w_pool, pool_scale, ts, deps=()):
    s = u.shape[0]
    halo = 16

    def body(u_ref, wp_ref, ps_ref, a_ref, pooled_ref, prev):
        g = pl.program_id(0)
        i = pl.program_id(1)

        @pl.when(i == 0)
        def _():
            prev[...] = jnp.zeros_like(prev)

        cur = u_ref[...]
        ext = jnp.concatenate([prev[...], cur], axis=0)
        t = (i * ts + lax.broadcasted_iota(jnp.int32, (ts, 1), 0)).astype(F32)
        for gi in range(4):
            @pl.when(g == gi)
            def _(gi=gi):
                w = 2 << gi
                acc, span = ext, 1
                while span < w:
                    acc = acc + pltpu.roll(acc, span, 0)
                    span *= 2
                inv = 1.0 / jnp.minimum(t + 1.0, float(w))
                pooled = (acc[halo:halo + ts] * inv - cur).astype(BF)
                pooled_ref[...] = pooled
                a_ref[...] = (_mm(pooled, wp_ref[0]) * ps_ref[...]).astype(BF)

        prev[...] = cur[ts - halo:ts]

    col = pl.BlockSpec((ts, POOL_GROUP), lambda g, i: (i, g))
    return pl.pallas_call(
        _after(body, 3, deps), grid=(4, s // ts),
        in_specs=[col, pl.BlockSpec((1, POOL_GROUP, POOL_GROUP), lambda g, i: (g, 0, 0)),
                  pl.BlockSpec((1, POOL_GROUP), lambda g, i: (0, g))] + _any_specs(deps),
        out_specs=[col, col],
        out_shape=[_sds((s, D), BF), _sds((s, D), BF)],
        scratch_shapes=[pltpu.VMEM((halo, POOL_GROUP), F32)],
        compiler_params=_cp(("parallel", "arbitrary")), name="pool_fwd")(u, w_pool, pool_scale, *deps)


def _mix_out_proj(a, b, gates, x, w_out, g2, tm):
    s = x.shape[0]

    def body(a_ref, b_ref, gate_ref, x_ref, w_ref, g_ref, mix_ref, y_ref, h_ref):
        mix = (gate_ref[:, 0:D].astype(F32) * a_ref[...].astype(F32)
               + gate_ref[:, D:2 * D].astype(F32) * b_ref[...].astype(F32)).astype(BF)
        mix_ref[...] = mix
        y = x_ref[...] + _mm(mix, w_ref[...])
        y_ref[...] = y
        r = lax.rsqrt(jnp.mean(y * y, axis=-1, keepdims=True) + EPS)
        h_ref[...] = (y * r * g_ref[...]).astype(BF)

    row = lambda w: pl.BlockSpec((tm, w), lambda i: (i, 0))
    return pl.pallas_call(
        body, grid=(s // tm,),
        in_specs=[row(D), row(D), row(2 * D), row(D), _full((D, D)), _full((1, D))],
        out_specs=[row(D), row(D), row(D)],
        out_shape=[_sds((s, D), BF), _sds((s, D), F32), _sds((s, D), BF)],
        compiler_params=_cp(("parallel",)), name="mix_out_proj")(a, b, gates, x, w_out, g2)


def _ffn_up(h2, w_up, conv_w, conv_b, tm):
    s = h2.shape[0]

    def body(h_ref, wg_ref, wv_ref, cwg_ref, cwv_ref, cbg_ref, cbv_ref,
             preg_ref, prev_ref, upg_ref, upv_ref, act_ref, halog, halov):
        i = pl.program_id(1)

        @pl.when(i == 0)
        def _():
            halog[...] = jnp.zeros_like(halog)
            halov[...] = jnp.zeros_like(halov)

        h = h_ref[...]

        def conv_half(w_ref, cw_ref, cb_ref, halo, pre_ref, up_ref):
            pre = _mm(h, w_ref[0])
            pre_ref[...] = pre.astype(BF)
            ext = jnp.concatenate([halo[...], pre], axis=0)
            cw = cw_ref[0]
            up = cb_ref[...] + cw[0:1] * pltpu.roll(ext, 2, 0)[8:8 + tm]
            up = up + cw[1:2] * pltpu.roll(ext, 1, 0)[8:8 + tm]
            up = up + cw[2:3] * pre
            halo[...] = pre[tm - 8:tm]
            up_ref[...] = up.astype(BF)
            return up

        gate = conv_half(wg_ref, cwg_ref, cbg_ref, halog, preg_ref, upg_ref)
        val = conv_half(wv_ref, cwv_ref, cbv_ref, halov, prev_ref, upv_ref)
        act_ref[...] = (gate * jax.nn.sigmoid(gate) * val).astype(BF)

    tile = pl.BlockSpec((tm, UP_SHARD), lambda j, i: (i, j))
    wspec = lambda off: pl.BlockSpec((1, D, UP_SHARD), lambda j, i: (j + off, 0, 0))
    cwspec = lambda off: pl.BlockSpec((1, 3, UP_SHARD), lambda j, i: (j + off, 0, 0))
    cbspec = lambda off: pl.BlockSpec((1, UP_SHARD), lambda j, i: (0, j + off))
    half = _sds((s, D_FF), BF)
    return pl.pallas_call(
        body, grid=(2, s // tm),
        in_specs=[pl.BlockSpec((tm, D), lambda j, i: (i, 0)), wspec(0), wspec(2), cwspec(0), cwspec(2),
                  cbspec(0), cbspec(2)],
        out_specs=[tile] * 5, out_shape=[half] * 5,
        scratch_shapes=[pltpu.VMEM((8, UP_SHARD), F32), pltpu.VMEM((8, UP_SHARD), F32)],
        compiler_params=_cp(("parallel", "arbitrary")), name="ffn_up")(
            h2, w_up, w_up, conv_w, conv_w, conv_b, conv_b)


def _ffn_down_loss(act, w_down, y1, tgt, tm):
    s = y1.shape[0]

    def body(act_ref, w_ref, y_ref, t_ref, dy_ref, dyb_ref, loss_ref):
        @pl.when(pl.program_id(0) == 0)
        def _():
            loss_ref[...] = jnp.zeros_like(loss_ref)

        e = y_ref[...] + _mm(act_ref[...], w_ref[...]) - t_ref[...]
        dy = e * (1.0 / D)
        dy_ref[...] = dy
        dyb_ref[...] = dy.astype(BF)
        e2 = (e * e).reshape(tm // 8, 8, D).sum(axis=0)
        part = e2[:, 0:128]
        for j in range(1, D // 128):
            part = part + e2[:, 128 * j:128 * (j + 1)]
        loss_ref[...] += part

    row = lambda w: pl.BlockSpec((tm, w), lambda i: (i, 0))
    return pl.pallas_call(
        body, grid=(s // tm,),
        in_specs=[row(D_FF), _full((D_FF, D)), row(D), row(D)],
        out_specs=[row(D), row(D), _full((8, 128))],
        out_shape=[_sds((s, D), F32), _sds((s, D), BF), _sds((8, 128), F32)],
        compiler_params=_cp(("arbitrary",)), name="ffn_down_loss")(act, w_down, y1, tgt)


def _grad_matmul(a, b, tn, tk, name, lead=None, prev=None, lead_off=0):
    s, m = a.shape
    n = b.shape[1]
    nj = n // tn

    def body(*refs):
        a_ref, b_ref = refs[0], refs[1]
        o_ref = refs[-1]
        acc = _tn(a_ref[...], b_ref[...])
        acc = acc if lead is None else acc[None]

        @pl.when(pl.program_id(1) == 0)
        def _():
            o_ref[...] = acc

        @pl.when(pl.program_id(1) > 0)
        def _():
            o_ref[...] += acc

    in_specs = [pl.BlockSpec((tk, m), lambda j, k: (k, 0)), pl.BlockSpec((tk, tn), lambda j, k: (k, j))]
    args = [a, b]
    aliases = {}
    if lead is None:
        out_spec = pl.BlockSpec((m, tn), lambda j, k: (0, j))
        out_shape = _sds((m, n), F32)
    else:
        out_spec = pl.BlockSpec((1, m, tn), lambda j, k: (j + lead_off, 0, 0))
        out_shape = _sds((lead, m, tn), F32)
        if prev is not None:
            in_specs.append(pl.BlockSpec(memory_space=pl.ANY))
            args.append(prev)
            aliases = {2: 0}
    return pl.pallas_call(
        body, grid=(nj, s // tk), in_specs=in_specs, out_specs=out_spec, out_shape=out_shape,
        input_output_aliases=aliases,
        compiler_params=_cp(("parallel", "arbitrary")), name=name)(*args)


def _ffn_act_bwd(dyb, w_down, up_g, up_v, pre_g, pre_v, conv_w, tm):
    s = dyb.shape[0]
    nt = s // tm

    def body(dy_ref, wd_ref, upg_ref, upv_ref, preg_ref, prev_ref, cwg_ref, cwv_ref,
             dpg_ref, dpv_ref, dcwg_ref, dcwv_ref, dcbg_ref, dcbv_ref, nxg, nxv):
        i = pl.program_id(1)

        @pl.when(i == 0)
        def _():
            nxg[...] = jnp.zeros_like(nxg)
            nxv[...] = jnp.zeros_like(nxv)
            dcwg_ref[...] = jnp.zeros_like(dcwg_ref)
            dcwv_ref[...] = jnp.zeros_like(dcwv_ref)
            dcbg_ref[...] = jnp.zeros_like(dcbg_ref)
            dcbv_ref[...] = jnp.zeros_like(dcbv_ref)

        dact = _nt(dy_ref[...], wd_ref[...])
        g = upg_ref[...].astype(F32)
        v = upv_ref[...].astype(F32)
        sg = jax.nn.sigmoid(g)
        d_v = dact * (g * sg)
        d_g = dact * v * (sg * (1.0 + g * (1.0 - sg)))

        def conv_bwd(d_up, nx, pre_ref, cw_ref, dp_ref, dcw_ref, dcb_ref):
            ext = jnp.concatenate([d_up, nx[...]], axis=0)
            s1 = pltpu.roll(ext, tm + 8 - 1, 0)[0:tm]
            s2 = pltpu.roll(ext, tm + 8 - 2, 0)[0:tm]
            cw = cw_ref[0]
            dp_ref[...] = (cw[2:3] * d_up + cw[1:2] * s1 + cw[0:1] * s2).astype(BF)
            nx[...] = d_up[0:8]
            pre = pre_ref[...].astype(F32)
            dcw_ref[0, 0:1, :] += jnp.sum(s2 * pre, axis=0, keepdims=True)
            dcw_ref[0, 1:2, :] += jnp.sum(s1 * pre, axis=0, keepdims=True)
            dcw_ref[0, 2:3, :] += jnp.sum(d_up * pre, axis=0, keepdims=True)
            dcb_ref[...] += jnp.sum(d_up, axis=0, keepdims=True)

        conv_bwd(d_g, nxg, preg_ref, cwg_ref, dpg_ref, dcwg_ref, dcbg_ref)
        conv_bwd(d_v, nxv, prev_ref, cwv_ref, dpv_ref, dcwv_ref, dcbv_ref)

    tile = pl.BlockSpec((tm, UP_SHARD), lambda j, i: (nt - 1 - i, j))
    cwspec = lambda off: pl.BlockSpec((1, 3, UP_SHARD), lambda j, i: (j + off, 0, 0))
    acc_cw = pl.BlockSpec((1, 3, UP_SHARD), lambda j, i: (j, 0, 0))
    acc_cb = pl.BlockSpec((1, UP_SHARD), lambda j, i: (0, j))
    buf = pltpu.VMEM((8, UP_SHARD), F32)
    return pl.pallas_call(
        body, grid=(2, nt),
        in_specs=[pl.BlockSpec((tm, D), lambda j, i: (nt - 1 - i, 0)),
                  pl.BlockSpec((UP_SHARD, D), lambda j, i: (j, 0)),
                  tile, tile, tile, tile, cwspec(0), cwspec(2)],
        out_specs=[tile, tile, acc_cw, acc_cw, acc_cb, acc_cb],
        out_shape=[_sds((s, D_FF), BF), _sds((s, D_FF), BF), _sds((2, 3, UP_SHARD), F32),
                   _sds((2, 3, UP_SHARD), F32), _sds((1, D_FF), F32), _sds((1, D_FF), F32)],
        scratch_shapes=[buf, buf],
        compiler_params=_cp(("parallel", "arbitrary")), name="ffn_act_bwd")(
            dyb, w_down, up_g, up_v, pre_g, pre_v, conv_w, conv_w)


def _rms_bwd(dh, y, g):
    r = lax.rsqrt(jnp.mean(y * y, axis=-1, keepdims=True) + EPS)
    n = y * r
    dn = dh * g
    return r * (dn - n * jnp.mean(dn * n, axis=-1, keepdims=True)), dh * n


def _ffn_up_bwd(dp_g, dp_v, w_up, y1, dy2, g2, tm, deps=()):
    s = y1.shape[0]

    def body(dg_ref, dv_ref, w_ref, y_ref, dy2_ref, g_ref, dy1_ref, dgn_ref):
        @pl.when(pl.program_id(0) == 0)
        def _():
            dgn_ref[...] = jnp.zeros_like(dgn_ref)

        dh = _nt(dg_ref[:, 0:UP_SHARD], w_ref[0])
        dh = dh + _nt(dg_ref[:, UP_SHARD:D_FF], w_ref[1])
        dh = dh + _nt(dv_ref[:, 0:UP_SHARD], w_ref[2])
        dh = dh + _nt(dv_ref[:, UP_SHARD:D_FF], w_ref[3])
        dy, dgn = _rms_bwd(dh, y_ref[...], g_ref[...])
        dy1_ref[...] = dy2_ref[...] + dy
        dgn_ref[...] += jnp.sum(dgn, axis=0, keepdims=True)

    row = lambda w: pl.BlockSpec((tm, w), lambda i: (i, 0))
    return pl.pallas_call(
        _after(body, 6, deps), grid=(s // tm,),
        in_specs=[row(D_FF), row(D_FF), _full((4, D, UP_SHARD)), row(D), row(D), _full((1, D))] + _any_specs(deps),
        out_specs=[row(D), _full((1, D))],
        out_shape=[_sds((s, D), F32), _sds((1, D), F32)],
        compiler_params=_cp(("arbitrary",)), name="ffn_up_bwd")(dp_g, dp_v, w_up, y1, dy2, g2, *deps)


def _out_proj_bwd(dy1, w_out, a, b, gates, mix, tm, deps=()):
    s = dy1.shape[0]

    def body(dy_ref, w_ref, a_ref, b_ref, gate_ref, mix_ref, da_ref, db_ref, dzg_ref, dbg_ref, dw_ref):
        @pl.when(pl.program_id(0) == 0)
        def _():
            dbg_ref[...] = jnp.zeros_like(dbg_ref)
            dw_ref[...] = jnp.zeros_like(dw_ref)

        dyb = dy_ref[...].astype(BF)
        dmix = _nt(dyb, w_ref[...])
        gp = gate_ref[:, 0:D].astype(F32)
        ga = gate_ref[:, D:2 * D].astype(F32)
        da_ref[...] = (dmix * gp).astype(BF)
        db_ref[...] = (dmix * ga).astype(BF)
        dzp = dmix * a_ref[...].astype(F32) * (gp * (1.0 - gp))
        dza = dmix * b_ref[...].astype(F32) * (ga * (1.0 - ga))
        dzg_ref[:, 0:D] = dzp.astype(BF)
        dzg_ref[:, D:2 * D] = dza.astype(BF)
        dbg_ref[:, 0:D] += jnp.sum(dzp, axis=0, keepdims=True)
        dbg_ref[:, D:2 * D] += jnp.sum(dza, axis=0, keepdims=True)
        dw_ref[...] += _tn(mix_ref[...], dyb)

    row = lambda w: pl.BlockSpec((tm, w), lambda i: (i, 0))
    return pl.pallas_call(
        _after(body, 6, deps), grid=(s // tm,),
        in_specs=[row(D), _full((D, D)), row(D), row(D), row(2 * D), row(D)] + _any_specs(deps),
        out_specs=[row(D), row(D), row(2 * D), _full((1, 2 * D)), _full((D, D))],
        out_shape=[_sds((s, D), BF), _sds((s, D), BF), _sds((s, 2 * D), BF), _sds((1, 2 * D), F32),
                   _sds((D, D), F32)],
        compiler_params=_cp(("arbitrary",)), name="out_proj_bwd")(dy1, w_out, a, b, gates, mix, *deps)


def _pool_bwd(da, pooled, w_pool, pool_scale, h1, ts):
    s = da.shape[0]
    nt = s // ts
    halo = 16

    def body(da_ref, pooled_ref, wp_ref, ps_ref, h_ref, du_ref, dwp_ref, dps_ref, dwi_ref, nxt):
        g = pl.program_id(0)
        i = pl.program_id(1)
        ti = nt - 1 - i

        @pl.when(i == 0)
        def _():
            nxt[...] = jnp.zeros_like(nxt)
            dwp_ref[...] = jnp.zeros_like(dwp_ref)
            dps_ref[...] = jnp.zeros_like(dps_ref)
            dwi_ref[...] = jnp.zeros_like(dwi_ref)

        pooled = pooled_ref[...]
        dav = da_ref[...].astype(F32)
        dps_ref[...] += jnp.sum(dav * _mm(pooled, wp_ref[0]), axis=0, keepdims=True)
        dm = (dav * ps_ref[...]).astype(BF)
        dwp_ref[0] += _tn(pooled, dm)
        dpool = _nt(dm, wp_ref[0])
        t = (ti * ts + lax.broadcasted_iota(jnp.int32, (ts, 1), 0)).astype(F32)
        for gi in range(4):
            @pl.when(g == gi)
            def _(gi=gi):
                w = 2 << gi
                e = dpool * (1.0 / jnp.minimum(t + 1.0, float(w)))
                acc, span = jnp.concatenate([e, nxt[...]], axis=0), 1
                while span < w:
                    acc = acc + pltpu.roll(acc, ts + halo - span, 0)
                    span *= 2
                du = (acc[0:ts] - dpool).astype(BF)
                du_ref[...] = du
                dwi_ref[...] += _tn(du, h_ref[...])
                nxt[...] = e[0:halo]

    col = pl.BlockSpec((ts, POOL_GROUP), lambda g, i: (nt - 1 - i, g))
    return pl.pallas_call(
        body, grid=(4, nt),
        in_specs=[col, col, pl.BlockSpec((1, POOL_GROUP, POOL_GROUP), lambda g, i: (g, 0, 0)),
                  pl.BlockSpec((1, POOL_GROUP), lambda g, i: (0, g)),
                  pl.BlockSpec((ts, D), lambda g, i: (nt - 1 - i, 0))],
        out_specs=[col, pl.BlockSpec((1, POOL_GROUP, POOL_GROUP), lambda g, i: (g, 0, 0)),
                   pl.BlockSpec((1, POOL_GROUP), lambda g, i: (0, g)),
                   pl.BlockSpec((POOL_GROUP, D), lambda g, i: (g, 0))],
        out_shape=[_sds((s, D), BF), _sds((4, POOL_GROUP, POOL_GROUP), F32), _sds((1, D), F32), _sds((D, D), F32)],
        scratch_shapes=[pltpu.VMEM((halo, POOL_GROUP), F32)],
        compiler_params=_cp(("parallel", "arbitrary")), name="pool_bwd")(da, pooled, w_pool, pool_scale, h1)


def _attn_bwd(qh, kh, vh, sinks, db, deps=()):
    s = qh.shape[1]
    assert s % (2 * BLK) == 0
    pairs = s // (2 * BLK)

    def body(sink_ref, q_ref, kp_ref, kc_ref, vp_ref, vc_ref, do_ref,
             dq_ref, dke_ref, dko_ref, dve_ref, dvo_ref, dsink_ref, ck, cv):
        i = pl.program_id(0)

        @pl.when(i == 0)
        def _():
            ck[...] = jnp.zeros_like(ck)
            cv[...] = jnp.zeros_like(cv)
            dsink_ref[...] = jnp.zeros_like(dsink_ref)

        @pl.when(i < pairs)
        def _():
            for khd in range(N_KV):
                c0 = khd * GQA * HEAD
                band = []
                for sub in range(2):
                    q = q_ref[khd * GQA:(khd + 1) * GQA, sub * BLK:(sub + 1) * BLK].reshape(GQA * BLK, HEAD)
                    if sub == 0:
                        k = jnp.concatenate([kp_ref[khd], kc_ref[khd, 0:BLK]], axis=0)
                        v = jnp.concatenate([vp_ref[khd], vc_ref[khd, 0:BLK]], axis=0)
                    else:
                        k, v = kc_ref[khd], vc_ref[khd]
                    dov = do_ref[sub * BLK:(sub + 1) * BLK, :]
                    do = jnp.concatenate([dov[:, c0 + HEAD * g:c0 + HEAD * (g + 1)] for g in range(GQA)],
                                         axis=0).astype(BF)
                    probs, psink, lower = _attn_probs(q, k, 2 * i + sub, _sink_column(sink_ref, khd))
                    dp = _fold_band(_nt(do, v), lower)
                    delta = jnp.sum(probs * dp, axis=-1, keepdims=True)
                    ds = _unfold_band(probs * (dp - delta), lower).astype(BF)
                    dq_ref[khd * GQA:(khd + 1) * GQA, sub * BLK:(sub + 1) * BLK] = _mm(ds, k).reshape(
                        GQA, BLK, HEAD)
                    band.append((_tn(ds, q), _tn(_unfold_band(probs, lower).astype(BF), do)))
                    dsk = psink * delta
                    lane = lax.broadcasted_iota(jnp.int32, (1, 128), 1)
                    acc = jnp.zeros((1, 128), F32)
                    for g in range(GQA):
                        acc = acc - jnp.where(lane == khd * GQA + g,
                                              jnp.sum(dsk[g * BLK:(g + 1) * BLK], axis=0, keepdims=True), 0.0)
                    dsink_ref[...] += acc
                (dk0, dv0), (dk1, dv1) = band
                dko_ref[khd] = ck[khd] + dk0[0:BLK]
                dvo_ref[khd] = cv[khd] + dv0[0:BLK]
                dke_ref[khd] = dk0[BLK:2 * BLK] + dk1[0:BLK]
                dve_ref[khd] = dv0[BLK:2 * BLK] + dv1[0:BLK]
                ck[khd] = dk1[BLK:2 * BLK]
                cv[khd] = dv1[BLK:2 * BLK]

        @pl.when(i == pairs)
        def _():
            dko_ref[...] = ck[...]
            dvo_ref[...] = cv[...]

    last = pairs - 1
    at = lambda i: jnp.minimum(i, last)
    prev = pl.BlockSpec((N_KV, BLK, HEAD), lambda i: (0, jnp.maximum(2 * at(i) - 1, 0), 0))
    cur = pl.BlockSpec((N_KV, 2 * BLK, HEAD), lambda i: (0, at(i), 0))
    even = pl.BlockSpec((N_KV, BLK, HEAD), lambda i: (0, at(i), 0))
    odd = pl.BlockSpec((N_KV, BLK, HEAD), lambda i: (0, jnp.maximum(i - 1, 0), 0))
    halfkv = _sds((N_KV, s // 2, HEAD), F32)
    dq, dke, dko, dve, dvo, dsink = pl.pallas_call(
        _after(body, 7, deps), grid=(pairs + 1,),
        in_specs=[pl.BlockSpec(memory_space=pltpu.SMEM),
                  pl.BlockSpec((N_Q, 2 * BLK, HEAD), lambda i: (0, at(i), 0)),
                  prev, cur, prev, cur,
                  pl.BlockSpec((2 * BLK, Q_W), lambda i: (at(i), 0))] + _any_specs(deps),
        out_specs=[pl.BlockSpec((N_Q, 2 * BLK, HEAD), lambda i: (0, at(i), 0)), even, odd, even, odd,
                   _full((1, 128))],
        out_shape=[_sds((N_Q, s, HEAD), F32), halfkv, halfkv, halfkv, halfkv, _sds((1, 128), F32)],
        scratch_shapes=[pltpu.VMEM((N_KV, BLK, HEAD), F32), pltpu.VMEM((N_KV, BLK, HEAD), F32)],
        compiler_params=_cp(("arbitrary",)), name="attn_bwd")(sinks, qh, kh, kh, vh, vh, db, *deps)

    def interleave(ev, od):
        both = jnp.stack([ev.reshape(N_KV, pairs, BLK, HEAD), od.reshape(N_KV, pairs, BLK, HEAD)], axis=2)
        return both.reshape(N_KV, s, HEAD)

    return dq, interleave(dke, dko), interleave(dve, dvo), dsink


def _qk_prep_bwd(dqh, dkh, dvh, qkv, pos, wq, wk, invf, bd, tm, deps=()):
    s = qkv.shape[0]

    def fold_heads(row):
        out = row[:, 0:HEAD]
        for h in range(1, row.shape[1] // HEAD):
            out = out + row[:, HEAD * h:HEAD * (h + 1)]
        return out

    def body(dq_ref, dk_ref, dv_ref, qkv_ref, pos_ref, wq_ref, wk_ref, invf_ref, bd_ref,
             dz_ref, dwq_ref, dwk_ref):
        @pl.when(pl.program_id(0) == 0)
        def _():
            dwq_ref[...] = jnp.zeros_like(dwq_ref)
            dwk_ref[...] = jnp.zeros_like(dwk_ref)

        cos, sa, sb = _rope_tables(pos_ref, invf_ref)

        def norm_rope_bwd(dy, xin, w, bdm):
            dn = _rope_t(dy, cos, sa, sb)
            r = lax.rsqrt(_head_sum(xin * xin, bdm) * (1.0 / HEAD) + EPS)
            nh = xin * r
            gw = dn * w
            dx = r * (gw - nh * (_head_sum(gw * nh, bdm) * (1.0 / HEAD)))
            return dx, fold_heads(jnp.sum(dn * nh, axis=0, keepdims=True))

        dq = jnp.concatenate([dq_ref[h] for h in range(N_Q)], axis=1) * (HEAD ** -0.5)
        dk = jnp.concatenate([dk_ref[h] for h in range(N_KV)], axis=1)
        dxq, dwq = norm_rope_bwd(dq, qkv_ref[:, 0:Q_W], wq_ref[...], bd_ref[...])
        dxk, dwk = norm_rope_bwd(dk, qkv_ref[:, Q_W:Q_W + KV_W], wk_ref[...], bd_ref[0:KV_W, 0:KV_W])
        dz_ref[:, 0:Q_W] = dxq.astype(BF)
        dz_ref[:, Q_W:Q_W + KV_W] = dxk.astype(BF)
        dz_ref[:, Q_W + KV_W:QKV_W] = jnp.concatenate([dv_ref[h] for h in range(N_KV)], axis=1).astype(BF)
        dwq_ref[...] += dwq
        dwk_ref[...] += dwk

    heads = lambda n: pl.BlockSpec((n, tm, HEAD), lambda i: (0, i, 0))
    return pl.pallas_call(
        _after(body, 9, deps), grid=(s // tm,),
        in_specs=[heads(N_Q), heads(N_KV), heads(N_KV), pl.BlockSpec((tm, QKV_W), lambda i: (i, 0)),
                  pl.BlockSpec((tm, 1), lambda i: (i, 0)), _full((1, Q_W)), _full((1, KV_W)),
                  _full((1, 2 * HEAD)), _full((Q_W, Q_W))] + _any_specs(deps),
        out_specs=[pl.BlockSpec((tm, QKV_W), lambda i: (i, 0)), _full((1, HEAD)), _full((1, HEAD))],
        out_shape=[_sds((s, QKV_W), BF), _sds((1, HEAD), F32), _sds((1, HEAD), F32)],
        compiler_params=_cp(("arbitrary",)), name="qk_prep_bwd")(
            dqh, dkh, dvh, qkv, pos, wq, wk, invf, bd, *deps)


def _in_proj_bwd(du, dzq, dzg, w_in, x, g1, dy1, tm, deps=()):
    s = x.shape[0]

    def body(du_ref, dzq_ref, dzg_ref, w_ref, x_ref, g_ref, dy_ref, gx_ref, dgn_ref):
        @pl.when(pl.program_id(0) == 0)
        def _():
            dgn_ref[...] = jnp.zeros_like(dgn_ref)

        dh = _mm(du_ref[...], w_ref[0:D, :])
        dh = dh + _mm(dzq_ref[...], w_ref[D:D + QKV_W, :])
        dh = dh + _mm(dzg_ref[...], w_ref[D + QKV_W:IN_W, :])
        dx, dgn = _rms_bwd(dh, x_ref[...], g_ref[...])
        gx_ref[...] = dy_ref[...] + dx
        dgn_ref[...] += jnp.sum(dgn, axis=0, keepdims=True)

    row = lambda w: pl.BlockSpec((tm, w), lambda i: (i, 0))
    return pl.pallas_call(
        _after(body, 7, deps), grid=(s // tm,),
        in_specs=[row(D), row(QKV_W), row(2 * D), _full((IN_W, D)), row(D), _full((1, D)), row(D)] + _any_specs(deps),
        out_specs=[row(D), _full((1, D))],
        out_shape=[_sds((s, D), F32), _sds((1, D), F32)],
        compiler_params=_cp(("arbitrary",)), name="in_proj_bwd")(du, dzq, dzg, w_in, x, g1, dy1, *deps)


def _adamw_step(w, g, m, v):
    mn = B1 * m + (1.0 - B1) * g
    vn = B2 * v + (1.0 - B2) * (g * g)
    m_hat = mn / (1.0 - B1 ** STEP)
    v_hat = vn / (1.0 - B2 ** STEP)
    return -LR * (m_hat / (jnp.sqrt(v_hat) + ADAM_EPS) + WD * w), mn, vn


SMALL_ROWS = 16
SMALL_COLS = D_FF
SMALL_AT = {"b_gate": (1, 2 * D), "pool_scale": (2, D), "q_norm": (3, HEAD),
            "k_norm": (4, HEAD), "sinks": (5, N_Q), "ffn_norm": (6, D)}
SMALL_LOSS_ROW = 0
SMALL_CONV_B_ROW = 7
SMALL_CONV_W_ROW = 9


def _pack_small(loss_acc, d_bgate, d_pscale, d_qn, d_kn, dsink, d_ffn_norm, dcb_g, dcb_v, dcw_g, dcw_v, dev):
    def body(k_ref, ls_ref, bg_ref, ps_ref, qn_ref, kn_ref, sk_ref, fn_ref, cbg_ref, cbv_ref, cwg_ref, cwv_ref,
             o_ref):
        o_ref[...] = jnp.zeros_like(o_ref)
        o_ref[0, SMALL_LOSS_ROW:SMALL_LOSS_ROW + 1, 0:128] = jnp.sum(ls_ref[...], axis=0, keepdims=True)
        for nm, ref in (("b_gate", bg_ref), ("pool_scale", ps_ref), ("q_norm", qn_ref),
                        ("k_norm", kn_ref), ("ffn_norm", fn_ref)):
            row, n = SMALL_AT[nm]
            o_ref[0, row:row + 1, 0:n] = ref[...]
        row, _ = SMALL_AT["sinks"]
        o_ref[0, row:row + 1, 0:128] = sk_ref[...]
        o_ref[0, SMALL_CONV_B_ROW:SMALL_CONV_B_ROW + 1, :] = cbg_ref[...]
        o_ref[0, SMALL_CONV_B_ROW + 1:SMALL_CONV_B_ROW + 2, :] = cbv_ref[...]
        for k in range(3):
            row = SMALL_CONV_W_ROW + 2 * k
            for half in range(2):
                o_ref[0, row:row + 1, half * UP_SHARD:(half + 1) * UP_SHARD] = cwg_ref[half, k:k + 1, :]
                o_ref[0, row + 1:row + 2, half * UP_SHARD:(half + 1) * UP_SHARD] = cwv_ref[half, k:k + 1, :]

    args = [loss_acc, d_bgate, d_pscale, d_qn, d_kn, dsink, d_ffn_norm, dcb_g, dcb_v, dcw_g, dcw_v]
    grid_spec = pltpu.PrefetchScalarGridSpec(
        num_scalar_prefetch=1, grid=(1,),
        in_specs=[pl.BlockSpec(a.shape, functools.partial(lambda nd, i, k: (0,) * nd, a.ndim)) for a in args],
        out_specs=pl.BlockSpec((1, SMALL_ROWS, SMALL_COLS), lambda i, k: (k[0], 0, 0)))
    return pl.pallas_call(body, grid_spec=grid_spec, out_shape=_sds((N_DEV, SMALL_ROWS, SMALL_COLS), F32),
                          name="pack_small")(dev, *args)


def _small_update(stack, attn_stack, params):
    names = list(params)

    def body(*refs):
        s_ref, a_ref = refs[0], refs[1]
        ins = refs[2:2 + 3 * len(names)]
        outs = refs[2 + 3 * len(names):]
        tot, tot_a = s_ref[0], a_ref[0]
        for d in range(1, N_DEV):
            tot = tot + s_ref[d]
            tot_a = tot_a + a_ref[d]
        for i, nm in enumerate(names):
            if nm == "attn_norm":
                g = tot_a
            elif nm == "conv_b":
                g = jnp.concatenate([tot[SMALL_CONV_B_ROW:SMALL_CONV_B_ROW + 1, :],
                                     tot[SMALL_CONV_B_ROW + 1:SMALL_CONV_B_ROW + 2, :]], axis=1)
            else:
                row, n = SMALL_AT[nm]
                g = tot[row:row + 1, 0:n]
            delta, mn, vn = _adamw_step(ins[3 * i][...], g, ins[3 * i + 1][...], ins[3 * i + 2][...])
            outs[4 * i][...] = g
            outs[4 * i + 1][...] = delta
            outs[4 * i + 2][...] = mn
            outs[4 * i + 3][...] = vn
        for k in range(3):
            row = SMALL_CONV_W_ROW + 2 * k
            outs[-2][k:k + 1, 0:D_FF] = tot[row:row + 1, :]
            outs[-2][k:k + 1, D_FF:2 * D_FF] = tot[row + 1:row + 2, :]
        outs[-1][...] = jnp.sum(tot[SMALL_LOSS_ROW:SMALL_LOSS_ROW + 1, 0:128], axis=1, keepdims=True) * (0.5 / D)

    flat = [a for nm in names for a in params[nm]]
    out_shape = ([_sds(params[nm][0].shape, F32) for nm in names for _ in range(4)]
                 + [_sds((3, 2 * D_FF), F32), _sds((1, 1), F32)])
    res = pl.pallas_call(body, out_shape=out_shape, name="small_update")(stack, attn_stack, *flat)
    return {nm: list(res[4 * i:4 * i + 4]) for i, nm in enumerate(names)}, res[-2], res[-1]


def _adamw(w, g, m, v, tr, name, deps=()):
    r, c = w.shape

    def body(w_ref, g_ref, m_ref, v_ref, go_ref, d_ref, mo_ref, vo_ref):
        gv = g_ref[...]
        go_ref[...] = gv
        d_ref[...], mo_ref[...], vo_ref[...] = _adamw_step(w_ref[...], gv, m_ref[...], v_ref[...])

    blk = pl.BlockSpec((tr, c), lambda i: (i, 0))
    return pl.pallas_call(
        _after(body, 4, deps), grid=(r // tr,), in_specs=[blk] * 4 + _any_specs(deps), out_specs=[blk] * 4,
        out_shape=[_sds((r, c), F32)] * 4, compiler_params=_cp(("parallel",)), name=name)(w, g, m, v, *deps)


def _place():
    x, y, c = lax.axis_index("x"), lax.axis_index("y"), lax.axis_index("c")
    chips = [(1 - x, y), (x, 1 - y), (1 - x, 1 - y)]
    return x, y, c, chips


def _rows(ref, lead, h, rh):
    sl = pl.ds(pl.multiple_of(h * rh, 16), rh)
    return ref.at[sl, :] if lead is None else ref.at[lead, sl, :]


def _all_gather_weights(halved, whole, placed):
    nh, nw, npl = len(halved), len(whole), len(placed)
    na = nh + nw
    nall = na + npl
    arrays = list(halved) + list(whole) + list(placed)
    out_dtypes = [BF] * nh + [a.dtype for a in whole] + [BF] * npl
    cast_rows = 128

    def body(*refs):
        ins, outs = refs[:nall], refs[nall:2 * nall]
        raw, stage = refs[2 * nall:3 * nall], refs[3 * nall:3 * nall + nh + npl]
        ici_send, ici_recv, fwd_send, fwd_recv, in_sem, loc_sem = refs[3 * nall + nh + npl:]
        x, y, c, chips = _place()
        me = 2 * x + y
        sibling = (x, y, 1 - c)
        loads = [pltpu.make_async_copy(ins[a], raw[a], in_sem.at[a]) for a in range(nall)]
        for cp in loads:
            cp.start()

        def cast(a, dst):
            r = arrays[a].shape[0]
            for r0 in range(0, r, cast_rows):
                r1 = min(r0 + cast_rows, r)
                dst[r0:r1, :] = raw[a][r0:r1, :].astype(BF)

        def ici(a, j, src_chip, src=None):
            if a < nh:
                rh = arrays[a].shape[0] // 2
                dst = _rows(outs[a], src_chip, c, rh)
                src = dst if src is None else _rows(src, None, c, rh)
            else:
                dst = outs[a].at[src_chip]
                src = dst if src is None else src
            return pltpu.make_async_remote_copy(
                src_ref=src, dst_ref=dst, send_sem=ici_send.at[3 * a + j], recv_sem=ici_recv.at[3 * a + j],
                device_id=(*chips[j], c), device_id_type=MESH)

        def fwd(a, j, half):
            rh = arrays[a].shape[0] // 2
            kj = 2 * chips[j][0] + chips[j][1]
            blk = _rows(outs[a], kj, half, rh)
            return pltpu.make_async_remote_copy(
                src_ref=blk, dst_ref=blk, send_sem=fwd_send.at[3 * a + j], recv_sem=fwd_recv.at[3 * a + j],
                device_id=sibling, device_id_type=MESH)

        local, sends = [], []
        for a in range(na):
            loads[a].wait()
            if a < nh:
                cast(a, stage[a])
                own = stage[a]
            else:
                own = raw[a]
            cp = pltpu.make_async_copy(own, outs[a].at[me], loc_sem.at[a])
            cp.start()
            local.append(cp)
            for j in range(3):
                cp = ici(a, j, me, src=own)
                cp.start()
                sends.append(cp)
        for i in range(npl):
            loads[na + i].wait()
            cast(na + i, stage[nh + i])
            cp = pltpu.make_async_copy(stage[nh + i], outs[na + i].at[me], loc_sem.at[na + i])
            cp.start()
            local.append(cp)
        passed = []
        for a in range(na):
            for j in range(3):
                kj = 2 * chips[j][0] + chips[j][1]
                ici(a, j, kj).wait_recv()
                if a < nh:
                    cp = fwd(a, j, c)
                    cp.start()
                    passed.append(cp)
        for a in range(nh):
            for j in range(3):
                fwd(a, j, 1 - c).wait_recv()
        for cp in sends + passed:
            cp.wait_send()
        for cp in local:
            cp.wait()

    any_spec = pl.BlockSpec(memory_space=pl.ANY)
    return pl.pallas_call(
        body, in_specs=[any_spec] * nall, out_specs=[any_spec] * nall,
        out_shape=[_sds((N_CHIPS,) + a.shape, dt) for a, dt in zip(arrays, out_dtypes)],
        scratch_shapes=[pltpu.VMEM(a.shape, a.dtype) for a in arrays]
        + [pltpu.VMEM(a.shape, BF) for a in list(halved) + list(placed)]
        + [pltpu.SemaphoreType.DMA((3 * na,)), pltpu.SemaphoreType.DMA((3 * na,)),
           pltpu.SemaphoreType.DMA((3 * nh,)), pltpu.SemaphoreType.DMA((3 * nh,)),
           pltpu.SemaphoreType.DMA((nall,)), pltpu.SemaphoreType.DMA((nall,))],
        compiler_params=pltpu.CompilerParams(vmem_limit_bytes=VMEM_LIMIT_MB << 20),
        name="all_gather_weights")(*arrays)


def _pair_sum(g, recv, c, tr, name):
    _, r, cols = g.shape
    rh = r // 2
    nr = rh // tr

    def body(c_ref, g_ref, r_ref, o_ref):
        o_ref[...] = (g_ref[...] + r_ref[...]).astype(BF)

    grid_spec = pltpu.PrefetchScalarGridSpec(
        num_scalar_prefetch=1, grid=(N_CHIPS, nr),
        in_specs=[pl.BlockSpec((1, tr, cols), lambda k, i, c_ref: (k, c_ref[0] * nr + i, 0)),
                  pl.BlockSpec((1, tr, cols), lambda k, i, c_ref: (k, i, 0))],
        out_specs=pl.BlockSpec((1, tr, cols), lambda k, i, c_ref: (k, i, 0)))
    return pl.pallas_call(
        body, grid_spec=grid_spec, out_shape=_sds((N_CHIPS, rh, cols), BF),
        compiler_params=_cp(("parallel", "parallel")), name=name)(c, g, recv)


def _chip_sum(g, sib, recv, place, tr, name):
    _, r, cols = g.shape
    rh = r // 2
    nr = rh // tr

    def body(p_ref, g_ref, s_ref, r0_ref, r1_ref, r2_ref, o_ref):
        own = g_ref[0] + s_ref[0]
        o_ref[...] = ((own + r0_ref[0].astype(F32)) + r1_ref[0].astype(F32)) + r2_ref[0].astype(F32)

    rspec = lambda j: pl.BlockSpec((1, tr, cols), lambda i, p: (j, i, 0))
    grid_spec = pltpu.PrefetchScalarGridSpec(
        num_scalar_prefetch=1, grid=(nr,),
        in_specs=[pl.BlockSpec((1, tr, cols), lambda i, p: (p[0], p[1] * nr + i, 0)),
                  pl.BlockSpec((1, tr, cols), lambda i, p: (p[0], i, 0)), rspec(0), rspec(1), rspec(2)],
        out_specs=pl.BlockSpec((tr, cols), lambda i, p: (p[1] * nr + i, 0)))
    return pl.pallas_call(
        body, grid_spec=grid_spec, out_shape=_sds((r, cols), F32),
        compiler_params=_cp(("parallel",)), name=name)(place, g, sib, recv, recv, recv)


_HBM = pl.BlockSpec(memory_space=pltpu.HBM)
_SEM = pl.BlockSpec(memory_space=pltpu.SEMAPHORE)
_EFFECT = pltpu.SideEffectType.DATAFLOW_SIDE_EFFECTING


def _remote(src, dst, ssem, rsem, k, device):
    return pltpu.make_async_remote_copy(src_ref=src, dst_ref=dst, send_sem=ssem.at[k], recv_sem=rsem.at[k],
                                        device_id=device, device_id_type=MESH)


def _split_start(name, bufs, plan, n):
    nb = len(bufs)

    def body(*refs):
        sends, _ = plan(refs[:nb], refs[nb], refs[nb + 1])
        for cp in sends:
            cp.start()
        refs[-1][...] = jnp.zeros_like(refs[-1])

    res = pl.pallas_call(
        body, name=name,
        out_shape=(pltpu.SemaphoreType.DMA((n,)), pltpu.SemaphoreType.DMA((n,)))
        + tuple(pltpu.HBM(b.shape, b.dtype) for b in bufs) + (_sds((8, 128), F32),),
        in_specs=[_HBM] * nb,
        out_specs=(_SEM, _SEM) + (_HBM,) * nb + (pl.BlockSpec(memory_space=pltpu.VMEM),),
        input_output_aliases={i: i + 2 for i in range(nb)},
        compiler_params=pltpu.CompilerParams(has_side_effects=_EFFECT),
    )(*[pltpu.with_memory_space_constraint(b, pltpu.HBM) for b in bufs])
    return res[0], res[1], list(res[2:2 + nb]), res[2 + nb]


def _split_wait(name, send_sem, recv_sem, bufs, plan, after):
    nb = len(bufs)

    def body(*refs):
        sends, arrivals = plan(refs[:nb], refs[nb], refs[nb + 1])
        for cp in sends:
            cp.wait_send()
        for cp in arrivals:
            cp.wait_recv()

    res = pl.pallas_call(
        body, name=name, out_shape=tuple(pltpu.HBM(b.shape, b.dtype) for b in bufs),
        in_specs=[_HBM] * nb + [_SEM, _SEM, pl.BlockSpec(memory_space=pl.ANY)],
        out_specs=(_HBM,) * nb, input_output_aliases={i: i for i in range(nb)},
        compiler_params=pltpu.CompilerParams(has_side_effects=_EFFECT),
    )(*bufs, send_sem, recv_sem, after)
    return list(res)


def _plan_sibling_halves(shapes):
    na = len(shapes)

    def plan(refs, ssem, rsem):
        x, y, c, _ = _place()
        cps = []
        for a in range(na):
            rh = shapes[a][1] // 2
            src = refs[a].at[:, pl.ds(pl.multiple_of((1 - c) * rh, 8), rh), :]
            cps.append(_remote(src, refs[na + a], ssem, rsem, a, (x, y, 1 - c)))
        return cps, cps

    return plan


def _to_all(ref, ssem, rsem, base):
    x, y, c, _ = _place()
    mine = ref.at[4 * x + 2 * y + c]
    return [_remote(mine, mine, ssem, rsem, base + r - 1, (x ^ (r >> 2), y ^ ((r >> 1) & 1), c ^ (r & 1)))
            for r in range(1, N_DEV)]


def _plan_chip_exchange(na, with_small):
    def plan(refs, ssem, rsem):
        _, _, c, chips = _place()
        cps = []
        for a in range(na):
            for j in range(3):
                kj = 2 * chips[j][0] + chips[j][1]
                cps.append(_remote(refs[a].at[kj], refs[na + a].at[j], ssem, rsem, 3 * a + j, (*chips[j], c)))
        if with_small:
            cps += _to_all(refs[2 * na], ssem, rsem, 3 * na)
        return cps, cps

    return plan


def _plan_sibling_swap(shapes, with_small):
    def plan(refs, ssem, rsem):
        x, y, c, _ = _place()
        sends, arrivals = [], []
        for a, shp in enumerate(shapes):
            rh = shp[0] // 2
            mine, other = _rows(refs[a], None, c, rh), _rows(refs[a], None, 1 - c, rh)
            sends.append(_remote(mine, mine, ssem, rsem, a, (x, y, 1 - c)))
            arrivals.append(_remote(mine, other, ssem, rsem, a, (x, y, 1 - c)))
        if with_small:
            cps = _to_all(refs[len(shapes)], ssem, rsem, len(shapes))
            sends += cps
            arrivals += cps
        return sends, arrivals

    return plan


def _plan_gather_chips(shapes):
    def plan(refs, ssem, rsem):
        x, y, c, chips = _place()
        me = 2 * x + y
        sends, arrivals = [], []
        for a, shp in enumerate(shapes):
            rh = shp[1] // 2
            mine = _rows(refs[a], me, c, rh)
            for j in range(3):
                land = _rows(refs[a], 2 * chips[j][0] + chips[j][1], c, rh)
                sends.append(_remote(mine, mine, ssem, rsem, 3 * a + j, (*chips[j], c)))
                arrivals.append(_remote(land, land, ssem, rsem, 3 * a + j, (*chips[j], c)))
        return sends, arrivals

    return plan


def _plan_gather_sibling(shapes):
    def plan(refs, ssem, rsem):
        x, y, c, chips = _place()
        sends, arrivals = [], []
        for a, shp in enumerate(shapes):
            rh = shp[1] // 2
            for j in range(3):
                kj = 2 * chips[j][0] + chips[j][1]
                got, land = _rows(refs[a], kj, c, rh), _rows(refs[a], kj, 1 - c, rh)
                sends.append(_remote(got, got, ssem, rsem, 3 * a + j, (x, y, 1 - c)))
                arrivals.append(_remote(got, land, ssem, rsem, 3 * a + j, (x, y, 1 - c)))
        return sends, arrivals

    return plan


def _into_slice(w, k, n, tr, dtype, name, deps=()):
    r, cols = w.shape

    def body(k_ref, w_ref, o_ref):
        o_ref[0] = w_ref[...].astype(dtype)

    grid_spec = pltpu.PrefetchScalarGridSpec(
        num_scalar_prefetch=1, grid=(r // tr,),
        in_specs=[pl.BlockSpec((tr, cols), lambda i, k: (i, 0))] + _any_specs(deps),
        out_specs=pl.BlockSpec((1, tr, cols), lambda i, k: (k[0], i, 0)))
    return pl.pallas_call(_after(body, 2, deps), grid_spec=grid_spec, out_shape=_sds((n, r, cols), dtype),
                          compiler_params=_cp(("parallel",)), name=name)(k, w, *deps)


class _LateWeights:
    def __init__(self, bufs):
        self.n = 3 * len(bufs)
        self.chips, self.sibling = _plan_gather_chips([b.shape for b in bufs]), _plan_gather_sibling([b.shape for b in bufs])
        self.ssem, self.rsem, self.bufs, token = _split_start("gather_chips_start", bufs, self.chips, self.n)
        self.first = (token,)

    def middle(self, after):
        bufs = _split_wait("gather_chips_wait", self.ssem, self.rsem, self.bufs, self.chips, after)
        self.ssem, self.rsem, self.bufs, token = _split_start("gather_sibling_start", bufs, self.sibling, self.n)
        return (token,)

    def last(self, after):
        return _split_wait("gather_sibling_wait", self.ssem, self.rsem, self.bufs, self.sibling, after)


class _GradReduce:
    def __init__(self, tag, place, names, tiles):
        self.tag, self.place, self.names, self.tiles = tag, place, names, tiles
        self.small_all = None

    def first(self, grads):
        self.na = len(grads)
        self.p1 = _plan_sibling_halves([g.shape for g in grads])
        lands = [lax.empty((N_CHIPS, g.shape[1] // 2, g.shape[2]), F32) for g in grads]
        self.ssem, self.rsem, self.bufs, token = _split_start(
            self.tag + "_halves_start", list(grads) + lands, self.p1, self.na)
        return (token,)

    def second(self, after, small=None):
        bufs = _split_wait(self.tag + "_halves_wait", self.ssem, self.rsem, self.bufs, self.p1, after)
        self.grads, self.sib = bufs[:self.na], bufs[self.na:]
        halves = [_pair_sum(g, r, self.place[1:2], t, "pair_sum_" + nm)
                  for g, r, t, nm in zip(self.grads, self.sib, self.tiles, self.names)]
        lands = [lax.empty((3,) + h.shape[1:], h.dtype) for h in halves]
        extra = [] if small is None else [small]
        self.p2 = _plan_chip_exchange(self.na, small is not None)
        self.ssem, self.rsem, self.bufs, token = _split_start(
            self.tag + "_chips_start", halves + lands + extra, self.p2, 3 * self.na + (N_DEV - 1) * len(extra))
        return (token,)

    def third(self, after, small=None):
        bufs = _split_wait(self.tag + "_chips_wait", self.ssem, self.rsem, self.bufs, self.p2, after)
        if len(bufs) > 2 * self.na:
            self.small_chips = bufs[2 * self.na]
        mine = [_chip_sum(g, sb, r, self.place, t, "chip_sum_" + nm)
                for g, sb, r, t, nm in zip(self.grads, self.sib, bufs[self.na:2 * self.na], self.tiles, self.names)]
        extra = [] if small is None else [small]
        self.p3 = _plan_sibling_swap([m.shape for m in mine], small is not None)
        self.ssem, self.rsem, self.bufs, token = _split_start(
            self.tag + "_swap_start", mine + extra, self.p3, self.na + (N_DEV - 1) * len(extra))
        return (token,)

    def last(self, after):
        bufs = _split_wait(self.tag + "_swap_wait", self.ssem, self.rsem, self.bufs, self.p3, after)
        if len(bufs) > self.na:
            self.small_swap = bufs[self.na]
        return bufs[:self.na]


class _WeightsAtHand:
    def __init__(self, wup, wout, wdown):
        self.first, self.weights = (), [wup, wout, wdown]

    def middle(self, after):
        return ()

    def last(self, after):
        return self.weights


class _GradsKept:
    def first(self, grads):
        self.grads = list(grads)
        return ()

    def second(self, after, small=None):
        return ()

    def third(self, after, small=None):
        return ()

    def last(self, after):
        return self.grads


def _forward_backward(xs, pos, tgt, win, wpool, cw, attn_norm, b_gate, pool_scale, q_norm, k_norm, sinks,
                      ffn_norm, conv_b, late, early, rest, dev):
    s = xs.shape[0]
    tm = min(512, s)
    tk = min(2048, s)
    inv_freq = ROPE_THETA ** (-jnp.arange(0, ROPE_DIM, 2, dtype=F32) / ROPE_DIM)
    lane = jnp.arange(2 * HEAD) % HEAD
    invf = jnp.where(lane < ROPE_DIM, inv_freq[lane % (ROPE_DIM // 2)], 0.0).reshape(1, 2 * HEAD)
    wq = jnp.tile(q_norm, (1, N_Q))
    wk = jnp.tile(k_norm, (1, N_KV))
    head_of = jnp.arange(Q_W) // HEAD
    bd = (head_of[:, None] == head_of[None, :]).astype(BF)
    sink = sinks[0]

    h1, u, qkv, gates = _attn_in_proj(xs, attn_norm, win, b_gate, tm, deps=late.first)
    qh, kh, vh = _qk_prep(qkv, pos, wq, wk, invf, bd, tm)
    apool, pooled = _pool_fwd(u, wpool, pool_scale, min(2048, s), deps=(qh,))
    battn = _attn_fwd(qh, kh, vh, sink, deps=late.middle(apool))
    wup, wout, wdown = late.last(battn)
    wout = wout.reshape(D, D)
    wdown = wdown.reshape(D_FF, D)
    mix, y1, h2 = _mix_out_proj(apool, battn, gates, xs, wout, ffn_norm, tm)
    pre_g, pre_v, up_g, up_v, act = _ffn_up(h2, wup, cw, conv_b, tm)
    dy2, dy2b, loss_acc = _ffn_down_loss(act, wdown, y1, tgt, tm)

    d_wdown = _grad_matmul(act, dy2b, 512, tk, "grad_w_down")
    dp_g, dp_v, dcw_g, dcw_v, dcb_g, dcb_v = _ffn_act_bwd(dy2b, wdown, up_g, up_v, pre_g, pre_v, cw, tm)
    d_wup = _grad_matmul(h2, dp_g, UP_SHARD, tk, "grad_w_up_gate", lead=N_CHIPS)
    d_wup = _grad_matmul(h2, dp_v, UP_SHARD, tk, "grad_w_up_value", lead=N_CHIPS, prev=d_wup, lead_off=2)
    token = early.first([d_wdown.reshape(N_CHIPS, D_FF // N_CHIPS, D), d_wup])
    dy1, d_ffn_norm = _ffn_up_bwd(dp_g, dp_v, wup, y1, dy2, ffn_norm, tm, deps=token)
    token = early.second(dy1)
    da, db, dzg, d_bgate, d_wout = _out_proj_bwd(dy1, wout, apool, battn, gates, mix, tm, deps=token)
    du, d_wpool, d_pscale, d_win_pool = _pool_bwd(da, pooled, wpool, pool_scale, h1, min(2048, s))
    dqh, dkh, dvh, dsink = _attn_bwd(qh, kh, vh, sink, db)
    token = early.third(dqh)
    dzq, d_qn, d_kn = _qk_prep_bwd(dqh, dkh, dvh, qkv, pos, wq, wk, invf, bd, tm, deps=token)
    d_win_t = jnp.concatenate([
        d_win_pool,
        _grad_matmul(dzq, h1, D, tk, "grad_w_in_qkv"),
        _grad_matmul(dzg, h1, D, tk, "grad_w_in_gates")], axis=0)
    token = rest.first([
        d_win_t.reshape(N_CHIPS, IN_W // N_CHIPS, D),
        d_wout.reshape(N_CHIPS, D // N_CHIPS, D),
        d_wpool.reshape(4, N_CHIPS, 64, POOL_GROUP).transpose(1, 0, 2, 3).reshape(N_CHIPS, 4 * 64, POOL_GROUP)])
    small = _pack_small(loss_acc, d_bgate, d_pscale, d_qn, d_kn, dsink, d_ffn_norm, dcb_g, dcb_v, dcw_g, dcw_v, dev)
    token = rest.second(token[0] if token else None, small=small)
    grad_x, d_attn_norm = _in_proj_bwd(du, dzq, dzg, win, xs, attn_norm, dy1, tm, deps=token)
    return grad_x, d_attn_norm, small


def kernel(x, positions, attn_norm, w_in, b_gate, w_pool, pool_scale, q_norm, k_norm, sinks, w_out, ffn_norm, w_up, conv_w, conv_b, w_down, loss_target, m_attn_norm, m_w_in, m_b_gate, m_w_pool, m_pool_scale, m_q_norm, m_k_norm, m_sinks, m_w_out, m_ffn_norm, m_w_up, m_conv_w, m_conv_b, m_w_down, v_attn_norm, v_w_in, v_b_gate, v_w_pool, v_pool_scale, v_q_norm, v_k_norm, v_sinks, v_w_out, v_ffn_norm, v_w_up, v_conv_w, v_conv_b, v_w_down):
    s = x.shape[1]
    xs = x[0]
    tgt = loss_target[0]
    pos = positions[0].reshape(s, 1)
    cx, cy, cc = lax.axis_index("x"), lax.axis_index("y"), lax.axis_index("c")
    chip = 2 * cx + cy

    dev_arr = (2 * chip + cc).reshape(1).astype(jnp.int32)
    place = jnp.stack([chip, cc]).astype(jnp.int32)

    g_in, g_pool, g_cw, *own_late = _all_gather_weights(
        [jnp.swapaxes(w_in[0], 0, 1), w_pool[0].reshape(4 * 64, POOL_GROUP)], [conv_w[0]],
        [w_up[0], w_out[0], w_down[0]])
    win = g_in.reshape(IN_W, D)
    wpool = g_pool.reshape(N_CHIPS, 4, 64, POOL_GROUP).transpose(1, 0, 2, 3).reshape(4, POOL_GROUP, POOL_GROUP)
    late = _LateWeights(own_late)
    early = _GradReduce("early", place, ["w_down", "w_up"], [352, 512])
    rest = _GradReduce("rest", place, ["w_in", "w_out", "w_pool"], [544, 128, 128])

    grad_x, d_attn_norm, _ = _forward_backward(
        xs, pos, tgt, win, wpool, g_cw, attn_norm, b_gate, pool_scale, q_norm, k_norm, sinks, ffn_norm, conv_b,
        late, early, rest, dev_arr)

    def two_d(a):
        return a.reshape(-1, a.shape[-1])

    def update(nm, w, g, m, v, tr, deps=()):
        res = _adamw(two_d(w), g, two_d(m), two_d(v), tr, "adamw_" + nm, deps=deps)
        return [r.reshape(w.shape) for r in res]

    attn_stack = _into_slice(d_attn_norm, dev_arr, N_DEV, 1, F32, "own_attn_norm")
    g_wdown, g_wup = early.last(grad_x)
    big_out = {"w_up": update("w_up", w_up, g_wup, m_w_up, v_w_up, 256)}
    big_out["w_down"] = update("w_down", w_down, g_wdown, m_w_down, v_w_down, 352, deps=(big_out["w_up"][1],))
    token = rest.third(big_out["w_down"][1], small=attn_stack)
    g_win_t, g_wout, g_wpool = rest.last(token[0])
    small_out, g_convw_all, loss = _small_update(rest.small_chips, rest.small_swap, {
        "attn_norm": (attn_norm, m_attn_norm, v_attn_norm), "b_gate": (b_gate, m_b_gate, v_b_gate),
        "pool_scale": (pool_scale, m_pool_scale, v_pool_scale), "q_norm": (q_norm, m_q_norm, v_q_norm),
        "k_norm": (k_norm, m_k_norm, v_k_norm), "sinks": (sinks, m_sinks, v_sinks),
        "ffn_norm": (ffn_norm, m_ffn_norm, v_ffn_norm), "conv_b": (conv_b, m_conv_b, v_conv_b)})
    g_convw = lax.dynamic_slice_in_dim(g_convw_all, chip * UP_SHARD, UP_SHARD, axis=1)
    small_out["conv_w"] = update("conv_w", conv_w, g_convw, m_conv_w, v_conv_w, 3)
    flip = lambda a: jnp.swapaxes(a[0], 0, 1)
    res = _adamw(flip(w_in), g_win_t, flip(m_w_in), flip(v_w_in), 544, "adamw_w_in")
    big_out["w_in"] = [jnp.swapaxes(r, 0, 1)[None] for r in res]
    big_out["w_out"] = update("w_out", w_out, g_wout, m_w_out, v_w_out, 256)
    big_out["w_pool"] = update("w_pool", w_pool, g_wpool, m_w_pool, v_w_pool, 256)

    order = ["attn_norm", "w_in", "b_gate", "w_pool", "pool_scale", "q_norm", "k_norm", "sinks", "w_out",
             "ffn_norm", "w_up", "conv_w", "conv_b", "w_down"]
    allout = {**big_out, **small_out}
    outs = [loss.reshape(()), grad_x[None]]
    for k in range(4):
        outs += [allout[nm][k] for nm in order]
    return tuple(outs)
```

```python
import functools

import jax
import jax.numpy as jnp
import numpy as np
from jax import lax
from jax.experimental import pallas as pl
from jax.experimental.pallas import tpu as pltpu

D = 1024
D_FF = 2816
HEAD = 64
N_Q = 16
N_KV = 2
GQA = 8
BLK = 128
ROPE_DIM = 16
ROPE_THETA = 500000.0
POOL_GROUP = 256
Q_W = 1024
KV_W = 128
QKV_W = Q_W + 2 * KV_W
IN_W = 4352
UP_SHARD = 1408
EPS = 1e-6
N_CHIPS = 4
N_DEV = 8

LR = 0.001
B1 = 0.9
B2 = 0.999
ADAM_EPS = 1e-08
WD = 0.01
STEP = 10

BF = jnp.bfloat16
F32 = jnp.float32
MESH = pl.DeviceIdType.MESH
VMEM_LIMIT_MB = 56


def _cp(sem, vmem_mb=VMEM_LIMIT_MB):
    return pltpu.CompilerParams(dimension_semantics=sem, vmem_limit_bytes=vmem_mb << 20)


def _full(shape):
    nd = len(shape)
    return pl.BlockSpec(shape, lambda *_: (0,) * nd)


def _sds(shape, dtype):
    return jax.ShapeDtypeStruct(shape, dtype)


def _after(body, n_in, deps):
    nd = len(deps)
    if nd == 0:
        return body

    def ordered(*refs):
        return body(*refs[:n_in], *refs[n_in + nd:])

    return ordered


def _any_specs(deps):
    return [pl.BlockSpec(memory_space=pl.ANY)] * len(deps)


def _nt(a, b):
    return lax.dot_general(a, b, (((1,), (1,)), ((), ())), preferred_element_type=F32)


def _tn(a, b):
    return lax.dot_general(a, b, (((0,), (0,)), ((), ())), preferred_element_type=F32)


def _mm(a, b):
    return jnp.dot(a, b, preferred_element_type=F32)


def _head_sum(v, bd):
    return _mm(v.astype(BF), bd)


def _rope_tables(pos_ref, invf_ref):
    ang = pos_ref[...].astype(F32) * invf_ref[...]
    cos = jnp.cos(ang)
    sin = jnp.sin(ang)
    lane = lax.broadcasted_iota(jnp.int32, (1, 2 * HEAD), 1) % HEAD
    sa = jnp.where(lane < ROPE_DIM // 2, -sin, 0.0)
    sb = jnp.where(lane < ROPE_DIM // 2, 0.0, jnp.where(lane < ROPE_DIM, sin, 0.0))
    return cos, sa, sb


def _tile_lanes(t, reps):
    return t if reps == 1 else jnp.tile(t, (1, reps))


def _rope(v, cos, sa, sb):
    w = v.shape[1]
    reps = w // (2 * HEAD)
    half = ROPE_DIM // 2
    return (v * _tile_lanes(cos, reps) + pltpu.roll(v, w - half, 1) * _tile_lanes(sa, reps)
            + pltpu.roll(v, half, 1) * _tile_lanes(sb, reps))


def _rope_t(dy, cos, sa, sb):
    w = dy.shape[1]
    reps = w // (2 * HEAD)
    half = ROPE_DIM // 2
    return (dy * _tile_lanes(cos, reps) + pltpu.roll(dy * _tile_lanes(sa, reps), half, 1)
            + pltpu.roll(dy * _tile_lanes(sb, reps), w - half, 1))


def _attn_in_proj(x, g1, w_in, b_gate, tm, deps=()):
    s = x.shape[0]

    def body(x_ref, g_ref, w_ref, b_ref, h_ref, u_ref, qkv_ref, gate_ref):
        xv = x_ref[...]
        r = lax.rsqrt(jnp.mean(xv * xv, axis=-1, keepdims=True) + EPS)
        h = (xv * r * g_ref[...]).astype(BF)
        h_ref[...] = h
        u_ref[...] = _nt(h, w_ref[0:D, :])
        qkv_ref[...] = _nt(h, w_ref[D:D + QKV_W, :])
        gate_ref[...] = jax.nn.sigmoid(_nt(h, w_ref[D + QKV_W:IN_W, :]) + b_ref[...]).astype(BF)

    row = lambda w: pl.BlockSpec((tm, w), lambda i: (i, 0))
    return pl.pallas_call(
        _after(body, 4, deps), grid=(s // tm,),
        in_specs=[row(D), _full((1, D)), _full((IN_W, D)), _full((1, 2 * D))] + _any_specs(deps),
        out_specs=[row(D), row(D), row(QKV_W), row(2 * D)],
        out_shape=[_sds((s, D), BF), _sds((s, D), F32), _sds((s, QKV_W), F32), _sds((s, 2 * D), BF)],
        compiler_params=_cp(("parallel",)), name="attn_in_proj")(x, g1, w_in, b_gate, *deps)


def _qk_prep(qkv, pos, wq, wk, invf, bd, tm):
    s = qkv.shape[0]

    def body(qkv_ref, pos_ref, wq_ref, wk_ref, invf_ref, bd_ref, qh_ref, kh_ref, vh_ref):
        cos, sa, sb = _rope_tables(pos_ref, invf_ref)
        q = qkv_ref[:, 0:Q_W]
        k = qkv_ref[:, Q_W:Q_W + KV_W]
        v = qkv_ref[:, Q_W + KV_W:QKV_W]
        rq = lax.rsqrt(_head_sum(q * q, bd_ref[...]) * (1.0 / HEAD) + EPS)
        qr = _rope(q * rq * wq_ref[...], cos, sa, sb) * (HEAD ** -0.5)
        rk = lax.rsqrt(_head_sum(k * k, bd_ref[0:KV_W, 0:KV_W]) * (1.0 / HEAD) + EPS)
        kr = _rope(k * rk * wk_ref[...], cos, sa, sb)
        for h in range(N_Q):
            qh_ref[h] = qr[:, HEAD * h:HEAD * (h + 1)].astype(BF)
        for h in range(N_KV):
            kh_ref[h] = kr[:, HEAD * h:HEAD * (h + 1)].astype(BF)
            vh_ref[h] = v[:, HEAD * h:HEAD * (h + 1)].astype(BF)

    heads = lambda n: pl.BlockSpec((n, tm, HEAD), lambda i: (0, i, 0))
    return pl.pallas_call(
        body, grid=(s // tm,),
        in_specs=[pl.BlockSpec((tm, QKV_W), lambda i: (i, 0)), pl.BlockSpec((tm, 1), lambda i: (i, 0)),
                  _full((1, Q_W)), _full((1, KV_W)), _full((1, 2 * HEAD)), _full((Q_W, Q_W))],
        out_specs=[heads(N_Q), heads(N_KV), heads(N_KV)],
        out_shape=[_sds((N_Q, s, HEAD), BF), _sds((N_KV, s, HEAD), BF), _sds((N_KV, s, HEAD), BF)],
        compiler_params=_cp(("parallel",)), name="qk_prep")(qkv, pos, wq, wk, invf, bd)


def _sink_column(sink_ref, kh):
    row_g = lax.broadcasted_iota(jnp.int32, (GQA * BLK, 1), 0) // BLK
    col = jnp.zeros((GQA * BLK, 1), F32)
    for g in range(GQA):
        col = jnp.where(row_g == g, sink_ref[kh * GQA + g], col)
    return col


def _fold_band(band, lower, first=None):
    prev, cur = band[:, 0:BLK], band[:, BLK:2 * BLK]
    if first is not None:
        prev = jnp.where(first, -jnp.inf, prev)
    return jnp.where(lower, cur, prev)


def _unfold_band(x, lower):
    return jnp.concatenate([jnp.where(lower, 0.0, x), jnp.where(lower, x, 0.0)], axis=1)


def _attn_probs(q, k, n, sink_col):
    sc = _nt(q, k)
    qi = lax.broadcasted_iota(jnp.int32, (sc.shape[0], BLK), 0) % BLK
    ki = lax.broadcasted_iota(jnp.int32, (sc.shape[0], BLK), 1)
    lower = ki <= qi
    sc = _fold_band(sc, lower, first=n == 0)
    m = jnp.maximum(jnp.max(sc, axis=-1, keepdims=True), sink_col)
    p = jnp.exp(sc - m)
    es = jnp.exp(sink_col - m)
    inv = 1.0 / (jnp.sum(p, axis=-1, keepdims=True) + es)
    return p * inv, es * inv, lower


def _attn_fwd(qh, kh, vh, sinks, deps=()):
    s = qh.shape[1]
    assert s % (2 * BLK) == 0

    def body(sink_ref, q_ref, kp_ref, kc_ref, vp_ref, vc_ref, o_ref):
        i = pl.program_id(0)
        for sub in range(2):
            for khd in range(N_KV):
                q = q_ref[khd * GQA:(khd + 1) * GQA, sub * BLK:(sub + 1) * BLK].reshape(GQA * BLK, HEAD)
                if sub == 0:
                    k = jnp.concatenate([kp_ref[khd], kc_ref[khd, 0:BLK]], axis=0)
                    v = jnp.concatenate([vp_ref[khd], vc_ref[khd, 0:BLK]], axis=0)
                else:
                    k, v = kc_ref[khd], vc_ref[khd]
                probs, _, lower = _attn_probs(q, k, 2 * i + sub, _sink_column(sink_ref, khd))
                o = _mm(_unfold_band(probs, lower).astype(BF), v)
                for j in range(GQA // 2):
                    c0 = khd * GQA * HEAD + 2 * HEAD * j
                    o_ref[sub * BLK:(sub + 1) * BLK, c0:c0 + 2 * HEAD] = jnp.concatenate(
                        [o[2 * j * BLK:(2 * j + 1) * BLK], o[(2 * j + 1) * BLK:(2 * j + 2) * BLK]],
                        axis=1).astype(BF)

    prev = pl.BlockSpec((N_KV, BLK, HEAD), lambda i: (0, jnp.maximum(2 * i - 1, 0), 0))
    cur = pl.BlockSpec((N_KV, 2 * BLK, HEAD), lambda i: (0, i, 0))
    return pl.pallas_call(
        _after(body, 6, deps), grid=(s // (2 * BLK),),
        in_specs=[pl.BlockSpec(memory_space=pltpu.SMEM),
                  pl.BlockSpec((N_Q, 2 * BLK, HEAD), lambda i: (0, i, 0)), prev, cur, prev, cur] + _any_specs(deps),
        out_specs=pl.BlockSpec((2 * BLK, Q_W), lambda i: (i, 0)),
        out_shape=_sds((s, Q_W), BF),
        compiler_params=_cp(("parallel",)), name="attn_fwd")(sinks, qh, kh, kh, vh, vh, *deps)


def _pool_fwd(u, w_pool, pool_scale, ts, deps=()):
    s = u.shape[0]
    halo = 16

    def body(u_ref, wp_ref, ps_ref, a_ref, pooled_ref, prev):
        g = pl.program_id(0)
        i = pl.program_id(1)

        @pl.when(i == 0)
        def _():
            prev[...] = jnp.zeros_like(prev)

        cur = u_ref[...]
        ext = jnp.concatenate([prev[...], cur], axis=0)
        t = (i * ts + lax.broadcasted_iota(jnp.int32, (ts, 1), 0)).astype(F32)
        for gi in range(4):
            @pl.when(g == gi)
            def _(gi=gi):
                w = 2 << gi
                acc, span = ext, 1
                while span < w:
                    acc = acc + pltpu.roll(acc, span, 0)
                    span *= 2
                inv = 1.0 / jnp.minimum(t + 1.0, float(w))
                pooled = (acc[halo:halo + ts] * inv - cur).astype(BF)
                pooled_ref[...] = pooled
                a_ref[...] = (_mm(pooled, wp_ref[0]) * ps_ref[...]).astype(BF)

        prev[...] = cur[ts - halo:ts]

    col = pl.BlockSpec((ts, POOL_GROUP), lambda g, i: (i, g))
    return pl.pallas_call(
        _after(body, 3, deps), grid=(4, s // ts),
        in_specs=[col, pl.BlockSpec((1, POOL_GROUP, POOL_GROUP), lambda g, i: (g, 0, 0)),
                  pl.BlockSpec((1, POOL_GROUP), lambda g, i: (0, g))] + _any_specs(deps),
        out_specs=[col, col],
        out_shape=[_sds((s, D), BF), _sds((s, D), BF)],
        scratch_shapes=[pltpu.VMEM((halo, POOL_GROUP), F32)],
        compiler_params=_cp(("parallel", "arbitrary")), name="pool_fwd")(u, w_pool, pool_scale, *deps)


def _mix_out_proj(a, b, gates, x, w_out, g2, tm):
    s = x.shape[0]

    def body(a_ref, b_ref, gate_ref, x_ref, w_ref, g_ref, mix_ref, y_ref, h_ref):
        mix = (gate_ref[:, 0:D].astype(F32) * a_ref[...].astype(F32)
               + gate_ref[:, D:2 * D].astype(F32) * b_ref[...].astype(F32)).astype(BF)
        mix_ref[...] = mix
        y = x_ref[...] + _mm(mix, w_ref[...])
        y_ref[...] = y
        r = lax.rsqrt(jnp.mean(y * y, axis=-1, keepdims=True) + EPS)
        h_ref[...] = (y * r * g_ref[...]).astype(BF)

    row = lambda w: pl.BlockSpec((tm, w), lambda i: (i, 0))
    return pl.pallas_call(
        body, grid=(s // tm,),
        in_specs=[row(D), row(D), row(2 * D), row(D), _full((D, D)), _full((1, D))],
        out_specs=[row(D), row(D), row(D)],
        out_shape=[_sds((s, D), BF), _sds((s, D), F32), _sds((s, D), BF)],
        compiler_params=_cp(("parallel",)), name="mix_out_proj")(a, b, gates, x, w_out, g2)


def _ffn_up(h2, w_up, conv_w, conv_b, tm):
    s = h2.shape[0]

    def body(h_ref, wg_ref, wv_ref, cwg_ref, cwv_ref, cbg_ref, cbv_ref,
             preg_ref, prev_ref, upg_ref, upv_ref, act_ref, halog, halov):
        i = pl.program_id(1)

        @pl.when(i == 0)
        def _():
            halog[...] = jnp.zeros_like(halog)
            halov[...] = jnp.zeros_like(halov)

        h = h_ref[...]

        def conv_half(w_ref, cw_ref, cb_ref, halo, pre_ref, up_ref):
            pre = _mm(h, w_ref[0])
            pre_ref[...] = pre.astype(BF)
            ext = jnp.concatenate([halo[...], pre], axis=0)
            cw = cw_ref[0]
            up = cb_ref[...] + cw[0:1] * pltpu.roll(ext, 2, 0)[8:8 + tm]
            up = up + cw[1:2] * pltpu.roll(ext, 1, 0)[8:8 + tm]
            up = up + cw[2:3] * pre
            halo[...] = pre[tm - 8:tm]
            up_ref[...] = up.astype(BF)
            return up

        gate = conv_half(wg_ref, cwg_ref, cbg_ref, halog, preg_ref, upg_ref)
        val = conv_half(wv_ref, cwv_ref, cbv_ref, halov, prev_ref, upv_ref)
        act_ref[...] = (gate * jax.nn.sigmoid(gate) * val).astype(BF)

    tile = pl.BlockSpec((tm, UP_SHARD), lambda j, i: (i, j))
    wspec = lambda off: pl.BlockSpec((1, D, UP_SHARD), lambda j, i: (j + off, 0, 0))
    cwspec = lambda off: pl.BlockSpec((1, 3, UP_SHARD), lambda j, i: (j + off, 0, 0))
    cbspec = lambda off: pl.BlockSpec((1, UP_SHARD), lambda j, i: (0, j + off))
    half = _sds((s, D_FF), BF)
    return pl.pallas_call(
        body, grid=(2, s // tm),
        in_specs=[pl.BlockSpec((tm, D), lambda j, i: (i, 0)), wspec(0), wspec(2), cwspec(0), cwspec(2),
                  cbspec(0), cbspec(2)],
        out_specs=[tile] * 5, out_shape=[half] * 5,
        scratch_shapes=[pltpu.VMEM((8, UP_SHARD), F32), pltpu.VMEM((8, UP_SHARD), F32)],
        compiler_params=_cp(("parallel", "arbitrary")), name="ffn_up")(
            h2, w_up, w_up, conv_w, conv_w, conv_b, conv_b)


def _ffn_down_loss(act, w_down, y1, tgt, tm):
    s = y1.shape[0]

    def body(act_ref, w_ref, y_ref, t_ref, dy_ref, dyb_ref, loss_ref):
        @pl.when(pl.program_id(0) == 0)
        def _():
            loss_ref[...] = jnp.zeros_like(loss_ref)

        e = y_ref[...] + _mm(act_ref[...], w_ref[...]) - t_ref[...]
        dy = e * (1.0 / D)
        dy_ref[...] = dy
        dyb_ref[...] = dy.astype(BF)
        e2 = (e * e).reshape(tm // 8, 8, D).sum(axis=0)
        part = e2[:, 0:128]
        for j in range(1, D // 128):
            part = part + e2[:, 128 * j:128 * (j + 1)]
        loss_ref[...] += part

    row = lambda w: pl.BlockSpec((tm, w), lambda i: (i, 0))
    return pl.pallas_call(
        body, grid=(s // tm,),
        in_specs=[row(D_FF), _full((D_FF, D)), row(D), row(D)],
        out_specs=[row(D), row(D), _full((8, 128))],
        out_shape=[_sds((s, D), F32), _sds((s, D), BF), _sds((8, 128), F32)],
        compiler_params=_cp(("arbitrary",)), name="ffn_down_loss")(act, w_down, y1, tgt)


def _grad_matmul(a, b, tn, tk, name, lead=None, prev=None, lead_off=0):
    s, m = a.shape
    n = b.shape[1]
    nj = n // tn

    def body(*refs):
        a_ref, b_ref = refs[0], refs[1]
        o_ref = refs[-1]
        acc = _tn(a_ref[...], b_ref[...])
        acc = acc if lead is None else acc[None]

        @pl.when(pl.program_id(1) == 0)
        def _():
            o_ref[...] = acc

        @pl.when(pl.program_id(1) > 0)
        def _():
            o_ref[...] += acc

    in_specs = [pl.BlockSpec((tk, m), lambda j, k: (k, 0)), pl.BlockSpec((tk, tn), lambda j, k: (k, j))]
    args = [a, b]
    aliases = {}
    if lead is None:
        out_spec = pl.BlockSpec((m, tn), lambda j, k: (0, j))
        out_shape = _sds((m, n), F32)
    else:
        out_spec = pl.BlockSpec((1, m, tn), lambda j, k: (j + lead_off, 0, 0))
        out_shape = _sds((lead, m, tn), F32)
        if prev is not None:
            in_specs.append(pl.BlockSpec(memory_space=pl.ANY))
            args.append(prev)
            aliases = {2: 0}
    return pl.pallas_call(
        body, grid=(nj, s // tk), in_specs=in_specs, out_specs=out_spec, out_shape=out_shape,
        input_output_aliases=aliases,
        compiler_params=_cp(("parallel", "arbitrary")), name=name)(*args)


def _ffn_act_bwd(dyb, w_down, up_g, up_v, pre_g, pre_v, conv_w, tm):
    s = dyb.shape[0]
    nt = s // tm

    def body(dy_ref, wd_ref, upg_ref, upv_ref, preg_ref, prev_ref, cwg_ref, cwv_ref,
             dpg_ref, dpv_ref, dcwg_ref, dcwv_ref, dcbg_ref, dcbv_ref, nxg, nxv):
        i = pl.program_id(1)

        @pl.when(i == 0)
        def _():
            nxg[...] = jnp.zeros_like(nxg)
            nxv[...] = jnp.zeros_like(nxv)
            dcwg_ref[...] = jnp.zeros_like(dcwg_ref)
            dcwv_ref[...] = jnp.zeros_like(dcwv_ref)
            dcbg_ref[...] = jnp.zeros_like(dcbg_ref)
            dcbv_ref[...] = jnp.zeros_like(dcbv_ref)

        dact = _nt(dy_ref[...], wd_ref[...])
        g = upg_ref[...].astype(F32)
        v = upv_ref[...].astype(F32)
        sg = jax.nn.sigmoid(g)
        d_v = dact * (g * sg)
        d_g = dact * v * (sg * (1.0 + g * (1.0 - sg)))

        def conv_bwd(d_up, nx, pre_ref, cw_ref, dp_ref, dcw_ref, dcb_ref):
            ext = jnp.concatenate([d_up, nx[...]], axis=0)
            s1 = pltpu.roll(ext, tm + 8 - 1, 0)[0:tm]
            s2 = pltpu.roll(ext, tm + 8 - 2, 0)[0:tm]
            cw = cw_ref[0]
            dp_ref[...] = (cw[2:3] * d_up + cw[1:2] * s1 + cw[0:1] * s2).astype(BF)
            nx[...] = d_up[0:8]
            pre = pre_ref[...].astype(F32)
            dcw_ref[0, 0:1, :] += jnp.sum(s2 * pre, axis=0, keepdims=True)
            dcw_ref[0, 1:2, :] += jnp.sum(s1 * pre, axis=0, keepdims=True)
            dcw_ref[0, 2:3, :] += jnp.sum(d_up * pre, axis=0, keepdims=True)
            dcb_ref[...] += jnp.sum(d_up, axis=0, keepdims=True)

        conv_bwd(d_g, nxg, preg_ref, cwg_ref, dpg_ref, dcwg_ref, dcbg_ref)
        conv_bwd(d_v, nxv, prev_ref, cwv_ref, dpv_ref, dcwv_ref, dcbv_ref)

    tile = pl.BlockSpec((tm, UP_SHARD), lambda j, i: (nt - 1 - i, j))
    cwspec = lambda off: pl.BlockSpec((1, 3, UP_SHARD), lambda j, i: (j + off, 0, 0))
    acc_cw = pl.BlockSpec((1, 3, UP_SHARD), lambda j, i: (j, 0, 0))
    acc_cb = pl.BlockSpec((1, UP_SHARD), lambda j, i: (0, j))
    buf = pltpu.VMEM((8, UP_SHARD), F32)
    return pl.pallas_call(
        body, grid=(2, nt),
        in_specs=[pl.BlockSpec((tm, D), lambda j, i: (nt - 1 - i, 0)),
                  pl.BlockSpec((UP_SHARD, D), lambda j, i: (j, 0)),
                  tile, tile, tile, tile, cwspec(0), cwspec(2)],
        out_specs=[tile, tile, acc_cw, acc_cw, acc_cb, acc_cb],
        out_shape=[_sds((s, D_FF), BF), _sds((s, D_FF), BF), _sds((2, 3, UP_SHARD), F32),
                   _sds((2, 3, UP_SHARD), F32), _sds((1, D_FF), F32), _sds((1, D_FF), F32)],
        scratch_shapes=[buf, buf],
        compiler_params=_cp(("parallel", "arbitrary")), name="ffn_act_bwd")(
            dyb, w_down, up_g, up_v, pre_g, pre_v, conv_w, conv_w)


def _rms_bwd(dh, y, g):
    r = lax.rsqrt(jnp.mean(y * y, axis=-1, keepdims=True) + EPS)
    n = y * r
    dn = dh * g
    return r * (dn - n * jnp.mean(dn * n, axis=-1, keepdims=True)), dh * n


def _ffn_up_bwd(dp_g, dp_v, w_up, y1, dy2, g2, tm, deps=()):
    s = y1.shape[0]

    def body(dg_ref, dv_ref, w_ref, y_ref, dy2_ref, g_ref, dy1_ref, dgn_ref):
        @pl.when(pl.program_id(0) == 0)
        def _():
            dgn_ref[...] = jnp.zeros_like(dgn_ref)

        dh = _nt(dg_ref[:, 0:UP_SHARD], w_ref[0])
        dh = dh + _nt(dg_ref[:, UP_SHARD:D_FF], w_ref[1])
        dh = dh + _nt(dv_ref[:, 0:UP_SHARD], w_ref[2])
        dh = dh + _nt(dv_ref[:, UP_SHARD:D_FF], w_ref[3])
        dy, dgn = _rms_bwd(dh, y_ref[...], g_ref[...])
        dy1_ref[...] = dy2_ref[...] + dy
        dgn_ref[...] += jnp.sum(dgn, axis=0, keepdims=True)

    row = lambda w: pl.BlockSpec((tm, w), lambda i: (i, 0))
    return pl.pallas_call(
        _after(body, 6, deps), grid=(s // tm,),
        in_specs=[row(D_FF), row(D_FF), _full((4, D, UP_SHARD)), row(D), row(D), _full((1, D))] + _any_specs(deps),
        out_specs=[row(D), _full((1, D))],
        out_shape=[_sds((s, D), F32), _sds((1, D), F32)],
        compiler_params=_cp(("arbitrary",)), name="ffn_up_bwd")(dp_g, dp_v, w_up, y1, dy2, g2, *deps)


def _out_proj_bwd(dy1, w_out, a, b, gates, mix, tm, deps=()):
    s = dy1.shape[0]

    def body(dy_ref, w_ref, a_ref, b_ref, gate_ref, mix_ref, da_ref, db_ref, dzg_ref, dbg_ref, dw_ref):
        @pl.when(pl.program_id(0) == 0)
        def _():
            dbg_ref[...] = jnp.zeros_like(dbg_ref)
            dw_ref[...] = jnp.zeros_like(dw_ref)

        dyb = dy_ref[...].astype(BF)
        dmix = _nt(dyb, w_ref[...])
        gp = gate_ref[:, 0:D].astype(F32)
        ga = gate_ref[:, D:2 * D].astype(F32)
        da_ref[...] = (dmix * gp).astype(BF)
        db_ref[...] = (dmix * ga).astype(BF)
        dzp = dmix * a_ref[...].astype(F32) * (gp * (1.0 - gp))
        dza = dmix * b_ref[...].astype(F32) * (ga * (1.0 - ga))
        dzg_ref[:, 0:D] = dzp.astype(BF)
        dzg_ref[:, D:2 * D] = dza.astype(BF)
        dbg_ref[:, 0:D] += jnp.sum(dzp, axis=0, keepdims=True)
        dbg_ref[:, D:2 * D] += jnp.sum(dza, axis=0, keepdims=True)
        dw_ref[...] += _tn(mix_ref[...], dyb)

    row = lambda w: pl.BlockSpec((tm, w), lambda i: (i, 0))
    return pl.pallas_call(
        _after(body, 6, deps), grid=(s // tm,),
        in_specs=[row(D), _full((D, D)), row(D), row(D), row(2 * D), row(D)] + _any_specs(deps),
        out_specs=[row(D), row(D), row(2 * D), _full((1, 2 * D)), _full((D, D))],
        out_shape=[_sds((s, D), BF), _sds((s, D), BF), _sds((s, 2 * D), BF), _sds((1, 2 * D), F32),
                   _sds((D, D), F32)],
        compiler_params=_cp(("arbitrary",)), name="out_proj_bwd")(dy1, w_out, a, b, gates, mix, *deps)


def _pool_bwd(da, pooled, w_pool, pool_scale, h1, ts):
    s = da.shape[0]
    nt = s // ts
    halo = 16

    def body(da_ref, pooled_ref, wp_ref, ps_ref, h_ref, du_ref, dwp_ref, dps_ref, dwi_ref, nxt):
        g = pl.program_id(0)
        i = pl.program_id(1)
        ti = nt - 1 - i

        @pl.when(i == 0)
        def _():
            nxt[...] = jnp.zeros_like(nxt)
            dwp_ref[...] = jnp.zeros_like(dwp_ref)
            dps_ref[...] = jnp.zeros_like(dps_ref)
            dwi_ref[...] = jnp.zeros_like(dwi_ref)

        pooled = pooled_ref[...]
        dav = da_ref[...].astype(F32)
        dps_ref[...] += jnp.sum(dav * _mm(pooled, wp_ref[0]), axis=0, keepdims=True)
        dm = (dav * ps_ref[...]).astype(BF)
        dwp_ref[0] += _tn(pooled, dm)
        dpool = _nt(dm, wp_ref[0])
        t = (ti * ts + lax.broadcasted_iota(jnp.int32, (ts, 1), 0)).astype(F32)
        for gi in range(4):
            @pl.when(g == gi)
            def _(gi=gi):
                w = 2 << gi
                e = dpool * (1.0 / jnp.minimum(t + 1.0, float(w)))
                acc, span = jnp.concatenate([e, nxt[...]], axis=0), 1
                while span < w:
                    acc = acc + pltpu.roll(acc, ts + halo - span, 0)
                    span *= 2
                du = (acc[0:ts] - dpool).astype(BF)
                du_ref[...] = du
                dwi_ref[...] += _tn(du, h_ref[...])
                nxt[...] = e[0:halo]

    col = pl.BlockSpec((ts, POOL_GROUP), lambda g, i: (nt - 1 - i, g))
    return pl.pallas_call(
        body, grid=(4, nt),
        in_specs=[col, col, pl.BlockSpec((1, POOL_GROUP, POOL_GROUP), lambda g, i: (g, 0, 0)),
                  pl.BlockSpec((1, POOL_GROUP), lambda g, i: (0, g)),
                  pl.BlockSpec((ts, D), lambda g, i: (nt - 1 - i, 0))],
        out_specs=[col, pl.BlockSpec((1, POOL_GROUP, POOL_GROUP), lambda g, i: (g, 0, 0)),
                   pl.BlockSpec((1, POOL_GROUP), lambda g, i: (0, g)),
                   pl.BlockSpec((POOL_GROUP, D), lambda g, i: (g, 0))],
        out_shape=[_sds((s, D), BF), _sds((4, POOL_GROUP, POOL_GROUP), F32), _sds((1, D), F32), _sds((D, D), F32)],
        scratch_shapes=[pltpu.VMEM((halo, POOL_GROUP), F32)],
        compiler_params=_cp(("parallel", "arbitrary")), name="pool_bwd")(da, pooled, w_pool, pool_scale, h1)


def _attn_bwd(qh, kh, vh, sinks, db, deps=()):
    s = qh.shape[1]
    assert s % (2 * BLK) == 0
    pairs = s // (2 * BLK)

    def body(sink_ref, q_ref, kp_ref, kc_ref, vp_ref, vc_ref, do_ref,
             dq_ref, dke_ref, dko_ref, dve_ref, dvo_ref, dsink_ref, ck, cv):
        i = pl.program_id(0)

        @pl.when(i == 0)
        def _():
            ck[...] = jnp.zeros_like(ck)
            cv[...] = jnp.zeros_like(cv)
            dsink_ref[...] = jnp.zeros_like(dsink_ref)

        @pl.when(i < pairs)
        def _():
            for khd in range(N_KV):
                c0 = khd * GQA * HEAD
                band = []
                for sub in range(2):
                    q = q_ref[khd * GQA:(khd + 1) * GQA, sub * BLK:(sub + 1) * BLK].reshape(GQA * BLK, HEAD)
                    if sub == 0:
                        k = jnp.concatenate([kp_ref[khd], kc_ref[khd, 0:BLK]], axis=0)
                        v = jnp.concatenate([vp_ref[khd], vc_ref[khd, 0:BLK]], axis=0)
                    else:
                        k, v = kc_ref[khd], vc_ref[khd]
                    dov = do_ref[sub * BLK:(sub + 1) * BLK, :]
                    do = jnp.concatenate([dov[:, c0 + HEAD * g:c0 + HEAD * (g + 1)] for g in range(GQA)],
                                         axis=0).astype(BF)
                    probs, psink, lower = _attn_probs(q, k, 2 * i + sub, _sink_column(sink_ref, khd))
                    dp = _fold_band(_nt(do, v), lower)
                    delta = jnp.sum(probs * dp, axis=-1, keepdims=True)
                    ds = _unfold_band(probs * (dp - delta), lower).astype(BF)
                    dq_ref[khd * GQA:(khd + 1) * GQA, sub * BLK:(sub + 1) * BLK] = _mm(ds, k).reshape(
                        GQA, BLK, HEAD)
                    band.append((_tn(ds, q), _tn(_unfold_band(probs, lower).astype(BF), do)))
                    dsk = psink * delta
                    lane = lax.broadcasted_iota(jnp.int32, (1, 128), 1)
                    acc = jnp.zeros((1, 128), F32)
                    for g in range(GQA):
                        acc = acc - jnp.where(lane == khd * GQA + g,
                                              jnp.sum(dsk[g * BLK:(g + 1) * BLK], axis=0, keepdims=True), 0.0)
                    dsink_ref[...] += acc
                (dk0, dv0), (dk1, dv1) = band
                dko_ref[khd] = ck[khd] + dk0[0:BLK]
                dvo_ref[khd] = cv[khd] + dv0[0:BLK]
                dke_ref[khd] = dk0[BLK:2 * BLK] + dk1[0:BLK]
                dve_ref[khd] = dv0[BLK:2 * BLK] + dv1[0:BLK]
                ck[khd] = dk1[BLK:2 * BLK]
                cv[khd] = dv1[BLK:2 * BLK]

        @pl.when(i == pairs)
        def _():
            dko_ref[...] = ck[...]
            dvo_ref[...] = cv[...]

    last = pairs - 1
    at = lambda i: jnp.minimum(i, last)
    prev = pl.BlockSpec((N_KV, BLK, HEAD), lambda i: (0, jnp.maximum(2 * at(i) - 1, 0), 0))
    cur = pl.BlockSpec((N_KV, 2 * BLK, HEAD), lambda i: (0, at(i), 0))
    even = pl.BlockSpec((N_KV, BLK, HEAD), lambda i: (0, at(i), 0))
    odd = pl.BlockSpec((N_KV, BLK, HEAD), lambda i: (0, jnp.maximum(i - 1, 0), 0))
    halfkv = _sds((N_KV, s // 2, HEAD), F32)
    dq, dke, dko, dve, dvo, dsink = pl.pallas_call(
        _after(body, 7, deps), grid=(pairs + 1,),
        in_specs=[pl.BlockSpec(memory_space=pltpu.SMEM),
                  pl.BlockSpec((N_Q, 2 * BLK, HEAD), lambda i: (0, at(i), 0)),
                  prev, cur, prev, cur,
                  pl.BlockSpec((2 * BLK, Q_W), lambda i: (at(i), 0))] + _any_specs(deps),
        out_specs=[pl.BlockSpec((N_Q, 2 * BLK, HEAD), lambda i: (0, at(i), 0)), even, odd, even, odd,
                   _full((1, 128))],
        out_shape=[_sds((N_Q, s, HEAD), F32), halfkv, halfkv, halfkv, halfkv, _sds((1, 128), F32)],
        scratch_shapes=[pltpu.VMEM((N_KV, BLK, HEAD), F32), pltpu.VMEM((N_KV, BLK, HEAD), F32)],
        compiler_params=_cp(("arbitrary",)), name="attn_bwd")(sinks, qh, kh, kh, vh, vh, db, *deps)

    def interleave(ev, od):
        both = jnp.stack([ev.reshape(N_KV, pairs, BLK, HEAD), od.reshape(N_KV, pairs, BLK, HEAD)], axis=2)
        return both.reshape(N_KV, s, HEAD)

    return dq, interleave(dke, dko), interleave(dve, dvo), dsink


def _qk_prep_bwd(dqh, dkh, dvh, qkv, pos, wq, wk, invf, bd, tm, deps=()):
    s = qkv.shape[0]

    def fold_heads(row):
        out = row[:, 0:HEAD]
        for h in range(1, row.shape[1] // HEAD):
            out = out + row[:, HEAD * h:HEAD * (h + 1)]
        return out

    def body(dq_ref, dk_ref, dv_ref, qkv_ref, pos_ref, wq_ref, wk_ref, invf_ref, bd_ref,
             dz_ref, dwq_ref, dwk_ref):
        @pl.when(pl.program_id(0) == 0)
        def _():
            dwq_ref[...] = jnp.zeros_like(dwq_ref)
            dwk_ref[...] = jnp.zeros_like(dwk_ref)

        cos, sa, sb = _rope_tables(pos_ref, invf_ref)

        def norm_rope_bwd(dy, xin, w, bdm):
            dn = _rope_t(dy, cos, sa, sb)
            r = lax.rsqrt(_head_sum(xin * xin, bdm) * (1.0 / HEAD) + EPS)
            nh = xin * r
            gw = dn * w
            dx = r * (gw - nh * (_head_sum(gw * nh, bdm) * (1.0 / HEAD)))
            return dx, fold_heads(jnp.sum(dn * nh, axis=0, keepdims=True))

        dq = jnp.concatenate([dq_ref[h] for h in range(N_Q)], axis=1) * (HEAD ** -0.5)
        dk = jnp.concatenate([dk_ref[h] for h in range(N_KV)], axis=1)
        dxq, dwq = norm_rope_bwd(dq, qkv_ref[:, 0:Q_W], wq_ref[...], bd_ref[...])
        dxk, dwk = norm_rope_bwd(dk, qkv_ref[:, Q_W:Q_W + KV_W], wk_ref[...], bd_ref[0:KV_W, 0:KV_W])
        dz_ref[:, 0:Q_W] = dxq.astype(BF)
        dz_ref[:, Q_W:Q_W + KV_W] = dxk.astype(BF)
        dz_ref[:, Q_W + KV_W:QKV_W] = jnp.concatenate([dv_ref[h] for h in range(N_KV)], axis=1).astype(BF)
        dwq_ref[...] += dwq
        dwk_ref[...] += dwk

    heads = lambda n: pl.BlockSpec((n, tm, HEAD), lambda i: (0, i, 0))
    return pl.pallas_call(
        _after(body, 9, deps), grid=(s // tm,),
        in_specs=[heads(N_Q), heads(N_KV), heads(N_KV), pl.BlockSpec((tm, QKV_W), lambda i: (i, 0)),
                  pl.BlockSpec((tm, 1), lambda i: (i, 0)), _full((1, Q_W)), _full((1, KV_W)),
                  _full((1, 2 * HEAD)), _full((Q_W, Q_W))] + _any_specs(deps),
        out_specs=[pl.BlockSpec((tm, QKV_W), lambda i: (i, 0)), _full((1, HEAD)), _full((1, HEAD))],
        out_shape=[_sds((s, QKV_W), BF), _sds((1, HEAD), F32), _sds((1, HEAD), F32)],
        compiler_params=_cp(("arbitrary",)), name="qk_prep_bwd")(
            dqh, dkh, dvh, qkv, pos, wq, wk, invf, bd, *deps)


def _in_proj_bwd(du, dzq, dzg, w_in, x, g1, dy1, tm, deps=()):
    s = x.shape[0]

    def body(du_ref, dzq_ref, dzg_ref, w_ref, x_ref, g_ref, dy_ref, gx_ref, dgn_ref):
        @pl.when(pl.program_id(0) == 0)
        def _():
            dgn_ref[...] = jnp.zeros_like(dgn_ref)

        dh = _mm(du_ref[...], w_ref[0:D, :])
        dh = dh + _mm(dzq_ref[...], w_ref[D:D + QKV_W, :])
        dh = dh + _mm(dzg_ref[...], w_ref[D + QKV_W:IN_W, :])
        dx, dgn = _rms_bwd(dh, x_ref[...], g_ref[...])
        gx_ref[...] = dy_ref[...] + dx
        dgn_ref[...] += jnp.sum(dgn, axis=0, keepdims=True)

    row = lambda w: pl.BlockSpec((tm, w), lambda i: (i, 0))
    return pl.pallas_call(
        _after(body, 7, deps), grid=(s // tm,),
        in_specs=[row(D), row(QKV_W), row(2 * D), _full((IN_W, D)), row(D), _full((1, D)), row(D)] + _any_specs(deps),
        out_specs=[row(D), _full((1, D))],
        out_shape=[_sds((s, D), F32), _sds((1, D), F32)],
        compiler_params=_cp(("arbitrary",)), name="in_proj_bwd")(du, dzq, dzg, w_in, x, g1, dy1, *deps)


def _adamw_step(w, g, m, v):
    mn = B1 * m + (1.0 - B1) * g
    vn = B2 * v + (1.0 - B2) * (g * g)
    m_hat = mn / (1.0 - B1 ** STEP)
    v_hat = vn / (1.0 - B2 ** STEP)
    return -LR * (m_hat / (jnp.sqrt(v_hat) + ADAM_EPS) + WD * w), mn, vn


SMALL_ROWS = 16
SMALL_COLS = D_FF
SMALL_AT = {"b_gate": (1, 2 * D), "pool_scale": (2, D), "q_norm": (3, HEAD),
            "k_norm": (4, HEAD), "sinks": (5, N_Q), "ffn_norm": (6, D)}
SMALL_LOSS_ROW = 0
SMALL_CONV_B_ROW = 7
SMALL_CONV_W_ROW = 9


def _pack_small(loss_acc, d_bgate, d_pscale, d_qn, d_kn, dsink, d_ffn_norm, dcb_g, dcb_v, dcw_g, dcw_v, dev):
    def body(k_ref, ls_ref, bg_ref, ps_ref, qn_ref, kn_ref, sk_ref, fn_ref, cbg_ref, cbv_ref, cwg_ref, cwv_ref,
             o_ref):
        o_ref[...] = jnp.zeros_like(o_ref)
        o_ref[0, SMALL_LOSS_ROW:SMALL_LOSS_ROW + 1, 0:128] = jnp.sum(ls_ref[...], axis=0, keepdims=True)
        for nm, ref in (("b_gate", bg_ref), ("pool_scale", ps_ref), ("q_norm", qn_ref),
                        ("k_norm", kn_ref), ("ffn_norm", fn_ref)):
            row, n = SMALL_AT[nm]
            o_ref[0, row:row + 1, 0:n] = ref[...]
        row, _ = SMALL_AT["sinks"]
        o_ref[0, row:row + 1, 0:128] = sk_ref[...]
        o_ref[0, SMALL_CONV_B_ROW:SMALL_CONV_B_ROW + 1, :] = cbg_ref[...]
        o_ref[0, SMALL_CONV_B_ROW + 1:SMALL_CONV_B_ROW + 2, :] = cbv_ref[...]
        for k in range(3):
            row = SMALL_CONV_W_ROW + 2 * k
            for half in range(2):
                o_ref[0, row:row + 1, half * UP_SHARD:(half + 1) * UP_SHARD] = cwg_ref[half, k:k + 1, :]
                o_ref[0, row + 1:row + 2, half * UP_SHARD:(half + 1) * UP_SHARD] = cwv_ref[half, k:k + 1, :]

    args = [loss_acc, d_bgate, d_pscale, d_qn, d_kn, dsink, d_ffn_norm, dcb_g, dcb_v, dcw_g, dcw_v]
    grid_spec = pltpu.PrefetchScalarGridSpec(
        num_scalar_prefetch=1, grid=(1,),
        in_specs=[pl.BlockSpec(a.shape, functools.partial(lambda nd, i, k: (0,) * nd, a.ndim)) for a in args],
        out_specs=pl.BlockSpec((1, SMALL_ROWS, SMALL_COLS), lambda i, k: (k[0], 0, 0)))
    return pl.pallas_call(body, grid_spec=grid_spec, out_shape=_sds((N_DEV, SMALL_ROWS, SMALL_COLS), F32),
                          name="pack_small")(dev, *args)


def _small_update(stack, attn_stack, params):
    names = list(params)

    def body(*refs):
        s_ref, a_ref = refs[0], refs[1]
        ins = refs[2:2 + 3 * len(names)]
        outs = refs[2 + 3 * len(names):]
        tot, tot_a = s_ref[0], a_ref[0]
        for d in range(1, N_DEV):
            tot = tot + s_ref[d]
            tot_a = tot_a + a_ref[d]
        for i, nm in enumerate(names):
            if nm == "attn_norm":
                g = tot_a
            elif nm == "conv_b":
                g = jnp.concatenate([tot[SMALL_CONV_B_ROW:SMALL_CONV_B_ROW + 1, :],
                                     tot[SMALL_CONV_B_ROW + 1:SMALL_CONV_B_ROW + 2, :]], axis=1)
            else:
                row, n = SMALL_AT[nm]
                g = tot[row:row + 1, 0:n]
            delta, mn, vn = _adamw_step(ins[3 * i][...], g, ins[3 * i + 1][...], ins[3 * i + 2][...])
            outs[4 * i][...] = g
            outs[4 * i + 1][...] = delta
            outs[4 * i + 2][...] = mn
            outs[4 * i + 3][...] = vn
        for k in range(3):
            row = SMALL_CONV_W_ROW + 2 * k
            outs[-2][k:k + 1, 0:D_FF] = tot[row:row + 1, :]
            outs[-2][k:k + 1, D_FF:2 * D_FF] = tot[row + 1:row + 2, :]
        outs[-1][...] = jnp.sum(tot[SMALL_LOSS_ROW:SMALL_LOSS_ROW + 1, 0:128], axis=1, keepdims=True) * (0.5 / D)

    flat = [a for nm in names for a in params[nm]]
    out_shape = ([_sds(params[nm][0].shape, F32) for nm in names for _ in range(4)]
                 + [_sds((3, 2 * D_FF), F32), _sds((1, 1), F32)])
    res = pl.pallas_call(body, out_shape=out_shape, name="small_update")(stack, attn_stack, *flat)
    return {nm: list(res[4 * i:4 * i + 4]) for i, nm in enumerate(names)}, res[-2], res[-1]


def _adamw(w, g, m, v, tr, name, deps=()):
    r, c = w.shape

    def body(w_ref, g_ref, m_ref, v_ref, go_ref, d_ref, mo_ref, vo_ref):
        gv = g_ref[...]
        go_ref[...] = gv
        d_ref[...], mo_ref[...], vo_ref[...] = _adamw_step(w_ref[...], gv, m_ref[...], v_ref[...])

    blk = pl.BlockSpec((tr, c), lambda i: (i, 0))
    return pl.pallas_call(
        _after(body, 4, deps), grid=(r // tr,), in_specs=[blk] * 4 + _any_specs(deps), out_specs=[blk] * 4,
        out_shape=[_sds((r, c), F32)] * 4, compiler_params=_cp(("parallel",)), name=name)(w, g, m, v, *deps)


def _place():
    x, y, c = lax.axis_index("x"), lax.axis_index("y"), lax.axis_index("c")
    chips = [(1 - x, y), (x, 1 - y), (1 - x, 1 - y)]
    return x, y, c, chips


def _rows(ref, lead, h, rh):
    sl = pl.ds(pl.multiple_of(h * rh, 16), rh)
    return ref.at[sl, :] if lead is None else ref.at[lead, sl, :]


def _all_gather_weights(halved, whole, placed):
    nh, nw, npl = len(halved), len(whole), len(placed)
    na = nh + nw
    nall = na + npl
    arrays = list(halved) + list(whole) + list(placed)
    out_dtypes = [BF] * nh + [a.dtype for a in whole] + [BF] * npl
    cast_rows = 128

    def body(*refs):
        ins, outs = refs[:nall], refs[nall:2 * nall]
        raw, stage = refs[2 * nall:3 * nall], refs[3 * nall:3 * nall + nh + npl]
        ici_send, ici_recv, fwd_send, fwd_recv, in_sem, loc_sem = refs[3 * nall + nh + npl:]
        x, y, c, chips = _place()
        me = 2 * x + y
        sibling = (x, y, 1 - c)
        loads = [pltpu.make_async_copy(ins[a], raw[a], in_sem.at[a]) for a in range(nall)]
        for cp in loads:
            cp.start()

        def cast(a, dst):
            r = arrays[a].shape[0]
            for r0 in range(0, r, cast_rows):
                r1 = min(r0 + cast_rows, r)
                dst[r0:r1, :] = raw[a][r0:r1, :].astype(BF)

        def ici(a, j, src_chip, src=None):
            if a < nh:
                rh = arrays[a].shape[0] // 2
                dst = _rows(outs[a], src_chip, c, rh)
                src = dst if src is None else _rows(src, None, c, rh)
            else:
                dst = outs[a].at[src_chip]
                src = dst if src is None else src
            return pltpu.make_async_remote_copy(
                src_ref=src, dst_ref=dst, send_sem=ici_send.at[3 * a + j], recv_sem=ici_recv.at[3 * a + j],
                device_id=(*chips[j], c), device_id_type=MESH)

        def fwd(a, j, half):
            rh = arrays[a].shape[0] // 2
            kj = 2 * chips[j][0] + chips[j][1]
            blk = _rows(outs[a], kj, half, rh)
            return pltpu.make_async_remote_copy(
                src_ref=blk, dst_ref=blk, send_sem=fwd_send.at[3 * a + j], recv_sem=fwd_recv.at[3 * a + j],
                device_id=sibling, device_id_type=MESH)

        local, sends = [], []
        for a in range(na):
            loads[a].wait()
            if a < nh:
                cast(a, stage[a])
                own = stage[a]
            else:
                own = raw[a]
            cp = pltpu.make_async_copy(own, outs[a].at[me], loc_sem.at[a])
            cp.start()
            local.append(cp)
            for j in range(3):
                cp = ici(a, j, me, src=own)
                cp.start()
                sends.append(cp)
        for i in range(npl):
            loads[na + i].wait()
            cast(na + i, stage[nh + i])
            cp = pltpu.make_async_copy(stage[nh + i], outs[na + i].at[me], loc_sem.at[na + i])
            cp.start()
            local.append(cp)
        passed = []
        for a in range(na):
            for j in range(3):
                kj = 2 * chips[j][0] + chips[j][1]
                ici(a, j, kj).wait_recv()
                if a < nh:
                    cp = fwd(a, j, c)
                    cp.start()
                    passed.append(cp)
        for a in range(nh):
            for j in range(3):
                fwd(a, j, 1 - c).wait_recv()
        for cp in sends + passed:
            cp.wait_send()
        for cp in local:
            cp.wait()

    any_spec = pl.BlockSpec(memory_space=pl.ANY)
    return pl.pallas_call(
        body, in_specs=[any_spec] * nall, out_specs=[any_spec] * nall,
        out_shape=[_sds((N_CHIPS,) + a.shape, dt) for a, dt in zip(arrays, out_dtypes)],
        scratch_shapes=[pltpu.VMEM(a.shape, a.dtype) for a in arrays]
        + [pltpu.VMEM(a.shape, BF) for a in list(halved) + list(placed)]
        + [pltpu.SemaphoreType.DMA((3 * na,)), pltpu.SemaphoreType.DMA((3 * na,)),
           pltpu.SemaphoreType.DMA((3 * nh,)), pltpu.SemaphoreType.DMA((3 * nh,)),
           pltpu.SemaphoreType.DMA((nall,)), pltpu.SemaphoreType.DMA((nall,))],
        compiler_params=pltpu.CompilerParams(vmem_limit_bytes=VMEM_LIMIT_MB << 20),
        name="all_gather_weights")(*arrays)


def _pair_sum(g, recv, c, tr, name):
    _, r, cols = g.shape
    rh = r // 2
    nr = rh // tr

    def body(c_ref, g_ref, r_ref, o_ref):
        o_ref[...] = (g_ref[...] + r_ref[...]).astype(BF)

    grid_spec = pltpu.PrefetchScalarGridSpec(
        num_scalar_prefetch=1, grid=(N_CHIPS, nr),
        in_specs=[pl.BlockSpec((1, tr, cols), lambda k, i, c_ref: (k, c_ref[0] * nr + i, 0)),
                  pl.BlockSpec((1, tr, cols), lambda k, i, c_ref: (k, i, 0))],
        out_specs=pl.BlockSpec((1, tr, cols), lambda k, i, c_ref: (k, i, 0)))
    return pl.pallas_call(
        body, grid_spec=grid_spec, out_shape=_sds((N_CHIPS, rh, cols), BF),
        compiler_params=_cp(("parallel", "parallel")), name=name)(c, g, recv)


def _chip_sum(g, sib, recv, place, tr, name):
    _, r, cols = g.shape
    rh = r // 2
    nr = rh // tr

    def body(p_ref, g_ref, s_ref, r0_ref, r1_ref, r2_ref, o_ref):
        own = g_ref[0] + s_ref[0]
        o_ref[...] = ((own + r0_ref[0].astype(F32)) + r1_ref[0].astype(F32)) + r2_ref[0].astype(F32)

    rspec = lambda j: pl.BlockSpec((1, tr, cols), lambda i, p: (j, i, 0))
    grid_spec = pltpu.PrefetchScalarGridSpec(
        num_scalar_prefetch=1, grid=(nr,),
        in_specs=[pl.BlockSpec((1, tr, cols), lambda i, p: (p[0], p[1] * nr + i, 0)),
                  pl.BlockSpec((1, tr, cols), lambda i, p: (p[0], i, 0)), rspec(0), rspec(1), rspec(2)],
        out_specs=pl.BlockSpec((tr, cols), lambda i, p: (p[1] * nr + i, 0)))
    return pl.pallas_call(
        body, grid_spec=grid_spec, out_shape=_sds((r, cols), F32),
        compiler_params=_cp(("parallel",)), name=name)(place, g, sib, recv, recv, recv)


_HBM = pl.BlockSpec(memory_space=pltpu.HBM)
_SEM = pl.BlockSpec(memory_space=pltpu.SEMAPHORE)
_EFFECT = pltpu.SideEffectType.DATAFLOW_SIDE_EFFECTING


def _remote(src, dst, ssem, rsem, k, device):
    return pltpu.make_async_remote_copy(src_ref=src, dst_ref=dst, send_sem=ssem.at[k], recv_sem=rsem.at[k],
                                        device_id=device, device_id_type=MESH)


def _split_start(name, bufs, plan, n):
    nb = len(bufs)

    def body(*refs):
        sends, _ = plan(refs[:nb], refs[nb], refs[nb + 1])
        for cp in sends:
            cp.start()
        refs[-1][...] = jnp.zeros_like(refs[-1])

    res = pl.pallas_call(
        body, name=name,
        out_shape=(pltpu.SemaphoreType.DMA((n,)), pltpu.SemaphoreType.DMA((n,)))
        + tuple(pltpu.HBM(b.shape, b.dtype) for b in bufs) + (_sds((8, 128), F32),),
        in_specs=[_HBM] * nb,
        out_specs=(_SEM, _SEM) + (_HBM,) * nb + (pl.BlockSpec(memory_space=pltpu.VMEM),),
        input_output_aliases={i: i + 2 for i in range(nb)},
        compiler_params=pltpu.CompilerParams(has_side_effects=_EFFECT),
    )(*[pltpu.with_memory_space_constraint(b, pltpu.HBM) for b in bufs])
    return res[0], res[1], list(res[2:2 + nb]), res[2 + nb]


def _split_wait(name, send_sem, recv_sem, bufs, plan, after):
    nb = len(bufs)

    def body(*refs):
        sends, arrivals = plan(refs[:nb], refs[nb], refs[nb + 1])
        for cp in sends:
            cp.wait_send()
        for cp in arrivals:
            cp.wait_recv()

    res = pl.pallas_call(
        body, name=name, out_shape=tuple(pltpu.HBM(b.shape, b.dtype) for b in bufs),
        in_specs=[_HBM] * nb + [_SEM, _SEM, pl.BlockSpec(memory_space=pl.ANY)],
        out_specs=(_HBM,) * nb, input_output_aliases={i: i for i in range(nb)},
        compiler_params=pltpu.CompilerParams(has_side_effects=_EFFECT),
    )(*bufs, send_sem, recv_sem, after)
    return list(res)


def _plan_sibling_halves(shapes):
    na = len(shapes)

    def plan(refs, ssem, rsem):
        x, y, c, _ = _place()
        cps = []
        for a in range(na):
            rh = shapes[a][1] // 2
            src = refs[a].at[:, pl.ds(pl.multiple_of((1 - c) * rh, 8), rh), :]
            cps.append(_remote(src, refs[na + a], ssem, rsem, a, (x, y, 1 - c)))
        return cps, cps

    return plan


def _to_all(ref, ssem, rsem, base):
    x, y, c, _ = _place()
    mine = ref.at[4 * x + 2 * y + c]
    return [_remote(mine, mine, ssem, rsem, base + r - 1, (x ^ (r >> 2), y ^ ((r >> 1) & 1), c ^ (r & 1)))
            for r in range(1, N_DEV)]


def _plan_chip_exchange(na, with_small):
    def plan(refs, ssem, rsem):
        _, _, c, chips = _place()
        cps = []
        for a in range(na):
            for j in range(3):
                kj = 2 * chips[j][0] + chips[j][1]
                cps.append(_remote(refs[a].at[kj], refs[na + a].at[j], ssem, rsem, 3 * a + j, (*chips[j], c)))
        if with_small:
            cps += _to_all(refs[2 * na], ssem, rsem, 3 * na)
        return cps, cps

    return plan


def _plan_sibling_swap(shapes, with_small):
    def plan(refs, ssem, rsem):
        x, y, c, _ = _place()
        sends, arrivals = [], []
        for a, shp in enumerate(shapes):
            rh = shp[0] // 2
            mine, other = _rows(refs[a], None, c, rh), _rows(refs[a], None, 1 - c, rh)
            sends.append(_remote(mine, mine, ssem, rsem, a, (x, y, 1 - c)))
            arrivals.append(_remote(mine, other, ssem, rsem, a, (x, y, 1 - c)))
        if with_small:
            cps = _to_all(refs[len(shapes)], ssem, rsem, len(shapes))
            sends += cps
            arrivals += cps
        return sends, arrivals

    return plan


def _plan_gather_chips(shapes):
    def plan(refs, ssem, rsem):
        x, y, c, chips = _place()
        me = 2 * x + y
        sends, arrivals = [], []
        for a, shp in enumerate(shapes):
            rh = shp[1] // 2
            mine = _rows(refs[a], me, c, rh)
            for j in range(3):
                land = _rows(refs[a], 2 * chips[j][0] + chips[j][1], c, rh)
                sends.append(_remote(mine, mine, ssem, rsem, 3 * a + j, (*chips[j], c)))
                arrivals.append(_remote(land, land, ssem, rsem, 3 * a + j, (*chips[j], c)))
        return sends, arrivals

    return plan


def _plan_gather_sibling(shapes):
    def plan(refs, ssem, rsem):
        x, y, c, chips = _place()
        sends, arrivals = [], []
        for a, shp in enumerate(shapes):
            rh = shp[1] // 2
            for j in range(3):
                kj = 2 * chips[j][0] + chips[j][1]
                got, land = _rows(refs[a], kj, c, rh), _rows(refs[a], kj, 1 - c, rh)
                sends.append(_remote(got, got, ssem, rsem, 3 * a + j, (x, y, 1 - c)))
                arrivals.append(_remote(got, land, ssem, rsem, 3 * a + j, (x, y, 1 - c)))
        return sends, arrivals

    return plan


def _into_slice(w, k, n, tr, dtype, name, deps=()):
    r, cols = w.shape

    def body(k_ref, w_ref, o_ref):
        o_ref[0] = w_ref[...].astype(dtype)

    grid_spec = pltpu.PrefetchScalarGridSpec(
        num_scalar_prefetch=1, grid=(r // tr,),
        in_specs=[pl.BlockSpec((tr, cols), lambda i, k: (i, 0))] + _any_specs(deps),
        out_specs=pl.BlockSpec((1, tr, cols), lambda i, k: (k[0], i, 0)))
    return pl.pallas_call(_after(body, 2, deps), grid_spec=grid_spec, out_shape=_sds((n, r, cols), dtype),
                          compiler_params=_cp(("parallel",)), name=name)(k, w, *deps)


class _LateWeights:
    def __init__(self, bufs):
        self.n = 3 * len(bufs)
        self.chips, self.sibling = _plan_gather_chips([b.shape for b in bufs]), _plan_gather_sibling([b.shape for b in bufs])
        self.ssem, self.rsem, self.bufs, token = _split_start("gather_chips_start", bufs, self.chips, self.n)
        self.first = (token,)

    def middle(self, after):
        bufs = _split_wait("gather_chips_wait", self.ssem, self.rsem, self.bufs, self.chips, after)
        self.ssem, self.rsem, self.bufs, token = _split_start("gather_sibling_start", bufs, self.sibling, self.n)
        return (token,)

    def last(self, after):
        return _split_wait("gather_sibling_wait", self.ssem, self.rsem, self.bufs, self.sibling, after)


class _GradReduce:
    def __init__(self, tag, place, names, tiles):
        self.tag, self.place, self.names, self.tiles = tag, place, names, tiles
        self.small_all = None

    def first(self, grads):
        self.na = len(grads)
        self.p1 = _plan_sibling_halves([g.shape for g in grads])
        lands = [lax.empty((N_CHIPS, g.shape[1] // 2, g.shape[2]), F32) for g in grads]
        self.ssem, self.rsem, self.bufs, token = _split_start(
            self.tag + "_halves_start", list(grads) + lands, self.p1, self.na)
        return (token,)

    def second(self, after, small=None):
        bufs = _split_wait(self.tag + "_halves_wait", self.ssem, self.rsem, self.bufs, self.p1, after)
        self.grads, self.sib = bufs[:self.na], bufs[self.na:]
        halves = [_pair_sum(g, r, self.place[1:2], t, "pair_sum_" + nm)
                  for g, r, t, nm in zip(self.grads, self.sib, self.tiles, self.names)]
        lands = [lax.empty((3,) + h.shape[1:], h.dtype) for h in halves]
        extra = [] if small is None else [small]
        self.p2 = _plan_chip_exchange(self.na, small is not None)
        self.ssem, self.rsem, self.bufs, token = _split_start(
            self.tag + "_chips_start", halves + lands + extra, self.p2, 3 * self.na + (N_DEV - 1) * len(extra))
        return (token,)

    def third(self, after, small=None):
        bufs = _split_wait(self.tag + "_chips_wait", self.ssem, self.rsem, self.bufs, self.p2, after)
        if len(bufs) > 2 * self.na:
            self.small_chips = bufs[2 * self.na]
        mine = [_chip_sum(g, sb, r, self.place, t, "chip_sum_" + nm)
                for g, sb, r, t, nm in zip(self.grads, self.sib, bufs[self.na:2 * self.na], self.tiles, self.names)]
        extra = [] if small is None else [small]
        self.p3 = _plan_sibling_swap([m.shape for m in mine], small is not None)
        self.ssem, self.rsem, self.bufs, token = _split_start(
            self.tag + "_swap_start", mine + extra, self.p3, self.na + (N_DEV - 1) * len(extra))
        return (token,)

    def last(self, after):
        bufs = _split_wait(self.tag + "_swap_wait", self.ssem, self.rsem, self.bufs, self.p3, after)
        if len(bufs) > self.na:
            self.small_swap = bufs[self.na]
        return bufs[:self.na]


class _WeightsAtHand:
    def __init__(self, wup, wout, wdown):
        self.first, self.weights = (), [wup, wout, wdown]

    def middle(self, after):
        return ()

    def last(self, after):
        return self.weights


class _GradsKept:
    def first(self, grads):
        self.grads = list(grads)
        return ()

    def second(self, after, small=None):
        return ()

    def third(self, after, small=None):
        return ()

    def last(self, after):
        return self.grads


def _forward_backward(xs, pos, tgt, win, wpool, cw, attn_norm, b_gate, pool_scale, q_norm, k_norm, sinks,
                      ffn_norm, conv_b, late, early, rest, dev, between=None):
    s = xs.shape[0]
    tm = min(512, s)
    tk = min(2048, s)
    inv_freq = np.float32(ROPE_THETA) ** (-np.arange(0, ROPE_DIM, 2, dtype=np.float32) / np.float32(ROPE_DIM))
    lane = np.arange(2 * HEAD) % HEAD
    invf = jnp.asarray(np.where(lane < ROPE_DIM, inv_freq[lane % (ROPE_DIM // 2)], 0.0).astype(np.float32)
                       .reshape(1, 2 * HEAD))
    wq = jnp.tile(q_norm, (1, N_Q))
    wk = jnp.tile(k_norm, (1, N_KV))
    head_of = jnp.arange(Q_W) // HEAD
    bd = (head_of[:, None] == head_of[None, :]).astype(BF)
    sink = sinks[0]

    h1, u, qkv, gates = _attn_in_proj(xs, attn_norm, win, b_gate, tm, deps=late.first)
    qh, kh, vh = _qk_prep(qkv, pos, wq, wk, invf, bd, tm)
    apool, pooled = _pool_fwd(u, wpool, pool_scale, min(2048, s), deps=(qh,))
    battn = _attn_fwd(qh, kh, vh, sink, deps=late.middle(apool))
    wup, wout, wdown = late.last(battn)
    wout = wout.reshape(D, D)
    wdown = wdown.reshape(D_FF, D)
    mix, y1, h2 = _mix_out_proj(apool, battn, gates, xs, wout, ffn_norm, tm)
    pre_g, pre_v, up_g, up_v, act = _ffn_up(h2, wup, cw, conv_b, tm)
    dy2, dy2b, loss_acc = _ffn_down_loss(act, wdown, y1, tgt, tm)

    d_wdown = _grad_matmul(act, dy2b, 512, tk, "grad_w_down")
    dp_g, dp_v, dcw_g, dcw_v, dcb_g, dcb_v = _ffn_act_bwd(dy2b, wdown, up_g, up_v, pre_g, pre_v, cw, tm)
    d_wup = _grad_matmul(h2, dp_g, UP_SHARD, tk, "grad_w_up_gate", lead=N_CHIPS)
    d_wup = _grad_matmul(h2, dp_v, UP_SHARD, tk, "grad_w_up_value", lead=N_CHIPS, prev=d_wup, lead_off=2)
    token = early.first([d_wdown.reshape(N_CHIPS, D_FF // N_CHIPS, D), d_wup])
    dy1, d_ffn_norm = _ffn_up_bwd(dp_g, dp_v, wup, y1, dy2, ffn_norm, tm, deps=token)
    token = early.second(dy1)
    da, db, dzg, d_bgate, d_wout = _out_proj_bwd(dy1, wout, apool, battn, gates, mix, tm, deps=token)
    du, d_wpool, d_pscale, d_win_pool = _pool_bwd(da, pooled, wpool, pool_scale, h1, min(2048, s))
    dqh, dkh, dvh, dsink = _attn_bwd(qh, kh, vh, sink, db)
    token = early.third(dqh)
    dzq, d_qn, d_kn = _qk_prep_bwd(dqh, dkh, dvh, qkv, pos, wq, wk, invf, bd, tm, deps=token)
    d_win_t = jnp.concatenate([
        d_win_pool,
        _grad_matmul(dzq, h1, D, tk, "grad_w_in_qkv"),
        _grad_matmul(dzg, h1, D, tk, "grad_w_in_gates")], axis=0)
    token = rest.first([
        d_win_t.reshape(N_CHIPS, IN_W // N_CHIPS, D),
        d_wout.reshape(N_CHIPS, D // N_CHIPS, D),
        d_wpool.reshape(4, N_CHIPS, 64, POOL_GROUP).transpose(1, 0, 2, 3).reshape(N_CHIPS, 4 * 64, POOL_GROUP)])
    after = between(token) if between is not None else (token[0] if token else None)
    small = _pack_small(loss_acc, d_bgate, d_pscale, d_qn, d_kn, dsink, d_ffn_norm, dcb_g, dcb_v, dcw_g, dcw_v, dev)
    token = rest.second(after, small=small)
    grad_x, d_attn_norm = _in_proj_bwd(du, dzq, dzg, win, xs, attn_norm, dy1, tm, deps=token)
    return grad_x, d_attn_norm, small


def kernel(x, positions, attn_norm, w_in, b_gate, w_pool, pool_scale, q_norm, k_norm, sinks, w_out, ffn_norm, w_up, conv_w, conv_b, w_down, loss_target, m_attn_norm, m_w_in, m_b_gate, m_w_pool, m_pool_scale, m_q_norm, m_k_norm, m_sinks, m_w_out, m_ffn_norm, m_w_up, m_conv_w, m_conv_b, m_w_down, v_attn_norm, v_w_in, v_b_gate, v_w_pool, v_pool_scale, v_q_norm, v_k_norm, v_sinks, v_w_out, v_ffn_norm, v_w_up, v_conv_w, v_conv_b, v_w_down):
    s = x.shape[1]
    xs = x[0]
    tgt = loss_target[0]
    pos = positions[0].reshape(s, 1)
    cx, cy, cc = lax.axis_index("x"), lax.axis_index("y"), lax.axis_index("c")
    chip = 2 * cx + cy

    dev_arr = (2 * chip + cc).reshape(1).astype(jnp.int32)
    place = jnp.stack([chip, cc]).astype(jnp.int32)

    g_in, g_pool, g_cw, *own_late = _all_gather_weights(
        [jnp.swapaxes(w_in[0], 0, 1), w_pool[0].reshape(4 * 64, POOL_GROUP)], [conv_w[0]],
        [w_up[0], w_out[0], w_down[0]])
    win = g_in.reshape(IN_W, D)
    wpool = g_pool.reshape(N_CHIPS, 4, 64, POOL_GROUP).transpose(1, 0, 2, 3).reshape(4, POOL_GROUP, POOL_GROUP)
    late = _LateWeights(own_late)
    early = _GradReduce("early", place, ["w_down", "w_up"], [352, 512])
    rest = _GradReduce("rest", place, ["w_in", "w_out", "w_pool"], [544, 128, 128])

    def two_d(a):
        return a.reshape(-1, a.shape[-1])

    def update(nm, w, g, m, v, tr, deps=()):
        res = _adamw(two_d(w), g, two_d(m), two_d(v), tr, "adamw_" + nm, deps=deps)
        return [r.reshape(w.shape) for r in res]

    big_out = {}

    def update_early(token):
        g_wdown, g_wup = early.last(token[0])
        big_out["w_up"] = update("w_up", w_up, g_wup, m_w_up, v_w_up, 256, deps=token)
        big_out["w_down"] = update("w_down", w_down, g_wdown, m_w_down, v_w_down, 352, deps=(big_out["w_up"][1],))
        return big_out["w_down"][1]

    grad_x, d_attn_norm, _ = _forward_backward(
        xs, pos, tgt, win, wpool, g_cw, attn_norm, b_gate, pool_scale, q_norm, k_norm, sinks, ffn_norm, conv_b,
        late, early, rest, dev_arr, between=update_early)

    attn_stack = _into_slice(d_attn_norm, dev_arr, N_DEV, 1, F32, "own_attn_norm")
    token = rest.third(grad_x, small=attn_stack)
    g_win_t, g_wout, g_wpool = rest.last(token[0])
    small_out, g_convw_all, loss = _small_update(rest.small_chips, rest.small_swap, {
        "attn_norm": (attn_norm, m_attn_norm, v_attn_norm), "b_gate": (b_gate, m_b_gate, v_b_gate),
        "pool_scale": (pool_scale, m_pool_scale, v_pool_scale), "q_norm": (q_norm, m_q_norm, v_q_norm),
        "k_norm": (k_norm, m_k_norm, v_k_norm), "sinks": (sinks, m_sinks, v_sinks),
        "ffn_norm": (ffn_norm, m_ffn_norm, v_ffn_norm), "conv_b": (conv_b, m_conv_b, v_conv_b)})
    g_convw = lax.dynamic_slice_in_dim(g_convw_all, chip * UP_SHARD, UP_SHARD, axis=1)
    small_out["conv_w"] = update("conv_w", conv_w, g_convw, m_conv_w, v_conv_w, 3)
    flip = lambda a: jnp.swapaxes(a[0], 0, 1)
    res = _adamw(flip(w_in), g_win_t, flip(m_w_in), flip(v_w_in), 544, "adamw_w_in")
    big_out["w_in"] = [jnp.swapaxes(r, 0, 1)[None] for r in res]
    big_out["w_out"] = update("w_out", w_out, g_wout, m_w_out, v_w_out, 256)
    big_out["w_pool"] = update("w_pool", w_pool, g_wpool, m_w_pool, v_w_pool, 256)

    order = ["attn_norm", "w_in", "b_gate", "w_pool", "pool_scale", "q_norm", "k_norm", "sinks", "w_out",
             "ffn_norm", "w_up", "conv_w", "conv_b", "w_down"]
    allout = {**big_out, **small_out}
    outs = [loss.reshape(()), grad_x[None]]
    for k in range(4):
        outs += [allout[nm][k] for nm in order]
    return tuple(outs)
```

```python
import functools

import jax
import jax.numpy as jnp
import numpy as np
from jax import lax
from jax.experimental import pallas as pl
from jax.experimental.pallas import tpu as pltpu

D = 1024
D_FF = 2816
HEAD = 64
N_Q = 16
N_KV = 2
GQA = 8
BLK = 128
ROPE_DIM = 16
ROPE_THETA = 500000.0
POOL_GROUP = 256
Q_W = 1024
KV_W = 128
QKV_W = Q_W + 2 * KV_W
IN_W = 4352
UP_SHARD = 1408
EPS = 1e-6
N_CHIPS = 4
N_DEV = 8

LR = 0.001
B1 = 0.9
B2 = 0.999
ADAM_EPS = 1e-08
WD = 0.01
STEP = 10

BF = jnp.bfloat16
F32 = jnp.float32
MESH = pl.DeviceIdType.MESH
VMEM_LIMIT_MB = 56


def _cp(sem, vmem_mb=VMEM_LIMIT_MB):
    return pltpu.CompilerParams(dimension_semantics=sem, vmem_limit_bytes=vmem_mb << 20)


def _full(shape):
    nd = len(shape)
    return pl.BlockSpec(shape, lambda *_: (0,) * nd)


def _sds(shape, dtype):
    return jax.ShapeDtypeStruct(shape, dtype)


def _after(body, n_in, deps):
    nd = len(deps)
    if nd == 0:
        return body

    def ordered(*refs):
        return body(*refs[:n_in], *refs[n_in + nd:])

    return ordered


def _any_specs(deps):
    return [pl.BlockSpec(memory_space=pl.ANY)] * len(deps)


def _nt(a, b):
    return lax.dot_general(a, b, (((1,), (1,)), ((), ())), preferred_element_type=F32)


def _tn(a, b):
    return lax.dot_general(a, b, (((0,), (0,)), ((), ())), preferred_element_type=F32)


def _mm(a, b):
    return jnp.dot(a, b, preferred_element_type=F32)


def _head_sum(v, bd):
    return _mm(v.astype(BF), bd)


def _rope_tables(pos_ref, invf_ref):
    ang = pos_ref[...].astype(F32) * invf_ref[...]
    cos = jnp.cos(ang)
    sin = jnp.sin(ang)
    lane = lax.broadcasted_iota(jnp.int32, (1, 2 * HEAD), 1) % HEAD
    sa = jnp.where(lane < ROPE_DIM // 2, -sin, 0.0)
    sb = jnp.where(lane < ROPE_DIM // 2, 0.0, jnp.where(lane < ROPE_DIM, sin, 0.0))
    return cos, sa, sb


def _tile_lanes(t, reps):
    return t if reps == 1 else jnp.tile(t, (1, reps))


def _rope(v, cos, sa, sb):
    w = v.shape[1]
    reps = w // (2 * HEAD)
    half = ROPE_DIM // 2
    return (v * _tile_lanes(cos, reps) + pltpu.roll(v, w - half, 1) * _tile_lanes(sa, reps)
            + pltpu.roll(v, half, 1) * _tile_lanes(sb, reps))


def _rope_t(dy, cos, sa, sb):
    w = dy.shape[1]
    reps = w // (2 * HEAD)
    half = ROPE_DIM // 2
    return (dy * _tile_lanes(cos, reps) + pltpu.roll(dy * _tile_lanes(sa, reps), half, 1)
            + pltpu.roll(dy * _tile_lanes(sb, reps), w - half, 1))


def _attn_in_proj(x, g1, w_in, b_gate, tm, deps=()):
    s = x.shape[0]

    def body(x_ref, g_ref, w_ref, b_ref, h_ref, u_ref, qkv_ref, gate_ref):
        xv = x_ref[...]
        r = lax.rsqrt(jnp.mean(xv * xv, axis=-1, keepdims=True) + EPS)
        h = (xv * r * g_ref[...]).astype(BF)
        h_ref[...] = h
        u_ref[...] = _nt(h, w_ref[0:D, :])
        qkv_ref[...] = _nt(h, w_ref[D:D + QKV_W, :])
        gate_ref[...] = jax.nn.sigmoid(_nt(h, w_ref[D + QKV_W:IN_W, :]) + b_ref[...]).astype(BF)

    row = lambda w: pl.BlockSpec((tm, w), lambda i: (i, 0))
    return pl.pallas_call(
        _after(body, 4, deps), grid=(s // tm,),
        in_specs=[row(D), _full((1, D)), _full((IN_W, D)), _full((1, 2 * D))] + _any_specs(deps),
        out_specs=[row(D), row(D), row(QKV_W), row(2 * D)],
        out_shape=[_sds((s, D), BF), _sds((s, D), F32), _sds((s, QKV_W), F32), _sds((s, 2 * D), BF)],
        compiler_params=_cp(("parallel",)), name="attn_in_proj")(x, g1, w_in, b_gate, *deps)


def _qk_prep(qkv, pos, wq, wk, invf, bd, tm):
    s = qkv.shape[0]

    def body(qkv_ref, pos_ref, wq_ref, wk_ref, invf_ref, bd_ref, qh_ref, kh_ref, vh_ref):
        cos, sa, sb = _rope_tables(pos_ref, invf_ref)
        q = qkv_ref[:, 0:Q_W]
        k = qkv_ref[:, Q_W:Q_W + KV_W]
        v = qkv_ref[:, Q_W + KV_W:QKV_W]
        rq = lax.rsqrt(_head_sum(q * q, bd_ref[...]) * (1.0 / HEAD) + EPS)
        qr = _rope(q * rq * wq_ref[...], cos, sa, sb) * (HEAD ** -0.5)
        rk = lax.rsqrt(_head_sum(k * k, bd_ref[0:KV_W, 0:KV_W]) * (1.0 / HEAD) + EPS)
        kr = _rope(k * rk * wk_ref[...], cos, sa, sb)
        for h in range(N_Q):
            qh_ref[h] = qr[:, HEAD * h:HEAD * (h + 1)].astype(BF)
        for h in range(N_KV):
            kh_ref[h] = kr[:, HEAD * h:HEAD * (h + 1)].astype(BF)
            vh_ref[h] = v[:, HEAD * h:HEAD * (h + 1)].astype(BF)

    heads = lambda n: pl.BlockSpec((n, tm, HEAD), lambda i: (0, i, 0))
    return pl.pallas_call(
        body, grid=(s // tm,),
        in_specs=[pl.BlockSpec((tm, QKV_W), lambda i: (i, 0)), pl.BlockSpec((tm, 1), lambda i: (i, 0)),
                  _full((1, Q_W)), _full((1, KV_W)), _full((1, 2 * HEAD)), _full((Q_W, Q_W))],
        out_specs=[heads(N_Q), heads(N_KV), heads(N_KV)],
        out_shape=[_sds((N_Q, s, HEAD), BF), _sds((N_KV, s, HEAD), BF), _sds((N_KV, s, HEAD), BF)],
        compiler_params=_cp(("parallel",)), name="qk_prep")(qkv, pos, wq, wk, invf, bd)


def _sink_column(sink_ref, kh):
    row_g = lax.broadcasted_iota(jnp.int32, (GQA * BLK, 1), 0) // BLK
    col = jnp.zeros((GQA * BLK, 1), F32)
    for g in range(GQA):
        col = jnp.where(row_g == g, sink_ref[kh * GQA + g], col)
    return col


def _fold_band(band, lower, first=None):
    prev, cur = band[:, 0:BLK], band[:, BLK:2 * BLK]
    if first is not None:
        prev = jnp.where(first, -jnp.inf, prev)
    return jnp.where(lower, cur, prev)


def _unfold_band(x, lower):
    return jnp.concatenate([jnp.where(lower, 0.0, x), jnp.where(lower, x, 0.0)], axis=1)


def _attn_probs(q, k, n, sink_col):
    sc = _nt(q, k)
    qi = lax.broadcasted_iota(jnp.int32, (sc.shape[0], BLK), 0) % BLK
    ki = lax.broadcasted_iota(jnp.int32, (sc.shape[0], BLK), 1)
    lower = ki <= qi
    sc = _fold_band(sc, lower, first=n == 0)
    m = jnp.maximum(jnp.max(sc, axis=-1, keepdims=True), sink_col)
    p = jnp.exp(sc - m)
    es = jnp.exp(sink_col - m)
    inv = 1.0 / (jnp.sum(p, axis=-1, keepdims=True) + es)
    return p * inv, es * inv, lower


def _attn_fwd(qh, kh, vh, sinks, deps=()):
    s = qh.shape[1]
    assert s % (2 * BLK) == 0

    def body(sink_ref, q_ref, kp_ref, kc_ref, vp_ref, vc_ref, o_ref):
        i = pl.program_id(0)
        for sub in range(2):
            for khd in range(N_KV):
                q = q_ref[khd * GQA:(khd + 1) * GQA, sub * BLK:(sub + 1) * BLK].reshape(GQA * BLK, HEAD)
                if sub == 0:
                    k = jnp.concatenate([kp_ref[khd], kc_ref[khd, 0:BLK]], axis=0)
                    v = jnp.concatenate([vp_ref[khd], vc_ref[khd, 0:BLK]], axis=0)
                else:
                    k, v = kc_ref[khd], vc_ref[khd]
                probs, _, lower = _attn_probs(q, k, 2 * i + sub, _sink_column(sink_ref, khd))
                o = _mm(_unfold_band(probs, lower).astype(BF), v)
                for j in range(GQA // 2):
                    c0 = khd * GQA * HEAD + 2 * HEAD * j
                    o_ref[sub * BLK:(sub + 1) * BLK, c0:c0 + 2 * HEAD] = jnp.concatenate(
                        [o[2 * j * BLK:(2 * j + 1) * BLK], o[(2 * j + 1) * BLK:(2 * j + 2) * BLK]],
                        axis=1).astype(BF)

    prev = pl.BlockSpec((N_KV, BLK, HEAD), lambda i: (0, jnp.maximum(2 * i - 1, 0), 0))
    cur = pl.BlockSpec((N_KV, 2 * BLK, HEAD), lambda i: (0, i, 0))
    return pl.pallas_call(
        _after(body, 6, deps), grid=(s // (2 * BLK),),
        in_specs=[pl.BlockSpec(memory_space=pltpu.SMEM),
                  pl.BlockSpec((N_Q, 2 * BLK, HEAD), lambda i: (0, i, 0)), prev, cur, prev, cur] + _any_specs(deps),
        out_specs=pl.BlockSpec((2 * BLK, Q_W), lambda i: (i, 0)),
        out_shape=_sds((s, Q_W), BF),
        compiler_params=_cp(("parallel",)), name="attn_fwd")(sinks, qh, kh, kh, vh, vh, *deps)


def _pool_fwd(u, w_pool, pool_scale, ts, deps=()):
    s = u.shape[0]
    halo = 16

    def body(u_ref, wp_ref, ps_ref, a_ref, pooled_ref, prev):
        g = pl.program_id(0)
        i = pl.program_id(1)

        @pl.when(i == 0)
        def _():
            prev[...] = jnp.zeros_like(prev)

        cur = u_ref[...]
        ext = jnp.concatenate([prev[...], cur], axis=0)
        t = (i * ts + lax.broadcasted_iota(jnp.int32, (ts, 1), 0)).astype(F32)
        for gi in range(4):
            @pl.when(g == gi)
            def _(gi=gi):
                w = 2 << gi
                acc, span = ext, 1
                while span < w:
                    acc = acc + pltpu.roll(acc, span, 0)
                    span *= 2
                inv = 1.0 / jnp.minimum(t + 1.0, float(w))
                pooled = (acc[halo:halo + ts] * inv - cur).astype(BF)
                pooled_ref[...] = pooled
                a_ref[...] = (_mm(pooled, wp_ref[0]) * ps_ref[...]).astype(BF)

        prev[...] = cur[ts - halo:ts]

    col = pl.BlockSpec((ts, POOL_GROUP), lambda g, i: (i, g))
    return pl.pallas_call(
        _after(body, 3, deps), grid=(4, s // ts),
        in_specs=[col, pl.BlockSpec((1, POOL_GROUP, POOL_GROUP), lambda g, i: (g, 0, 0)),
                  pl.BlockSpec((1, POOL_GROUP), lambda g, i: (0, g))] + _any_specs(deps),
        out_specs=[col, col],
        out_shape=[_sds((s, D), BF), _sds((s, D), BF)],
        scratch_shapes=[pltpu.VMEM((halo, POOL_GROUP), F32)],
        compiler_params=_cp(("parallel", "arbitrary")), name="pool_fwd")(u, w_pool, pool_scale, *deps)


def _mix_out_proj(a, b, gates, x, w_out, g2, tm):
    s = x.shape[0]

    def body(a_ref, b_ref, gate_ref, x_ref, w_ref, g_ref, mix_ref, y_ref, h_ref):
        mix = (gate_ref[:, 0:D].astype(F32) * a_ref[...].astype(F32)
               + gate_ref[:, D:2 * D].astype(F32) * b_ref[...].astype(F32)).astype(BF)
        mix_ref[...] = mix
        y = x_ref[...] + _mm(mix, w_ref[...])
        y_ref[...] = y
        r = lax.rsqrt(jnp.mean(y * y, axis=-1, keepdims=True) + EPS)
        h_ref[...] = (y * r * g_ref[...]).astype(BF)

    row = lambda w: pl.BlockSpec((tm, w), lambda i: (i, 0))
    return pl.pallas_call(
        body, grid=(s // tm,),
        in_specs=[row(D), row(D), row(2 * D), row(D), _full((D, D)), _full((1, D))],
        out_specs=[row(D), row(D), row(D)],
        out_shape=[_sds((s, D), BF), _sds((s, D), F32), _sds((s, D), BF)],
        compiler_params=_cp(("parallel",)), name="mix_out_proj")(a, b, gates, x, w_out, g2)


def _ffn_up(h2, w_up, conv_w, conv_b, tm):
    s = h2.shape[0]

    def body(h_ref, wg_ref, wv_ref, cwg_ref, cwv_ref, cbg_ref, cbv_ref,
             preg_ref, prev_ref, upg_ref, upv_ref, act_ref, halog, halov):
        i = pl.program_id(1)

        @pl.when(i == 0)
        def _():
            halog[...] = jnp.zeros_like(halog)
            halov[...] = jnp.zeros_like(halov)

        h = h_ref[...]

        def conv_half(w_ref, cw_ref, cb_ref, halo, pre_ref, up_ref):
            pre = _mm(h, w_ref[0])
            pre_ref[...] = pre.astype(BF)
            ext = jnp.concatenate([halo[...], pre], axis=0)
            cw = cw_ref[0]
            up = cb_ref[...] + cw[0:1] * pltpu.roll(ext, 2, 0)[8:8 + tm]
            up = up + cw[1:2] * pltpu.roll(ext, 1, 0)[8:8 + tm]
            up = up + cw[2:3] * pre
            halo[...] = pre[tm - 8:tm]
            up_ref[...] = up.astype(BF)
            return up

        gate = conv_half(wg_ref, cwg_ref, cbg_ref, halog, preg_ref, upg_ref)
        val = conv_half(wv_ref, cwv_ref, cbv_ref, halov, prev_ref, upv_ref)
        act_ref[...] = (gate * jax.nn.sigmoid(gate) * val).astype(BF)

    tile = pl.BlockSpec((tm, UP_SHARD), lambda j, i: (i, j))
    wspec = lambda off: pl.BlockSpec((1, D, UP_SHARD), lambda j, i: (j + off, 0, 0))
    cwspec = lambda off: pl.BlockSpec((1, 3, UP_SHARD), lambda j, i: (j + off, 0, 0))
    cbspec = lambda off: pl.BlockSpec((1, UP_SHARD), lambda j, i: (0, j + off))
    half = _sds((s, D_FF), BF)
    return pl.pallas_call(
        body, grid=(2, s // tm),
        in_specs=[pl.BlockSpec((tm, D), lambda j, i: (i, 0)), wspec(0), wspec(2), cwspec(0), cwspec(2),
                  cbspec(0), cbspec(2)],
        out_specs=[tile] * 5, out_shape=[half] * 5,
        scratch_shapes=[pltpu.VMEM((8, UP_SHARD), F32), pltpu.VMEM((8, UP_SHARD), F32)],
        compiler_params=_cp(("parallel", "arbitrary")), name="ffn_up")(
            h2, w_up, w_up, conv_w, conv_w, conv_b, conv_b)


def _ffn_down_loss(act, w_down, y1, tgt, tm):
    s = y1.shape[0]

    def body(act_ref, w_ref, y_ref, t_ref, dy_ref, dyb_ref, loss_ref):
        @pl.when(pl.program_id(0) == 0)
        def _():
            loss_ref[...] = jnp.zeros_like(loss_ref)

        e = y_ref[...] + _mm(act_ref[...], w_ref[...]) - t_ref[...]
        dy = e * (1.0 / D)
        dy_ref[...] = dy
        dyb_ref[...] = dy.astype(BF)
        e2 = (e * e).reshape(tm // 8, 8, D).sum(axis=0)
        part = e2[:, 0:128]
        for j in range(1, D // 128):
            part = part + e2[:, 128 * j:128 * (j + 1)]
        loss_ref[...] += part

    row = lambda w: pl.BlockSpec((tm, w), lambda i: (i, 0))
    return pl.pallas_call(
        body, grid=(s // tm,),
        in_specs=[row(D_FF), _full((D_FF, D)), row(D), row(D)],
        out_specs=[row(D), row(D), _full((8, 128))],
        out_shape=[_sds((s, D), F32), _sds((s, D), BF), _sds((8, 128), F32)],
        compiler_params=_cp(("arbitrary",)), name="ffn_down_loss")(act, w_down, y1, tgt)


def _grad_matmul(a, b, tn, tk, name, lead=None, prev=None, lead_off=0):
    s, m = a.shape
    n = b.shape[1]
    nj = n // tn

    def body(*refs):
        a_ref, b_ref = refs[0], refs[1]
        o_ref = refs[-1]
        acc = _tn(a_ref[...], b_ref[...])
        acc = acc if lead is None else acc[None]

        @pl.when(pl.program_id(1) == 0)
        def _():
            o_ref[...] = acc

        @pl.when(pl.program_id(1) > 0)
        def _():
            o_ref[...] += acc

    in_specs = [pl.BlockSpec((tk, m), lambda j, k: (k, 0)), pl.BlockSpec((tk, tn), lambda j, k: (k, j))]
    args = [a, b]
    aliases = {}
    if lead is None:
        out_spec = pl.BlockSpec((m, tn), lambda j, k: (0, j))
        out_shape = _sds((m, n), F32)
    else:
        out_spec = pl.BlockSpec((1, m, tn), lambda j, k: (j + lead_off, 0, 0))
        out_shape = _sds((lead, m, tn), F32)
        if prev is not None:
            in_specs.append(pl.BlockSpec(memory_space=pl.ANY))
            args.append(prev)
            aliases = {2: 0}
    return pl.pallas_call(
        body, grid=(nj, s // tk), in_specs=in_specs, out_specs=out_spec, out_shape=out_shape,
        input_output_aliases=aliases,
        compiler_params=_cp(("parallel", "arbitrary")), name=name)(*args)


def _ffn_act_bwd(dyb, w_down, up_g, up_v, pre_g, pre_v, conv_w, tm):
    s = dyb.shape[0]
    nt = s // tm

    def body(dy_ref, wd_ref, upg_ref, upv_ref, preg_ref, prev_ref, cwg_ref, cwv_ref,
             dpg_ref, dpv_ref, dcwg_ref, dcwv_ref, dcbg_ref, dcbv_ref, nxg, nxv):
        i = pl.program_id(1)

        @pl.when(i == 0)
        def _():
            nxg[...] = jnp.zeros_like(nxg)
            nxv[...] = jnp.zeros_like(nxv)
            dcwg_ref[...] = jnp.zeros_like(dcwg_ref)
            dcwv_ref[...] = jnp.zeros_like(dcwv_ref)
            dcbg_ref[...] = jnp.zeros_like(dcbg_ref)
            dcbv_ref[...] = jnp.zeros_like(dcbv_ref)

        dact = _nt(dy_ref[...], wd_ref[...])
        g = upg_ref[...].astype(F32)
        v = upv_ref[...].astype(F32)
        sg = jax.nn.sigmoid(g)
        d_v = dact * (g * sg)
        d_g = dact * v * (sg * (1.0 + g * (1.0 - sg)))

        def conv_bwd(d_up, nx, pre_ref, cw_ref, dp_ref, dcw_ref, dcb_ref):
            ext = jnp.concatenate([d_up, nx[...]], axis=0)
            s1 = pltpu.roll(ext, tm + 8 - 1, 0)[0:tm]
            s2 = pltpu.roll(ext, tm + 8 - 2, 0)[0:tm]
            cw = cw_ref[0]
            dp_ref[...] = (cw[2:3] * d_up + cw[1:2] * s1 + cw[0:1] * s2).astype(BF)
            nx[...] = d_up[0:8]
            pre = pre_ref[...].astype(F32)
            dcw_ref[0, 0:1, :] += jnp.sum(s2 * pre, axis=0, keepdims=True)
            dcw_ref[0, 1:2, :] += jnp.sum(s1 * pre, axis=0, keepdims=True)
            dcw_ref[0, 2:3, :] += jnp.sum(d_up * pre, axis=0, keepdims=True)
            dcb_ref[...] += jnp.sum(d_up, axis=0, keepdims=True)

        conv_bwd(d_g, nxg, preg_ref, cwg_ref, dpg_ref, dcwg_ref, dcbg_ref)
        conv_bwd(d_v, nxv, prev_ref, cwv_ref, dpv_ref, dcwv_ref, dcbv_ref)

    tile = pl.BlockSpec((tm, UP_SHARD), lambda j, i: (nt - 1 - i, j))
    cwspec = lambda off: pl.BlockSpec((1, 3, UP_SHARD), lambda j, i: (j + off, 0, 0))
    acc_cw = pl.BlockSpec((1, 3, UP_SHARD), lambda j, i: (j, 0, 0))
    acc_cb = pl.BlockSpec((1, UP_SHARD), lambda j, i: (0, j))
    buf = pltpu.VMEM((8, UP_SHARD), F32)
    return pl.pallas_call(
        body, grid=(2, nt),
        in_specs=[pl.BlockSpec((tm, D), lambda j, i: (nt - 1 - i, 0)),
                  pl.BlockSpec((UP_SHARD, D), lambda j, i: (j, 0)),
                  tile, tile, tile, tile, cwspec(0), cwspec(2)],
        out_specs=[tile, tile, acc_cw, acc_cw, acc_cb, acc_cb],
        out_shape=[_sds((s, D_FF), BF), _sds((s, D_FF), BF), _sds((2, 3, UP_SHARD), F32),
                   _sds((2, 3, UP_SHARD), F32), _sds((1, D_FF), F32), _sds((1, D_FF), F32)],
        scratch_shapes=[buf, buf],
        compiler_params=_cp(("parallel", "arbitrary")), name="ffn_act_bwd")(
            dyb, w_down, up_g, up_v, pre_g, pre_v, conv_w, conv_w)


def _rms_bwd(dh, y, g):
    r = lax.rsqrt(jnp.mean(y * y, axis=-1, keepdims=True) + EPS)
    n = y * r
    dn = dh * g
    return r * (dn - n * jnp.mean(dn * n, axis=-1, keepdims=True)), dh * n


def _ffn_up_bwd(dp_g, dp_v, w_up, y1, dy2, g2, tm, deps=()):
    s = y1.shape[0]

    def body(dg_ref, dv_ref, w_ref, y_ref, dy2_ref, g_ref, dy1_ref, dgn_ref):
        @pl.when(pl.program_id(0) == 0)
        def _():
            dgn_ref[...] = jnp.zeros_like(dgn_ref)

        dh = _nt(dg_ref[:, 0:UP_SHARD], w_ref[0])
        dh = dh + _nt(dg_ref[:, UP_SHARD:D_FF], w_ref[1])
        dh = dh + _nt(dv_ref[:, 0:UP_SHARD], w_ref[2])
        dh = dh + _nt(dv_ref[:, UP_SHARD:D_FF], w_ref[3])
        dy, dgn = _rms_bwd(dh, y_ref[...], g_ref[...])
        dy1_ref[...] = dy2_ref[...] + dy
        dgn_ref[...] += jnp.sum(dgn, axis=0, keepdims=True)

    row = lambda w: pl.BlockSpec((tm, w), lambda i: (i, 0))
    return pl.pallas_call(
        _after(body, 6, deps), grid=(s // tm,),
        in_specs=[row(D_FF), row(D_FF), _full((4, D, UP_SHARD)), row(D), row(D), _full((1, D))] + _any_specs(deps),
        out_specs=[row(D), _full((1, D))],
        out_shape=[_sds((s, D), F32), _sds((1, D), F32)],
        compiler_params=_cp(("arbitrary",)), name="ffn_up_bwd")(dp_g, dp_v, w_up, y1, dy2, g2, *deps)


def _out_proj_bwd(dy1, w_out, a, b, gates, mix, tm, deps=()):
    s = dy1.shape[0]

    def body(dy_ref, w_ref, a_ref, b_ref, gate_ref, mix_ref, da_ref, db_ref, dzg_ref, dbg_ref, dw_ref):
        @pl.when(pl.program_id(0) == 0)
        def _():
            dbg_ref[...] = jnp.zeros_like(dbg_ref)
            dw_ref[...] = jnp.zeros_like(dw_ref)

        dyb = dy_ref[...].astype(BF)
        dmix = _nt(dyb, w_ref[...])
        gp = gate_ref[:, 0:D].astype(F32)
        ga = gate_ref[:, D:2 * D].astype(F32)
        da_ref[...] = (dmix * gp).astype(BF)
        db_ref[...] = (dmix * ga).astype(BF)
        dzp = dmix * a_ref[...].astype(F32) * (gp * (1.0 - gp))
        dza = dmix * b_ref[...].astype(F32) * (ga * (1.0 - ga))
        dzg_ref[:, 0:D] = dzp.astype(BF)
        dzg_ref[:, D:2 * D] = dza.astype(BF)
        dbg_ref[:, 0:D] += jnp.sum(dzp, axis=0, keepdims=True)
        dbg_ref[:, D:2 * D] += jnp.sum(dza, axis=0, keepdims=True)
        dw_ref[...] += _tn(mix_ref[...], dyb)

    row = lambda w: pl.BlockSpec((tm, w), lambda i: (i, 0))
    return pl.pallas_call(
        _after(body, 6, deps), grid=(s // tm,),
        in_specs=[row(D), _full((D, D)), row(D), row(D), row(2 * D), row(D)] + _any_specs(deps),
        out_specs=[row(D), row(D), row(2 * D), _full((1, 2 * D)), _full((D, D))],
        out_shape=[_sds((s, D), BF), _sds((s, D), BF), _sds((s, 2 * D), BF), _sds((1, 2 * D), F32),
                   _sds((D, D), F32)],
        compiler_params=_cp(("arbitrary",)), name="out_proj_bwd")(dy1, w_out, a, b, gates, mix, *deps)


def _pool_bwd(da, pooled, w_pool, pool_scale, h1, ts):
    s = da.shape[0]
    nt = s // ts
    halo = 16

    def body(da_ref, pooled_ref, wp_ref, ps_ref, h_ref, du_ref, dwp_ref, dps_ref, dwi_ref, nxt):
        g = pl.program_id(0)
        i = pl.program_id(1)
        ti = nt - 1 - i

        @pl.when(i == 0)
        def _():
            nxt[...] = jnp.zeros_like(nxt)
            dwp_ref[...] = jnp.zeros_like(dwp_ref)
            dps_ref[...] = jnp.zeros_like(dps_ref)
            dwi_ref[...] = jnp.zeros_like(dwi_ref)

        pooled = pooled_ref[...]
        dav = da_ref[...].astype(F32)
        dps_ref[...] += jnp.sum(dav * _mm(pooled, wp_ref[0]), axis=0, keepdims=True)
        dm = (dav * ps_ref[...]).astype(BF)
        dwp_ref[0] += _tn(pooled, dm)
        dpool = _nt(dm, wp_ref[0])
        t = (ti * ts + lax.broadcasted_iota(jnp.int32, (ts, 1), 0)).astype(F32)
        for gi in range(4):
            @pl.when(g == gi)
            def _(gi=gi):
                w = 2 << gi
                e = dpool * (1.0 / jnp.minimum(t + 1.0, float(w)))
                acc, span = jnp.concatenate([e, nxt[...]], axis=0), 1
                while span < w:
                    acc = acc + pltpu.roll(acc, ts + halo - span, 0)
                    span *= 2
                du = (acc[0:ts] - dpool).astype(BF)
                du_ref[...] = du
                dwi_ref[...] += _tn(du, h_ref[...])
                nxt[...] = e[0:halo]

    col = pl.BlockSpec((ts, POOL_GROUP), lambda g, i: (nt - 1 - i, g))
    return pl.pallas_call(
        body, grid=(4, nt),
        in_specs=[col, col, pl.BlockSpec((1, POOL_GROUP, POOL_GROUP), lambda g, i: (g, 0, 0)),
                  pl.BlockSpec((1, POOL_GROUP), lambda g, i: (0, g)),
                  pl.BlockSpec((ts, D), lambda g, i: (nt - 1 - i, 0))],
        out_specs=[col, pl.BlockSpec((1, POOL_GROUP, POOL_GROUP), lambda g, i: (g, 0, 0)),
                   pl.BlockSpec((1, POOL_GROUP), lambda g, i: (0, g)),
                   pl.BlockSpec((POOL_GROUP, D), lambda g, i: (g, 0))],
        out_shape=[_sds((s, D), BF), _sds((4, POOL_GROUP, POOL_GROUP), F32), _sds((1, D), F32), _sds((D, D), F32)],
        scratch_shapes=[pltpu.VMEM((halo, POOL_GROUP), F32)],
        compiler_params=_cp(("parallel", "arbitrary")), name="pool_bwd")(da, pooled, w_pool, pool_scale, h1)


def _attn_bwd(qh, kh, vh, sinks, db, deps=()):
    s = qh.shape[1]
    assert s % (2 * BLK) == 0
    pairs = s // (2 * BLK)

    def body(sink_ref, q_ref, kp_ref, kc_ref, vp_ref, vc_ref, do_ref,
             dq_ref, dke_ref, dko_ref, dve_ref, dvo_ref, dsink_ref, ck, cv):
        i = pl.program_id(0)

        @pl.when(i == 0)
        def _():
            ck[...] = jnp.zeros_like(ck)
            cv[...] = jnp.zeros_like(cv)
            dsink_ref[...] = jnp.zeros_like(dsink_ref)

        @pl.when(i < pairs)
        def _():
            for khd in range(N_KV):
                c0 = khd * GQA * HEAD
                band = []
                for sub in range(2):
                    q = q_ref[khd * GQA:(khd + 1) * GQA, sub * BLK:(sub + 1) * BLK].reshape(GQA * BLK, HEAD)
                    if sub == 0:
                        k = jnp.concatenate([kp_ref[khd], kc_ref[khd, 0:BLK]], axis=0)
                        v = jnp.concatenate([vp_ref[khd], vc_ref[khd, 0:BLK]], axis=0)
                    else:
                        k, v = kc_ref[khd], vc_ref[khd]
                    dov = do_ref[sub * BLK:(sub + 1) * BLK, :]
                    do = jnp.concatenate([dov[:, c0 + HEAD * g:c0 + HEAD * (g + 1)] for g in range(GQA)],
                                         axis=0).astype(BF)
                    probs, psink, lower = _attn_probs(q, k, 2 * i + sub, _sink_column(sink_ref, khd))
                    dp = _fold_band(_nt(do, v), lower)
                    delta = jnp.sum(probs * dp, axis=-1, keepdims=True)
                    ds = _unfold_band(probs * (dp - delta), lower).astype(BF)
                    dq_ref[khd * GQA:(khd + 1) * GQA, sub * BLK:(sub + 1) * BLK] = _mm(ds, k).reshape(
                        GQA, BLK, HEAD)
                    band.append((_tn(ds, q), _tn(_unfold_band(probs, lower).astype(BF), do)))
                    dsk = psink * delta
                    lane = lax.broadcasted_iota(jnp.int32, (1, 128), 1)
                    acc = jnp.zeros((1, 128), F32)
                    for g in range(GQA):
                        acc = acc - jnp.where(lane == khd * GQA + g,
                                              jnp.sum(dsk[g * BLK:(g + 1) * BLK], axis=0, keepdims=True), 0.0)
                    dsink_ref[...] += acc
                (dk0, dv0), (dk1, dv1) = band
                dko_ref[khd] = ck[khd] + dk0[0:BLK]
                dvo_ref[khd] = cv[khd] + dv0[0:BLK]
                dke_ref[khd] = dk0[BLK:2 * BLK] + dk1[0:BLK]
                dve_ref[khd] = dv0[BLK:2 * BLK] + dv1[0:BLK]
                ck[khd] = dk1[BLK:2 * BLK]
                cv[khd] = dv1[BLK:2 * BLK]

        @pl.when(i == pairs)
        def _():
            dko_ref[...] = ck[...]
            dvo_ref[...] = cv[...]

    last = pairs - 1
    at = lambda i: jnp.minimum(i, last)
    prev = pl.BlockSpec((N_KV, BLK, HEAD), lambda i: (0, jnp.maximum(2 * at(i) - 1, 0), 0))
    cur = pl.BlockSpec((N_KV, 2 * BLK, HEAD), lambda i: (0, at(i), 0))
    even = pl.BlockSpec((N_KV, BLK, HEAD), lambda i: (0, at(i), 0))
    odd = pl.BlockSpec((N_KV, BLK, HEAD), lambda i: (0, jnp.maximum(i - 1, 0), 0))
    halfkv = _sds((N_KV, s // 2, HEAD), F32)
    dq, dke, dko, dve, dvo, dsink = pl.pallas_call(
        _after(body, 7, deps), grid=(pairs + 1,),
        in_specs=[pl.BlockSpec(memory_space=pltpu.SMEM),
                  pl.BlockSpec((N_Q, 2 * BLK, HEAD), lambda i: (0, at(i), 0)),
                  prev, cur, prev, cur,
                  pl.BlockSpec((2 * BLK, Q_W), lambda i: (at(i), 0))] + _any_specs(deps),
        out_specs=[pl.BlockSpec((N_Q, 2 * BLK, HEAD), lambda i: (0, at(i), 0)), even, odd, even, odd,
                   _full((1, 128))],
        out_shape=[_sds((N_Q, s, HEAD), F32), halfkv, halfkv, halfkv, halfkv, _sds((1, 128), F32)],
        scratch_shapes=[pltpu.VMEM((N_KV, BLK, HEAD), F32), pltpu.VMEM((N_KV, BLK, HEAD), F32)],
        compiler_params=_cp(("arbitrary",)), name="attn_bwd")(sinks, qh, kh, kh, vh, vh, db, *deps)

    return dq, (dke, dko), (dve, dvo), dsink


def _qk_prep_bwd(dqh, dk_eo, dv_eo, qkv, pos, wq, wk, invf, bd, tm, deps=()):
    s = qkv.shape[0]
    assert tm % (2 * BLK) == 0

    def by_token(ev_ref, od_ref):
        cols = []
        for h in range(N_KV):
            rows = []
            for m in range(tm // (2 * BLK)):
                rows += [ev_ref[h, m * BLK:(m + 1) * BLK], od_ref[h, m * BLK:(m + 1) * BLK]]
            cols.append(jnp.concatenate(rows, axis=0))
        return jnp.concatenate(cols, axis=1)

    def fold_heads(row):
        out = row[:, 0:HEAD]
        for h in range(1, row.shape[1] // HEAD):
            out = out + row[:, HEAD * h:HEAD * (h + 1)]
        return out

    def body(dq_ref, dke_ref, dko_ref, dve_ref, dvo_ref, qkv_ref, pos_ref, wq_ref, wk_ref, invf_ref, bd_ref,
             dz_ref, dwq_ref, dwk_ref):
        @pl.when(pl.program_id(0) == 0)
        def _():
            dwq_ref[...] = jnp.zeros_like(dwq_ref)
            dwk_ref[...] = jnp.zeros_like(dwk_ref)

        cos, sa, sb = _rope_tables(pos_ref, invf_ref)

        def norm_rope_bwd(dy, xin, w, bdm):
            dn = _rope_t(dy, cos, sa, sb)
            r = lax.rsqrt(_head_sum(xin * xin, bdm) * (1.0 / HEAD) + EPS)
            nh = xin * r
            gw = dn * w
            dx = r * (gw - nh * (_head_sum(gw * nh, bdm) * (1.0 / HEAD)))
            return dx, fold_heads(jnp.sum(dn * nh, axis=0, keepdims=True))

        dq = jnp.concatenate([dq_ref[h] for h in range(N_Q)], axis=1) * (HEAD ** -0.5)
        dk = by_token(dke_ref, dko_ref)
        dxq, dwq = norm_rope_bwd(dq, qkv_ref[:, 0:Q_W], wq_ref[...], bd_ref[...])
        dxk, dwk = norm_rope_bwd(dk, qkv_ref[:, Q_W:Q_W + KV_W], wk_ref[...], bd_ref[0:KV_W, 0:KV_W])
        dz_ref[:, 0:Q_W] = dxq.astype(BF)
        dz_ref[:, Q_W:Q_W + KV_W] = dxk.astype(BF)
        dz_ref[:, Q_W + KV_W:QKV_W] = by_token(dve_ref, dvo_ref).astype(BF)
        dwq_ref[...] += dwq
        dwk_ref[...] += dwk

    heads = lambda n: pl.BlockSpec((n, tm, HEAD), lambda i: (0, i, 0))
    half = pl.BlockSpec((N_KV, tm // 2, HEAD), lambda i: (0, i, 0))
    return pl.pallas_call(
        _after(body, 11, deps), grid=(s // tm,),
        in_specs=[heads(N_Q), half, half, half, half, pl.BlockSpec((tm, QKV_W), lambda i: (i, 0)),
                  pl.BlockSpec((tm, 1), lambda i: (i, 0)), _full((1, Q_W)), _full((1, KV_W)),
                  _full((1, 2 * HEAD)), _full((Q_W, Q_W))] + _any_specs(deps),
        out_specs=[pl.BlockSpec((tm, QKV_W), lambda i: (i, 0)), _full((1, HEAD)), _full((1, HEAD))],
        out_shape=[_sds((s, QKV_W), BF), _sds((1, HEAD), F32), _sds((1, HEAD), F32)],
        compiler_params=_cp(("arbitrary",)), name="qk_prep_bwd")(
            dqh, *dk_eo, *dv_eo, qkv, pos, wq, wk, invf, bd, *deps)


def _in_proj_bwd(du, dzq, dzg, w_in, x, g1, dy1, tm, deps=()):
    s = x.shape[0]

    def body(du_ref, dzq_ref, dzg_ref, w_ref, x_ref, g_ref, dy_ref, gx_ref, dgn_ref):
        @pl.when(pl.program_id(0) == 0)
        def _():
            dgn_ref[...] = jnp.zeros_like(dgn_ref)

        dh = _mm(du_ref[...], w_ref[0:D, :])
        dh = dh + _mm(dzq_ref[...], w_ref[D:D + QKV_W, :])
        dh = dh + _mm(dzg_ref[...], w_ref[D + QKV_W:IN_W, :])
        dx, dgn = _rms_bwd(dh, x_ref[...], g_ref[...])
        gx_ref[...] = dy_ref[...] + dx
        dgn_ref[...] += jnp.sum(dgn, axis=0, keepdims=True)

    row = lambda w: pl.BlockSpec((tm, w), lambda i: (i, 0))
    return pl.pallas_call(
        _after(body, 7, deps), grid=(s // tm,),
        in_specs=[row(D), row(QKV_W), row(2 * D), _full((IN_W, D)), row(D), _full((1, D)), row(D)] + _any_specs(deps),
        out_specs=[row(D), _full((1, D))],
        out_shape=[_sds((s, D), F32), _sds((1, D), F32)],
        compiler_params=_cp(("arbitrary",)), name="in_proj_bwd")(du, dzq, dzg, w_in, x, g1, dy1, *deps)


def _adamw_step(w, g, m, v):
    mn = B1 * m + (1.0 - B1) * g
    vn = B2 * v + (1.0 - B2) * (g * g)
    m_hat = mn / (1.0 - B1 ** STEP)
    v_hat = vn / (1.0 - B2 ** STEP)
    return -LR * (m_hat / (jnp.sqrt(v_hat) + ADAM_EPS) + WD * w), mn, vn


SMALL_ROWS = 16
SMALL_COLS = D_FF
SMALL_AT = {"b_gate": (1, 2 * D), "pool_scale": (2, D), "q_norm": (3, HEAD),
            "k_norm": (4, HEAD), "sinks": (5, N_Q), "ffn_norm": (6, D)}
SMALL_LOSS_ROW = 0
SMALL_CONV_B_ROW = 7
SMALL_CONV_W_ROW = 9


def _pack_small(loss_acc, d_bgate, d_pscale, d_qn, d_kn, dsink, d_ffn_norm, dcb_g, dcb_v, dcw_g, dcw_v, dev):
    def body(k_ref, ls_ref, bg_ref, ps_ref, qn_ref, kn_ref, sk_ref, fn_ref, cbg_ref, cbv_ref, cwg_ref, cwv_ref,
             o_ref):
        o_ref[...] = jnp.zeros_like(o_ref)
        o_ref[0, SMALL_LOSS_ROW:SMALL_LOSS_ROW + 1, 0:128] = jnp.sum(ls_ref[...], axis=0, keepdims=True)
        for nm, ref in (("b_gate", bg_ref), ("pool_scale", ps_ref), ("q_norm", qn_ref),
                        ("k_norm", kn_ref), ("ffn_norm", fn_ref)):
            row, n = SMALL_AT[nm]
            o_ref[0, row:row + 1, 0:n] = ref[...]
        row, _ = SMALL_AT["sinks"]
        o_ref[0, row:row + 1, 0:128] = sk_ref[...]
        o_ref[0, SMALL_CONV_B_ROW:SMALL_CONV_B_ROW + 1, :] = cbg_ref[...]
        o_ref[0, SMALL_CONV_B_ROW + 1:SMALL_CONV_B_ROW + 2, :] = cbv_ref[...]
        for k in range(3):
            row = SMALL_CONV_W_ROW + 2 * k
            for half in range(2):
                o_ref[0, row:row + 1, half * UP_SHARD:(half + 1) * UP_SHARD] = cwg_ref[half, k:k + 1, :]
                o_ref[0, row + 1:row + 2, half * UP_SHARD:(half + 1) * UP_SHARD] = cwv_ref[half, k:k + 1, :]

    args = [loss_acc, d_bgate, d_pscale, d_qn, d_kn, dsink, d_ffn_norm, dcb_g, dcb_v, dcw_g, dcw_v]
    grid_spec = pltpu.PrefetchScalarGridSpec(
        num_scalar_prefetch=1, grid=(1,),
        in_specs=[pl.BlockSpec(a.shape, functools.partial(lambda nd, i, k: (0,) * nd, a.ndim)) for a in args],
        out_specs=pl.BlockSpec((1, SMALL_ROWS, SMALL_COLS), lambda i, k: (k[0], 0, 0)))
    return pl.pallas_call(body, grid_spec=grid_spec, out_shape=_sds((N_DEV, SMALL_ROWS, SMALL_COLS), F32),
                          name="pack_small")(dev, *args)


def _small_update(stack, attn_stack, params):
    names = list(params)

    def body(*refs):
        s_ref, a_ref = refs[0], refs[1]
        ins = refs[2:2 + 3 * len(names)]
        outs = refs[2 + 3 * len(names):]
        tot, tot_a = s_ref[0], a_ref[0]
        for d in range(1, N_DEV):
            tot = tot + s_ref[d]
            tot_a = tot_a + a_ref[d]
        for i, nm in enumerate(names):
            if nm == "attn_norm":
                g = tot_a
            elif nm == "conv_b":
                g = jnp.concatenate([tot[SMALL_CONV_B_ROW:SMALL_CONV_B_ROW + 1, :],
                                     tot[SMALL_CONV_B_ROW + 1:SMALL_CONV_B_ROW + 2, :]], axis=1)
            else:
                row, n = SMALL_AT[nm]
                g = tot[row:row + 1, 0:n]
            delta, mn, vn = _adamw_step(ins[3 * i][...], g, ins[3 * i + 1][...], ins[3 * i + 2][...])
            outs[4 * i][...] = g
            outs[4 * i + 1][...] = delta
            outs[4 * i + 2][...] = mn
            outs[4 * i + 3][...] = vn
        for k in range(3):
            row = SMALL_CONV_W_ROW + 2 * k
            outs[-2][k:k + 1, 0:D_FF] = tot[row:row + 1, :]
            outs[-2][k:k + 1, D_FF:2 * D_FF] = tot[row + 1:row + 2, :]
        outs[-1][...] = jnp.sum(tot[SMALL_LOSS_ROW:SMALL_LOSS_ROW + 1, 0:128], axis=1, keepdims=True) * (0.5 / D)

    flat = [a for nm in names for a in params[nm]]
    out_shape = ([_sds(params[nm][0].shape, F32) for nm in names for _ in range(4)]
                 + [_sds((3, 2 * D_FF), F32), _sds((1, 1), F32)])
    res = pl.pallas_call(body, out_shape=out_shape, name="small_update")(stack, attn_stack, *flat)
    return {nm: list(res[4 * i:4 * i + 4]) for i, nm in enumerate(names)}, res[-2], res[-1]


def _adamw(w, g, m, v, tr, name, deps=()):
    r, c = w.shape

    def body(w_ref, g_ref, m_ref, v_ref, go_ref, d_ref, mo_ref, vo_ref):
        gv = g_ref[...]
        go_ref[...] = gv
        d_ref[...], mo_ref[...], vo_ref[...] = _adamw_step(w_ref[...], gv, m_ref[...], v_ref[...])

    blk = pl.BlockSpec((tr, c), lambda i: (i, 0))
    return pl.pallas_call(
        _after(body, 4, deps), grid=(r // tr,), in_specs=[blk] * 4 + _any_specs(deps), out_specs=[blk] * 4,
        out_shape=[_sds((r, c), F32)] * 4, compiler_params=_cp(("parallel",)), name=name)(w, g, m, v, *deps)


def _place():
    x, y, c = lax.axis_index("x"), lax.axis_index("y"), lax.axis_index("c")
    chips = [(1 - x, y), (x, 1 - y), (1 - x, 1 - y)]
    return x, y, c, chips


def _rows(ref, lead, h, rh):
    sl = pl.ds(pl.multiple_of(h * rh, 16), rh)
    return ref.at[sl, :] if lead is None else ref.at[lead, sl, :]


def _all_gather_weights(halved, whole, placed):
    nh, nw, npl = len(halved), len(whole), len(placed)
    na = nh + nw
    nall = na + npl
    arrays = list(halved) + list(whole) + list(placed)
    out_dtypes = [BF] * nh + [a.dtype for a in whole] + [BF] * npl
    cast_rows = 128

    def body(*refs):
        ins, outs = refs[:nall], refs[nall:2 * nall]
        raw, stage = refs[2 * nall:3 * nall], refs[3 * nall:3 * nall + nh + npl]
        ici_send, ici_recv, fwd_send, fwd_recv, in_sem, loc_sem = refs[3 * nall + nh + npl:]
        x, y, c, chips = _place()
        me = 2 * x + y
        sibling = (x, y, 1 - c)
        loads = [pltpu.make_async_copy(ins[a], raw[a], in_sem.at[a]) for a in range(nall)]
        for cp in loads:
            cp.start()

        def cast(a, dst):
            r = arrays[a].shape[0]
            for r0 in range(0, r, cast_rows):
                r1 = min(r0 + cast_rows, r)
                dst[r0:r1, :] = raw[a][r0:r1, :].astype(BF)

        def ici(a, j, src_chip, src=None):
            if a < nh:
                rh = arrays[a].shape[0] // 2
                dst = _rows(outs[a], src_chip, c, rh)
                src = dst if src is None else _rows(src, None, c, rh)
            else:
                dst = outs[a].at[src_chip]
                src = dst if src is None else src
            return pltpu.make_async_remote_copy(
                src_ref=src, dst_ref=dst, send_sem=ici_send.at[3 * a + j], recv_sem=ici_recv.at[3 * a + j],
                device_id=(*chips[j], c), device_id_type=MESH)

        def fwd(a, j, half):
            rh = arrays[a].shape[0] // 2
            kj = 2 * chips[j][0] + chips[j][1]
            blk = _rows(outs[a], kj, half, rh)
            return pltpu.make_async_remote_copy(
                src_ref=blk, dst_ref=blk, send_sem=fwd_send.at[3 * a + j], recv_sem=fwd_recv.at[3 * a + j],
                device_id=sibling, device_id_type=MESH)

        local, sends = [], []
        for a in range(na):
            loads[a].wait()
            if a < nh:
                cast(a, stage[a])
                own = stage[a]
            else:
                own = raw[a]
            cp = pltpu.make_async_copy(own, outs[a].at[me], loc_sem.at[a])
            cp.start()
            local.append(cp)
            for j in range(3):
                cp = ici(a, j, me, src=own)
                cp.start()
                sends.append(cp)
        for i in range(npl):
            loads[na + i].wait()
            cast(na + i, stage[nh + i])
            cp = pltpu.make_async_copy(stage[nh + i], outs[na + i].at[me], loc_sem.at[na + i])
            cp.start()
            local.append(cp)
        passed = []
        for a in range(na):
            for j in range(3):
                kj = 2 * chips[j][0] + chips[j][1]
                ici(a, j, kj).wait_recv()
                if a < nh:
                    cp = fwd(a, j, c)
                    cp.start()
                    passed.append(cp)
        for a in range(nh):
            for j in range(3):
                fwd(a, j, 1 - c).wait_recv()
        for cp in sends + passed:
            cp.wait_send()
        for cp in local:
            cp.wait()

    any_spec = pl.BlockSpec(memory_space=pl.ANY)
    return pl.pallas_call(
        body, in_specs=[any_spec] * nall, out_specs=[any_spec] * nall,
        out_shape=[_sds((N_CHIPS,) + a.shape, dt) for a, dt in zip(arrays, out_dtypes)],
        scratch_shapes=[pltpu.VMEM(a.shape, a.dtype) for a in arrays]
        + [pltpu.VMEM(a.shape, BF) for a in list(halved) + list(placed)]
        + [pltpu.SemaphoreType.DMA((3 * na,)), pltpu.SemaphoreType.DMA((3 * na,)),
           pltpu.SemaphoreType.DMA((3 * nh,)), pltpu.SemaphoreType.DMA((3 * nh,)),
           pltpu.SemaphoreType.DMA((nall,)), pltpu.SemaphoreType.DMA((nall,))],
        compiler_params=pltpu.CompilerParams(vmem_limit_bytes=VMEM_LIMIT_MB << 20),
        name="all_gather_weights")(*arrays)


def _pair_sum(g, recv, c, tr, name):
    _, r, cols = g.shape
    rh = r // 2
    nr = rh // tr

    def body(c_ref, g_ref, r_ref, o_ref):
        o_ref[...] = (g_ref[...] + r_ref[...]).astype(BF)

    grid_spec = pltpu.PrefetchScalarGridSpec(
        num_scalar_prefetch=1, grid=(N_CHIPS, nr),
        in_specs=[pl.BlockSpec((1, tr, cols), lambda k, i, c_ref: (k, c_ref[0] * nr + i, 0)),
                  pl.BlockSpec((1, tr, cols), lambda k, i, c_ref: (k, i, 0))],
        out_specs=pl.BlockSpec((1, tr, cols), lambda k, i, c_ref: (k, i, 0)))
    return pl.pallas_call(
        body, grid_spec=grid_spec, out_shape=_sds((N_CHIPS, rh, cols), BF),
        compiler_params=_cp(("parallel", "parallel")), name=name)(c, g, recv)


def _chip_sum(g, sib, recv, place, tr, name):
    _, r, cols = g.shape
    rh = r // 2
    nr = rh // tr

    def body(p_ref, g_ref, s_ref, r0_ref, r1_ref, r2_ref, o_ref):
        own = g_ref[0] + s_ref[0]
        o_ref[...] = ((own + r0_ref[0].astype(F32)) + r1_ref[0].astype(F32)) + r2_ref[0].astype(F32)

    rspec = lambda j: pl.BlockSpec((1, tr, cols), lambda i, p: (j, i, 0))
    grid_spec = pltpu.PrefetchScalarGridSpec(
        num_scalar_prefetch=1, grid=(nr,),
        in_specs=[pl.BlockSpec((1, tr, cols), lambda i, p: (p[0], p[1] * nr + i, 0)),
                  pl.BlockSpec((1, tr, cols), lambda i, p: (p[0], i, 0)), rspec(0), rspec(1), rspec(2)],
        out_specs=pl.BlockSpec((tr, cols), lambda i, p: (p[1] * nr + i, 0)))
    return pl.pallas_call(
        body, grid_spec=grid_spec, out_shape=_sds((r, cols), F32),
        compiler_params=_cp(("parallel",)), name=name)(place, g, sib, recv, recv, recv)


_HBM = pl.BlockSpec(memory_space=pltpu.HBM)
_SEM = pl.BlockSpec(memory_space=pltpu.SEMAPHORE)
_EFFECT = pltpu.SideEffectType.DATAFLOW_SIDE_EFFECTING


def _remote(src, dst, ssem, rsem, k, device):
    return pltpu.make_async_remote_copy(src_ref=src, dst_ref=dst, send_sem=ssem.at[k], recv_sem=rsem.at[k],
                                        device_id=device, device_id_type=MESH)


def _split_start(name, bufs, plan, n):
    nb = len(bufs)

    def body(*refs):
        sends, _ = plan(refs[:nb], refs[nb], refs[nb + 1])
        for cp in sends:
            cp.start()
        refs[-1][...] = jnp.zeros_like(refs[-1])

    res = pl.pallas_call(
        body, name=name,
        out_shape=(pltpu.SemaphoreType.DMA((n,)), pltpu.SemaphoreType.DMA((n,)))
        + tuple(pltpu.HBM(b.shape, b.dtype) for b in bufs) + (_sds((8, 128), F32),),
        in_specs=[_HBM] * nb,
        out_specs=(_SEM, _SEM) + (_HBM,) * nb + (pl.BlockSpec(memory_space=pltpu.VMEM),),
        input_output_aliases={i: i + 2 for i in range(nb)},
        compiler_params=pltpu.CompilerParams(has_side_effects=_EFFECT),
    )(*[pltpu.with_memory_space_constraint(b, pltpu.HBM) for b in bufs])
    return res[0], res[1], list(res[2:2 + nb]), res[2 + nb]


def _split_wait(name, send_sem, recv_sem, bufs, plan, after):
    nb = len(bufs)

    def body(*refs):
        sends, arrivals = plan(refs[:nb], refs[nb], refs[nb + 1])
        for cp in sends:
            cp.wait_send()
        for cp in arrivals:
            cp.wait_recv()

    res = pl.pallas_call(
        body, name=name, out_shape=tuple(pltpu.HBM(b.shape, b.dtype) for b in bufs),
        in_specs=[_HBM] * nb + [_SEM, _SEM, pl.BlockSpec(memory_space=pl.ANY)],
        out_specs=(_HBM,) * nb, input_output_aliases={i: i for i in range(nb)},
        compiler_params=pltpu.CompilerParams(has_side_effects=_EFFECT),
    )(*bufs, send_sem, recv_sem, after)
    return list(res)


def _plan_sibling_halves(shapes):
    na = len(shapes)

    def plan(refs, ssem, rsem):
        x, y, c, _ = _place()
        cps = []
        for a in range(na):
            rh = shapes[a][1] // 2
            src = refs[a].at[:, pl.ds(pl.multiple_of((1 - c) * rh, 8), rh), :]
            cps.append(_remote(src, refs[na + a], ssem, rsem, a, (x, y, 1 - c)))
        return cps, cps

    return plan


def _to_all(ref, ssem, rsem, base):
    x, y, c, _ = _place()
    mine = ref.at[4 * x + 2 * y + c]
    return [_remote(mine, mine, ssem, rsem, base + r - 1, (x ^ (r >> 2), y ^ ((r >> 1) & 1), c ^ (r & 1)))
            for r in range(1, N_DEV)]


def _plan_chip_exchange(na, with_small):
    def plan(refs, ssem, rsem):
        _, _, c, chips = _place()
        cps = []
        for a in range(na):
            for j in range(3):
                kj = 2 * chips[j][0] + chips[j][1]
                cps.append(_remote(refs[a].at[kj], refs[na + a].at[j], ssem, rsem, 3 * a + j, (*chips[j], c)))
        if with_small:
            cps += _to_all(refs[2 * na], ssem, rsem, 3 * na)
        return cps, cps

    return plan


def _plan_sibling_swap(shapes, with_small):
    def plan(refs, ssem, rsem):
        x, y, c, _ = _place()
        sends, arrivals = [], []
        for a, shp in enumerate(shapes):
            rh = shp[0] // 2
            mine, other = _rows(refs[a], None, c, rh), _rows(refs[a], None, 1 - c, rh)
            sends.append(_remote(mine, mine, ssem, rsem, a, (x, y, 1 - c)))
            arrivals.append(_remote(mine, other, ssem, rsem, a, (x, y, 1 - c)))
        if with_small:
            cps = _to_all(refs[len(shapes)], ssem, rsem, len(shapes))
            sends += cps
            arrivals += cps
        return sends, arrivals

    return plan


def _plan_gather_chips(shapes):
    def plan(refs, ssem, rsem):
        x, y, c, chips = _place()
        me = 2 * x + y
        sends, arrivals = [], []
        for a, shp in enumerate(shapes):
            rh = shp[1] // 2
            mine = _rows(refs[a], me, c, rh)
            for j in range(3):
                land = _rows(refs[a], 2 * chips[j][0] + chips[j][1], c, rh)
                sends.append(_remote(mine, mine, ssem, rsem, 3 * a + j, (*chips[j], c)))
                arrivals.append(_remote(land, land, ssem, rsem, 3 * a + j, (*chips[j], c)))
        return sends, arrivals

    return plan


def _plan_gather_sibling(shapes):
    def plan(refs, ssem, rsem):
        x, y, c, chips = _place()
        sends, arrivals = [], []
        for a, shp in enumerate(shapes):
            rh = shp[1] // 2
            for j in range(3):
                kj = 2 * chips[j][0] + chips[j][1]
                got, land = _rows(refs[a], kj, c, rh), _rows(refs[a], kj, 1 - c, rh)
                sends.append(_remote(got, got, ssem, rsem, 3 * a + j, (x, y, 1 - c)))
                arrivals.append(_remote(got, land, ssem, rsem, 3 * a + j, (x, y, 1 - c)))
        return sends, arrivals

    return plan


def _into_slice(w, k, n, tr, dtype, name, deps=()):
    r, cols = w.shape

    def body(k_ref, w_ref, o_ref):
        o_ref[0] = w_ref[...].astype(dtype)

    grid_spec = pltpu.PrefetchScalarGridSpec(
        num_scalar_prefetch=1, grid=(r // tr,),
        in_specs=[pl.BlockSpec((tr, cols), lambda i, k: (i, 0))] + _any_specs(deps),
        out_specs=pl.BlockSpec((1, tr, cols), lambda i, k: (k[0], i, 0)))
    return pl.pallas_call(_after(body, 2, deps), grid_spec=grid_spec, out_shape=_sds((n, r, cols), dtype),
                          compiler_params=_cp(("parallel",)), name=name)(k, w, *deps)


class _LateWeights:
    def __init__(self, bufs):
        self.n = 3 * len(bufs)
        self.chips, self.sibling = _plan_gather_chips([b.shape for b in bufs]), _plan_gather_sibling([b.shape for b in bufs])
        self.ssem, self.rsem, self.bufs, token = _split_start("gather_chips_start", bufs, self.chips, self.n)
        self.first = (token,)

    def middle(self, after):
        bufs = _split_wait("gather_chips_wait", self.ssem, self.rsem, self.bufs, self.chips, after)
        self.ssem, self.rsem, self.bufs, token = _split_start("gather_sibling_start", bufs, self.sibling, self.n)
        return (token,)

    def last(self, after):
        return _split_wait("gather_sibling_wait", self.ssem, self.rsem, self.bufs, self.sibling, after)


class _GradReduce:
    def __init__(self, tag, place, names, tiles):
        self.tag, self.place, self.names, self.tiles = tag, place, names, tiles
        self.small_all = None

    def first(self, grads):
        self.na = len(grads)
        self.p1 = _plan_sibling_halves([g.shape for g in grads])
        lands = [lax.empty((N_CHIPS, g.shape[1] // 2, g.shape[2]), F32) for g in grads]
        self.ssem, self.rsem, self.bufs, token = _split_start(
            self.tag + "_halves_start", list(grads) + lands, self.p1, self.na)
        return (token,)

    def second(self, after, small=None):
        bufs = _split_wait(self.tag + "_halves_wait", self.ssem, self.rsem, self.bufs, self.p1, after)
        self.grads, self.sib = bufs[:self.na], bufs[self.na:]
        halves = [_pair_sum(g, r, self.place[1:2], t, "pair_sum_" + nm)
                  for g, r, t, nm in zip(self.grads, self.sib, self.tiles, self.names)]
        lands = [lax.empty((3,) + h.shape[1:], h.dtype) for h in halves]
        extra = [] if small is None else [small]
        self.p2 = _plan_chip_exchange(self.na, small is not None)
        self.ssem, self.rsem, self.bufs, token = _split_start(
            self.tag + "_chips_start", halves + lands + extra, self.p2, 3 * self.na + (N_DEV - 1) * len(extra))
        return (token,)

    def third(self, after, small=None):
        bufs = _split_wait(self.tag + "_chips_wait", self.ssem, self.rsem, self.bufs, self.p2, after)
        if len(bufs) > 2 * self.na:
            self.small_chips = bufs[2 * self.na]
        mine = [_chip_sum(g, sb, r, self.place, t, "chip_sum_" + nm)
                for g, sb, r, t, nm in zip(self.grads, self.sib, bufs[self.na:2 * self.na], self.tiles, self.names)]
        extra = [] if small is None else [small]
        self.p3 = _plan_sibling_swap([m.shape for m in mine], small is not None)
        self.ssem, self.rsem, self.bufs, token = _split_start(
            self.tag + "_swap_start", mine + extra, self.p3, self.na + (N_DEV - 1) * len(extra))
        return (token,)

    def last(self, after):
        bufs = _split_wait(self.tag + "_swap_wait", self.ssem, self.rsem, self.bufs, self.p3, after)
        if len(bufs) > self.na:
            self.small_swap = bufs[self.na]
        return bufs[:self.na]


class _WeightsAtHand:
    def __init__(self, wup, wout, wdown):
        self.first, self.weights = (), [wup, wout, wdown]

    def middle(self, after):
        return ()

    def last(self, after):
        return self.weights


class _GradsKept:
    def first(self, grads):
        self.grads = list(grads)
        return ()

    def second(self, after, small=None):
        return ()

    def third(self, after, small=None):
        return ()

    def last(self, after):
        return self.grads


def _forward_backward(xs, pos, tgt, win, wpool, cw, attn_norm, b_gate, pool_scale, q_norm, k_norm, sinks,
                      ffn_norm, conv_b, late, early, rest, dev):
    s = xs.shape[0]
    tm = min(512, s)
    tk = min(2048, s)
    inv_freq = np.float32(ROPE_THETA) ** (-np.arange(0, ROPE_DIM, 2, dtype=np.float32) / np.float32(ROPE_DIM))
    lane = np.arange(2 * HEAD) % HEAD
    invf = jnp.asarray(np.where(lane < ROPE_DIM, inv_freq[lane % (ROPE_DIM // 2)], 0.0).astype(np.float32)
                       .reshape(1, 2 * HEAD))
    wq = jnp.tile(q_norm, (1, N_Q))
    wk = jnp.tile(k_norm, (1, N_KV))
    head_of = jnp.arange(Q_W) // HEAD
    bd = (head_of[:, None] == head_of[None, :]).astype(BF)
    sink = sinks[0]

    h1, u, qkv, gates = _attn_in_proj(xs, attn_norm, win, b_gate, tm, deps=late.first)
    qh, kh, vh = _qk_prep(qkv, pos, wq, wk, invf, bd, tm)
    apool, pooled = _pool_fwd(u, wpool, pool_scale, min(2048, s), deps=(qh,))
    battn = _attn_fwd(qh, kh, vh, sink, deps=late.middle(apool))
    wup, wout, wdown = late.last(battn)
    wout = wout.reshape(D, D)
    wdown = wdown.reshape(D_FF, D)
    mix, y1, h2 = _mix_out_proj(apool, battn, gates, xs, wout, ffn_norm, tm)
    pre_g, pre_v, up_g, up_v, act = _ffn_up(h2, wup, cw, conv_b, tm)
    dy2, dy2b, loss_acc = _ffn_down_loss(act, wdown, y1, tgt, tm)

    d_wdown = _grad_matmul(act, dy2b, 512, tk, "grad_w_down")
    dp_g, dp_v, dcw_g, dcw_v, dcb_g, dcb_v = _ffn_act_bwd(dy2b, wdown, up_g, up_v, pre_g, pre_v, cw, tm)
    d_wup = _grad_matmul(h2, dp_g, UP_SHARD, tk, "grad_w_up_gate", lead=N_CHIPS)
    d_wup = _grad_matmul(h2, dp_v, UP_SHARD, tk, "grad_w_up_value", lead=N_CHIPS, prev=d_wup, lead_off=2)
    token = early.first([d_wdown.reshape(N_CHIPS, D_FF // N_CHIPS, D), d_wup])
    dy1, d_ffn_norm = _ffn_up_bwd(dp_g, dp_v, wup, y1, dy2, ffn_norm, tm, deps=token)
    token = early.second(dy1)
    da, db, dzg, d_bgate, d_wout = _out_proj_bwd(dy1, wout, apool, battn, gates, mix, tm, deps=token)
    du, d_wpool, d_pscale, d_win_pool = _pool_bwd(da, pooled, wpool, pool_scale, h1, min(2048, s))
    dqh, dkh, dvh, dsink = _attn_bwd(qh, kh, vh, sink, db)
    token = early.third(dqh)
    dzq, d_qn, d_kn = _qk_prep_bwd(dqh, dkh, dvh, qkv, pos, wq, wk, invf, bd, tm, deps=token)
    d_win_t = jnp.concatenate([
        d_win_pool,
        _grad_matmul(dzq, h1, D, tk, "grad_w_in_qkv"),
        _grad_matmul(dzg, h1, D, tk, "grad_w_in_gates")], axis=0)
    token = rest.first([
        d_win_t.reshape(N_CHIPS, IN_W // N_CHIPS, D),
        d_wout.reshape(N_CHIPS, D // N_CHIPS, D),
        d_wpool.reshape(4, N_CHIPS, 64, POOL_GROUP).transpose(1, 0, 2, 3).reshape(N_CHIPS, 4 * 64, POOL_GROUP)])
    small = _pack_small(loss_acc, d_bgate, d_pscale, d_qn, d_kn, dsink, d_ffn_norm, dcb_g, dcb_v, dcw_g, dcw_v, dev)
    token = rest.second(token[0] if token else None, small=small)
    grad_x, d_attn_norm = _in_proj_bwd(du, dzq, dzg, win, xs, attn_norm, dy1, tm, deps=token)
    return grad_x, d_attn_norm, small


def kernel(x, positions, attn_norm, w_in, b_gate, w_pool, pool_scale, q_norm, k_norm, sinks, w_out, ffn_norm, w_up, conv_w, conv_b, w_down, loss_target, m_attn_norm, m_w_in, m_b_gate, m_w_pool, m_pool_scale, m_q_norm, m_k_norm, m_sinks, m_w_out, m_ffn_norm, m_w_up, m_conv_w, m_conv_b, m_w_down, v_attn_norm, v_w_in, v_b_gate, v_w_pool, v_pool_scale, v_q_norm, v_k_norm, v_sinks, v_w_out, v_ffn_norm, v_w_up, v_conv_w, v_conv_b, v_w_down):
    s = x.shape[1]
    xs = x[0]
    tgt = loss_target[0]
    pos = positions[0].reshape(s, 1)
    cx, cy, cc = lax.axis_index("x"), lax.axis_index("y"), lax.axis_index("c")
    chip = 2 * cx + cy

    dev_arr = (2 * chip + cc).reshape(1).astype(jnp.int32)
    place = jnp.stack([chip, cc]).astype(jnp.int32)

    g_in, g_pool, g_cw, *own_late = _all_gather_weights(
        [jnp.swapaxes(w_in[0], 0, 1), w_pool[0].reshape(4 * 64, POOL_GROUP)], [conv_w[0]],
        [w_up[0], w_out[0], w_down[0]])
    win = g_in.reshape(IN_W, D)
    wpool = g_pool.reshape(N_CHIPS, 4, 64, POOL_GROUP).transpose(1, 0, 2, 3).reshape(4, POOL_GROUP, POOL_GROUP)
    late = _LateWeights(own_late)
    early = _GradReduce("early", place, ["w_down", "w_up"], [352, 512])
    rest = _GradReduce("rest", place, ["w_in", "w_out", "w_pool"], [544, 128, 128])

    def two_d(a):
        return a.reshape(-1, a.shape[-1])

    def update(nm, w, g, m, v, tr, deps=()):
        res = _adamw(two_d(w), g, two_d(m), two_d(v), tr, "adamw_" + nm, deps=deps)
        return [r.reshape(w.shape) for r in res]

    grad_x, d_attn_norm, _ = _forward_backward(
        xs, pos, tgt, win, wpool, g_cw, attn_norm, b_gate, pool_scale, q_norm, k_norm, sinks, ffn_norm, conv_b,
        late, early, rest, dev_arr)

    attn_stack = _into_slice(d_attn_norm, dev_arr, N_DEV, 1, F32, "own_attn_norm")
    g_wdown, g_wup = early.last(grad_x)
    big_out = {"w_up": update("w_up", w_up, g_wup, m_w_up, v_w_up, 256)}
    big_out["w_down"] = update("w_down", w_down, g_wdown, m_w_down, v_w_down, 352, deps=(big_out["w_up"][1],))
    token = rest.third(big_out["w_down"][1], small=attn_stack)
    g_win_t, g_wout, g_wpool = rest.last(token[0])
    small_out, g_convw_all, loss = _small_update(rest.small_chips, rest.small_swap, {
        "attn_norm": (attn_norm, m_attn_norm, v_attn_norm), "b_gate": (b_gate, m_b_gate, v_b_gate),
        "pool_scale": (pool_scale, m_pool_scale, v_pool_scale), "q_norm": (q_norm, m_q_norm, v_q_norm),
        "k_norm": (k_norm, m_k_norm, v_k_norm), "sinks": (sinks, m_sinks, v_sinks),
        "ffn_norm": (ffn_norm, m_ffn_norm, v_ffn_norm), "conv_b": (conv_b, m_conv_b, v_conv_b)})
    g_convw = lax.dynamic_slice_in_dim(g_convw_all, chip * UP_SHARD, UP_SHARD, axis=1)
    small_out["conv_w"] = update("conv_w", conv_w, g_convw, m_conv_w, v_conv_w, 3)
    flip = lambda a: jnp.swapaxes(a[0], 0, 1)
    res = _adamw(flip(w_in), g_win_t, flip(m_w_in), flip(v_w_in), 544, "adamw_w_in")
    big_out["w_in"] = [jnp.swapaxes(r, 0, 1)[None] for r in res]
    big_out["w_out"] = update("w_out", w_out, g_wout, m_w_out, v_w_out, 256)
    big_out["w_pool"] = update("w_pool", w_pool, g_wpool, m_w_pool, v_w_pool, 256)

    order = ["attn_norm", "w_in", "b_gate", "w_pool", "pool_scale", "q_norm", "k_norm", "sinks", "w_out",
             "ffn_norm", "w_up", "conv_w", "conv_b", "w_down"]
    allout = {**big_out, **small_out}
    outs = [loss.reshape(()), grad_x[None]]
    for k in range(4):
        outs += [allout[nm][k] for nm in order]
    return tuple(outs)
```

```python
import functools

import jax
import jax.numpy as jnp
import numpy as np
from jax import lax
from jax.experimental import pallas as pl
from jax.experimental.pallas import tpu as pltpu

D = 1024
D_FF = 2816
HEAD = 64
N_Q = 16
N_KV = 2
GQA = 8
BLK = 128
ROPE_DIM = 16
ROPE_THETA = 500000.0
POOL_GROUP = 256
Q_W = 1024
KV_W = 128
QKV_W = Q_W + 2 * KV_W
IN_W = 4352
UP_SHARD = 1408
EPS = 1e-6
N_CHIPS = 4
N_DEV = 8

LR = 0.001
B1 = 0.9
B2 = 0.999
ADAM_EPS = 1e-08
WD = 0.01
STEP = 10

BF = jnp.bfloat16
F32 = jnp.float32
MESH = pl.DeviceIdType.MESH
VMEM_LIMIT_MB = 56


def _cp(sem, vmem_mb=VMEM_LIMIT_MB):
    return pltpu.CompilerParams(dimension_semantics=sem, vmem_limit_bytes=vmem_mb << 20)


def _full(shape):
    nd = len(shape)
    return pl.BlockSpec(shape, lambda *_: (0,) * nd)


def _sds(shape, dtype):
    return jax.ShapeDtypeStruct(shape, dtype)


def _after(body, n_in, deps):
    nd = len(deps)
    if nd == 0:
        return body

    def ordered(*refs):
        return body(*refs[:n_in], *refs[n_in + nd:])

    return ordered


def _any_specs(deps):
    return [pl.BlockSpec(memory_space=pl.ANY)] * len(deps)


def _nt(a, b):
    return lax.dot_general(a, b, (((1,), (1,)), ((), ())), preferred_element_type=F32)


def _tn(a, b):
    return lax.dot_general(a, b, (((0,), (0,)), ((), ())), preferred_element_type=F32)


def _mm(a, b):
    return jnp.dot(a, b, preferred_element_type=F32)


def _head_sum(v, bd):
    return _mm(v.astype(BF), bd)


def _rope_tables(pos_ref, invf_ref):
    ang = pos_ref[...].astype(F32) * invf_ref[...]
    cos = jnp.cos(ang)
    sin = jnp.sin(ang)
    lane = lax.broadcasted_iota(jnp.int32, (1, 2 * HEAD), 1) % HEAD
    sa = jnp.where(lane < ROPE_DIM // 2, -sin, 0.0)
    sb = jnp.where(lane < ROPE_DIM // 2, 0.0, jnp.where(lane < ROPE_DIM, sin, 0.0))
    return cos, sa, sb


def _tile_lanes(t, reps):
    return t if reps == 1 else jnp.tile(t, (1, reps))


def _rope(v, cos, sa, sb):
    w = v.shape[1]
    reps = w // (2 * HEAD)
    half = ROPE_DIM // 2
    return (v * _tile_lanes(cos, reps) + pltpu.roll(v, w - half, 1) * _tile_lanes(sa, reps)
            + pltpu.roll(v, half, 1) * _tile_lanes(sb, reps))


def _rope_t(dy, cos, sa, sb):
    w = dy.shape[1]
    reps = w // (2 * HEAD)
    half = ROPE_DIM // 2
    return (dy * _tile_lanes(cos, reps) + pltpu.roll(dy * _tile_lanes(sa, reps), half, 1)
            + pltpu.roll(dy * _tile_lanes(sb, reps), w - half, 1))


def _attn_in_proj(x, g1, w_in, b_gate, tm, deps=()):
    s = x.shape[0]

    def body(x_ref, g_ref, w_ref, b_ref, h_ref, u_ref, qkv_ref, gate_ref):
        xv = x_ref[...]
        r = lax.rsqrt(jnp.mean(xv * xv, axis=-1, keepdims=True) + EPS)
        h = (xv * r * g_ref[...]).astype(BF)
        h_ref[...] = h
        u_ref[...] = _nt(h, w_ref[0:D, :])
        qkv_ref[...] = _nt(h, w_ref[D:D + QKV_W, :])
        gate_ref[...] = jax.nn.sigmoid(_nt(h, w_ref[D + QKV_W:IN_W, :]) + b_ref[...]).astype(BF)

    row = lambda w: pl.BlockSpec((tm, w), lambda i: (i, 0))
    return pl.pallas_call(
        _after(body, 4, deps), grid=(s // tm,),
        in_specs=[row(D), _full((1, D)), _full((IN_W, D)), _full((1, 2 * D))] + _any_specs(deps),
        out_specs=[row(D), row(D), row(QKV_W), row(2 * D)],
        out_shape=[_sds((s, D), BF), _sds((s, D), F32), _sds((s, QKV_W), F32), _sds((s, 2 * D), BF)],
        compiler_params=_cp(("parallel",)), name="attn_in_proj")(x, g1, w_in, b_gate, *deps)


def _qk_prep(qkv, pos, wq, wk, invf, bd, tm):
    s = qkv.shape[0]

    def body(qkv_ref, pos_ref, wq_ref, wk_ref, invf_ref, bd_ref, qh_ref, kh_ref, vh_ref):
        cos, sa, sb = _rope_tables(pos_ref, invf_ref)
        q = qkv_ref[:, 0:Q_W]
        k = qkv_ref[:, Q_W:Q_W + KV_W]
        v = qkv_ref[:, Q_W + KV_W:QKV_W]
        rq = lax.rsqrt(_head_sum(q * q, bd_ref[...]) * (1.0 / HEAD) + EPS)
        qr = _rope(q * rq * wq_ref[...], cos, sa, sb) * (HEAD ** -0.5)
        rk = lax.rsqrt(_head_sum(k * k, bd_ref[0:KV_W, 0:KV_W]) * (1.0 / HEAD) + EPS)
        kr = _rope(k * rk * wk_ref[...], cos, sa, sb)
        for h in range(N_Q):
            qh_ref[h] = qr[:, HEAD * h:HEAD * (h + 1)].astype(BF)
        for h in range(N_KV):
            kh_ref[h] = kr[:, HEAD * h:HEAD * (h + 1)].astype(BF)
            vh_ref[h] = v[:, HEAD * h:HEAD * (h + 1)].astype(BF)

    heads = lambda n: pl.BlockSpec((n, tm, HEAD), lambda i: (0, i, 0))
    return pl.pallas_call(
        body, grid=(s // tm,),
        in_specs=[pl.BlockSpec((tm, QKV_W), lambda i: (i, 0)), pl.BlockSpec((tm, 1), lambda i: (i, 0)),
                  _full((1, Q_W)), _full((1, KV_W)), _full((1, 2 * HEAD)), _full((Q_W, Q_W))],
        out_specs=[heads(N_Q), heads(N_KV), heads(N_KV)],
        out_shape=[_sds((N_Q, s, HEAD), BF), _sds((N_KV, s, HEAD), BF), _sds((N_KV, s, HEAD), BF)],
        compiler_params=_cp(("parallel",)), name="qk_prep")(qkv, pos, wq, wk, invf, bd)


def _sink_column(sink_ref, kh):
    row_g = lax.broadcasted_iota(jnp.int32, (GQA * BLK, 1), 0) // BLK
    col = jnp.zeros((GQA * BLK, 1), F32)
    for g in range(GQA):
        col = jnp.where(row_g == g, sink_ref[kh * GQA + g], col)
    return col


def _fold_band(band, lower, first=None):
    prev, cur = band[:, 0:BLK], band[:, BLK:2 * BLK]
    if first is not None:
        prev = jnp.where(first, -jnp.inf, prev)
    return jnp.where(lower, cur, prev)


def _unfold_band(x, lower):
    return jnp.concatenate([jnp.where(lower, 0.0, x), jnp.where(lower, x, 0.0)], axis=1)


def _attn_probs(q, k, n, sink_col):
    sc = _nt(q, k)
    qi = lax.broadcasted_iota(jnp.int32, (sc.shape[0], BLK), 0) % BLK
    ki = lax.broadcasted_iota(jnp.int32, (sc.shape[0], BLK), 1)
    lower = ki <= qi
    sc = _fold_band(sc, lower, first=n == 0)
    m = jnp.maximum(jnp.max(sc, axis=-1, keepdims=True), sink_col)
    p = jnp.exp(sc - m)
    es = jnp.exp(sink_col - m)
    inv = 1.0 / (jnp.sum(p, axis=-1, keepdims=True) + es)
    return p * inv, es * inv, lower


def _attn_fwd(qh, kh, vh, sinks, deps=()):
    s = qh.shape[1]
    assert s % (2 * BLK) == 0

    def body(sink_ref, q_ref, kp_ref, kc_ref, vp_ref, vc_ref, o_ref):
        i = pl.program_id(0)
        for sub in range(2):
            for khd in range(N_KV):
                q = q_ref[khd * GQA:(khd + 1) * GQA, sub * BLK:(sub + 1) * BLK].reshape(GQA * BLK, HEAD)
                if sub == 0:
                    k = jnp.concatenate([kp_ref[khd], kc_ref[khd, 0:BLK]], axis=0)
                    v = jnp.concatenate([vp_ref[khd], vc_ref[khd, 0:BLK]], axis=0)
                else:
                    k, v = kc_ref[khd], vc_ref[khd]
                probs, _, lower = _attn_probs(q, k, 2 * i + sub, _sink_column(sink_ref, khd))
                o = _mm(_unfold_band(probs, lower).astype(BF), v)
                for j in range(GQA // 2):
                    c0 = khd * GQA * HEAD + 2 * HEAD * j
                    o_ref[sub * BLK:(sub + 1) * BLK, c0:c0 + 2 * HEAD] = jnp.concatenate(
                        [o[2 * j * BLK:(2 * j + 1) * BLK], o[(2 * j + 1) * BLK:(2 * j + 2) * BLK]],
                        axis=1).astype(BF)

    prev = pl.BlockSpec((N_KV, BLK, HEAD), lambda i: (0, jnp.maximum(2 * i - 1, 0), 0))
    cur = pl.BlockSpec((N_KV, 2 * BLK, HEAD), lambda i: (0, i, 0))
    return pl.pallas_call(
        _after(body, 6, deps), grid=(s // (2 * BLK),),
        in_specs=[pl.BlockSpec(memory_space=pltpu.SMEM),
                  pl.BlockSpec((N_Q, 2 * BLK, HEAD), lambda i: (0, i, 0)), prev, cur, prev, cur] + _any_specs(deps),
        out_specs=pl.BlockSpec((2 * BLK, Q_W), lambda i: (i, 0)),
        out_shape=_sds((s, Q_W), BF),
        compiler_params=_cp(("parallel",)), name="attn_fwd")(sinks, qh, kh, kh, vh, vh, *deps)


def _pool_fwd(u, w_pool, pool_scale, ts, deps=()):
    s = u.shape[0]
    halo = 16

    def body(u_ref, wp_ref, ps_ref, a_ref, pooled_ref, prev):
        g = pl.program_id(0)
        i = pl.program_id(1)

        @pl.when(i == 0)
        def _():
            prev[...] = jnp.zeros_like(prev)

        cur = u_ref[...]
        ext = jnp.concatenate([prev[...], cur], axis=0)
        t = (i * ts + lax.broadcasted_iota(jnp.int32, (ts, 1), 0)).astype(F32)
        for gi in range(4):
            @pl.when(g == gi)
            def _(gi=gi):
                w = 2 << gi
                acc, span = ext, 1
                while span < w:
                    acc = acc + pltpu.roll(acc, span, 0)
                    span *= 2
                inv = 1.0 / jnp.minimum(t + 1.0, float(w))
                pooled = (acc[halo:halo + ts] * inv - cur).astype(BF)
                pooled_ref[...] = pooled
                a_ref[...] = (_mm(pooled, wp_ref[0]) * ps_ref[...]).astype(BF)

        prev[...] = cur[ts - halo:ts]

    col = pl.BlockSpec((ts, POOL_GROUP), lambda g, i: (i, g))
    return pl.pallas_call(
        _after(body, 3, deps), grid=(4, s // ts),
        in_specs=[col, pl.BlockSpec((1, POOL_GROUP, POOL_GROUP), lambda g, i: (g, 0, 0)),
                  pl.BlockSpec((1, POOL_GROUP), lambda g, i: (0, g))] + _any_specs(deps),
        out_specs=[col, col],
        out_shape=[_sds((s, D), BF), _sds((s, D), BF)],
        scratch_shapes=[pltpu.VMEM((halo, POOL_GROUP), F32)],
        compiler_params=_cp(("parallel", "arbitrary")), name="pool_fwd")(u, w_pool, pool_scale, *deps)


def _mix_out_proj(a, b, gates, x, w_out, g2, tm):
    s = x.shape[0]

    def body(a_ref, b_ref, gate_ref, x_ref, w_ref, g_ref, mix_ref, y_ref, h_ref):
        mix = (gate_ref[:, 0:D].astype(F32) * a_ref[...].astype(F32)
               + gate_ref[:, D:2 * D].astype(F32) * b_ref[...].astype(F32)).astype(BF)
        mix_ref[...] = mix
        y = x_ref[...] + _mm(mix, w_ref[...])
        y_ref[...] = y
        r = lax.rsqrt(jnp.mean(y * y, axis=-1, keepdims=True) + EPS)
        h_ref[...] = (y * r * g_ref[...]).astype(BF)

    row = lambda w: pl.BlockSpec((tm, w), lambda i: (i, 0))
    return pl.pallas_call(
        body, grid=(s // tm,),
        in_specs=[row(D), row(D), row(2 * D), row(D), _full((D, D)), _full((1, D))],
        out_specs=[row(D), row(D), row(D)],
        out_shape=[_sds((s, D), BF), _sds((s, D), F32), _sds((s, D), BF)],
        compiler_params=_cp(("parallel",)), name="mix_out_proj")(a, b, gates, x, w_out, g2)


def _ffn_up(h2, w_up, conv_w, conv_b, tm):
    s = h2.shape[0]

    def body(h_ref, wg_ref, wv_ref, cwg_ref, cwv_ref, cbg_ref, cbv_ref,
             preg_ref, prev_ref, upg_ref, upv_ref, act_ref, halog, halov):
        i = pl.program_id(1)

        @pl.when(i == 0)
        def _():
            halog[...] = jnp.zeros_like(halog)
            halov[...] = jnp.zeros_like(halov)

        h = h_ref[...]

        def conv_half(w_ref, cw_ref, cb_ref, halo, pre_ref, up_ref):
            pre = _mm(h, w_ref[0])
            pre_ref[...] = pre.astype(BF)
            ext = jnp.concatenate([halo[...], pre], axis=0)
            cw = cw_ref[0]
            up = cb_ref[...] + cw[0:1] * pltpu.roll(ext, 2, 0)[8:8 + tm]
            up = up + cw[1:2] * pltpu.roll(ext, 1, 0)[8:8 + tm]
            up = up + cw[2:3] * pre
            halo[...] = pre[tm - 8:tm]
            up_ref[...] = up.astype(BF)
            return up

        gate = conv_half(wg_ref, cwg_ref, cbg_ref, halog, preg_ref, upg_ref)
        val = conv_half(wv_ref, cwv_ref, cbv_ref, halov, prev_ref, upv_ref)
        act_ref[...] = (gate * jax.nn.sigmoid(gate) * val).astype(BF)

    tile = pl.BlockSpec((tm, UP_SHARD), lambda j, i: (i, j))
    wspec = lambda off: pl.BlockSpec((1, D, UP_SHARD), lambda j, i: (j + off, 0, 0))
    cwspec = lambda off: pl.BlockSpec((1, 3, UP_SHARD), lambda j, i: (j + off, 0, 0))
    cbspec = lambda off: pl.BlockSpec((1, UP_SHARD), lambda j, i: (0, j + off))
    half = _sds((s, D_FF), BF)
    return pl.pallas_call(
        body, grid=(2, s // tm),
        in_specs=[pl.BlockSpec((tm, D), lambda j, i: (i, 0)), wspec(0), wspec(2), cwspec(0), cwspec(2),
                  cbspec(0), cbspec(2)],
        out_specs=[tile] * 5, out_shape=[half] * 5,
        scratch_shapes=[pltpu.VMEM((8, UP_SHARD), F32), pltpu.VMEM((8, UP_SHARD), F32)],
        compiler_params=_cp(("parallel", "arbitrary")), name="ffn_up")(
            h2, w_up, w_up, conv_w, conv_w, conv_b, conv_b)


def _ffn_down_loss(act, w_down, y1, tgt, tm):
    s = y1.shape[0]

    def body(act_ref, w_ref, y_ref, t_ref, dy_ref, dyb_ref, loss_ref):
        @pl.when(pl.program_id(0) == 0)
        def _():
            loss_ref[...] = jnp.zeros_like(loss_ref)

        e = y_ref[...] + _mm(act_ref[...], w_ref[...]) - t_ref[...]
        dy = e * (1.0 / D)
        dy_ref[...] = dy
        dyb_ref[...] = dy.astype(BF)
        e2 = (e * e).reshape(tm // 8, 8, D).sum(axis=0)
        part = e2[:, 0:128]
        for j in range(1, D // 128):
            part = part + e2[:, 128 * j:128 * (j + 1)]
        loss_ref[...] += part

    row = lambda w: pl.BlockSpec((tm, w), lambda i: (i, 0))
    return pl.pallas_call(
        body, grid=(s // tm,),
        in_specs=[row(D_FF), _full((D_FF, D)), row(D), row(D)],
        out_specs=[row(D), row(D), _full((8, 128))],
        out_shape=[_sds((s, D), F32), _sds((s, D), BF), _sds((8, 128), F32)],
        compiler_params=_cp(("arbitrary",)), name="ffn_down_loss")(act, w_down, y1, tgt)


def _grad_matmul(a, b, tn, tk, name, lead=None, prev=None, lead_off=0):
    s, m = a.shape
    n = b.shape[1]
    nj = n // tn

    def body(*refs):
        a_ref, b_ref = refs[0], refs[1]
        o_ref = refs[-1]
        acc = _tn(a_ref[...], b_ref[...])
        acc = acc if lead is None else acc[None]

        @pl.when(pl.program_id(1) == 0)
        def _():
            o_ref[...] = acc

        @pl.when(pl.program_id(1) > 0)
        def _():
            o_ref[...] += acc

    in_specs = [pl.BlockSpec((tk, m), lambda j, k: (k, 0)), pl.BlockSpec((tk, tn), lambda j, k: (k, j))]
    args = [a, b]
    aliases = {}
    if lead is None:
        out_spec = pl.BlockSpec((m, tn), lambda j, k: (0, j))
        out_shape = _sds((m, n), F32)
    else:
        out_spec = pl.BlockSpec((1, m, tn), lambda j, k: (j + lead_off, 0, 0))
        out_shape = _sds((lead, m, tn), F32)
        if prev is not None:
            in_specs.append(pl.BlockSpec(memory_space=pl.ANY))
            args.append(prev)
            aliases = {2: 0}
    return pl.pallas_call(
        body, grid=(nj, s // tk), in_specs=in_specs, out_specs=out_spec, out_shape=out_shape,
        input_output_aliases=aliases,
        compiler_params=_cp(("parallel", "arbitrary")), name=name)(*args)


def _ffn_act_bwd(dyb, w_down, up_g, up_v, pre_g, pre_v, conv_w, tm):
    s = dyb.shape[0]
    nt = s // tm

    def body(dy_ref, wd_ref, upg_ref, upv_ref, preg_ref, prev_ref, cwg_ref, cwv_ref,
             dpg_ref, dpv_ref, dcwg_ref, dcwv_ref, dcbg_ref, dcbv_ref, nxg, nxv):
        i = pl.program_id(1)

        @pl.when(i == 0)
        def _():
            nxg[...] = jnp.zeros_like(nxg)
            nxv[...] = jnp.zeros_like(nxv)
            dcwg_ref[...] = jnp.zeros_like(dcwg_ref)
            dcwv_ref[...] = jnp.zeros_like(dcwv_ref)
            dcbg_ref[...] = jnp.zeros_like(dcbg_ref)
            dcbv_ref[...] = jnp.zeros_like(dcbv_ref)

        dact = _nt(dy_ref[...], wd_ref[...])
        g = upg_ref[...].astype(F32)
        v = upv_ref[...].astype(F32)
        sg = jax.nn.sigmoid(g)
        d_v = dact * (g * sg)
        d_g = dact * v * (sg * (1.0 + g * (1.0 - sg)))

        def conv_bwd(d_up, nx, pre_ref, cw_ref, dp_ref, dcw_ref, dcb_ref):
            ext = jnp.concatenate([d_up, nx[...]], axis=0)
            s1 = pltpu.roll(ext, tm + 8 - 1, 0)[0:tm]
            s2 = pltpu.roll(ext, tm + 8 - 2, 0)[0:tm]
            cw = cw_ref[0]
            dp_ref[...] = (cw[2:3] * d_up + cw[1:2] * s1 + cw[0:1] * s2).astype(BF)
            nx[...] = d_up[0:8]
            pre = pre_ref[...].astype(F32)
            dcw_ref[0, 0:1, :] += jnp.sum(s2 * pre, axis=0, keepdims=True)
            dcw_ref[0, 1:2, :] += jnp.sum(s1 * pre, axis=0, keepdims=True)
            dcw_ref[0, 2:3, :] += jnp.sum(d_up * pre, axis=0, keepdims=True)
            dcb_ref[...] += jnp.sum(d_up, axis=0, keepdims=True)

        conv_bwd(d_g, nxg, preg_ref, cwg_ref, dpg_ref, dcwg_ref, dcbg_ref)
        conv_bwd(d_v, nxv, prev_ref, cwv_ref, dpv_ref, dcwv_ref, dcbv_ref)

    tile = pl.BlockSpec((tm, UP_SHARD), lambda j, i: (nt - 1 - i, j))
    cwspec = lambda off: pl.BlockSpec((1, 3, UP_SHARD), lambda j, i: (j + off, 0, 0))
    acc_cw = pl.BlockSpec((1, 3, UP_SHARD), lambda j, i: (j, 0, 0))
    acc_cb = pl.BlockSpec((1, UP_SHARD), lambda j, i: (0, j))
    buf = pltpu.VMEM((8, UP_SHARD), F32)
    return pl.pallas_call(
        body, grid=(2, nt),
        in_specs=[pl.BlockSpec((tm, D), lambda j, i: (nt - 1 - i, 0)),
                  pl.BlockSpec((UP_SHARD, D), lambda j, i: (j, 0)),
                  tile, tile, tile, tile, cwspec(0), cwspec(2)],
        out_specs=[tile, tile, acc_cw, acc_cw, acc_cb, acc_cb],
        out_shape=[_sds((s, D_FF), BF), _sds((s, D_FF), BF), _sds((2, 3, UP_SHARD), F32),
                   _sds((2, 3, UP_SHARD), F32), _sds((1, D_FF), F32), _sds((1, D_FF), F32)],
        scratch_shapes=[buf, buf],
        compiler_params=_cp(("parallel", "arbitrary")), name="ffn_act_bwd")(
            dyb, w_down, up_g, up_v, pre_g, pre_v, conv_w, conv_w)


def _rms_bwd(dh, y, g):
    r = lax.rsqrt(jnp.mean(y * y, axis=-1, keepdims=True) + EPS)
    n = y * r
    dn = dh * g
    return r * (dn - n * jnp.mean(dn * n, axis=-1, keepdims=True)), dh * n


def _ffn_up_bwd(dp_g, dp_v, w_up, y1, dy2, g2, tm, deps=()):
    s = y1.shape[0]

    def body(dg_ref, dv_ref, w_ref, y_ref, dy2_ref, g_ref, dy1_ref, dgn_ref):
        @pl.when(pl.program_id(0) == 0)
        def _():
            dgn_ref[...] = jnp.zeros_like(dgn_ref)

        dh = _nt(dg_ref[:, 0:UP_SHARD], w_ref[0])
        dh = dh + _nt(dg_ref[:, UP_SHARD:D_FF], w_ref[1])
        dh = dh + _nt(dv_ref[:, 0:UP_SHARD], w_ref[2])
        dh = dh + _nt(dv_ref[:, UP_SHARD:D_FF], w_ref[3])
        dy, dgn = _rms_bwd(dh, y_ref[...], g_ref[...])
        dy1_ref[...] = dy2_ref[...] + dy
        dgn_ref[...] += jnp.sum(dgn, axis=0, keepdims=True)

    row = lambda w: pl.BlockSpec((tm, w), lambda i: (i, 0))
    return pl.pallas_call(
        _after(body, 6, deps), grid=(s // tm,),
        in_specs=[row(D_FF), row(D_FF), _full((4, D, UP_SHARD)), row(D), row(D), _full((1, D))] + _any_specs(deps),
        out_specs=[row(D), _full((1, D))],
        out_shape=[_sds((s, D), F32), _sds((1, D), F32)],
        compiler_params=_cp(("arbitrary",)), name="ffn_up_bwd")(dp_g, dp_v, w_up, y1, dy2, g2, *deps)


def _out_proj_bwd(dy1, w_out, a, b, gates, mix, tm, deps=()):
    s = dy1.shape[0]

    def body(dy_ref, w_ref, a_ref, b_ref, gate_ref, mix_ref, da_ref, db_ref, dzg_ref, dbg_ref, dw_ref):
        @pl.when(pl.program_id(0) == 0)
        def _():
            dbg_ref[...] = jnp.zeros_like(dbg_ref)
            dw_ref[...] = jnp.zeros_like(dw_ref)

        dyb = dy_ref[...].astype(BF)
        dmix = _nt(dyb, w_ref[...])
        gp = gate_ref[:, 0:D].astype(F32)
        ga = gate_ref[:, D:2 * D].astype(F32)
        da_ref[...] = (dmix * gp).astype(BF)
        db_ref[...] = (dmix * ga).astype(BF)
        dzp = dmix * a_ref[...].astype(F32) * (gp * (1.0 - gp))
        dza = dmix * b_ref[...].astype(F32) * (ga * (1.0 - ga))
        dzg_ref[:, 0:D] = dzp.astype(BF)
        dzg_ref[:, D:2 * D] = dza.astype(BF)
        dbg_ref[:, 0:D] += jnp.sum(dzp, axis=0, keepdims=True)
        dbg_ref[:, D:2 * D] += jnp.sum(dza, axis=0, keepdims=True)
        dw_ref[...] += _tn(mix_ref[...], dyb)

    row = lambda w: pl.BlockSpec((tm, w), lambda i: (i, 0))
    return pl.pallas_call(
        _after(body, 6, deps), grid=(s // tm,),
        in_specs=[row(D), _full((D, D)), row(D), row(D), row(2 * D), row(D)] + _any_specs(deps),
        out_specs=[row(D), row(D), row(2 * D), _full((1, 2 * D)), _full((D, D))],
        out_shape=[_sds((s, D), BF), _sds((s, D), BF), _sds((s, 2 * D), BF), _sds((1, 2 * D), F32),
                   _sds((D, D), F32)],
        compiler_params=_cp(("arbitrary",)), name="out_proj_bwd")(dy1, w_out, a, b, gates, mix, *deps)


def _pool_bwd(da, pooled, w_pool, pool_scale, h1, ts):
    s = da.shape[0]
    nt = s // ts
    halo = 16

    def body(da_ref, pooled_ref, wp_ref, ps_ref, h_ref, du_ref, dwp_ref, dps_ref, dwi_ref, nxt):
        g = pl.program_id(0)
        i = pl.program_id(1)
        ti = nt - 1 - i

        @pl.when(i == 0)
        def _():
            nxt[...] = jnp.zeros_like(nxt)
            dwp_ref[...] = jnp.zeros_like(dwp_ref)
            dps_ref[...] = jnp.zeros_like(dps_ref)
            dwi_ref[...] = jnp.zeros_like(dwi_ref)

        pooled = pooled_ref[...]
        dav = da_ref[...].astype(F32)
        dps_ref[...] += jnp.sum(dav * _mm(pooled, wp_ref[0]), axis=0, keepdims=True)
        dm = (dav * ps_ref[...]).astype(BF)
        dwp_ref[0] += _tn(pooled, dm)
        dpool = _nt(dm, wp_ref[0])
        t = (ti * ts + lax.broadcasted_iota(jnp.int32, (ts, 1), 0)).astype(F32)
        for gi in range(4):
            @pl.when(g == gi)
            def _(gi=gi):
                w = 2 << gi
                e = dpool * (1.0 / jnp.minimum(t + 1.0, float(w)))
                acc, span = jnp.concatenate([e, nxt[...]], axis=0), 1
                while span < w:
                    acc = acc + pltpu.roll(acc, ts + halo - span, 0)
                    span *= 2
                du = (acc[0:ts] - dpool).astype(BF)
                du_ref[...] = du
                dwi_ref[...] += _tn(du, h_ref[...])
                nxt[...] = e[0:halo]

    col = pl.BlockSpec((ts, POOL_GROUP), lambda g, i: (nt - 1 - i, g))
    return pl.pallas_call(
        body, grid=(4, nt),
        in_specs=[col, col, pl.BlockSpec((1, POOL_GROUP, POOL_GROUP), lambda g, i: (g, 0, 0)),
                  pl.BlockSpec((1, POOL_GROUP), lambda g, i: (0, g)),
                  pl.BlockSpec((ts, D), lambda g, i: (nt - 1 - i, 0))],
        out_specs=[col, pl.BlockSpec((1, POOL_GROUP, POOL_GROUP), lambda g, i: (g, 0, 0)),
                   pl.BlockSpec((1, POOL_GROUP), lambda g, i: (0, g)),
                   pl.BlockSpec((POOL_GROUP, D), lambda g, i: (g, 0))],
        out_shape=[_sds((s, D), BF), _sds((4, POOL_GROUP, POOL_GROUP), F32), _sds((1, D), F32), _sds((D, D), F32)],
        scratch_shapes=[pltpu.VMEM((halo, POOL_GROUP), F32)],
        compiler_params=_cp(("parallel", "arbitrary")), name="pool_bwd")(da, pooled, w_pool, pool_scale, h1)


def _attn_bwd(qh, kh, vh, sinks, db, deps=()):
    s = qh.shape[1]
    assert s % (2 * BLK) == 0
    pairs = s // (2 * BLK)

    def body(sink_ref, q_ref, kp_ref, kc_ref, vp_ref, vc_ref, do_ref,
             dq_ref, dke_ref, dko_ref, dve_ref, dvo_ref, dsink_ref, ck, cv):
        i = pl.program_id(0)

        @pl.when(i == 0)
        def _():
            ck[...] = jnp.zeros_like(ck)
            cv[...] = jnp.zeros_like(cv)
            dsink_ref[...] = jnp.zeros_like(dsink_ref)

        @pl.when(i < pairs)
        def _():
            for khd in range(N_KV):
                c0 = khd * GQA * HEAD
                band = []
                for sub in range(2):
                    q = q_ref[khd * GQA:(khd + 1) * GQA, sub * BLK:(sub + 1) * BLK].reshape(GQA * BLK, HEAD)
                    if sub == 0:
                        k = jnp.concatenate([kp_ref[khd], kc_ref[khd, 0:BLK]], axis=0)
                        v = jnp.concatenate([vp_ref[khd], vc_ref[khd, 0:BLK]], axis=0)
                    else:
                        k, v = kc_ref[khd], vc_ref[khd]
                    dov = do_ref[sub * BLK:(sub + 1) * BLK, :]
                    do = jnp.concatenate([dov[:, c0 + HEAD * g:c0 + HEAD * (g + 1)] for g in range(GQA)],
                                         axis=0).astype(BF)
                    probs, psink, lower = _attn_probs(q, k, 2 * i + sub, _sink_column(sink_ref, khd))
                    dp = _fold_band(_nt(do, v), lower)
                    delta = jnp.sum(probs * dp, axis=-1, keepdims=True)
                    ds = _unfold_band(probs * (dp - delta), lower).astype(BF)
                    dq_ref[khd * GQA:(khd + 1) * GQA, sub * BLK:(sub + 1) * BLK] = _mm(ds, k).reshape(
                        GQA, BLK, HEAD)
                    band.append((_tn(ds, q), _tn(_unfold_band(probs, lower).astype(BF), do)))
                    dsk = psink * delta
                    lane = lax.broadcasted_iota(jnp.int32, (1, 128), 1)
                    acc = jnp.zeros((1, 128), F32)
                    for g in range(GQA):
                        acc = acc - jnp.where(lane == khd * GQA + g,
                                              jnp.sum(dsk[g * BLK:(g + 1) * BLK], axis=0, keepdims=True), 0.0)
                    dsink_ref[...] += acc
                (dk0, dv0), (dk1, dv1) = band
                dko_ref[khd] = ck[khd] + dk0[0:BLK]
                dvo_ref[khd] = cv[khd] + dv0[0:BLK]
                dke_ref[khd] = dk0[BLK:2 * BLK] + dk1[0:BLK]
                dve_ref[khd] = dv0[BLK:2 * BLK] + dv1[0:BLK]
                ck[khd] = dk1[BLK:2 * BLK]
                cv[khd] = dv1[BLK:2 * BLK]

        @pl.when(i == pairs)
        def _():
            dko_ref[...] = ck[...]
            dvo_ref[...] = cv[...]

    last = pairs - 1
    at = lambda i: jnp.minimum(i, last)
    prev = pl.BlockSpec((N_KV, BLK, HEAD), lambda i: (0, jnp.maximum(2 * at(i) - 1, 0), 0))
    cur = pl.BlockSpec((N_KV, 2 * BLK, HEAD), lambda i: (0, at(i), 0))
    even = pl.BlockSpec((N_KV, BLK, HEAD), lambda i: (0, at(i), 0))
    odd = pl.BlockSpec((N_KV, BLK, HEAD), lambda i: (0, jnp.maximum(i - 1, 0), 0))
    halfkv = _sds((N_KV, s // 2, HEAD), F32)
    dq, dke, dko, dve, dvo, dsink = pl.pallas_call(
        _after(body, 7, deps), grid=(pairs + 1,),
        in_specs=[pl.BlockSpec(memory_space=pltpu.SMEM),
                  pl.BlockSpec((N_Q, 2 * BLK, HEAD), lambda i: (0, at(i), 0)),
                  prev, cur, prev, cur,
                  pl.BlockSpec((2 * BLK, Q_W), lambda i: (at(i), 0))] + _any_specs(deps),
        out_specs=[pl.BlockSpec((N_Q, 2 * BLK, HEAD), lambda i: (0, at(i), 0)), even, odd, even, odd,
                   _full((1, 128))],
        out_shape=[_sds((N_Q, s, HEAD), F32), halfkv, halfkv, halfkv, halfkv, _sds((1, 128), F32)],
        scratch_shapes=[pltpu.VMEM((N_KV, BLK, HEAD), F32), pltpu.VMEM((N_KV, BLK, HEAD), F32)],
        compiler_params=_cp(("arbitrary",)), name="attn_bwd")(sinks, qh, kh, kh, vh, vh, db, *deps)

    return dq, (dke, dko), (dve, dvo), dsink


def _qk_prep_bwd(dqh, dk_eo, dv_eo, qkv, pos, wq, wk, invf, bd, tm, deps=()):
    s = qkv.shape[0]
    assert tm % (2 * BLK) == 0

    def by_token(ev_ref, od_ref):
        cols = []
        for h in range(N_KV):
            rows = []
            for m in range(tm // (2 * BLK)):
                rows += [ev_ref[h, m * BLK:(m + 1) * BLK], od_ref[h, m * BLK:(m + 1) * BLK]]
            cols.append(jnp.concatenate(rows, axis=0))
        return jnp.concatenate(cols, axis=1)

    def fold_heads(row):
        out = row[:, 0:HEAD]
        for h in range(1, row.shape[1] // HEAD):
            out = out + row[:, HEAD * h:HEAD * (h + 1)]
        return out

    def body(dq_ref, dke_ref, dko_ref, dve_ref, dvo_ref, qkv_ref, pos_ref, wq_ref, wk_ref, invf_ref, bd_ref,
             dz_ref, dwq_ref, dwk_ref):
        @pl.when(pl.program_id(0) == 0)
        def _():
            dwq_ref[...] = jnp.zeros_like(dwq_ref)
            dwk_ref[...] = jnp.zeros_like(dwk_ref)

        cos, sa, sb = _rope_tables(pos_ref, invf_ref)

        def norm_rope_bwd(dy, xin, w, bdm):
            dn = _rope_t(dy, cos, sa, sb)
            r = lax.rsqrt(_head_sum(xin * xin, bdm) * (1.0 / HEAD) + EPS)
            nh = xin * r
            gw = dn * w
            dx = r * (gw - nh * (_head_sum(gw * nh, bdm) * (1.0 / HEAD)))
            return dx, fold_heads(jnp.sum(dn * nh, axis=0, keepdims=True))

        dq = jnp.concatenate([dq_ref[h] for h in range(N_Q)], axis=1) * (HEAD ** -0.5)
        dk = by_token(dke_ref, dko_ref)
        dxq, dwq = norm_rope_bwd(dq, qkv_ref[:, 0:Q_W], wq_ref[...], bd_ref[...])
        dxk, dwk = norm_rope_bwd(dk, qkv_ref[:, Q_W:Q_W + KV_W], wk_ref[...], bd_ref[0:KV_W, 0:KV_W])
        dz_ref[:, 0:Q_W] = dxq.astype(BF)
        dz_ref[:, Q_W:Q_W + KV_W] = dxk.astype(BF)
        dz_ref[:, Q_W + KV_W:QKV_W] = by_token(dve_ref, dvo_ref).astype(BF)
        dwq_ref[...] += dwq
        dwk_ref[...] += dwk

    heads = lambda n: pl.BlockSpec((n, tm, HEAD), lambda i: (0, i, 0))
    half = pl.BlockSpec((N_KV, tm // 2, HEAD), lambda i: (0, i, 0))
    return pl.pallas_call(
        _after(body, 11, deps), grid=(s // tm,),
        in_specs=[heads(N_Q), half, half, half, half, pl.BlockSpec((tm, QKV_W), lambda i: (i, 0)),
                  pl.BlockSpec((tm, 1), lambda i: (i, 0)), _full((1, Q_W)), _full((1, KV_W)),
                  _full((1, 2 * HEAD)), _full((Q_W, Q_W))] + _any_specs(deps),
        out_specs=[pl.BlockSpec((tm, QKV_W), lambda i: (i, 0)), _full((1, HEAD)), _full((1, HEAD))],
        out_shape=[_sds((s, QKV_W), BF), _sds((1, HEAD), F32), _sds((1, HEAD), F32)],
        compiler_params=_cp(("arbitrary",)), name="qk_prep_bwd")(
            dqh, *dk_eo, *dv_eo, qkv, pos, wq, wk, invf, bd, *deps)


def _in_proj_bwd(du, dzq, dzg, w_in, x, g1, dy1, tm, deps=()):
    s = x.shape[0]

    def body(du_ref, dzq_ref, dzg_ref, w_ref, x_ref, g_ref, dy_ref, gx_ref, dgn_ref):
        @pl.when(pl.program_id(0) == 0)
        def _():
            dgn_ref[...] = jnp.zeros_like(dgn_ref)

        dh = _mm(du_ref[...], w_ref[0:D, :])
        dh = dh + _mm(dzq_ref[...], w_ref[D:D + QKV_W, :])
        dh = dh + _mm(dzg_ref[...], w_ref[D + QKV_W:IN_W, :])
        dx, dgn = _rms_bwd(dh, x_ref[...], g_ref[...])
        gx_ref[...] = dy_ref[...] + dx
        dgn_ref[...] += jnp.sum(dgn, axis=0, keepdims=True)

    row = lambda w: pl.BlockSpec((tm, w), lambda i: (i, 0))
    return pl.pallas_call(
        _after(body, 7, deps), grid=(s // tm,),
        in_specs=[row(D), row(QKV_W), row(2 * D), _full((IN_W, D)), row(D), _full((1, D)), row(D)] + _any_specs(deps),
        out_specs=[row(D), _full((1, D))],
        out_shape=[_sds((s, D), F32), _sds((1, D), F32)],
        compiler_params=_cp(("arbitrary",)), name="in_proj_bwd")(du, dzq, dzg, w_in, x, g1, dy1, *deps)


def _adamw_step(w, g, m, v):
    mn = B1 * m + (1.0 - B1) * g
    vn = B2 * v + (1.0 - B2) * (g * g)
    m_hat = mn / (1.0 - B1 ** STEP)
    v_hat = vn / (1.0 - B2 ** STEP)
    return -LR * (m_hat / (jnp.sqrt(v_hat) + ADAM_EPS) + WD * w), mn, vn


SMALL_ROWS = 16
SMALL_COLS = D_FF
SMALL_AT = {"b_gate": (1, 2 * D), "pool_scale": (2, D), "q_norm": (3, HEAD),
            "k_norm": (4, HEAD), "sinks": (5, N_Q), "ffn_norm": (6, D)}
SMALL_LOSS_ROW = 0
SMALL_CONV_B_ROW = 7
SMALL_CONV_W_ROW = 9


def _pack_small(loss_acc, d_bgate, d_pscale, d_qn, d_kn, dsink, d_ffn_norm, dcb_g, dcb_v, dcw_g, dcw_v, dev):
    def body(k_ref, ls_ref, bg_ref, ps_ref, qn_ref, kn_ref, sk_ref, fn_ref, cbg_ref, cbv_ref, cwg_ref, cwv_ref,
             o_ref):
        o_ref[...] = jnp.zeros_like(o_ref)
        o_ref[0, SMALL_LOSS_ROW:SMALL_LOSS_ROW + 1, 0:128] = jnp.sum(ls_ref[...], axis=0, keepdims=True)
        for nm, ref in (("b_gate", bg_ref), ("pool_scale", ps_ref), ("q_norm", qn_ref),
                        ("k_norm", kn_ref), ("ffn_norm", fn_ref)):
            row, n = SMALL_AT[nm]
            o_ref[0, row:row + 1, 0:n] = ref[...]
        row, _ = SMALL_AT["sinks"]
        o_ref[0, row:row + 1, 0:128] = sk_ref[...]
        o_ref[0, SMALL_CONV_B_ROW:SMALL_CONV_B_ROW + 1, :] = cbg_ref[...]
        o_ref[0, SMALL_CONV_B_ROW + 1:SMALL_CONV_B_ROW + 2, :] = cbv_ref[...]
        for k in range(3):
            row = SMALL_CONV_W_ROW + 2 * k
            for half in range(2):
                o_ref[0, row:row + 1, half * UP_SHARD:(half + 1) * UP_SHARD] = cwg_ref[half, k:k + 1, :]
                o_ref[0, row + 1:row + 2, half * UP_SHARD:(half + 1) * UP_SHARD] = cwv_ref[half, k:k + 1, :]

    args = [loss_acc, d_bgate, d_pscale, d_qn, d_kn, dsink, d_ffn_norm, dcb_g, dcb_v, dcw_g, dcw_v]
    grid_spec = pltpu.PrefetchScalarGridSpec(
        num_scalar_prefetch=1, grid=(1,),
        in_specs=[pl.BlockSpec(a.shape, functools.partial(lambda nd, i, k: (0,) * nd, a.ndim)) for a in args],
        out_specs=pl.BlockSpec((1, SMALL_ROWS, SMALL_COLS), lambda i, k: (k[0], 0, 0)))
    return pl.pallas_call(body, grid_spec=grid_spec, out_shape=_sds((N_DEV, SMALL_ROWS, SMALL_COLS), F32),
                          name="pack_small")(dev, *args)


def _small_update(stack, attn_stack, params):
    names = list(params)

    def body(*refs):
        s_ref, a_ref = refs[0], refs[1]
        ins = refs[2:2 + 3 * len(names)]
        outs = refs[2 + 3 * len(names):]
        tot, tot_a = s_ref[0], a_ref[0]
        for d in range(1, N_DEV):
            tot = tot + s_ref[d]
            tot_a = tot_a + a_ref[d]
        for i, nm in enumerate(names):
            if nm == "attn_norm":
                g = tot_a
            elif nm == "conv_b":
                g = jnp.concatenate([tot[SMALL_CONV_B_ROW:SMALL_CONV_B_ROW + 1, :],
                                     tot[SMALL_CONV_B_ROW + 1:SMALL_CONV_B_ROW + 2, :]], axis=1)
            else:
                row, n = SMALL_AT[nm]
                g = tot[row:row + 1, 0:n]
            delta, mn, vn = _adamw_step(ins[3 * i][...], g, ins[3 * i + 1][...], ins[3 * i + 2][...])
            outs[4 * i][...] = g
            outs[4 * i + 1][...] = delta
            outs[4 * i + 2][...] = mn
            outs[4 * i + 3][...] = vn
        for k in range(3):
            row = SMALL_CONV_W_ROW + 2 * k
            outs[-2][k:k + 1, 0:D_FF] = tot[row:row + 1, :]
            outs[-2][k:k + 1, D_FF:2 * D_FF] = tot[row + 1:row + 2, :]
        outs[-1][...] = jnp.sum(tot[SMALL_LOSS_ROW:SMALL_LOSS_ROW + 1, 0:128], axis=1, keepdims=True) * (0.5 / D)

    flat = [a for nm in names for a in params[nm]]
    out_shape = ([_sds(params[nm][0].shape, F32) for nm in names for _ in range(4)]
                 + [_sds((3, 2 * D_FF), F32), _sds((1, 1), F32)])
    res = pl.pallas_call(body, out_shape=out_shape, name="small_update")(stack, attn_stack, *flat)
    return {nm: list(res[4 * i:4 * i + 4]) for i, nm in enumerate(names)}, res[-2], res[-1]


def _adamw(w, g, m, v, tr, name, deps=()):
    r, c = w.shape

    def body(w_ref, g_ref, m_ref, v_ref, go_ref, d_ref, mo_ref, vo_ref):
        gv = g_ref[...]
        go_ref[...] = gv
        d_ref[...], mo_ref[...], vo_ref[...] = _adamw_step(w_ref[...], gv, m_ref[...], v_ref[...])

    blk = pl.BlockSpec((tr, c), lambda i: (i, 0))
    return pl.pallas_call(
        _after(body, 4, deps), grid=(r // tr,), in_specs=[blk] * 4 + _any_specs(deps), out_specs=[blk] * 4,
        out_shape=[_sds((r, c), F32)] * 4, compiler_params=_cp(("parallel",)), name=name)(w, g, m, v, *deps)


def _place():
    x, y, c = lax.axis_index("x"), lax.axis_index("y"), lax.axis_index("c")
    chips = [(1 - x, y), (x, 1 - y), (1 - x, 1 - y)]
    return x, y, c, chips


def _rows(ref, lead, h, rh):
    sl = pl.ds(pl.multiple_of(h * rh, 16), rh)
    return ref.at[sl, :] if lead is None else ref.at[lead, sl, :]


def _all_gather_weights(halved, whole, placed):
    nh, nw, npl = len(halved), len(whole), len(placed)
    na = nh + nw
    nall = na + npl
    arrays = list(halved) + list(whole) + list(placed)
    out_dtypes = [BF] * nh + [a.dtype for a in whole] + [BF] * npl
    cast_rows = 128

    def body(*refs):
        ins, outs = refs[:nall], refs[nall:2 * nall]
        raw, stage = refs[2 * nall:3 * nall], refs[3 * nall:3 * nall + nh + npl]
        ici_send, ici_recv, fwd_send, fwd_recv, in_sem, loc_sem = refs[3 * nall + nh + npl:]
        x, y, c, chips = _place()
        me = 2 * x + y
        sibling = (x, y, 1 - c)
        loads = [pltpu.make_async_copy(ins[a], raw[a], in_sem.at[a]) for a in range(nall)]
        for cp in loads:
            cp.start()

        def cast(a, dst):
            r = arrays[a].shape[0]
            for r0 in range(0, r, cast_rows):
                r1 = min(r0 + cast_rows, r)
                dst[r0:r1, :] = raw[a][r0:r1, :].astype(BF)

        def ici(a, j, src_chip, src=None):
            if a < nh:
                rh = arrays[a].shape[0] // 2
                dst = _rows(outs[a], src_chip, c, rh)
                src = dst if src is None else _rows(src, None, c, rh)
            else:
                dst = outs[a].at[src_chip]
                src = dst if src is None else src
            return pltpu.make_async_remote_copy(
                src_ref=src, dst_ref=dst, send_sem=ici_send.at[3 * a + j], recv_sem=ici_recv.at[3 * a + j],
                device_id=(*chips[j], c), device_id_type=MESH)

        def fwd(a, j, half):
            rh = arrays[a].shape[0] // 2
            kj = 2 * chips[j][0] + chips[j][1]
            blk = _rows(outs[a], kj, half, rh)
            return pltpu.make_async_remote_copy(
                src_ref=blk, dst_ref=blk, send_sem=fwd_send.at[3 * a + j], recv_sem=fwd_recv.at[3 * a + j],
                device_id=sibling, device_id_type=MESH)

        local, sends = [], []
        for a in range(na):
            loads[a].wait()
            if a < nh:
                cast(a, stage[a])
                own = stage[a]
            else:
                own = raw[a]
            cp = pltpu.make_async_copy(own, outs[a].at[me], loc_sem.at[a])
            cp.start()
            local.append(cp)
            for j in range(3):
                cp = ici(a, j, me, src=own)
                cp.start()
                sends.append(cp)
        for i in range(npl):
            loads[na + i].wait()
            cast(na + i, stage[nh + i])
            cp = pltpu.make_async_copy(stage[nh + i], outs[na + i].at[me], loc_sem.at[na + i])
            cp.start()
            local.append(cp)
        passed = []
        for a in range(na):
            for j in range(3):
                kj = 2 * chips[j][0] + chips[j][1]
                ici(a, j, kj).wait_recv()
                if a < nh:
                    cp = fwd(a, j, c)
                    cp.start()
                    passed.append(cp)
        for a in range(nh):
            for j in range(3):
                fwd(a, j, 1 - c).wait_recv()
        for cp in sends + passed:
            cp.wait_send()
        for cp in local:
            cp.wait()

    any_spec = pl.BlockSpec(memory_space=pl.ANY)
    return pl.pallas_call(
        body, in_specs=[any_spec] * nall, out_specs=[any_spec] * nall,
        out_shape=[_sds((N_CHIPS,) + a.shape, dt) for a, dt in zip(arrays, out_dtypes)],
        scratch_shapes=[pltpu.VMEM(a.shape, a.dtype) for a in arrays]
        + [pltpu.VMEM(a.shape, BF) for a in list(halved) + list(placed)]
        + [pltpu.SemaphoreType.DMA((3 * na,)), pltpu.SemaphoreType.DMA((3 * na,)),
           pltpu.SemaphoreType.DMA((3 * nh,)), pltpu.SemaphoreType.DMA((3 * nh,)),
           pltpu.SemaphoreType.DMA((nall,)), pltpu.SemaphoreType.DMA((nall,))],
        compiler_params=pltpu.CompilerParams(vmem_limit_bytes=VMEM_LIMIT_MB << 20),
        name="all_gather_weights")(*arrays)


def _pair_sum(g, recv, c, tr, name):
    _, r, cols = g.shape
    rh = r // 2
    nr = rh // tr

    def body(c_ref, g_ref, r_ref, o_ref):
        o_ref[...] = (g_ref[...] + r_ref[...]).astype(BF)

    grid_spec = pltpu.PrefetchScalarGridSpec(
        num_scalar_prefetch=1, grid=(N_CHIPS, nr),
        in_specs=[pl.BlockSpec((1, tr, cols), lambda k, i, c_ref: (k, c_ref[0] * nr + i, 0)),
                  pl.BlockSpec((1, tr, cols), lambda k, i, c_ref: (k, i, 0))],
        out_specs=pl.BlockSpec((1, tr, cols), lambda k, i, c_ref: (k, i, 0)))
    return pl.pallas_call(
        body, grid_spec=grid_spec, out_shape=_sds((N_CHIPS, rh, cols), BF),
        compiler_params=_cp(("parallel", "parallel")), name=name)(c, g, recv)


def _chip_sum(g, sib, recv, place, tr, name):
    _, r, cols = g.shape
    rh = r // 2
    nr = rh // tr

    def body(p_ref, g_ref, s_ref, r0_ref, r1_ref, r2_ref, o_ref):
        own = g_ref[0] + s_ref[0]
        o_ref[...] = ((own + r0_ref[0].astype(F32)) + r1_ref[0].astype(F32)) + r2_ref[0].astype(F32)

    rspec = lambda j: pl.BlockSpec((1, tr, cols), lambda i, p: (j, i, 0))
    grid_spec = pltpu.PrefetchScalarGridSpec(
        num_scalar_prefetch=1, grid=(nr,),
        in_specs=[pl.BlockSpec((1, tr, cols), lambda i, p: (p[0], p[1] * nr + i, 0)),
                  pl.BlockSpec((1, tr, cols), lambda i, p: (p[0], i, 0)), rspec(0), rspec(1), rspec(2)],
        out_specs=pl.BlockSpec((tr, cols), lambda i, p: (p[1] * nr + i, 0)))
    return pl.pallas_call(
        body, grid_spec=grid_spec, out_shape=_sds((r, cols), F32),
        compiler_params=_cp(("parallel",)), name=name)(place, g, sib, recv, recv, recv)


_HBM = pl.BlockSpec(memory_space=pltpu.HBM)
_SEM = pl.BlockSpec(memory_space=pltpu.SEMAPHORE)
_EFFECT = pltpu.SideEffectType.DATAFLOW_SIDE_EFFECTING


def _remote(src, dst, ssem, rsem, k, device):
    return pltpu.make_async_remote_copy(src_ref=src, dst_ref=dst, send_sem=ssem.at[k], recv_sem=rsem.at[k],
                                        device_id=device, device_id_type=MESH)


def _split_start(name, bufs, plan, n):
    nb = len(bufs)

    def body(*refs):
        sends, _ = plan(refs[:nb], refs[nb], refs[nb + 1])
        for cp in sends:
            cp.start()
        refs[-1][...] = jnp.zeros_like(refs[-1])

    res = pl.pallas_call(
        body, name=name,
        out_shape=(pltpu.SemaphoreType.DMA((n,)), pltpu.SemaphoreType.DMA((n,)))
        + tuple(pltpu.HBM(b.shape, b.dtype) for b in bufs) + (_sds((8, 128), F32),),
        in_specs=[_HBM] * nb,
        out_specs=(_SEM, _SEM) + (_HBM,) * nb + (pl.BlockSpec(memory_space=pltpu.VMEM),),
        input_output_aliases={i: i + 2 for i in range(nb)},
        compiler_params=pltpu.CompilerParams(has_side_effects=_EFFECT),
    )(*[pltpu.with_memory_space_constraint(b, pltpu.HBM) for b in bufs])
    return res[0], res[1], list(res[2:2 + nb]), res[2 + nb]


def _split_wait(name, send_sem, recv_sem, bufs, plan, after):
    nb = len(bufs)

    def body(*refs):
        sends, arrivals = plan(refs[:nb], refs[nb], refs[nb + 1])
        for cp in sends:
            cp.wait_send()
        for cp in arrivals:
            cp.wait_recv()

    res = pl.pallas_call(
        body, name=name, out_shape=tuple(pltpu.HBM(b.shape, b.dtype) for b in bufs),
        in_specs=[_HBM] * nb + [_SEM, _SEM, pl.BlockSpec(memory_space=pl.ANY)],
        out_specs=(_HBM,) * nb, input_output_aliases={i: i for i in range(nb)},
        compiler_params=pltpu.CompilerParams(has_side_effects=_EFFECT),
    )(*bufs, send_sem, recv_sem, after)
    return list(res)


def _plan_sibling_halves(shapes):
    na = len(shapes)

    def plan(refs, ssem, rsem):
        x, y, c, _ = _place()
        cps = []
        for a in range(na):
            rh = shapes[a][1] // 2
            src = refs[a].at[:, pl.ds(pl.multiple_of((1 - c) * rh, 8), rh), :]
            cps.append(_remote(src, refs[na + a], ssem, rsem, a, (x, y, 1 - c)))
        return cps, cps

    return plan


def _to_all(ref, ssem, rsem, base):
    x, y, c, _ = _place()
    mine = ref.at[4 * x + 2 * y + c]
    return [_remote(mine, mine, ssem, rsem, base + r - 1, (x ^ (r >> 2), y ^ ((r >> 1) & 1), c ^ (r & 1)))
            for r in range(1, N_DEV)]


def _plan_chip_exchange(na, with_small):
    def plan(refs, ssem, rsem):
        _, _, c, chips = _place()
        cps = []
        for a in range(na):
            for j in range(3):
                kj = 2 * chips[j][0] + chips[j][1]
                cps.append(_remote(refs[a].at[kj], refs[na + a].at[j], ssem, rsem, 3 * a + j, (*chips[j], c)))
        if with_small:
            cps += _to_all(refs[2 * na], ssem, rsem, 3 * na)
        return cps, cps

    return plan


def _plan_sibling_swap(shapes, with_small):
    def plan(refs, ssem, rsem):
        x, y, c, _ = _place()
        sends, arrivals = [], []
        for a, shp in enumerate(shapes):
            rh = shp[0] // 2
            mine, other = _rows(refs[a], None, c, rh), _rows(refs[a], None, 1 - c, rh)
            sends.append(_remote(mine, mine, ssem, rsem, a, (x, y, 1 - c)))
            arrivals.append(_remote(mine, other, ssem, rsem, a, (x, y, 1 - c)))
        if with_small:
            cps = _to_all(refs[len(shapes)], ssem, rsem, len(shapes))
            sends += cps
            arrivals += cps
        return sends, arrivals

    return plan


def _plan_gather_chips(shapes):
    def plan(refs, ssem, rsem):
        x, y, c, chips = _place()
        me = 2 * x + y
        sends, arrivals = [], []
        for a, shp in enumerate(shapes):
            rh = shp[1] // 2
            mine = _rows(refs[a], me, c, rh)
            for j in range(3):
                land = _rows(refs[a], 2 * chips[j][0] + chips[j][1], c, rh)
                sends.append(_remote(mine, mine, ssem, rsem, 3 * a + j, (*chips[j], c)))
                arrivals.append(_remote(land, land, ssem, rsem, 3 * a + j, (*chips[j], c)))
        return sends, arrivals

    return plan


def _plan_gather_sibling(shapes):
    def plan(refs, ssem, rsem):
        x, y, c, chips = _place()
        sends, arrivals = [], []
        for a, shp in enumerate(shapes):
            rh = shp[1] // 2
            for j in range(3):
                kj = 2 * chips[j][0] + chips[j][1]
                got, land = _rows(refs[a], kj, c, rh), _rows(refs[a], kj, 1 - c, rh)
                sends.append(_remote(got, got, ssem, rsem, 3 * a + j, (x, y, 1 - c)))
                arrivals.append(_remote(got, land, ssem, rsem, 3 * a + j, (x, y, 1 - c)))
        return sends, arrivals

    return plan


def _into_slice(w, k, n, tr, dtype, name, deps=()):
    r, cols = w.shape

    def body(k_ref, w_ref, o_ref):
        o_ref[0] = w_ref[...].astype(dtype)

    grid_spec = pltpu.PrefetchScalarGridSpec(
        num_scalar_prefetch=1, grid=(r // tr,),
        in_specs=[pl.BlockSpec((tr, cols), lambda i, k: (i, 0))] + _any_specs(deps),
        out_specs=pl.BlockSpec((1, tr, cols), lambda i, k: (k[0], i, 0)))
    return pl.pallas_call(_after(body, 2, deps), grid_spec=grid_spec, out_shape=_sds((n, r, cols), dtype),
                          compiler_params=_cp(("parallel",)), name=name)(k, w, *deps)


class _LateWeights:
    def __init__(self, bufs):
        self.n = 3 * len(bufs)
        self.chips, self.sibling = _plan_gather_chips([b.shape for b in bufs]), _plan_gather_sibling([b.shape for b in bufs])
        self.ssem, self.rsem, self.bufs, token = _split_start("gather_chips_start", bufs, self.chips, self.n)
        self.first = (token,)

    def middle(self, after):
        bufs = _split_wait("gather_chips_wait", self.ssem, self.rsem, self.bufs, self.chips, after)
        self.ssem, self.rsem, self.bufs, token = _split_start("gather_sibling_start", bufs, self.sibling, self.n)
        return (token,)

    def last(self, after):
        return _split_wait("gather_sibling_wait", self.ssem, self.rsem, self.bufs, self.sibling, after)


class _GradReduce:
    def __init__(self, tag, place, names, tiles):
        self.tag, self.place, self.names, self.tiles = tag, place, names, tiles
        self.small_all = None

    def first(self, grads):
        self.na = len(grads)
        self.p1 = _plan_sibling_halves([g.shape for g in grads])
        lands = [lax.empty((N_CHIPS, g.shape[1] // 2, g.shape[2]), F32) for g in grads]
        self.ssem, self.rsem, self.bufs, token = _split_start(
            self.tag + "_halves_start", list(grads) + lands, self.p1, self.na)
        return (token,)

    def second(self, after, small=None):
        bufs = _split_wait(self.tag + "_halves_wait", self.ssem, self.rsem, self.bufs, self.p1, after)
        self.grads, self.sib = bufs[:self.na], bufs[self.na:]
        halves = [_pair_sum(g, r, self.place[1:2], t, "pair_sum_" + nm)
                  for g, r, t, nm in zip(self.grads, self.sib, self.tiles, self.names)]
        lands = [lax.empty((3,) + h.shape[1:], h.dtype) for h in halves]
        extra = [] if small is None else [small]
        self.p2 = _plan_chip_exchange(self.na, small is not None)
        self.ssem, self.rsem, self.bufs, token = _split_start(
            self.tag + "_chips_start", halves + lands + extra, self.p2, 3 * self.na + (N_DEV - 1) * len(extra))
        return (token,)

    def third(self, after, small=None):
        bufs = _split_wait(self.tag + "_chips_wait", self.ssem, self.rsem, self.bufs, self.p2, after)
        if len(bufs) > 2 * self.na:
            self.small_chips = bufs[2 * self.na]
        mine = [_chip_sum(g, sb, r, self.place, t, "chip_sum_" + nm)
                for g, sb, r, t, nm in zip(self.grads, self.sib, bufs[self.na:2 * self.na], self.tiles, self.names)]
        extra = [] if small is None else [small]
        self.p3 = _plan_sibling_swap([m.shape for m in mine], small is not None)
        self.ssem, self.rsem, self.bufs, token = _split_start(
            self.tag + "_swap_start", mine + extra, self.p3, self.na + (N_DEV - 1) * len(extra))
        return (token,)

    def last(self, after):
        bufs = _split_wait(self.tag + "_swap_wait", self.ssem, self.rsem, self.bufs, self.p3, after)
        if len(bufs) > self.na:
            self.small_swap = bufs[self.na]
        return bufs[:self.na]


class _WeightsAtHand:
    def __init__(self, wup, wout, wdown):
        self.first, self.weights = (), [wup, wout, wdown]

    def middle(self, after):
        return ()

    def last(self, after):
        return self.weights


class _GradsKept:
    def first(self, grads):
        self.grads = list(grads)
        return ()

    def second(self, after, small=None):
        return ()

    def third(self, after, small=None):
        return ()

    def last(self, after):
        return self.grads


def _forward_backward(xs, pos, tgt, win, wpool, cw, attn_norm, b_gate, pool_scale, q_norm, k_norm, sinks,
                      ffn_norm, conv_b, late, early, rest, dev):
    s = xs.shape[0]
    tm = min(512, s)
    tk = min(2048, s)
    inv_freq = np.float32(ROPE_THETA) ** (-np.arange(0, ROPE_DIM, 2, dtype=np.float32) / np.float32(ROPE_DIM))
    lane = np.arange(2 * HEAD) % HEAD
    invf = jnp.asarray(np.where(lane < ROPE_DIM, inv_freq[lane % (ROPE_DIM // 2)], 0.0).astype(np.float32)
                       .reshape(1, 2 * HEAD))
    wq = jnp.tile(q_norm, (1, N_Q))
    wk = jnp.tile(k_norm, (1, N_KV))
    head_of = jnp.arange(Q_W) // HEAD
    bd = (head_of[:, None] == head_of[None, :]).astype(BF)
    sink = sinks[0]

    h1, u, qkv, gates = _attn_in_proj(xs, attn_norm, win, b_gate, tm, deps=late.first)
    qh, kh, vh = _qk_prep(qkv, pos, wq, wk, invf, bd, tm)
    apool, pooled = _pool_fwd(u, wpool, pool_scale, min(4096, s), deps=(qh,))
    battn = _attn_fwd(qh, kh, vh, sink, deps=late.middle(apool))
    wup, wout, wdown = late.last(battn)
    wout = wout.reshape(D, D)
    wdown = wdown.reshape(D_FF, D)
    mix, y1, h2 = _mix_out_proj(apool, battn, gates, xs, wout, ffn_norm, tm)
    pre_g, pre_v, up_g, up_v, act = _ffn_up(h2, wup, cw, conv_b, tm)
    dy2, dy2b, loss_acc = _ffn_down_loss(act, wdown, y1, tgt, tm)

    d_wdown = _grad_matmul(act, dy2b, 512, tk, "grad_w_down")
    dp_g, dp_v, dcw_g, dcw_v, dcb_g, dcb_v = _ffn_act_bwd(dy2b, wdown, up_g, up_v, pre_g, pre_v, cw, tm)
    d_wup = _grad_matmul(h2, dp_g, UP_SHARD, tk, "grad_w_up_gate", lead=N_CHIPS)
    d_wup = _grad_matmul(h2, dp_v, UP_SHARD, tk, "grad_w_up_value", lead=N_CHIPS, prev=d_wup, lead_off=2)
    token = early.first([d_wdown.reshape(N_CHIPS, D_FF // N_CHIPS, D), d_wup])
    dy1, d_ffn_norm = _ffn_up_bwd(dp_g, dp_v, wup, y1, dy2, ffn_norm, tm, deps=token)
    token = early.second(dy1)
    da, db, dzg, d_bgate, d_wout = _out_proj_bwd(dy1, wout, apool, battn, gates, mix, tm, deps=token)
    du, d_wpool, d_pscale, d_win_pool = _pool_bwd(da, pooled, wpool, pool_scale, h1, min(4096, s))
    dqh, dkh, dvh, dsink = _attn_bwd(qh, kh, vh, sink, db)
    token = early.third(dqh)
    dzq, d_qn, d_kn = _qk_prep_bwd(dqh, dkh, dvh, qkv, pos, wq, wk, invf, bd, tm, deps=token)
    d_win_t = jnp.concatenate([
        d_win_pool,
        _grad_matmul(dzq, h1, D, tk, "grad_w_in_qkv"),
        _grad_matmul(dzg, h1, D, tk, "grad_w_in_gates")], axis=0)
    token = rest.first([
        d_win_t.reshape(N_CHIPS, IN_W // N_CHIPS, D),
        d_wout.reshape(N_CHIPS, D // N_CHIPS, D),
        d_wpool.reshape(4, N_CHIPS, 64, POOL_GROUP).transpose(1, 0, 2, 3).reshape(N_CHIPS, 4 * 64, POOL_GROUP)])
    small = _pack_small(loss_acc, d_bgate, d_pscale, d_qn, d_kn, dsink, d_ffn_norm, dcb_g, dcb_v, dcw_g, dcw_v, dev)
    token = rest.second(token[0] if token else None, small=small)
    grad_x, d_attn_norm = _in_proj_bwd(du, dzq, dzg, win, xs, attn_norm, dy1, tm, deps=token)
    return grad_x, d_attn_norm, small


def kernel(x, positions, attn_norm, w_in, b_gate, w_pool, pool_scale, q_norm, k_norm, sinks, w_out, ffn_norm, w_up, conv_w, conv_b, w_down, loss_target, m_attn_norm, m_w_in, m_b_gate, m_w_pool, m_pool_scale, m_q_norm, m_k_norm, m_sinks, m_w_out, m_ffn_norm, m_w_up, m_conv_w, m_conv_b, m_w_down, v_attn_norm, v_w_in, v_b_gate, v_w_pool, v_pool_scale, v_q_norm, v_k_norm, v_sinks, v_w_out, v_ffn_norm, v_w_up, v_conv_w, v_conv_b, v_w_down):
    s = x.shape[1]
    xs = x[0]
    tgt = loss_target[0]
    pos = positions[0].reshape(s, 1)
    cx, cy, cc = lax.axis_index("x"), lax.axis_index("y"), lax.axis_index("c")
    chip = 2 * cx + cy

    dev_arr = (2 * chip + cc).reshape(1).astype(jnp.int32)
    place = jnp.stack([chip, cc]).astype(jnp.int32)

    g_in, g_pool, g_cw, *own_late = _all_gather_weights(
        [jnp.swapaxes(w_in[0], 0, 1), w_pool[0].reshape(4 * 64, POOL_GROUP)], [conv_w[0]],
        [w_up[0], w_out[0], w_down[0]])
    win = g_in.reshape(IN_W, D)
    wpool = g_pool.reshape(N_CHIPS, 4, 64, POOL_GROUP).transpose(1, 0, 2, 3).reshape(4, POOL_GROUP, POOL_GROUP)
    late = _LateWeights(own_late)
    early = _GradReduce("early", place, ["w_down", "w_up"], [352, 512])
    rest = _GradReduce("rest", place, ["w_in", "w_out", "w_pool"], [544, 128, 128])

    def two_d(a):
        return a.reshape(-1, a.shape[-1])

    def update(nm, w, g, m, v, tr, deps=()):
        res = _adamw(two_d(w), g, two_d(m), two_d(v), tr, "adamw_" + nm, deps=deps)
        return [r.reshape(w.shape) for r in res]

    grad_x, d_attn_norm, _ = _forward_backward(
        xs, pos, tgt, win, wpool, g_cw, attn_norm, b_gate, pool_scale, q_norm, k_norm, sinks, ffn_norm, conv_b,
        late, early, rest, dev_arr)

    attn_stack = _into_slice(d_attn_norm, dev_arr, N_DEV, 1, F32, "own_attn_norm")
    g_wdown, g_wup = early.last(grad_x)
    big_out = {"w_up": update("w_up", w_up, g_wup, m_w_up, v_w_up, 256)}
    big_out["w_down"] = update("w_down", w_down, g_wdown, m_w_down, v_w_down, 352, deps=(big_out["w_up"][1],))
    token = rest.third(big_out["w_down"][1], small=attn_stack)
    g_win_t, g_wout, g_wpool = rest.last(token[0])
    small_out, g_convw_all, loss = _small_update(rest.small_chips, rest.small_swap, {
        "attn_norm": (attn_norm, m_attn_norm, v_attn_norm), "b_gate": (b_gate, m_b_gate, v_b_gate),
        "pool_scale": (pool_scale, m_pool_scale, v_pool_scale), "q_norm": (q_norm, m_q_norm, v_q_norm),
        "k_norm": (k_norm, m_k_norm, v_k_norm), "sinks": (sinks, m_sinks, v_sinks),
        "ffn_norm": (ffn_norm, m_ffn_norm, v_ffn_norm), "conv_b": (conv_b, m_conv_b, v_conv_b)})
    g_convw = lax.dynamic_slice_in_dim(g_convw_all, chip * UP_SHARD, UP_SHARD, axis=1)
    small_out["conv_w"] = update("conv_w", conv_w, g_convw, m_conv_w, v_conv_w, 3)
    flip = lambda a: jnp.swapaxes(a[0], 0, 1)
    res = _adamw(flip(w_in), g_win_t, flip(m_w_in), flip(v_w_in), 544, "adamw_w_in")
    big_out["w_in"] = [jnp.swapaxes(r, 0, 1)[None] for r in res]
    big_out["w_out"] = update("w_out", w_out, g_wout, m_w_out, v_w_out, 256)
    big_out["w_pool"] = update("w_pool", w_pool, g_wpool, m_w_pool, v_w_pool, 256)

    order = ["attn_norm", "w_in", "b_gate", "w_pool", "pool_scale", "q_norm", "k_norm", "sinks", "w_out",
             "ffn_norm", "w_up", "conv_w", "conv_b", "w_down"]
    allout = {**big_out, **small_out}
    outs = [loss.reshape(()), grad_x[None]]
    for k in range(4):
        outs += [allout[nm][k] for nm in order]
    return tuple(outs)
```
